```python
import math
import jax, jax.numpy as jnp
from jax import lax
import numpy as np

D_MODEL = 1024
BATCH = 4
SEQ = 4096
DEPTH = 1

HEAD_DIM = 64
ATTN_SCALE = HEAD_DIM ** -0.5
SWA_Q_HEADS = 8
SWA_KV_HEADS = 2
SWA_GROUP = SWA_Q_HEADS // SWA_KV_HEADS
SWA_WINDOW = 128
SWA_BLOCK = 128
MOBA_HEADS = 8
MOBA_BLOCK = 256
MOBA_TOPK = 3
MOBA_Q_CHUNK = 32
NUM_BUCKETS = 32
MAX_EXACT = NUM_BUCKETS // 2
MAX_DISTANCE = 2048
N_ATTN_HEADS = SWA_Q_HEADS + MOBA_HEADS
SWA_Q_W = SWA_Q_HEADS * HEAD_DIM
SWA_KV_W = SWA_KV_HEADS * HEAD_DIM
MOBA_W = MOBA_HEADS * HEAD_DIM
N_BRANCHES = 2
IN_WIDTHS = (SWA_Q_W, SWA_KV_W, SWA_KV_W, MOBA_W, MOBA_W, MOBA_W, D_MODEL, D_MODEL)
IN_TOTAL = SWA_Q_W + 2 * SWA_KV_W + 3 * MOBA_W + N_BRANCHES * D_MODEL
D_FF = 4 * D_MODEL
N_MOD = 6
RMS_EPS = 1e-6

kernel_name = "hybrid_swa_moba_gated_adaln_block"


def rmsnorm(x, g):
    xf = x.astype(jnp.float32)
    y = xf * lax.rsqrt(jnp.mean(xf * xf, axis=-1, keepdims=True) + RMS_EPS)
    return (y * g.astype(jnp.float32)).astype(x.dtype)


def modulate(h, shift, scale):
    return h * (1.0 + scale[:, None, :]) + shift[:, None, :]


def t5_bucket(dist):
    n = jnp.maximum(dist, 0)
    nf = jnp.maximum(n, 1).astype(jnp.float32)
    large = MAX_EXACT + (jnp.log(nf / MAX_EXACT) / math.log(MAX_DISTANCE / MAX_EXACT)
                         * (NUM_BUCKETS - MAX_EXACT)).astype(jnp.int32)
    large = jnp.minimum(large, NUM_BUCKETS - 1)
    return jnp.where(n < MAX_EXACT, n, large)


def swa_attention(q, k, v, sinks, rel_bias):
    B, S = q.shape[0], q.shape[1]
    L = SWA_BLOCK
    nb = S // L
    qb = q.reshape(B, nb, L, SWA_KV_HEADS, SWA_GROUP, HEAD_DIM)
    kb = k.reshape(B, nb, L, SWA_KV_HEADS, HEAD_DIM)
    vb = v.reshape(B, nb, L, SWA_KV_HEADS, HEAD_DIM)
    pad = ((0, 0), (1, 0), (0, 0), (0, 0), (0, 0))
    kk = jnp.concatenate([jnp.pad(kb, pad)[:, :-1], kb], axis=2)
    vv = jnp.concatenate([jnp.pad(vb, pad)[:, :-1], vb], axis=2)
    s = jnp.einsum('bnqhgd,bnkhd->bnhgqk', qb, kk).astype(jnp.float32) * ATTN_SCALE
    i = jnp.arange(L)[:, None] + L
    j = jnp.arange(2 * L)[None, :]
    dist = i - j
    bias = rel_bias[t5_bucket(dist)][..., :SWA_Q_HEADS].astype(jnp.float32)
    bias = bias.transpose(2, 0, 1).reshape(SWA_KV_HEADS, SWA_GROUP, L, 2 * L)
    kabs = jnp.arange(nb)[:, None, None] * L - L + j[None]
    mask = ((dist >= 0) & (dist < SWA_WINDOW))[None] & (kabs >= 0)
    s = jnp.where(mask[None, :, None, None], s + bias, -jnp.inf)
    sink = sinks.astype(jnp.float32).reshape(SWA_KV_HEADS, SWA_GROUP)[None, None, :, :, None, None]
    m = jnp.maximum(jnp.max(s, axis=-1, keepdims=True), sink)
    p = jnp.exp(s - m)
    p = p / (jnp.sum(p, axis=-1, keepdims=True) + jnp.exp(sink - m))
    o = jnp.einsum('bnhgqk,bnkhd->bnqhgd', p.astype(v.dtype), vv)
    return o.reshape(B, S, SWA_Q_HEADS * HEAD_DIM)


def moba_attention(q, k, v, rel_bias):
    B, S = q.shape[0], q.shape[1]
    H = MOBA_HEADS
    MB = MOBA_BLOCK
    nblk = (S + MB - 1) // MB
    Sp = nblk * MB
    k_eff = max(1, min(MOBA_TOPK, nblk - 1))
    padw = ((0, 0), (0, Sp - S), (0, 0), (0, 0))
    q = jnp.pad(q, padw)
    k = jnp.pad(k, padw)
    v = jnp.pad(v, padw)
    qh = q.transpose(0, 2, 1, 3)
    kb = k.reshape(B, nblk, MB, H, HEAD_DIM).transpose(0, 3, 1, 2, 4)
    vb = v.reshape(B, nblk, MB, H, HEAD_DIM).transpose(0, 3, 1, 2, 4)
    kmean = jnp.mean(kb.astype(jnp.float32), axis=3)
    gate = jnp.einsum('bhsd,bhnd->bhsn', qh.astype(jnp.float32), kmean)
    qblk = jnp.arange(Sp) // MB
    past = jnp.arange(nblk)[None, :] < qblk[:, None]
    gate = jnp.where(past, gate, -jnp.inf)
    _, idx = lax.top_k(gate, k_eff)
    valid = jnp.arange(k_eff)[None, :] < qblk[:, None]
    table_b = rel_bias[:, SWA_Q_HEADS:].T.astype(jnp.float32)
    bi = jnp.arange(B)[:, None, None, None]
    hi = jnp.arange(H)[None, :, None, None]
    hi5 = hi[..., None]
    n_chunks = Sp // MOBA_Q_CHUNK

    def chunk_fn(ci):
        start = ci * MOBA_Q_CHUNK
        q_c = lax.dynamic_slice_in_dim(qh, start, MOBA_Q_CHUNK, axis=2)
        idx_c = lax.dynamic_slice_in_dim(idx, start, MOBA_Q_CHUNK, axis=2)
        valid_c = lax.dynamic_slice_in_dim(valid, start, MOBA_Q_CHUNK, axis=0)
        qpos = start + jnp.arange(MOBA_Q_CHUNK)
        own = start // MB
        k_sel = kb[bi, hi, idx_c]
        v_sel = vb[bi, hi, idx_c]
        s_past = jnp.einsum('bhqd,bhqrkd->bhqrk', q_c, k_sel).astype(jnp.float32) * ATTN_SCALE
        kpos_past = idx_c[..., None] * MB + jnp.arange(MB)
        bias_past = table_b[hi5, t5_bucket(qpos[:, None, None] - kpos_past)]
        s_past = jnp.where(valid_c[None, None, :, :, None], s_past + bias_past, -jnp.inf)
        s_past = s_past.reshape(B, H, MOBA_Q_CHUNK, k_eff * MB)
        k_own = lax.dynamic_index_in_dim(kb, own, axis=2, keepdims=False)
        v_own = lax.dynamic_index_in_dim(vb, own, axis=2, keepdims=False)
        s_own = jnp.einsum('bhqd,bhkd->bhqk', q_c, k_own).astype(jnp.float32) * ATTN_SCALE
        dist_own = qpos[:, None] - (own * MB + jnp.arange(MB))[None, :]
        bias_own = table_b[:, t5_bucket(dist_own)]
        s_own = jnp.where(dist_own >= 0, s_own + bias_own, -jnp.inf)
        p = jax.nn.softmax(jnp.concatenate([s_past, s_own], axis=-1), axis=-1)
        p_past = p[..., :k_eff * MB].reshape(B, H, MOBA_Q_CHUNK, k_eff, MB)
        p_own = p[..., k_eff * MB:]
        o = (jnp.einsum('bhqrk,bhqrkd->bhqd', p_past.astype(v_sel.dtype), v_sel)
             + jnp.einsum('bhqk,bhkd->bhqd', p_own.astype(v_own.dtype), v_own))
        return o

    out = lax.map(chunk_fn, jnp.arange(n_chunks))
    out = out.transpose(1, 0, 3, 2, 4).reshape(B, Sp, H * HEAD_DIM)
    return out[:, :S]


def setup_inputs(seed: int = 0) -> dict:
    key = jax.random.key(seed)
    ks = jax.random.split(key, 16)
    f32 = jnp.float32
    nrm = lambda k, shape, s: jax.random.normal(k, shape, f32) * s
    return {
        "x": nrm(ks[0], (BATCH, SEQ, D_MODEL), 1.0),
        "c": nrm(ks[1], (BATCH, D_MODEL), 1.0),
        "ada_w": nrm(ks[2], (DEPTH, D_MODEL, N_MOD * D_MODEL), D_MODEL ** -0.5),
        "ada_b": nrm(ks[3], (DEPTH, N_MOD * D_MODEL), 0.02),
        "norm1_g": 1.0 + nrm(ks[4], (DEPTH, D_MODEL), 0.02),
        "norm2_g": 1.0 + nrm(ks[5], (DEPTH, D_MODEL), 0.02),
        "w_in": nrm(ks[6], (DEPTH, D_MODEL, IN_TOTAL), D_MODEL ** -0.5),
        "attn_sinks": nrm(ks[7], (DEPTH, SWA_Q_HEADS), 0.5),
        "rel_bias": nrm(ks[8], (NUM_BUCKETS, N_ATTN_HEADS), 0.5),
        "w_branch_a": nrm(ks[9], (DEPTH, SWA_Q_W, D_MODEL), SWA_Q_W ** -0.5),
        "w_branch_b": nrm(ks[10], (DEPTH, MOBA_W, D_MODEL), MOBA_W ** -0.5),
        "w_out": nrm(ks[11], (DEPTH, D_MODEL, D_MODEL), D_MODEL ** -0.5),
        "w_mlp_in": nrm(ks[12], (DEPTH, D_MODEL, D_FF), D_MODEL ** -0.5),
        "w_mlp_out": nrm(ks[13], (DEPTH, D_FF, D_MODEL), D_FF ** -0.5),
        "final_g": 1.0 + nrm(ks[14], (D_MODEL,), 0.02),
    }


def reference(x, c, ada_w, ada_b, norm1_g, norm2_g, w_in, attn_sinks, rel_bias,
              w_branch_a, w_branch_b, w_out, w_mlp_in, w_mlp_out, final_g):
    B, S = x.shape[0], x.shape[1]
    offsets = []
    acc = 0
    for w in IN_WIDTHS[:-1]:
        acc += w
        offsets.append(acc)
    cs = jax.nn.silu(c)
    for l in range(DEPTH):
        mod = cs @ ada_w[l] + ada_b[l]
        shift1, scale1, gate1, shift2, scale2, gate2 = jnp.split(mod, N_MOD, axis=-1)
        h = modulate(rmsnorm(x, norm1_g[l]), shift1, scale1)
        proj = h @ w_in[l]
        qa, ka, va, qb, kb, vb, ga, gb = jnp.split(proj, offsets, axis=-1)
        ya = swa_attention(qa.reshape(B, S, SWA_Q_HEADS, HEAD_DIM),
                           ka.reshape(B, S, SWA_KV_HEADS, HEAD_DIM),
                           va.reshape(B, S, SWA_KV_HEADS, HEAD_DIM),
                           attn_sinks[l], rel_bias) @ w_branch_a[l]
        yb = moba_attention(qb.reshape(B, S, MOBA_HEADS, HEAD_DIM),
                            kb.reshape(B, S, MOBA_HEADS, HEAD_DIM),
                            vb.reshape(B, S, MOBA_HEADS, HEAD_DIM),
                            rel_bias) @ w_branch_b[l]
        merged = jax.nn.sigmoid(ga) * ya + jax.nn.sigmoid(gb) * yb
        x = x + gate1[:, None, :] * (merged @ w_out[l])
        h2 = modulate(rmsnorm(x, norm2_g[l]), shift2, scale2)
        y = jnp.square(jax.nn.relu(h2 @ w_mlp_in[l])) @ w_mlp_out[l]
        x = x + gate2[:, None, :] * y
    return rmsnorm(x, final_g)
```

```python
import functools
import math

import jax
import jax.numpy as jnp
from jax import lax
from jax.experimental import pallas as pl
from jax.experimental.pallas import tpu as pltpu

D_MODEL = 1024
HEAD_DIM = 64
ATTN_SCALE = HEAD_DIM ** -0.5
SWA_Q_HEADS = 8
SWA_KV_HEADS = 2
SWA_GROUP = SWA_Q_HEADS // SWA_KV_HEADS
SWA_WINDOW = 128
SWA_BLOCK = 128
MOBA_HEADS = 8
MOBA_BLOCK = 256
MOBA_TOPK = 3
NUM_BUCKETS = 32
MAX_EXACT = NUM_BUCKETS // 2
MAX_DISTANCE = 2048
SWA_Q_W = SWA_Q_HEADS * HEAD_DIM
SWA_KV_W = SWA_KV_HEADS * HEAD_DIM
MOBA_W = MOBA_HEADS * HEAD_DIM
D_FF = 4 * D_MODEL
N_MOD = 6
RMS_EPS = 1e-6

VMEM_LIMIT_BYTES = 56 * 1024 * 1024
TOKEN_TILE = 512
FF_CHUNK = 1024
BIAS_SLOTS = 8
NEG_INF = float("-inf")

F32 = jnp.float32
BF16 = jnp.bfloat16


def _resident(block_shape, index_map):
    return pl.BlockSpec(block_shape, index_map, pipeline_mode=pl.Buffered(1))


def _dot(a, b):
    return jnp.dot(a, b, preferred_element_type=F32)


def _dot_nt(a, b):
    return lax.dot_general(a, b, (((1,), (1,)), ((), ())), preferred_element_type=F32)


def _rms_modulate(xv, g, shift, scale):
    ms = jnp.mean(xv * xv, axis=-1, keepdims=True)
    y = xv * lax.rsqrt(ms + RMS_EPS)
    return (y * g) * (1.0 + scale) + shift


def _ada_kernel(c_ref, w_ref, b_ref, o_ref):
    cs = jax.nn.silu(c_ref[...])
    o_ref[...] = jnp.dot(cs, w_ref[...], preferred_element_type=F32,
                         precision=lax.Precision.HIGHEST) + b_ref[...]


def _ada(c_pad, w, b):
    n = w.shape[1]
    bn = 1536
    return pl.pallas_call(
        _ada_kernel,
        grid=(n // bn,),
        in_specs=[pl.BlockSpec((8, D_MODEL), lambda j: (0, 0)),
                  pl.BlockSpec((D_MODEL, bn), lambda j: (0, j)),
                  pl.BlockSpec((1, bn), lambda j: (0, j))],
        out_specs=pl.BlockSpec((8, bn), lambda j: (0, j)),
        out_shape=jax.ShapeDtypeStruct((8, n), F32),
        compiler_params=pltpu.CompilerParams(dimension_semantics=("arbitrary",),
                                             vmem_limit_bytes=VMEM_LIMIT_BYTES),
        name="ada",
    )(c_pad, w, b)


def _proj_kernel(x_ref, shift_ref, scale_ref, g_ref, wtok_ref, wfeat_ref,
                 ka_ref, kb_ref, ga_ref, gb_ref, qa_ref, va_ref, qb_ref, vb_ref):
    h = _rms_modulate(x_ref[...], g_ref[...], shift_ref[0, 0], scale_ref[0, 0]).astype(BF16)
    o = 0
    for ref, w in ((ka_ref, SWA_KV_W), (kb_ref, MOBA_W), (ga_ref, D_MODEL), (gb_ref, D_MODEL)):
        ref[...] = _dot(h, wtok_ref[:, o:o + w]).astype(ref.dtype)
        o += w
    o = 0
    for ref, w in ((qa_ref, SWA_Q_W), (va_ref, SWA_KV_W), (qb_ref, MOBA_W), (vb_ref, MOBA_W)):
        ft = _dot_nt(wfeat_ref[o:o + w, :], h).astype(ref.dtype)
        blk = ref.shape[-1]
        for t in range(TOKEN_TILE // blk):
            ref[0, t] = ft[:, t * blk:(t + 1) * blk]
        o += w


def _proj(x2, mod4, g1, wtok, wfeat, batch, seq):
    tokens = x2.shape[0]
    tiles_per_seq = seq // TOKEN_TILE
    n_tok = wtok.shape[1]
    n_feat = wfeat.shape[0]
    na = TOKEN_TILE // SWA_BLOCK
    nb = TOKEN_TILE // MOBA_BLOCK

    def feat_spec(width, blk, per_tile):
        return pl.BlockSpec((1, per_tile, width, blk),
                            lambda i: (i // tiles_per_seq, i % tiles_per_seq, 0, 0))

    out_shape = (
        jax.ShapeDtypeStruct((tokens, SWA_KV_W), BF16),
        jax.ShapeDtypeStruct((tokens, MOBA_W), BF16),
        jax.ShapeDtypeStruct((tokens, D_MODEL), BF16),
        jax.ShapeDtypeStruct((tokens, D_MODEL), BF16),
        jax.ShapeDtypeStruct((batch, seq // SWA_BLOCK, SWA_Q_W, SWA_BLOCK), BF16),
        jax.ShapeDtypeStruct((batch, seq // SWA_BLOCK, SWA_KV_W, SWA_BLOCK), BF16),
        jax.ShapeDtypeStruct((batch, seq // MOBA_BLOCK, MOBA_W, MOBA_BLOCK), BF16),
        jax.ShapeDtypeStruct((batch, seq // MOBA_BLOCK, MOBA_W, MOBA_BLOCK), BF16),
    )
    tok_spec = lambda w: pl.BlockSpec((TOKEN_TILE, w), lambda i: (i, 0))
    return pl.pallas_call(
        _proj_kernel,
        grid=(tokens // TOKEN_TILE,),
        in_specs=[
            pl.BlockSpec((TOKEN_TILE, D_MODEL), lambda i: (i, 0)),
            pl.BlockSpec((1, 1, 1, D_MODEL), lambda i: (i // tiles_per_seq, 0, 0, 0)),
            pl.BlockSpec((1, 1, 1, D_MODEL), lambda i: (i // tiles_per_seq, 1, 0, 0)),
            _resident((1, D_MODEL), lambda i: (0, 0)),
            _resident((D_MODEL, n_tok), lambda i: (0, 0)),
            _resident((n_feat, D_MODEL), lambda i: (0, 0)),
        ],
        out_specs=(tok_spec(SWA_KV_W), tok_spec(MOBA_W), tok_spec(D_MODEL), tok_spec(D_MODEL),
                   feat_spec(SWA_Q_W, SWA_BLOCK, na), feat_spec(SWA_KV_W, SWA_BLOCK, na),
                   feat_spec(MOBA_W, MOBA_BLOCK, nb), feat_spec(MOBA_W, MOBA_BLOCK, nb)),
        out_shape=out_shape,
        compiler_params=pltpu.CompilerParams(dimension_semantics=("arbitrary",),
                                             vmem_limit_bytes=VMEM_LIMIT_BYTES),
        name="proj",
    )(x2, mod4, mod4, g1, wtok, wfeat)


def _swa_kernel(q_ref, kp_ref, kc_ref, vp_ref, vc_ref, bias_ref, sink_ref, o_ref, ot_ref):
    n = pl.program_id(1)
    L = SWA_BLOCK
    first_rows = jnp.where(n == 0, L, 0)
    key_row = lax.broadcasted_iota(jnp.int32, (2 * L, SWA_GROUP * L), 0)
    for g in range(SWA_KV_HEADS):
        hs = slice(g * HEAD_DIM, (g + 1) * HEAD_DIM)
        kw = jnp.concatenate([kp_ref[0, :, hs], kc_ref[0, :, hs]], axis=0)
        qg = jnp.concatenate(
            [q_ref[0, 0, (g * SWA_GROUP + gi) * HEAD_DIM:(g * SWA_GROUP + gi + 1) * HEAD_DIM, :]
             for gi in range(SWA_GROUP)], axis=1)
        s = _dot(kw, qg) + bias_ref[g]
        s = jnp.where(key_row < first_rows, NEG_INF, s)
        sink = sink_ref[g]
        m = jnp.maximum(jnp.max(s, axis=0, keepdims=True), sink)
        p = jnp.exp(s - m)
        denom = jnp.sum(p, axis=0, keepdims=True) + jnp.exp(sink - m)
        vw = jnp.concatenate([vp_ref[0, 0, hs, :], vc_ref[0, 0, hs, :]], axis=1)
        o = _dot(vw, p.astype(BF16)) / denom
        for gi in range(SWA_GROUP):
            r = (g * SWA_GROUP + gi) * HEAD_DIM
            ot_ref[r:r + HEAD_DIM, :] = o[:, gi * L:(gi + 1) * L]
    o_ref[0] = ot_ref[...].T.astype(o_ref.dtype)


def _swa(qa_t, ka3, va_t, bias_a, sink_rows):
    batch, nblk = qa_t.shape[0], qa_t.shape[1]
    seq = nblk * SWA_BLOCK
    prev = lambda n: jnp.maximum(n - 1, 0)
    return pl.pallas_call(
        _swa_kernel,
        grid=(batch, nblk),
        in_specs=[
            pl.BlockSpec((1, 1, SWA_Q_W, SWA_BLOCK), lambda b, n: (b, n, 0, 0)),
            pl.BlockSpec((1, SWA_BLOCK, SWA_KV_W), lambda b, n: (b, prev(n), 0)),
            pl.BlockSpec((1, SWA_BLOCK, SWA_KV_W), lambda b, n: (b, n, 0)),
            pl.BlockSpec((1, 1, SWA_KV_W, SWA_BLOCK), lambda b, n: (b, prev(n), 0, 0)),
            pl.BlockSpec((1, 1, SWA_KV_W, SWA_BLOCK), lambda b, n: (b, n, 0, 0)),
            _resident(bias_a.shape, lambda b, n: (0, 0, 0)),
            _resident(sink_rows.shape, lambda b, n: (0, 0, 0)),
        ],
        out_specs=pl.BlockSpec((1, SWA_BLOCK, SWA_Q_W), lambda b, n: (b, n, 0)),
        out_shape=jax.ShapeDtypeStruct((batch, seq, SWA_Q_W), BF16),
        scratch_shapes=[pltpu.VMEM((SWA_Q_W, SWA_BLOCK), F32)],
        compiler_params=pltpu.CompilerParams(dimension_semantics=("arbitrary", "arbitrary"),
                                             vmem_limit_bytes=VMEM_LIMIT_BYTES),
        name="swa",
    )(qa_t, ka3, ka3, va_t, va_t, bias_a, sink_rows)


def _select_kernel(k_ref, q_ref, o_ref):
    seq = k_ref.shape[1]
    nblk = seq // MOBA_BLOCK
    blk = lax.broadcasted_iota(jnp.int32, (nblk, seq), 0)
    qblk = lax.broadcasted_iota(jnp.int32, (nblk, seq), 1) // MOBA_BLOCK
    past = blk < qblk
    member = jnp.where(blk == qblk, 1.0 / MOBA_BLOCK, 0.0).astype(BF16)
    for hh in range(2):
        hs = slice(hh * HEAD_DIM, (hh + 1) * HEAD_DIM)
        kmean = _dot(member, k_ref[0, :, hs])
        q_t = jnp.concatenate([q_ref[0, t, hs, :] for t in range(nblk)], axis=1)
        gate = jnp.dot(kmean, q_t.astype(F32), preferred_element_type=F32,
                       precision=lax.Precision.HIGHEST)
        gate = jnp.where(past, gate, NEG_INF)
        rank = jnp.zeros((nblk, seq), jnp.int32)
        for r in range(nblk):
            row = gate[r:r + 1, :]
            rank = rank + jnp.where(blk > r, jnp.where(row >= gate, 1, 0),
                                    jnp.where(row > gate, 1, 0))
        sel = jnp.logical_and(past, rank < MOBA_TOPK)
        o_ref[0, hh] = jnp.where(sel, 0.0, NEG_INF)


def _select(kb3, qb_t):
    batch, seq = kb3.shape[0], kb3.shape[1]
    nblk = seq // MOBA_BLOCK
    pairs = MOBA_HEADS // 2
    return pl.pallas_call(
        _select_kernel,
        grid=(batch, pairs),
        in_specs=[
            pl.BlockSpec((1, seq, 2 * HEAD_DIM), lambda b, hp: (b, 0, hp)),
            pl.BlockSpec((1, nblk, 2 * HEAD_DIM, MOBA_BLOCK), lambda b, hp: (b, 0, hp, 0)),
        ],
        out_specs=pl.BlockSpec((1, 2, nblk, seq), lambda b, hp: (b, hp, 0, 0)),
        out_shape=jax.ShapeDtypeStruct((batch, MOBA_HEADS, nblk, seq), F32),
        compiler_params=pltpu.CompilerParams(dimension_semantics=("arbitrary", "arbitrary"),
                                             vmem_limit_bytes=VMEM_LIMIT_BYTES),
        name="moba_select",
    )(kb3, qb_t)


def _moba_kernel(q_ref, k_ref, v_ref, sel_ref, bias_ref, o_ref, ot_ref):
    i = pl.program_id(2)
    MB = MOBA_BLOCK
    for hh in range(2):
        hs = slice(hh * HEAD_DIM, (hh + 1) * HEAD_DIM)
        q_t = q_ref[0, 0, hs, :]

        def scores(j, slot):
            kj = k_ref[0, pl.ds(pl.multiple_of(j * MB, MB), MB), hs]
            return _dot(kj, q_t) + bias_ref[hh, slot]

        s = scores(i, 0)
        m0 = jnp.max(s, axis=0, keepdims=True)
        p = jnp.exp(s - m0)
        l0 = jnp.sum(p, axis=0, keepdims=True)
        acc0 = _dot(v_ref[0, i, hs, :], p.astype(BF16))

        def body(j, carry):
            m, l, acc = carry
            s = scores(j, jnp.minimum(i - j, BIAS_SLOTS - 1)) + sel_ref[0, hh, 0, j]
            m_new = jnp.maximum(m, jnp.max(s, axis=0, keepdims=True))
            alpha = jnp.exp(m - m_new)
            p = jnp.exp(s - m_new)
            l = alpha * l + jnp.sum(p, axis=0, keepdims=True)
            acc = alpha * acc + _dot(v_ref[0, j, hs, :], p.astype(BF16))
            return m_new, l, acc

        _, l, acc = lax.fori_loop(0, i, body, (m0, l0, acc0))
        ot_ref[hs, :] = acc / l
    o_ref[0] = ot_ref[...].T.astype(o_ref.dtype)


def _moba(qb_t, kb3, vb_t, sel6, bias_b):
    batch, nblk = qb_t.shape[0], qb_t.shape[1]
    seq = nblk * MOBA_BLOCK
    pairs = MOBA_HEADS // 2
    return pl.pallas_call(
        _moba_kernel,
        grid=(pairs, batch, nblk),
        in_specs=[
            pl.BlockSpec((1, 1, 2 * HEAD_DIM, MOBA_BLOCK), lambda hp, b, i: (b, i, hp, 0)),
            pl.BlockSpec((1, seq, 2 * HEAD_DIM), lambda hp, b, i: (b, 0, hp)),
            pl.BlockSpec((1, nblk, 2 * HEAD_DIM, MOBA_BLOCK), lambda hp, b, i: (b, 0, hp, 0)),
            pl.BlockSpec((1, 2, 1, nblk, 1, MOBA_BLOCK), lambda hp, b, i: (b, hp, i, 0, 0, 0)),
            pl.BlockSpec((2, BIAS_SLOTS, MOBA_BLOCK, MOBA_BLOCK), lambda hp, b, i: (hp, 0, 0, 0)),
        ],
        out_specs=pl.BlockSpec((1, MOBA_BLOCK, 2 * HEAD_DIM), lambda hp, b, i: (b, i, hp)),
        out_shape=jax.ShapeDtypeStruct((batch, seq, MOBA_W), BF16),
        scratch_shapes=[pltpu.VMEM((2 * HEAD_DIM, MOBA_BLOCK), F32)],
        compiler_params=pltpu.CompilerParams(
            dimension_semantics=("arbitrary", "arbitrary", "arbitrary"),
            vmem_limit_bytes=VMEM_LIMIT_BYTES),
        name="moba",
    )(qb_t, kb3, vb_t, sel6, bias_b)


def _post_kernel(x_ref, ya_ref, yb_ref, ga_ref, gb_ref, gate1_ref, shift2_ref, scale2_ref,
                 gate2_ref, g2_ref, gf_ref, wa_ref, wb_ref, wo_ref, w1_ref, w2_ref, o_ref):
    a = _dot(ya_ref[...], wa_ref[...])
    b = _dot(yb_ref[...], wb_ref[...])
    merged = (jax.nn.sigmoid(ga_ref[...].astype(F32)) * a
              + jax.nn.sigmoid(gb_ref[...].astype(F32)) * b)
    x1 = x_ref[...] + gate1_ref[0, 0] * _dot(merged.astype(BF16), wo_ref[...])
    h2 = _rms_modulate(x1, g2_ref[...], shift2_ref[0, 0], scale2_ref[0, 0]).astype(BF16)
    y = jnp.zeros_like(x1)
    for c in range(D_FF // FF_CHUNK):
        cs = slice(c * FF_CHUNK, (c + 1) * FF_CHUNK)
        u = jnp.square(jnp.maximum(_dot(h2, w1_ref[:, cs]), 0.0)).astype(BF16)
        y = y + _dot(u, w2_ref[cs, :])
    x2 = x1 + gate2_ref[0, 0] * y
    ms = jnp.mean(x2 * x2, axis=-1, keepdims=True)
    o_ref[...] = (x2 * lax.rsqrt(ms + RMS_EPS)) * gf_ref[...]


def _post(x2, ya, yb, ga, gb, mod4, g2, gf, wa, wb, wo, w1, w2, seq):
    tokens = x2.shape[0]
    tiles_per_seq = seq // TOKEN_TILE
    tok = lambda w: pl.BlockSpec((TOKEN_TILE, w), lambda i: (i, 0))
    modrow = lambda k: pl.BlockSpec((1, 1, 1, D_MODEL), lambda i: (i // tiles_per_seq, k, 0, 0))
    full = lambda a: _resident(a.shape, lambda i: (0, 0))
    return pl.pallas_call(
        _post_kernel,
        grid=(tokens // TOKEN_TILE,),
        in_specs=[tok(D_MODEL), tok(SWA_Q_W), tok(MOBA_W), tok(D_MODEL), tok(D_MODEL),
                  modrow(2), modrow(3), modrow(4), modrow(5),
                  full(g2), full(gf), full(wa), full(wb), full(wo), full(w1), full(w2)],
        out_specs=tok(D_MODEL),
        out_shape=jax.ShapeDtypeStruct((tokens, D_MODEL), F32),
        compiler_params=pltpu.CompilerParams(dimension_semantics=("arbitrary",),
                                             vmem_limit_bytes=VMEM_LIMIT_BYTES),
        name="post",
    )(x2, ya, yb, ga, gb, mod4, mod4, mod4, mod4, g2, gf, wa, wb, wo, w1, w2)


def _t5_bucket(dist):
    n = jnp.maximum(dist, 0)
    nf = jnp.maximum(n, 1).astype(F32)
    large = MAX_EXACT + (jnp.log(nf / MAX_EXACT) / math.log(MAX_DISTANCE / MAX_EXACT)
                         * (NUM_BUCKETS - MAX_EXACT)).astype(jnp.int32)
    large = jnp.minimum(large, NUM_BUCKETS - 1)
    return jnp.where(n < MAX_EXACT, n, large)


def _swa_bias(rel_bias):
    L = SWA_BLOCK
    dist = (jnp.arange(L)[None, :] + L) - jnp.arange(2 * L)[:, None]
    valid = (dist >= 0) & (dist < SWA_WINDOW)
    tab = rel_bias[:, :SWA_Q_HEADS].astype(F32)[_t5_bucket(dist)]
    tab = jnp.where(valid[..., None], tab, NEG_INF)
    tab = tab.reshape(2 * L, L, SWA_KV_HEADS, SWA_GROUP).transpose(2, 0, 3, 1)
    return tab.reshape(SWA_KV_HEADS, 2 * L, SWA_GROUP * L)


def _moba_bias(rel_bias, nblk):
    MB = MOBA_BLOCK
    d = jnp.arange(BIAS_SLOTS)[:, None, None]
    dist = d * MB + jnp.arange(MB)[None, None, :] - jnp.arange(MB)[None, :, None]
    tab = rel_bias[:, SWA_Q_HEADS:].astype(F32)[_t5_bucket(dist)]
    tab = jnp.where((dist >= 0)[..., None], tab, NEG_INF)
    return tab.transpose(3, 0, 1, 2)


def kernel(x, c, ada_w, ada_b, norm1_g, norm2_g, w_in, attn_sinks, rel_bias, w_branch_a,
           w_branch_b, w_out, w_mlp_in, w_mlp_out, final_g):
    batch, seq, _ = x.shape
    depth = ada_w.shape[0]
    nblk_b = seq // MOBA_BLOCK
    assert seq % TOKEN_TILE == 0 and TOKEN_TILE % MOBA_BLOCK == 0 and batch <= 8
    assert (BIAS_SLOTS - 1) * MOBA_BLOCK - (MOBA_BLOCK - 1) >= 1513 and MAX_DISTANCE == 2048

    bias_a = _swa_bias(rel_bias)
    bias_b = _moba_bias(rel_bias, nblk_b)
    c_pad = jnp.zeros((8, D_MODEL), F32).at[:batch].set(c)

    widths = (SWA_Q_W, SWA_KV_W, SWA_KV_W, MOBA_W, MOBA_W, MOBA_W, D_MODEL, D_MODEL)
    offs = [0]
    for w in widths:
        offs.append(offs[-1] + w)
    col = lambda k: slice(offs[k], offs[k + 1])

    assert depth == 1
    l = 0
    x2 = x.reshape(batch * seq, D_MODEL)
    mod = _ada(c_pad, ada_w[l], ada_b[l][None, :])
    mod4 = mod[:batch].reshape(batch, N_MOD, 1, D_MODEL)
    wl = w_in[l]
    wtok = jnp.concatenate([wl[:, col(1)], wl[:, col(4)], wl[:, col(6)], wl[:, col(7)]],
                           axis=1).astype(BF16)
    wfeat = jnp.concatenate([wl[:, col(0)] * ATTN_SCALE, wl[:, col(2)],
                             wl[:, col(3)] * ATTN_SCALE, wl[:, col(5)]], axis=1).T.astype(BF16)
    ka, kb, ga, gb, qa_t, va_t, qb_t, vb_t = _proj(
        x2, mod4, norm1_g[l][None, :], wtok, wfeat, batch, seq)

    sink_rows = jnp.broadcast_to(
        attn_sinks[l].astype(F32).reshape(SWA_KV_HEADS, 1, SWA_GROUP, 1),
        (SWA_KV_HEADS, 1, SWA_GROUP, SWA_BLOCK)).reshape(SWA_KV_HEADS, 1, SWA_GROUP * SWA_BLOCK)
    ya = _swa(qa_t, ka.reshape(batch, seq, SWA_KV_W), va_t, bias_a, sink_rows)

    kb3 = kb.reshape(batch, seq, MOBA_W)
    sel = _select(kb3, qb_t)
    sel6 = sel.reshape(batch, MOBA_HEADS, nblk_b, nblk_b, 1, MOBA_BLOCK).transpose(0, 1, 3, 2, 4, 5)
    yb = _moba(qb_t, kb3, vb_t, sel6, bias_b)

    out = _post(x2, ya.reshape(batch * seq, SWA_Q_W), yb.reshape(batch * seq, MOBA_W), ga, gb,
                mod4, norm2_g[l][None, :], final_g[None, :],
                w_branch_a[l].astype(BF16), w_branch_b[l].astype(BF16), w_out[l].astype(BF16),
                w_mlp_in[l].astype(BF16), w_mlp_out[l].astype(BF16), seq)
    return out.reshape(batch, seq, D_MODEL)
```

```python
import math

import jax
import jax.numpy as jnp
from jax import lax
from jax.experimental import pallas as pl
from jax.experimental.pallas import tpu as pltpu

D_MODEL = 1024
HEAD_DIM = 64
ATTN_SCALE = HEAD_DIM ** -0.5
LOG2E = math.log2(math.e)
SWA_Q_HEADS = 8
SWA_KV_HEADS = 2
SWA_GROUP = SWA_Q_HEADS // SWA_KV_HEADS
SWA_WINDOW = 128
SWA_BLOCK = 128
MOBA_HEADS = 8
MOBA_BLOCK = 256
MOBA_TOPK = 3
NUM_BUCKETS = 32
MAX_EXACT = NUM_BUCKETS // 2
MAX_DISTANCE = 2048
N_ATTN_HEADS = SWA_Q_HEADS + MOBA_HEADS
SWA_Q_W = SWA_Q_HEADS * HEAD_DIM
SWA_KV_W = SWA_KV_HEADS * HEAD_DIM
MOBA_W = MOBA_HEADS * HEAD_DIM
D_FF = 4 * D_MODEL
N_MOD = 6
RMS_EPS = 1e-6

VMEM_LIMIT_BYTES = 56 * 1024 * 1024
TOKEN_TILE = 512
FF_CHUNK = 1024
BIAS_SLOTS = 8
NEG_INF = float("-inf")

F32 = jnp.float32
BF16 = jnp.bfloat16


def _resident(block_shape, index_map):
    return pl.BlockSpec(block_shape, index_map, pipeline_mode=pl.Buffered(1))


def _dot(a, b):
    return jnp.dot(a, b, preferred_element_type=F32)


def _dot_nt(a, b):
    return lax.dot_general(a, b, (((1,), (1,)), ((), ())), preferred_element_type=F32)


def _rms_modulate(xv, g, shift, scale):
    ms = jnp.mean(xv * xv, axis=-1, keepdims=True)
    y = xv * lax.rsqrt(ms + RMS_EPS)
    return (y * g) * (1.0 + scale) + shift


def _toeplitz(u, rows):
    wide = jnp.broadcast_to(u, (rows, 2 * rows))
    return pltpu.roll(wide, 0, 1, stride=1, stride_axis=0)[:, rows:]


def _ada_kernel(c_ref, w_ref, b_ref, o_ref):
    cs = jax.nn.silu(c_ref[...])
    o_ref[...] = jnp.dot(cs, w_ref[...], preferred_element_type=F32,
                         precision=lax.Precision.HIGHEST) + b_ref[...]


def _ada(c_pad, w, b):
    n = w.shape[1]
    bn = 1536
    return pl.pallas_call(
        _ada_kernel,
        grid=(n // bn,),
        in_specs=[pl.BlockSpec((8, D_MODEL), lambda j: (0, 0)),
                  pl.BlockSpec((D_MODEL, bn), lambda j: (0, j)),
                  pl.BlockSpec((1, bn), lambda j: (0, j))],
        out_specs=pl.BlockSpec((8, bn), lambda j: (0, j)),
        out_shape=jax.ShapeDtypeStruct((8, n), F32),
        compiler_params=pltpu.CompilerParams(dimension_semantics=("arbitrary",),
                                             vmem_limit_bytes=VMEM_LIMIT_BYTES),
        name="ada",
    )(c_pad, w, b)


def _t5_bucket(dist):
    n = jnp.maximum(dist, 0)
    nf = jnp.maximum(n, 1).astype(F32)
    large = MAX_EXACT + (jnp.log(nf / MAX_EXACT) / math.log(MAX_DISTANCE / MAX_EXACT)
                         * (NUM_BUCKETS - MAX_EXACT)).astype(jnp.int32)
    large = jnp.minimum(large, NUM_BUCKETS - 1)
    return jnp.where(n < MAX_EXACT, n, large)


def _bias_rows_kernel(bucket_ref, rbt_ref, o_ref):
    bucket = bucket_ref[...]
    n = bucket.shape[1]
    onehot = jnp.where(lax.broadcasted_iota(jnp.int32, (NUM_BUCKETS, n), 0) == bucket, 1.0, 0.0)
    rows = jnp.dot(rbt_ref[...], onehot, preferred_element_type=F32,
                   precision=lax.Precision.HIGHEST)
    rows = jnp.where(bucket >= 0, rows * LOG2E, NEG_INF)
    for d in range(BIAS_SLOTS):
        o_ref[d] = rows[:, d * MOBA_BLOCK:(d + 2) * MOBA_BLOCK]


def _bias_rows(rel_bias):
    n = (BIAS_SLOTS + 1) * MOBA_BLOCK
    dist = jnp.arange(n, dtype=jnp.int32) - MOBA_BLOCK
    bucket = jnp.where(dist >= 0, _t5_bucket(dist), -1)[None, :]
    return pl.pallas_call(
        _bias_rows_kernel,
        out_shape=jax.ShapeDtypeStruct((BIAS_SLOTS, N_ATTN_HEADS, 2 * MOBA_BLOCK), F32),
        name="bias_rows",
    )(bucket, rel_bias.astype(F32).T)


def _proj_kernel(x_ref, shift_ref, scale_ref, g_ref, wtok_ref, wfeat_ref,
                 ka_ref, kb_ref, ga_ref, gb_ref, qa_ref, va_ref, qb_ref, vb_ref):
    h = _rms_modulate(x_ref[...], g_ref[...], shift_ref[0, 0], scale_ref[0, 0]).astype(BF16)
    o = 0
    for ref, w in ((ka_ref, SWA_KV_W), (kb_ref, MOBA_W), (ga_ref, D_MODEL), (gb_ref, D_MODEL)):
        ref[...] = _dot(h, wtok_ref[:, o:o + w]).astype(ref.dtype)
        o += w
    o = 0
    for ref, w in ((qa_ref, SWA_Q_W), (va_ref, SWA_KV_W), (qb_ref, MOBA_W), (vb_ref, MOBA_W)):
        ft = _dot_nt(wfeat_ref[o:o + w, :], h).astype(ref.dtype)
        blk = ref.shape[-1]
        for t in range(TOKEN_TILE // blk):
            ref[0, t] = ft[:, t * blk:(t + 1) * blk]
        o += w


def _proj(x2, mod4, g1, wtok, wfeat, batch, seq):
    tokens = x2.shape[0]
    tiles_per_seq = seq // TOKEN_TILE
    n_tok = wtok.shape[1]
    n_feat = wfeat.shape[0]
    na = TOKEN_TILE // SWA_BLOCK
    nb = TOKEN_TILE // MOBA_BLOCK

    def feat_spec(width, blk, per_tile):
        return pl.BlockSpec((1, per_tile, width, blk),
                            lambda i: (i // tiles_per_seq, i % tiles_per_seq, 0, 0))

    out_shape = (
        jax.ShapeDtypeStruct((tokens, SWA_KV_W), BF16),
        jax.ShapeDtypeStruct((tokens, MOBA_W), BF16),
        jax.ShapeDtypeStruct((tokens, D_MODEL), BF16),
        jax.ShapeDtypeStruct((tokens, D_MODEL), BF16),
        jax.ShapeDtypeStruct((batch, seq // SWA_BLOCK, SWA_Q_W, SWA_BLOCK), BF16),
        jax.ShapeDtypeStruct((batch, seq // SWA_BLOCK, SWA_KV_W, SWA_BLOCK), BF16),
        jax.ShapeDtypeStruct((batch, seq // MOBA_BLOCK, MOBA_W, MOBA_BLOCK), BF16),
        jax.ShapeDtypeStruct((batch, seq // MOBA_BLOCK, MOBA_W, MOBA_BLOCK), BF16),
    )
    tok_spec = lambda w: pl.BlockSpec((TOKEN_TILE, w), lambda i: (i, 0))
    return pl.pallas_call(
        _proj_kernel,
        grid=(tokens // TOKEN_TILE,),
        in_specs=[
            pl.BlockSpec((TOKEN_TILE, D_MODEL), lambda i: (i, 0)),
            pl.BlockSpec((1, 1, 1, D_MODEL), lambda i: (i // tiles_per_seq, 0, 0, 0)),
            pl.BlockSpec((1, 1, 1, D_MODEL), lambda i: (i // tiles_per_seq, 1, 0, 0)),
            _resident((1, D_MODEL), lambda i: (0, 0)),
            _resident((D_MODEL, n_tok), lambda i: (0, 0)),
            _resident((n_feat, D_MODEL), lambda i: (0, 0)),
        ],
        out_specs=(tok_spec(SWA_KV_W), tok_spec(MOBA_W), tok_spec(D_MODEL), tok_spec(D_MODEL),
                   feat_spec(SWA_Q_W, SWA_BLOCK, na), feat_spec(SWA_KV_W, SWA_BLOCK, na),
                   feat_spec(MOBA_W, MOBA_BLOCK, nb), feat_spec(MOBA_W, MOBA_BLOCK, nb)),
        out_shape=out_shape,
        compiler_params=pltpu.CompilerParams(dimension_semantics=("arbitrary",),
                                             vmem_limit_bytes=VMEM_LIMIT_BYTES),
        name="proj",
    )(x2, mod4, mod4, g1, wtok, wfeat)


def _swa_kernel(q_ref, kp_ref, kc_ref, vp_ref, vc_ref, rows_ref, sink_ref, o_ref, bias_ref, ot_ref):
    n = pl.program_id(1)
    L = SWA_BLOCK

    @pl.when(jnp.logical_and(pl.program_id(0) == 0, n == 0))
    def _():
        k_idx = lax.broadcasted_iota(jnp.int32, (L, L), 0)
        q_idx = lax.broadcasted_iota(jnp.int32, (L, L), 1)
        for h in range(SWA_Q_HEADS):
            g, gi = divmod(h, SWA_GROUP)
            t0 = rows_ref[0, h:h + 1, MOBA_BLOCK:MOBA_BLOCK + L]
            r = _toeplitz(jnp.concatenate([t0, t0], axis=1), L)
            cols = slice(gi * L, (gi + 1) * L)
            bias_ref[g, 0:L, cols] = jnp.where(q_idx < k_idx, r, NEG_INF)
            bias_ref[g, L:2 * L, cols] = jnp.where(q_idx >= k_idx, r, NEG_INF)

    first_rows = jnp.where(n == 0, L, 0)
    key_row = lax.broadcasted_iota(jnp.int32, (2 * L, SWA_GROUP * L), 0)
    def scores(g):
        hs = slice(g * HEAD_DIM, (g + 1) * HEAD_DIM)
        kw = jnp.concatenate([kp_ref[0, :, hs], kc_ref[0, :, hs]], axis=0)
        qg = jnp.concatenate(
            [q_ref[0, 0, (g * SWA_GROUP + gi) * HEAD_DIM:(g * SWA_GROUP + gi + 1) * HEAD_DIM, :]
             for gi in range(SWA_GROUP)], axis=1)
        s = _dot(kw, qg) + bias_ref[g]
        return jnp.where(key_row < first_rows, NEG_INF, s)

    all_scores = [scores(g) for g in range(SWA_KV_HEADS)]
    for g, s in enumerate(all_scores):
        hs = slice(g * HEAD_DIM, (g + 1) * HEAD_DIM)
        sink = sink_ref[g]
        m = jnp.maximum(jnp.max(s, axis=0, keepdims=True), sink)
        p = jnp.exp2(s - m)
        denom = jnp.sum(p, axis=0, keepdims=True) + jnp.exp2(sink - m)
        vw = jnp.concatenate([vp_ref[0, 0, hs, :], vc_ref[0, 0, hs, :]], axis=1)
        o = _dot(vw, p.astype(BF16)) / denom
        for gi in range(SWA_GROUP):
            r = (g * SWA_GROUP + gi) * HEAD_DIM
            ot_ref[r:r + HEAD_DIM, :] = o[:, gi * L:(gi + 1) * L]
    o_ref[0] = ot_ref[...].T.astype(o_ref.dtype)


def _swa(qa_t, ka3, va_t, rows, sink_rows):
    batch, nblk = qa_t.shape[0], qa_t.shape[1]
    seq = nblk * SWA_BLOCK
    prev = lambda n: jnp.maximum(n - 1, 0)
    return pl.pallas_call(
        _swa_kernel,
        grid=(batch, nblk),
        in_specs=[
            pl.BlockSpec((1, 1, SWA_Q_W, SWA_BLOCK), lambda b, n: (b, n, 0, 0)),
            pl.BlockSpec((1, SWA_BLOCK, SWA_KV_W), lambda b, n: (b, prev(n), 0)),
            pl.BlockSpec((1, SWA_BLOCK, SWA_KV_W), lambda b, n: (b, n, 0)),
            pl.BlockSpec((1, 1, SWA_KV_W, SWA_BLOCK), lambda b, n: (b, prev(n), 0, 0)),
            pl.BlockSpec((1, 1, SWA_KV_W, SWA_BLOCK), lambda b, n: (b, n, 0, 0)),
            _resident((1, N_ATTN_HEADS, 2 * MOBA_BLOCK), lambda b, n: (0, 0, 0)),
            _resident(sink_rows.shape, lambda b, n: (0, 0, 0)),
        ],
        out_specs=pl.BlockSpec((1, SWA_BLOCK, SWA_Q_W), lambda b, n: (b, n, 0)),
        out_shape=jax.ShapeDtypeStruct((batch, seq, SWA_Q_W), BF16),
        scratch_shapes=[pltpu.VMEM((SWA_KV_HEADS, 2 * SWA_BLOCK, SWA_GROUP * SWA_BLOCK), F32),
                        pltpu.VMEM((SWA_Q_W, SWA_BLOCK), F32)],
        compiler_params=pltpu.CompilerParams(dimension_semantics=("arbitrary", "arbitrary"),
                                             vmem_limit_bytes=VMEM_LIMIT_BYTES),
        name="swa",
    )(qa_t, ka3, ka3, va_t, va_t, rows, sink_rows)


def _select_kernel(k_ref, q_ref, o_ref):
    seq = k_ref.shape[1]
    nblk = seq // MOBA_BLOCK
    blk = lax.broadcasted_iota(jnp.int32, (nblk, seq), 0)
    qblk = lax.broadcasted_iota(jnp.int32, (nblk, seq), 1) // MOBA_BLOCK
    past = blk < qblk
    member = jnp.where(blk == qblk, 1.0 / MOBA_BLOCK, 0.0).astype(BF16)
    for hh in range(2):
        hs = slice(hh * HEAD_DIM, (hh + 1) * HEAD_DIM)
        kmean = _dot(member, k_ref[0, :, hs])
        q_t = jnp.concatenate([q_ref[0, t, hs, :] for t in range(nblk)], axis=1)
        gate = jnp.dot(kmean, q_t.astype(F32), preferred_element_type=F32,
                       precision=lax.Precision.HIGHEST)
        gate = jnp.where(past, gate, NEG_INF)
        rank = jnp.zeros((nblk, seq), jnp.int32)
        for r in range(nblk):
            row = gate[r:r + 1, :]
            rank = rank + jnp.where(blk > r, jnp.where(row >= gate, 1, 0),
                                    jnp.where(row > gate, 1, 0))
        sel = jnp.logical_and(past, rank < MOBA_TOPK)
        o_ref[0, hh] = jnp.where(jnp.logical_or(sel, blk == qblk), 0.0, NEG_INF)


def _select(kb3, qb_t):
    batch, seq = kb3.shape[0], kb3.shape[1]
    nblk = seq // MOBA_BLOCK
    pairs = MOBA_HEADS // 2
    return pl.pallas_call(
        _select_kernel,
        grid=(batch, pairs),
        in_specs=[
            pl.BlockSpec((1, seq, 2 * HEAD_DIM), lambda b, hp: (b, 0, hp)),
            pl.BlockSpec((1, nblk, 2 * HEAD_DIM, MOBA_BLOCK), lambda b, hp: (b, 0, hp, 0)),
        ],
        out_specs=pl.BlockSpec((1, 2, nblk, seq), lambda b, hp: (b, hp, 0, 0)),
        out_shape=jax.ShapeDtypeStruct((batch, MOBA_HEADS, nblk, seq), F32),
        compiler_params=pltpu.CompilerParams(dimension_semantics=("arbitrary", "arbitrary"),
                                             vmem_limit_bytes=VMEM_LIMIT_BYTES),
        name="moba_select",
    )(kb3, qb_t)


def _moba_kernel(q_ref, k_ref, v_ref, sel_ref, rows_ref, o_ref, bias_ref, m_ref, l_ref, acc_ref):
    i = pl.program_id(1)
    MB = MOBA_BLOCK

    @pl.when(jnp.logical_and(pl.program_id(0) == 0, i == 0))
    def _():
        for h in range(MOBA_HEADS):
            for d in range(BIAS_SLOTS):
                bias_ref[h, d] = _toeplitz(rows_ref[d, SWA_Q_HEADS + h:SWA_Q_HEADS + h + 1, :], MB)

    m_ref[...] = jnp.full(m_ref.shape, NEG_INF, F32)
    l_ref[...] = jnp.zeros(l_ref.shape, F32)
    acc_ref[...] = jnp.zeros(acc_ref.shape, F32)

    def body(t, carry):
        j = i - t
        slot = jnp.minimum(t, BIAS_SLOTS - 1)
        rows = pl.ds(pl.multiple_of(j * MB, MB), MB)
        heads = [slice(h * HEAD_DIM, (h + 1) * HEAD_DIM) for h in range(MOBA_HEADS)]
        all_scores = [_dot(k_ref[0, rows, hs], q_ref[0, 0, hs, :])
                      + bias_ref[h, slot] + sel_ref[0, h, 0, j] for h, hs in enumerate(heads)]
        for h, (hs, s) in enumerate(zip(heads, all_scores)):
            m_old = m_ref[h]
            m_new = jnp.maximum(m_old, jnp.max(s, axis=0, keepdims=True))
            alpha = jnp.exp2(m_old - m_new)
            p = jnp.exp2(s - m_new)
            l_ref[h] = alpha * l_ref[h] + jnp.sum(p, axis=0, keepdims=True)
            acc_ref[hs, :] = alpha * acc_ref[hs, :] + _dot(v_ref[0, j, hs, :], p.astype(BF16))
            m_ref[h] = m_new
        return carry

    lax.fori_loop(0, i + 1, body, 0)
    for h in range(MOBA_HEADS):
        hs = slice(h * HEAD_DIM, (h + 1) * HEAD_DIM)
        acc_ref[hs, :] = acc_ref[hs, :] / l_ref[h]
    o_ref[0] = acc_ref[...].T.astype(o_ref.dtype)


def _moba(qb_t, kb3, vb_t, sel6, rows):
    batch, nblk = qb_t.shape[0], qb_t.shape[1]
    seq = nblk * MOBA_BLOCK
    return pl.pallas_call(
        _moba_kernel,
        grid=(batch, nblk),
        in_specs=[
            pl.BlockSpec((1, 1, MOBA_W, MOBA_BLOCK), lambda b, i: (b, i, 0, 0)),
            pl.BlockSpec((1, seq, MOBA_W), lambda b, i: (b, 0, 0)),
            pl.BlockSpec((1, nblk, MOBA_W, MOBA_BLOCK), lambda b, i: (b, 0, 0, 0)),
            pl.BlockSpec((1, MOBA_HEADS, 1, nblk, 1, MOBA_BLOCK), lambda b, i: (b, 0, i, 0, 0, 0)),
            _resident(rows.shape, lambda b, i: (0, 0, 0)),
        ],
        out_specs=pl.BlockSpec((1, MOBA_BLOCK, MOBA_W), lambda b, i: (b, i, 0)),
        out_shape=jax.ShapeDtypeStruct((batch, seq, MOBA_W), BF16),
        scratch_shapes=[pltpu.VMEM((MOBA_HEADS, BIAS_SLOTS, MOBA_BLOCK, MOBA_BLOCK), F32),
                        pltpu.VMEM((MOBA_HEADS, 1, MOBA_BLOCK), F32),
                        pltpu.VMEM((MOBA_HEADS, 1, MOBA_BLOCK), F32),
                        pltpu.VMEM((MOBA_W, MOBA_BLOCK), F32)],
        compiler_params=pltpu.CompilerParams(dimension_semantics=("arbitrary", "arbitrary"),
                                             vmem_limit_bytes=VMEM_LIMIT_BYTES),
        name="moba",
    )(qb_t, kb3, vb_t, sel6, rows)


def _post_kernel(x_ref, ya_ref, yb_ref, ga_ref, gb_ref, gate1_ref, shift2_ref, scale2_ref,
                 gate2_ref, g2_ref, gf_ref, wa_ref, wb_ref, wo_ref, w1_ref, w2_ref, o_ref):
    a = _dot(ya_ref[...], wa_ref[...])
    b = _dot(yb_ref[...], wb_ref[...])
    merged = (jax.nn.sigmoid(ga_ref[...].astype(F32)) * a
              + jax.nn.sigmoid(gb_ref[...].astype(F32)) * b)
    x1 = x_ref[...] + gate1_ref[0, 0] * _dot(merged.astype(BF16), wo_ref[...])
    h2 = _rms_modulate(x1, g2_ref[...], shift2_ref[0, 0], scale2_ref[0, 0]).astype(BF16)
    y = jnp.zeros_like(x1)
    for c in range(D_FF // FF_CHUNK):
        cs = slice(c * FF_CHUNK, (c + 1) * FF_CHUNK)
        u = jnp.square(jnp.maximum(_dot(h2, w1_ref[:, cs]), 0.0)).astype(BF16)
        y = y + _dot(u, w2_ref[cs, :])
    x2 = x1 + gate2_ref[0, 0] * y
    ms = jnp.mean(x2 * x2, axis=-1, keepdims=True)
    o_ref[...] = (x2 * lax.rsqrt(ms + RMS_EPS)) * gf_ref[...]


def _post(x2, ya, yb, ga, gb, mod4, g2, gf, wa, wb, wo, w1, w2, seq):
    tokens = x2.shape[0]
    tiles_per_seq = seq // TOKEN_TILE
    tok = lambda w: pl.BlockSpec((TOKEN_TILE, w), lambda i: (i, 0))
    modrow = lambda k: pl.BlockSpec((1, 1, 1, D_MODEL), lambda i: (i // tiles_per_seq, k, 0, 0))
    full = lambda a: _resident(a.shape, lambda i: (0, 0))
    return pl.pallas_call(
        _post_kernel,
        grid=(tokens // TOKEN_TILE,),
        in_specs=[tok(D_MODEL), tok(SWA_Q_W), tok(MOBA_W), tok(D_MODEL), tok(D_MODEL),
                  modrow(2), modrow(3), modrow(4), modrow(5),
                  full(g2), full(gf), full(wa), full(wb), full(wo), full(w1), full(w2)],
        out_specs=tok(D_MODEL),
        out_shape=jax.ShapeDtypeStruct((tokens, D_MODEL), F32),
        compiler_params=pltpu.CompilerParams(dimension_semantics=("arbitrary",),
                                             vmem_limit_bytes=VMEM_LIMIT_BYTES),
        name="post",
    )(x2, ya, yb, ga, gb, mod4, mod4, mod4, mod4, g2, gf, wa, wb, wo, w1, w2)


def kernel(x, c, ada_w, ada_b, norm1_g, norm2_g, w_in, attn_sinks, rel_bias, w_branch_a,
           w_branch_b, w_out, w_mlp_in, w_mlp_out, final_g):
    batch, seq, _ = x.shape
    depth = ada_w.shape[0]
    nblk_b = seq // MOBA_BLOCK
    assert seq % TOKEN_TILE == 0 and TOKEN_TILE % MOBA_BLOCK == 0 and batch <= 8
    assert (BIAS_SLOTS - 1) * MOBA_BLOCK - (MOBA_BLOCK - 1) >= 1513 and MAX_DISTANCE == 2048
    assert 2 * SWA_BLOCK <= MOBA_BLOCK and SWA_WINDOW == SWA_BLOCK
    assert depth == 1
    l = 0

    rows = _bias_rows(rel_bias)
    c_pad = jnp.zeros((8, D_MODEL), F32).at[:batch].set(c)

    widths = (SWA_Q_W, SWA_KV_W, SWA_KV_W, MOBA_W, MOBA_W, MOBA_W, D_MODEL, D_MODEL)
    offs = [0]
    for w in widths:
        offs.append(offs[-1] + w)
    col = lambda k: slice(offs[k], offs[k + 1])

    x2 = x.reshape(batch * seq, D_MODEL)
    mod = _ada(c_pad, ada_w[l], ada_b[l][None, :])
    mod4 = mod[:batch].reshape(batch, N_MOD, 1, D_MODEL)
    wl = w_in[l]
    q_scale = ATTN_SCALE * LOG2E
    wtok = jnp.concatenate([wl[:, col(1)], wl[:, col(4)], wl[:, col(6)], wl[:, col(7)]],
                           axis=1).astype(BF16)
    wfeat = jnp.concatenate([wl[:, col(0)] * q_scale, wl[:, col(2)],
                             wl[:, col(3)] * q_scale, wl[:, col(5)]], axis=1).T.astype(BF16)
    ka, kb, ga, gb, qa_t, va_t, qb_t, vb_t = _proj(
        x2, mod4, norm1_g[l][None, :], wtok, wfeat, batch, seq)

    sink_rows = jnp.broadcast_to(
        (attn_sinks[l].astype(F32) * LOG2E).reshape(SWA_KV_HEADS, 1, SWA_GROUP, 1),
        (SWA_KV_HEADS, 1, SWA_GROUP, SWA_BLOCK)).reshape(SWA_KV_HEADS, 1, SWA_GROUP * SWA_BLOCK)
    ya = _swa(qa_t, ka.reshape(batch, seq, SWA_KV_W), va_t, rows, sink_rows)

    kb3 = kb.reshape(batch, seq, MOBA_W)
    sel = _select(kb3, qb_t)
    sel6 = sel.reshape(batch, MOBA_HEADS, nblk_b, nblk_b, 1, MOBA_BLOCK).transpose(0, 1, 3, 2, 4, 5)
    yb = _moba(qb_t, kb3, vb_t, sel6, rows)

    out = _post(x2, ya.reshape(batch * seq, SWA_Q_W), yb.reshape(batch * seq, MOBA_W), ga, gb,
                mod4, norm2_g[l][None, :], final_g[None, :],
                w_branch_a[l].astype(BF16), w_branch_b[l].astype(BF16), w_out[l].astype(BF16),
                w_mlp_in[l].astype(BF16), w_mlp_out[l].astype(BF16), seq)
    return out.reshape(batch, seq, D_MODEL)
```

```python
import math

import jax
import jax.numpy as jnp
from jax import lax
from jax.experimental import pallas as pl
from jax.experimental.pallas import tpu as pltpu

D_MODEL = 1024
HEAD_DIM = 64
ATTN_SCALE = HEAD_DIM ** -0.5
LOG2E = math.log2(math.e)
SWA_Q_HEADS = 8
SWA_KV_HEADS = 2
SWA_GROUP = SWA_Q_HEADS // SWA_KV_HEADS
SWA_WINDOW = 128
SWA_BLOCK = 128
MOBA_HEADS = 8
MOBA_BLOCK = 256
MOBA_TOPK = 3
NUM_BUCKETS = 32
MAX_EXACT = NUM_BUCKETS // 2
MAX_DISTANCE = 2048
N_ATTN_HEADS = SWA_Q_HEADS + MOBA_HEADS
SWA_Q_W = SWA_Q_HEADS * HEAD_DIM
SWA_KV_W = SWA_KV_HEADS * HEAD_DIM
MOBA_W = MOBA_HEADS * HEAD_DIM
D_FF = 4 * D_MODEL
N_MOD = 6
RMS_EPS = 1e-6

VMEM_LIMIT_BYTES = 56 * 1024 * 1024
TOKEN_TILE = 512
FF_CHUNK = 1024
BIAS_SLOTS = 8
NEG_INF = float("-inf")

F32 = jnp.float32
BF16 = jnp.bfloat16


def _resident(block_shape, index_map):
    return pl.BlockSpec(block_shape, index_map, pipeline_mode=pl.Buffered(1))


def _dot(a, b):
    return jnp.dot(a, b, preferred_element_type=F32)


def _dot_nt(a, b):
    return lax.dot_general(a, b, (((1,), (1,)), ((), ())), preferred_element_type=F32)


def _rms_modulate(xv, g, shift, scale):
    ms = jnp.mean(xv * xv, axis=-1, keepdims=True)
    y = xv * lax.rsqrt(ms + RMS_EPS)
    return (y * g) * (1.0 + scale) + shift


def _toeplitz(u, rows):
    wide = jnp.broadcast_to(u, (rows, 2 * rows))
    return pltpu.roll(wide, 0, 1, stride=1, stride_axis=0)[:, rows:]


def _ada_kernel(c_ref, w_ref, b_ref, o_ref):
    cs = jax.nn.silu(c_ref[...])
    o_ref[...] = jnp.dot(cs, w_ref[...], preferred_element_type=F32,
                         precision=lax.Precision.HIGHEST) + b_ref[...]


def _ada(c_pad, w, b):
    n = w.shape[1]
    bn = 1536
    return pl.pallas_call(
        _ada_kernel,
        grid=(n // bn,),
        in_specs=[pl.BlockSpec((8, D_MODEL), lambda j: (0, 0)),
                  pl.BlockSpec((D_MODEL, bn), lambda j: (0, j)),
                  pl.BlockSpec((1, bn), lambda j: (0, j))],
        out_specs=pl.BlockSpec((8, bn), lambda j: (0, j)),
        out_shape=jax.ShapeDtypeStruct((8, n), F32),
        compiler_params=pltpu.CompilerParams(dimension_semantics=("arbitrary",),
                                             vmem_limit_bytes=VMEM_LIMIT_BYTES),
        name="ada",
    )(c_pad, w, b)


def _t5_bucket(dist):
    n = jnp.maximum(dist, 0)
    nf = jnp.maximum(n, 1).astype(F32)
    large = MAX_EXACT + (jnp.log(nf / MAX_EXACT) / math.log(MAX_DISTANCE / MAX_EXACT)
                         * (NUM_BUCKETS - MAX_EXACT)).astype(jnp.int32)
    large = jnp.minimum(large, NUM_BUCKETS - 1)
    return jnp.where(n < MAX_EXACT, n, large)


def _bias_rows_kernel(bucket_ref, rbt_ref, o_ref):
    bucket = bucket_ref[...]
    n = bucket.shape[1]
    onehot = jnp.where(lax.broadcasted_iota(jnp.int32, (NUM_BUCKETS, n), 0) == bucket, 1.0, 0.0)
    rows = jnp.dot(rbt_ref[...], onehot, preferred_element_type=F32,
                   precision=lax.Precision.HIGHEST)
    rows = jnp.where(bucket >= 0, rows * LOG2E, NEG_INF)
    for d in range(BIAS_SLOTS):
        o_ref[d] = rows[:, d * MOBA_BLOCK:(d + 2) * MOBA_BLOCK]


def _bias_rows(rel_bias):
    n = (BIAS_SLOTS + 1) * MOBA_BLOCK
    dist = jnp.arange(n, dtype=jnp.int32) - MOBA_BLOCK
    bucket = jnp.where(dist >= 0, _t5_bucket(dist), -1)[None, :]
    return pl.pallas_call(
        _bias_rows_kernel,
        out_shape=jax.ShapeDtypeStruct((BIAS_SLOTS, N_ATTN_HEADS, 2 * MOBA_BLOCK), F32),
        name="bias_rows",
    )(bucket, rel_bias.astype(F32).T)


def _proj_kernel(x_ref, shift_ref, scale_ref, g_ref, wtok_ref, wfeat_ref,
                 ka_ref, kb_ref, ga_ref, gb_ref, qa_ref, va_ref, qb_ref, vb_ref):
    h = _rms_modulate(x_ref[...], g_ref[...], shift_ref[0, 0], scale_ref[0, 0]).astype(BF16)
    o = 0
    for ref, w in ((ka_ref, SWA_KV_W), (kb_ref, MOBA_W), (ga_ref, D_MODEL), (gb_ref, D_MODEL)):
        ref[...] = _dot(h, wtok_ref[:, o:o + w]).astype(ref.dtype)
        o += w
    o = 0
    for ref, w in ((qa_ref, SWA_Q_W), (va_ref, SWA_KV_W), (qb_ref, MOBA_W), (vb_ref, MOBA_W)):
        ft = _dot_nt(wfeat_ref[o:o + w, :], h).astype(ref.dtype)
        blk = ref.shape[-1]
        for t in range(TOKEN_TILE // blk):
            ref[0, t] = ft[:, t * blk:(t + 1) * blk]
        o += w


def _proj(x2, mod4, g1, wtok, wfeat, batch, seq):
    tokens = x2.shape[0]
    tiles_per_seq = seq // TOKEN_TILE
    n_tok = wtok.shape[1]
    n_feat = wfeat.shape[0]
    na = TOKEN_TILE // SWA_BLOCK
    nb = TOKEN_TILE // MOBA_BLOCK

    def feat_spec(width, blk, per_tile):
        return pl.BlockSpec((1, per_tile, width, blk),
                            lambda i: (i // tiles_per_seq, i % tiles_per_seq, 0, 0))

    out_shape = (
        jax.ShapeDtypeStruct((tokens, SWA_KV_W), BF16),
        jax.ShapeDtypeStruct((tokens, MOBA_W), BF16),
        jax.ShapeDtypeStruct((tokens, D_MODEL), BF16),
        jax.ShapeDtypeStruct((tokens, D_MODEL), BF16),
        jax.ShapeDtypeStruct((batch, seq // SWA_BLOCK, SWA_Q_W, SWA_BLOCK), BF16),
        jax.ShapeDtypeStruct((batch, seq // SWA_BLOCK, SWA_KV_W, SWA_BLOCK), BF16),
        jax.ShapeDtypeStruct((batch, seq // MOBA_BLOCK, MOBA_W, MOBA_BLOCK), BF16),
        jax.ShapeDtypeStruct((batch, seq // MOBA_BLOCK, MOBA_W, MOBA_BLOCK), BF16),
    )
    tok_spec = lambda w: pl.BlockSpec((TOKEN_TILE, w), lambda i: (i, 0))
    return pl.pallas_call(
        _proj_kernel,
        grid=(tokens // TOKEN_TILE,),
        in_specs=[
            pl.BlockSpec((TOKEN_TILE, D_MODEL), lambda i: (i, 0)),
            pl.BlockSpec((1, 1, 1, D_MODEL), lambda i: (i // tiles_per_seq, 0, 0, 0)),
            pl.BlockSpec((1, 1, 1, D_MODEL), lambda i: (i // tiles_per_seq, 1, 0, 0)),
            _resident((1, D_MODEL), lambda i: (0, 0)),
            _resident((D_MODEL, n_tok), lambda i: (0, 0)),
            _resident((n_feat, D_MODEL), lambda i: (0, 0)),
        ],
        out_specs=(tok_spec(SWA_KV_W), tok_spec(MOBA_W), tok_spec(D_MODEL), tok_spec(D_MODEL),
                   feat_spec(SWA_Q_W, SWA_BLOCK, na), feat_spec(SWA_KV_W, SWA_BLOCK, na),
                   feat_spec(MOBA_W, MOBA_BLOCK, nb), feat_spec(MOBA_W, MOBA_BLOCK, nb)),
        out_shape=out_shape,
        compiler_params=pltpu.CompilerParams(dimension_semantics=("arbitrary",),
                                             vmem_limit_bytes=VMEM_LIMIT_BYTES),
        name="proj",
    )(x2, mod4, mod4, g1, wtok, wfeat)


def _swa_kernel(q_ref, kp_ref, kc_ref, vp_ref, vc_ref, rows_ref, sink_ref, o_ref, bias_ref, ot_ref):
    n = pl.program_id(1)
    L = SWA_BLOCK

    @pl.when(jnp.logical_and(pl.program_id(0) == 0, n == 0))
    def _():
        k_idx = lax.broadcasted_iota(jnp.int32, (L, L), 0)
        q_idx = lax.broadcasted_iota(jnp.int32, (L, L), 1)
        for h in range(SWA_Q_HEADS):
            g, gi = divmod(h, SWA_GROUP)
            t0 = rows_ref[0, h:h + 1, MOBA_BLOCK:MOBA_BLOCK + L]
            r = _toeplitz(jnp.concatenate([t0, t0], axis=1), L)
            cols = slice(gi * L, (gi + 1) * L)
            bias_ref[g, 0:L, cols] = jnp.where(q_idx < k_idx, r, NEG_INF)
            bias_ref[g, L:2 * L, cols] = jnp.where(q_idx >= k_idx, r, NEG_INF)

    first_rows = jnp.where(n == 0, L, 0)
    key_row = lax.broadcasted_iota(jnp.int32, (2 * L, SWA_GROUP * L), 0)
    def scores(g):
        hs = slice(g * HEAD_DIM, (g + 1) * HEAD_DIM)
        kw = jnp.concatenate([kp_ref[0, :, hs], kc_ref[0, :, hs]], axis=0)
        qg = jnp.concatenate(
            [q_ref[0, 0, (g * SWA_GROUP + gi) * HEAD_DIM:(g * SWA_GROUP + gi + 1) * HEAD_DIM, :]
             for gi in range(SWA_GROUP)], axis=1)
        s = _dot(kw, qg) + bias_ref[g]
        return jnp.where(key_row < first_rows, NEG_INF, s)

    all_scores = [scores(g) for g in range(SWA_KV_HEADS)]
    for g, s in enumerate(all_scores):
        hs = slice(g * HEAD_DIM, (g + 1) * HEAD_DIM)
        sink = sink_ref[g]
        m = jnp.maximum(jnp.max(s, axis=0, keepdims=True), sink)
        p = jnp.exp2(s - m)
        denom = jnp.sum(p, axis=0, keepdims=True) + jnp.exp2(sink - m)
        vw = jnp.concatenate([vp_ref[0, 0, hs, :], vc_ref[0, 0, hs, :]], axis=1)
        o = _dot(vw, p.astype(BF16)) / denom
        for gi in range(SWA_GROUP):
            r = (g * SWA_GROUP + gi) * HEAD_DIM
            ot_ref[r:r + HEAD_DIM, :] = o[:, gi * L:(gi + 1) * L]
    o_ref[0] = ot_ref[...].T.astype(o_ref.dtype)


def _swa(qa_t, ka3, va_t, rows, sink_rows):
    batch, nblk = qa_t.shape[0], qa_t.shape[1]
    seq = nblk * SWA_BLOCK
    prev = lambda n: jnp.maximum(n - 1, 0)
    return pl.pallas_call(
        _swa_kernel,
        grid=(batch, nblk),
        in_specs=[
            pl.BlockSpec((1, 1, SWA_Q_W, SWA_BLOCK), lambda b, n: (b, n, 0, 0)),
            pl.BlockSpec((1, SWA_BLOCK, SWA_KV_W), lambda b, n: (b, prev(n), 0)),
            pl.BlockSpec((1, SWA_BLOCK, SWA_KV_W), lambda b, n: (b, n, 0)),
            pl.BlockSpec((1, 1, SWA_KV_W, SWA_BLOCK), lambda b, n: (b, prev(n), 0, 0)),
            pl.BlockSpec((1, 1, SWA_KV_W, SWA_BLOCK), lambda b, n: (b, n, 0, 0)),
            _resident((1, N_ATTN_HEADS, 2 * MOBA_BLOCK), lambda b, n: (0, 0, 0)),
            _resident(sink_rows.shape, lambda b, n: (0, 0, 0)),
        ],
        out_specs=pl.BlockSpec((1, SWA_BLOCK, SWA_Q_W), lambda b, n: (b, n, 0)),
        out_shape=jax.ShapeDtypeStruct((batch, seq, SWA_Q_W), BF16),
        scratch_shapes=[pltpu.VMEM((SWA_KV_HEADS, 2 * SWA_BLOCK, SWA_GROUP * SWA_BLOCK), F32),
                        pltpu.VMEM((SWA_Q_W, SWA_BLOCK), F32)],
        compiler_params=pltpu.CompilerParams(dimension_semantics=("arbitrary", "arbitrary"),
                                             vmem_limit_bytes=VMEM_LIMIT_BYTES),
        name="swa",
    )(qa_t, ka3, ka3, va_t, va_t, rows, sink_rows)


def _select_kernel(k_ref, q_ref, o_ref):
    seq = k_ref.shape[1]
    nblk = seq // MOBA_BLOCK
    blk = lax.broadcasted_iota(jnp.int32, (nblk, seq), 0)
    qblk = lax.broadcasted_iota(jnp.int32, (nblk, seq), 1) // MOBA_BLOCK
    past = blk < qblk
    member = jnp.where(blk == qblk, 1.0 / MOBA_BLOCK, 0.0).astype(BF16)
    for hh in range(2):
        hs = slice(hh * HEAD_DIM, (hh + 1) * HEAD_DIM)
        kmean = _dot(member, k_ref[0, :, hs])
        q_t = jnp.concatenate([q_ref[0, t, hs, :] for t in range(nblk)], axis=1)
        gate = jnp.dot(kmean, q_t.astype(F32), preferred_element_type=F32,
                       precision=lax.Precision.HIGHEST)
        gate = jnp.where(past, gate, NEG_INF)
        rank = jnp.zeros((nblk, seq), jnp.int32)
        for r in range(nblk):
            row = gate[r:r + 1, :]
            rank = rank + jnp.where(blk > r, jnp.where(row >= gate, 1, 0),
                                    jnp.where(row > gate, 1, 0))
        sel = jnp.logical_and(past, rank < MOBA_TOPK)
        o_ref[0, hh] = jnp.where(jnp.logical_or(sel, blk == qblk), 0.0, NEG_INF)


def _select(kb3, qb_t):
    batch, seq = kb3.shape[0], kb3.shape[1]
    nblk = seq // MOBA_BLOCK
    pairs = MOBA_HEADS // 2
    return pl.pallas_call(
        _select_kernel,
        grid=(batch, pairs),
        in_specs=[
            pl.BlockSpec((1, seq, 2 * HEAD_DIM), lambda b, hp: (b, 0, hp)),
            pl.BlockSpec((1, nblk, 2 * HEAD_DIM, MOBA_BLOCK), lambda b, hp: (b, 0, hp, 0)),
        ],
        out_specs=pl.BlockSpec((1, 2, nblk, seq), lambda b, hp: (b, hp, 0, 0)),
        out_shape=jax.ShapeDtypeStruct((batch, MOBA_HEADS, nblk, seq), F32),
        compiler_params=pltpu.CompilerParams(dimension_semantics=("arbitrary", "arbitrary"),
                                             vmem_limit_bytes=VMEM_LIMIT_BYTES),
        name="moba_select",
    )(kb3, qb_t)


ACC_ROWS = HEAD_DIM + 16


def _moba_kernel(q_ref, k_ref, v_ref, sel_ref, rows_ref, o_ref, bias_ref, m_ref, acc_ref, s_ref, cm_ref):
    i = pl.program_id(1)
    MB = MOBA_BLOCK

    @pl.when(jnp.logical_and(pl.program_id(0) == 0, i == 0))
    def _():
        for h in range(MOBA_HEADS):
            for d in range(BIAS_SLOTS):
                bias_ref[h, d] = _toeplitz(rows_ref[d, SWA_Q_HEADS + h:SWA_Q_HEADS + h + 1, :], MB)

    m_ref[...] = jnp.full(m_ref.shape, NEG_INF, F32)
    acc_ref[...] = jnp.zeros(acc_ref.shape, F32)
    heads = [slice(h * HEAD_DIM, (h + 1) * HEAD_DIM) for h in range(MOBA_HEADS)]
    ones_blk = jnp.concatenate([jnp.ones((1, MB), BF16), jnp.zeros((ACC_ROWS - HEAD_DIM - 1, MB), BF16)],
                               axis=0)

    def scores(j, slot, h, hs):
        rows = pl.ds(pl.multiple_of(j * MB, MB), MB)
        s = _dot(k_ref[0, rows, hs], q_ref[0, 0, hs, :]) + bias_ref[h, slot] + sel_ref[0, h, pl.ds(j, 1), :]
        s_ref[h] = s
        cm_ref[h] = jnp.max(s, axis=0, keepdims=True)

    def accumulate(j, h, hs):
        m_old = m_ref[h]
        m_new = jnp.maximum(m_old, cm_ref[h])
        alpha = jnp.exp2(m_old - m_new)
        p = jnp.exp2(s_ref[h] - m_new).astype(BF16)
        v_aug = jnp.concatenate([v_ref[0, j, hs, :], ones_blk], axis=0)
        rs = slice(h * ACC_ROWS, (h + 1) * ACC_ROWS)
        acc_ref[rs, :] = alpha * acc_ref[rs, :] + _dot(v_aug, p)
        m_ref[h] = m_new

    for h, hs in enumerate(heads):
        scores(i, 0, h, hs)

    def body(t, carry):
        j = i - t
        slot = jnp.minimum(t + 1, BIAS_SLOTS - 1)
        for h, hs in enumerate(heads):
            accumulate(j, h, hs)
            scores(j - 1, slot, h, hs)
        return carry

    lax.fori_loop(0, i, body, 0)
    outs = []
    for h, hs in enumerate(heads):
        accumulate(0, h, hs)
        base = h * ACC_ROWS
        outs.append(acc_ref[base:base + HEAD_DIM, :] / acc_ref[base + HEAD_DIM:base + HEAD_DIM + 1, :])
    o_ref[0] = jnp.concatenate(outs, axis=0).T.astype(o_ref.dtype)


def _moba(qb_t, kb3, vb_t, sel, rows):
    batch, nblk = qb_t.shape[0], qb_t.shape[1]
    seq = nblk * MOBA_BLOCK
    return pl.pallas_call(
        _moba_kernel,
        grid=(batch, nblk),
        in_specs=[
            pl.BlockSpec((1, 1, MOBA_W, MOBA_BLOCK), lambda b, i: (b, i, 0, 0)),
            pl.BlockSpec((1, seq, MOBA_W), lambda b, i: (b, 0, 0)),
            pl.BlockSpec((1, nblk, MOBA_W, MOBA_BLOCK), lambda b, i: (b, 0, 0, 0)),
            pl.BlockSpec((1, MOBA_HEADS, nblk, MOBA_BLOCK), lambda b, i: (b, 0, 0, i)),
            _resident(rows.shape, lambda b, i: (0, 0, 0)),
        ],
        out_specs=pl.BlockSpec((1, MOBA_BLOCK, MOBA_W), lambda b, i: (b, i, 0)),
        out_shape=jax.ShapeDtypeStruct((batch, seq, MOBA_W), BF16),
        scratch_shapes=[pltpu.VMEM((MOBA_HEADS, BIAS_SLOTS, MOBA_BLOCK, MOBA_BLOCK), F32),
                        pltpu.VMEM((MOBA_HEADS, 1, MOBA_BLOCK), F32),
                        pltpu.VMEM((MOBA_HEADS * ACC_ROWS, MOBA_BLOCK), F32),
                        pltpu.VMEM((MOBA_HEADS, MOBA_BLOCK, MOBA_BLOCK), F32),
                        pltpu.VMEM((MOBA_HEADS, 1, MOBA_BLOCK), F32)],
        compiler_params=pltpu.CompilerParams(dimension_semantics=("arbitrary", "arbitrary"),
                                             vmem_limit_bytes=VMEM_LIMIT_BYTES),
        name="moba",
    )(qb_t, kb3, vb_t, sel, rows)


def _post_kernel(x_ref, ya_ref, yb_ref, ga_ref, gb_ref, gate1_ref, shift2_ref, scale2_ref,
                 gate2_ref, g2_ref, gf_ref, wa_ref, wb_ref, wo_ref, w1_ref, w2_ref, o_ref):
    a = _dot(ya_ref[...], wa_ref[...])
    b = _dot(yb_ref[...], wb_ref[...])
    merged = (jax.nn.sigmoid(ga_ref[...].astype(F32)) * a
              + jax.nn.sigmoid(gb_ref[...].astype(F32)) * b)
    x1 = x_ref[...] + gate1_ref[0, 0] * _dot(merged.astype(BF16), wo_ref[...])
    h2 = _rms_modulate(x1, g2_ref[...], shift2_ref[0, 0], scale2_ref[0, 0]).astype(BF16)
    y = jnp.zeros_like(x1)
    for c in range(D_FF // FF_CHUNK):
        cs = slice(c * FF_CHUNK, (c + 1) * FF_CHUNK)
        u = jnp.square(jnp.maximum(_dot(h2, w1_ref[:, cs]), 0.0)).astype(BF16)
        y = y + _dot(u, w2_ref[cs, :])
    x2 = x1 + gate2_ref[0, 0] * y
    ms = jnp.mean(x2 * x2, axis=-1, keepdims=True)
    o_ref[...] = (x2 * lax.rsqrt(ms + RMS_EPS)) * gf_ref[...]


def _post(x2, ya, yb, ga, gb, mod4, g2, gf, wa, wb, wo, w1, w2, seq):
    tokens = x2.shape[0]
    tiles_per_seq = seq // TOKEN_TILE
    tok = lambda w: pl.BlockSpec((TOKEN_TILE, w), lambda i: (i, 0))
    modrow = lambda k: pl.BlockSpec((1, 1, 1, D_MODEL), lambda i: (i // tiles_per_seq, k, 0, 0))
    full = lambda a: _resident(a.shape, lambda i: (0, 0))
    return pl.pallas_call(
        _post_kernel,
        grid=(tokens // TOKEN_TILE,),
        in_specs=[tok(D_MODEL), tok(SWA_Q_W), tok(MOBA_W), tok(D_MODEL), tok(D_MODEL),
                  modrow(2), modrow(3), modrow(4), modrow(5),
                  full(g2), full(gf), full(wa), full(wb), full(wo), full(w1), full(w2)],
        out_specs=tok(D_MODEL),
        out_shape=jax.ShapeDtypeStruct((tokens, D_MODEL), F32),
        compiler_params=pltpu.CompilerParams(dimension_semantics=("arbitrary",),
                                             vmem_limit_bytes=VMEM_LIMIT_BYTES),
        name="post",
    )(x2, ya, yb, ga, gb, mod4, mod4, mod4, mod4, g2, gf, wa, wb, wo, w1, w2)


def kernel(x, c, ada_w, ada_b, norm1_g, norm2_g, w_in, attn_sinks, rel_bias, w_branch_a,
           w_branch_b, w_out, w_mlp_in, w_mlp_out, final_g):
    batch, seq, _ = x.shape
    depth = ada_w.shape[0]
    nblk_b = seq // MOBA_BLOCK
    assert seq % TOKEN_TILE == 0 and TOKEN_TILE % MOBA_BLOCK == 0 and batch <= 8
    assert (BIAS_SLOTS - 1) * MOBA_BLOCK - (MOBA_BLOCK - 1) >= 1513 and MAX_DISTANCE == 2048
    assert 2 * SWA_BLOCK <= MOBA_BLOCK and SWA_WINDOW == SWA_BLOCK
    assert depth == 1
    l = 0

    rows = _bias_rows(rel_bias)
    c_pad = jnp.zeros((8, D_MODEL), F32).at[:batch].set(c)

    widths = (SWA_Q_W, SWA_KV_W, SWA_KV_W, MOBA_W, MOBA_W, MOBA_W, D_MODEL, D_MODEL)
    offs = [0]
    for w in widths:
        offs.append(offs[-1] + w)
    col = lambda k: slice(offs[k], offs[k + 1])

    x2 = x.reshape(batch * seq, D_MODEL)
    mod = _ada(c_pad, ada_w[l], ada_b[l][None, :])
    mod4 = mod[:batch].reshape(batch, N_MOD, 1, D_MODEL)
    wl = w_in[l]
    q_scale = ATTN_SCALE * LOG2E
    wtok = jnp.concatenate([wl[:, col(1)], wl[:, col(4)], wl[:, col(6)], wl[:, col(7)]],
                           axis=1).astype(BF16)
    wfeat = jnp.concatenate([wl[:, col(0)] * q_scale, wl[:, col(2)],
                             wl[:, col(3)] * q_scale, wl[:, col(5)]], axis=1).T.astype(BF16)
    ka, kb, ga, gb, qa_t, va_t, qb_t, vb_t = _proj(
        x2, mod4, norm1_g[l][None, :], wtok, wfeat, batch, seq)

    sink_rows = jnp.broadcast_to(
        (attn_sinks[l].astype(F32) * LOG2E).reshape(SWA_KV_HEADS, 1, SWA_GROUP, 1),
        (SWA_KV_HEADS, 1, SWA_GROUP, SWA_BLOCK)).reshape(SWA_KV_HEADS, 1, SWA_GROUP * SWA_BLOCK)
    ya = _swa(qa_t, ka.reshape(batch, seq, SWA_KV_W), va_t, rows, sink_rows)

    kb3 = kb.reshape(batch, seq, MOBA_W)
    sel = _select(kb3, qb_t)
    yb = _moba(qb_t, kb3, vb_t, sel, rows)

    out = _post(x2, ya.reshape(batch * seq, SWA_Q_W), yb.reshape(batch * seq, MOBA_W), ga, gb,
                mod4, norm2_g[l][None, :], final_g[None, :],
                w_branch_a[l].astype(BF16), w_branch_b[l].astype(BF16), w_out[l].astype(BF16),
                w_mlp_in[l].astype(BF16), w_mlp_out[l].astype(BF16), seq)
    return out.reshape(batch, seq, D_MODEL)
```

```python
import math

import jax
import jax.numpy as jnp
from jax import lax
from jax.experimental import pallas as pl
from jax.experimental.pallas import tpu as pltpu

D_MODEL = 1024
HEAD_DIM = 64
ATTN_SCALE = HEAD_DIM ** -0.5
LOG2E = math.log2(math.e)
SWA_Q_HEADS = 8
SWA_KV_HEADS = 2
SWA_GROUP = SWA_Q_HEADS // SWA_KV_HEADS
SWA_WINDOW = 128
SWA_BLOCK = 128
MOBA_HEADS = 8
MOBA_BLOCK = 256
MOBA_TOPK = 3
NUM_BUCKETS = 32
MAX_EXACT = NUM_BUCKETS // 2
MAX_DISTANCE = 2048
N_ATTN_HEADS = SWA_Q_HEADS + MOBA_HEADS
SWA_Q_W = SWA_Q_HEADS * HEAD_DIM
SWA_KV_W = SWA_KV_HEADS * HEAD_DIM
MOBA_W = MOBA_HEADS * HEAD_DIM
D_FF = 4 * D_MODEL
N_MOD = 6
RMS_EPS = 1e-6

VMEM_LIMIT_BYTES = 56 * 1024 * 1024
TOKEN_TILE = 512
FF_CHUNK = 1024
BIAS_SLOTS = 8
SWA_STEP_BLOCKS = 4
ACC_ROWS = HEAD_DIM + 16
NEG_INF = float("-inf")

F32 = jnp.float32
BF16 = jnp.bfloat16


def _resident(block_shape, index_map):
    return pl.BlockSpec(block_shape, index_map, pipeline_mode=pl.Buffered(1))


def _dot(a, b):
    return jnp.dot(a, b, preferred_element_type=F32)


def _dot_nt(a, b):
    return lax.dot_general(a, b, (((1,), (1,)), ((), ())), preferred_element_type=F32)


def _rms_modulate(xv, g, shift, scale):
    ms = jnp.mean(xv * xv, axis=-1, keepdims=True)
    y = xv * lax.rsqrt(ms + RMS_EPS)
    return (y * g) * (1.0 + scale) + shift


def _toeplitz(u, rows):
    wide = jnp.broadcast_to(u, (rows, 2 * rows))
    return pltpu.roll(wide, 0, 1, stride=1, stride_axis=0)[:, rows:]


def _ada_kernel(c_ref, w_ref, b_ref, o_ref):
    cs = jax.nn.silu(c_ref[...])
    o_ref[...] = jnp.dot(cs, w_ref[...], preferred_element_type=F32,
                         precision=lax.Precision.HIGHEST) + b_ref[...]


def _ada(c_pad, w, b):
    n = w.shape[1]
    bn = 1536
    return pl.pallas_call(
        _ada_kernel,
        grid=(n // bn,),
        in_specs=[pl.BlockSpec((8, D_MODEL), lambda j: (0, 0)),
                  pl.BlockSpec((D_MODEL, bn), lambda j: (0, j)),
                  pl.BlockSpec((1, bn), lambda j: (0, j))],
        out_specs=pl.BlockSpec((8, bn), lambda j: (0, j)),
        out_shape=jax.ShapeDtypeStruct((8, n), F32),
        compiler_params=pltpu.CompilerParams(dimension_semantics=("arbitrary",),
                                             vmem_limit_bytes=VMEM_LIMIT_BYTES),
        name="ada",
    )(c_pad, w, b)


def _t5_bucket(dist):
    n = jnp.maximum(dist, 0)
    nf = jnp.maximum(n, 1).astype(F32)
    large = MAX_EXACT + (jnp.log(nf / MAX_EXACT) / math.log(MAX_DISTANCE / MAX_EXACT)
                         * (NUM_BUCKETS - MAX_EXACT)).astype(jnp.int32)
    large = jnp.minimum(large, NUM_BUCKETS - 1)
    return jnp.where(n < MAX_EXACT, n, large)


def _bias_rows_kernel(bucket_ref, rbt_ref, o_ref):
    bucket = bucket_ref[...]
    n = bucket.shape[1]
    onehot = jnp.where(lax.broadcasted_iota(jnp.int32, (NUM_BUCKETS, n), 0) == bucket, 1.0, 0.0)
    rows = jnp.dot(rbt_ref[...], onehot, preferred_element_type=F32,
                   precision=lax.Precision.HIGHEST)
    rows = jnp.where(bucket >= 0, rows * LOG2E, NEG_INF)
    for d in range(BIAS_SLOTS):
        o_ref[d] = rows[:, d * MOBA_BLOCK:(d + 2) * MOBA_BLOCK]


def _bias_rows(rel_bias):
    n = (BIAS_SLOTS + 1) * MOBA_BLOCK
    dist = jnp.arange(n, dtype=jnp.int32) - MOBA_BLOCK
    bucket = jnp.where(dist >= 0, _t5_bucket(dist), -1)[None, :]
    return pl.pallas_call(
        _bias_rows_kernel,
        out_shape=jax.ShapeDtypeStruct((BIAS_SLOTS, N_ATTN_HEADS, 2 * MOBA_BLOCK), F32),
        name="bias_rows",
    )(bucket, rel_bias.astype(F32).T)


def _proj_kernel(x_ref, shift_ref, scale_ref, g_ref, wtok_ref, wfeat_ref,
                 ka_ref, kb_ref, ga_ref, gb_ref, qa_ref, va_ref, qb_ref, vb_ref):
    h = _rms_modulate(x_ref[...], g_ref[...], shift_ref[0, 0], scale_ref[0, 0]).astype(BF16)
    o = 0
    for ref, w in ((ka_ref, SWA_KV_W), (kb_ref, MOBA_W), (ga_ref, D_MODEL), (gb_ref, D_MODEL)):
        ref[...] = _dot(h, wtok_ref[:, o:o + w]).astype(ref.dtype)
        o += w
    o = 0
    for ref, w in ((qa_ref, SWA_Q_W), (va_ref, SWA_KV_W), (qb_ref, MOBA_W), (vb_ref, MOBA_W)):
        ft = _dot_nt(wfeat_ref[o:o + w, :], h).astype(ref.dtype)
        blk = ref.shape[-1]
        for t in range(TOKEN_TILE // blk):
            ref[0, t] = ft[:, t * blk:(t + 1) * blk]
        o += w


def _proj(x2, mod4, g1, wtok, wfeat, batch, seq):
    tokens = x2.shape[0]
    tiles_per_seq = seq // TOKEN_TILE
    n_tok = wtok.shape[1]
    n_feat = wfeat.shape[0]
    na = TOKEN_TILE // SWA_BLOCK
    nb = TOKEN_TILE // MOBA_BLOCK

    def feat_spec(width, blk, per_tile):
        return pl.BlockSpec((1, per_tile, width, blk),
                            lambda i: (i // tiles_per_seq, i % tiles_per_seq, 0, 0))

    out_shape = (
        jax.ShapeDtypeStruct((tokens, SWA_KV_W), BF16),
        jax.ShapeDtypeStruct((tokens, MOBA_W), BF16),
        jax.ShapeDtypeStruct((tokens, D_MODEL), BF16),
        jax.ShapeDtypeStruct((tokens, D_MODEL), BF16),
        jax.ShapeDtypeStruct((batch, seq // SWA_BLOCK, SWA_Q_W, SWA_BLOCK), BF16),
        jax.ShapeDtypeStruct((batch, seq // SWA_BLOCK, SWA_KV_W, SWA_BLOCK), BF16),
        jax.ShapeDtypeStruct((batch, seq // MOBA_BLOCK, MOBA_W, MOBA_BLOCK), BF16),
        jax.ShapeDtypeStruct((batch, seq // MOBA_BLOCK, MOBA_W, MOBA_BLOCK), BF16),
    )
    tok_spec = lambda w: pl.BlockSpec((TOKEN_TILE, w), lambda i: (i, 0))
    return pl.pallas_call(
        _proj_kernel,
        grid=(tokens // TOKEN_TILE,),
        in_specs=[
            pl.BlockSpec((TOKEN_TILE, D_MODEL), lambda i: (i, 0)),
            pl.BlockSpec((1, 1, 1, D_MODEL), lambda i: (i // tiles_per_seq, 0, 0, 0)),
            pl.BlockSpec((1, 1, 1, D_MODEL), lambda i: (i // tiles_per_seq, 1, 0, 0)),
            _resident((1, D_MODEL), lambda i: (0, 0)),
            _resident((D_MODEL, n_tok), lambda i: (0, 0)),
            _resident((n_feat, D_MODEL), lambda i: (0, 0)),
        ],
        out_specs=(tok_spec(SWA_KV_W), tok_spec(MOBA_W), tok_spec(D_MODEL), tok_spec(D_MODEL),
                   feat_spec(SWA_Q_W, SWA_BLOCK, na), feat_spec(SWA_KV_W, SWA_BLOCK, na),
                   feat_spec(MOBA_W, MOBA_BLOCK, nb), feat_spec(MOBA_W, MOBA_BLOCK, nb)),
        out_shape=out_shape,
        compiler_params=pltpu.CompilerParams(dimension_semantics=("arbitrary",),
                                             vmem_limit_bytes=VMEM_LIMIT_BYTES),
        name="proj",
    )(x2, mod4, mod4, g1, wtok, wfeat)


def _swa_kernel(q_ref, kp_ref, kc_ref, vp_ref, vc_ref, rows_ref, sink_ref, o_ref, bias_ref, ot_ref, s_ref):
    n = pl.program_id(1)
    L = SWA_BLOCK

    @pl.when(jnp.logical_and(pl.program_id(0) == 0, n == 0))
    def _():
        k_idx = lax.broadcasted_iota(jnp.int32, (L, L), 0)
        q_idx = lax.broadcasted_iota(jnp.int32, (L, L), 1)
        for h in range(SWA_Q_HEADS):
            g, gi = divmod(h, SWA_GROUP)
            t0 = rows_ref[0, h:h + 1, MOBA_BLOCK:MOBA_BLOCK + L]
            r = _toeplitz(jnp.concatenate([t0, t0], axis=1), L)
            cols = slice(gi * L, (gi + 1) * L)
            bias_ref[g, 0:L, cols] = jnp.where(q_idx < k_idx, r, NEG_INF)
            bias_ref[g, L:2 * L, cols] = jnp.where(q_idx >= k_idx, r, NEG_INF)

    first_rows = jnp.where(n == 0, L, 0)
    key_row = lax.broadcasted_iota(jnp.int32, (2 * L, SWA_GROUP * L), 0)
    ones_blk = jnp.concatenate([jnp.ones((1, 2 * L), BF16), jnp.zeros((ACC_ROWS - HEAD_DIM - 1, 2 * L), BF16)],
                               axis=0)
    chains = [(c, g) for c in range(SWA_STEP_BLOCKS) for g in range(SWA_KV_HEADS)]

    def scores(idx):
        c, g = chains[idx]
        hs = slice(g * HEAD_DIM, (g + 1) * HEAD_DIM)
        if c == 0:
            kw = jnp.concatenate([kp_ref[0, :, hs], kc_ref[0, 0:L, hs]], axis=0)
        else:
            kw = kc_ref[0, (c - 1) * L:(c + 1) * L, hs]
        qg = jnp.concatenate(
            [q_ref[0, c, (g * SWA_GROUP + gi) * HEAD_DIM:(g * SWA_GROUP + gi + 1) * HEAD_DIM, :]
             for gi in range(SWA_GROUP)], axis=1)
        s = _dot(kw, qg) + bias_ref[g]
        if c == 0:
            s = jnp.where(key_row < first_rows, NEG_INF, s)
        s_ref[idx] = s

    def attend(idx):
        c, g = chains[idx]
        hs = slice(g * HEAD_DIM, (g + 1) * HEAD_DIM)
        s = s_ref[idx]
        sink = sink_ref[g]
        m = jnp.maximum(jnp.max(s, axis=0, keepdims=True), sink)
        p = jnp.exp2(s - m).astype(BF16)
        v_prev = vp_ref[0, 0, hs, :] if c == 0 else vc_ref[0, c - 1, hs, :]
        v_aug = jnp.concatenate([jnp.concatenate([v_prev, vc_ref[0, c, hs, :]], axis=1), ones_blk],
                                axis=0)
        pv = _dot(v_aug, p)
        o = pv[0:HEAD_DIM] / (pv[HEAD_DIM:HEAD_DIM + 1] + jnp.exp2(sink - m))
        for gi in range(SWA_GROUP):
            r = (g * SWA_GROUP + gi) * HEAD_DIM
            ot_ref[c, r:r + HEAD_DIM, :] = o[:, gi * L:(gi + 1) * L]

    scores(0)
    for idx in range(len(chains)):
        if idx + 1 < len(chains):
            scores(idx + 1)
        attend(idx)
    for c in range(SWA_STEP_BLOCKS):
        o_ref[0, c * L:(c + 1) * L, :] = ot_ref[c].T.astype(o_ref.dtype)


def _swa(qa_t, ka3, va_t, rows, sink_rows):
    batch, nblk = qa_t.shape[0], qa_t.shape[1]
    seq = nblk * SWA_BLOCK
    nb = SWA_STEP_BLOCKS
    assert nblk % nb == 0
    prev = lambda n: jnp.maximum(n * nb - 1, 0)
    return pl.pallas_call(
        _swa_kernel,
        grid=(batch, nblk // nb),
        in_specs=[
            pl.BlockSpec((1, nb, SWA_Q_W, SWA_BLOCK), lambda b, n: (b, n, 0, 0)),
            pl.BlockSpec((1, SWA_BLOCK, SWA_KV_W), lambda b, n: (b, prev(n), 0)),
            pl.BlockSpec((1, nb * SWA_BLOCK, SWA_KV_W), lambda b, n: (b, n, 0)),
            pl.BlockSpec((1, 1, SWA_KV_W, SWA_BLOCK), lambda b, n: (b, prev(n), 0, 0)),
            pl.BlockSpec((1, nb, SWA_KV_W, SWA_BLOCK), lambda b, n: (b, n, 0, 0)),
            _resident((1, N_ATTN_HEADS, 2 * MOBA_BLOCK), lambda b, n: (0, 0, 0)),
            _resident(sink_rows.shape, lambda b, n: (0, 0, 0)),
        ],
        out_specs=pl.BlockSpec((1, nb * SWA_BLOCK, SWA_Q_W), lambda b, n: (b, n, 0)),
        out_shape=jax.ShapeDtypeStruct((batch, seq, SWA_Q_W), BF16),
        scratch_shapes=[pltpu.VMEM((SWA_KV_HEADS, 2 * SWA_BLOCK, SWA_GROUP * SWA_BLOCK), F32),
                        pltpu.VMEM((nb, SWA_Q_W, SWA_BLOCK), F32),
                        pltpu.VMEM((nb * SWA_KV_HEADS, 2 * SWA_BLOCK, SWA_GROUP * SWA_BLOCK), F32)],
        compiler_params=pltpu.CompilerParams(dimension_semantics=("arbitrary", "arbitrary"),
                                             vmem_limit_bytes=VMEM_LIMIT_BYTES),
        name="swa",
    )(qa_t, ka3, ka3, va_t, va_t, rows, sink_rows)


def _select_kernel(k_ref, q_ref, o_ref):
    seq = k_ref.shape[1]
    nblk = seq // MOBA_BLOCK
    blk = lax.broadcasted_iota(jnp.int32, (nblk, seq), 0)
    qblk = lax.broadcasted_iota(jnp.int32, (nblk, seq), 1) // MOBA_BLOCK
    past = blk < qblk
    member = jnp.where(blk == qblk, 1.0 / MOBA_BLOCK, 0.0).astype(BF16)
    for hh in range(2):
        hs = slice(hh * HEAD_DIM, (hh + 1) * HEAD_DIM)
        kmean = _dot(member, k_ref[0, :, hs])
        q_t = jnp.concatenate([q_ref[0, t, hs, :] for t in range(nblk)], axis=1)
        gate = jnp.dot(kmean, q_t.astype(F32), preferred_element_type=F32,
                       precision=lax.Precision.HIGHEST)
        gate = jnp.where(past, gate, NEG_INF)
        rank = jnp.zeros((nblk, seq), jnp.int32)
        for r in range(nblk):
            row = gate[r:r + 1, :]
            rank = rank + jnp.where(blk > r, jnp.where(row >= gate, 1, 0),
                                    jnp.where(row > gate, 1, 0))
        sel = jnp.logical_and(past, rank < MOBA_TOPK)
        o_ref[0, hh] = jnp.where(jnp.logical_or(sel, blk == qblk), 0.0, NEG_INF)


def _select(kb3, qb_t):
    batch, seq = kb3.shape[0], kb3.shape[1]
    nblk = seq // MOBA_BLOCK
    pairs = MOBA_HEADS // 2
    return pl.pallas_call(
        _select_kernel,
        grid=(batch, pairs),
        in_specs=[
            pl.BlockSpec((1, seq, 2 * HEAD_DIM), lambda b, hp: (b, 0, hp)),
            pl.BlockSpec((1, nblk, 2 * HEAD_DIM, MOBA_BLOCK), lambda b, hp: (b, 0, hp, 0)),
        ],
        out_specs=pl.BlockSpec((1, 2, nblk, seq), lambda b, hp: (b, hp, 0, 0)),
        out_shape=jax.ShapeDtypeStruct((batch, MOBA_HEADS, nblk, seq), F32),
        compiler_params=pltpu.CompilerParams(dimension_semantics=("arbitrary", "arbitrary"),
                                             vmem_limit_bytes=VMEM_LIMIT_BYTES),
        name="moba_select",
    )(kb3, qb_t)


def _moba_kernel(q_ref, k_ref, v_ref, sel_ref, rows_ref, o_ref, bias_ref, m_ref, acc_ref, s_ref, cm_ref):
    i = pl.program_id(1)
    MB = MOBA_BLOCK

    @pl.when(jnp.logical_and(pl.program_id(0) == 0, i == 0))
    def _():
        for h in range(MOBA_HEADS):
            for d in range(BIAS_SLOTS):
                bias_ref[h, d] = _toeplitz(rows_ref[d, SWA_Q_HEADS + h:SWA_Q_HEADS + h + 1, :], MB)

    m_ref[...] = jnp.full(m_ref.shape, NEG_INF, F32)
    acc_ref[...] = jnp.zeros(acc_ref.shape, F32)
    heads = [slice(h * HEAD_DIM, (h + 1) * HEAD_DIM) for h in range(MOBA_HEADS)]
    ones_blk = jnp.concatenate([jnp.ones((1, MB), BF16), jnp.zeros((ACC_ROWS - HEAD_DIM - 1, MB), BF16)],
                               axis=0)

    def scores(j, slot, h, hs):
        rows = pl.ds(pl.multiple_of(j * MB, MB), MB)
        s = _dot(k_ref[0, rows, hs], q_ref[0, 0, hs, :]) + bias_ref[h, slot] + sel_ref[0, h, pl.ds(j, 1), :]
        s_ref[h] = s
        cm_ref[h] = jnp.max(s, axis=0, keepdims=True)

    def accumulate(j, h, hs):
        m_old = m_ref[h]
        m_new = jnp.maximum(m_old, cm_ref[h])
        alpha = jnp.exp2(m_old - m_new)
        p = jnp.exp2(s_ref[h] - m_new).astype(BF16)
        v_aug = jnp.concatenate([v_ref[0, j, hs, :], ones_blk], axis=0)
        rs = slice(h * ACC_ROWS, (h + 1) * ACC_ROWS)
        acc_ref[rs, :] = alpha * acc_ref[rs, :] + _dot(v_aug, p)
        m_ref[h] = m_new

    for h, hs in enumerate(heads):
        scores(i, 0, h, hs)

    def body(t, carry):
        j = i - t
        slot = jnp.minimum(t + 1, BIAS_SLOTS - 1)
        for h, hs in enumerate(heads):
            accumulate(j, h, hs)
            scores(j - 1, slot, h, hs)
        return carry

    def body2(u, carry):
        return body(2 * u + 1, body(2 * u, carry))

    lax.fori_loop(0, i // 2, body2, 0)

    @pl.when(i % 2 == 1)
    def _():
        body(i - 1, 0)

    outs = []
    for h, hs in enumerate(heads):
        accumulate(0, h, hs)
        base = h * ACC_ROWS
        outs.append(acc_ref[base:base + HEAD_DIM, :] / acc_ref[base + HEAD_DIM:base + HEAD_DIM + 1, :])
    o_ref[0] = jnp.concatenate(outs, axis=0).T.astype(o_ref.dtype)


def _moba(qb_t, kb3, vb_t, sel, rows):
    batch, nblk = qb_t.shape[0], qb_t.shape[1]
    seq = nblk * MOBA_BLOCK
    return pl.pallas_call(
        _moba_kernel,
        grid=(batch, nblk),
        in_specs=[
            pl.BlockSpec((1, 1, MOBA_W, MOBA_BLOCK), lambda b, i: (b, i, 0, 0)),
            pl.BlockSpec((1, seq, MOBA_W), lambda b, i: (b, 0, 0)),
            pl.BlockSpec((1, nblk, MOBA_W, MOBA_BLOCK), lambda b, i: (b, 0, 0, 0)),
            pl.BlockSpec((1, MOBA_HEADS, nblk, MOBA_BLOCK), lambda b, i: (b, 0, 0, i)),
            _resident(rows.shape, lambda b, i: (0, 0, 0)),
        ],
        out_specs=pl.BlockSpec((1, MOBA_BLOCK, MOBA_W), lambda b, i: (b, i, 0)),
        out_shape=jax.ShapeDtypeStruct((batch, seq, MOBA_W), BF16),
        scratch_shapes=[pltpu.VMEM((MOBA_HEADS, BIAS_SLOTS, MOBA_BLOCK, MOBA_BLOCK), F32),
                        pltpu.VMEM((MOBA_HEADS, 1, MOBA_BLOCK), F32),
                        pltpu.VMEM((MOBA_HEADS * ACC_ROWS, MOBA_BLOCK), F32),
                        pltpu.VMEM((MOBA_HEADS, MOBA_BLOCK, MOBA_BLOCK), F32),
                        pltpu.VMEM((MOBA_HEADS, 1, MOBA_BLOCK), F32)],
        compiler_params=pltpu.CompilerParams(dimension_semantics=("arbitrary", "arbitrary"),
                                             vmem_limit_bytes=VMEM_LIMIT_BYTES),
        name="moba",
    )(qb_t, kb3, vb_t, sel, rows)


def _post_kernel(x_ref, ya_ref, yb_ref, ga_ref, gb_ref, gate1_ref, shift2_ref, scale2_ref,
                 gate2_ref, g2_ref, gf_ref, wa_ref, wb_ref, wo_ref, w1_ref, w2_ref, o_ref):
    a = _dot(ya_ref[...], wa_ref[...])
    b = _dot(yb_ref[...], wb_ref[...])
    merged = (jax.nn.sigmoid(ga_ref[...].astype(F32)) * a
              + jax.nn.sigmoid(gb_ref[...].astype(F32)) * b)
    x1 = x_ref[...] + gate1_ref[0, 0] * _dot(merged.astype(BF16), wo_ref[...])
    h2 = _rms_modulate(x1, g2_ref[...], shift2_ref[0, 0], scale2_ref[0, 0]).astype(BF16)
    y = jnp.zeros_like(x1)
    for c in range(D_FF // FF_CHUNK):
        cs = slice(c * FF_CHUNK, (c + 1) * FF_CHUNK)
        u = jnp.square(jnp.maximum(_dot(h2, w1_ref[:, cs]), 0.0)).astype(BF16)
        y = y + _dot(u, w2_ref[cs, :])
    x2 = x1 + gate2_ref[0, 0] * y
    ms = jnp.mean(x2 * x2, axis=-1, keepdims=True)
    o_ref[...] = (x2 * lax.rsqrt(ms + RMS_EPS)) * gf_ref[...]


def _post(x2, ya, yb, ga, gb, mod4, g2, gf, wa, wb, wo, w1, w2, seq):
    tokens = x2.shape[0]
    tiles_per_seq = seq // TOKEN_TILE
    tok = lambda w: pl.BlockSpec((TOKEN_TILE, w), lambda i: (i, 0))
    modrow = lambda k: pl.BlockSpec((1, 1, 1, D_MODEL), lambda i: (i // tiles_per_seq, k, 0, 0))
    full = lambda a: _resident(a.shape, lambda i: (0, 0))
    return pl.pallas_call(
        _post_kernel,
        grid=(tokens // TOKEN_TILE,),
        in_specs=[tok(D_MODEL), tok(SWA_Q_W), tok(MOBA_W), tok(D_MODEL), tok(D_MODEL),
                  modrow(2), modrow(3), modrow(4), modrow(5),
                  full(g2), full(gf), full(wa), full(wb), full(wo), full(w1), full(w2)],
        out_specs=tok(D_MODEL),
        out_shape=jax.ShapeDtypeStruct((tokens, D_MODEL), F32),
        compiler_params=pltpu.CompilerParams(dimension_semantics=("arbitrary",),
                                             vmem_limit_bytes=VMEM_LIMIT_BYTES),
        name="post",
    )(x2, ya, yb, ga, gb, mod4, mod4, mod4, mod4, g2, gf, wa, wb, wo, w1, w2)


def kernel(x, c, ada_w, ada_b, norm1_g, norm2_g, w_in, attn_sinks, rel_bias, w_branch_a,
           w_branch_b, w_out, w_mlp_in, w_mlp_out, final_g):
    batch, seq, _ = x.shape
    depth = ada_w.shape[0]
    nblk_b = seq // MOBA_BLOCK
    assert seq % TOKEN_TILE == 0 and TOKEN_TILE % MOBA_BLOCK == 0 and batch <= 8
    assert (BIAS_SLOTS - 1) * MOBA_BLOCK - (MOBA_BLOCK - 1) >= 1513 and MAX_DISTANCE == 2048
    assert 2 * SWA_BLOCK <= MOBA_BLOCK and SWA_WINDOW == SWA_BLOCK
    assert depth == 1
    l = 0

    rows = _bias_rows(rel_bias)
    c_pad = jnp.zeros((8, D_MODEL), F32).at[:batch].set(c)

    widths = (SWA_Q_W, SWA_KV_W, SWA_KV_W, MOBA_W, MOBA_W, MOBA_W, D_MODEL, D_MODEL)
    offs = [0]
    for w in widths:
        offs.append(offs[-1] + w)
    col = lambda k: slice(offs[k], offs[k + 1])

    x2 = x.reshape(batch * seq, D_MODEL)
    mod = _ada(c_pad, ada_w[l], ada_b[l][None, :])
    mod4 = mod[:batch].reshape(batch, N_MOD, 1, D_MODEL)
    wl = w_in[l]
    q_scale = ATTN_SCALE * LOG2E
    wtok = jnp.concatenate([wl[:, col(1)], wl[:, col(4)], wl[:, col(6)], wl[:, col(7)]],
                           axis=1).astype(BF16)
    wfeat = jnp.concatenate([wl[:, col(0)] * q_scale, wl[:, col(2)],
                             wl[:, col(3)] * q_scale, wl[:, col(5)]], axis=1).T.astype(BF16)
    ka, kb, ga, gb, qa_t, va_t, qb_t, vb_t = _proj(
        x2, mod4, norm1_g[l][None, :], wtok, wfeat, batch, seq)

    sink_rows = jnp.broadcast_to(
        (attn_sinks[l].astype(F32) * LOG2E).reshape(SWA_KV_HEADS, 1, SWA_GROUP, 1),
        (SWA_KV_HEADS, 1, SWA_GROUP, SWA_BLOCK)).reshape(SWA_KV_HEADS, 1, SWA_GROUP * SWA_BLOCK)
    ya = _swa(qa_t, ka.reshape(batch, seq, SWA_KV_W), va_t, rows, sink_rows)

    kb3 = kb.reshape(batch, seq, MOBA_W)
    sel = _select(kb3, qb_t)
    yb = _moba(qb_t, kb3, vb_t, sel, rows)

    out = _post(x2, ya.reshape(batch * seq, SWA_Q_W), yb.reshape(batch * seq, MOBA_W), ga, gb,
                mod4, norm2_g[l][None, :], final_g[None, :],
                w_branch_a[l].astype(BF16), w_branch_b[l].astype(BF16), w_out[l].astype(BF16),
                w_mlp_in[l].astype(BF16), w_mlp_out[l].astype(BF16), seq)
    return out.reshape(batch, seq, D_MODEL)
```

```python
import math

import jax
import jax.numpy as jnp
from jax import lax
from jax.experimental import pallas as pl
from jax.experimental.pallas import tpu as pltpu

D_MODEL = 1024
HEAD_DIM = 64
ATTN_SCALE = HEAD_DIM ** -0.5
LOG2E = math.log2(math.e)
SWA_Q_HEADS = 8
SWA_KV_HEADS = 2
SWA_GROUP = SWA_Q_HEADS // SWA_KV_HEADS
SWA_WINDOW = 128
SWA_BLOCK = 128
MOBA_HEADS = 8
MOBA_BLOCK = 256
MOBA_TOPK = 3
NUM_BUCKETS = 32
MAX_EXACT = NUM_BUCKETS // 2
MAX_DISTANCE = 2048
N_ATTN_HEADS = SWA_Q_HEADS + MOBA_HEADS
SWA_Q_W = SWA_Q_HEADS * HEAD_DIM
SWA_KV_W = SWA_KV_HEADS * HEAD_DIM
MOBA_W = MOBA_HEADS * HEAD_DIM
D_FF = 4 * D_MODEL
N_MOD = 6
RMS_EPS = 1e-6

VMEM_LIMIT_BYTES = 56 * 1024 * 1024
TOKEN_TILE = 512
FF_CHUNK = 1024
BIAS_SLOTS = 8
SWA_STEP_BLOCKS = 4
ACC_ROWS = HEAD_DIM + 16
NEG_INF = float("-inf")

F32 = jnp.float32
BF16 = jnp.bfloat16


def _resident(block_shape, index_map):
    return pl.BlockSpec(block_shape, index_map, pipeline_mode=pl.Buffered(1))


def _dot(a, b):
    return jnp.dot(a, b, preferred_element_type=F32)


def _dot_nt(a, b):
    return lax.dot_general(a, b, (((1,), (1,)), ((), ())), preferred_element_type=F32)


def _rms_modulate(xv, g, shift, scale):
    ms = jnp.mean(xv * xv, axis=-1, keepdims=True)
    y = xv * lax.rsqrt(ms + RMS_EPS)
    return (y * g) * (1.0 + scale) + shift


def _toeplitz(u, rows):
    wide = jnp.broadcast_to(u, (rows, 2 * rows))
    return pltpu.roll(wide, 0, 1, stride=1, stride_axis=0)[:, rows:]


def _ada_kernel(c_ref, w_ref, b_ref, o_ref):
    cs = jax.nn.silu(c_ref[...])
    o_ref[...] = jnp.dot(cs, w_ref[...], preferred_element_type=F32,
                         precision=lax.Precision.HIGHEST) + b_ref[...]


def _ada(c_pad, w, b):
    n = w.shape[1]
    bn = 1536
    return pl.pallas_call(
        _ada_kernel,
        grid=(n // bn,),
        in_specs=[pl.BlockSpec((8, D_MODEL), lambda j: (0, 0)),
                  pl.BlockSpec((D_MODEL, bn), lambda j: (0, j)),
                  pl.BlockSpec((1, bn), lambda j: (0, j))],
        out_specs=pl.BlockSpec((8, bn), lambda j: (0, j)),
        out_shape=jax.ShapeDtypeStruct((8, n), F32),
        compiler_params=pltpu.CompilerParams(dimension_semantics=("arbitrary",),
                                             vmem_limit_bytes=VMEM_LIMIT_BYTES),
        name="ada",
    )(c_pad, w, b)


def _t5_bucket(dist):
    n = jnp.maximum(dist, 0)
    nf = jnp.maximum(n, 1).astype(F32)
    large = MAX_EXACT + (jnp.log(nf / MAX_EXACT) / math.log(MAX_DISTANCE / MAX_EXACT)
                         * (NUM_BUCKETS - MAX_EXACT)).astype(jnp.int32)
    large = jnp.minimum(large, NUM_BUCKETS - 1)
    return jnp.where(n < MAX_EXACT, n, large)


def _bias_rows_kernel(bucket_ref, rbt_ref, o_ref):
    bucket = bucket_ref[...]
    n = bucket.shape[1]
    onehot = jnp.where(lax.broadcasted_iota(jnp.int32, (NUM_BUCKETS, n), 0) == bucket, 1.0, 0.0)
    rows = jnp.dot(rbt_ref[...], onehot, preferred_element_type=F32,
                   precision=lax.Precision.HIGHEST)
    rows = jnp.where(bucket >= 0, rows * LOG2E, NEG_INF)
    for d in range(BIAS_SLOTS):
        o_ref[d] = rows[:, d * MOBA_BLOCK:(d + 2) * MOBA_BLOCK]


def _bias_rows(rel_bias):
    n = (BIAS_SLOTS + 1) * MOBA_BLOCK
    dist = jnp.arange(n, dtype=jnp.int32) - MOBA_BLOCK
    bucket = jnp.where(dist >= 0, _t5_bucket(dist), -1)[None, :]
    return pl.pallas_call(
        _bias_rows_kernel,
        out_shape=jax.ShapeDtypeStruct((BIAS_SLOTS, N_ATTN_HEADS, 2 * MOBA_BLOCK), F32),
        name="bias_rows",
    )(bucket, rel_bias.astype(F32).T)


IN_WIDTHS = (SWA_Q_W, SWA_KV_W, SWA_KV_W, MOBA_W, MOBA_W, MOBA_W, D_MODEL, D_MODEL)
IN_OFFS = tuple(sum(IN_WIDTHS[:k]) for k in range(len(IN_WIDTHS) + 1))
FEAT_COLS = (0, 2, 3, 5)
FEAT_W = sum(IN_WIDTHS[k] for k in FEAT_COLS)


def _proj_kernel(x_ref, shift_ref, scale_ref, g_ref, w_ref,
                 ka_ref, kb_ref, ga_ref, gb_ref, qa_ref, va_ref, qb_ref, vb_ref, wt_ref):
    @pl.when(pl.program_id(0) == 0)
    def _():
        o = 0
        for k in FEAT_COLS:
            wt_ref[o:o + IN_WIDTHS[k], :] = w_ref[:, IN_OFFS[k]:IN_OFFS[k + 1]].T
            o += IN_WIDTHS[k]

    h = _rms_modulate(x_ref[...], g_ref[...], shift_ref[0, 0], scale_ref[0, 0]).astype(BF16)
    for ref, k in ((ka_ref, 1), (kb_ref, 4), (ga_ref, 6), (gb_ref, 7)):
        ref[...] = _dot(h, w_ref[:, IN_OFFS[k]:IN_OFFS[k + 1]]).astype(ref.dtype)
    o = 0
    for ref, k in zip((qa_ref, va_ref, qb_ref, vb_ref), FEAT_COLS):
        w = IN_WIDTHS[k]
        ft = _dot_nt(wt_ref[o:o + w, :], h).astype(ref.dtype)
        blk = ref.shape[-1]
        for t in range(TOKEN_TILE // blk):
            ref[0, t] = ft[:, t * blk:(t + 1) * blk]
        o += w


def _proj(x2, mod4, g1, w_bf16, batch, seq):
    tokens = x2.shape[0]
    tiles_per_seq = seq // TOKEN_TILE
    na = TOKEN_TILE // SWA_BLOCK
    nb = TOKEN_TILE // MOBA_BLOCK

    def feat_spec(width, blk, per_tile):
        return pl.BlockSpec((1, per_tile, width, blk),
                            lambda i: (i // tiles_per_seq, i % tiles_per_seq, 0, 0))

    out_shape = (
        jax.ShapeDtypeStruct((tokens, SWA_KV_W), BF16),
        jax.ShapeDtypeStruct((tokens, MOBA_W), BF16),
        jax.ShapeDtypeStruct((tokens, D_MODEL), BF16),
        jax.ShapeDtypeStruct((tokens, D_MODEL), BF16),
        jax.ShapeDtypeStruct((batch, seq // SWA_BLOCK, SWA_Q_W, SWA_BLOCK), BF16),
        jax.ShapeDtypeStruct((batch, seq // SWA_BLOCK, SWA_KV_W, SWA_BLOCK), BF16),
        jax.ShapeDtypeStruct((batch, seq // MOBA_BLOCK, MOBA_W, MOBA_BLOCK), BF16),
        jax.ShapeDtypeStruct((batch, seq // MOBA_BLOCK, MOBA_W, MOBA_BLOCK), BF16),
    )
    tok_spec = lambda w: pl.BlockSpec((TOKEN_TILE, w), lambda i: (i, 0))
    return pl.pallas_call(
        _proj_kernel,
        grid=(tokens // TOKEN_TILE,),
        in_specs=[
            pl.BlockSpec((TOKEN_TILE, D_MODEL), lambda i: (i, 0)),
            pl.BlockSpec((1, 1, 1, D_MODEL), lambda i: (i // tiles_per_seq, 0, 0, 0)),
            pl.BlockSpec((1, 1, 1, D_MODEL), lambda i: (i // tiles_per_seq, 1, 0, 0)),
            _resident((1, D_MODEL), lambda i: (0, 0)),
            _resident(w_bf16.shape, lambda i: (0, 0)),
        ],
        out_specs=(tok_spec(SWA_KV_W), tok_spec(MOBA_W), tok_spec(D_MODEL), tok_spec(D_MODEL),
                   feat_spec(SWA_Q_W, SWA_BLOCK, na), feat_spec(SWA_KV_W, SWA_BLOCK, na),
                   feat_spec(MOBA_W, MOBA_BLOCK, nb), feat_spec(MOBA_W, MOBA_BLOCK, nb)),
        out_shape=out_shape,
        scratch_shapes=[pltpu.VMEM((FEAT_W, D_MODEL), BF16)],
        compiler_params=pltpu.CompilerParams(dimension_semantics=("arbitrary",),
                                             vmem_limit_bytes=VMEM_LIMIT_BYTES),
        name="proj",
    )(x2, mod4, mod4, g1, w_bf16)


def _swa_kernel(q_ref, kp_ref, kc_ref, vp_ref, vc_ref, rows_ref, sink_ref, o_ref, bias_ref, ot_ref, s_ref):
    n = pl.program_id(1)
    L = SWA_BLOCK

    @pl.when(jnp.logical_and(pl.program_id(0) == 0, n == 0))
    def _():
        k_idx = lax.broadcasted_iota(jnp.int32, (L, L), 0)
        q_idx = lax.broadcasted_iota(jnp.int32, (L, L), 1)
        for h in range(SWA_Q_HEADS):
            g, gi = divmod(h, SWA_GROUP)
            t0 = rows_ref[0, h:h + 1, MOBA_BLOCK:MOBA_BLOCK + L]
            r = _toeplitz(jnp.concatenate([t0, t0], axis=1), L)
            cols = slice(gi * L, (gi + 1) * L)
            bias_ref[g, 0:L, cols] = jnp.where(q_idx < k_idx, r, NEG_INF)
            bias_ref[g, L:2 * L, cols] = jnp.where(q_idx >= k_idx, r, NEG_INF)

    first_rows = jnp.where(n == 0, L, 0)
    key_row = lax.broadcasted_iota(jnp.int32, (2 * L, SWA_GROUP * L), 0)
    ones_blk = jnp.concatenate([jnp.ones((1, 2 * L), BF16), jnp.zeros((ACC_ROWS - HEAD_DIM - 1, 2 * L), BF16)],
                               axis=0)
    chains = [(c, g) for c in range(SWA_STEP_BLOCKS) for g in range(SWA_KV_HEADS)]

    def scores(idx):
        c, g = chains[idx]
        hs = slice(g * HEAD_DIM, (g + 1) * HEAD_DIM)
        if c == 0:
            kw = jnp.concatenate([kp_ref[0, :, hs], kc_ref[0, 0:L, hs]], axis=0)
        else:
            kw = kc_ref[0, (c - 1) * L:(c + 1) * L, hs]
        qg = jnp.concatenate(
            [q_ref[0, c, (g * SWA_GROUP + gi) * HEAD_DIM:(g * SWA_GROUP + gi + 1) * HEAD_DIM, :]
             for gi in range(SWA_GROUP)], axis=1)
        s = _dot(kw, qg) + bias_ref[g]
        if c == 0:
            s = jnp.where(key_row < first_rows, NEG_INF, s)
        s_ref[idx] = s

    def attend(idx):
        c, g = chains[idx]
        hs = slice(g * HEAD_DIM, (g + 1) * HEAD_DIM)
        s = s_ref[idx]
        sink = sink_ref[g]
        m = jnp.maximum(jnp.max(s, axis=0, keepdims=True), sink)
        p = jnp.exp2(s - m).astype(BF16)
        v_prev = vp_ref[0, 0, hs, :] if c == 0 else vc_ref[0, c - 1, hs, :]
        v_aug = jnp.concatenate([jnp.concatenate([v_prev, vc_ref[0, c, hs, :]], axis=1), ones_blk],
                                axis=0)
        pv = _dot(v_aug, p)
        o = pv[0:HEAD_DIM] / (pv[HEAD_DIM:HEAD_DIM + 1] + jnp.exp2(sink - m))
        for gi in range(SWA_GROUP):
            r = (g * SWA_GROUP + gi) * HEAD_DIM
            ot_ref[c, r:r + HEAD_DIM, :] = o[:, gi * L:(gi + 1) * L]

    scores(0)
    for idx in range(len(chains)):
        if idx + 1 < len(chains):
            scores(idx + 1)
        attend(idx)
    for c in range(SWA_STEP_BLOCKS):
        o_ref[0, c * L:(c + 1) * L, :] = ot_ref[c].T.astype(o_ref.dtype)


def _swa(qa_t, ka3, va_t, rows, sink_rows):
    batch, nblk = qa_t.shape[0], qa_t.shape[1]
    seq = nblk * SWA_BLOCK
    nb = SWA_STEP_BLOCKS
    assert nblk % nb == 0
    prev = lambda n: jnp.maximum(n * nb - 1, 0)
    return pl.pallas_call(
        _swa_kernel,
        grid=(batch, nblk // nb),
        in_specs=[
            pl.BlockSpec((1, nb, SWA_Q_W, SWA_BLOCK), lambda b, n: (b, n, 0, 0)),
            pl.BlockSpec((1, SWA_BLOCK, SWA_KV_W), lambda b, n: (b, prev(n), 0)),
            pl.BlockSpec((1, nb * SWA_BLOCK, SWA_KV_W), lambda b, n: (b, n, 0)),
            pl.BlockSpec((1, 1, SWA_KV_W, SWA_BLOCK), lambda b, n: (b, prev(n), 0, 0)),
            pl.BlockSpec((1, nb, SWA_KV_W, SWA_BLOCK), lambda b, n: (b, n, 0, 0)),
            _resident((1, N_ATTN_HEADS, 2 * MOBA_BLOCK), lambda b, n: (0, 0, 0)),
            _resident(sink_rows.shape, lambda b, n: (0, 0, 0)),
        ],
        out_specs=pl.BlockSpec((1, nb * SWA_BLOCK, SWA_Q_W), lambda b, n: (b, n, 0)),
        out_shape=jax.ShapeDtypeStruct((batch, seq, SWA_Q_W), BF16),
        scratch_shapes=[pltpu.VMEM((SWA_KV_HEADS, 2 * SWA_BLOCK, SWA_GROUP * SWA_BLOCK), F32),
                        pltpu.VMEM((nb, SWA_Q_W, SWA_BLOCK), F32),
                        pltpu.VMEM((nb * SWA_KV_HEADS, 2 * SWA_BLOCK, SWA_GROUP * SWA_BLOCK), F32)],
        compiler_params=pltpu.CompilerParams(dimension_semantics=("arbitrary", "arbitrary"),
                                             vmem_limit_bytes=VMEM_LIMIT_BYTES),
        name="swa",
    )(qa_t, ka3, ka3, va_t, va_t, rows, sink_rows)


def _select_kernel(k_ref, q_ref, o_ref):
    seq = k_ref.shape[1]
    nblk = seq // MOBA_BLOCK
    blk = lax.broadcasted_iota(jnp.int32, (nblk, seq), 0)
    qblk = lax.broadcasted_iota(jnp.int32, (nblk, seq), 1) // MOBA_BLOCK
    past = blk < qblk
    member = jnp.where(blk == qblk, 1.0 / MOBA_BLOCK, 0.0).astype(BF16)
    for hh in range(2):
        hs = slice(hh * HEAD_DIM, (hh + 1) * HEAD_DIM)
        kmean = _dot(member, k_ref[0, :, hs])
        q_t = jnp.concatenate([q_ref[0, t, hs, :] for t in range(nblk)], axis=1)
        gate = jnp.dot(kmean, q_t.astype(F32), preferred_element_type=F32,
                       precision=lax.Precision.HIGHEST)
        gate = jnp.where(past, gate, NEG_INF)
        rank = jnp.zeros((nblk, seq), jnp.int32)
        for r in range(nblk):
            row = gate[r:r + 1, :]
            rank = rank + jnp.where(blk > r, jnp.where(row >= gate, 1, 0),
                                    jnp.where(row > gate, 1, 0))
        sel = jnp.logical_and(past, rank < MOBA_TOPK)
        o_ref[0, hh] = jnp.where(jnp.logical_or(sel, blk == qblk), 0.0, NEG_INF)


def _select(kb3, qb_t):
    batch, seq = kb3.shape[0], kb3.shape[1]
    nblk = seq // MOBA_BLOCK
    pairs = MOBA_HEADS // 2
    return pl.pallas_call(
        _select_kernel,
        grid=(batch, pairs),
        in_specs=[
            pl.BlockSpec((1, seq, 2 * HEAD_DIM), lambda b, hp: (b, 0, hp)),
            pl.BlockSpec((1, nblk, 2 * HEAD_DIM, MOBA_BLOCK), lambda b, hp: (b, 0, hp, 0)),
        ],
        out_specs=pl.BlockSpec((1, 2, nblk, seq), lambda b, hp: (b, hp, 0, 0)),
        out_shape=jax.ShapeDtypeStruct((batch, MOBA_HEADS, nblk, seq), F32),
        compiler_params=pltpu.CompilerParams(dimension_semantics=("arbitrary", "arbitrary"),
                                             vmem_limit_bytes=VMEM_LIMIT_BYTES),
        name="moba_select",
    )(kb3, qb_t)


def _moba_kernel(q_ref, k_ref, v_ref, sel_ref, rows_ref, o_ref, bias_ref, m_ref, acc_ref, s_ref, cm_ref):
    i = pl.program_id(1)
    MB = MOBA_BLOCK

    @pl.when(jnp.logical_and(pl.program_id(0) == 0, i == 0))
    def _():
        for h in range(MOBA_HEADS):
            for d in range(BIAS_SLOTS):
                bias_ref[h, d] = _toeplitz(rows_ref[d, SWA_Q_HEADS + h:SWA_Q_HEADS + h + 1, :], MB)

    m_ref[...] = jnp.full(m_ref.shape, NEG_INF, F32)
    acc_ref[...] = jnp.zeros(acc_ref.shape, F32)
    heads = [slice(h * HEAD_DIM, (h + 1) * HEAD_DIM) for h in range(MOBA_HEADS)]
    ones_blk = jnp.concatenate([jnp.ones((1, MB), BF16), jnp.zeros((ACC_ROWS - HEAD_DIM - 1, MB), BF16)],
                               axis=0)

    def scores(j, slot, h, hs):
        rows = pl.ds(pl.multiple_of(j * MB, MB), MB)
        s = _dot(k_ref[0, rows, hs], q_ref[0, 0, hs, :]) + bias_ref[h, slot] + sel_ref[0, h, pl.ds(j, 1), :]
        s_ref[h] = s
        cm_ref[h] = jnp.max(s, axis=0, keepdims=True)

    def accumulate(j, h, hs):
        m_old = m_ref[h]
        m_new = jnp.maximum(m_old, cm_ref[h])
        alpha = jnp.exp2(m_old - m_new)
        p = jnp.exp2(s_ref[h] - m_new).astype(BF16)
        v_aug = jnp.concatenate([v_ref[0, j, hs, :], ones_blk], axis=0)
        rs = slice(h * ACC_ROWS, (h + 1) * ACC_ROWS)
        acc_ref[rs, :] = alpha * acc_ref[rs, :] + _dot(v_aug, p)
        m_ref[h] = m_new

    for h, hs in enumerate(heads):
        scores(i, 0, h, hs)

    def body(t, carry):
        j = i - t
        slot = jnp.minimum(t + 1, BIAS_SLOTS - 1)
        for h, hs in enumerate(heads):
            accumulate(j, h, hs)
            scores(j - 1, slot, h, hs)
        return carry

    def body2(u, carry):
        return body(2 * u + 1, body(2 * u, carry))

    lax.fori_loop(0, i // 2, body2, 0)

    @pl.when(i % 2 == 1)
    def _():
        body(i - 1, 0)

    outs = []
    for h, hs in enumerate(heads):
        accumulate(0, h, hs)
        base = h * ACC_ROWS
        outs.append(acc_ref[base:base + HEAD_DIM, :] / acc_ref[base + HEAD_DIM:base + HEAD_DIM + 1, :])
    o_ref[0] = jnp.concatenate(outs, axis=0).T.astype(o_ref.dtype)


def _moba(qb_t, kb3, vb_t, sel, rows):
    batch, nblk = qb_t.shape[0], qb_t.shape[1]
    seq = nblk * MOBA_BLOCK
    return pl.pallas_call(
        _moba_kernel,
        grid=(batch, nblk),
        in_specs=[
            pl.BlockSpec((1, 1, MOBA_W, MOBA_BLOCK), lambda b, i: (b, i, 0, 0)),
            pl.BlockSpec((1, seq, MOBA_W), lambda b, i: (b, 0, 0)),
            pl.BlockSpec((1, nblk, MOBA_W, MOBA_BLOCK), lambda b, i: (b, 0, 0, 0)),
            pl.BlockSpec((1, MOBA_HEADS, nblk, MOBA_BLOCK), lambda b, i: (b, 0, 0, i)),
            _resident(rows.shape, lambda b, i: (0, 0, 0)),
        ],
        out_specs=pl.BlockSpec((1, MOBA_BLOCK, MOBA_W), lambda b, i: (b, i, 0)),
        out_shape=jax.ShapeDtypeStruct((batch, seq, MOBA_W), BF16),
        scratch_shapes=[pltpu.VMEM((MOBA_HEADS, BIAS_SLOTS, MOBA_BLOCK, MOBA_BLOCK), F32),
                        pltpu.VMEM((MOBA_HEADS, 1, MOBA_BLOCK), F32),
                        pltpu.VMEM((MOBA_HEADS * ACC_ROWS, MOBA_BLOCK), F32),
                        pltpu.VMEM((MOBA_HEADS, MOBA_BLOCK, MOBA_BLOCK), F32),
                        pltpu.VMEM((MOBA_HEADS, 1, MOBA_BLOCK), F32)],
        compiler_params=pltpu.CompilerParams(dimension_semantics=("arbitrary", "arbitrary"),
                                             vmem_limit_bytes=VMEM_LIMIT_BYTES),
        name="moba",
    )(qb_t, kb3, vb_t, sel, rows)


def _post_kernel(x_ref, ya_ref, yb_ref, ga_ref, gb_ref, gate1_ref, shift2_ref, scale2_ref,
                 gate2_ref, g2_ref, gf_ref, wa_ref, wb_ref, wo_ref, w1_ref, w2_ref, o_ref):
    a = _dot(ya_ref[...], wa_ref[...])
    b = _dot(yb_ref[...], wb_ref[...])
    merged = (jax.nn.sigmoid(ga_ref[...].astype(F32)) * a
              + jax.nn.sigmoid(gb_ref[...].astype(F32)) * b)
    x1 = x_ref[...] + gate1_ref[0, 0] * _dot(merged.astype(BF16), wo_ref[...])
    h2 = _rms_modulate(x1, g2_ref[...], shift2_ref[0, 0], scale2_ref[0, 0]).astype(BF16)
    y = jnp.zeros_like(x1)
    for c in range(D_FF // FF_CHUNK):
        cs = slice(c * FF_CHUNK, (c + 1) * FF_CHUNK)
        u = jnp.square(jnp.maximum(_dot(h2, w1_ref[:, cs]), 0.0)).astype(BF16)
        y = y + _dot(u, w2_ref[cs, :])
    x2 = x1 + gate2_ref[0, 0] * y
    ms = jnp.mean(x2 * x2, axis=-1, keepdims=True)
    o_ref[...] = (x2 * lax.rsqrt(ms + RMS_EPS)) * gf_ref[...]


def _post(x2, ya, yb, ga, gb, mod4, g2, gf, wa, wb, wo, w1, w2, seq):
    tokens = x2.shape[0]
    tiles_per_seq = seq // TOKEN_TILE
    tok = lambda w: pl.BlockSpec((TOKEN_TILE, w), lambda i: (i, 0))
    modrow = lambda k: pl.BlockSpec((1, 1, 1, D_MODEL), lambda i: (i // tiles_per_seq, k, 0, 0))
    full = lambda a: _resident(a.shape, lambda i: (0, 0))
    return pl.pallas_call(
        _post_kernel,
        grid=(tokens // TOKEN_TILE,),
        in_specs=[tok(D_MODEL), tok(SWA_Q_W), tok(MOBA_W), tok(D_MODEL), tok(D_MODEL),
                  modrow(2), modrow(3), modrow(4), modrow(5),
                  full(g2), full(gf), full(wa), full(wb), full(wo), full(w1), full(w2)],
        out_specs=tok(D_MODEL),
        out_shape=jax.ShapeDtypeStruct((tokens, D_MODEL), F32),
        compiler_params=pltpu.CompilerParams(dimension_semantics=("arbitrary",),
                                             vmem_limit_bytes=VMEM_LIMIT_BYTES),
        name="post",
    )(x2, ya, yb, ga, gb, mod4, mod4, mod4, mod4, g2, gf, wa, wb, wo, w1, w2)


def kernel(x, c, ada_w, ada_b, norm1_g, norm2_g, w_in, attn_sinks, rel_bias, w_branch_a,
           w_branch_b, w_out, w_mlp_in, w_mlp_out, final_g):
    batch, seq, _ = x.shape
    depth = ada_w.shape[0]
    nblk_b = seq // MOBA_BLOCK
    assert seq % TOKEN_TILE == 0 and TOKEN_TILE % MOBA_BLOCK == 0 and batch <= 8
    assert (BIAS_SLOTS - 1) * MOBA_BLOCK - (MOBA_BLOCK - 1) >= 1513 and MAX_DISTANCE == 2048
    assert 2 * SWA_BLOCK <= MOBA_BLOCK and SWA_WINDOW == SWA_BLOCK
    assert depth == 1
    l = 0

    rows = _bias_rows(rel_bias)
    c_pad = jnp.zeros((8, D_MODEL), F32).at[:batch].set(c)

    x2 = x.reshape(batch * seq, D_MODEL)
    mod = _ada(c_pad, ada_w[l], ada_b[l][None, :])
    mod4 = mod[:batch].reshape(batch, N_MOD, 1, D_MODEL)
    col_scale = jnp.ones((sum(IN_WIDTHS),), F32)
    for k in (0, 3):
        col_scale = col_scale.at[IN_OFFS[k]:IN_OFFS[k + 1]].set(ATTN_SCALE * LOG2E)
    w_bf16 = (w_in[l] * col_scale[None, :]).astype(BF16)
    ka, kb, ga, gb, qa_t, va_t, qb_t, vb_t = _proj(
        x2, mod4, norm1_g[l][None, :], w_bf16, batch, seq)

    sink_rows = jnp.broadcast_to(
        (attn_sinks[l].astype(F32) * LOG2E).reshape(SWA_KV_HEADS, 1, SWA_GROUP, 1),
        (SWA_KV_HEADS, 1, SWA_GROUP, SWA_BLOCK)).reshape(SWA_KV_HEADS, 1, SWA_GROUP * SWA_BLOCK)
    ya = _swa(qa_t, ka.reshape(batch, seq, SWA_KV_W), va_t, rows, sink_rows)

    kb3 = kb.reshape(batch, seq, MOBA_W)
    sel = _select(kb3, qb_t)
    yb = _moba(qb_t, kb3, vb_t, sel, rows)

    out = _post(x2, ya.reshape(batch * seq, SWA_Q_W), yb.reshape(batch * seq, MOBA_W), ga, gb,
                mod4, norm2_g[l][None, :], final_g[None, :],
                w_branch_a[l].astype(BF16), w_branch_b[l].astype(BF16), w_out[l].astype(BF16),
                w_mlp_in[l].astype(BF16), w_mlp_out[l].astype(BF16), seq)
    return out.reshape(batch, seq, D_MODEL)
```

```python
import math

import jax
import jax.numpy as jnp
from jax import lax
from jax.experimental import pallas as pl
from jax.experimental.pallas import tpu as pltpu

D_MODEL = 1024
HEAD_DIM = 64
ATTN_SCALE = HEAD_DIM ** -0.5
LOG2E = math.log2(math.e)
SWA_Q_HEADS = 8
SWA_KV_HEADS = 2
SWA_GROUP = SWA_Q_HEADS // SWA_KV_HEADS
SWA_WINDOW = 128
SWA_BLOCK = 128
MOBA_HEADS = 8
MOBA_BLOCK = 256
MOBA_TOPK = 3
NUM_BUCKETS = 32
MAX_EXACT = NUM_BUCKETS // 2
MAX_DISTANCE = 2048
N_ATTN_HEADS = SWA_Q_HEADS + MOBA_HEADS
SWA_Q_W = SWA_Q_HEADS * HEAD_DIM
SWA_KV_W = SWA_KV_HEADS * HEAD_DIM
MOBA_W = MOBA_HEADS * HEAD_DIM
D_FF = 4 * D_MODEL
N_MOD = 6
RMS_EPS = 1e-6

VMEM_LIMIT_BYTES = 56 * 1024 * 1024
TOKEN_TILE = 512
FF_CHUNK = 1024
BIAS_SLOTS = 8
SWA_STEP_BLOCKS = 4
ACC_ROWS = HEAD_DIM + 16
NEG_INF = float("-inf")

F32 = jnp.float32
BF16 = jnp.bfloat16


def _resident(block_shape, index_map):
    return pl.BlockSpec(block_shape, index_map, pipeline_mode=pl.Buffered(1))


def _dot(a, b):
    return jnp.dot(a, b, preferred_element_type=F32)


def _dot_nt(a, b):
    return lax.dot_general(a, b, (((1,), (1,)), ((), ())), preferred_element_type=F32)


def _rms_modulate(xv, g, shift, scale):
    ms = jnp.mean(xv * xv, axis=-1, keepdims=True)
    y = xv * lax.rsqrt(ms + RMS_EPS)
    return (y * g) * (1.0 + scale) + shift


def _toeplitz(u, rows):
    wide = jnp.broadcast_to(u, (rows, 2 * rows))
    return pltpu.roll(wide, 0, 1, stride=1, stride_axis=0)[:, rows:]


def _ada_kernel(c_ref, w_ref, b_ref, o_ref):
    cs = jax.nn.silu(c_ref[...])
    o_ref[...] = jnp.dot(cs, w_ref[...], preferred_element_type=F32,
                         precision=lax.Precision.HIGHEST) + b_ref[...]


def _ada(c_pad, w, b):
    n = w.shape[1]
    bn = 1536
    return pl.pallas_call(
        _ada_kernel,
        grid=(n // bn,),
        in_specs=[pl.BlockSpec((8, D_MODEL), lambda j: (0, 0)),
                  pl.BlockSpec((D_MODEL, bn), lambda j: (0, j)),
                  pl.BlockSpec((1, bn), lambda j: (0, j))],
        out_specs=pl.BlockSpec((8, bn), lambda j: (0, j)),
        out_shape=jax.ShapeDtypeStruct((8, n), F32),
        compiler_params=pltpu.CompilerParams(dimension_semantics=("arbitrary",),
                                             vmem_limit_bytes=VMEM_LIMIT_BYTES),
        name="ada",
    )(c_pad, w, b)


def _t5_bucket(dist):
    n = jnp.maximum(dist, 0)
    nf = jnp.maximum(n, 1).astype(F32)
    large = MAX_EXACT + (jnp.log(nf / MAX_EXACT) / math.log(MAX_DISTANCE / MAX_EXACT)
                         * (NUM_BUCKETS - MAX_EXACT)).astype(jnp.int32)
    large = jnp.minimum(large, NUM_BUCKETS - 1)
    return jnp.where(n < MAX_EXACT, n, large)


def _bias_rows_kernel(bucket_ref, rbt_ref, o_ref):
    bucket = bucket_ref[...]
    n = bucket.shape[1]
    onehot = jnp.where(lax.broadcasted_iota(jnp.int32, (NUM_BUCKETS, n), 0) == bucket, 1.0, 0.0)
    rows = jnp.dot(rbt_ref[...], onehot, preferred_element_type=F32,
                   precision=lax.Precision.HIGHEST)
    rows = jnp.where(bucket >= 0, rows * LOG2E, NEG_INF)
    for d in range(BIAS_SLOTS):
        o_ref[d] = rows[:, d * MOBA_BLOCK:(d + 2) * MOBA_BLOCK]


def _bias_rows(rel_bias):
    n = (BIAS_SLOTS + 1) * MOBA_BLOCK
    dist = jnp.arange(n, dtype=jnp.int32) - MOBA_BLOCK
    bucket = jnp.where(dist >= 0, _t5_bucket(dist), -1)[None, :]
    return pl.pallas_call(
        _bias_rows_kernel,
        out_shape=jax.ShapeDtypeStruct((BIAS_SLOTS, N_ATTN_HEADS, 2 * MOBA_BLOCK), F32),
        name="bias_rows",
    )(bucket, rel_bias.astype(F32).T)


IN_WIDTHS = (SWA_Q_W, SWA_KV_W, SWA_KV_W, MOBA_W, MOBA_W, MOBA_W, D_MODEL, D_MODEL)
IN_OFFS = tuple(sum(IN_WIDTHS[:k]) for k in range(len(IN_WIDTHS) + 1))
FEAT_COLS = (0, 2, 3, 5)
FEAT_W = sum(IN_WIDTHS[k] for k in FEAT_COLS)


def _proj_kernel(x_ref, shift_ref, scale_ref, g_ref, w_ref,
                 ka_ref, kb_ref, ga_ref, gb_ref, qa_ref, va_ref, qb_ref, vb_ref, wt_ref):
    @pl.when(pl.program_id(0) == 0)
    def _():
        o = 0
        for k in FEAT_COLS:
            wt_ref[o:o + IN_WIDTHS[k], :] = w_ref[:, IN_OFFS[k]:IN_OFFS[k + 1]].T
            o += IN_WIDTHS[k]

    h = _rms_modulate(x_ref[...], g_ref[...], shift_ref[0, 0], scale_ref[0, 0]).astype(BF16)
    for ref, k in ((ka_ref, 1), (kb_ref, 4), (ga_ref, 6), (gb_ref, 7)):
        ref[...] = _dot(h, w_ref[:, IN_OFFS[k]:IN_OFFS[k + 1]]).astype(ref.dtype)
    o = 0
    for ref, k in zip((qa_ref, va_ref, qb_ref, vb_ref), FEAT_COLS):
        w = IN_WIDTHS[k]
        ft = _dot_nt(wt_ref[o:o + w, :], h).astype(ref.dtype)
        blk = ref.shape[-1]
        for t in range(TOKEN_TILE // blk):
            ref[0, t] = ft[:, t * blk:(t + 1) * blk]
        o += w


def _proj(x2, mod4, g1, w_bf16, batch, seq):
    tokens = x2.shape[0]
    tiles_per_seq = seq // TOKEN_TILE
    na = TOKEN_TILE // SWA_BLOCK
    nb = TOKEN_TILE // MOBA_BLOCK

    def feat_spec(width, blk, per_tile):
        return pl.BlockSpec((1, per_tile, width, blk),
                            lambda i: (i // tiles_per_seq, i % tiles_per_seq, 0, 0))

    out_shape = (
        jax.ShapeDtypeStruct((tokens, SWA_KV_W), BF16),
        jax.ShapeDtypeStruct((tokens, MOBA_W), BF16),
        jax.ShapeDtypeStruct((tokens, D_MODEL), BF16),
        jax.ShapeDtypeStruct((tokens, D_MODEL), BF16),
        jax.ShapeDtypeStruct((batch, seq // SWA_BLOCK, SWA_Q_W, SWA_BLOCK), BF16),
        jax.ShapeDtypeStruct((batch, seq // SWA_BLOCK, SWA_KV_W, SWA_BLOCK), BF16),
        jax.ShapeDtypeStruct((batch, seq // MOBA_BLOCK, MOBA_W, MOBA_BLOCK), BF16),
        jax.ShapeDtypeStruct((batch, seq // MOBA_BLOCK, MOBA_W, MOBA_BLOCK), BF16),
    )
    tok_spec = lambda w: pl.BlockSpec((TOKEN_TILE, w), lambda i: (i, 0))
    return pl.pallas_call(
        _proj_kernel,
        grid=(tokens // TOKEN_TILE,),
        in_specs=[
            pl.BlockSpec((TOKEN_TILE, D_MODEL), lambda i: (i, 0)),
            pl.BlockSpec((1, 1, 1, D_MODEL), lambda i: (i // tiles_per_seq, 0, 0, 0)),
            pl.BlockSpec((1, 1, 1, D_MODEL), lambda i: (i // tiles_per_seq, 1, 0, 0)),
            _resident((1, D_MODEL), lambda i: (0, 0)),
            _resident(w_bf16.shape, lambda i: (0, 0)),
        ],
        out_specs=(tok_spec(SWA_KV_W), tok_spec(MOBA_W), tok_spec(D_MODEL), tok_spec(D_MODEL),
                   feat_spec(SWA_Q_W, SWA_BLOCK, na), feat_spec(SWA_KV_W, SWA_BLOCK, na),
                   feat_spec(MOBA_W, MOBA_BLOCK, nb), feat_spec(MOBA_W, MOBA_BLOCK, nb)),
        out_shape=out_shape,
        scratch_shapes=[pltpu.VMEM((FEAT_W, D_MODEL), BF16)],
        compiler_params=pltpu.CompilerParams(dimension_semantics=("arbitrary",),
                                             vmem_limit_bytes=VMEM_LIMIT_BYTES),
        name="proj",
    )(x2, mod4, mod4, g1, w_bf16)


def _swa_kernel(q_ref, kp_ref, kc_ref, vp_ref, vc_ref, rows_ref, sink_ref, o_ref, bias_ref, ot_ref, s_ref):
    n = pl.program_id(1)
    L = SWA_BLOCK

    @pl.when(jnp.logical_and(pl.program_id(0) == 0, n == 0))
    def _():
        k_idx = lax.broadcasted_iota(jnp.int32, (L, L), 0)
        q_idx = lax.broadcasted_iota(jnp.int32, (L, L), 1)
        for h in range(SWA_Q_HEADS):
            g, gi = divmod(h, SWA_GROUP)
            t0 = rows_ref[0, h:h + 1, MOBA_BLOCK:MOBA_BLOCK + L]
            r = _toeplitz(jnp.concatenate([t0, t0], axis=1), L)
            cols = slice(gi * L, (gi + 1) * L)
            bias_ref[g, 0:L, cols] = jnp.where(q_idx < k_idx, r, NEG_INF)
            bias_ref[g, L:2 * L, cols] = jnp.where(q_idx >= k_idx, r, NEG_INF)

    first_rows = jnp.where(n == 0, L, 0)
    key_row = lax.broadcasted_iota(jnp.int32, (2 * L, SWA_GROUP * L), 0)
    ones_blk = jnp.concatenate([jnp.ones((1, 2 * L), BF16), jnp.zeros((ACC_ROWS - HEAD_DIM - 1, 2 * L), BF16)],
                               axis=0)
    chains = [(c, g) for c in range(SWA_STEP_BLOCKS) for g in range(SWA_KV_HEADS)]

    def scores(idx):
        c, g = chains[idx]
        hs = slice(g * HEAD_DIM, (g + 1) * HEAD_DIM)
        if c == 0:
            kw = jnp.concatenate([kp_ref[0, :, hs], kc_ref[0, 0:L, hs]], axis=0)
        else:
            kw = kc_ref[0, (c - 1) * L:(c + 1) * L, hs]
        qg = jnp.concatenate(
            [q_ref[0, c, (g * SWA_GROUP + gi) * HEAD_DIM:(g * SWA_GROUP + gi + 1) * HEAD_DIM, :]
             for gi in range(SWA_GROUP)], axis=1)
        s = _dot(kw, qg) + bias_ref[g]
        if c == 0:
            s = jnp.where(key_row < first_rows, NEG_INF, s)
        s_ref[idx] = s

    def attend(idx):
        c, g = chains[idx]
        hs = slice(g * HEAD_DIM, (g + 1) * HEAD_DIM)
        s = s_ref[idx]
        sink = sink_ref[g]
        m = jnp.maximum(jnp.max(s, axis=0, keepdims=True), sink)
        p = jnp.exp2(s - m).astype(BF16)
        v_prev = vp_ref[0, 0, hs, :] if c == 0 else vc_ref[0, c - 1, hs, :]
        v_aug = jnp.concatenate([jnp.concatenate([v_prev, vc_ref[0, c, hs, :]], axis=1), ones_blk],
                                axis=0)
        pv = _dot(v_aug, p)
        o = pv[0:HEAD_DIM] / (pv[HEAD_DIM:HEAD_DIM + 1] + jnp.exp2(sink - m))
        for gi in range(SWA_GROUP):
            r = (g * SWA_GROUP + gi) * HEAD_DIM
            ot_ref[c, r:r + HEAD_DIM, :] = o[:, gi * L:(gi + 1) * L]

    scores(0)
    for idx in range(len(chains)):
        if idx + 1 < len(chains):
            scores(idx + 1)
        attend(idx)
    for c in range(SWA_STEP_BLOCKS):
        o_ref[0, c * L:(c + 1) * L, :] = ot_ref[c].T.astype(o_ref.dtype)


def _swa(qa_t, ka3, va_t, rows, sink_rows):
    batch, nblk = qa_t.shape[0], qa_t.shape[1]
    seq = nblk * SWA_BLOCK
    nb = SWA_STEP_BLOCKS
    assert nblk % nb == 0
    prev = lambda n: jnp.maximum(n * nb - 1, 0)
    return pl.pallas_call(
        _swa_kernel,
        grid=(batch, nblk // nb),
        in_specs=[
            pl.BlockSpec((1, nb, SWA_Q_W, SWA_BLOCK), lambda b, n: (b, n, 0, 0)),
            pl.BlockSpec((1, SWA_BLOCK, SWA_KV_W), lambda b, n: (b, prev(n), 0)),
            pl.BlockSpec((1, nb * SWA_BLOCK, SWA_KV_W), lambda b, n: (b, n, 0)),
            pl.BlockSpec((1, 1, SWA_KV_W, SWA_BLOCK), lambda b, n: (b, prev(n), 0, 0)),
            pl.BlockSpec((1, nb, SWA_KV_W, SWA_BLOCK), lambda b, n: (b, n, 0, 0)),
            _resident((1, N_ATTN_HEADS, 2 * MOBA_BLOCK), lambda b, n: (0, 0, 0)),
            _resident(sink_rows.shape, lambda b, n: (0, 0, 0)),
        ],
        out_specs=pl.BlockSpec((1, nb * SWA_BLOCK, SWA_Q_W), lambda b, n: (b, n, 0)),
        out_shape=jax.ShapeDtypeStruct((batch, seq, SWA_Q_W), BF16),
        scratch_shapes=[pltpu.VMEM((SWA_KV_HEADS, 2 * SWA_BLOCK, SWA_GROUP * SWA_BLOCK), F32),
                        pltpu.VMEM((nb, SWA_Q_W, SWA_BLOCK), F32),
                        pltpu.VMEM((nb * SWA_KV_HEADS, 2 * SWA_BLOCK, SWA_GROUP * SWA_BLOCK), F32)],
        compiler_params=pltpu.CompilerParams(dimension_semantics=("arbitrary", "arbitrary"),
                                             vmem_limit_bytes=VMEM_LIMIT_BYTES),
        name="swa",
    )(qa_t, ka3, ka3, va_t, va_t, rows, sink_rows)


def _select_kernel(k_ref, q_ref, o_ref):
    seq = k_ref.shape[1]
    nblk = seq // MOBA_BLOCK
    blk = lax.broadcasted_iota(jnp.int32, (nblk, seq), 0)
    qblk = lax.broadcasted_iota(jnp.int32, (nblk, seq), 1) // MOBA_BLOCK
    past = blk < qblk
    member = jnp.where(blk == qblk, 1.0 / MOBA_BLOCK, 0.0).astype(BF16)
    for hh in range(2):
        hs = slice(hh * HEAD_DIM, (hh + 1) * HEAD_DIM)
        kmean = _dot(member, k_ref[0, :, hs])
        q_t = jnp.concatenate([q_ref[0, t, hs, :] for t in range(nblk)], axis=1)
        gate = jnp.dot(kmean, q_t.astype(F32), preferred_element_type=F32,
                       precision=lax.Precision.HIGHEST)
        gate = jnp.where(past, gate, NEG_INF)
        rank = jnp.zeros((nblk, seq), jnp.int32)
        for r in range(nblk):
            row = gate[r:r + 1, :]
            rank = rank + jnp.where(blk > r, jnp.where(row >= gate, 1, 0),
                                    jnp.where(row > gate, 1, 0))
        sel = jnp.logical_and(past, rank < MOBA_TOPK)
        o_ref[0, hh] = jnp.where(jnp.logical_or(sel, blk == qblk), 0.0, NEG_INF)


def _select(kb3, qb_t):
    batch, seq = kb3.shape[0], kb3.shape[1]
    nblk = seq // MOBA_BLOCK
    pairs = MOBA_HEADS // 2
    return pl.pallas_call(
        _select_kernel,
        grid=(batch, pairs),
        in_specs=[
            pl.BlockSpec((1, seq, 2 * HEAD_DIM), lambda b, hp: (b, 0, hp)),
            pl.BlockSpec((1, nblk, 2 * HEAD_DIM, MOBA_BLOCK), lambda b, hp: (b, 0, hp, 0)),
        ],
        out_specs=pl.BlockSpec((1, 2, nblk, seq), lambda b, hp: (b, hp, 0, 0)),
        out_shape=jax.ShapeDtypeStruct((batch, MOBA_HEADS, nblk, seq), F32),
        compiler_params=pltpu.CompilerParams(dimension_semantics=("arbitrary", "arbitrary"),
                                             vmem_limit_bytes=VMEM_LIMIT_BYTES),
        name="moba_select",
    )(kb3, qb_t)


def _moba_kernel(q_ref, k_ref, v_ref, sel_ref, rows_ref, o_ref, bias_ref, m_ref, acc_ref, s_ref, cm_ref):
    i = pl.program_id(1)
    MB = MOBA_BLOCK

    @pl.when(jnp.logical_and(pl.program_id(0) == 0, i == 0))
    def _():
        for h in range(MOBA_HEADS):
            for d in range(BIAS_SLOTS):
                bias_ref[h, d] = _toeplitz(rows_ref[d, SWA_Q_HEADS + h:SWA_Q_HEADS + h + 1, :], MB)

    m_ref[...] = jnp.full(m_ref.shape, NEG_INF, F32)
    acc_ref[...] = jnp.zeros(acc_ref.shape, F32)
    heads = [slice(h * HEAD_DIM, (h + 1) * HEAD_DIM) for h in range(MOBA_HEADS)]
    ones_blk = jnp.concatenate([jnp.ones((1, MB), BF16), jnp.zeros((ACC_ROWS - HEAD_DIM - 1, MB), BF16)],
                               axis=0)

    def scores(j, slot, h, hs):
        rows = pl.ds(pl.multiple_of(j * MB, MB), MB)
        s = _dot(k_ref[0, rows, hs], q_ref[0, 0, hs, :]) + bias_ref[h, slot]
        s_ref[h] = s
        cm_ref[h] = jnp.max(s, axis=0, keepdims=True) + sel_ref[0, h, pl.ds(j, 1), :]

    def accumulate(j, h, hs):
        m_old = m_ref[h]
        m_new = jnp.maximum(m_old, cm_ref[h])
        alpha = jnp.exp2(m_old - m_new)
        p = jnp.exp2(s_ref[h] - (m_new - sel_ref[0, h, pl.ds(j, 1), :])).astype(BF16)
        v_aug = jnp.concatenate([v_ref[0, j, hs, :], ones_blk], axis=0)
        rs = slice(h * ACC_ROWS, (h + 1) * ACC_ROWS)
        acc_ref[rs, :] = alpha * acc_ref[rs, :] + _dot(v_aug, p)
        m_ref[h] = m_new

    for h, hs in enumerate(heads):
        scores(i, 0, h, hs)

    def body(t, carry):
        j = i - t
        slot = jnp.minimum(t + 1, BIAS_SLOTS - 1)
        for h, hs in enumerate(heads):
            accumulate(j, h, hs)
            scores(j - 1, slot, h, hs)
        return carry

    def body2(u, carry):
        return body(2 * u + 1, body(2 * u, carry))

    lax.fori_loop(0, i // 2, body2, 0)

    @pl.when(i % 2 == 1)
    def _():
        body(i - 1, 0)

    outs = []
    for h, hs in enumerate(heads):
        accumulate(0, h, hs)
        base = h * ACC_ROWS
        outs.append(acc_ref[base:base + HEAD_DIM, :] / acc_ref[base + HEAD_DIM:base + HEAD_DIM + 1, :])
    o_ref[0] = jnp.concatenate(outs, axis=0).T.astype(o_ref.dtype)


def _moba(qb_t, kb3, vb_t, sel, rows):
    batch, nblk = qb_t.shape[0], qb_t.shape[1]
    seq = nblk * MOBA_BLOCK
    return pl.pallas_call(
        _moba_kernel,
        grid=(batch, nblk),
        in_specs=[
            pl.BlockSpec((1, 1, MOBA_W, MOBA_BLOCK), lambda b, i: (b, i, 0, 0)),
            pl.BlockSpec((1, seq, MOBA_W), lambda b, i: (b, 0, 0)),
            pl.BlockSpec((1, nblk, MOBA_W, MOBA_BLOCK), lambda b, i: (b, 0, 0, 0)),
            pl.BlockSpec((1, MOBA_HEADS, nblk, MOBA_BLOCK), lambda b, i: (b, 0, 0, i)),
            _resident(rows.shape, lambda b, i: (0, 0, 0)),
        ],
        out_specs=pl.BlockSpec((1, MOBA_BLOCK, MOBA_W), lambda b, i: (b, i, 0)),
        out_shape=jax.ShapeDtypeStruct((batch, seq, MOBA_W), BF16),
        scratch_shapes=[pltpu.VMEM((MOBA_HEADS, BIAS_SLOTS, MOBA_BLOCK, MOBA_BLOCK), F32),
                        pltpu.VMEM((MOBA_HEADS, 1, MOBA_BLOCK), F32),
                        pltpu.VMEM((MOBA_HEADS * ACC_ROWS, MOBA_BLOCK), F32),
                        pltpu.VMEM((MOBA_HEADS, MOBA_BLOCK, MOBA_BLOCK), F32),
                        pltpu.VMEM((MOBA_HEADS, 1, MOBA_BLOCK), F32)],
        compiler_params=pltpu.CompilerParams(dimension_semantics=("arbitrary", "arbitrary"),
                                             vmem_limit_bytes=VMEM_LIMIT_BYTES),
        name="moba",
    )(qb_t, kb3, vb_t, sel, rows)


def _post_kernel(x_ref, ya_ref, yb_ref, ga_ref, gb_ref, gate1_ref, shift2_ref, scale2_ref,
                 gate2_ref, g2_ref, gf_ref, wa_ref, wb_ref, wo_ref, w1_ref, w2_ref, o_ref):
    a = _dot(ya_ref[...], wa_ref[...])
    b = _dot(yb_ref[...], wb_ref[...])
    merged = (jax.nn.sigmoid(ga_ref[...].astype(F32)) * a
              + jax.nn.sigmoid(gb_ref[...].astype(F32)) * b)
    x1 = x_ref[...] + gate1_ref[0, 0] * _dot(merged.astype(BF16), wo_ref[...])
    h2 = _rms_modulate(x1, g2_ref[...], shift2_ref[0, 0], scale2_ref[0, 0]).astype(BF16)
    y = jnp.zeros_like(x1)
    for c in range(D_FF // FF_CHUNK):
        cs = slice(c * FF_CHUNK, (c + 1) * FF_CHUNK)
        u = jnp.square(jnp.maximum(_dot(h2, w1_ref[:, cs]), 0.0)).astype(BF16)
        y = y + _dot(u, w2_ref[cs, :])
    x2 = x1 + gate2_ref[0, 0] * y
    ms = jnp.mean(x2 * x2, axis=-1, keepdims=True)
    o_ref[...] = (x2 * lax.rsqrt(ms + RMS_EPS)) * gf_ref[...]


def _post(x2, ya, yb, ga, gb, mod4, g2, gf, wa, wb, wo, w1, w2, seq):
    tokens = x2.shape[0]
    tiles_per_seq = seq // TOKEN_TILE
    tok = lambda w: pl.BlockSpec((TOKEN_TILE, w), lambda i: (i, 0))
    modrow = lambda k: pl.BlockSpec((1, 1, 1, D_MODEL), lambda i: (i // tiles_per_seq, k, 0, 0))
    full = lambda a: _resident(a.shape, lambda i: (0, 0))
    return pl.pallas_call(
        _post_kernel,
        grid=(tokens // TOKEN_TILE,),
        in_specs=[tok(D_MODEL), tok(SWA_Q_W), tok(MOBA_W), tok(D_MODEL), tok(D_MODEL),
                  modrow(2), modrow(3), modrow(4), modrow(5),
                  full(g2), full(gf), full(wa), full(wb), full(wo), full(w1), full(w2)],
        out_specs=tok(D_MODEL),
        out_shape=jax.ShapeDtypeStruct((tokens, D_MODEL), F32),
        compiler_params=pltpu.CompilerParams(dimension_semantics=("arbitrary",),
                                             vmem_limit_bytes=VMEM_LIMIT_BYTES),
        name="post",
    )(x2, ya, yb, ga, gb, mod4, mod4, mod4, mod4, g2, gf, wa, wb, wo, w1, w2)


def kernel(x, c, ada_w, ada_b, norm1_g, norm2_g, w_in, attn_sinks, rel_bias, w_branch_a,
           w_branch_b, w_out, w_mlp_in, w_mlp_out, final_g):
    batch, seq, _ = x.shape
    depth = ada_w.shape[0]
    nblk_b = seq // MOBA_BLOCK
    assert seq % TOKEN_TILE == 0 and TOKEN_TILE % MOBA_BLOCK == 0 and batch <= 8
    assert (BIAS_SLOTS - 1) * MOBA_BLOCK - (MOBA_BLOCK - 1) >= 1513 and MAX_DISTANCE == 2048
    assert 2 * SWA_BLOCK <= MOBA_BLOCK and SWA_WINDOW == SWA_BLOCK
    assert depth == 1
    l = 0

    rows = _bias_rows(rel_bias)
    c_pad = jnp.zeros((8, D_MODEL), F32).at[:batch].set(c)

    x2 = x.reshape(batch * seq, D_MODEL)
    mod = _ada(c_pad, ada_w[l], ada_b[l][None, :])
    mod4 = mod[:batch].reshape(batch, N_MOD, 1, D_MODEL)
    col_scale = jnp.ones((sum(IN_WIDTHS),), F32)
    for k in (0, 3):
        col_scale = col_scale.at[IN_OFFS[k]:IN_OFFS[k + 1]].set(ATTN_SCALE * LOG2E)
    w_bf16 = (w_in[l] * col_scale[None, :]).astype(BF16)
    ka, kb, ga, gb, qa_t, va_t, qb_t, vb_t = _proj(
        x2, mod4, norm1_g[l][None, :], w_bf16, batch, seq)

    sink_rows = jnp.broadcast_to(
        (attn_sinks[l].astype(F32) * LOG2E).reshape(SWA_KV_HEADS, 1, SWA_GROUP, 1),
        (SWA_KV_HEADS, 1, SWA_GROUP, SWA_BLOCK)).reshape(SWA_KV_HEADS, 1, SWA_GROUP * SWA_BLOCK)
    ya = _swa(qa_t, ka.reshape(batch, seq, SWA_KV_W), va_t, rows, sink_rows)

    kb3 = kb.reshape(batch, seq, MOBA_W)
    sel = _select(kb3, qb_t)
    yb = _moba(qb_t, kb3, vb_t, sel, rows)

    out = _post(x2, ya.reshape(batch * seq, SWA_Q_W), yb.reshape(batch * seq, MOBA_W), ga, gb,
                mod4, norm2_g[l][None, :], final_g[None, :],
                w_branch_a[l].astype(BF16), w_branch_b[l].astype(BF16), w_out[l].astype(BF16),
                w_mlp_in[l].astype(BF16), w_mlp_out[l].astype(BF16), seq)
    return out.reshape(batch, seq, D_MODEL)
```

```python
import math

import jax
import jax.numpy as jnp
from jax import lax
from jax.experimental import pallas as pl
from jax.experimental.pallas import tpu as pltpu

D_MODEL = 1024
HEAD_DIM = 64
ATTN_SCALE = HEAD_DIM ** -0.5
LOG2E = math.log2(math.e)
SWA_Q_HEADS = 8
SWA_KV_HEADS = 2
SWA_GROUP = SWA_Q_HEADS // SWA_KV_HEADS
SWA_WINDOW = 128
SWA_BLOCK = 128
MOBA_HEADS = 8
MOBA_BLOCK = 256
MOBA_TOPK = 3
NUM_BUCKETS = 32
MAX_EXACT = NUM_BUCKETS // 2
MAX_DISTANCE = 2048
N_ATTN_HEADS = SWA_Q_HEADS + MOBA_HEADS
SWA_Q_W = SWA_Q_HEADS * HEAD_DIM
SWA_KV_W = SWA_KV_HEADS * HEAD_DIM
MOBA_W = MOBA_HEADS * HEAD_DIM
D_FF = 4 * D_MODEL
N_MOD = 6
RMS_EPS = 1e-6

VMEM_LIMIT_BYTES = 56 * 1024 * 1024
TOKEN_TILE = 512
FF_CHUNK = 1024
BIAS_SLOTS = 8
SWA_STEP_BLOCKS = 4
ACC_ROWS = HEAD_DIM + 16
NEG_INF = float("-inf")

F32 = jnp.float32
BF16 = jnp.bfloat16


def _resident(block_shape, index_map):
    return pl.BlockSpec(block_shape, index_map, pipeline_mode=pl.Buffered(1))


def _dot(a, b):
    return jnp.dot(a, b, preferred_element_type=F32)


def _dot_nt(a, b):
    return lax.dot_general(a, b, (((1,), (1,)), ((), ())), preferred_element_type=F32)


def _rms_modulate(xv, g, shift, scale):
    ms = jnp.mean(xv * xv, axis=-1, keepdims=True)
    y = xv * lax.rsqrt(ms + RMS_EPS)
    return (y * g) * (1.0 + scale) + shift


def _toeplitz(u, rows):
    wide = jnp.broadcast_to(u, (rows, 2 * rows))
    return pltpu.roll(wide, 0, 1, stride=1, stride_axis=0)[:, rows:]


def _ada_kernel(c_ref, w_ref, b_ref, o_ref):
    cs = jax.nn.silu(c_ref[...])
    o_ref[...] = jnp.dot(cs, w_ref[...], preferred_element_type=F32,
                         precision=lax.Precision.HIGHEST) + b_ref[...]


def _ada(c_pad, w, b):
    n = w.shape[1]
    bn = 1536
    return pl.pallas_call(
        _ada_kernel,
        grid=(n // bn,),
        in_specs=[pl.BlockSpec((8, D_MODEL), lambda j: (0, 0)),
                  pl.BlockSpec((D_MODEL, bn), lambda j: (0, j)),
                  pl.BlockSpec((1, bn), lambda j: (0, j))],
        out_specs=pl.BlockSpec((8, bn), lambda j: (0, j)),
        out_shape=jax.ShapeDtypeStruct((8, n), F32),
        compiler_params=pltpu.CompilerParams(dimension_semantics=("arbitrary",),
                                             vmem_limit_bytes=VMEM_LIMIT_BYTES),
        name="ada",
    )(c_pad, w, b)


def _t5_bucket(dist):
    n = jnp.maximum(dist, 0)
    nf = jnp.maximum(n, 1).astype(F32)
    large = MAX_EXACT + (jnp.log(nf / MAX_EXACT) / math.log(MAX_DISTANCE / MAX_EXACT)
                         * (NUM_BUCKETS - MAX_EXACT)).astype(jnp.int32)
    large = jnp.minimum(large, NUM_BUCKETS - 1)
    return jnp.where(n < MAX_EXACT, n, large)


def _bias_rows_kernel(bucket_ref, rbt_ref, o_ref):
    bucket = bucket_ref[...]
    n = bucket.shape[1]
    onehot = jnp.where(lax.broadcasted_iota(jnp.int32, (NUM_BUCKETS, n), 0) == bucket, 1.0, 0.0)
    rows = jnp.dot(rbt_ref[...], onehot, preferred_element_type=F32,
                   precision=lax.Precision.HIGHEST)
    rows = jnp.where(bucket >= 0, rows * LOG2E, NEG_INF)
    for d in range(BIAS_SLOTS):
        o_ref[d] = rows[:, d * MOBA_BLOCK:(d + 2) * MOBA_BLOCK]


def _bias_rows(rel_bias):
    n = (BIAS_SLOTS + 1) * MOBA_BLOCK
    dist = jnp.arange(n, dtype=jnp.int32) - MOBA_BLOCK
    bucket = jnp.where(dist >= 0, _t5_bucket(dist), -1)[None, :]
    return pl.pallas_call(
        _bias_rows_kernel,
        out_shape=jax.ShapeDtypeStruct((BIAS_SLOTS, N_ATTN_HEADS, 2 * MOBA_BLOCK), F32),
        name="bias_rows",
    )(bucket, rel_bias.astype(F32).T)


IN_WIDTHS = (SWA_Q_W, SWA_KV_W, SWA_KV_W, MOBA_W, MOBA_W, MOBA_W, D_MODEL, D_MODEL)
IN_OFFS = tuple(sum(IN_WIDTHS[:k]) for k in range(len(IN_WIDTHS) + 1))
FEAT_COLS = (0, 2, 3, 5)
FEAT_W = sum(IN_WIDTHS[k] for k in FEAT_COLS)


def _proj_kernel(x_ref, shift_ref, scale_ref, g_ref, w_ref,
                 ka_ref, kb_ref, ga_ref, gb_ref, qa_ref, va_ref, qb_ref, vb_ref, wt_ref):
    @pl.when(pl.program_id(0) == 0)
    def _():
        o = 0
        for k in FEAT_COLS:
            wt_ref[o:o + IN_WIDTHS[k], :] = w_ref[:, IN_OFFS[k]:IN_OFFS[k + 1]].T
            o += IN_WIDTHS[k]

    h = _rms_modulate(x_ref[...], g_ref[...], shift_ref[0, 0], scale_ref[0, 0]).astype(BF16)
    for ref, k in ((ka_ref, 1), (kb_ref, 4), (ga_ref, 6), (gb_ref, 7)):
        ref[...] = _dot(h, w_ref[:, IN_OFFS[k]:IN_OFFS[k + 1]]).astype(ref.dtype)
    o = 0
    for ref, k in zip((qa_ref, va_ref, qb_ref, vb_ref), FEAT_COLS):
        w = IN_WIDTHS[k]
        ft = _dot_nt(wt_ref[o:o + w, :], h).astype(ref.dtype)
        blk = ref.shape[-1]
        for t in range(TOKEN_TILE // blk):
            ref[0, t] = ft[:, t * blk:(t + 1) * blk]
        o += w


def _proj(x2, mod4, g1, w_bf16, batch, seq):
    tokens = x2.shape[0]
    tiles_per_seq = seq // TOKEN_TILE
    na = TOKEN_TILE // SWA_BLOCK
    nb = TOKEN_TILE // MOBA_BLOCK

    def feat_spec(width, blk, per_tile):
        return pl.BlockSpec((1, per_tile, width, blk),
                            lambda i: (i // tiles_per_seq, i % tiles_per_seq, 0, 0))

    out_shape = (
        jax.ShapeDtypeStruct((tokens, SWA_KV_W), BF16),
        jax.ShapeDtypeStruct((tokens, MOBA_W), BF16),
        jax.ShapeDtypeStruct((tokens, D_MODEL), BF16),
        jax.ShapeDtypeStruct((tokens, D_MODEL), BF16),
        jax.ShapeDtypeStruct((batch, seq // SWA_BLOCK, SWA_Q_W, SWA_BLOCK), BF16),
        jax.ShapeDtypeStruct((batch, seq // SWA_BLOCK, SWA_KV_W, SWA_BLOCK), BF16),
        jax.ShapeDtypeStruct((batch, seq // MOBA_BLOCK, MOBA_W, MOBA_BLOCK), BF16),
        jax.ShapeDtypeStruct((batch, seq // MOBA_BLOCK, MOBA_W, MOBA_BLOCK), BF16),
    )
    tok_spec = lambda w: pl.BlockSpec((TOKEN_TILE, w), lambda i: (i, 0))
    return pl.pallas_call(
        _proj_kernel,
        grid=(tokens // TOKEN_TILE,),
        in_specs=[
            pl.BlockSpec((TOKEN_TILE, D_MODEL), lambda i: (i, 0)),
            pl.BlockSpec((1, 1, 1, D_MODEL), lambda i: (i // tiles_per_seq, 0, 0, 0)),
            pl.BlockSpec((1, 1, 1, D_MODEL), lambda i: (i // tiles_per_seq, 1, 0, 0)),
            _resident((1, D_MODEL), lambda i: (0, 0)),
            _resident(w_bf16.shape, lambda i: (0, 0)),
        ],
        out_specs=(tok_spec(SWA_KV_W), tok_spec(MOBA_W), tok_spec(D_MODEL), tok_spec(D_MODEL),
                   feat_spec(SWA_Q_W, SWA_BLOCK, na), feat_spec(SWA_KV_W, SWA_BLOCK, na),
                   feat_spec(MOBA_W, MOBA_BLOCK, nb), feat_spec(MOBA_W, MOBA_BLOCK, nb)),
        out_shape=out_shape,
        scratch_shapes=[pltpu.VMEM((FEAT_W, D_MODEL), BF16)],
        compiler_params=pltpu.CompilerParams(dimension_semantics=("arbitrary",),
                                             vmem_limit_bytes=VMEM_LIMIT_BYTES),
        name="proj",
    )(x2, mod4, mod4, g1, w_bf16)


def _swa_kernel(q_ref, kp_ref, kc_ref, vp_ref, vc_ref, rows_ref, sink_ref, o_ref, bias_ref, ot_ref, s_ref):
    n = pl.program_id(1)
    L = SWA_BLOCK

    @pl.when(jnp.logical_and(pl.program_id(0) == 0, n == 0))
    def _():
        k_idx = lax.broadcasted_iota(jnp.int32, (L, L), 0)
        q_idx = lax.broadcasted_iota(jnp.int32, (L, L), 1)
        for h in range(SWA_Q_HEADS):
            g, gi = divmod(h, SWA_GROUP)
            t0 = rows_ref[0, h:h + 1, MOBA_BLOCK:MOBA_BLOCK + L]
            r = _toeplitz(jnp.concatenate([t0, t0], axis=1), L)
            cols = slice(gi * L, (gi + 1) * L)
            bias_ref[g, 0:L, cols] = jnp.where(q_idx < k_idx, r, NEG_INF)
            bias_ref[g, L:2 * L, cols] = jnp.where(q_idx >= k_idx, r, NEG_INF)

    first_rows = jnp.where(n == 0, L, 0)
    key_row = lax.broadcasted_iota(jnp.int32, (2 * L, SWA_GROUP * L), 0)
    ones_blk = jnp.concatenate([jnp.ones((1, 2 * L), BF16), jnp.zeros((ACC_ROWS - HEAD_DIM - 1, 2 * L), BF16)],
                               axis=0)
    chains = [(c, g) for c in range(SWA_STEP_BLOCKS) for g in range(SWA_KV_HEADS)]

    def scores(idx):
        c, g = chains[idx]
        hs = slice(g * HEAD_DIM, (g + 1) * HEAD_DIM)
        if c == 0:
            kw = jnp.concatenate([kp_ref[0, :, hs], kc_ref[0, 0:L, hs]], axis=0)
        else:
            kw = kc_ref[0, (c - 1) * L:(c + 1) * L, hs]
        qg = jnp.concatenate(
            [q_ref[0, c, (g * SWA_GROUP + gi) * HEAD_DIM:(g * SWA_GROUP + gi + 1) * HEAD_DIM, :]
             for gi in range(SWA_GROUP)], axis=1)
        s = _dot(kw, qg) + bias_ref[g]
        if c == 0:
            s = jnp.where(key_row < first_rows, NEG_INF, s)
        s_ref[idx] = s

    def attend(idx):
        c, g = chains[idx]
        hs = slice(g * HEAD_DIM, (g + 1) * HEAD_DIM)
        s = s_ref[idx]
        sink = sink_ref[g]
        m = jnp.maximum(jnp.max(s, axis=0, keepdims=True), sink)
        p = jnp.exp2(s - m).astype(BF16)
        v_prev = vp_ref[0, 0, hs, :] if c == 0 else vc_ref[0, c - 1, hs, :]
        v_aug = jnp.concatenate([jnp.concatenate([v_prev, vc_ref[0, c, hs, :]], axis=1), ones_blk],
                                axis=0)
        pv = _dot(v_aug, p)
        o = pv[0:HEAD_DIM] / (pv[HEAD_DIM:HEAD_DIM + 1] + jnp.exp2(sink - m))
        for gi in range(SWA_GROUP):
            r = (g * SWA_GROUP + gi) * HEAD_DIM
            ot_ref[c, r:r + HEAD_DIM, :] = o[:, gi * L:(gi + 1) * L]

    scores(0)
    for idx in range(len(chains)):
        if idx + 1 < len(chains):
            scores(idx + 1)
        attend(idx)
    for c in range(SWA_STEP_BLOCKS):
        o_ref[0, c * L:(c + 1) * L, :] = ot_ref[c].T.astype(o_ref.dtype)


def _swa(qa_t, ka3, va_t, rows, sink_rows):
    batch, nblk = qa_t.shape[0], qa_t.shape[1]
    seq = nblk * SWA_BLOCK
    nb = SWA_STEP_BLOCKS
    assert nblk % nb == 0
    prev = lambda n: jnp.maximum(n * nb - 1, 0)
    return pl.pallas_call(
        _swa_kernel,
        grid=(batch, nblk // nb),
        in_specs=[
            pl.BlockSpec((1, nb, SWA_Q_W, SWA_BLOCK), lambda b, n: (b, n, 0, 0)),
            pl.BlockSpec((1, SWA_BLOCK, SWA_KV_W), lambda b, n: (b, prev(n), 0)),
            pl.BlockSpec((1, nb * SWA_BLOCK, SWA_KV_W), lambda b, n: (b, n, 0)),
            pl.BlockSpec((1, 1, SWA_KV_W, SWA_BLOCK), lambda b, n: (b, prev(n), 0, 0)),
            pl.BlockSpec((1, nb, SWA_KV_W, SWA_BLOCK), lambda b, n: (b, n, 0, 0)),
            _resident((1, N_ATTN_HEADS, 2 * MOBA_BLOCK), lambda b, n: (0, 0, 0)),
            _resident(sink_rows.shape, lambda b, n: (0, 0, 0)),
        ],
        out_specs=pl.BlockSpec((1, nb * SWA_BLOCK, SWA_Q_W), lambda b, n: (b, n, 0)),
        out_shape=jax.ShapeDtypeStruct((batch, seq, SWA_Q_W), BF16),
        scratch_shapes=[pltpu.VMEM((SWA_KV_HEADS, 2 * SWA_BLOCK, SWA_GROUP * SWA_BLOCK), F32),
                        pltpu.VMEM((nb, SWA_Q_W, SWA_BLOCK), F32),
                        pltpu.VMEM((nb * SWA_KV_HEADS, 2 * SWA_BLOCK, SWA_GROUP * SWA_BLOCK), F32)],
        compiler_params=pltpu.CompilerParams(dimension_semantics=("arbitrary", "arbitrary"),
                                             vmem_limit_bytes=VMEM_LIMIT_BYTES),
        name="swa",
    )(qa_t, ka3, ka3, va_t, va_t, rows, sink_rows)


def _select_kernel(k_ref, q_ref, o_ref):
    seq = k_ref.shape[1]
    nblk = seq // MOBA_BLOCK
    blk = lax.broadcasted_iota(jnp.int32, (nblk, seq), 0)
    qblk = lax.broadcasted_iota(jnp.int32, (nblk, seq), 1) // MOBA_BLOCK
    past = blk < qblk
    member = jnp.where(blk == qblk, 1.0 / MOBA_BLOCK, 0.0).astype(BF16)
    for hh in range(2):
        hs = slice(hh * HEAD_DIM, (hh + 1) * HEAD_DIM)
        kmean = _dot(member, k_ref[0, :, hs])
        q_t = jnp.concatenate([q_ref[0, t, hs, :] for t in range(nblk)], axis=1)
        gate = jnp.dot(kmean, q_t.astype(F32), preferred_element_type=F32,
                       precision=lax.Precision.HIGHEST)
        gate = jnp.where(past, gate, NEG_INF)
        rank = jnp.zeros((nblk, seq), jnp.int32)
        for r in range(nblk):
            row = gate[r:r + 1, :]
            rank = rank + jnp.where(blk > r, jnp.where(row >= gate, 1, 0),
                                    jnp.where(row > gate, 1, 0))
        sel = jnp.logical_and(past, rank < MOBA_TOPK)
        o_ref[0, hh] = jnp.where(jnp.logical_or(sel, blk == qblk), 0.0, NEG_INF)


def _select(kb3, qb_t):
    batch, seq = kb3.shape[0], kb3.shape[1]
    nblk = seq // MOBA_BLOCK
    pairs = MOBA_HEADS // 2
    return pl.pallas_call(
        _select_kernel,
        grid=(batch, pairs),
        in_specs=[
            pl.BlockSpec((1, seq, 2 * HEAD_DIM), lambda b, hp: (b, 0, hp)),
            pl.BlockSpec((1, nblk, 2 * HEAD_DIM, MOBA_BLOCK), lambda b, hp: (b, 0, hp, 0)),
        ],
        out_specs=pl.BlockSpec((1, 2, nblk, seq), lambda b, hp: (b, hp, 0, 0)),
        out_shape=jax.ShapeDtypeStruct((batch, MOBA_HEADS, nblk, seq), F32),
        compiler_params=pltpu.CompilerParams(dimension_semantics=("arbitrary", "arbitrary"),
                                             vmem_limit_bytes=VMEM_LIMIT_BYTES),
        name="moba_select",
    )(kb3, qb_t)


def _moba_kernel(q_ref, qn_ref, k_ref, v_ref, sel_ref, rows_ref, o_ref, bias_ref, m_ref, acc_ref, s_ref, cm_ref):
    i = pl.program_id(1)
    nblk = pl.num_programs(1)
    MB = MOBA_BLOCK

    @pl.when(jnp.logical_and(pl.program_id(0) == 0, i == 0))
    def _():
        for h in range(MOBA_HEADS):
            for d in range(BIAS_SLOTS):
                bias_ref[h, d] = _toeplitz(rows_ref[d, SWA_Q_HEADS + h:SWA_Q_HEADS + h + 1, :], MB)

    m_ref[...] = jnp.full(m_ref.shape, NEG_INF, F32)
    acc_ref[...] = jnp.zeros(acc_ref.shape, F32)
    heads = [slice(h * HEAD_DIM, (h + 1) * HEAD_DIM) for h in range(MOBA_HEADS)]
    ones_blk = jnp.concatenate([jnp.ones((1, MB), BF16), jnp.zeros((ACC_ROWS - HEAD_DIM - 1, MB), BF16)],
                               axis=0)

    def scores(j, slot, h, hs):
        rows = pl.ds(pl.multiple_of(j * MB, MB), MB)
        s = _dot(k_ref[0, rows, hs], q_ref[0, 0, hs, :]) + bias_ref[h, slot]
        s_ref[h] = s
        cm_ref[h] = jnp.max(s, axis=0, keepdims=True) + sel_ref[0, h, pl.ds(j, 1), :]

    def own_scores(blk, qr, h, hs):
        rows = pl.ds(pl.multiple_of(blk * MB, MB), MB)
        s = _dot(k_ref[0, rows, hs], qr[0, 0, hs, :]) + bias_ref[h, 0]
        s_ref[h] = s
        cm_ref[h] = jnp.max(s, axis=0, keepdims=True)

    def accumulate(j, h, hs):
        m_old = m_ref[h]
        m_new = jnp.maximum(m_old, cm_ref[h])
        alpha = jnp.exp2(m_old - m_new)
        p = jnp.exp2(s_ref[h] - (m_new - sel_ref[0, h, pl.ds(j, 1), :])).astype(BF16)
        v_aug = jnp.concatenate([v_ref[0, j, hs, :], ones_blk], axis=0)
        rs = slice(h * ACC_ROWS, (h + 1) * ACC_ROWS)
        acc_ref[rs, :] = alpha * acc_ref[rs, :] + _dot(v_aug, p)
        m_ref[h] = m_new

    @pl.when(i == 0)
    def _():
        for h, hs in enumerate(heads):
            own_scores(i, q_ref, h, hs)

    def body(t, carry):
        j = i - t
        slot = jnp.minimum(t + 1, BIAS_SLOTS - 1)
        for h, hs in enumerate(heads):
            accumulate(j, h, hs)
            scores(j - 1, slot, h, hs)
        return carry

    def body2(u, carry):
        return body(2 * u + 1, body(2 * u, carry))

    lax.fori_loop(0, i // 2, body2, 0)

    @pl.when(i % 2 == 1)
    def _():
        body(i - 1, 0)

    @pl.when(i < nblk - 1)
    def _():
        for h, hs in enumerate(heads):
            accumulate(0, h, hs)
            own_scores(i + 1, qn_ref, h, hs)

    @pl.when(i == nblk - 1)
    def _():
        for h, hs in enumerate(heads):
            accumulate(0, h, hs)

    outs = []
    for h, hs in enumerate(heads):
        base = h * ACC_ROWS
        outs.append(acc_ref[base:base + HEAD_DIM, :] / acc_ref[base + HEAD_DIM:base + HEAD_DIM + 1, :])
    o_ref[0] = jnp.concatenate(outs, axis=0).T.astype(o_ref.dtype)


def _moba(qb_t, kb3, vb_t, sel, rows):
    batch, nblk = qb_t.shape[0], qb_t.shape[1]
    seq = nblk * MOBA_BLOCK
    return pl.pallas_call(
        _moba_kernel,
        grid=(batch, nblk),
        in_specs=[
            pl.BlockSpec((1, 1, MOBA_W, MOBA_BLOCK), lambda b, i: (b, i, 0, 0)),
            pl.BlockSpec((1, 1, MOBA_W, MOBA_BLOCK), lambda b, i: (b, jnp.minimum(i + 1, nblk - 1), 0, 0)),
            pl.BlockSpec((1, seq, MOBA_W), lambda b, i: (b, 0, 0)),
            pl.BlockSpec((1, nblk, MOBA_W, MOBA_BLOCK), lambda b, i: (b, 0, 0, 0)),
            pl.BlockSpec((1, MOBA_HEADS, nblk, MOBA_BLOCK), lambda b, i: (b, 0, 0, i)),
            _resident(rows.shape, lambda b, i: (0, 0, 0)),
        ],
        out_specs=pl.BlockSpec((1, MOBA_BLOCK, MOBA_W), lambda b, i: (b, i, 0)),
        out_shape=jax.ShapeDtypeStruct((batch, seq, MOBA_W), BF16),
        scratch_shapes=[pltpu.VMEM((MOBA_HEADS, BIAS_SLOTS, MOBA_BLOCK, MOBA_BLOCK), F32),
                        pltpu.VMEM((MOBA_HEADS, 1, MOBA_BLOCK), F32),
                        pltpu.VMEM((MOBA_HEADS * ACC_ROWS, MOBA_BLOCK), F32),
                        pltpu.VMEM((MOBA_HEADS, MOBA_BLOCK, MOBA_BLOCK), F32),
                        pltpu.VMEM((MOBA_HEADS, 1, MOBA_BLOCK), F32)],
        compiler_params=pltpu.CompilerParams(dimension_semantics=("arbitrary", "arbitrary"),
                                             vmem_limit_bytes=VMEM_LIMIT_BYTES),
        name="moba",
    )(qb_t, qb_t, kb3, vb_t, sel, rows)


def _post_kernel(x_ref, ya_ref, yb_ref, ga_ref, gb_ref, gate1_ref, shift2_ref, scale2_ref,
                 gate2_ref, g2_ref, gf_ref, wa_ref, wb_ref, wo_ref, w1_ref, w2_ref, o_ref):
    a = _dot(ya_ref[...], wa_ref[...])
    b = _dot(yb_ref[...], wb_ref[...])
    merged = (jax.nn.sigmoid(ga_ref[...].astype(F32)) * a
              + jax.nn.sigmoid(gb_ref[...].astype(F32)) * b)
    x1 = x_ref[...] + gate1_ref[0, 0] * _dot(merged.astype(BF16), wo_ref[...])
    h2 = _rms_modulate(x1, g2_ref[...], shift2_ref[0, 0], scale2_ref[0, 0]).astype(BF16)
    y = jnp.zeros_like(x1)
    for c in range(D_FF // FF_CHUNK):
        cs = slice(c * FF_CHUNK, (c + 1) * FF_CHUNK)
        u = jnp.square(jnp.maximum(_dot(h2, w1_ref[:, cs]), 0.0)).astype(BF16)
        y = y + _dot(u, w2_ref[cs, :])
    x2 = x1 + gate2_ref[0, 0] * y
    ms = jnp.mean(x2 * x2, axis=-1, keepdims=True)
    o_ref[...] = (x2 * lax.rsqrt(ms + RMS_EPS)) * gf_ref[...]


def _post(x2, ya, yb, ga, gb, mod4, g2, gf, wa, wb, wo, w1, w2, seq):
    tokens = x2.shape[0]
    tiles_per_seq = seq // TOKEN_TILE
    tok = lambda w: pl.BlockSpec((TOKEN_TILE, w), lambda i: (i, 0))
    modrow = lambda k: pl.BlockSpec((1, 1, 1, D_MODEL), lambda i: (i // tiles_per_seq, k, 0, 0))
    full = lambda a: _resident(a.shape, lambda i: (0, 0))
    return pl.pallas_call(
        _post_kernel,
        grid=(tokens // TOKEN_TILE,),
        in_specs=[tok(D_MODEL), tok(SWA_Q_W), tok(MOBA_W), tok(D_MODEL), tok(D_MODEL),
                  modrow(2), modrow(3), modrow(4), modrow(5),
                  full(g2), full(gf), full(wa), full(wb), full(wo), full(w1), full(w2)],
        out_specs=tok(D_MODEL),
        out_shape=jax.ShapeDtypeStruct((tokens, D_MODEL), F32),
        compiler_params=pltpu.CompilerParams(dimension_semantics=("arbitrary",),
                                             vmem_limit_bytes=VMEM_LIMIT_BYTES),
        name="post",
    )(x2, ya, yb, ga, gb, mod4, mod4, mod4, mod4, g2, gf, wa, wb, wo, w1, w2)


def kernel(x, c, ada_w, ada_b, norm1_g, norm2_g, w_in, attn_sinks, rel_bias, w_branch_a,
           w_branch_b, w_out, w_mlp_in, w_mlp_out, final_g):
    batch, seq, _ = x.shape
    depth = ada_w.shape[0]
    nblk_b = seq // MOBA_BLOCK
    assert seq % TOKEN_TILE == 0 and TOKEN_TILE % MOBA_BLOCK == 0 and batch <= 8
    assert (BIAS_SLOTS - 1) * MOBA_BLOCK - (MOBA_BLOCK - 1) >= 1513 and MAX_DISTANCE == 2048
    assert 2 * SWA_BLOCK <= MOBA_BLOCK and SWA_WINDOW == SWA_BLOCK
    assert depth == 1
    l = 0

    rows = _bias_rows(rel_bias)
    c_pad = jnp.zeros((8, D_MODEL), F32).at[:batch].set(c)

    x2 = x.reshape(batch * seq, D_MODEL)
    mod = _ada(c_pad, ada_w[l], ada_b[l][None, :])
    mod4 = mod[:batch].reshape(batch, N_MOD, 1, D_MODEL)
    col_scale = jnp.ones((sum(IN_WIDTHS),), F32)
    for k in (0, 3):
        col_scale = col_scale.at[IN_OFFS[k]:IN_OFFS[k + 1]].set(ATTN_SCALE * LOG2E)
    w_bf16 = (w_in[l] * col_scale[None, :]).astype(BF16)
    ka, kb, ga, gb, qa_t, va_t, qb_t, vb_t = _proj(
        x2, mod4, norm1_g[l][None, :], w_bf16, batch, seq)

    sink_rows = jnp.broadcast_to(
        (attn_sinks[l].astype(F32) * LOG2E).reshape(SWA_KV_HEADS, 1, SWA_GROUP, 1),
        (SWA_KV_HEADS, 1, SWA_GROUP, SWA_BLOCK)).reshape(SWA_KV_HEADS, 1, SWA_GROUP * SWA_BLOCK)
    ya = _swa(qa_t, ka.reshape(batch, seq, SWA_KV_W), va_t, rows, sink_rows)

    kb3 = kb.reshape(batch, seq, MOBA_W)
    sel = _select(kb3, qb_t)
    yb = _moba(qb_t, kb3, vb_t, sel, rows)

    out = _post(x2, ya.reshape(batch * seq, SWA_Q_W), yb.reshape(batch * seq, MOBA_W), ga, gb,
                mod4, norm2_g[l][None, :], final_g[None, :],
                w_branch_a[l].astype(BF16), w_branch_b[l].astype(BF16), w_out[l].astype(BF16),
                w_mlp_in[l].astype(BF16), w_mlp_out[l].astype(BF16), seq)
    return out.reshape(batch, seq, D_MODEL)
```

```python
import functools
import math

import jax
import jax.numpy as jnp
from jax import lax
from jax.experimental import pallas as pl
from jax.experimental.pallas import tpu as pltpu

D_MODEL = 1024
HEAD_DIM = 64
ATTN_SCALE = HEAD_DIM ** -0.5
LOG2E = math.log2(math.e)
SWA_Q_HEADS = 8
SWA_KV_HEADS = 2
SWA_GROUP = SWA_Q_HEADS // SWA_KV_HEADS
SWA_WINDOW = 128
SWA_BLOCK = 128
MOBA_HEADS = 8
MOBA_BLOCK = 256
MOBA_TOPK = 3
NUM_BUCKETS = 32
MAX_EXACT = NUM_BUCKETS // 2
MAX_DISTANCE = 2048
N_ATTN_HEADS = SWA_Q_HEADS + MOBA_HEADS
SWA_Q_W = SWA_Q_HEADS * HEAD_DIM
SWA_KV_W = SWA_KV_HEADS * HEAD_DIM
MOBA_W = MOBA_HEADS * HEAD_DIM
D_FF = 4 * D_MODEL
N_MOD = 6
RMS_EPS = 1e-6

VMEM_LIMIT_BYTES = 56 * 1024 * 1024
TOKEN_TILE = 512
FF_CHUNK = 1024
BIAS_SLOTS = 8
SWA_STEP_BLOCKS = 4
ACC_ROWS = HEAD_DIM + 16
NEG_INF = float("-inf")

F32 = jnp.float32
BF16 = jnp.bfloat16


def _resident(block_shape, index_map):
    return pl.BlockSpec(block_shape, index_map, pipeline_mode=pl.Buffered(1))


def _dot(a, b):
    return jnp.dot(a, b, preferred_element_type=F32)


def _dot_nt(a, b):
    return lax.dot_general(a, b, (((1,), (1,)), ((), ())), preferred_element_type=F32)


def _rms_modulate(xv, g, shift, scale):
    ms = jnp.mean(xv * xv, axis=-1, keepdims=True)
    y = xv * lax.rsqrt(ms + RMS_EPS)
    return (y * g) * (1.0 + scale) + shift


def _toeplitz(u, rows):
    wide = jnp.broadcast_to(u, (rows, 2 * rows))
    return pltpu.roll(wide, 0, 1, stride=1, stride_axis=0)[:, rows:]


def _ada_kernel(ct_ref, w_ref, b_ref, o_ref, *, batch):
    cs_t = jax.nn.silu(ct_ref[...])
    w = w_ref[...]
    rows = [jnp.sum(w * cs_t[:, b:b + 1], axis=0, keepdims=True) for b in range(batch)]
    rows.append(jnp.zeros((8 - batch, w.shape[1]), F32))
    o_ref[...] = jnp.concatenate(rows, axis=0) + b_ref[...]


def _ada(c_t, w, b, batch):
    n = w.shape[1]
    bn = 1536
    return pl.pallas_call(
        functools.partial(_ada_kernel, batch=batch),
        grid=(n // bn,),
        in_specs=[pl.BlockSpec((D_MODEL, 8), lambda j: (0, 0)),
                  pl.BlockSpec((D_MODEL, bn), lambda j: (0, j)),
                  pl.BlockSpec((1, bn), lambda j: (0, j))],
        out_specs=pl.BlockSpec((8, bn), lambda j: (0, j)),
        out_shape=jax.ShapeDtypeStruct((8, n), F32),
        compiler_params=pltpu.CompilerParams(dimension_semantics=("arbitrary",),
                                             vmem_limit_bytes=VMEM_LIMIT_BYTES),
        name="ada",
    )(c_t, w, b)


def _t5_bucket(dist):
    n = jnp.maximum(dist, 0)
    nf = jnp.maximum(n, 1).astype(F32)
    large = MAX_EXACT + (jnp.log(nf / MAX_EXACT) / math.log(MAX_DISTANCE / MAX_EXACT)
                         * (NUM_BUCKETS - MAX_EXACT)).astype(jnp.int32)
    large = jnp.minimum(large, NUM_BUCKETS - 1)
    return jnp.where(n < MAX_EXACT, n, large)


def _bias_rows_kernel(bucket_ref, rbt_ref, o_ref):
    bucket = bucket_ref[...]
    n = bucket.shape[1]
    onehot = jnp.where(lax.broadcasted_iota(jnp.int32, (NUM_BUCKETS, n), 0) == bucket, 1.0, 0.0)
    rows = jnp.dot(rbt_ref[...], onehot, preferred_element_type=F32,
                   precision=lax.Precision.HIGHEST)
    rows = jnp.where(bucket >= 0, rows * LOG2E, NEG_INF)
    for d in range(BIAS_SLOTS):
        o_ref[d] = rows[:, d * MOBA_BLOCK:(d + 2) * MOBA_BLOCK]


def _bias_rows(rel_bias):
    n = (BIAS_SLOTS + 1) * MOBA_BLOCK
    dist = jnp.arange(n, dtype=jnp.int32) - MOBA_BLOCK
    bucket = jnp.where(dist >= 0, _t5_bucket(dist), -1)[None, :]
    return pl.pallas_call(
        _bias_rows_kernel,
        out_shape=jax.ShapeDtypeStruct((BIAS_SLOTS, N_ATTN_HEADS, 2 * MOBA_BLOCK), F32),
        name="bias_rows",
    )(bucket, rel_bias.astype(F32).T)


IN_WIDTHS = (SWA_Q_W, SWA_KV_W, SWA_KV_W, MOBA_W, MOBA_W, MOBA_W, D_MODEL, D_MODEL)
IN_OFFS = tuple(sum(IN_WIDTHS[:k]) for k in range(len(IN_WIDTHS) + 1))
FEAT_COLS = (0, 2, 3, 5)
FEAT_W = sum(IN_WIDTHS[k] for k in FEAT_COLS)


def _proj_kernel(x_ref, shift_ref, scale_ref, g_ref, w_ref,
                 ka_ref, kb_ref, ga_ref, gb_ref, qa_ref, va_ref, qb_ref, vb_ref, wt_ref):
    @pl.when(pl.program_id(0) == 0)
    def _():
        o = 0
        for k in FEAT_COLS:
            wt_ref[o:o + IN_WIDTHS[k], :] = w_ref[:, IN_OFFS[k]:IN_OFFS[k + 1]].T
            o += IN_WIDTHS[k]

    h = _rms_modulate(x_ref[...], g_ref[...], shift_ref[0, 0], scale_ref[0, 0]).astype(BF16)
    for ref, k in ((ka_ref, 1), (kb_ref, 4), (ga_ref, 6), (gb_ref, 7)):
        ref[...] = _dot(h, w_ref[:, IN_OFFS[k]:IN_OFFS[k + 1]]).astype(ref.dtype)
    o = 0
    for ref, k in zip((qa_ref, va_ref, qb_ref, vb_ref), FEAT_COLS):
        w = IN_WIDTHS[k]
        ft = _dot_nt(wt_ref[o:o + w, :], h).astype(ref.dtype)
        blk = ref.shape[-1]
        for t in range(TOKEN_TILE // blk):
            ref[0, t] = ft[:, t * blk:(t + 1) * blk]
        o += w


def _proj(x2, mod4, g1, w_bf16, batch, seq):
    tokens = x2.shape[0]
    tiles_per_seq = seq // TOKEN_TILE
    na = TOKEN_TILE // SWA_BLOCK
    nb = TOKEN_TILE // MOBA_BLOCK

    def feat_spec(width, blk, per_tile):
        return pl.BlockSpec((1, per_tile, width, blk),
                            lambda i: (i // tiles_per_seq, i % tiles_per_seq, 0, 0))

    out_shape = (
        jax.ShapeDtypeStruct((tokens, SWA_KV_W), BF16),
        jax.ShapeDtypeStruct((tokens, MOBA_W), BF16),
        jax.ShapeDtypeStruct((tokens, D_MODEL), BF16),
        jax.ShapeDtypeStruct((tokens, D_MODEL), BF16),
        jax.ShapeDtypeStruct((batch, seq // SWA_BLOCK, SWA_Q_W, SWA_BLOCK), BF16),
        jax.ShapeDtypeStruct((batch, seq // SWA_BLOCK, SWA_KV_W, SWA_BLOCK), BF16),
        jax.ShapeDtypeStruct((batch, seq // MOBA_BLOCK, MOBA_W, MOBA_BLOCK), BF16),
        jax.ShapeDtypeStruct((batch, seq // MOBA_BLOCK, MOBA_W, MOBA_BLOCK), BF16),
    )
    tok_spec = lambda w: pl.BlockSpec((TOKEN_TILE, w), lambda i: (i, 0))
    return pl.pallas_call(
        _proj_kernel,
        grid=(tokens // TOKEN_TILE,),
        in_specs=[
            pl.BlockSpec((TOKEN_TILE, D_MODEL), lambda i: (i, 0)),
            pl.BlockSpec((1, 1, 1, D_MODEL), lambda i: (i // tiles_per_seq, 0, 0, 0)),
            pl.BlockSpec((1, 1, 1, D_MODEL), lambda i: (i // tiles_per_seq, 1, 0, 0)),
            _resident((1, D_MODEL), lambda i: (0, 0)),
            _resident(w_bf16.shape, lambda i: (0, 0)),
        ],
        out_specs=(tok_spec(SWA_KV_W), tok_spec(MOBA_W), tok_spec(D_MODEL), tok_spec(D_MODEL),
                   feat_spec(SWA_Q_W, SWA_BLOCK, na), feat_spec(SWA_KV_W, SWA_BLOCK, na),
                   feat_spec(MOBA_W, MOBA_BLOCK, nb), feat_spec(MOBA_W, MOBA_BLOCK, nb)),
        out_shape=out_shape,
        scratch_shapes=[pltpu.VMEM((FEAT_W, D_MODEL), BF16)],
        compiler_params=pltpu.CompilerParams(dimension_semantics=("arbitrary",),
                                             vmem_limit_bytes=VMEM_LIMIT_BYTES),
        name="proj",
    )(x2, mod4, mod4, g1, w_bf16)


def _swa_kernel(q_ref, kp_ref, kc_ref, vp_ref, vc_ref, rows_ref, sink_ref, o_ref, bias_ref, ot_ref, s_ref):
    n = pl.program_id(1)
    L = SWA_BLOCK

    @pl.when(jnp.logical_and(pl.program_id(0) == 0, n == 0))
    def _():
        k_idx = lax.broadcasted_iota(jnp.int32, (L, L), 0)
        q_idx = lax.broadcasted_iota(jnp.int32, (L, L), 1)
        for h in range(SWA_Q_HEADS):
            g, gi = divmod(h, SWA_GROUP)
            t0 = rows_ref[0, h:h + 1, MOBA_BLOCK:MOBA_BLOCK + L]
            r = _toeplitz(jnp.concatenate([t0, t0], axis=1), L)
            cols = slice(gi * L, (gi + 1) * L)
            bias_ref[g, 0:L, cols] = jnp.where(q_idx < k_idx, r, NEG_INF)
            bias_ref[g, L:2 * L, cols] = jnp.where(q_idx >= k_idx, r, NEG_INF)

    first_rows = jnp.where(n == 0, L, 0)
    key_row = lax.broadcasted_iota(jnp.int32, (2 * L, SWA_GROUP * L), 0)
    ones_blk = jnp.concatenate([jnp.ones((1, 2 * L), BF16), jnp.zeros((ACC_ROWS - HEAD_DIM - 1, 2 * L), BF16)],
                               axis=0)
    chains = [(c, g) for c in range(SWA_STEP_BLOCKS) for g in range(SWA_KV_HEADS)]

    def scores(idx):
        c, g = chains[idx]
        hs = slice(g * HEAD_DIM, (g + 1) * HEAD_DIM)
        if c == 0:
            kw = jnp.concatenate([kp_ref[0, :, hs], kc_ref[0, 0:L, hs]], axis=0)
        else:
            kw = kc_ref[0, (c - 1) * L:(c + 1) * L, hs]
        qg = jnp.concatenate(
            [q_ref[0, c, (g * SWA_GROUP + gi) * HEAD_DIM:(g * SWA_GROUP + gi + 1) * HEAD_DIM, :]
             for gi in range(SWA_GROUP)], axis=1)
        s = _dot(kw, qg) + bias_ref[g]
        if c == 0:
            s = jnp.where(key_row < first_rows, NEG_INF, s)
        s_ref[idx] = s

    def attend(idx):
        c, g = chains[idx]
        hs = slice(g * HEAD_DIM, (g + 1) * HEAD_DIM)
        s = s_ref[idx]
        sink = sink_ref[g]
        m = jnp.maximum(jnp.max(s, axis=0, keepdims=True), sink)
        p = jnp.exp2(s - m).astype(BF16)
        v_prev = vp_ref[0, 0, hs, :] if c == 0 else vc_ref[0, c - 1, hs, :]
        v_aug = jnp.concatenate([jnp.concatenate([v_prev, vc_ref[0, c, hs, :]], axis=1), ones_blk],
                                axis=0)
        pv = _dot(v_aug, p)
        o = pv[0:HEAD_DIM] / (pv[HEAD_DIM:HEAD_DIM + 1] + jnp.exp2(sink - m))
        for gi in range(SWA_GROUP):
            r = (g * SWA_GROUP + gi) * HEAD_DIM
            ot_ref[c, r:r + HEAD_DIM, :] = o[:, gi * L:(gi + 1) * L]

    scores(0)
    for idx in range(len(chains)):
        if idx + 1 < len(chains):
            scores(idx + 1)
        attend(idx)
    for c in range(SWA_STEP_BLOCKS):
        o_ref[0, c * L:(c + 1) * L, :] = ot_ref[c].T.astype(o_ref.dtype)


def _swa(qa_t, ka3, va_t, rows, sink_rows):
    batch, nblk = qa_t.shape[0], qa_t.shape[1]
    seq = nblk * SWA_BLOCK
    nb = SWA_STEP_BLOCKS
    assert nblk % nb == 0
    prev = lambda n: jnp.maximum(n * nb - 1, 0)
    return pl.pallas_call(
        _swa_kernel,
        grid=(batch, nblk // nb),
        in_specs=[
            pl.BlockSpec((1, nb, SWA_Q_W, SWA_BLOCK), lambda b, n: (b, n, 0, 0)),
            pl.BlockSpec((1, SWA_BLOCK, SWA_KV_W), lambda b, n: (b, prev(n), 0)),
            pl.BlockSpec((1, nb * SWA_BLOCK, SWA_KV_W), lambda b, n: (b, n, 0)),
            pl.BlockSpec((1, 1, SWA_KV_W, SWA_BLOCK), lambda b, n: (b, prev(n), 0, 0)),
            pl.BlockSpec((1, nb, SWA_KV_W, SWA_BLOCK), lambda b, n: (b, n, 0, 0)),
            _resident((1, N_ATTN_HEADS, 2 * MOBA_BLOCK), lambda b, n: (0, 0, 0)),
            _resident(sink_rows.shape, lambda b, n: (0, 0, 0)),
        ],
        out_specs=pl.BlockSpec((1, nb * SWA_BLOCK, SWA_Q_W), lambda b, n: (b, n, 0)),
        out_shape=jax.ShapeDtypeStruct((batch, seq, SWA_Q_W), BF16),
        scratch_shapes=[pltpu.VMEM((SWA_KV_HEADS, 2 * SWA_BLOCK, SWA_GROUP * SWA_BLOCK), F32),
                        pltpu.VMEM((nb, SWA_Q_W, SWA_BLOCK), F32),
                        pltpu.VMEM((nb * SWA_KV_HEADS, 2 * SWA_BLOCK, SWA_GROUP * SWA_BLOCK), F32)],
        compiler_params=pltpu.CompilerParams(dimension_semantics=("arbitrary", "arbitrary"),
                                             vmem_limit_bytes=VMEM_LIMIT_BYTES),
        name="swa",
    )(qa_t, ka3, ka3, va_t, va_t, rows, sink_rows)


def _select_kernel(k_ref, q_ref, o_ref):
    seq = k_ref.shape[1]
    nblk = seq // MOBA_BLOCK
    blk = lax.broadcasted_iota(jnp.int32, (nblk, seq), 0)
    qblk = lax.broadcasted_iota(jnp.int32, (nblk, seq), 1) // MOBA_BLOCK
    past = blk < qblk
    member = jnp.where(blk == qblk, 1.0 / MOBA_BLOCK, 0.0).astype(BF16)
    for hh in range(2):
        hs = slice(hh * HEAD_DIM, (hh + 1) * HEAD_DIM)
        kmean = _dot(member, k_ref[0, :, hs])
        q_t = jnp.concatenate([q_ref[0, t, hs, :] for t in range(nblk)], axis=1)
        k1 = kmean.astype(BF16)
        r1 = kmean - k1.astype(F32)
        k2 = r1.astype(BF16)
        k3 = (r1 - k2.astype(F32)).astype(BF16)
        parts = _dot(jnp.concatenate([k1, k2, k3], axis=0), q_t)
        gate = parts[0:nblk] + parts[nblk:2 * nblk] + parts[2 * nblk:3 * nblk]
        gate = jnp.where(past, gate, NEG_INF)
        rank = jnp.zeros((nblk, seq), jnp.int32)
        for r in range(nblk):
            row = gate[r:r + 1, :]
            rank = rank + jnp.where(blk > r, jnp.where(row >= gate, 1, 0),
                                    jnp.where(row > gate, 1, 0))
        sel = jnp.logical_and(past, rank < MOBA_TOPK)
        o_ref[0, hh] = jnp.where(jnp.logical_or(sel, blk == qblk), 0.0, NEG_INF)


def _select(kb3, qb_t):
    batch, seq = kb3.shape[0], kb3.shape[1]
    nblk = seq // MOBA_BLOCK
    pairs = MOBA_HEADS // 2
    return pl.pallas_call(
        _select_kernel,
        grid=(batch, pairs),
        in_specs=[
            pl.BlockSpec((1, seq, 2 * HEAD_DIM), lambda b, hp: (b, 0, hp)),
            pl.BlockSpec((1, nblk, 2 * HEAD_DIM, MOBA_BLOCK), lambda b, hp: (b, 0, hp, 0)),
        ],
        out_specs=pl.BlockSpec((1, 2, nblk, seq), lambda b, hp: (b, hp, 0, 0)),
        out_shape=jax.ShapeDtypeStruct((batch, MOBA_HEADS, nblk, seq), F32),
        compiler_params=pltpu.CompilerParams(dimension_semantics=("arbitrary", "arbitrary"),
                                             vmem_limit_bytes=VMEM_LIMIT_BYTES),
        name="moba_select",
    )(kb3, qb_t)


def _moba_kernel(q_ref, qn_ref, k_ref, v_ref, sel_ref, rows_ref, o_ref, bias_ref, m_ref, acc_ref, s_ref, cm_ref):
    i = pl.program_id(1)
    nblk = pl.num_programs(1)
    MB = MOBA_BLOCK

    @pl.when(jnp.logical_and(pl.program_id(0) == 0, i == 0))
    def _():
        for h in range(MOBA_HEADS):
            for d in range(BIAS_SLOTS):
                bias_ref[h, d] = _toeplitz(rows_ref[d, SWA_Q_HEADS + h:SWA_Q_HEADS + h + 1, :], MB)

    m_ref[...] = jnp.full(m_ref.shape, NEG_INF, F32)
    acc_ref[...] = jnp.zeros(acc_ref.shape, F32)
    heads = [slice(h * HEAD_DIM, (h + 1) * HEAD_DIM) for h in range(MOBA_HEADS)]
    ones_blk = jnp.concatenate([jnp.ones((1, MB), BF16), jnp.zeros((ACC_ROWS - HEAD_DIM - 1, MB), BF16)],
                               axis=0)

    def scores(j, slot, h, hs):
        rows = pl.ds(pl.multiple_of(j * MB, MB), MB)
        s = _dot(k_ref[0, rows, hs], q_ref[0, 0, hs, :]) + bias_ref[h, slot]
        s_ref[h] = s
        cm_ref[h] = jnp.max(s, axis=0, keepdims=True) + sel_ref[0, h, pl.ds(j, 1), :]

    def own_scores(blk, qr, h, hs):
        rows = pl.ds(pl.multiple_of(blk * MB, MB), MB)
        s = _dot(k_ref[0, rows, hs], qr[0, 0, hs, :]) + bias_ref[h, 0]
        s_ref[h] = s
        cm_ref[h] = jnp.max(s, axis=0, keepdims=True)

    def accumulate(j, h, hs):
        m_old = m_ref[h]
        m_new = jnp.maximum(m_old, cm_ref[h])
        alpha = jnp.exp2(m_old - m_new)
        p = jnp.exp2(s_ref[h] - (m_new - sel_ref[0, h, pl.ds(j, 1), :])).astype(BF16)
        v_aug = jnp.concatenate([v_ref[0, j, hs, :], ones_blk], axis=0)
        rs = slice(h * ACC_ROWS, (h + 1) * ACC_ROWS)
        acc_ref[rs, :] = alpha * acc_ref[rs, :] + _dot(v_aug, p)
        m_ref[h] = m_new

    @pl.when(i == 0)
    def _():
        for h, hs in enumerate(heads):
            own_scores(i, q_ref, h, hs)

    def body(t, carry):
        j = i - t
        slot = jnp.minimum(t + 1, BIAS_SLOTS - 1)
        for h, hs in enumerate(heads):
            accumulate(j, h, hs)
            scores(j - 1, slot, h, hs)
        return carry

    def body2(u, carry):
        return body(2 * u + 1, body(2 * u, carry))

    lax.fori_loop(0, i // 2, body2, 0)

    @pl.when(i % 2 == 1)
    def _():
        body(i - 1, 0)

    @pl.when(i < nblk - 1)
    def _():
        for h, hs in enumerate(heads):
            accumulate(0, h, hs)
            own_scores(i + 1, qn_ref, h, hs)

    @pl.when(i == nblk - 1)
    def _():
        for h, hs in enumerate(heads):
            accumulate(0, h, hs)

    outs = []
    for h, hs in enumerate(heads):
        base = h * ACC_ROWS
        outs.append(acc_ref[base:base + HEAD_DIM, :] / acc_ref[base + HEAD_DIM:base + HEAD_DIM + 1, :])
    o_ref[0] = jnp.concatenate(outs, axis=0).T.astype(o_ref.dtype)


def _moba(qb_t, kb3, vb_t, sel, rows):
    batch, nblk = qb_t.shape[0], qb_t.shape[1]
    seq = nblk * MOBA_BLOCK
    return pl.pallas_call(
        _moba_kernel,
        grid=(batch, nblk),
        in_specs=[
            pl.BlockSpec((1, 1, MOBA_W, MOBA_BLOCK), lambda b, i: (b, i, 0, 0)),
            pl.BlockSpec((1, 1, MOBA_W, MOBA_BLOCK), lambda b, i: (b, jnp.minimum(i + 1, nblk - 1), 0, 0)),
            pl.BlockSpec((1, seq, MOBA_W), lambda b, i: (b, 0, 0)),
            pl.BlockSpec((1, nblk, MOBA_W, MOBA_BLOCK), lambda b, i: (b, 0, 0, 0)),
            pl.BlockSpec((1, MOBA_HEADS, nblk, MOBA_BLOCK), lambda b, i: (b, 0, 0, i)),
            _resident(rows.shape, lambda b, i: (0, 0, 0)),
        ],
        out_specs=pl.BlockSpec((1, MOBA_BLOCK, MOBA_W), lambda b, i: (b, i, 0)),
        out_shape=jax.ShapeDtypeStruct((batch, seq, MOBA_W), BF16),
        scratch_shapes=[pltpu.VMEM((MOBA_HEADS, BIAS_SLOTS, MOBA_BLOCK, MOBA_BLOCK), F32),
                        pltpu.VMEM((MOBA_HEADS, 1, MOBA_BLOCK), F32),
                        pltpu.VMEM((MOBA_HEADS * ACC_ROWS, MOBA_BLOCK), F32),
                        pltpu.VMEM((MOBA_HEADS, MOBA_BLOCK, MOBA_BLOCK), F32),
                        pltpu.VMEM((MOBA_HEADS, 1, MOBA_BLOCK), F32)],
        compiler_params=pltpu.CompilerParams(dimension_semantics=("arbitrary", "arbitrary"),
                                             vmem_limit_bytes=VMEM_LIMIT_BYTES),
        name="moba",
    )(qb_t, qb_t, kb3, vb_t, sel, rows)


def _post_kernel(x_ref, ya_ref, yb_ref, ga_ref, gb_ref, gate1_ref, shift2_ref, scale2_ref,
                 gate2_ref, g2_ref, gf_ref, wa_ref, wb_ref, wo_ref, w1_ref, w2_ref, o_ref):
    a = _dot(ya_ref[...], wa_ref[...])
    b = _dot(yb_ref[...], wb_ref[...])
    merged = (jax.nn.sigmoid(ga_ref[...].astype(F32)) * a
              + jax.nn.sigmoid(gb_ref[...].astype(F32)) * b)
    x1 = x_ref[...] + gate1_ref[0, 0] * _dot(merged.astype(BF16), wo_ref[...])
    h2 = _rms_modulate(x1, g2_ref[...], shift2_ref[0, 0], scale2_ref[0, 0]).astype(BF16)
    y = jnp.zeros_like(x1)
    for c in range(D_FF // FF_CHUNK):
        cs = slice(c * FF_CHUNK, (c + 1) * FF_CHUNK)
        u = jnp.square(jnp.maximum(_dot(h2, w1_ref[:, cs]), 0.0)).astype(BF16)
        y = y + _dot(u, w2_ref[cs, :])
    x2 = x1 + gate2_ref[0, 0] * y
    ms = jnp.mean(x2 * x2, axis=-1, keepdims=True)
    o_ref[...] = (x2 * lax.rsqrt(ms + RMS_EPS)) * gf_ref[...]


def _post(x2, ya, yb, ga, gb, mod4, g2, gf, wa, wb, wo, w1, w2, seq):
    tokens = x2.shape[0]
    tiles_per_seq = seq // TOKEN_TILE
    tok = lambda w: pl.BlockSpec((TOKEN_TILE, w), lambda i: (i, 0))
    modrow = lambda k: pl.BlockSpec((1, 1, 1, D_MODEL), lambda i: (i // tiles_per_seq, k, 0, 0))
    full = lambda a: _resident(a.shape, lambda i: (0, 0))
    return pl.pallas_call(
        _post_kernel,
        grid=(tokens // TOKEN_TILE,),
        in_specs=[tok(D_MODEL), tok(SWA_Q_W), tok(MOBA_W), tok(D_MODEL), tok(D_MODEL),
                  modrow(2), modrow(3), modrow(4), modrow(5),
                  full(g2), full(gf), full(wa), full(wb), full(wo), full(w1), full(w2)],
        out_specs=tok(D_MODEL),
        out_shape=jax.ShapeDtypeStruct((tokens, D_MODEL), F32),
        compiler_params=pltpu.CompilerParams(dimension_semantics=("arbitrary",),
                                             vmem_limit_bytes=VMEM_LIMIT_BYTES),
        name="post",
    )(x2, ya, yb, ga, gb, mod4, mod4, mod4, mod4, g2, gf, wa, wb, wo, w1, w2)


def kernel(x, c, ada_w, ada_b, norm1_g, norm2_g, w_in, attn_sinks, rel_bias, w_branch_a,
           w_branch_b, w_out, w_mlp_in, w_mlp_out, final_g):
    batch, seq, _ = x.shape
    depth = ada_w.shape[0]
    nblk_b = seq // MOBA_BLOCK
    assert seq % TOKEN_TILE == 0 and TOKEN_TILE % MOBA_BLOCK == 0 and batch <= 8
    assert (BIAS_SLOTS - 1) * MOBA_BLOCK - (MOBA_BLOCK - 1) >= 1513 and MAX_DISTANCE == 2048
    assert 2 * SWA_BLOCK <= MOBA_BLOCK and SWA_WINDOW == SWA_BLOCK
    assert depth == 1
    l = 0

    rows = _bias_rows(rel_bias)
    c_t = jnp.zeros((D_MODEL, 8), F32).at[:, :batch].set(c.astype(F32).T)

    x2 = x.reshape(batch * seq, D_MODEL)
    mod = _ada(c_t, ada_w[l], ada_b[l][None, :], batch)
    mod4 = mod[:batch].reshape(batch, N_MOD, 1, D_MODEL)
    col_scale = jnp.ones((sum(IN_WIDTHS),), F32)
    for k in (0, 3):
        col_scale = col_scale.at[IN_OFFS[k]:IN_OFFS[k + 1]].set(ATTN_SCALE * LOG2E)
    w_bf16 = (w_in[l] * col_scale[None, :]).astype(BF16)
    ka, kb, ga, gb, qa_t, va_t, qb_t, vb_t = _proj(
        x2, mod4, norm1_g[l][None, :], w_bf16, batch, seq)

    sink_rows = jnp.broadcast_to(
        (attn_sinks[l].astype(F32) * LOG2E).reshape(SWA_KV_HEADS, 1, SWA_GROUP, 1),
        (SWA_KV_HEADS, 1, SWA_GROUP, SWA_BLOCK)).reshape(SWA_KV_HEADS, 1, SWA_GROUP * SWA_BLOCK)
    ya = _swa(qa_t, ka.reshape(batch, seq, SWA_KV_W), va_t, rows, sink_rows)

    kb3 = kb.reshape(batch, seq, MOBA_W)
    sel = _select(kb3, qb_t)
    yb = _moba(qb_t, kb3, vb_t, sel, rows)

    out = _post(x2, ya.reshape(batch * seq, SWA_Q_W), yb.reshape(batch * seq, MOBA_W), ga, gb,
                mod4, norm2_g[l][None, :], final_g[None, :],
                w_branch_a[l].astype(BF16), w_branch_b[l].astype(BF16), w_out[l].astype(BF16),
                w_mlp_in[l].astype(BF16), w_mlp_out[l].astype(BF16), seq)
    return out.reshape(batch, seq, D_MODEL)
```

```python
import functools
import math

import jax
import jax.numpy as jnp
from jax import lax
from jax.experimental import pallas as pl
from jax.experimental.pallas import tpu as pltpu

D_MODEL = 1024
HEAD_DIM = 64
ATTN_SCALE = HEAD_DIM ** -0.5
LOG2E = math.log2(math.e)
SWA_Q_HEADS = 8
SWA_KV_HEADS = 2
SWA_GROUP = SWA_Q_HEADS // SWA_KV_HEADS
SWA_WINDOW = 128
SWA_BLOCK = 128
MOBA_HEADS = 8
MOBA_BLOCK = 256
MOBA_TOPK = 3
NUM_BUCKETS = 32
MAX_EXACT = NUM_BUCKETS // 2
MAX_DISTANCE = 2048
N_ATTN_HEADS = SWA_Q_HEADS + MOBA_HEADS
SWA_Q_W = SWA_Q_HEADS * HEAD_DIM
SWA_KV_W = SWA_KV_HEADS * HEAD_DIM
MOBA_W = MOBA_HEADS * HEAD_DIM
D_FF = 4 * D_MODEL
N_MOD = 6
RMS_EPS = 1e-6

VMEM_LIMIT_BYTES = 56 * 1024 * 1024
TOKEN_TILE = 512
FF_CHUNK = 1024
BIAS_SLOTS = 8
SWA_UNROLL = 3
ACC_ROWS = HEAD_DIM + 16
NEG_INF = float("-inf")

F32 = jnp.float32
BF16 = jnp.bfloat16


def _resident(block_shape, index_map):
    return pl.BlockSpec(block_shape, index_map, pipeline_mode=pl.Buffered(1))


def _dot(a, b):
    return jnp.dot(a, b, preferred_element_type=F32)


def _dot_nt(a, b):
    return lax.dot_general(a, b, (((1,), (1,)), ((), ())), preferred_element_type=F32)


def _rms_modulate(xv, g, shift, scale):
    ms = jnp.mean(xv * xv, axis=-1, keepdims=True)
    y = xv * lax.rsqrt(ms + RMS_EPS)
    return (y * g) * (1.0 + scale) + shift


def _toeplitz(u, rows):
    wide = jnp.broadcast_to(u, (rows, 2 * rows))
    return pltpu.roll(wide, 0, 1, stride=1, stride_axis=0)[:, rows:]


def _ada_kernel(ct_ref, w_ref, b_ref, o_ref, *, batch):
    cs_t = jax.nn.silu(ct_ref[...])
    w = w_ref[...]
    rows = [jnp.sum(w * cs_t[:, b:b + 1], axis=0, keepdims=True) for b in range(batch)]
    rows.append(jnp.zeros((8 - batch, w.shape[1]), F32))
    o_ref[...] = jnp.concatenate(rows, axis=0) + b_ref[...]


def _ada(c_t, w, b, batch):
    n = w.shape[1]
    bn = 1536
    return pl.pallas_call(
        functools.partial(_ada_kernel, batch=batch),
        grid=(n // bn,),
        in_specs=[pl.BlockSpec((D_MODEL, 8), lambda j: (0, 0)),
                  pl.BlockSpec((D_MODEL, bn), lambda j: (0, j)),
                  pl.BlockSpec((1, bn), lambda j: (0, j))],
        out_specs=pl.BlockSpec((8, bn), lambda j: (0, j)),
        out_shape=jax.ShapeDtypeStruct((8, n), F32),
        compiler_params=pltpu.CompilerParams(dimension_semantics=("arbitrary",),
                                             vmem_limit_bytes=VMEM_LIMIT_BYTES),
        name="ada",
    )(c_t, w, b)


def _t5_bucket(dist):
    n = jnp.maximum(dist, 0)
    nf = jnp.maximum(n, 1).astype(F32)
    large = MAX_EXACT + (jnp.log(nf / MAX_EXACT) / math.log(MAX_DISTANCE / MAX_EXACT)
                         * (NUM_BUCKETS - MAX_EXACT)).astype(jnp.int32)
    large = jnp.minimum(large, NUM_BUCKETS - 1)
    return jnp.where(n < MAX_EXACT, n, large)


def _bias_rows_kernel(bucket_ref, rbt_ref, o_ref):
    bucket = bucket_ref[...]
    n = bucket.shape[1]
    onehot = jnp.where(lax.broadcasted_iota(jnp.int32, (NUM_BUCKETS, n), 0) == bucket, 1.0, 0.0)
    rows = jnp.dot(rbt_ref[...], onehot, preferred_element_type=F32,
                   precision=lax.Precision.HIGHEST)
    rows = jnp.where(bucket >= 0, rows * LOG2E, NEG_INF)
    for d in range(BIAS_SLOTS):
        o_ref[d] = rows[:, d * MOBA_BLOCK:(d + 2) * MOBA_BLOCK]


def _bias_rows(rel_bias):
    n = (BIAS_SLOTS + 1) * MOBA_BLOCK
    dist = jnp.arange(n, dtype=jnp.int32) - MOBA_BLOCK
    bucket = jnp.where(dist >= 0, _t5_bucket(dist), -1)[None, :]
    return pl.pallas_call(
        _bias_rows_kernel,
        out_shape=jax.ShapeDtypeStruct((BIAS_SLOTS, N_ATTN_HEADS, 2 * MOBA_BLOCK), F32),
        name="bias_rows",
    )(bucket, rel_bias.astype(F32).T)


IN_WIDTHS = (SWA_Q_W, SWA_KV_W, SWA_KV_W, MOBA_W, MOBA_W, MOBA_W, D_MODEL, D_MODEL)
IN_OFFS = tuple(sum(IN_WIDTHS[:k]) for k in range(len(IN_WIDTHS) + 1))
FEAT_COLS = (0, 2, 3, 5)
FEAT_W = sum(IN_WIDTHS[k] for k in FEAT_COLS)


def _proj_kernel(x_ref, shift_ref, scale_ref, g_ref, w_ref,
                 ka_ref, kb_ref, ga_ref, gb_ref, qa_ref, va_ref, qb_ref, vb_ref, wt_ref):
    @pl.when(pl.program_id(0) == 0)
    def _():
        o = 0
        for k in FEAT_COLS:
            wt_ref[o:o + IN_WIDTHS[k], :] = w_ref[:, IN_OFFS[k]:IN_OFFS[k + 1]].T
            o += IN_WIDTHS[k]

    h = _rms_modulate(x_ref[...], g_ref[...], shift_ref[0, 0], scale_ref[0, 0]).astype(BF16)
    for ref, k in ((ka_ref, 1), (kb_ref, 4), (ga_ref, 6), (gb_ref, 7)):
        ref[...] = _dot(h, w_ref[:, IN_OFFS[k]:IN_OFFS[k + 1]]).astype(ref.dtype)
    o = 0
    for ref, k in zip((qa_ref, va_ref, qb_ref, vb_ref), FEAT_COLS):
        w = IN_WIDTHS[k]
        ft = _dot_nt(wt_ref[o:o + w, :], h).astype(ref.dtype)
        blk = ref.shape[-1]
        for t in range(TOKEN_TILE // blk):
            ref[0, t] = ft[:, t * blk:(t + 1) * blk]
        o += w


def _proj(x2, mod4, g1, w_bf16, batch, seq):
    tokens = x2.shape[0]
    tiles_per_seq = seq // TOKEN_TILE
    na = TOKEN_TILE // SWA_BLOCK
    nb = TOKEN_TILE // MOBA_BLOCK

    def feat_spec(width, blk, per_tile):
        return pl.BlockSpec((1, per_tile, width, blk),
                            lambda i: (i // tiles_per_seq, i % tiles_per_seq, 0, 0))

    out_shape = (
        jax.ShapeDtypeStruct((tokens, SWA_KV_W), BF16),
        jax.ShapeDtypeStruct((tokens, MOBA_W), BF16),
        jax.ShapeDtypeStruct((tokens, D_MODEL), BF16),
        jax.ShapeDtypeStruct((tokens, D_MODEL), BF16),
        jax.ShapeDtypeStruct((batch, seq // SWA_BLOCK, SWA_Q_W, SWA_BLOCK), BF16),
        jax.ShapeDtypeStruct((batch, seq // SWA_BLOCK, SWA_KV_W, SWA_BLOCK), BF16),
        jax.ShapeDtypeStruct((batch, seq // MOBA_BLOCK, MOBA_W, MOBA_BLOCK), BF16),
        jax.ShapeDtypeStruct((batch, seq // MOBA_BLOCK, MOBA_W, MOBA_BLOCK), BF16),
    )
    tok_spec = lambda w: pl.BlockSpec((TOKEN_TILE, w), lambda i: (i, 0))
    return pl.pallas_call(
        _proj_kernel,
        grid=(tokens // TOKEN_TILE,),
        in_specs=[
            pl.BlockSpec((TOKEN_TILE, D_MODEL), lambda i: (i, 0)),
            pl.BlockSpec((1, 1, 1, D_MODEL), lambda i: (i // tiles_per_seq, 0, 0, 0)),
            pl.BlockSpec((1, 1, 1, D_MODEL), lambda i: (i // tiles_per_seq, 1, 0, 0)),
            _resident((1, D_MODEL), lambda i: (0, 0)),
            _resident(w_bf16.shape, lambda i: (0, 0)),
        ],
        out_specs=(tok_spec(SWA_KV_W), tok_spec(MOBA_W), tok_spec(D_MODEL), tok_spec(D_MODEL),
                   feat_spec(SWA_Q_W, SWA_BLOCK, na), feat_spec(SWA_KV_W, SWA_BLOCK, na),
                   feat_spec(MOBA_W, MOBA_BLOCK, nb), feat_spec(MOBA_W, MOBA_BLOCK, nb)),
        out_shape=out_shape,
        scratch_shapes=[pltpu.VMEM((FEAT_W, D_MODEL), BF16)],
        compiler_params=pltpu.CompilerParams(dimension_semantics=("arbitrary",),
                                             vmem_limit_bytes=VMEM_LIMIT_BYTES),
        name="proj",
    )(x2, mod4, mod4, g1, w_bf16)


def _swa_kernel(q_ref, k_ref, v_ref, rows_ref, sink_ref, o_ref, bias_ref, ot_ref, s_ref):
    L = SWA_BLOCK
    nblk = q_ref.shape[1]

    @pl.when(pl.program_id(0) == 0)
    def _():
        k_idx = lax.broadcasted_iota(jnp.int32, (L, L), 0)
        q_idx = lax.broadcasted_iota(jnp.int32, (L, L), 1)
        for h in range(SWA_Q_HEADS):
            g, gi = divmod(h, SWA_GROUP)
            t0 = rows_ref[0, h:h + 1, MOBA_BLOCK:MOBA_BLOCK + L]
            r = _toeplitz(jnp.concatenate([t0, t0], axis=1), L)
            cols = slice(gi * L, (gi + 1) * L)
            bias_ref[g, 0:L, cols] = jnp.where(q_idx < k_idx, r, NEG_INF)
            bias_ref[g, L:2 * L, cols] = jnp.where(q_idx >= k_idx, r, NEG_INF)

    ones_blk = jnp.concatenate([jnp.ones((1, 2 * L), BF16), jnp.zeros((ACC_ROWS - HEAD_DIM - 1, 2 * L), BF16)],
                               axis=0)
    PW = 2 * L
    chains = [(g, hp) for g in range(SWA_KV_HEADS) for hp in range(SWA_GROUP // 2)]
    kv_cols = [slice(g * HEAD_DIM, (g + 1) * HEAD_DIM) for g in range(SWA_KV_HEADS)]

    def q_pair(c, g, hp):
        h0 = g * SWA_GROUP + 2 * hp
        return jnp.concatenate([q_ref[0, c, (h0 + e) * HEAD_DIM:(h0 + e + 1) * HEAD_DIM, :] for e in range(2)],
                               axis=1)

    def scores_first(idx):
        g, hp = chains[idx]
        lanes = slice(hp * PW, (hp + 1) * PW)
        s_ref[idx, 0:L, :] = jnp.full((L, PW), NEG_INF, F32)
        s_ref[idx, L:2 * L, :] = _dot(k_ref[0, 0:L, kv_cols[g]], q_pair(0, g, hp)) + bias_ref[g, L:2 * L, lanes]

    def scores(c, idx):
        g, hp = chains[idx]
        kw = k_ref[0, pl.ds(pl.multiple_of((c - 1) * L, L), 2 * L), kv_cols[g]]
        s_ref[idx] = _dot(kw, q_pair(c, g, hp)) + bias_ref[g, :, hp * PW:(hp + 1) * PW]

    def attend(c, c_prev, idx):
        g, hp = chains[idx]
        hs = kv_cols[g]
        s = s_ref[idx]
        sink = sink_ref[g, :, hp * PW:(hp + 1) * PW]
        m = jnp.maximum(jnp.max(s, axis=0, keepdims=True), sink)
        p = jnp.exp2(s - m).astype(BF16)
        v_win = jnp.concatenate([v_ref[0, c_prev, hs, :], v_ref[0, c, hs, :]], axis=1)
        pv = _dot(jnp.concatenate([v_win, ones_blk], axis=0), p)
        o = pv[0:HEAD_DIM] / (pv[HEAD_DIM:HEAD_DIM + 1] + jnp.exp2(sink - m))
        for e in range(2):
            r = (g * SWA_GROUP + 2 * hp + e) * HEAD_DIM
            ot_ref[r:r + HEAD_DIM, :] = o[:, e * L:(e + 1) * L]

    def emit(c):
        o_ref[0, pl.ds(pl.multiple_of(c * L, L), L), :] = ot_ref[...].T.astype(o_ref.dtype)

    for idx in range(len(chains)):
        scores_first(idx)
    for idx in range(len(chains)):
        attend(0, 0, idx)
        scores(1, idx)
    emit(0)

    def body(c, carry):
        for idx in range(len(chains)):
            attend(c, c - 1, idx)
            scores(c + 1, idx)
        emit(c)
        return carry

    def body_unrolled(u, carry):
        for e in range(SWA_UNROLL):
            carry = body(SWA_UNROLL * u + 1 + e, carry)
        return carry

    lax.fori_loop(0, (nblk - 2) // SWA_UNROLL, body_unrolled, 0)
    for idx in range(len(chains)):
        attend(nblk - 1, nblk - 2, idx)
    emit(nblk - 1)


def _swa(qa_t, ka3, va_t, rows, sink_rows):
    batch, nblk = qa_t.shape[0], qa_t.shape[1]
    seq = nblk * SWA_BLOCK
    assert nblk >= 2 and (nblk - 2) % SWA_UNROLL == 0
    return pl.pallas_call(
        _swa_kernel,
        grid=(batch,),
        in_specs=[
            pl.BlockSpec((1, nblk, SWA_Q_W, SWA_BLOCK), lambda b: (b, 0, 0, 0)),
            pl.BlockSpec((1, seq, SWA_KV_W), lambda b: (b, 0, 0)),
            pl.BlockSpec((1, nblk, SWA_KV_W, SWA_BLOCK), lambda b: (b, 0, 0, 0)),
            _resident((1, N_ATTN_HEADS, 2 * MOBA_BLOCK), lambda b: (0, 0, 0)),
            _resident(sink_rows.shape, lambda b: (0, 0, 0)),
        ],
        out_specs=pl.BlockSpec((1, seq, SWA_Q_W), lambda b: (b, 0, 0)),
        out_shape=jax.ShapeDtypeStruct((batch, seq, SWA_Q_W), BF16),
        scratch_shapes=[pltpu.VMEM((SWA_KV_HEADS, 2 * SWA_BLOCK, SWA_GROUP * SWA_BLOCK), F32),
                        pltpu.VMEM((SWA_Q_W, SWA_BLOCK), F32),
                        pltpu.VMEM((SWA_Q_HEADS // 2, 2 * SWA_BLOCK, 2 * SWA_BLOCK), F32)],
        compiler_params=pltpu.CompilerParams(dimension_semantics=("arbitrary",),
                                             vmem_limit_bytes=VMEM_LIMIT_BYTES),
        name="swa",
    )(qa_t, ka3, va_t, rows, sink_rows)


def _select_kernel(k_ref, q_ref, o_ref):
    seq = k_ref.shape[1]
    nblk = seq // MOBA_BLOCK
    blk = lax.broadcasted_iota(jnp.int32, (nblk, seq), 0)
    qblk = lax.broadcasted_iota(jnp.int32, (nblk, seq), 1) // MOBA_BLOCK
    past = blk < qblk
    member = jnp.where(blk == qblk, 1.0 / MOBA_BLOCK, 0.0).astype(BF16)
    for hh in range(2):
        hs = slice(hh * HEAD_DIM, (hh + 1) * HEAD_DIM)
        kmean = _dot(member, k_ref[0, :, hs])
        q_t = jnp.concatenate([q_ref[0, t, hs, :] for t in range(nblk)], axis=1)
        k1 = kmean.astype(BF16)
        r1 = kmean - k1.astype(F32)
        k2 = r1.astype(BF16)
        k3 = (r1 - k2.astype(F32)).astype(BF16)
        parts = _dot(jnp.concatenate([k1, k2, k3], axis=0), q_t)
        gate = parts[0:nblk] + parts[nblk:2 * nblk] + parts[2 * nblk:3 * nblk]
        gate = jnp.where(past, gate, NEG_INF)
        rank = jnp.zeros((nblk, seq), jnp.int32)
        for r in range(nblk):
            row = gate[r:r + 1, :]
            rank = rank + jnp.where(blk > r, jnp.where(row >= gate, 1, 0),
                                    jnp.where(row > gate, 1, 0))
        sel = jnp.logical_and(past, rank < MOBA_TOPK)
        o_ref[0, hh] = jnp.where(jnp.logical_or(sel, blk == qblk), 0.0, NEG_INF)


def _select(kb3, qb_t):
    batch, seq = kb3.shape[0], kb3.shape[1]
    nblk = seq // MOBA_BLOCK
    pairs = MOBA_HEADS // 2
    return pl.pallas_call(
        _select_kernel,
        grid=(batch, pairs),
        in_specs=[
            pl.BlockSpec((1, seq, 2 * HEAD_DIM), lambda b, hp: (b, 0, hp)),
            pl.BlockSpec((1, nblk, 2 * HEAD_DIM, MOBA_BLOCK), lambda b, hp: (b, 0, hp, 0)),
        ],
        out_specs=pl.BlockSpec((1, 2, nblk, seq), lambda b, hp: (b, hp, 0, 0)),
        out_shape=jax.ShapeDtypeStruct((batch, MOBA_HEADS, nblk, seq), F32),
        compiler_params=pltpu.CompilerParams(dimension_semantics=("arbitrary", "arbitrary"),
                                             vmem_limit_bytes=VMEM_LIMIT_BYTES),
        name="moba_select",
    )(kb3, qb_t)


def _moba_kernel(q_ref, qn_ref, k_ref, v_ref, sel_ref, rows_ref, o_ref, bias_ref, m_ref, acc_ref, s_ref, cm_ref):
    i = pl.program_id(1)
    nblk = pl.num_programs(1)
    MB = MOBA_BLOCK

    @pl.when(jnp.logical_and(pl.program_id(0) == 0, i == 0))
    def _():
        for h in range(MOBA_HEADS):
            for d in range(BIAS_SLOTS):
                bias_ref[h, d] = _toeplitz(rows_ref[d, SWA_Q_HEADS + h:SWA_Q_HEADS + h + 1, :], MB)

    m_ref[...] = jnp.full(m_ref.shape, NEG_INF, F32)
    acc_ref[...] = jnp.zeros(acc_ref.shape, F32)
    heads = [slice(h * HEAD_DIM, (h + 1) * HEAD_DIM) for h in range(MOBA_HEADS)]
    ones_blk = jnp.concatenate([jnp.ones((1, MB), BF16), jnp.zeros((ACC_ROWS - HEAD_DIM - 1, MB), BF16)],
                               axis=0)

    def scores(j, slot, h, hs):
        rows = pl.ds(pl.multiple_of(j * MB, MB), MB)
        s = _dot(k_ref[0, rows, hs], q_ref[0, 0, hs, :]) + bias_ref[h, slot]
        s_ref[h] = s
        cm_ref[h] = jnp.max(s, axis=0, keepdims=True) + sel_ref[0, h, pl.ds(j, 1), :]

    def own_scores(blk, qr, h, hs):
        rows = pl.ds(pl.multiple_of(blk * MB, MB), MB)
        s = _dot(k_ref[0, rows, hs], qr[0, 0, hs, :]) + bias_ref[h, 0]
        s_ref[h] = s
        cm_ref[h] = jnp.max(s, axis=0, keepdims=True)

    def accumulate(j, h, hs):
        m_old = m_ref[h]
        m_new = jnp.maximum(m_old, cm_ref[h])
        alpha = jnp.exp2(m_old - m_new)
        p = jnp.exp2(s_ref[h] - (m_new - sel_ref[0, h, pl.ds(j, 1), :])).astype(BF16)
        v_aug = jnp.concatenate([v_ref[0, j, hs, :], ones_blk], axis=0)
        rs = slice(h * ACC_ROWS, (h + 1) * ACC_ROWS)
        acc_ref[rs, :] = alpha * acc_ref[rs, :] + _dot(v_aug, p)
        m_ref[h] = m_new

    @pl.when(i == 0)
    def _():
        for h, hs in enumerate(heads):
            own_scores(i, q_ref, h, hs)

    def body(t, carry):
        j = i - t
        slot = jnp.minimum(t + 1, BIAS_SLOTS - 1)
        for h, hs in enumerate(heads):
            accumulate(j, h, hs)
            scores(j - 1, slot, h, hs)
        return carry

    def body2(u, carry):
        return body(2 * u + 1, body(2 * u, carry))

    lax.fori_loop(0, i // 2, body2, 0)

    @pl.when(i % 2 == 1)
    def _():
        body(i - 1, 0)

    @pl.when(i < nblk - 1)
    def _():
        for h, hs in enumerate(heads):
            accumulate(0, h, hs)
            own_scores(i + 1, qn_ref, h, hs)

    @pl.when(i == nblk - 1)
    def _():
        for h, hs in enumerate(heads):
            accumulate(0, h, hs)

    outs = []
    for h, hs in enumerate(heads):
        base = h * ACC_ROWS
        outs.append(acc_ref[base:base + HEAD_DIM, :] / acc_ref[base + HEAD_DIM:base + HEAD_DIM + 1, :])
    o_ref[0] = jnp.concatenate(outs, axis=0).T.astype(o_ref.dtype)


def _moba(qb_t, kb3, vb_t, sel, rows):
    batch, nblk = qb_t.shape[0], qb_t.shape[1]
    seq = nblk * MOBA_BLOCK
    return pl.pallas_call(
        _moba_kernel,
        grid=(batch, nblk),
        in_specs=[
            pl.BlockSpec((1, 1, MOBA_W, MOBA_BLOCK), lambda b, i: (b, i, 0, 0)),
            pl.BlockSpec((1, 1, MOBA_W, MOBA_BLOCK), lambda b, i: (b, jnp.minimum(i + 1, nblk - 1), 0, 0)),
            pl.BlockSpec((1, seq, MOBA_W), lambda b, i: (b, 0, 0)),
            pl.BlockSpec((1, nblk, MOBA_W, MOBA_BLOCK), lambda b, i: (b, 0, 0, 0)),
            pl.BlockSpec((1, MOBA_HEADS, nblk, MOBA_BLOCK), lambda b, i: (b, 0, 0, i)),
            _resident(rows.shape, lambda b, i: (0, 0, 0)),
        ],
        out_specs=pl.BlockSpec((1, MOBA_BLOCK, MOBA_W), lambda b, i: (b, i, 0)),
        out_shape=jax.ShapeDtypeStruct((batch, seq, MOBA_W), BF16),
        scratch_shapes=[pltpu.VMEM((MOBA_HEADS, BIAS_SLOTS, MOBA_BLOCK, MOBA_BLOCK), F32),
                        pltpu.VMEM((MOBA_HEADS, 1, MOBA_BLOCK), F32),
                        pltpu.VMEM((MOBA_HEADS * ACC_ROWS, MOBA_BLOCK), F32),
                        pltpu.VMEM((MOBA_HEADS, MOBA_BLOCK, MOBA_BLOCK), F32),
                        pltpu.VMEM((MOBA_HEADS, 1, MOBA_BLOCK), F32)],
        compiler_params=pltpu.CompilerParams(dimension_semantics=("arbitrary", "arbitrary"),
                                             vmem_limit_bytes=VMEM_LIMIT_BYTES),
        name="moba",
    )(qb_t, qb_t, kb3, vb_t, sel, rows)


def _post_kernel(x_ref, ya_ref, yb_ref, ga_ref, gb_ref, gate1_ref, shift2_ref, scale2_ref,
                 gate2_ref, g2_ref, gf_ref, wa_ref, wb_ref, wo_ref, w1_ref, w2_ref, o_ref):
    a = _dot(ya_ref[...], wa_ref[...])
    b = _dot(yb_ref[...], wb_ref[...])
    merged = (jax.nn.sigmoid(ga_ref[...].astype(F32)) * a
              + jax.nn.sigmoid(gb_ref[...].astype(F32)) * b)
    x1 = x_ref[...] + gate1_ref[0, 0] * _dot(merged.astype(BF16), wo_ref[...])
    h2 = _rms_modulate(x1, g2_ref[...], shift2_ref[0, 0], scale2_ref[0, 0]).astype(BF16)
    y = jnp.zeros_like(x1)
    for c in range(D_FF // FF_CHUNK):
        cs = slice(c * FF_CHUNK, (c + 1) * FF_CHUNK)
        u = jnp.square(jnp.maximum(_dot(h2, w1_ref[:, cs]), 0.0)).astype(BF16)
        y = y + _dot(u, w2_ref[cs, :])
    x2 = x1 + gate2_ref[0, 0] * y
    ms = jnp.mean(x2 * x2, axis=-1, keepdims=True)
    o_ref[...] = (x2 * lax.rsqrt(ms + RMS_EPS)) * gf_ref[...]


def _post(x2, ya, yb, ga, gb, mod4, g2, gf, wa, wb, wo, w1, w2, seq):
    tokens = x2.shape[0]
    tiles_per_seq = seq // TOKEN_TILE
    tok = lambda w: pl.BlockSpec((TOKEN_TILE, w), lambda i: (i, 0))
    modrow = lambda k: pl.BlockSpec((1, 1, 1, D_MODEL), lambda i: (i // tiles_per_seq, k, 0, 0))
    full = lambda a: _resident(a.shape, lambda i: (0, 0))
    return pl.pallas_call(
        _post_kernel,
        grid=(tokens // TOKEN_TILE,),
        in_specs=[tok(D_MODEL), tok(SWA_Q_W), tok(MOBA_W), tok(D_MODEL), tok(D_MODEL),
                  modrow(2), modrow(3), modrow(4), modrow(5),
                  full(g2), full(gf), full(wa), full(wb), full(wo), full(w1), full(w2)],
        out_specs=tok(D_MODEL),
        out_shape=jax.ShapeDtypeStruct((tokens, D_MODEL), F32),
        compiler_params=pltpu.CompilerParams(dimension_semantics=("arbitrary",),
                                             vmem_limit_bytes=VMEM_LIMIT_BYTES),
        name="post",
    )(x2, ya, yb, ga, gb, mod4, mod4, mod4, mod4, g2, gf, wa, wb, wo, w1, w2)


def kernel(x, c, ada_w, ada_b, norm1_g, norm2_g, w_in, attn_sinks, rel_bias, w_branch_a,
           w_branch_b, w_out, w_mlp_in, w_mlp_out, final_g):
    batch, seq, _ = x.shape
    depth = ada_w.shape[0]
    nblk_b = seq // MOBA_BLOCK
    assert seq % TOKEN_TILE == 0 and TOKEN_TILE % MOBA_BLOCK == 0 and batch <= 8
    assert (BIAS_SLOTS - 1) * MOBA_BLOCK - (MOBA_BLOCK - 1) >= 1513 and MAX_DISTANCE == 2048
    assert 2 * SWA_BLOCK <= MOBA_BLOCK and SWA_WINDOW == SWA_BLOCK
    assert depth == 1
    l = 0

    rows = _bias_rows(rel_bias)
    c_t = jnp.zeros((D_MODEL, 8), F32).at[:, :batch].set(c.astype(F32).T)

    x2 = x.reshape(batch * seq, D_MODEL)
    mod = _ada(c_t, ada_w[l], ada_b[l][None, :], batch)
    mod4 = mod[:batch].reshape(batch, N_MOD, 1, D_MODEL)
    col_scale = jnp.ones((sum(IN_WIDTHS),), F32)
    for k in (0, 3):
        col_scale = col_scale.at[IN_OFFS[k]:IN_OFFS[k + 1]].set(ATTN_SCALE * LOG2E)
    w_bf16 = (w_in[l] * col_scale[None, :]).astype(BF16)
    ka, kb, ga, gb, qa_t, va_t, qb_t, vb_t = _proj(
        x2, mod4, norm1_g[l][None, :], w_bf16, batch, seq)

    sink_rows = jnp.broadcast_to(
        (attn_sinks[l].astype(F32) * LOG2E).reshape(SWA_KV_HEADS, 1, SWA_GROUP, 1),
        (SWA_KV_HEADS, 1, SWA_GROUP, SWA_BLOCK)).reshape(SWA_KV_HEADS, 1, SWA_GROUP * SWA_BLOCK)
    ya = _swa(qa_t, ka.reshape(batch, seq, SWA_KV_W), va_t, rows, sink_rows)

    kb3 = kb.reshape(batch, seq, MOBA_W)
    sel = _select(kb3, qb_t)
    yb = _moba(qb_t, kb3, vb_t, sel, rows)

    out = _post(x2, ya.reshape(batch * seq, SWA_Q_W), yb.reshape(batch * seq, MOBA_W), ga, gb,
                mod4, norm2_g[l][None, :], final_g[None, :],
                w_branch_a[l].astype(BF16), w_branch_b[l].astype(BF16), w_out[l].astype(BF16),
                w_mlp_in[l].astype(BF16), w_mlp_out[l].astype(BF16), seq)
    return out.reshape(batch, seq, D_MODEL)
```

```python
import functools
import math

import jax
import jax.numpy as jnp
from jax import lax
from jax.experimental import pallas as pl
from jax.experimental.pallas import tpu as pltpu

D_MODEL = 1024
HEAD_DIM = 64
ATTN_SCALE = HEAD_DIM ** -0.5
LOG2E = math.log2(math.e)
SWA_Q_HEADS = 8
SWA_KV_HEADS = 2
SWA_GROUP = SWA_Q_HEADS // SWA_KV_HEADS
SWA_WINDOW = 128
SWA_BLOCK = 128
MOBA_HEADS = 8
MOBA_BLOCK = 256
MOBA_TOPK = 3
NUM_BUCKETS = 32
MAX_EXACT = NUM_BUCKETS // 2
MAX_DISTANCE = 2048
N_ATTN_HEADS = SWA_Q_HEADS + MOBA_HEADS
SWA_Q_W = SWA_Q_HEADS * HEAD_DIM
SWA_KV_W = SWA_KV_HEADS * HEAD_DIM
MOBA_W = MOBA_HEADS * HEAD_DIM
D_FF = 4 * D_MODEL
N_MOD = 6
RMS_EPS = 1e-6

VMEM_LIMIT_BYTES = 56 * 1024 * 1024
TOKEN_TILE = 512
FF_CHUNK = 1024
BIAS_SLOTS = 8
SWA_UNROLL = 3
ACC_ROWS = HEAD_DIM + 16
NEG_INF = float("-inf")

F32 = jnp.float32
BF16 = jnp.bfloat16


def _resident(block_shape, index_map):
    return pl.BlockSpec(block_shape, index_map, pipeline_mode=pl.Buffered(1))


def _dot(a, b):
    return jnp.dot(a, b, preferred_element_type=F32)


def _dot_nt(a, b):
    return lax.dot_general(a, b, (((1,), (1,)), ((), ())), preferred_element_type=F32)


def _rms_modulate(xv, g, shift, scale):
    ms = jnp.mean(xv * xv, axis=-1, keepdims=True)
    y = xv * lax.rsqrt(ms + RMS_EPS)
    return (y * g) * (1.0 + scale) + shift


def _toeplitz(u, rows):
    wide = jnp.broadcast_to(u, (rows, 2 * rows))
    return pltpu.roll(wide, 0, 1, stride=1, stride_axis=0)[:, rows:]


def _ada_kernel(ct_ref, w_ref, b_ref, o_ref, *, batch):
    cs_t = jax.nn.silu(ct_ref[...])
    w = w_ref[...]
    rows = [jnp.sum(w * cs_t[:, b:b + 1], axis=0, keepdims=True) for b in range(batch)]
    rows.append(jnp.zeros((8 - batch, w.shape[1]), F32))
    o_ref[...] = jnp.concatenate(rows, axis=0) + b_ref[...]


def _ada(c_t, w, b, batch):
    n = w.shape[1]
    bn = 1536
    return pl.pallas_call(
        functools.partial(_ada_kernel, batch=batch),
        grid=(n // bn,),
        in_specs=[pl.BlockSpec((D_MODEL, 8), lambda j: (0, 0)),
                  pl.BlockSpec((D_MODEL, bn), lambda j: (0, j)),
                  pl.BlockSpec((1, bn), lambda j: (0, j))],
        out_specs=pl.BlockSpec((8, bn), lambda j: (0, j)),
        out_shape=jax.ShapeDtypeStruct((8, n), F32),
        compiler_params=pltpu.CompilerParams(dimension_semantics=("arbitrary",),
                                             vmem_limit_bytes=VMEM_LIMIT_BYTES),
        name="ada",
    )(c_t, w, b)


def _t5_bucket(dist):
    n = jnp.maximum(dist, 0)
    nf = jnp.maximum(n, 1).astype(F32)
    large = MAX_EXACT + (jnp.log(nf / MAX_EXACT) / math.log(MAX_DISTANCE / MAX_EXACT)
                         * (NUM_BUCKETS - MAX_EXACT)).astype(jnp.int32)
    large = jnp.minimum(large, NUM_BUCKETS - 1)
    return jnp.where(n < MAX_EXACT, n, large)


def _bias_rows_kernel(bucket_ref, rbt_ref, o_ref):
    bucket = bucket_ref[...]
    n = bucket.shape[1]
    onehot = jnp.where(lax.broadcasted_iota(jnp.int32, (NUM_BUCKETS, n), 0) == bucket, 1.0, 0.0)
    rows = jnp.dot(rbt_ref[...], onehot, preferred_element_type=F32,
                   precision=lax.Precision.HIGHEST)
    rows = jnp.where(bucket >= 0, rows * LOG2E, NEG_INF)
    for d in range(BIAS_SLOTS):
        o_ref[d] = rows[:, d * MOBA_BLOCK:(d + 2) * MOBA_BLOCK]


def _bias_rows(rel_bias):
    n = (BIAS_SLOTS + 1) * MOBA_BLOCK
    dist = jnp.arange(n, dtype=jnp.int32) - MOBA_BLOCK
    bucket = jnp.where(dist >= 0, _t5_bucket(dist), -1)[None, :]
    return pl.pallas_call(
        _bias_rows_kernel,
        out_shape=jax.ShapeDtypeStruct((BIAS_SLOTS, N_ATTN_HEADS, 2 * MOBA_BLOCK), F32),
        name="bias_rows",
    )(bucket, rel_bias.astype(F32).T)


IN_WIDTHS = (SWA_Q_W, SWA_KV_W, SWA_KV_W, MOBA_W, MOBA_W, MOBA_W, D_MODEL, D_MODEL)
IN_OFFS = tuple(sum(IN_WIDTHS[:k]) for k in range(len(IN_WIDTHS) + 1))
FEAT_COLS = (0, 2, 3, 5)
FEAT_W = sum(IN_WIDTHS[k] for k in FEAT_COLS)


def _proj_kernel(x_ref, shift_ref, scale_ref, g_ref, w_ref,
                 ka_ref, kb_ref, ga_ref, gb_ref, qa_ref, va_ref, qb_ref, vb_ref, wt_ref):
    @pl.when(pl.program_id(0) == 0)
    def _():
        o = 0
        for k in FEAT_COLS:
            wt_ref[o:o + IN_WIDTHS[k], :] = w_ref[:, IN_OFFS[k]:IN_OFFS[k + 1]].T
            o += IN_WIDTHS[k]

    h = _rms_modulate(x_ref[...], g_ref[...], shift_ref[0, 0], scale_ref[0, 0]).astype(BF16)
    for ref, k in ((ka_ref, 1), (kb_ref, 4), (ga_ref, 6), (gb_ref, 7)):
        ref[...] = _dot(h, w_ref[:, IN_OFFS[k]:IN_OFFS[k + 1]]).astype(ref.dtype)
    o = 0
    for ref, k in zip((qa_ref, va_ref, qb_ref, vb_ref), FEAT_COLS):
        w = IN_WIDTHS[k]
        ft = _dot_nt(wt_ref[o:o + w, :], h).astype(ref.dtype)
        blk = ref.shape[-1]
        for t in range(TOKEN_TILE // blk):
            ref[0, t] = ft[:, t * blk:(t + 1) * blk]
        o += w


def _proj(x2, mod4, g1, w_bf16, batch, seq):
    tokens = x2.shape[0]
    tiles_per_seq = seq // TOKEN_TILE
    na = TOKEN_TILE // SWA_BLOCK
    nb = TOKEN_TILE // MOBA_BLOCK

    def feat_spec(width, blk, per_tile):
        return pl.BlockSpec((1, per_tile, width, blk),
                            lambda i: (i // tiles_per_seq, i % tiles_per_seq, 0, 0))

    out_shape = (
        jax.ShapeDtypeStruct((tokens, SWA_KV_W), BF16),
        jax.ShapeDtypeStruct((tokens, MOBA_W), BF16),
        jax.ShapeDtypeStruct((tokens, D_MODEL), BF16),
        jax.ShapeDtypeStruct((tokens, D_MODEL), BF16),
        jax.ShapeDtypeStruct((batch, seq // SWA_BLOCK, SWA_Q_W, SWA_BLOCK), BF16),
        jax.ShapeDtypeStruct((batch, seq // SWA_BLOCK, SWA_KV_W, SWA_BLOCK), BF16),
        jax.ShapeDtypeStruct((batch, seq // MOBA_BLOCK, MOBA_W, MOBA_BLOCK), BF16),
        jax.ShapeDtypeStruct((batch, seq // MOBA_BLOCK, MOBA_W, MOBA_BLOCK), BF16),
    )
    tok_spec = lambda w: pl.BlockSpec((TOKEN_TILE, w), lambda i: (i, 0))
    return pl.pallas_call(
        _proj_kernel,
        grid=(tokens // TOKEN_TILE,),
        in_specs=[
            pl.BlockSpec((TOKEN_TILE, D_MODEL), lambda i: (i, 0)),
            pl.BlockSpec((1, 1, 1, D_MODEL), lambda i: (i // tiles_per_seq, 0, 0, 0)),
            pl.BlockSpec((1, 1, 1, D_MODEL), lambda i: (i // tiles_per_seq, 1, 0, 0)),
            _resident((1, D_MODEL), lambda i: (0, 0)),
            _resident(w_bf16.shape, lambda i: (0, 0)),
        ],
        out_specs=(tok_spec(SWA_KV_W), tok_spec(MOBA_W), tok_spec(D_MODEL), tok_spec(D_MODEL),
                   feat_spec(SWA_Q_W, SWA_BLOCK, na), feat_spec(SWA_KV_W, SWA_BLOCK, na),
                   feat_spec(MOBA_W, MOBA_BLOCK, nb), feat_spec(MOBA_W, MOBA_BLOCK, nb)),
        out_shape=out_shape,
        scratch_shapes=[pltpu.VMEM((FEAT_W, D_MODEL), BF16)],
        compiler_params=pltpu.CompilerParams(dimension_semantics=("arbitrary",),
                                             vmem_limit_bytes=VMEM_LIMIT_BYTES),
        name="proj",
    )(x2, mod4, mod4, g1, w_bf16)


def _swa_kernel(q_ref, k_ref, v_ref, rows_ref, sink_ref, o_ref, bias_ref, ot_ref, s_ref):
    L = SWA_BLOCK
    nblk = q_ref.shape[1]

    @pl.when(pl.program_id(0) == 0)
    def _():
        k_idx = lax.broadcasted_iota(jnp.int32, (L, L), 0)
        q_idx = lax.broadcasted_iota(jnp.int32, (L, L), 1)
        for h in range(SWA_Q_HEADS):
            g, gi = divmod(h, SWA_GROUP)
            t0 = rows_ref[0, h:h + 1, MOBA_BLOCK:MOBA_BLOCK + L]
            r = _toeplitz(jnp.concatenate([t0, t0], axis=1), L)
            cols = slice(gi * L, (gi + 1) * L)
            bias_ref[g, 0:L, cols] = jnp.where(q_idx < k_idx, r, NEG_INF)
            bias_ref[g, L:2 * L, cols] = jnp.where(q_idx >= k_idx, r, NEG_INF)

    ones_blk = jnp.concatenate([jnp.ones((1, 2 * L), BF16), jnp.zeros((ACC_ROWS - HEAD_DIM - 1, 2 * L), BF16)],
                               axis=0)
    PW = 2 * L
    chains = [(g, hp) for g in range(SWA_KV_HEADS) for hp in range(SWA_GROUP // 2)]
    kv_cols = [slice(g * HEAD_DIM, (g + 1) * HEAD_DIM) for g in range(SWA_KV_HEADS)]

    def q_pair(c, g, hp):
        h0 = g * SWA_GROUP + 2 * hp
        return jnp.concatenate([q_ref[0, c, (h0 + e) * HEAD_DIM:(h0 + e + 1) * HEAD_DIM, :] for e in range(2)],
                               axis=1)

    def scores_first(idx):
        g, hp = chains[idx]
        lanes = slice(hp * PW, (hp + 1) * PW)
        s_ref[idx, 0:L, :] = jnp.full((L, PW), NEG_INF, F32)
        s_ref[idx, L:2 * L, :] = _dot(k_ref[0, 0:L, kv_cols[g]], q_pair(0, g, hp)) + bias_ref[g, L:2 * L, lanes]

    def scores(c, idx):
        g, hp = chains[idx]
        kw = k_ref[0, pl.ds(pl.multiple_of((c - 1) * L, L), 2 * L), kv_cols[g]]
        s_ref[idx] = _dot(kw, q_pair(c, g, hp)) + bias_ref[g, :, hp * PW:(hp + 1) * PW]

    def attend(c, c_prev, idx):
        g, hp = chains[idx]
        hs = kv_cols[g]
        s = s_ref[idx]
        sink = sink_ref[g, :, hp * PW:(hp + 1) * PW]
        m = jnp.maximum(jnp.max(s, axis=0, keepdims=True), sink)
        p = jnp.exp2(s - m).astype(BF16)
        v_win = jnp.concatenate([v_ref[0, c_prev, hs, :], v_ref[0, c, hs, :]], axis=1)
        pv = _dot(jnp.concatenate([v_win, ones_blk], axis=0), p)
        o = pv[0:HEAD_DIM] / (pv[HEAD_DIM:HEAD_DIM + 1] + jnp.exp2(sink - m))
        for e in range(2):
            r = (g * SWA_GROUP + 2 * hp + e) * HEAD_DIM
            ot_ref[r:r + HEAD_DIM, :] = o[:, e * L:(e + 1) * L]

    def emit(c):
        o_ref[0, pl.ds(pl.multiple_of(c * L, L), L), :] = ot_ref[...].T.astype(o_ref.dtype)

    for idx in range(len(chains)):
        scores_first(idx)
    for idx in range(len(chains)):
        attend(0, 0, idx)
        scores(1, idx)
    emit(0)

    def body(c, carry):
        for idx in range(len(chains)):
            attend(c, c - 1, idx)
            scores(c + 1, idx)
        emit(c)
        return carry

    def body_unrolled(u, carry):
        for e in range(SWA_UNROLL):
            carry = body(SWA_UNROLL * u + 1 + e, carry)
        return carry

    lax.fori_loop(0, (nblk - 2) // SWA_UNROLL, body_unrolled, 0)
    for idx in range(len(chains)):
        attend(nblk - 1, nblk - 2, idx)
    emit(nblk - 1)


def _swa(qa_t, ka3, va_t, rows, sink_rows):
    batch, nblk = qa_t.shape[0], qa_t.shape[1]
    seq = nblk * SWA_BLOCK
    assert nblk >= 2 and (nblk - 2) % SWA_UNROLL == 0
    return pl.pallas_call(
        _swa_kernel,
        grid=(batch,),
        in_specs=[
            pl.BlockSpec((1, nblk, SWA_Q_W, SWA_BLOCK), lambda b: (b, 0, 0, 0)),
            pl.BlockSpec((1, seq, SWA_KV_W), lambda b: (b, 0, 0)),
            pl.BlockSpec((1, nblk, SWA_KV_W, SWA_BLOCK), lambda b: (b, 0, 0, 0)),
            _resident((1, N_ATTN_HEADS, 2 * MOBA_BLOCK), lambda b: (0, 0, 0)),
            _resident(sink_rows.shape, lambda b: (0, 0, 0)),
        ],
        out_specs=pl.BlockSpec((1, seq, SWA_Q_W), lambda b: (b, 0, 0)),
        out_shape=jax.ShapeDtypeStruct((batch, seq, SWA_Q_W), BF16),
        scratch_shapes=[pltpu.VMEM((SWA_KV_HEADS, 2 * SWA_BLOCK, SWA_GROUP * SWA_BLOCK), F32),
                        pltpu.VMEM((SWA_Q_W, SWA_BLOCK), F32),
                        pltpu.VMEM((SWA_Q_HEADS // 2, 2 * SWA_BLOCK, 2 * SWA_BLOCK), F32)],
        compiler_params=pltpu.CompilerParams(dimension_semantics=("arbitrary",),
                                             vmem_limit_bytes=VMEM_LIMIT_BYTES),
        name="swa",
    )(qa_t, ka3, va_t, rows, sink_rows)


def _select_kernel(k_ref, q_ref, o_ref):
    seq = k_ref.shape[1]
    nblk = seq // MOBA_BLOCK
    blk = lax.broadcasted_iota(jnp.int32, (nblk, seq), 0)
    qblk = lax.broadcasted_iota(jnp.int32, (nblk, seq), 1) // MOBA_BLOCK
    past = blk < qblk
    member = jnp.where(blk == qblk, 1.0 / MOBA_BLOCK, 0.0).astype(BF16)
    for hh in range(2):
        hs = slice(hh * HEAD_DIM, (hh + 1) * HEAD_DIM)
        kmean = _dot(member, k_ref[0, :, hs])
        q_t = jnp.concatenate([q_ref[0, t, hs, :] for t in range(nblk)], axis=1)
        k1 = kmean.astype(BF16)
        r1 = kmean - k1.astype(F32)
        k2 = r1.astype(BF16)
        k3 = (r1 - k2.astype(F32)).astype(BF16)
        parts = _dot(jnp.concatenate([k1, k2, k3], axis=0), q_t)
        gate = parts[0:nblk] + parts[nblk:2 * nblk] + parts[2 * nblk:3 * nblk]
        gate = jnp.where(past, gate, NEG_INF)
        mask = jnp.where(blk == qblk, 0.0, NEG_INF)
        blk_f = blk.astype(F32)
        for _ in range(MOBA_TOPK):
            top = jnp.max(gate, axis=0, keepdims=True)
            first = jnp.min(jnp.where(gate == top, blk_f, float(nblk)), axis=0, keepdims=True)
            pick = blk_f == first
            mask = jnp.where(jnp.logical_and(pick, past), 0.0, mask)
            gate = jnp.where(pick, NEG_INF, gate)
        o_ref[0, hh] = mask


def _select(kb3, qb_t):
    batch, seq = kb3.shape[0], kb3.shape[1]
    nblk = seq // MOBA_BLOCK
    pairs = MOBA_HEADS // 2
    return pl.pallas_call(
        _select_kernel,
        grid=(batch, pairs),
        in_specs=[
            pl.BlockSpec((1, seq, 2 * HEAD_DIM), lambda b, hp: (b, 0, hp)),
            pl.BlockSpec((1, nblk, 2 * HEAD_DIM, MOBA_BLOCK), lambda b, hp: (b, 0, hp, 0)),
        ],
        out_specs=pl.BlockSpec((1, 2, nblk, seq), lambda b, hp: (b, hp, 0, 0)),
        out_shape=jax.ShapeDtypeStruct((batch, MOBA_HEADS, nblk, seq), F32),
        compiler_params=pltpu.CompilerParams(dimension_semantics=("arbitrary", "arbitrary"),
                                             vmem_limit_bytes=VMEM_LIMIT_BYTES),
        name="moba_select",
    )(kb3, qb_t)


def _moba_kernel(q_ref, qn_ref, k_ref, v_ref, sel_ref, rows_ref, o_ref, bias_ref, m_ref, acc_ref, s_ref, cm_ref):
    i = pl.program_id(1)
    nblk = pl.num_programs(1)
    MB = MOBA_BLOCK

    @pl.when(jnp.logical_and(pl.program_id(0) == 0, i == 0))
    def _():
        for h in range(MOBA_HEADS):
            for d in range(BIAS_SLOTS):
                bias_ref[h, d] = _toeplitz(rows_ref[d, SWA_Q_HEADS + h:SWA_Q_HEADS + h + 1, :], MB)

    m_ref[...] = jnp.full(m_ref.shape, NEG_INF, F32)
    acc_ref[...] = jnp.zeros(acc_ref.shape, F32)
    heads = [slice(h * HEAD_DIM, (h + 1) * HEAD_DIM) for h in range(MOBA_HEADS)]
    ones_blk = jnp.concatenate([jnp.ones((1, MB), BF16), jnp.zeros((ACC_ROWS - HEAD_DIM - 1, MB), BF16)],
                               axis=0)

    def scores(j, slot, h, hs):
        rows = pl.ds(pl.multiple_of(j * MB, MB), MB)
        s = _dot(k_ref[0, rows, hs], q_ref[0, 0, hs, :]) + bias_ref[h, slot]
        s_ref[h] = s
        cm_ref[h] = jnp.max(s, axis=0, keepdims=True) + sel_ref[0, h, pl.ds(j, 1), :]

    def own_scores(blk, qr, h, hs):
        rows = pl.ds(pl.multiple_of(blk * MB, MB), MB)
        s = _dot(k_ref[0, rows, hs], qr[0, 0, hs, :]) + bias_ref[h, 0]
        s_ref[h] = s
        cm_ref[h] = jnp.max(s, axis=0, keepdims=True)

    def accumulate(j, h, hs):
        m_old = m_ref[h]
        m_new = jnp.maximum(m_old, cm_ref[h])
        alpha = jnp.exp2(m_old - m_new)
        p = jnp.exp2(s_ref[h] - (m_new - sel_ref[0, h, pl.ds(j, 1), :])).astype(BF16)
        v_aug = jnp.concatenate([v_ref[0, j, hs, :], ones_blk], axis=0)
        rs = slice(h * ACC_ROWS, (h + 1) * ACC_ROWS)
        acc_ref[rs, :] = alpha * acc_ref[rs, :] + _dot(v_aug, p)
        m_ref[h] = m_new

    @pl.when(i == 0)
    def _():
        for h, hs in enumerate(heads):
            own_scores(i, q_ref, h, hs)

    def body(t, carry):
        j = i - t
        slot = jnp.minimum(t + 1, BIAS_SLOTS - 1)
        for h, hs in enumerate(heads):
            accumulate(j, h, hs)
            scores(j - 1, slot, h, hs)
        return carry

    def body2(u, carry):
        return body(2 * u + 1, body(2 * u, carry))

    lax.fori_loop(0, i // 2, body2, 0)

    @pl.when(i % 2 == 1)
    def _():
        body(i - 1, 0)

    @pl.when(i < nblk - 1)
    def _():
        for h, hs in enumerate(heads):
            accumulate(0, h, hs)
            own_scores(i + 1, qn_ref, h, hs)

    @pl.when(i == nblk - 1)
    def _():
        for h, hs in enumerate(heads):
            accumulate(0, h, hs)

    outs = []
    for h, hs in enumerate(heads):
        base = h * ACC_ROWS
        outs.append(acc_ref[base:base + HEAD_DIM, :] / acc_ref[base + HEAD_DIM:base + HEAD_DIM + 1, :])
    o_ref[0] = jnp.concatenate(outs, axis=0).T.astype(o_ref.dtype)


def _moba(qb_t, kb3, vb_t, sel, rows):
    batch, nblk = qb_t.shape[0], qb_t.shape[1]
    seq = nblk * MOBA_BLOCK
    return pl.pallas_call(
        _moba_kernel,
        grid=(batch, nblk),
        in_specs=[
            pl.BlockSpec((1, 1, MOBA_W, MOBA_BLOCK), lambda b, i: (b, i, 0, 0)),
            pl.BlockSpec((1, 1, MOBA_W, MOBA_BLOCK), lambda b, i: (b, jnp.minimum(i + 1, nblk - 1), 0, 0)),
            pl.BlockSpec((1, seq, MOBA_W), lambda b, i: (b, 0, 0)),
            pl.BlockSpec((1, nblk, MOBA_W, MOBA_BLOCK), lambda b, i: (b, 0, 0, 0)),
            pl.BlockSpec((1, MOBA_HEADS, nblk, MOBA_BLOCK), lambda b, i: (b, 0, 0, i)),
            _resident(rows.shape, lambda b, i: (0, 0, 0)),
        ],
        out_specs=pl.BlockSpec((1, MOBA_BLOCK, MOBA_W), lambda b, i: (b, i, 0)),
        out_shape=jax.ShapeDtypeStruct((batch, seq, MOBA_W), BF16),
        scratch_shapes=[pltpu.VMEM((MOBA_HEADS, BIAS_SLOTS, MOBA_BLOCK, MOBA_BLOCK), F32),
                        pltpu.VMEM((MOBA_HEADS, 1, MOBA_BLOCK), F32),
                        pltpu.VMEM((MOBA_HEADS * ACC_ROWS, MOBA_BLOCK), F32),
                        pltpu.VMEM((MOBA_HEADS, MOBA_BLOCK, MOBA_BLOCK), F32),
                        pltpu.VMEM((MOBA_HEADS, 1, MOBA_BLOCK), F32)],
        compiler_params=pltpu.CompilerParams(dimension_semantics=("arbitrary", "arbitrary"),
                                             vmem_limit_bytes=VMEM_LIMIT_BYTES),
        name="moba",
    )(qb_t, qb_t, kb3, vb_t, sel, rows)


def _post_kernel(x_ref, ya_ref, yb_ref, ga_ref, gb_ref, gate1_ref, shift2_ref, scale2_ref,
                 gate2_ref, g2_ref, gf_ref, wa_ref, wb_ref, wo_ref, w1_ref, w2_ref, o_ref):
    a = _dot(ya_ref[...], wa_ref[...])
    b = _dot(yb_ref[...], wb_ref[...])
    merged = (jax.nn.sigmoid(ga_ref[...].astype(F32)) * a
              + jax.nn.sigmoid(gb_ref[...].astype(F32)) * b)
    x1 = x_ref[...] + gate1_ref[0, 0] * _dot(merged.astype(BF16), wo_ref[...])
    h2 = _rms_modulate(x1, g2_ref[...], shift2_ref[0, 0], scale2_ref[0, 0]).astype(BF16)
    y = jnp.zeros_like(x1)
    for c in range(D_FF // FF_CHUNK):
        cs = slice(c * FF_CHUNK, (c + 1) * FF_CHUNK)
        u = jnp.square(jnp.maximum(_dot(h2, w1_ref[:, cs]), 0.0)).astype(BF16)
        y = y + _dot(u, w2_ref[cs, :])
    x2 = x1 + gate2_ref[0, 0] * y
    ms = jnp.mean(x2 * x2, axis=-1, keepdims=True)
    o_ref[...] = (x2 * lax.rsqrt(ms + RMS_EPS)) * gf_ref[...]


def _post(x2, ya, yb, ga, gb, mod4, g2, gf, wa, wb, wo, w1, w2, seq):
    tokens = x2.shape[0]
    tiles_per_seq = seq // TOKEN_TILE
    tok = lambda w: pl.BlockSpec((TOKEN_TILE, w), lambda i: (i, 0))
    modrow = lambda k: pl.BlockSpec((1, 1, 1, D_MODEL), lambda i: (i // tiles_per_seq, k, 0, 0))
    full = lambda a: _resident(a.shape, lambda i: (0, 0))
    return pl.pallas_call(
        _post_kernel,
        grid=(tokens // TOKEN_TILE,),
        in_specs=[tok(D_MODEL), tok(SWA_Q_W), tok(MOBA_W), tok(D_MODEL), tok(D_MODEL),
                  modrow(2), modrow(3), modrow(4), modrow(5),
                  full(g2), full(gf), full(wa), full(wb), full(wo), full(w1), full(w2)],
        out_specs=tok(D_MODEL),
        out_shape=jax.ShapeDtypeStruct((tokens, D_MODEL), F32),
        compiler_params=pltpu.CompilerParams(dimension_semantics=("arbitrary",),
                                             vmem_limit_bytes=VMEM_LIMIT_BYTES),
        name="post",
    )(x2, ya, yb, ga, gb, mod4, mod4, mod4, mod4, g2, gf, wa, wb, wo, w1, w2)


def kernel(x, c, ada_w, ada_b, norm1_g, norm2_g, w_in, attn_sinks, rel_bias, w_branch_a,
           w_branch_b, w_out, w_mlp_in, w_mlp_out, final_g):
    batch, seq, _ = x.shape
    depth = ada_w.shape[0]
    nblk_b = seq // MOBA_BLOCK
    assert seq % TOKEN_TILE == 0 and TOKEN_TILE % MOBA_BLOCK == 0 and batch <= 8
    assert (BIAS_SLOTS - 1) * MOBA_BLOCK - (MOBA_BLOCK - 1) >= 1513 and MAX_DISTANCE == 2048
    assert 2 * SWA_BLOCK <= MOBA_BLOCK and SWA_WINDOW == SWA_BLOCK
    assert depth == 1
    l = 0

    rows = _bias_rows(rel_bias)
    c_t = jnp.zeros((D_MODEL, 8), F32).at[:, :batch].set(c.astype(F32).T)

    x2 = x.reshape(batch * seq, D_MODEL)
    mod = _ada(c_t, ada_w[l], ada_b[l][None, :], batch)
    mod4 = mod[:batch].reshape(batch, N_MOD, 1, D_MODEL)
    col_scale = jnp.ones((sum(IN_WIDTHS),), F32)
    for k in (0, 3):
        col_scale = col_scale.at[IN_OFFS[k]:IN_OFFS[k + 1]].set(ATTN_SCALE * LOG2E)
    w_bf16 = (w_in[l] * col_scale[None, :]).astype(BF16)
    ka, kb, ga, gb, qa_t, va_t, qb_t, vb_t = _proj(
        x2, mod4, norm1_g[l][None, :], w_bf16, batch, seq)

    sink_rows = jnp.broadcast_to(
        (attn_sinks[l].astype(F32) * LOG2E).reshape(SWA_KV_HEADS, 1, SWA_GROUP, 1),
        (SWA_KV_HEADS, 1, SWA_GROUP, SWA_BLOCK)).reshape(SWA_KV_HEADS, 1, SWA_GROUP * SWA_BLOCK)
    ya = _swa(qa_t, ka.reshape(batch, seq, SWA_KV_W), va_t, rows, sink_rows)

    kb3 = kb.reshape(batch, seq, MOBA_W)
    sel = _select(kb3, qb_t)
    yb = _moba(qb_t, kb3, vb_t, sel, rows)

    out = _post(x2, ya.reshape(batch * seq, SWA_Q_W), yb.reshape(batch * seq, MOBA_W), ga, gb,
                mod4, norm2_g[l][None, :], final_g[None, :],
                w_branch_a[l].astype(BF16), w_branch_b[l].astype(BF16), w_out[l].astype(BF16),
                w_mlp_in[l].astype(BF16), w_mlp_out[l].astype(BF16), seq)
    return out.reshape(batch, seq, D_MODEL)
```

```python
import functools
import math

import jax
import jax.numpy as jnp
from jax import lax
from jax.experimental import pallas as pl
from jax.experimental.pallas import tpu as pltpu

D_MODEL = 1024
HEAD_DIM = 64
ATTN_SCALE = HEAD_DIM ** -0.5
LOG2E = math.log2(math.e)
SWA_Q_HEADS = 8
SWA_KV_HEADS = 2
SWA_GROUP = SWA_Q_HEADS // SWA_KV_HEADS
SWA_WINDOW = 128
SWA_BLOCK = 128
MOBA_HEADS = 8
MOBA_BLOCK = 256
MOBA_TOPK = 3
NUM_BUCKETS = 32
MAX_EXACT = NUM_BUCKETS // 2
MAX_DISTANCE = 2048
N_ATTN_HEADS = SWA_Q_HEADS + MOBA_HEADS
SWA_Q_W = SWA_Q_HEADS * HEAD_DIM
SWA_KV_W = SWA_KV_HEADS * HEAD_DIM
MOBA_W = MOBA_HEADS * HEAD_DIM
D_FF = 4 * D_MODEL
N_MOD = 6
RMS_EPS = 1e-6

VMEM_LIMIT_BYTES = 56 * 1024 * 1024
TOKEN_TILE = 512
FF_CHUNK = 1024
BIAS_SLOTS = 8
SWA_UNROLL = 3
ACC_ROWS = HEAD_DIM + 16
NEG_INF = float("-inf")

F32 = jnp.float32
BF16 = jnp.bfloat16


def _resident(block_shape, index_map):
    return pl.BlockSpec(block_shape, index_map, pipeline_mode=pl.Buffered(1))


def _dot(a, b):
    return jnp.dot(a, b, preferred_element_type=F32)


def _dot_nt(a, b):
    return lax.dot_general(a, b, (((1,), (1,)), ((), ())), preferred_element_type=F32)


def _rms_modulate(xv, g, shift, scale):
    ms = jnp.mean(xv * xv, axis=-1, keepdims=True)
    y = xv * lax.rsqrt(ms + RMS_EPS)
    return (y * g) * (1.0 + scale) + shift


def _toeplitz(u, rows):
    wide = jnp.broadcast_to(u, (rows, 2 * rows))
    return pltpu.roll(wide, 0, 1, stride=1, stride_axis=0)[:, rows:]


def _ada_kernel(ct_ref, w_ref, b_ref, o_ref, *, batch):
    cs_t = jax.nn.silu(ct_ref[...])
    w = w_ref[...]
    rows = [jnp.sum(w * cs_t[:, b:b + 1], axis=0, keepdims=True) for b in range(batch)]
    rows.append(jnp.zeros((8 - batch, w.shape[1]), F32))
    o_ref[...] = jnp.concatenate(rows, axis=0) + b_ref[...]


def _ada(c_t, w, b, batch):
    n = w.shape[1]
    bn = 1536
    return pl.pallas_call(
        functools.partial(_ada_kernel, batch=batch),
        grid=(n // bn,),
        in_specs=[pl.BlockSpec((D_MODEL, 8), lambda j: (0, 0)),
                  pl.BlockSpec((D_MODEL, bn), lambda j: (0, j)),
                  pl.BlockSpec((1, bn), lambda j: (0, j))],
        out_specs=pl.BlockSpec((8, bn), lambda j: (0, j)),
        out_shape=jax.ShapeDtypeStruct((8, n), F32),
        compiler_params=pltpu.CompilerParams(dimension_semantics=("arbitrary",),
                                             vmem_limit_bytes=VMEM_LIMIT_BYTES),
        name="ada",
    )(c_t, w, b)


def _t5_bucket(dist):
    n = jnp.maximum(dist, 0)
    nf = jnp.maximum(n, 1).astype(F32)
    large = MAX_EXACT + (jnp.log(nf / MAX_EXACT) / math.log(MAX_DISTANCE / MAX_EXACT)
                         * (NUM_BUCKETS - MAX_EXACT)).astype(jnp.int32)
    large = jnp.minimum(large, NUM_BUCKETS - 1)
    return jnp.where(n < MAX_EXACT, n, large)


def _bias_rows_kernel(bucket_ref, rbt_ref, o_ref):
    bucket = bucket_ref[...]
    n = bucket.shape[1]
    onehot = jnp.where(lax.broadcasted_iota(jnp.int32, (NUM_BUCKETS, n), 0) == bucket, 1.0, 0.0)
    rows = jnp.dot(rbt_ref[...], onehot, preferred_element_type=F32,
                   precision=lax.Precision.HIGHEST)
    rows = jnp.where(bucket >= 0, rows * LOG2E, NEG_INF)
    for d in range(BIAS_SLOTS):
        o_ref[d] = rows[:, d * MOBA_BLOCK:(d + 2) * MOBA_BLOCK]


def _bias_rows(rel_bias):
    n = (BIAS_SLOTS + 1) * MOBA_BLOCK
    dist = jnp.arange(n, dtype=jnp.int32) - MOBA_BLOCK
    bucket = jnp.where(dist >= 0, _t5_bucket(dist), -1)[None, :]
    return pl.pallas_call(
        _bias_rows_kernel,
        out_shape=jax.ShapeDtypeStruct((BIAS_SLOTS, N_ATTN_HEADS, 2 * MOBA_BLOCK), F32),
        name="bias_rows",
    )(bucket, rel_bias.astype(F32).T)


IN_WIDTHS = (SWA_Q_W, SWA_KV_W, SWA_KV_W, MOBA_W, MOBA_W, MOBA_W, D_MODEL, D_MODEL)
IN_OFFS = tuple(sum(IN_WIDTHS[:k]) for k in range(len(IN_WIDTHS) + 1))
FEAT_COLS = (0, 2, 3, 5)
FEAT_W = sum(IN_WIDTHS[k] for k in FEAT_COLS)


def _proj_kernel(x_ref, shift_ref, scale_ref, g_ref, w_ref, *refs):
    n_cast = (len(refs) - 9) // 2
    cast_in, refs = refs[:n_cast], refs[n_cast:]
    ka_ref, kb_ref, ga_ref, gb_ref, qa_ref, va_ref, qb_ref, vb_ref = refs[:8]
    cast_out, wt_ref = refs[8:8 + n_cast], refs[8 + n_cast]

    for src, dst in zip(cast_in, cast_out):
        dst[...] = src[...].astype(dst.dtype)

    @pl.when(pl.program_id(0) == 0)
    def _():
        o = 0
        for k in FEAT_COLS:
            wt_ref[o:o + IN_WIDTHS[k], :] = w_ref[:, IN_OFFS[k]:IN_OFFS[k + 1]].T
            o += IN_WIDTHS[k]

    h = _rms_modulate(x_ref[...], g_ref[...], shift_ref[0, 0], scale_ref[0, 0]).astype(BF16)
    for ref, k in ((ka_ref, 1), (kb_ref, 4), (ga_ref, 6), (gb_ref, 7)):
        ref[...] = _dot(h, w_ref[:, IN_OFFS[k]:IN_OFFS[k + 1]]).astype(ref.dtype)
    o = 0
    for ref, k in zip((qa_ref, va_ref, qb_ref, vb_ref), FEAT_COLS):
        w = IN_WIDTHS[k]
        ft = _dot_nt(wt_ref[o:o + w, :], h).astype(ref.dtype)
        blk = ref.shape[-1]
        for t in range(TOKEN_TILE // blk):
            ref[0, t] = ft[:, t * blk:(t + 1) * blk]
        o += w


def _proj(x2, mod4, g1, w_bf16, cast_weights, batch, seq):
    tokens = x2.shape[0]
    steps = tokens // TOKEN_TILE
    assert all(w.shape[0] % (16 * steps) == 0 for w in cast_weights)
    cast_specs = [pl.BlockSpec((w.shape[0] // steps, w.shape[1]), lambda i: (i, 0)) for w in cast_weights]
    tiles_per_seq = seq // TOKEN_TILE
    na = TOKEN_TILE // SWA_BLOCK
    nb = TOKEN_TILE // MOBA_BLOCK

    def feat_spec(width, blk, per_tile):
        return pl.BlockSpec((1, per_tile, width, blk),
                            lambda i: (i // tiles_per_seq, i % tiles_per_seq, 0, 0))

    out_shape = (
        jax.ShapeDtypeStruct((tokens, SWA_KV_W), BF16),
        jax.ShapeDtypeStruct((tokens, MOBA_W), BF16),
        jax.ShapeDtypeStruct((tokens, D_MODEL), BF16),
        jax.ShapeDtypeStruct((tokens, D_MODEL), BF16),
        jax.ShapeDtypeStruct((batch, seq // SWA_BLOCK, SWA_Q_W, SWA_BLOCK), BF16),
        jax.ShapeDtypeStruct((batch, seq // SWA_BLOCK, SWA_KV_W, SWA_BLOCK), BF16),
        jax.ShapeDtypeStruct((batch, seq // MOBA_BLOCK, MOBA_W, MOBA_BLOCK), BF16),
        jax.ShapeDtypeStruct((batch, seq // MOBA_BLOCK, MOBA_W, MOBA_BLOCK), BF16),
    )
    tok_spec = lambda w: pl.BlockSpec((TOKEN_TILE, w), lambda i: (i, 0))
    return pl.pallas_call(
        _proj_kernel,
        grid=(steps,),
        in_specs=[
            pl.BlockSpec((TOKEN_TILE, D_MODEL), lambda i: (i, 0)),
            pl.BlockSpec((1, 1, 1, D_MODEL), lambda i: (i // tiles_per_seq, 0, 0, 0)),
            pl.BlockSpec((1, 1, 1, D_MODEL), lambda i: (i // tiles_per_seq, 1, 0, 0)),
            _resident((1, D_MODEL), lambda i: (0, 0)),
            _resident(w_bf16.shape, lambda i: (0, 0)),
        ] + cast_specs,
        out_specs=(tok_spec(SWA_KV_W), tok_spec(MOBA_W), tok_spec(D_MODEL), tok_spec(D_MODEL),
                   feat_spec(SWA_Q_W, SWA_BLOCK, na), feat_spec(SWA_KV_W, SWA_BLOCK, na),
                   feat_spec(MOBA_W, MOBA_BLOCK, nb), feat_spec(MOBA_W, MOBA_BLOCK, nb))
        + tuple(cast_specs),
        out_shape=out_shape + tuple(jax.ShapeDtypeStruct(w.shape, BF16) for w in cast_weights),
        scratch_shapes=[pltpu.VMEM((FEAT_W, D_MODEL), BF16)],
        compiler_params=pltpu.CompilerParams(dimension_semantics=("arbitrary",),
                                             vmem_limit_bytes=VMEM_LIMIT_BYTES),
        name="proj",
    )(x2, mod4, mod4, g1, w_bf16, *cast_weights)


def _swa_kernel(q_ref, k_ref, v_ref, rows_ref, sink_ref, o_ref, bias_ref, ot_ref, s_ref):
    L = SWA_BLOCK
    nblk = q_ref.shape[1]

    @pl.when(pl.program_id(0) == 0)
    def _():
        k_idx = lax.broadcasted_iota(jnp.int32, (L, L), 0)
        q_idx = lax.broadcasted_iota(jnp.int32, (L, L), 1)
        for h in range(SWA_Q_HEADS):
            g, gi = divmod(h, SWA_GROUP)
            t0 = rows_ref[0, h:h + 1, MOBA_BLOCK:MOBA_BLOCK + L]
            r = _toeplitz(jnp.concatenate([t0, t0], axis=1), L)
            cols = slice(gi * L, (gi + 1) * L)
            bias_ref[g, 0:L, cols] = jnp.where(q_idx < k_idx, r, NEG_INF)
            bias_ref[g, L:2 * L, cols] = jnp.where(q_idx >= k_idx, r, NEG_INF)

    ones_blk = jnp.concatenate([jnp.ones((1, 2 * L), BF16), jnp.zeros((ACC_ROWS - HEAD_DIM - 1, 2 * L), BF16)],
                               axis=0)
    PW = 2 * L
    chains = [(g, hp) for g in range(SWA_KV_HEADS) for hp in range(SWA_GROUP // 2)]
    kv_cols = [slice(g * HEAD_DIM, (g + 1) * HEAD_DIM) for g in range(SWA_KV_HEADS)]

    def q_pair(c, g, hp):
        h0 = g * SWA_GROUP + 2 * hp
        return jnp.concatenate([q_ref[0, c, (h0 + e) * HEAD_DIM:(h0 + e + 1) * HEAD_DIM, :] for e in range(2)],
                               axis=1)

    def scores_first(idx):
        g, hp = chains[idx]
        lanes = slice(hp * PW, (hp + 1) * PW)
        s_ref[idx, 0:L, :] = jnp.full((L, PW), NEG_INF, F32)
        s_ref[idx, L:2 * L, :] = _dot(k_ref[0, 0:L, kv_cols[g]], q_pair(0, g, hp)) + bias_ref[g, L:2 * L, lanes]

    def scores(c, idx):
        g, hp = chains[idx]
        kw = k_ref[0, pl.ds(pl.multiple_of((c - 1) * L, L), 2 * L), kv_cols[g]]
        s_ref[idx] = _dot(kw, q_pair(c, g, hp)) + bias_ref[g, :, hp * PW:(hp + 1) * PW]

    def attend(c, c_prev, idx):
        g, hp = chains[idx]
        hs = kv_cols[g]
        s = s_ref[idx]
        sink = sink_ref[g, :, hp * PW:(hp + 1) * PW]
        m = jnp.maximum(jnp.max(s, axis=0, keepdims=True), sink)
        p = jnp.exp2(s - m).astype(BF16)
        v_win = jnp.concatenate([v_ref[0, c_prev, hs, :], v_ref[0, c, hs, :]], axis=1)
        pv = _dot(jnp.concatenate([v_win, ones_blk], axis=0), p)
        o = pv[0:HEAD_DIM] / (pv[HEAD_DIM:HEAD_DIM + 1] + jnp.exp2(sink - m))
        for e in range(2):
            r = (g * SWA_GROUP + 2 * hp + e) * HEAD_DIM
            ot_ref[r:r + HEAD_DIM, :] = o[:, e * L:(e + 1) * L]

    def emit(c):
        o_ref[0, pl.ds(pl.multiple_of(c * L, L), L), :] = ot_ref[...].T.astype(o_ref.dtype)

    for idx in range(len(chains)):
        scores_first(idx)
    for idx in range(len(chains)):
        attend(0, 0, idx)
        scores(1, idx)
    emit(0)

    def body(c, carry):
        for idx in range(len(chains)):
            attend(c, c - 1, idx)
            scores(c + 1, idx)
        emit(c)
        return carry

    def body_unrolled(u, carry):
        for e in range(SWA_UNROLL):
            carry = body(SWA_UNROLL * u + 1 + e, carry)
        return carry

    lax.fori_loop(0, (nblk - 2) // SWA_UNROLL, body_unrolled, 0)
    for idx in range(len(chains)):
        attend(nblk - 1, nblk - 2, idx)
    emit(nblk - 1)


def _swa(qa_t, ka3, va_t, rows, sink_rows):
    batch, nblk = qa_t.shape[0], qa_t.shape[1]
    seq = nblk * SWA_BLOCK
    assert nblk >= 2 and (nblk - 2) % SWA_UNROLL == 0
    return pl.pallas_call(
        _swa_kernel,
        grid=(batch,),
        in_specs=[
            pl.BlockSpec((1, nblk, SWA_Q_W, SWA_BLOCK), lambda b: (b, 0, 0, 0)),
            pl.BlockSpec((1, seq, SWA_KV_W), lambda b: (b, 0, 0)),
            pl.BlockSpec((1, nblk, SWA_KV_W, SWA_BLOCK), lambda b: (b, 0, 0, 0)),
            _resident((1, N_ATTN_HEADS, 2 * MOBA_BLOCK), lambda b: (0, 0, 0)),
            _resident(sink_rows.shape, lambda b: (0, 0, 0)),
        ],
        out_specs=pl.BlockSpec((1, seq, SWA_Q_W), lambda b: (b, 0, 0)),
        out_shape=jax.ShapeDtypeStruct((batch, seq, SWA_Q_W), BF16),
        scratch_shapes=[pltpu.VMEM((SWA_KV_HEADS, 2 * SWA_BLOCK, SWA_GROUP * SWA_BLOCK), F32),
                        pltpu.VMEM((SWA_Q_W, SWA_BLOCK), F32),
                        pltpu.VMEM((SWA_Q_HEADS // 2, 2 * SWA_BLOCK, 2 * SWA_BLOCK), F32)],
        compiler_params=pltpu.CompilerParams(dimension_semantics=("arbitrary",),
                                             vmem_limit_bytes=VMEM_LIMIT_BYTES),
        name="swa",
    )(qa_t, ka3, va_t, rows, sink_rows)


def _select_kernel(k_ref, q_ref, o_ref):
    seq = k_ref.shape[1]
    nblk = seq // MOBA_BLOCK
    blk = lax.broadcasted_iota(jnp.int32, (nblk, seq), 0)
    qblk = lax.broadcasted_iota(jnp.int32, (nblk, seq), 1) // MOBA_BLOCK
    past = blk < qblk
    member = jnp.where(blk == qblk, 1.0 / MOBA_BLOCK, 0.0).astype(BF16)
    for hh in range(2):
        hs = slice(hh * HEAD_DIM, (hh + 1) * HEAD_DIM)
        kmean = _dot(member, k_ref[0, :, hs])
        q_t = jnp.concatenate([q_ref[0, t, hs, :] for t in range(nblk)], axis=1)
        k1 = kmean.astype(BF16)
        r1 = kmean - k1.astype(F32)
        k2 = r1.astype(BF16)
        k3 = (r1 - k2.astype(F32)).astype(BF16)
        parts = _dot(jnp.concatenate([k1, k2, k3], axis=0), q_t)
        gate = parts[0:nblk] + parts[nblk:2 * nblk] + parts[2 * nblk:3 * nblk]
        gate = jnp.where(past, gate, NEG_INF)
        mask = jnp.where(blk == qblk, 0.0, NEG_INF)
        blk_f = blk.astype(F32)
        for _ in range(MOBA_TOPK):
            top = jnp.max(gate, axis=0, keepdims=True)
            first = jnp.min(jnp.where(gate == top, blk_f, float(nblk)), axis=0, keepdims=True)
            pick = blk_f == first
            mask = jnp.where(jnp.logical_and(pick, past), 0.0, mask)
            gate = jnp.where(pick, NEG_INF, gate)
        o_ref[0, hh] = mask


def _select(kb3, qb_t):
    batch, seq = kb3.shape[0], kb3.shape[1]
    nblk = seq // MOBA_BLOCK
    pairs = MOBA_HEADS // 2
    return pl.pallas_call(
        _select_kernel,
        grid=(batch, pairs),
        in_specs=[
            pl.BlockSpec((1, seq, 2 * HEAD_DIM), lambda b, hp: (b, 0, hp)),
            pl.BlockSpec((1, nblk, 2 * HEAD_DIM, MOBA_BLOCK), lambda b, hp: (b, 0, hp, 0)),
        ],
        out_specs=pl.BlockSpec((1, 2, nblk, seq), lambda b, hp: (b, hp, 0, 0)),
        out_shape=jax.ShapeDtypeStruct((batch, MOBA_HEADS, nblk, seq), F32),
        compiler_params=pltpu.CompilerParams(dimension_semantics=("arbitrary", "arbitrary"),
                                             vmem_limit_bytes=VMEM_LIMIT_BYTES),
        name="moba_select",
    )(kb3, qb_t)


def _moba_kernel(q_ref, qn_ref, k_ref, v_ref, sel_ref, rows_ref, o_ref, bias_ref, m_ref, acc_ref, s_ref, cm_ref):
    i = pl.program_id(1)
    nblk = pl.num_programs(1)
    MB = MOBA_BLOCK

    @pl.when(jnp.logical_and(pl.program_id(0) == 0, i == 0))
    def _():
        for h in range(MOBA_HEADS):
            for d in range(BIAS_SLOTS):
                bias_ref[h, d] = _toeplitz(rows_ref[d, SWA_Q_HEADS + h:SWA_Q_HEADS + h + 1, :], MB)

    m_ref[...] = jnp.full(m_ref.shape, NEG_INF, F32)
    acc_ref[...] = jnp.zeros(acc_ref.shape, F32)
    heads = [slice(h * HEAD_DIM, (h + 1) * HEAD_DIM) for h in range(MOBA_HEADS)]
    ones_blk = jnp.concatenate([jnp.ones((1, MB), BF16), jnp.zeros((ACC_ROWS - HEAD_DIM - 1, MB), BF16)],
                               axis=0)

    def scores(j, slot, h, hs):
        rows = pl.ds(pl.multiple_of(j * MB, MB), MB)
        s = _dot(k_ref[0, rows, hs], q_ref[0, 0, hs, :]) + bias_ref[h, slot]
        s_ref[h] = s
        cm_ref[h] = jnp.max(s, axis=0, keepdims=True) + sel_ref[0, h, pl.ds(j, 1), :]

    def own_scores(blk, qr, h, hs):
        rows = pl.ds(pl.multiple_of(blk * MB, MB), MB)
        s = _dot(k_ref[0, rows, hs], qr[0, 0, hs, :]) + bias_ref[h, 0]
        s_ref[h] = s
        cm_ref[h] = jnp.max(s, axis=0, keepdims=True)

    def accumulate(j, h, hs):
        m_old = m_ref[h]
        m_new = jnp.maximum(m_old, cm_ref[h])
        alpha = jnp.exp2(m_old - m_new)
        p = jnp.exp2(s_ref[h] - (m_new - sel_ref[0, h, pl.ds(j, 1), :])).astype(BF16)
        v_aug = jnp.concatenate([v_ref[0, j, hs, :], ones_blk], axis=0)
        rs = slice(h * ACC_ROWS, (h + 1) * ACC_ROWS)
        acc_ref[rs, :] = alpha * acc_ref[rs, :] + _dot(v_aug, p)
        m_ref[h] = m_new

    @pl.when(i == 0)
    def _():
        for h, hs in enumerate(heads):
            own_scores(i, q_ref, h, hs)

    def body(t, carry):
        j = i - t
        slot = jnp.minimum(t + 1, BIAS_SLOTS - 1)
        for h, hs in enumerate(heads):
            accumulate(j, h, hs)
            scores(j - 1, slot, h, hs)
        return carry

    def body2(u, carry):
        return body(2 * u + 1, body(2 * u, carry))

    lax.fori_loop(0, i // 2, body2, 0)

    @pl.when(i % 2 == 1)
    def _():
        body(i - 1, 0)

    @pl.when(i < nblk - 1)
    def _():
        for h, hs in enumerate(heads):
            accumulate(0, h, hs)
            own_scores(i + 1, qn_ref, h, hs)

    @pl.when(i == nblk - 1)
    def _():
        for h, hs in enumerate(heads):
            accumulate(0, h, hs)

    outs = []
    for h, hs in enumerate(heads):
        base = h * ACC_ROWS
        outs.append(acc_ref[base:base + HEAD_DIM, :] / acc_ref[base + HEAD_DIM:base + HEAD_DIM + 1, :])
    o_ref[0] = jnp.concatenate(outs, axis=0).T.astype(o_ref.dtype)


def _moba(qb_t, kb3, vb_t, sel, rows):
    batch, nblk = qb_t.shape[0], qb_t.shape[1]
    seq = nblk * MOBA_BLOCK
    return pl.pallas_call(
        _moba_kernel,
        grid=(batch, nblk),
        in_specs=[
            pl.BlockSpec((1, 1, MOBA_W, MOBA_BLOCK), lambda b, i: (b, i, 0, 0)),
            pl.BlockSpec((1, 1, MOBA_W, MOBA_BLOCK), lambda b, i: (b, jnp.minimum(i + 1, nblk - 1), 0, 0)),
            pl.BlockSpec((1, seq, MOBA_W), lambda b, i: (b, 0, 0)),
            pl.BlockSpec((1, nblk, MOBA_W, MOBA_BLOCK), lambda b, i: (b, 0, 0, 0)),
            pl.BlockSpec((1, MOBA_HEADS, nblk, MOBA_BLOCK), lambda b, i: (b, 0, 0, i)),
            _resident(rows.shape, lambda b, i: (0, 0, 0)),
        ],
        out_specs=pl.BlockSpec((1, MOBA_BLOCK, MOBA_W), lambda b, i: (b, i, 0)),
        out_shape=jax.ShapeDtypeStruct((batch, seq, MOBA_W), BF16),
        scratch_shapes=[pltpu.VMEM((MOBA_HEADS, BIAS_SLOTS, MOBA_BLOCK, MOBA_BLOCK), F32),
                        pltpu.VMEM((MOBA_HEADS, 1, MOBA_BLOCK), F32),
                        pltpu.VMEM((MOBA_HEADS * ACC_ROWS, MOBA_BLOCK), F32),
                        pltpu.VMEM((MOBA_HEADS, MOBA_BLOCK, MOBA_BLOCK), F32),
                        pltpu.VMEM((MOBA_HEADS, 1, MOBA_BLOCK), F32)],
        compiler_params=pltpu.CompilerParams(dimension_semantics=("arbitrary", "arbitrary"),
                                             vmem_limit_bytes=VMEM_LIMIT_BYTES),
        name="moba",
    )(qb_t, qb_t, kb3, vb_t, sel, rows)


def _post_kernel(x_ref, ya_ref, yb_ref, ga_ref, gb_ref, gate1_ref, shift2_ref, scale2_ref,
                 gate2_ref, g2_ref, gf_ref, wa_ref, wb_ref, wo_ref, w1_ref, w2_ref, o_ref):
    a = _dot(ya_ref[...], wa_ref[...])
    b = _dot(yb_ref[...], wb_ref[...])
    merged = (jax.nn.sigmoid(ga_ref[...].astype(F32)) * a
              + jax.nn.sigmoid(gb_ref[...].astype(F32)) * b)
    x1 = x_ref[...] + gate1_ref[0, 0] * _dot(merged.astype(BF16), wo_ref[...])
    h2 = _rms_modulate(x1, g2_ref[...], shift2_ref[0, 0], scale2_ref[0, 0]).astype(BF16)
    y = jnp.zeros_like(x1)
    for c in range(D_FF // FF_CHUNK):
        cs = slice(c * FF_CHUNK, (c + 1) * FF_CHUNK)
        u = jnp.square(jnp.maximum(_dot(h2, w1_ref[:, cs]), 0.0)).astype(BF16)
        y = y + _dot(u, w2_ref[cs, :])
    x2 = x1 + gate2_ref[0, 0] * y
    ms = jnp.mean(x2 * x2, axis=-1, keepdims=True)
    o_ref[...] = (x2 * lax.rsqrt(ms + RMS_EPS)) * gf_ref[...]


def _post(x2, ya, yb, ga, gb, mod4, g2, gf, wa, wb, wo, w1, w2, seq):
    tokens = x2.shape[0]
    tiles_per_seq = seq // TOKEN_TILE
    tok = lambda w: pl.BlockSpec((TOKEN_TILE, w), lambda i: (i, 0))
    modrow = lambda k: pl.BlockSpec((1, 1, 1, D_MODEL), lambda i: (i // tiles_per_seq, k, 0, 0))
    full = lambda a: _resident(a.shape, lambda i: (0, 0))
    return pl.pallas_call(
        _post_kernel,
        grid=(tokens // TOKEN_TILE,),
        in_specs=[tok(D_MODEL), tok(SWA_Q_W), tok(MOBA_W), tok(D_MODEL), tok(D_MODEL),
                  modrow(2), modrow(3), modrow(4), modrow(5),
                  full(g2), full(gf), full(wa), full(wb), full(wo), full(w1), full(w2)],
        out_specs=tok(D_MODEL),
        out_shape=jax.ShapeDtypeStruct((tokens, D_MODEL), F32),
        compiler_params=pltpu.CompilerParams(dimension_semantics=("arbitrary",),
                                             vmem_limit_bytes=VMEM_LIMIT_BYTES),
        name="post",
    )(x2, ya, yb, ga, gb, mod4, mod4, mod4, mod4, g2, gf, wa, wb, wo, w1, w2)


def kernel(x, c, ada_w, ada_b, norm1_g, norm2_g, w_in, attn_sinks, rel_bias, w_branch_a,
           w_branch_b, w_out, w_mlp_in, w_mlp_out, final_g):
    batch, seq, _ = x.shape
    depth = ada_w.shape[0]
    nblk_b = seq // MOBA_BLOCK
    assert seq % TOKEN_TILE == 0 and TOKEN_TILE % MOBA_BLOCK == 0 and batch <= 8
    assert (BIAS_SLOTS - 1) * MOBA_BLOCK - (MOBA_BLOCK - 1) >= 1513 and MAX_DISTANCE == 2048
    assert 2 * SWA_BLOCK <= MOBA_BLOCK and SWA_WINDOW == SWA_BLOCK
    assert depth == 1
    l = 0

    rows = _bias_rows(rel_bias)
    c_t = jnp.zeros((D_MODEL, 8), F32).at[:, :batch].set(c.astype(F32).T)

    x2 = x.reshape(batch * seq, D_MODEL)
    mod = _ada(c_t, ada_w[l], ada_b[l][None, :], batch)
    mod4 = mod[:batch].reshape(batch, N_MOD, 1, D_MODEL)
    col_scale = jnp.ones((sum(IN_WIDTHS),), F32)
    for k in (0, 3):
        col_scale = col_scale.at[IN_OFFS[k]:IN_OFFS[k + 1]].set(ATTN_SCALE * LOG2E)
    w_bf16 = (w_in[l] * col_scale[None, :]).astype(BF16)
    later_weights = [w.astype(F32) for w in (w_branch_a[l], w_branch_b[l], w_out[l], w_mlp_in[l], w_mlp_out[l])]
    ka, kb, ga, gb, qa_t, va_t, qb_t, vb_t, wa, wb, wo, w1, w2 = _proj(
        x2, mod4, norm1_g[l][None, :], w_bf16, later_weights, batch, seq)

    sink_rows = jnp.broadcast_to(
        (attn_sinks[l].astype(F32) * LOG2E).reshape(SWA_KV_HEADS, 1, SWA_GROUP, 1),
        (SWA_KV_HEADS, 1, SWA_GROUP, SWA_BLOCK)).reshape(SWA_KV_HEADS, 1, SWA_GROUP * SWA_BLOCK)
    ya = _swa(qa_t, ka.reshape(batch, seq, SWA_KV_W), va_t, rows, sink_rows)

    kb3 = kb.reshape(batch, seq, MOBA_W)
    sel = _select(kb3, qb_t)
    yb = _moba(qb_t, kb3, vb_t, sel, rows)

    out = _post(x2, ya.reshape(batch * seq, SWA_Q_W), yb.reshape(batch * seq, MOBA_W), ga, gb,
                mod4, norm2_g[l][None, :], final_g[None, :], wa, wb, wo, w1, w2, seq)
    return out.reshape(batch, seq, D_MODEL)
```

```python
import functools
import math

import jax
import jax.numpy as jnp
from jax import lax
from jax.experimental import pallas as pl
from jax.experimental.pallas import tpu as pltpu

D_MODEL = 1024
HEAD_DIM = 64
ATTN_SCALE = HEAD_DIM ** -0.5
LOG2E = math.log2(math.e)
SWA_Q_HEADS = 8
SWA_KV_HEADS = 2
SWA_GROUP = SWA_Q_HEADS // SWA_KV_HEADS
SWA_WINDOW = 128
SWA_BLOCK = 128
MOBA_HEADS = 8
MOBA_BLOCK = 256
MOBA_TOPK = 3
NUM_BUCKETS = 32
MAX_EXACT = NUM_BUCKETS // 2
MAX_DISTANCE = 2048
N_ATTN_HEADS = SWA_Q_HEADS + MOBA_HEADS
SWA_Q_W = SWA_Q_HEADS * HEAD_DIM
SWA_KV_W = SWA_KV_HEADS * HEAD_DIM
MOBA_W = MOBA_HEADS * HEAD_DIM
D_FF = 4 * D_MODEL
N_MOD = 6
RMS_EPS = 1e-6

VMEM_LIMIT_BYTES = 56 * 1024 * 1024
TOKEN_TILE = 512
FF_CHUNK = 1024
BIAS_SLOTS = 8
SWA_UNROLL = 3
ACC_ROWS = HEAD_DIM + 16
NEG_INF = float("-inf")

F32 = jnp.float32
BF16 = jnp.bfloat16


def _resident(block_shape, index_map):
    return pl.BlockSpec(block_shape, index_map, pipeline_mode=pl.Buffered(1))


def _dot(a, b):
    return jnp.dot(a, b, preferred_element_type=F32)


def _dot_nt(a, b):
    return lax.dot_general(a, b, (((1,), (1,)), ((), ())), preferred_element_type=F32)


def _rms_modulate(xv, g, shift, scale):
    ms = jnp.mean(xv * xv, axis=-1, keepdims=True)
    y = xv * lax.rsqrt(ms + RMS_EPS)
    return (y * g) * (1.0 + scale) + shift


def _toeplitz(u, rows):
    wide = jnp.broadcast_to(u, (rows, 2 * rows))
    return pltpu.roll(wide, 0, 1, stride=1, stride_axis=0)[:, rows:]


def _ada_kernel(ct_ref, w_ref, b_ref, o_ref, *, batch):
    cs_t = jax.nn.silu(ct_ref[...])
    w = w_ref[...]
    rows = [jnp.sum(w * cs_t[:, b:b + 1], axis=0, keepdims=True) for b in range(batch)]
    rows.append(jnp.zeros((8 - batch, w.shape[1]), F32))
    o_ref[...] = jnp.concatenate(rows, axis=0) + b_ref[...]


def _ada(c_t, w, b, batch):
    n = w.shape[1]
    bn = 1536
    return pl.pallas_call(
        functools.partial(_ada_kernel, batch=batch),
        grid=(n // bn,),
        in_specs=[pl.BlockSpec((D_MODEL, 8), lambda j: (0, 0)),
                  pl.BlockSpec((D_MODEL, bn), lambda j: (0, j)),
                  pl.BlockSpec((1, bn), lambda j: (0, j))],
        out_specs=pl.BlockSpec((8, bn), lambda j: (0, j)),
        out_shape=jax.ShapeDtypeStruct((8, n), F32),
        compiler_params=pltpu.CompilerParams(dimension_semantics=("arbitrary",),
                                             vmem_limit_bytes=VMEM_LIMIT_BYTES),
        name="ada",
    )(c_t, w, b)


def _t5_bucket(dist):
    n = jnp.maximum(dist, 0)
    nf = jnp.maximum(n, 1).astype(F32)
    large = MAX_EXACT + (jnp.log(nf / MAX_EXACT) / math.log(MAX_DISTANCE / MAX_EXACT)
                         * (NUM_BUCKETS - MAX_EXACT)).astype(jnp.int32)
    large = jnp.minimum(large, NUM_BUCKETS - 1)
    return jnp.where(n < MAX_EXACT, n, large)


def _bias_rows_kernel(bucket_ref, rbt_ref, o_ref):
    bucket = bucket_ref[...]
    n = bucket.shape[1]
    onehot = jnp.where(lax.broadcasted_iota(jnp.int32, (NUM_BUCKETS, n), 0) == bucket, 1.0, 0.0)
    rows = jnp.dot(rbt_ref[...], onehot, preferred_element_type=F32,
                   precision=lax.Precision.HIGHEST)
    rows = jnp.where(bucket >= 0, rows * LOG2E, NEG_INF)
    for d in range(BIAS_SLOTS):
        o_ref[d] = rows[:, d * MOBA_BLOCK:(d + 2) * MOBA_BLOCK]


def _bias_rows(rel_bias):
    n = (BIAS_SLOTS + 1) * MOBA_BLOCK
    dist = jnp.arange(n, dtype=jnp.int32) - MOBA_BLOCK
    bucket = jnp.where(dist >= 0, _t5_bucket(dist), -1)[None, :]
    return pl.pallas_call(
        _bias_rows_kernel,
        out_shape=jax.ShapeDtypeStruct((BIAS_SLOTS, N_ATTN_HEADS, 2 * MOBA_BLOCK), F32),
        name="bias_rows",
    )(bucket, rel_bias.astype(F32).T)


IN_WIDTHS = (SWA_Q_W, SWA_KV_W, SWA_KV_W, MOBA_W, MOBA_W, MOBA_W, D_MODEL, D_MODEL)
IN_OFFS = tuple(sum(IN_WIDTHS[:k]) for k in range(len(IN_WIDTHS) + 1))
FEAT_COLS = (0, 2, 3, 5)
FEAT_W = sum(IN_WIDTHS[k] for k in FEAT_COLS)
TOK_COLS = (1, 4, 6, 7)
TOK_W = sum(IN_WIDTHS[k] for k in TOK_COLS)
Q_COLS = (0, 3)


def _proj_kernel(x_ref, shift_ref, scale_ref, g_ref, w_ref, *refs):
    n_cast = (len(refs) - 10) // 2
    cast_in, refs = refs[:n_cast], refs[n_cast:]
    ka_ref, kb_ref, ga_ref, gb_ref, qa_ref, va_ref, qb_ref, vb_ref = refs[:8]
    cast_out, (wk_ref, wt_ref) = refs[8:8 + n_cast], refs[8 + n_cast:]

    for src, dst in zip(cast_in, cast_out):
        dst[...] = src[...].astype(dst.dtype)

    @pl.when(pl.program_id(0) == 0)
    def _():
        o = 0
        for k in TOK_COLS:
            wk_ref[:, o:o + IN_WIDTHS[k]] = w_ref[:, IN_OFFS[k]:IN_OFFS[k + 1]].astype(BF16)
            o += IN_WIDTHS[k]
        o = 0
        for k in FEAT_COLS:
            wcol = w_ref[:, IN_OFFS[k]:IN_OFFS[k + 1]]
            if k in Q_COLS:
                wcol = wcol * (ATTN_SCALE * LOG2E)
            wt_ref[o:o + IN_WIDTHS[k], :] = wcol.T.astype(BF16)
            o += IN_WIDTHS[k]

    h = _rms_modulate(x_ref[...], g_ref[...], shift_ref[0, 0], scale_ref[0, 0]).astype(BF16)
    o = 0
    for ref, k in zip((ka_ref, kb_ref, ga_ref, gb_ref), TOK_COLS):
        ref[...] = _dot(h, wk_ref[:, o:o + IN_WIDTHS[k]]).astype(ref.dtype)
        o += IN_WIDTHS[k]
    o = 0
    for ref, k in zip((qa_ref, va_ref, qb_ref, vb_ref), FEAT_COLS):
        w = IN_WIDTHS[k]
        ft = _dot_nt(wt_ref[o:o + w, :], h).astype(ref.dtype)
        blk = ref.shape[-1]
        for t in range(TOKEN_TILE // blk):
            ref[0, t] = ft[:, t * blk:(t + 1) * blk]
        o += w


def _proj(x2, mod4, g1, w_in, cast_weights, batch, seq):
    tokens = x2.shape[0]
    steps = tokens // TOKEN_TILE
    assert all(w.shape[0] % (16 * steps) == 0 for w in cast_weights)
    cast_specs = [pl.BlockSpec((w.shape[0] // steps, w.shape[1]), lambda i: (i, 0)) for w in cast_weights]
    tiles_per_seq = seq // TOKEN_TILE
    na = TOKEN_TILE // SWA_BLOCK
    nb = TOKEN_TILE // MOBA_BLOCK

    def feat_spec(width, blk, per_tile):
        return pl.BlockSpec((1, per_tile, width, blk),
                            lambda i: (i // tiles_per_seq, i % tiles_per_seq, 0, 0))

    out_shape = (
        jax.ShapeDtypeStruct((tokens, SWA_KV_W), BF16),
        jax.ShapeDtypeStruct((tokens, MOBA_W), BF16),
        jax.ShapeDtypeStruct((tokens, D_MODEL), BF16),
        jax.ShapeDtypeStruct((tokens, D_MODEL), BF16),
        jax.ShapeDtypeStruct((batch, seq // SWA_BLOCK, SWA_Q_W, SWA_BLOCK), BF16),
        jax.ShapeDtypeStruct((batch, seq // SWA_BLOCK, SWA_KV_W, SWA_BLOCK), BF16),
        jax.ShapeDtypeStruct((batch, seq // MOBA_BLOCK, MOBA_W, MOBA_BLOCK), BF16),
        jax.ShapeDtypeStruct((batch, seq // MOBA_BLOCK, MOBA_W, MOBA_BLOCK), BF16),
    )
    tok_spec = lambda w: pl.BlockSpec((TOKEN_TILE, w), lambda i: (i, 0))
    return pl.pallas_call(
        _proj_kernel,
        grid=(steps,),
        in_specs=[
            pl.BlockSpec((TOKEN_TILE, D_MODEL), lambda i: (i, 0)),
            pl.BlockSpec((1, 1, 1, D_MODEL), lambda i: (i // tiles_per_seq, 0, 0, 0)),
            pl.BlockSpec((1, 1, 1, D_MODEL), lambda i: (i // tiles_per_seq, 1, 0, 0)),
            _resident((1, D_MODEL), lambda i: (0, 0)),
            _resident(w_in.shape, lambda i: (0, 0)),
        ] + cast_specs,
        out_specs=(tok_spec(SWA_KV_W), tok_spec(MOBA_W), tok_spec(D_MODEL), tok_spec(D_MODEL),
                   feat_spec(SWA_Q_W, SWA_BLOCK, na), feat_spec(SWA_KV_W, SWA_BLOCK, na),
                   feat_spec(MOBA_W, MOBA_BLOCK, nb), feat_spec(MOBA_W, MOBA_BLOCK, nb))
        + tuple(cast_specs),
        out_shape=out_shape + tuple(jax.ShapeDtypeStruct(w.shape, BF16) for w in cast_weights),
        scratch_shapes=[pltpu.VMEM((D_MODEL, TOK_W), BF16), pltpu.VMEM((FEAT_W, D_MODEL), BF16)],
        compiler_params=pltpu.CompilerParams(dimension_semantics=("arbitrary",),
                                             vmem_limit_bytes=VMEM_LIMIT_BYTES),
        name="proj",
    )(x2, mod4, mod4, g1, w_in, *cast_weights)


def _swa_kernel(q_ref, k_ref, v_ref, rows_ref, sink_ref, o_ref, bias_ref, ot_ref, s_ref):
    L = SWA_BLOCK
    nblk = q_ref.shape[1]

    @pl.when(pl.program_id(0) == 0)
    def _():
        k_idx = lax.broadcasted_iota(jnp.int32, (L, L), 0)
        q_idx = lax.broadcasted_iota(jnp.int32, (L, L), 1)
        for h in range(SWA_Q_HEADS):
            g, gi = divmod(h, SWA_GROUP)
            t0 = rows_ref[0, h:h + 1, MOBA_BLOCK:MOBA_BLOCK + L]
            r = _toeplitz(jnp.concatenate([t0, t0], axis=1), L)
            cols = slice(gi * L, (gi + 1) * L)
            bias_ref[g, 0:L, cols] = jnp.where(q_idx < k_idx, r, NEG_INF)
            bias_ref[g, L:2 * L, cols] = jnp.where(q_idx >= k_idx, r, NEG_INF)

    ones_blk = jnp.concatenate([jnp.ones((1, 2 * L), BF16), jnp.zeros((ACC_ROWS - HEAD_DIM - 1, 2 * L), BF16)],
                               axis=0)
    PW = 2 * L
    chains = [(g, hp) for g in range(SWA_KV_HEADS) for hp in range(SWA_GROUP // 2)]
    kv_cols = [slice(g * HEAD_DIM, (g + 1) * HEAD_DIM) for g in range(SWA_KV_HEADS)]

    def q_pair(c, g, hp):
        h0 = g * SWA_GROUP + 2 * hp
        return jnp.concatenate([q_ref[0, c, (h0 + e) * HEAD_DIM:(h0 + e + 1) * HEAD_DIM, :] for e in range(2)],
                               axis=1)

    def scores_first(idx):
        g, hp = chains[idx]
        lanes = slice(hp * PW, (hp + 1) * PW)
        s_ref[idx, 0:L, :] = jnp.full((L, PW), NEG_INF, F32)
        s_ref[idx, L:2 * L, :] = _dot(k_ref[0, 0:L, kv_cols[g]], q_pair(0, g, hp)) + bias_ref[g, L:2 * L, lanes]

    def scores(c, idx):
        g, hp = chains[idx]
        kw = k_ref[0, pl.ds(pl.multiple_of((c - 1) * L, L), 2 * L), kv_cols[g]]
        s_ref[idx] = _dot(kw, q_pair(c, g, hp)) + bias_ref[g, :, hp * PW:(hp + 1) * PW]

    def attend(c, c_prev, idx):
        g, hp = chains[idx]
        hs = kv_cols[g]
        s = s_ref[idx]
        sink = sink_ref[g, :, hp * PW:(hp + 1) * PW]
        m = jnp.maximum(jnp.max(s, axis=0, keepdims=True), sink)
        p = jnp.exp2(s - m).astype(BF16)
        v_win = jnp.concatenate([v_ref[0, c_prev, hs, :], v_ref[0, c, hs, :]], axis=1)
        pv = _dot(jnp.concatenate([v_win, ones_blk], axis=0), p)
        o = pv[0:HEAD_DIM] / (pv[HEAD_DIM:HEAD_DIM + 1] + jnp.exp2(sink - m))
        for e in range(2):
            r = (g * SWA_GROUP + 2 * hp + e) * HEAD_DIM
            ot_ref[r:r + HEAD_DIM, :] = o[:, e * L:(e + 1) * L]

    def emit(c):
        o_ref[0, pl.ds(pl.multiple_of(c * L, L), L), :] = ot_ref[...].T.astype(o_ref.dtype)

    for idx in range(len(chains)):
        scores_first(idx)
    for idx in range(len(chains)):
        attend(0, 0, idx)
        scores(1, idx)
    emit(0)

    def body(c, carry):
        for idx in range(len(chains)):
            attend(c, c - 1, idx)
            scores(c + 1, idx)
        emit(c)
        return carry

    def body_unrolled(u, carry):
        for e in range(SWA_UNROLL):
            carry = body(SWA_UNROLL * u + 1 + e, carry)
        return carry

    lax.fori_loop(0, (nblk - 2) // SWA_UNROLL, body_unrolled, 0)
    for idx in range(len(chains)):
        attend(nblk - 1, nblk - 2, idx)
    emit(nblk - 1)


def _swa(qa_t, ka3, va_t, rows, sink_rows):
    batch, nblk = qa_t.shape[0], qa_t.shape[1]
    seq = nblk * SWA_BLOCK
    assert nblk >= 2 and (nblk - 2) % SWA_UNROLL == 0
    return pl.pallas_call(
        _swa_kernel,
        grid=(batch,),
        in_specs=[
            pl.BlockSpec((1, nblk, SWA_Q_W, SWA_BLOCK), lambda b: (b, 0, 0, 0)),
            pl.BlockSpec((1, seq, SWA_KV_W), lambda b: (b, 0, 0)),
            pl.BlockSpec((1, nblk, SWA_KV_W, SWA_BLOCK), lambda b: (b, 0, 0, 0)),
            _resident((1, N_ATTN_HEADS, 2 * MOBA_BLOCK), lambda b: (0, 0, 0)),
            _resident(sink_rows.shape, lambda b: (0, 0, 0)),
        ],
        out_specs=pl.BlockSpec((1, seq, SWA_Q_W), lambda b: (b, 0, 0)),
        out_shape=jax.ShapeDtypeStruct((batch, seq, SWA_Q_W), BF16),
        scratch_shapes=[pltpu.VMEM((SWA_KV_HEADS, 2 * SWA_BLOCK, SWA_GROUP * SWA_BLOCK), F32),
                        pltpu.VMEM((SWA_Q_W, SWA_BLOCK), F32),
                        pltpu.VMEM((SWA_Q_HEADS // 2, 2 * SWA_BLOCK, 2 * SWA_BLOCK), F32)],
        compiler_params=pltpu.CompilerParams(dimension_semantics=("arbitrary",),
                                             vmem_limit_bytes=VMEM_LIMIT_BYTES),
        name="swa",
    )(qa_t, ka3, va_t, rows, sink_rows)


def _select_kernel(k_ref, q_ref, o_ref):
    seq = k_ref.shape[1]
    nblk = seq // MOBA_BLOCK
    blk = lax.broadcasted_iota(jnp.int32, (nblk, seq), 0)
    qblk = lax.broadcasted_iota(jnp.int32, (nblk, seq), 1) // MOBA_BLOCK
    past = blk < qblk
    member = jnp.where(blk == qblk, 1.0 / MOBA_BLOCK, 0.0).astype(BF16)
    for hh in range(2):
        hs = slice(hh * HEAD_DIM, (hh + 1) * HEAD_DIM)
        kmean = _dot(member, k_ref[0, :, hs])
        q_t = jnp.concatenate([q_ref[0, t, hs, :] for t in range(nblk)], axis=1)
        k1 = kmean.astype(BF16)
        r1 = kmean - k1.astype(F32)
        k2 = r1.astype(BF16)
        k3 = (r1 - k2.astype(F32)).astype(BF16)
        parts = _dot(jnp.concatenate([k1, k2, k3], axis=0), q_t)
        gate = parts[0:nblk] + parts[nblk:2 * nblk] + parts[2 * nblk:3 * nblk]
        gate = jnp.where(past, gate, NEG_INF)
        mask = jnp.where(blk == qblk, 0.0, NEG_INF)
        blk_f = blk.astype(F32)
        for _ in range(MOBA_TOPK):
            top = jnp.max(gate, axis=0, keepdims=True)
            first = jnp.min(jnp.where(gate == top, blk_f, float(nblk)), axis=0, keepdims=True)
            pick = blk_f == first
            mask = jnp.where(jnp.logical_and(pick, past), 0.0, mask)
            gate = jnp.where(pick, NEG_INF, gate)
        o_ref[0, hh] = mask


def _select(kb3, qb_t):
    batch, seq = kb3.shape[0], kb3.shape[1]
    nblk = seq // MOBA_BLOCK
    pairs = MOBA_HEADS // 2
    return pl.pallas_call(
        _select_kernel,
        grid=(batch, pairs),
        in_specs=[
            pl.BlockSpec((1, seq, 2 * HEAD_DIM), lambda b, hp: (b, 0, hp)),
            pl.BlockSpec((1, nblk, 2 * HEAD_DIM, MOBA_BLOCK), lambda b, hp: (b, 0, hp, 0)),
        ],
        out_specs=pl.BlockSpec((1, 2, nblk, seq), lambda b, hp: (b, hp, 0, 0)),
        out_shape=jax.ShapeDtypeStruct((batch, MOBA_HEADS, nblk, seq), F32),
        compiler_params=pltpu.CompilerParams(dimension_semantics=("arbitrary", "arbitrary"),
                                             vmem_limit_bytes=VMEM_LIMIT_BYTES),
        name="moba_select",
    )(kb3, qb_t)


def _moba_kernel(q_ref, qn_ref, k_ref, v_ref, sel_ref, rows_ref, o_ref, bias_ref, m_ref, acc_ref, s_ref, cm_ref):
    i = pl.program_id(1)
    nblk = pl.num_programs(1)
    MB = MOBA_BLOCK

    @pl.when(jnp.logical_and(pl.program_id(0) == 0, i == 0))
    def _():
        for h in range(MOBA_HEADS):
            for d in range(BIAS_SLOTS):
                bias_ref[h, d] = _toeplitz(rows_ref[d, SWA_Q_HEADS + h:SWA_Q_HEADS + h + 1, :], MB)

    m_ref[...] = jnp.full(m_ref.shape, NEG_INF, F32)
    acc_ref[...] = jnp.zeros(acc_ref.shape, F32)
    heads = [slice(h * HEAD_DIM, (h + 1) * HEAD_DIM) for h in range(MOBA_HEADS)]
    ones_blk = jnp.concatenate([jnp.ones((1, MB), BF16), jnp.zeros((ACC_ROWS - HEAD_DIM - 1, MB), BF16)],
                               axis=0)

    def scores(j, slot, h, hs):
        rows = pl.ds(pl.multiple_of(j * MB, MB), MB)
        s = _dot(k_ref[0, rows, hs], q_ref[0, 0, hs, :]) + bias_ref[h, slot]
        s_ref[h] = s
        cm_ref[h] = jnp.max(s, axis=0, keepdims=True) + sel_ref[0, h, pl.ds(j, 1), :]

    def own_scores(blk, qr, h, hs):
        rows = pl.ds(pl.multiple_of(blk * MB, MB), MB)
        s = _dot(k_ref[0, rows, hs], qr[0, 0, hs, :]) + bias_ref[h, 0]
        s_ref[h] = s
        cm_ref[h] = jnp.max(s, axis=0, keepdims=True)

    def accumulate(j, h, hs):
        m_old = m_ref[h]
        m_new = jnp.maximum(m_old, cm_ref[h])
        alpha = jnp.exp2(m_old - m_new)
        p = jnp.exp2(s_ref[h] - (m_new - sel_ref[0, h, pl.ds(j, 1), :])).astype(BF16)
        v_aug = jnp.concatenate([v_ref[0, j, hs, :], ones_blk], axis=0)
        rs = slice(h * ACC_ROWS, (h + 1) * ACC_ROWS)
        acc_ref[rs, :] = alpha * acc_ref[rs, :] + _dot(v_aug, p)
        m_ref[h] = m_new

    @pl.when(i == 0)
    def _():
        for h, hs in enumerate(heads):
            own_scores(i, q_ref, h, hs)

    def body(t, carry):
        j = i - t
        slot = jnp.minimum(t + 1, BIAS_SLOTS - 1)
        for h, hs in enumerate(heads):
            accumulate(j, h, hs)
            scores(j - 1, slot, h, hs)
        return carry

    def body2(u, carry):
        return body(2 * u + 1, body(2 * u, carry))

    lax.fori_loop(0, i // 2, body2, 0)

    @pl.when(i % 2 == 1)
    def _():
        body(i - 1, 0)

    @pl.when(i < nblk - 1)
    def _():
        for h, hs in enumerate(heads):
            accumulate(0, h, hs)
            own_scores(i + 1, qn_ref, h, hs)

    @pl.when(i == nblk - 1)
    def _():
        for h, hs in enumerate(heads):
            accumulate(0, h, hs)

    outs = []
    for h, hs in enumerate(heads):
        base = h * ACC_ROWS
        outs.append(acc_ref[base:base + HEAD_DIM, :] / acc_ref[base + HEAD_DIM:base + HEAD_DIM + 1, :])
    o_ref[0] = jnp.concatenate(outs, axis=0).T.astype(o_ref.dtype)


def _moba(qb_t, kb3, vb_t, sel, rows):
    batch, nblk = qb_t.shape[0], qb_t.shape[1]
    seq = nblk * MOBA_BLOCK
    return pl.pallas_call(
        _moba_kernel,
        grid=(batch, nblk),
        in_specs=[
            pl.BlockSpec((1, 1, MOBA_W, MOBA_BLOCK), lambda b, i: (b, i, 0, 0)),
            pl.BlockSpec((1, 1, MOBA_W, MOBA_BLOCK), lambda b, i: (b, jnp.minimum(i + 1, nblk - 1), 0, 0)),
            pl.BlockSpec((1, seq, MOBA_W), lambda b, i: (b, 0, 0)),
            pl.BlockSpec((1, nblk, MOBA_W, MOBA_BLOCK), lambda b, i: (b, 0, 0, 0)),
            pl.BlockSpec((1, MOBA_HEADS, nblk, MOBA_BLOCK), lambda b, i: (b, 0, 0, i)),
            _resident(rows.shape, lambda b, i: (0, 0, 0)),
        ],
        out_specs=pl.BlockSpec((1, MOBA_BLOCK, MOBA_W), lambda b, i: (b, i, 0)),
        out_shape=jax.ShapeDtypeStruct((batch, seq, MOBA_W), BF16),
        scratch_shapes=[pltpu.VMEM((MOBA_HEADS, BIAS_SLOTS, MOBA_BLOCK, MOBA_BLOCK), F32),
                        pltpu.VMEM((MOBA_HEADS, 1, MOBA_BLOCK), F32),
                        pltpu.VMEM((MOBA_HEADS * ACC_ROWS, MOBA_BLOCK), F32),
                        pltpu.VMEM((MOBA_HEADS, MOBA_BLOCK, MOBA_BLOCK), F32),
                        pltpu.VMEM((MOBA_HEADS, 1, MOBA_BLOCK), F32)],
        compiler_params=pltpu.CompilerParams(dimension_semantics=("arbitrary", "arbitrary"),
                                             vmem_limit_bytes=VMEM_LIMIT_BYTES),
        name="moba",
    )(qb_t, qb_t, kb3, vb_t, sel, rows)


def _post_kernel(x_ref, ya_ref, yb_ref, ga_ref, gb_ref, gate1_ref, shift2_ref, scale2_ref,
                 gate2_ref, g2_ref, gf_ref, wa_ref, wb_ref, wo_ref, w1_ref, w2_ref, o_ref):
    a = _dot(ya_ref[...], wa_ref[...])
    b = _dot(yb_ref[...], wb_ref[...])
    merged = (jax.nn.sigmoid(ga_ref[...].astype(F32)) * a
              + jax.nn.sigmoid(gb_ref[...].astype(F32)) * b)
    x1 = x_ref[...] + gate1_ref[0, 0] * _dot(merged.astype(BF16), wo_ref[...])
    h2 = _rms_modulate(x1, g2_ref[...], shift2_ref[0, 0], scale2_ref[0, 0]).astype(BF16)
    y = jnp.zeros_like(x1)
    for c in range(D_FF // FF_CHUNK):
        cs = slice(c * FF_CHUNK, (c + 1) * FF_CHUNK)
        u = jnp.square(jnp.maximum(_dot(h2, w1_ref[:, cs]), 0.0)).astype(BF16)
        y = y + _dot(u, w2_ref[cs, :])
    x2 = x1 + gate2_ref[0, 0] * y
    ms = jnp.mean(x2 * x2, axis=-1, keepdims=True)
    o_ref[...] = (x2 * lax.rsqrt(ms + RMS_EPS)) * gf_ref[...]


def _post(x2, ya, yb, ga, gb, mod4, g2, gf, wa, wb, wo, w1, w2, seq):
    tokens = x2.shape[0]
    tiles_per_seq = seq // TOKEN_TILE
    tok = lambda w: pl.BlockSpec((TOKEN_TILE, w), lambda i: (i, 0))
    modrow = lambda k: pl.BlockSpec((1, 1, 1, D_MODEL), lambda i: (i // tiles_per_seq, k, 0, 0))
    full = lambda a: _resident(a.shape, lambda i: (0, 0))
    return pl.pallas_call(
        _post_kernel,
        grid=(tokens // TOKEN_TILE,),
        in_specs=[tok(D_MODEL), tok(SWA_Q_W), tok(MOBA_W), tok(D_MODEL), tok(D_MODEL),
                  modrow(2), modrow(3), modrow(4), modrow(5),
                  full(g2), full(gf), full(wa), full(wb), full(wo), full(w1), full(w2)],
        out_specs=tok(D_MODEL),
        out_shape=jax.ShapeDtypeStruct((tokens, D_MODEL), F32),
        compiler_params=pltpu.CompilerParams(dimension_semantics=("arbitrary",),
                                             vmem_limit_bytes=VMEM_LIMIT_BYTES),
        name="post",
    )(x2, ya, yb, ga, gb, mod4, mod4, mod4, mod4, g2, gf, wa, wb, wo, w1, w2)


def kernel(x, c, ada_w, ada_b, norm1_g, norm2_g, w_in, attn_sinks, rel_bias, w_branch_a,
           w_branch_b, w_out, w_mlp_in, w_mlp_out, final_g):
    batch, seq, _ = x.shape
    depth = ada_w.shape[0]
    nblk_b = seq // MOBA_BLOCK
    assert seq % TOKEN_TILE == 0 and TOKEN_TILE % MOBA_BLOCK == 0 and batch <= 8
    assert (BIAS_SLOTS - 1) * MOBA_BLOCK - (MOBA_BLOCK - 1) >= 1513 and MAX_DISTANCE == 2048
    assert 2 * SWA_BLOCK <= MOBA_BLOCK and SWA_WINDOW == SWA_BLOCK
    assert depth == 1
    l = 0

    rows = _bias_rows(rel_bias)
    c_t = jnp.zeros((D_MODEL, 8), F32).at[:, :batch].set(c.astype(F32).T)

    x2 = x.reshape(batch * seq, D_MODEL)
    mod = _ada(c_t, ada_w[l], ada_b[l][None, :], batch)
    mod4 = mod[:batch].reshape(batch, N_MOD, 1, D_MODEL)
    later_weights = [w.astype(F32) for w in (w_branch_a[l], w_branch_b[l], w_out[l], w_mlp_in[l], w_mlp_out[l])]
    ka, kb, ga, gb, qa_t, va_t, qb_t, vb_t, wa, wb, wo, w1, w2 = _proj(
        x2, mod4, norm1_g[l][None, :], w_in[l].astype(F32), later_weights, batch, seq)

    sink_rows = jnp.broadcast_to(
        (attn_sinks[l].astype(F32) * LOG2E).reshape(SWA_KV_HEADS, 1, SWA_GROUP, 1),
        (SWA_KV_HEADS, 1, SWA_GROUP, SWA_BLOCK)).reshape(SWA_KV_HEADS, 1, SWA_GROUP * SWA_BLOCK)
    ya = _swa(qa_t, ka.reshape(batch, seq, SWA_KV_W), va_t, rows, sink_rows)

    kb3 = kb.reshape(batch, seq, MOBA_W)
    sel = _select(kb3, qb_t)
    yb = _moba(qb_t, kb3, vb_t, sel, rows)

    out = _post(x2, ya.reshape(batch * seq, SWA_Q_W), yb.reshape(batch * seq, MOBA_W), ga, gb,
                mod4, norm2_g[l][None, :], final_g[None, :], wa, wb, wo, w1, w2, seq)
    return out.reshape(batch, seq, D_MODEL)
```

```python
import functools
import math

import jax
import jax.numpy as jnp
from jax import lax
from jax.experimental import pallas as pl
from jax.experimental.pallas import tpu as pltpu

D_MODEL = 1024
HEAD_DIM = 64
ATTN_SCALE = HEAD_DIM ** -0.5
LOG2E = math.log2(math.e)
SWA_Q_HEADS = 8
SWA_KV_HEADS = 2
SWA_GROUP = SWA_Q_HEADS // SWA_KV_HEADS
SWA_WINDOW = 128
SWA_BLOCK = 128
MOBA_HEADS = 8
MOBA_BLOCK = 256
MOBA_TOPK = 3
NUM_BUCKETS = 32
MAX_EXACT = NUM_BUCKETS // 2
MAX_DISTANCE = 2048
N_ATTN_HEADS = SWA_Q_HEADS + MOBA_HEADS
SWA_Q_W = SWA_Q_HEADS * HEAD_DIM
SWA_KV_W = SWA_KV_HEADS * HEAD_DIM
MOBA_W = MOBA_HEADS * HEAD_DIM
D_FF = 4 * D_MODEL
N_MOD = 6
RMS_EPS = 1e-6

VMEM_LIMIT_BYTES = 56 * 1024 * 1024
TOKEN_TILE = 512
FF_CHUNK = 1024
BIAS_SLOTS = 8
SWA_UNROLL = 3
ACC_ROWS = HEAD_DIM + 16
NEG_INF = float("-inf")

F32 = jnp.float32
BF16 = jnp.bfloat16


def _resident(block_shape, index_map):
    return pl.BlockSpec(block_shape, index_map, pipeline_mode=pl.Buffered(1))


def _dot(a, b):
    return jnp.dot(a, b, preferred_element_type=F32)


def _dot_nt(a, b):
    return lax.dot_general(a, b, (((1,), (1,)), ((), ())), preferred_element_type=F32)


def _rms_modulate(xv, g, shift, scale):
    ms = jnp.mean(xv * xv, axis=-1, keepdims=True)
    y = xv * lax.rsqrt(ms + RMS_EPS)
    return (y * g) * (1.0 + scale) + shift


def _toeplitz(u, rows):
    wide = jnp.broadcast_to(u, (rows, 2 * rows))
    return pltpu.roll(wide, 0, 1, stride=1, stride_axis=0)[:, rows:]


def _ada_kernel(ct_ref, w_ref, b_ref, o_ref, *, batch):
    cs_t = jax.nn.silu(ct_ref[...])
    w = w_ref[...]
    rows = [jnp.sum(w * cs_t[:, b:b + 1], axis=0, keepdims=True) for b in range(batch)]
    rows.append(jnp.zeros((8 - batch, w.shape[1]), F32))
    o_ref[...] = jnp.concatenate(rows, axis=0) + b_ref[...]


def _ada(c_t, w, b, batch):
    n = w.shape[1]
    bn = 1536
    return pl.pallas_call(
        functools.partial(_ada_kernel, batch=batch),
        grid=(n // bn,),
        in_specs=[pl.BlockSpec((D_MODEL, 8), lambda j: (0, 0)),
                  pl.BlockSpec((D_MODEL, bn), lambda j: (0, j)),
                  pl.BlockSpec((1, bn), lambda j: (0, j))],
        out_specs=pl.BlockSpec((8, bn), lambda j: (0, j)),
        out_shape=jax.ShapeDtypeStruct((8, n), F32),
        compiler_params=pltpu.CompilerParams(dimension_semantics=("arbitrary",),
                                             vmem_limit_bytes=VMEM_LIMIT_BYTES),
        name="ada",
    )(c_t, w, b)


def _t5_bucket(dist):
    n = jnp.maximum(dist, 0)
    nf = jnp.maximum(n, 1).astype(F32)
    large = MAX_EXACT + (jnp.log(nf / MAX_EXACT) / math.log(MAX_DISTANCE / MAX_EXACT)
                         * (NUM_BUCKETS - MAX_EXACT)).astype(jnp.int32)
    large = jnp.minimum(large, NUM_BUCKETS - 1)
    return jnp.where(n < MAX_EXACT, n, large)


def _bias_rows_kernel(bucket_ref, rbt_ref, o_ref):
    bucket = bucket_ref[...]
    n = bucket.shape[1]
    onehot = jnp.where(lax.broadcasted_iota(jnp.int32, (NUM_BUCKETS, n), 0) == bucket, 1.0, 0.0)
    rows = jnp.dot(rbt_ref[...], onehot, preferred_element_type=F32,
                   precision=lax.Precision.HIGHEST)
    rows = jnp.where(bucket >= 0, rows * LOG2E, NEG_INF)
    for d in range(BIAS_SLOTS):
        o_ref[d] = rows[:, d * MOBA_BLOCK:(d + 2) * MOBA_BLOCK]


def _bias_rows(rel_bias):
    n = (BIAS_SLOTS + 1) * MOBA_BLOCK
    dist = jnp.arange(n, dtype=jnp.int32) - MOBA_BLOCK
    bucket = jnp.where(dist >= 0, _t5_bucket(dist), -1)[None, :]
    return pl.pallas_call(
        _bias_rows_kernel,
        out_shape=jax.ShapeDtypeStruct((BIAS_SLOTS, N_ATTN_HEADS, 2 * MOBA_BLOCK), F32),
        name="bias_rows",
    )(bucket, rel_bias.astype(F32).T)


IN_WIDTHS = (SWA_Q_W, SWA_KV_W, SWA_KV_W, MOBA_W, MOBA_W, MOBA_W, D_MODEL, D_MODEL)
IN_OFFS = tuple(sum(IN_WIDTHS[:k]) for k in range(len(IN_WIDTHS) + 1))
FEAT_COLS = (0, 2, 3, 5)
FEAT_W = sum(IN_WIDTHS[k] for k in FEAT_COLS)
TOK_COLS = (1, 4, 6, 7)
TOK_W = sum(IN_WIDTHS[k] for k in TOK_COLS)
Q_COLS = (0, 3)


def _proj_kernel(x_ref, shift_ref, scale_ref, g_ref, w_ref, *refs):
    n_cast = (len(refs) - 12) // 2
    cast_in, refs = refs[:n_cast], refs[n_cast:]
    ka_ref, kb_ref, ga_ref, gb_ref, qa_ref, va_ref, qb_ref, vb_ref, sel_ref = refs[:9]
    cast_out, (wk_ref, wt_ref, kmean_ref) = refs[9:9 + n_cast], refs[9 + n_cast:]

    for src, dst in zip(cast_in, cast_out):
        dst[...] = src[...].astype(dst.dtype)

    tile = pl.program_id(0) % (kmean_ref.shape[0] // BLOCKS_PER_TILE)

    @pl.when(tile == 0)
    def _():
        kmean_ref[...] = jnp.zeros(kmean_ref.shape, F32)

    @pl.when(pl.program_id(0) == 0)
    def _():
        o = 0
        for k in TOK_COLS:
            wk_ref[:, o:o + IN_WIDTHS[k]] = w_ref[:, IN_OFFS[k]:IN_OFFS[k + 1]].astype(BF16)
            o += IN_WIDTHS[k]
        o = 0
        for k in FEAT_COLS:
            wcol = w_ref[:, IN_OFFS[k]:IN_OFFS[k + 1]]
            if k in Q_COLS:
                wcol = wcol * (ATTN_SCALE * LOG2E)
            wt_ref[o:o + IN_WIDTHS[k], :] = wcol.T.astype(BF16)
            o += IN_WIDTHS[k]

    h = _rms_modulate(x_ref[...], g_ref[...], shift_ref[0, 0], scale_ref[0, 0]).astype(BF16)
    tok_refs = dict(zip(TOK_COLS, (ka_ref, kb_ref, ga_ref, gb_ref)))
    feat_refs = dict(zip(FEAT_COLS, (qa_ref, va_ref, qb_ref, vb_ref)))

    def token_major(k):
        o = sum(IN_WIDTHS[c] for c in TOK_COLS[:TOK_COLS.index(k)])
        out = _dot(h, wk_ref[:, o:o + IN_WIDTHS[k]])
        tok_refs[k][...] = out.astype(BF16)
        return out

    def feature_major(k):
        o = sum(IN_WIDTHS[c] for c in FEAT_COLS[:FEAT_COLS.index(k)])
        out = _dot_nt(wt_ref[o:o + IN_WIDTHS[k], :], h).astype(BF16)
        blk = feat_refs[k].shape[-1]
        for t in range(TOKEN_TILE // blk):
            feat_refs[k][0, t] = out[:, t * blk:(t + 1) * blk]
        return out

    _update_kmean(token_major(4), kmean_ref, tile)
    qb_t = feature_major(3)
    token_major(1)
    token_major(6)
    _moba_select(qb_t, sel_ref, kmean_ref, tile)
    token_major(7)
    for k in (0, 2, 5):
        feature_major(k)


BLOCKS_PER_TILE = TOKEN_TILE // MOBA_BLOCK


def _update_kmean(kb, kmean_ref, tile):
    for t in range(BLOCKS_PER_TILE):
        kmean_ref[pl.ds(tile * BLOCKS_PER_TILE + t, 1), :] = jnp.mean(
            kb[t * MOBA_BLOCK:(t + 1) * MOBA_BLOCK, :], axis=0, keepdims=True)


def _moba_select(qb_t, sel_ref, kmean_ref, tile):
    nblk = kmean_ref.shape[0]
    blocks_per_tile = BLOCKS_PER_TILE
    blk = lax.broadcasted_iota(jnp.int32, (nblk, TOKEN_TILE), 0)
    qblk = tile * blocks_per_tile + lax.broadcasted_iota(jnp.int32, (nblk, TOKEN_TILE), 1) // MOBA_BLOCK
    past = blk < qblk
    own = jnp.where(blk == qblk, 0.0, NEG_INF)
    blk_f = blk.astype(F32)
    for h in range(MOBA_HEADS):
        hs = slice(h * HEAD_DIM, (h + 1) * HEAD_DIM)
        kmean = kmean_ref[:, hs]
        k1 = kmean.astype(BF16)
        r1 = kmean - k1.astype(F32)
        k2 = r1.astype(BF16)
        k3 = (r1 - k2.astype(F32)).astype(BF16)
        parts = _dot(jnp.concatenate([k1, k2, k3], axis=0), qb_t[hs, :])
        gate = parts[0:nblk] + parts[nblk:2 * nblk] + parts[2 * nblk:3 * nblk]
        gate = jnp.where(past, gate, NEG_INF)
        mask = own
        for _ in range(MOBA_TOPK):
            top = jnp.max(gate, axis=0, keepdims=True)
            first = jnp.min(jnp.where(gate == top, blk_f, float(nblk)), axis=0, keepdims=True)
            pick = blk_f == first
            mask = jnp.where(jnp.logical_and(pick, past), 0.0, mask)
            gate = jnp.where(pick, NEG_INF, gate)
        sel_ref[0, h] = mask


def _proj(x2, mod4, g1, w_in, cast_weights, batch, seq):
    tokens = x2.shape[0]
    steps = tokens // TOKEN_TILE
    assert all(w.shape[0] % (16 * steps) == 0 for w in cast_weights)
    cast_specs = [pl.BlockSpec((w.shape[0] // steps, w.shape[1]), lambda i: (i, 0)) for w in cast_weights]
    tiles_per_seq = seq // TOKEN_TILE
    na = TOKEN_TILE // SWA_BLOCK
    nb = TOKEN_TILE // MOBA_BLOCK

    def feat_spec(width, blk, per_tile):
        return pl.BlockSpec((1, per_tile, width, blk),
                            lambda i: (i // tiles_per_seq, i % tiles_per_seq, 0, 0))

    out_shape = (
        jax.ShapeDtypeStruct((tokens, SWA_KV_W), BF16),
        jax.ShapeDtypeStruct((tokens, MOBA_W), BF16),
        jax.ShapeDtypeStruct((tokens, D_MODEL), BF16),
        jax.ShapeDtypeStruct((tokens, D_MODEL), BF16),
        jax.ShapeDtypeStruct((batch, seq // SWA_BLOCK, SWA_Q_W, SWA_BLOCK), BF16),
        jax.ShapeDtypeStruct((batch, seq // SWA_BLOCK, SWA_KV_W, SWA_BLOCK), BF16),
        jax.ShapeDtypeStruct((batch, seq // MOBA_BLOCK, MOBA_W, MOBA_BLOCK), BF16),
        jax.ShapeDtypeStruct((batch, seq // MOBA_BLOCK, MOBA_W, MOBA_BLOCK), BF16),
        jax.ShapeDtypeStruct((batch, MOBA_HEADS, seq // MOBA_BLOCK, seq), F32),
    )
    tok_spec = lambda w: pl.BlockSpec((TOKEN_TILE, w), lambda i: (i, 0))
    return pl.pallas_call(
        _proj_kernel,
        grid=(steps,),
        in_specs=[
            pl.BlockSpec((TOKEN_TILE, D_MODEL), lambda i: (i, 0)),
            pl.BlockSpec((1, 1, 1, D_MODEL), lambda i: (i // tiles_per_seq, 0, 0, 0)),
            pl.BlockSpec((1, 1, 1, D_MODEL), lambda i: (i // tiles_per_seq, 1, 0, 0)),
            _resident((1, D_MODEL), lambda i: (0, 0)),
            _resident(w_in.shape, lambda i: (0, 0)),
        ] + cast_specs,
        out_specs=(tok_spec(SWA_KV_W), tok_spec(MOBA_W), tok_spec(D_MODEL), tok_spec(D_MODEL),
                   feat_spec(SWA_Q_W, SWA_BLOCK, na), feat_spec(SWA_KV_W, SWA_BLOCK, na),
                   feat_spec(MOBA_W, MOBA_BLOCK, nb), feat_spec(MOBA_W, MOBA_BLOCK, nb),
                   pl.BlockSpec((1, MOBA_HEADS, seq // MOBA_BLOCK, TOKEN_TILE),
                                lambda i: (i // tiles_per_seq, 0, 0, i % tiles_per_seq)))
        + tuple(cast_specs),
        out_shape=out_shape + tuple(jax.ShapeDtypeStruct(w.shape, BF16) for w in cast_weights),
        scratch_shapes=[pltpu.VMEM((D_MODEL, TOK_W), BF16), pltpu.VMEM((FEAT_W, D_MODEL), BF16),
                        pltpu.VMEM((seq // MOBA_BLOCK, MOBA_W), F32)],
        compiler_params=pltpu.CompilerParams(dimension_semantics=("arbitrary",),
                                             vmem_limit_bytes=VMEM_LIMIT_BYTES),
        name="proj",
    )(x2, mod4, mod4, g1, w_in, *cast_weights)


def _swa_kernel(q_ref, k_ref, v_ref, rows_ref, sink_ref, o_ref, bias_ref, ot_ref, s_ref):
    L = SWA_BLOCK
    nblk = q_ref.shape[1]

    @pl.when(pl.program_id(0) == 0)
    def _():
        k_idx = lax.broadcasted_iota(jnp.int32, (L, L), 0)
        q_idx = lax.broadcasted_iota(jnp.int32, (L, L), 1)
        for h in range(SWA_Q_HEADS):
            g, gi = divmod(h, SWA_GROUP)
            t0 = rows_ref[0, h:h + 1, MOBA_BLOCK:MOBA_BLOCK + L]
            r = _toeplitz(jnp.concatenate([t0, t0], axis=1), L)
            cols = slice(gi * L, (gi + 1) * L)
            bias_ref[g, 0:L, cols] = jnp.where(q_idx < k_idx, r, NEG_INF)
            bias_ref[g, L:2 * L, cols] = jnp.where(q_idx >= k_idx, r, NEG_INF)

    ones_blk = jnp.concatenate([jnp.ones((1, 2 * L), BF16), jnp.zeros((ACC_ROWS - HEAD_DIM - 1, 2 * L), BF16)],
                               axis=0)
    PW = 2 * L
    chains = [(g, hp) for g in range(SWA_KV_HEADS) for hp in range(SWA_GROUP // 2)]
    kv_cols = [slice(g * HEAD_DIM, (g + 1) * HEAD_DIM) for g in range(SWA_KV_HEADS)]

    def q_pair(c, g, hp):
        h0 = g * SWA_GROUP + 2 * hp
        return jnp.concatenate([q_ref[0, c, (h0 + e) * HEAD_DIM:(h0 + e + 1) * HEAD_DIM, :] for e in range(2)],
                               axis=1)

    def scores_first(idx):
        g, hp = chains[idx]
        lanes = slice(hp * PW, (hp + 1) * PW)
        s_ref[idx, 0:L, :] = jnp.full((L, PW), NEG_INF, F32)
        s_ref[idx, L:2 * L, :] = _dot(k_ref[0, 0:L, kv_cols[g]], q_pair(0, g, hp)) + bias_ref[g, L:2 * L, lanes]

    def scores(c, idx):
        g, hp = chains[idx]
        kw = k_ref[0, pl.ds(pl.multiple_of((c - 1) * L, L), 2 * L), kv_cols[g]]
        s_ref[idx] = _dot(kw, q_pair(c, g, hp)) + bias_ref[g, :, hp * PW:(hp + 1) * PW]

    def attend(c, c_prev, idx):
        g, hp = chains[idx]
        hs = kv_cols[g]
        s = s_ref[idx]
        sink = sink_ref[g, :, hp * PW:(hp + 1) * PW]
        m = jnp.maximum(jnp.max(s, axis=0, keepdims=True), sink)
        p = jnp.exp2(s - m).astype(BF16)
        v_win = jnp.concatenate([v_ref[0, c_prev, hs, :], v_ref[0, c, hs, :]], axis=1)
        pv = _dot(jnp.concatenate([v_win, ones_blk], axis=0), p)
        o = pv[0:HEAD_DIM] / (pv[HEAD_DIM:HEAD_DIM + 1] + jnp.exp2(sink - m))
        for e in range(2):
            r = (g * SWA_GROUP + 2 * hp + e) * HEAD_DIM
            ot_ref[r:r + HEAD_DIM, :] = o[:, e * L:(e + 1) * L]

    def emit(c):
        o_ref[0, pl.ds(pl.multiple_of(c * L, L), L), :] = ot_ref[...].T.astype(o_ref.dtype)

    for idx in range(len(chains)):
        scores_first(idx)
    for idx in range(len(chains)):
        attend(0, 0, idx)
        scores(1, idx)
    emit(0)

    def body(c, carry):
        for idx in range(len(chains)):
            attend(c, c - 1, idx)
            scores(c + 1, idx)
        emit(c)
        return carry

    def body_unrolled(u, carry):
        for e in range(SWA_UNROLL):
            carry = body(SWA_UNROLL * u + 1 + e, carry)
        return carry

    lax.fori_loop(0, (nblk - 2) // SWA_UNROLL, body_unrolled, 0)
    for idx in range(len(chains)):
        attend(nblk - 1, nblk - 2, idx)
    emit(nblk - 1)


def _swa(qa_t, ka3, va_t, rows, sink_rows):
    batch, nblk = qa_t.shape[0], qa_t.shape[1]
    seq = nblk * SWA_BLOCK
    assert nblk >= 2 and (nblk - 2) % SWA_UNROLL == 0
    return pl.pallas_call(
        _swa_kernel,
        grid=(batch,),
        in_specs=[
            pl.BlockSpec((1, nblk, SWA_Q_W, SWA_BLOCK), lambda b: (b, 0, 0, 0)),
            pl.BlockSpec((1, seq, SWA_KV_W), lambda b: (b, 0, 0)),
            pl.BlockSpec((1, nblk, SWA_KV_W, SWA_BLOCK), lambda b: (b, 0, 0, 0)),
            _resident((1, N_ATTN_HEADS, 2 * MOBA_BLOCK), lambda b: (0, 0, 0)),
            _resident(sink_rows.shape, lambda b: (0, 0, 0)),
        ],
        out_specs=pl.BlockSpec((1, seq, SWA_Q_W), lambda b: (b, 0, 0)),
        out_shape=jax.ShapeDtypeStruct((batch, seq, SWA_Q_W), BF16),
        scratch_shapes=[pltpu.VMEM((SWA_KV_HEADS, 2 * SWA_BLOCK, SWA_GROUP * SWA_BLOCK), F32),
                        pltpu.VMEM((SWA_Q_W, SWA_BLOCK), F32),
                        pltpu.VMEM((SWA_Q_HEADS // 2, 2 * SWA_BLOCK, 2 * SWA_BLOCK), F32)],
        compiler_params=pltpu.CompilerParams(dimension_semantics=("arbitrary",),
                                             vmem_limit_bytes=VMEM_LIMIT_BYTES),
        name="swa",
    )(qa_t, ka3, va_t, rows, sink_rows)


def _moba_kernel(q_ref, qn_ref, k_ref, v_ref, sel_ref, rows_ref, o_ref, bias_ref, m_ref, acc_ref, s_ref, cm_ref):
    i = pl.program_id(1)
    nblk = pl.num_programs(1)
    MB = MOBA_BLOCK

    @pl.when(jnp.logical_and(pl.program_id(0) == 0, i == 0))
    def _():
        for h in range(MOBA_HEADS):
            for d in range(BIAS_SLOTS):
                bias_ref[h, d] = _toeplitz(rows_ref[d, SWA_Q_HEADS + h:SWA_Q_HEADS + h + 1, :], MB)

    m_ref[...] = jnp.full(m_ref.shape, NEG_INF, F32)
    acc_ref[...] = jnp.zeros(acc_ref.shape, F32)
    heads = [slice(h * HEAD_DIM, (h + 1) * HEAD_DIM) for h in range(MOBA_HEADS)]
    ones_blk = jnp.concatenate([jnp.ones((1, MB), BF16), jnp.zeros((ACC_ROWS - HEAD_DIM - 1, MB), BF16)],
                               axis=0)

    def scores(j, slot, h, hs):
        rows = pl.ds(pl.multiple_of(j * MB, MB), MB)
        s = _dot(k_ref[0, rows, hs], q_ref[0, 0, hs, :]) + bias_ref[h, slot]
        s_ref[h] = s
        cm_ref[h] = jnp.max(s, axis=0, keepdims=True) + sel_ref[0, h, pl.ds(j, 1), :]

    def own_scores(blk, qr, h, hs):
        rows = pl.ds(pl.multiple_of(blk * MB, MB), MB)
        s = _dot(k_ref[0, rows, hs], qr[0, 0, hs, :]) + bias_ref[h, 0]
        s_ref[h] = s
        cm_ref[h] = jnp.max(s, axis=0, keepdims=True)

    def accumulate(j, h, hs):
        m_old = m_ref[h]
        m_new = jnp.maximum(m_old, cm_ref[h])
        alpha = jnp.exp2(m_old - m_new)
        p = jnp.exp2(s_ref[h] - (m_new - sel_ref[0, h, pl.ds(j, 1), :])).astype(BF16)
        v_aug = jnp.concatenate([v_ref[0, j, hs, :], ones_blk], axis=0)
        rs = slice(h * ACC_ROWS, (h + 1) * ACC_ROWS)
        acc_ref[rs, :] = alpha * acc_ref[rs, :] + _dot(v_aug, p)
        m_ref[h] = m_new

    @pl.when(i == 0)
    def _():
        for h, hs in enumerate(heads):
            own_scores(i, q_ref, h, hs)

    def body(t, carry):
        j = i - t
        slot = jnp.minimum(t + 1, BIAS_SLOTS - 1)
        for h, hs in enumerate(heads):
            accumulate(j, h, hs)
            scores(j - 1, slot, h, hs)
        return carry

    def body2(u, carry):
        return body(2 * u + 1, body(2 * u, carry))

    lax.fori_loop(0, i // 2, body2, 0)

    @pl.when(i % 2 == 1)
    def _():
        body(i - 1, 0)

    @pl.when(i < nblk - 1)
    def _():
        for h, hs in enumerate(heads):
            accumulate(0, h, hs)
            own_scores(i + 1, qn_ref, h, hs)

    @pl.when(i == nblk - 1)
    def _():
        for h, hs in enumerate(heads):
            accumulate(0, h, hs)

    outs = []
    for h, hs in enumerate(heads):
        base = h * ACC_ROWS
        outs.append(acc_ref[base:base + HEAD_DIM, :] / acc_ref[base + HEAD_DIM:base + HEAD_DIM + 1, :])
    o_ref[0] = jnp.concatenate(outs, axis=0).T.astype(o_ref.dtype)


def _moba(qb_t, kb3, vb_t, sel, rows):
    batch, nblk = qb_t.shape[0], qb_t.shape[1]
    seq = nblk * MOBA_BLOCK
    return pl.pallas_call(
        _moba_kernel,
        grid=(batch, nblk),
        in_specs=[
            pl.BlockSpec((1, 1, MOBA_W, MOBA_BLOCK), lambda b, i: (b, i, 0, 0)),
            pl.BlockSpec((1, 1, MOBA_W, MOBA_BLOCK), lambda b, i: (b, jnp.minimum(i + 1, nblk - 1), 0, 0)),
            pl.BlockSpec((1, seq, MOBA_W), lambda b, i: (b, 0, 0)),
            pl.BlockSpec((1, nblk, MOBA_W, MOBA_BLOCK), lambda b, i: (b, 0, 0, 0)),
            pl.BlockSpec((1, MOBA_HEADS, nblk, MOBA_BLOCK), lambda b, i: (b, 0, 0, i)),
            _resident(rows.shape, lambda b, i: (0, 0, 0)),
        ],
        out_specs=pl.BlockSpec((1, MOBA_BLOCK, MOBA_W), lambda b, i: (b, i, 0)),
        out_shape=jax.ShapeDtypeStruct((batch, seq, MOBA_W), BF16),
        scratch_shapes=[pltpu.VMEM((MOBA_HEADS, BIAS_SLOTS, MOBA_BLOCK, MOBA_BLOCK), F32),
                        pltpu.VMEM((MOBA_HEADS, 1, MOBA_BLOCK), F32),
                        pltpu.VMEM((MOBA_HEADS * ACC_ROWS, MOBA_BLOCK), F32),
                        pltpu.VMEM((MOBA_HEADS, MOBA_BLOCK, MOBA_BLOCK), F32),
                        pltpu.VMEM((MOBA_HEADS, 1, MOBA_BLOCK), F32)],
        compiler_params=pltpu.CompilerParams(dimension_semantics=("arbitrary", "arbitrary"),
                                             vmem_limit_bytes=VMEM_LIMIT_BYTES),
        name="moba",
    )(qb_t, qb_t, kb3, vb_t, sel, rows)


def _post_kernel(x_ref, ya_ref, yb_ref, ga_ref, gb_ref, gate1_ref, shift2_ref, scale2_ref,
                 gate2_ref, g2_ref, gf_ref, wa_ref, wb_ref, wo_ref, w1_ref, w2_ref, o_ref):
    a = _dot(ya_ref[...], wa_ref[...])
    b = _dot(yb_ref[...], wb_ref[...])
    merged = (jax.nn.sigmoid(ga_ref[...].astype(F32)) * a
              + jax.nn.sigmoid(gb_ref[...].astype(F32)) * b)
    x1 = x_ref[...] + gate1_ref[0, 0] * _dot(merged.astype(BF16), wo_ref[...])
    h2 = _rms_modulate(x1, g2_ref[...], shift2_ref[0, 0], scale2_ref[0, 0]).astype(BF16)
    y = jnp.zeros_like(x1)
    for c in range(D_FF // FF_CHUNK):
        cs = slice(c * FF_CHUNK, (c + 1) * FF_CHUNK)
        u = jnp.square(jnp.maximum(_dot(h2, w1_ref[:, cs]), 0.0)).astype(BF16)
        y = y + _dot(u, w2_ref[cs, :])
    x2 = x1 + gate2_ref[0, 0] * y
    ms = jnp.mean(x2 * x2, axis=-1, keepdims=True)
    o_ref[...] = (x2 * lax.rsqrt(ms + RMS_EPS)) * gf_ref[...]


def _post(x2, ya, yb, ga, gb, mod4, g2, gf, wa, wb, wo, w1, w2, seq):
    tokens = x2.shape[0]
    tiles_per_seq = seq // TOKEN_TILE
    tok = lambda w: pl.BlockSpec((TOKEN_TILE, w), lambda i: (i, 0))
    modrow = lambda k: pl.BlockSpec((1, 1, 1, D_MODEL), lambda i: (i // tiles_per_seq, k, 0, 0))
    full = lambda a: _resident(a.shape, lambda i: (0, 0))
    return pl.pallas_call(
        _post_kernel,
        grid=(tokens // TOKEN_TILE,),
        in_specs=[tok(D_MODEL), tok(SWA_Q_W), tok(MOBA_W), tok(D_MODEL), tok(D_MODEL),
                  modrow(2), modrow(3), modrow(4), modrow(5),
                  full(g2), full(gf), full(wa), full(wb), full(wo), full(w1), full(w2)],
        out_specs=tok(D_MODEL),
        out_shape=jax.ShapeDtypeStruct((tokens, D_MODEL), F32),
        compiler_params=pltpu.CompilerParams(dimension_semantics=("arbitrary",),
                                             vmem_limit_bytes=VMEM_LIMIT_BYTES),
        name="post",
    )(x2, ya, yb, ga, gb, mod4, mod4, mod4, mod4, g2, gf, wa, wb, wo, w1, w2)


def kernel(x, c, ada_w, ada_b, norm1_g, norm2_g, w_in, attn_sinks, rel_bias, w_branch_a,
           w_branch_b, w_out, w_mlp_in, w_mlp_out, final_g):
    batch, seq, _ = x.shape
    depth = ada_w.shape[0]
    nblk_b = seq // MOBA_BLOCK
    assert seq % TOKEN_TILE == 0 and TOKEN_TILE % MOBA_BLOCK == 0 and batch <= 8
    assert (BIAS_SLOTS - 1) * MOBA_BLOCK - (MOBA_BLOCK - 1) >= 1513 and MAX_DISTANCE == 2048
    assert 2 * SWA_BLOCK <= MOBA_BLOCK and SWA_WINDOW == SWA_BLOCK
    assert depth == 1
    l = 0

    rows = _bias_rows(rel_bias)
    c_t = jnp.zeros((D_MODEL, 8), F32).at[:, :batch].set(c.astype(F32).T)

    x2 = x.reshape(batch * seq, D_MODEL)
    mod = _ada(c_t, ada_w[l], ada_b[l][None, :], batch)
    mod4 = mod[:batch].reshape(batch, N_MOD, 1, D_MODEL)
    later_weights = [w.astype(F32) for w in (w_branch_a[l], w_branch_b[l], w_out[l], w_mlp_in[l], w_mlp_out[l])]
    ka, kb, ga, gb, qa_t, va_t, qb_t, vb_t, sel, wa, wb, wo, w1, w2 = _proj(
        x2, mod4, norm1_g[l][None, :], w_in[l].astype(F32), later_weights, batch, seq)

    sink_rows = jnp.broadcast_to(
        (attn_sinks[l].astype(F32) * LOG2E).reshape(SWA_KV_HEADS, 1, SWA_GROUP, 1),
        (SWA_KV_HEADS, 1, SWA_GROUP, SWA_BLOCK)).reshape(SWA_KV_HEADS, 1, SWA_GROUP * SWA_BLOCK)
    ya = _swa(qa_t, ka.reshape(batch, seq, SWA_KV_W), va_t, rows, sink_rows)

    yb = _moba(qb_t, kb.reshape(batch, seq, MOBA_W), vb_t, sel, rows)

    out = _post(x2, ya.reshape(batch * seq, SWA_Q_W), yb.reshape(batch * seq, MOBA_W), ga, gb,
                mod4, norm2_g[l][None, :], final_g[None, :], wa, wb, wo, w1, w2, seq)
    return out.reshape(batch, seq, D_MODEL)
```

```python
import functools
import math

import jax
import jax.numpy as jnp
from jax import lax
from jax.experimental import pallas as pl
from jax.experimental.pallas import tpu as pltpu

D_MODEL = 1024
HEAD_DIM = 64
ATTN_SCALE = HEAD_DIM ** -0.5
LOG2E = math.log2(math.e)
SWA_Q_HEADS = 8
SWA_KV_HEADS = 2
SWA_GROUP = SWA_Q_HEADS // SWA_KV_HEADS
SWA_WINDOW = 128
SWA_BLOCK = 128
MOBA_HEADS = 8
MOBA_BLOCK = 256
MOBA_TOPK = 3
NUM_BUCKETS = 32
MAX_EXACT = NUM_BUCKETS // 2
MAX_DISTANCE = 2048
N_ATTN_HEADS = SWA_Q_HEADS + MOBA_HEADS
SWA_Q_W = SWA_Q_HEADS * HEAD_DIM
SWA_KV_W = SWA_KV_HEADS * HEAD_DIM
MOBA_W = MOBA_HEADS * HEAD_DIM
D_FF = 4 * D_MODEL
N_MOD = 6
RMS_EPS = 1e-6

VMEM_LIMIT_BYTES = 56 * 1024 * 1024
TOKEN_TILE = 512
FF_CHUNK = 1024
BIAS_SLOTS = 8
SWA_UNROLL = 3
ACC_ROWS = HEAD_DIM + 16
NEG_INF = float("-inf")

F32 = jnp.float32
BF16 = jnp.bfloat16


def _resident(block_shape, index_map):
    return pl.BlockSpec(block_shape, index_map, pipeline_mode=pl.Buffered(1))


def _dot(a, b):
    return jnp.dot(a, b, preferred_element_type=F32)


def _dot_nt(a, b):
    return lax.dot_general(a, b, (((1,), (1,)), ((), ())), preferred_element_type=F32)


def _rms_modulate(xv, g, shift, scale):
    ms = jnp.mean(xv * xv, axis=-1, keepdims=True)
    y = xv * lax.rsqrt(ms + RMS_EPS)
    return (y * g) * (1.0 + scale) + shift


def _toeplitz(u, rows):
    wide = jnp.broadcast_to(u, (rows, 2 * rows))
    return pltpu.roll(wide, 0, 1, stride=1, stride_axis=0)[:, rows:]


def _ada_kernel(ct_ref, w_ref, b_ref, o_ref, *, batch):
    cs_t = jax.nn.silu(ct_ref[...])
    w = w_ref[...]
    rows = [jnp.sum(w * cs_t[:, b:b + 1], axis=0, keepdims=True) for b in range(batch)]
    rows.append(jnp.zeros((8 - batch, w.shape[1]), F32))
    o_ref[...] = jnp.concatenate(rows, axis=0) + b_ref[...]


def _ada(c_t, w, b, batch):
    n = w.shape[1]
    bn = 1536
    return pl.pallas_call(
        functools.partial(_ada_kernel, batch=batch),
        grid=(n // bn,),
        in_specs=[pl.BlockSpec((D_MODEL, 8), lambda j: (0, 0)),
                  pl.BlockSpec((D_MODEL, bn), lambda j: (0, j)),
                  pl.BlockSpec((1, bn), lambda j: (0, j))],
        out_specs=pl.BlockSpec((8, bn), lambda j: (0, j)),
        out_shape=jax.ShapeDtypeStruct((8, n), F32),
        compiler_params=pltpu.CompilerParams(dimension_semantics=("arbitrary",),
                                             vmem_limit_bytes=VMEM_LIMIT_BYTES),
        name="ada",
    )(c_t, w, b)


def _t5_bucket(dist):
    n = jnp.maximum(dist, 0)
    nf = jnp.maximum(n, 1).astype(F32)
    large = MAX_EXACT + (jnp.log(nf / MAX_EXACT) / math.log(MAX_DISTANCE / MAX_EXACT)
                         * (NUM_BUCKETS - MAX_EXACT)).astype(jnp.int32)
    large = jnp.minimum(large, NUM_BUCKETS - 1)
    return jnp.where(n < MAX_EXACT, n, large)


def _bias_rows_kernel(bucket_ref, rbt_ref, o_ref):
    bucket = bucket_ref[...]
    n = bucket.shape[1]
    onehot = jnp.where(lax.broadcasted_iota(jnp.int32, (NUM_BUCKETS, n), 0) == bucket, 1.0, 0.0)
    rows = jnp.dot(rbt_ref[...], onehot, preferred_element_type=F32,
                   precision=lax.Precision.HIGHEST)
    rows = jnp.where(bucket >= 0, rows * LOG2E, NEG_INF)
    for d in range(BIAS_SLOTS):
        o_ref[d] = rows[:, d * MOBA_BLOCK:(d + 2) * MOBA_BLOCK]


def _bias_rows(rel_bias):
    n = (BIAS_SLOTS + 1) * MOBA_BLOCK
    dist = jnp.arange(n, dtype=jnp.int32) - MOBA_BLOCK
    bucket = jnp.where(dist >= 0, _t5_bucket(dist), -1)[None, :]
    return pl.pallas_call(
        _bias_rows_kernel,
        out_shape=jax.ShapeDtypeStruct((BIAS_SLOTS, N_ATTN_HEADS, 2 * MOBA_BLOCK), F32),
        name="bias_rows",
    )(bucket, rel_bias.astype(F32).T)


IN_WIDTHS = (SWA_Q_W, SWA_KV_W, SWA_KV_W, MOBA_W, MOBA_W, MOBA_W, D_MODEL, D_MODEL)
IN_OFFS = tuple(sum(IN_WIDTHS[:k]) for k in range(len(IN_WIDTHS) + 1))
FEAT_COLS = (0, 2, 3, 5)
FEAT_W = sum(IN_WIDTHS[k] for k in FEAT_COLS)
TOK_COLS = (1, 4, 6, 7)
TOK_W = sum(IN_WIDTHS[k] for k in TOK_COLS)
Q_COLS = (0, 3)


def _proj_kernel(x_ref, shift_ref, scale_ref, g_ref, w_ref, *refs):
    n_cast = (len(refs) - 12) // 2
    cast_in, refs = refs[:n_cast], refs[n_cast:]
    ka_ref, kb_ref, ga_ref, gb_ref, qa_ref, va_ref, qb_ref, vb_ref, sel_ref = refs[:9]
    cast_out, (wk_ref, wt_ref, kmean_ref) = refs[9:9 + n_cast], refs[9 + n_cast:]

    for src, dst in zip(cast_in, cast_out):
        dst[...] = src[...].astype(dst.dtype)

    tile = pl.program_id(0) % (kmean_ref.shape[0] // BLOCKS_PER_TILE)

    @pl.when(tile == 0)
    def _():
        kmean_ref[...] = jnp.zeros(kmean_ref.shape, F32)

    @pl.when(pl.program_id(0) == 0)
    def _():
        o = 0
        for k in TOK_COLS:
            wk_ref[:, o:o + IN_WIDTHS[k]] = w_ref[:, IN_OFFS[k]:IN_OFFS[k + 1]].astype(BF16)
            o += IN_WIDTHS[k]
        o = 0
        for k in FEAT_COLS:
            wcol = w_ref[:, IN_OFFS[k]:IN_OFFS[k + 1]]
            if k in Q_COLS:
                wcol = wcol * (ATTN_SCALE * LOG2E)
            wt_ref[o:o + IN_WIDTHS[k], :] = wcol.T.astype(BF16)
            o += IN_WIDTHS[k]

    h = _rms_modulate(x_ref[...], g_ref[...], shift_ref[0, 0], scale_ref[0, 0]).astype(BF16)
    tok_refs = dict(zip(TOK_COLS, (ka_ref, kb_ref, ga_ref, gb_ref)))
    feat_refs = dict(zip(FEAT_COLS, (qa_ref, va_ref, qb_ref, vb_ref)))

    def token_major(k):
        o = sum(IN_WIDTHS[c] for c in TOK_COLS[:TOK_COLS.index(k)])
        out = _dot(h, wk_ref[:, o:o + IN_WIDTHS[k]])
        tok_refs[k][...] = out.astype(BF16)
        return out

    def feature_major(k):
        o = sum(IN_WIDTHS[c] for c in FEAT_COLS[:FEAT_COLS.index(k)])
        out = _dot_nt(wt_ref[o:o + IN_WIDTHS[k], :], h).astype(BF16)
        blk = feat_refs[k].shape[-1]
        for t in range(TOKEN_TILE // blk):
            feat_refs[k][0, t] = out[:, t * blk:(t + 1) * blk]
        return out

    _update_kmean(token_major(4), kmean_ref, tile)
    qb_t = feature_major(3)
    token_major(1)
    token_major(6)
    _moba_select(qb_t, sel_ref, kmean_ref, tile)
    token_major(7)
    for k in (0, 2, 5):
        feature_major(k)


BLOCKS_PER_TILE = TOKEN_TILE // MOBA_BLOCK


def _update_kmean(kb, kmean_ref, tile):
    for t in range(BLOCKS_PER_TILE):
        kmean_ref[pl.ds(tile * BLOCKS_PER_TILE + t, 1), :] = jnp.mean(
            kb[t * MOBA_BLOCK:(t + 1) * MOBA_BLOCK, :], axis=0, keepdims=True)


def _moba_select(qb_t, sel_ref, kmean_ref, tile):
    nblk = kmean_ref.shape[0]
    blocks_per_tile = BLOCKS_PER_TILE
    blk = lax.broadcasted_iota(jnp.int32, (nblk, TOKEN_TILE), 0)
    qblk = tile * blocks_per_tile + lax.broadcasted_iota(jnp.int32, (nblk, TOKEN_TILE), 1) // MOBA_BLOCK
    past = blk < qblk
    own = jnp.where(blk == qblk, 0.0, NEG_INF)
    blk_f = blk.astype(F32)
    for h in range(MOBA_HEADS):
        hs = slice(h * HEAD_DIM, (h + 1) * HEAD_DIM)
        kmean = kmean_ref[:, hs]
        k1 = kmean.astype(BF16)
        r1 = kmean - k1.astype(F32)
        k2 = r1.astype(BF16)
        k3 = (r1 - k2.astype(F32)).astype(BF16)
        parts = _dot(jnp.concatenate([k1, k2, k3], axis=0), qb_t[hs, :])
        gate = parts[0:nblk] + parts[nblk:2 * nblk] + parts[2 * nblk:3 * nblk]
        gate = jnp.where(past, gate, NEG_INF)
        mask = own
        for _ in range(MOBA_TOPK):
            top = jnp.max(gate, axis=0, keepdims=True)
            first = jnp.min(jnp.where(gate == top, blk_f, float(nblk)), axis=0, keepdims=True)
            pick = blk_f == first
            mask = jnp.where(jnp.logical_and(pick, past), 0.0, mask)
            gate = jnp.where(pick, NEG_INF, gate)
        sel_ref[0, h] = mask


def _proj(x2, mod4, g1, w_in, cast_weights, batch, seq):
    tokens = x2.shape[0]
    steps = tokens // TOKEN_TILE
    assert all(w.shape[0] % (16 * steps) == 0 for w in cast_weights)
    cast_specs = [pl.BlockSpec((w.shape[0] // steps, w.shape[1]), lambda i: (i, 0)) for w in cast_weights]
    tiles_per_seq = seq // TOKEN_TILE
    na = TOKEN_TILE // SWA_BLOCK
    nb = TOKEN_TILE // MOBA_BLOCK

    def feat_spec(width, blk, per_tile):
        return pl.BlockSpec((1, per_tile, width, blk),
                            lambda i: (i // tiles_per_seq, i % tiles_per_seq, 0, 0))

    out_shape = (
        jax.ShapeDtypeStruct((tokens, SWA_KV_W), BF16),
        jax.ShapeDtypeStruct((tokens, MOBA_W), BF16),
        jax.ShapeDtypeStruct((tokens, D_MODEL), BF16),
        jax.ShapeDtypeStruct((tokens, D_MODEL), BF16),
        jax.ShapeDtypeStruct((batch, seq // SWA_BLOCK, SWA_Q_W, SWA_BLOCK), BF16),
        jax.ShapeDtypeStruct((batch, seq // SWA_BLOCK, SWA_KV_W, SWA_BLOCK), BF16),
        jax.ShapeDtypeStruct((batch, seq // MOBA_BLOCK, MOBA_W, MOBA_BLOCK), BF16),
        jax.ShapeDtypeStruct((batch, seq // MOBA_BLOCK, MOBA_W, MOBA_BLOCK), BF16),
        jax.ShapeDtypeStruct((batch, MOBA_HEADS, seq // MOBA_BLOCK, seq), F32),
    )
    tok_spec = lambda w: pl.BlockSpec((TOKEN_TILE, w), lambda i: (i, 0))
    return pl.pallas_call(
        _proj_kernel,
        grid=(steps,),
        in_specs=[
            pl.BlockSpec((TOKEN_TILE, D_MODEL), lambda i: (i, 0)),
            pl.BlockSpec((1, 1, 1, D_MODEL), lambda i: (i // tiles_per_seq, 0, 0, 0)),
            pl.BlockSpec((1, 1, 1, D_MODEL), lambda i: (i // tiles_per_seq, 1, 0, 0)),
            _resident((1, D_MODEL), lambda i: (0, 0)),
            _resident(w_in.shape, lambda i: (0, 0)),
        ] + cast_specs,
        out_specs=(tok_spec(SWA_KV_W), tok_spec(MOBA_W), tok_spec(D_MODEL), tok_spec(D_MODEL),
                   feat_spec(SWA_Q_W, SWA_BLOCK, na), feat_spec(SWA_KV_W, SWA_BLOCK, na),
                   feat_spec(MOBA_W, MOBA_BLOCK, nb), feat_spec(MOBA_W, MOBA_BLOCK, nb),
                   pl.BlockSpec((1, MOBA_HEADS, seq // MOBA_BLOCK, TOKEN_TILE),
                                lambda i: (i // tiles_per_seq, 0, 0, i % tiles_per_seq)))
        + tuple(cast_specs),
        out_shape=out_shape + tuple(jax.ShapeDtypeStruct(w.shape, BF16) for w in cast_weights),
        scratch_shapes=[pltpu.VMEM((D_MODEL, TOK_W), BF16), pltpu.VMEM((FEAT_W, D_MODEL), BF16),
                        pltpu.VMEM((seq // MOBA_BLOCK, MOBA_W), F32)],
        compiler_params=pltpu.CompilerParams(dimension_semantics=("arbitrary",),
                                             vmem_limit_bytes=VMEM_LIMIT_BYTES),
        name="proj",
    )(x2, mod4, mod4, g1, w_in, *cast_weights)


def _swa_kernel(q_ref, k_ref, v_ref, rows_ref, sink_ref, o_ref, bias_ref, ot_ref, s_ref):
    L = SWA_BLOCK
    nblk = q_ref.shape[1]

    @pl.when(pl.program_id(0) == 0)
    def _():
        k_idx = lax.broadcasted_iota(jnp.int32, (L, L), 0)
        q_idx = lax.broadcasted_iota(jnp.int32, (L, L), 1)
        for h in range(SWA_Q_HEADS):
            g, gi = divmod(h, SWA_GROUP)
            t0 = rows_ref[0, h:h + 1, MOBA_BLOCK:MOBA_BLOCK + L]
            r = _toeplitz(jnp.concatenate([t0, t0], axis=1), L)
            cols = slice(gi * L, (gi + 1) * L)
            bias_ref[g, 0:L, cols] = jnp.where(q_idx < k_idx, r, NEG_INF)
            bias_ref[g, L:2 * L, cols] = jnp.where(q_idx >= k_idx, r, NEG_INF)

    ones_blk = jnp.concatenate([jnp.ones((1, 2 * L), BF16), jnp.zeros((ACC_ROWS - HEAD_DIM - 1, 2 * L), BF16)],
                               axis=0)
    PW = 2 * L
    chains = [(g, hp) for g in range(SWA_KV_HEADS) for hp in range(SWA_GROUP // 2)]
    kv_cols = [slice(g * HEAD_DIM, (g + 1) * HEAD_DIM) for g in range(SWA_KV_HEADS)]

    def q_pair(c, g, hp):
        h0 = g * SWA_GROUP + 2 * hp
        return jnp.concatenate([q_ref[0, c, (h0 + e) * HEAD_DIM:(h0 + e + 1) * HEAD_DIM, :] for e in range(2)],
                               axis=1)

    def scores_first(idx):
        g, hp = chains[idx]
        lanes = slice(hp * PW, (hp + 1) * PW)
        s_ref[idx, 0:L, :] = jnp.full((L, PW), NEG_INF, F32)
        s_ref[idx, L:2 * L, :] = _dot(k_ref[0, 0:L, kv_cols[g]], q_pair(0, g, hp)) + bias_ref[g, L:2 * L, lanes]

    def scores(c, idx):
        g, hp = chains[idx]
        kw = k_ref[0, pl.ds(pl.multiple_of((c - 1) * L, L), 2 * L), kv_cols[g]]
        s_ref[idx] = _dot(kw, q_pair(c, g, hp)) + bias_ref[g, :, hp * PW:(hp + 1) * PW]

    def attend(c, c_prev, idx):
        g, hp = chains[idx]
        hs = kv_cols[g]
        s = s_ref[idx]
        sink = sink_ref[g, :, hp * PW:(hp + 1) * PW]
        m = jnp.maximum(jnp.max(s, axis=0, keepdims=True), sink)
        p = jnp.exp2(s - m).astype(BF16)
        v_win = jnp.concatenate([v_ref[0, c_prev, hs, :], v_ref[0, c, hs, :]], axis=1)
        pv = _dot(jnp.concatenate([v_win, ones_blk], axis=0), p)
        o = pv[0:HEAD_DIM] / (pv[HEAD_DIM:HEAD_DIM + 1] + jnp.exp2(sink - m))
        for e in range(2):
            r = (g * SWA_GROUP + 2 * hp + e) * HEAD_DIM
            ot_ref[r:r + HEAD_DIM, :] = o[:, e * L:(e + 1) * L]

    def emit(c):
        o_ref[0, pl.ds(pl.multiple_of(c * L, L), L), :] = ot_ref[...].T.astype(o_ref.dtype)

    for idx in range(len(chains)):
        scores_first(idx)
    for idx in range(len(chains)):
        attend(0, 0, idx)
        scores(1, idx)
    emit(0)

    def body(c, carry):
        for idx in range(len(chains)):
            attend(c, c - 1, idx)
            scores(c + 1, idx)
        emit(c)
        return carry

    def body_unrolled(u, carry):
        for e in range(SWA_UNROLL):
            carry = body(SWA_UNROLL * u + 1 + e, carry)
        return carry

    lax.fori_loop(0, (nblk - 2) // SWA_UNROLL, body_unrolled, 0)
    for idx in range(len(chains)):
        attend(nblk - 1, nblk - 2, idx)
    emit(nblk - 1)


def _swa(qa_t, ka3, va_t, rows, sink_rows):
    batch, nblk = qa_t.shape[0], qa_t.shape[1]
    seq = nblk * SWA_BLOCK
    assert nblk >= 2 and (nblk - 2) % SWA_UNROLL == 0
    return pl.pallas_call(
        _swa_kernel,
        grid=(batch,),
        in_specs=[
            pl.BlockSpec((1, nblk, SWA_Q_W, SWA_BLOCK), lambda b: (b, 0, 0, 0)),
            pl.BlockSpec((1, seq, SWA_KV_W), lambda b: (b, 0, 0)),
            pl.BlockSpec((1, nblk, SWA_KV_W, SWA_BLOCK), lambda b: (b, 0, 0, 0)),
            _resident((1, N_ATTN_HEADS, 2 * MOBA_BLOCK), lambda b: (0, 0, 0)),
            _resident(sink_rows.shape, lambda b: (0, 0, 0)),
        ],
        out_specs=pl.BlockSpec((1, seq, SWA_Q_W), lambda b: (b, 0, 0)),
        out_shape=jax.ShapeDtypeStruct((batch, seq, SWA_Q_W), BF16),
        scratch_shapes=[pltpu.VMEM((SWA_KV_HEADS, 2 * SWA_BLOCK, SWA_GROUP * SWA_BLOCK), F32),
                        pltpu.VMEM((SWA_Q_W, SWA_BLOCK), F32),
                        pltpu.VMEM((SWA_Q_HEADS // 2, 2 * SWA_BLOCK, 2 * SWA_BLOCK), F32)],
        compiler_params=pltpu.CompilerParams(dimension_semantics=("arbitrary",),
                                             vmem_limit_bytes=VMEM_LIMIT_BYTES),
        name="swa",
    )(qa_t, ka3, va_t, rows, sink_rows)


def _moba_kernel(q_ref, qn_ref, k_ref, v_ref, sel_ref, rows_ref, o_ref, bias_ref, m_ref, acc_ref, s_ref, cm_ref):
    i = pl.program_id(1)
    nblk = pl.num_programs(1)
    MB = MOBA_BLOCK

    @pl.when(jnp.logical_and(pl.program_id(0) == 0, i == 0))
    def _():
        for h in range(MOBA_HEADS):
            for d in range(BIAS_SLOTS):
                bias_ref[h, d, :, 0:MB] = _toeplitz(rows_ref[d, SWA_Q_HEADS + h:SWA_Q_HEADS + h + 1, :], MB)

    m_ref[...] = jnp.full(m_ref.shape, NEG_INF, F32)
    acc_ref[...] = jnp.zeros(acc_ref.shape, F32)
    heads = [slice(h * HEAD_DIM, (h + 1) * HEAD_DIM) for h in range(MOBA_HEADS)]
    ones_blk = jnp.concatenate([jnp.ones((1, MB), BF16), jnp.zeros((ACC_ROWS - HEAD_DIM - 1, MB), BF16)],
                               axis=0)

    def scores(j, slot, h, hs):
        rows = pl.ds(pl.multiple_of(j * MB, MB), MB)
        s = _dot(k_ref[0, rows, hs], q_ref[0, 0, hs, :]) + bias_ref[h, slot, :, 0:MB]
        s_ref[h] = s
        cm_ref[h] = jnp.max(s, axis=0, keepdims=True) + sel_ref[0, h, pl.ds(j, 1), :]

    def own_scores(blk, qr, h, hs):
        rows = pl.ds(pl.multiple_of(blk * MB, MB), MB)
        s = _dot(k_ref[0, rows, hs], qr[0, 0, hs, :]) + bias_ref[h, 0, :, 0:MB]
        s_ref[h] = s
        cm_ref[h] = jnp.max(s, axis=0, keepdims=True)

    def accumulate(j, h, hs):
        m_old = m_ref[h]
        m_new = jnp.maximum(m_old, cm_ref[h])
        alpha = jnp.exp2(m_old - m_new)
        p = jnp.exp2(s_ref[h] - (m_new - sel_ref[0, h, pl.ds(j, 1), :])).astype(BF16)
        v_aug = jnp.concatenate([v_ref[0, j, hs, :], ones_blk], axis=0)
        rs = slice(h * ACC_ROWS, (h + 1) * ACC_ROWS)
        acc_ref[rs, :] = alpha * acc_ref[rs, :] + _dot(v_aug, p)
        m_ref[h] = m_new

    @pl.when(i == 0)
    def _():
        for h, hs in enumerate(heads):
            own_scores(i, q_ref, h, hs)

    def body(t, carry):
        j = i - t
        slot = jnp.minimum(t + 1, BIAS_SLOTS - 1)
        for h, hs in enumerate(heads):
            accumulate(j, h, hs)
            scores(j - 1, slot, h, hs)
        return carry

    def body2(u, carry):
        return body(2 * u + 1, body(2 * u, carry))

    lax.fori_loop(0, i // 2, body2, 0)

    @pl.when(i % 2 == 1)
    def _():
        body(i - 1, 0)

    @pl.when(i < nblk - 1)
    def _():
        for h, hs in enumerate(heads):
            accumulate(0, h, hs)
            own_scores(i + 1, qn_ref, h, hs)

    @pl.when(i == nblk - 1)
    def _():
        for h, hs in enumerate(heads):
            accumulate(0, h, hs)

    outs = []
    for h, hs in enumerate(heads):
        base = h * ACC_ROWS
        outs.append(acc_ref[base:base + HEAD_DIM, :] / acc_ref[base + HEAD_DIM:base + HEAD_DIM + 1, :])
    o_ref[0] = jnp.concatenate(outs, axis=0).T.astype(o_ref.dtype)


def _moba(qb_t, kb3, vb_t, sel, rows):
    batch, nblk = qb_t.shape[0], qb_t.shape[1]
    seq = nblk * MOBA_BLOCK
    return pl.pallas_call(
        _moba_kernel,
        grid=(batch, nblk),
        in_specs=[
            pl.BlockSpec((1, 1, MOBA_W, MOBA_BLOCK), lambda b, i: (b, i, 0, 0)),
            pl.BlockSpec((1, 1, MOBA_W, MOBA_BLOCK), lambda b, i: (b, jnp.minimum(i + 1, nblk - 1), 0, 0)),
            pl.BlockSpec((1, seq, MOBA_W), lambda b, i: (b, 0, 0)),
            pl.BlockSpec((1, nblk, MOBA_W, MOBA_BLOCK), lambda b, i: (b, 0, 0, 0)),
            pl.BlockSpec((1, MOBA_HEADS, nblk, MOBA_BLOCK), lambda b, i: (b, 0, 0, i)),
            _resident(rows.shape, lambda b, i: (0, 0, 0)),
        ],
        out_specs=pl.BlockSpec((1, MOBA_BLOCK, MOBA_W), lambda b, i: (b, i, 0)),
        out_shape=jax.ShapeDtypeStruct((batch, seq, MOBA_W), BF16),
        scratch_shapes=[pltpu.VMEM((MOBA_HEADS, BIAS_SLOTS, MOBA_BLOCK, MOBA_BLOCK + 128), F32),
                        pltpu.VMEM((MOBA_HEADS, 1, MOBA_BLOCK), F32),
                        pltpu.VMEM((MOBA_HEADS * ACC_ROWS, MOBA_BLOCK), F32),
                        pltpu.VMEM((MOBA_HEADS, MOBA_BLOCK, MOBA_BLOCK), F32),
                        pltpu.VMEM((MOBA_HEADS, 1, MOBA_BLOCK), F32)],
        compiler_params=pltpu.CompilerParams(dimension_semantics=("arbitrary", "arbitrary"),
                                             vmem_limit_bytes=VMEM_LIMIT_BYTES),
        name="moba",
    )(qb_t, qb_t, kb3, vb_t, sel, rows)


def _post_kernel(x_ref, ya_ref, yb_ref, ga_ref, gb_ref, gate1_ref, shift2_ref, scale2_ref,
                 gate2_ref, g2_ref, gf_ref, wa_ref, wb_ref, wo_ref, w1_ref, w2_ref, o_ref):
    a = _dot(ya_ref[...], wa_ref[...])
    b = _dot(yb_ref[...], wb_ref[...])
    merged = (jax.nn.sigmoid(ga_ref[...].astype(F32)) * a
              + jax.nn.sigmoid(gb_ref[...].astype(F32)) * b)
    x1 = x_ref[...] + gate1_ref[0, 0] * _dot(merged.astype(BF16), wo_ref[...])
    h2 = _rms_modulate(x1, g2_ref[...], shift2_ref[0, 0], scale2_ref[0, 0]).astype(BF16)
    y = jnp.zeros_like(x1)
    for c in range(D_FF // FF_CHUNK):
        cs = slice(c * FF_CHUNK, (c + 1) * FF_CHUNK)
        u = jnp.square(jnp.maximum(_dot(h2, w1_ref[:, cs]), 0.0)).astype(BF16)
        y = y + _dot(u, w2_ref[cs, :])
    x2 = x1 + gate2_ref[0, 0] * y
    ms = jnp.mean(x2 * x2, axis=-1, keepdims=True)
    o_ref[...] = (x2 * lax.rsqrt(ms + RMS_EPS)) * gf_ref[...]


def _post(x2, ya, yb, ga, gb, mod4, g2, gf, wa, wb, wo, w1, w2, seq):
    tokens = x2.shape[0]
    tiles_per_seq = seq // TOKEN_TILE
    tok = lambda w: pl.BlockSpec((TOKEN_TILE, w), lambda i: (i, 0))
    modrow = lambda k: pl.BlockSpec((1, 1, 1, D_MODEL), lambda i: (i // tiles_per_seq, k, 0, 0))
    full = lambda a: _resident(a.shape, lambda i: (0, 0))
    return pl.pallas_call(
        _post_kernel,
        grid=(tokens // TOKEN_TILE,),
        in_specs=[tok(D_MODEL), tok(SWA_Q_W), tok(MOBA_W), tok(D_MODEL), tok(D_MODEL),
                  modrow(2), modrow(3), modrow(4), modrow(5),
                  full(g2), full(gf), full(wa), full(wb), full(wo), full(w1), full(w2)],
        out_specs=tok(D_MODEL),
        out_shape=jax.ShapeDtypeStruct((tokens, D_MODEL), F32),
        compiler_params=pltpu.CompilerParams(dimension_semantics=("arbitrary",),
                                             vmem_limit_bytes=VMEM_LIMIT_BYTES),
        name="post",
    )(x2, ya, yb, ga, gb, mod4, mod4, mod4, mod4, g2, gf, wa, wb, wo, w1, w2)


def kernel(x, c, ada_w, ada_b, norm1_g, norm2_g, w_in, attn_sinks, rel_bias, w_branch_a,
           w_branch_b, w_out, w_mlp_in, w_mlp_out, final_g):
    batch, seq, _ = x.shape
    depth = ada_w.shape[0]
    nblk_b = seq // MOBA_BLOCK
    assert seq % TOKEN_TILE == 0 and TOKEN_TILE % MOBA_BLOCK == 0 and batch <= 8
    assert (BIAS_SLOTS - 1) * MOBA_BLOCK - (MOBA_BLOCK - 1) >= 1513 and MAX_DISTANCE == 2048
    assert 2 * SWA_BLOCK <= MOBA_BLOCK and SWA_WINDOW == SWA_BLOCK
    assert depth == 1
    l = 0

    rows = _bias_rows(rel_bias)
    c_t = jnp.zeros((D_MODEL, 8), F32).at[:, :batch].set(c.astype(F32).T)

    x2 = x.reshape(batch * seq, D_MODEL)
    mod = _ada(c_t, ada_w[l], ada_b[l][None, :], batch)
    mod4 = mod[:batch].reshape(batch, N_MOD, 1, D_MODEL)
    later_weights = [w.astype(F32) for w in (w_branch_a[l], w_branch_b[l], w_out[l], w_mlp_in[l], w_mlp_out[l])]
    ka, kb, ga, gb, qa_t, va_t, qb_t, vb_t, sel, wa, wb, wo, w1, w2 = _proj(
        x2, mod4, norm1_g[l][None, :], w_in[l].astype(F32), later_weights, batch, seq)

    sink_rows = jnp.broadcast_to(
        (attn_sinks[l].astype(F32) * LOG2E).reshape(SWA_KV_HEADS, 1, SWA_GROUP, 1),
        (SWA_KV_HEADS, 1, SWA_GROUP, SWA_BLOCK)).reshape(SWA_KV_HEADS, 1, SWA_GROUP * SWA_BLOCK)
    ya = _swa(qa_t, ka.reshape(batch, seq, SWA_KV_W), va_t, rows, sink_rows)

    yb = _moba(qb_t, kb.reshape(batch, seq, MOBA_W), vb_t, sel, rows)

    out = _post(x2, ya.reshape(batch * seq, SWA_Q_W), yb.reshape(batch * seq, MOBA_W), ga, gb,
                mod4, norm2_g[l][None, :], final_g[None, :], wa, wb, wo, w1, w2, seq)
    return out.reshape(batch, seq, D_MODEL)
```

```python
import functools
import math

import jax
import jax.numpy as jnp
import numpy as np
from jax import lax
from jax.experimental import pallas as pl
from jax.experimental.pallas import tpu as pltpu

D_MODEL = 1024
HEAD_DIM = 64
ATTN_SCALE = HEAD_DIM ** -0.5
LOG2E = math.log2(math.e)
SWA_Q_HEADS = 8
SWA_KV_HEADS = 2
SWA_GROUP = SWA_Q_HEADS // SWA_KV_HEADS
SWA_WINDOW = 128
SWA_BLOCK = 128
MOBA_HEADS = 8
MOBA_BLOCK = 256
MOBA_TOPK = 3
NUM_BUCKETS = 32
MAX_EXACT = NUM_BUCKETS // 2
MAX_DISTANCE = 2048
N_ATTN_HEADS = SWA_Q_HEADS + MOBA_HEADS
SWA_Q_W = SWA_Q_HEADS * HEAD_DIM
SWA_KV_W = SWA_KV_HEADS * HEAD_DIM
MOBA_W = MOBA_HEADS * HEAD_DIM
D_FF = 4 * D_MODEL
N_MOD = 6
RMS_EPS = 1e-6

VMEM_LIMIT_BYTES = 56 * 1024 * 1024
TOKEN_TILE = 512
FF_CHUNK = 1024
BIAS_SLOTS = 8
SWA_UNROLL = 3
ACC_ROWS = HEAD_DIM + 16
NEG_INF = float("-inf")

F32 = jnp.float32
BF16 = jnp.bfloat16


def _resident(block_shape, index_map):
    return pl.BlockSpec(block_shape, index_map, pipeline_mode=pl.Buffered(1))


def _dot(a, b):
    return jnp.dot(a, b, preferred_element_type=F32)


def _dot_nt(a, b):
    return lax.dot_general(a, b, (((1,), (1,)), ((), ())), preferred_element_type=F32)


def _rms_modulate(xv, g, shift, scale):
    ms = jnp.mean(xv * xv, axis=-1, keepdims=True)
    y = xv * lax.rsqrt(ms + RMS_EPS)
    return (y * g) * (1.0 + scale) + shift


def _toeplitz(u, rows):
    wide = jnp.broadcast_to(u, (rows, 2 * rows))
    return pltpu.roll(wide, 0, 1, stride=1, stride_axis=0)[:, rows:]


def _ada_kernel(ct_ref, w_ref, b_ref, o_ref, *, batch):
    cs_t = jax.nn.silu(ct_ref[...])
    w = w_ref[...]
    rows = [jnp.sum(w * cs_t[:, b:b + 1], axis=0, keepdims=True) for b in range(batch)]
    rows.append(jnp.zeros((8 - batch, w.shape[1]), F32))
    o_ref[...] = jnp.concatenate(rows, axis=0) + b_ref[...]


def _ada(c_t, w, b, batch):
    n = w.shape[1]
    bn = 1536
    return pl.pallas_call(
        functools.partial(_ada_kernel, batch=batch),
        grid=(n // bn,),
        in_specs=[pl.BlockSpec((D_MODEL, 8), lambda j: (0, 0)),
                  pl.BlockSpec((D_MODEL, bn), lambda j: (0, j)),
                  pl.BlockSpec((1, bn), lambda j: (0, j))],
        out_specs=pl.BlockSpec((8, bn), lambda j: (0, j)),
        out_shape=jax.ShapeDtypeStruct((8, n), F32),
        compiler_params=pltpu.CompilerParams(dimension_semantics=("arbitrary",),
                                             vmem_limit_bytes=VMEM_LIMIT_BYTES),
        name="ada",
    )(c_t, w, b)


def _t5_bucket(dist):
    n = np.maximum(dist, 0)
    nf = np.maximum(n, 1).astype(np.float32)
    large = MAX_EXACT + (np.log(nf / np.float32(MAX_EXACT)) / np.float32(math.log(MAX_DISTANCE / MAX_EXACT))
                         * np.float32(NUM_BUCKETS - MAX_EXACT)).astype(np.int32)
    large = np.minimum(large, NUM_BUCKETS - 1)
    return np.where(n < MAX_EXACT, n, large).astype(np.int32)


def _bias_rows_kernel(bucket_ref, rbt_ref, o_ref):
    bucket = bucket_ref[...]
    n = bucket.shape[1]
    onehot = jnp.where(lax.broadcasted_iota(jnp.int32, (NUM_BUCKETS, n), 0) == bucket, 1.0, 0.0)
    rows = jnp.dot(rbt_ref[...], onehot, preferred_element_type=F32,
                   precision=lax.Precision.HIGHEST)
    rows = jnp.where(bucket >= 0, rows * LOG2E, NEG_INF)
    for d in range(BIAS_SLOTS):
        o_ref[d] = rows[:, d * MOBA_BLOCK:(d + 2) * MOBA_BLOCK]


def _bias_rows(rel_bias):
    n = (BIAS_SLOTS + 1) * MOBA_BLOCK
    dist = np.arange(n, dtype=np.int32) - MOBA_BLOCK
    bucket = jnp.asarray(np.where(dist >= 0, _t5_bucket(dist), -1)[None, :], dtype=jnp.int32)
    return pl.pallas_call(
        _bias_rows_kernel,
        out_shape=jax.ShapeDtypeStruct((BIAS_SLOTS, N_ATTN_HEADS, 2 * MOBA_BLOCK), F32),
        name="bias_rows",
    )(bucket, rel_bias.astype(F32).T)


IN_WIDTHS = (SWA_Q_W, SWA_KV_W, SWA_KV_W, MOBA_W, MOBA_W, MOBA_W, D_MODEL, D_MODEL)
IN_OFFS = tuple(sum(IN_WIDTHS[:k]) for k in range(len(IN_WIDTHS) + 1))
FEAT_COLS = (0, 2, 3, 5)
FEAT_W = sum(IN_WIDTHS[k] for k in FEAT_COLS)
TOK_COLS = (1, 4, 6, 7)
TOK_W = sum(IN_WIDTHS[k] for k in TOK_COLS)
Q_COLS = (0, 3)


def _proj_kernel(x_ref, shift_ref, scale_ref, g_ref, w_ref, *refs):
    n_cast = (len(refs) - 12) // 2
    cast_in, refs = refs[:n_cast], refs[n_cast:]
    ka_ref, kb_ref, ga_ref, gb_ref, qa_ref, va_ref, qb_ref, vb_ref, sel_ref = refs[:9]
    cast_out, (wk_ref, wt_ref, kmean_ref) = refs[9:9 + n_cast], refs[9 + n_cast:]

    for src, dst in zip(cast_in, cast_out):
        dst[...] = src[...].astype(dst.dtype)

    tile = pl.program_id(0) % (kmean_ref.shape[0] // BLOCKS_PER_TILE)

    @pl.when(tile == 0)
    def _():
        kmean_ref[...] = jnp.zeros(kmean_ref.shape, F32)

    @pl.when(pl.program_id(0) == 0)
    def _():
        o = 0
        for k in TOK_COLS:
            wk_ref[:, o:o + IN_WIDTHS[k]] = w_ref[:, IN_OFFS[k]:IN_OFFS[k + 1]].astype(BF16)
            o += IN_WIDTHS[k]
        o = 0
        for k in FEAT_COLS:
            wcol = w_ref[:, IN_OFFS[k]:IN_OFFS[k + 1]]
            if k in Q_COLS:
                wcol = wcol * (ATTN_SCALE * LOG2E)
            wt_ref[o:o + IN_WIDTHS[k], :] = wcol.T.astype(BF16)
            o += IN_WIDTHS[k]

    h = _rms_modulate(x_ref[...], g_ref[...], shift_ref[0, 0], scale_ref[0, 0]).astype(BF16)
    tok_refs = dict(zip(TOK_COLS, (ka_ref, kb_ref, ga_ref, gb_ref)))
    feat_refs = dict(zip(FEAT_COLS, (qa_ref, va_ref, qb_ref, vb_ref)))

    def token_major(k):
        o = sum(IN_WIDTHS[c] for c in TOK_COLS[:TOK_COLS.index(k)])
        out = _dot(h, wk_ref[:, o:o + IN_WIDTHS[k]])
        tok_refs[k][...] = out.astype(BF16)
        return out

    def feature_major(k):
        o = sum(IN_WIDTHS[c] for c in FEAT_COLS[:FEAT_COLS.index(k)])
        out = _dot_nt(wt_ref[o:o + IN_WIDTHS[k], :], h).astype(BF16)
        blk = feat_refs[k].shape[-1]
        for t in range(TOKEN_TILE // blk):
            feat_refs[k][0, t] = out[:, t * blk:(t + 1) * blk]
        return out

    _update_kmean(token_major(4), kmean_ref, tile)
    qb_t = feature_major(3)
    token_major(1)
    token_major(6)
    _moba_select(qb_t, sel_ref, kmean_ref, tile)
    token_major(7)
    for k in (0, 2, 5):
        feature_major(k)


BLOCKS_PER_TILE = TOKEN_TILE // MOBA_BLOCK


def _update_kmean(kb, kmean_ref, tile):
    for t in range(BLOCKS_PER_TILE):
        kmean_ref[pl.ds(tile * BLOCKS_PER_TILE + t, 1), :] = jnp.mean(
            kb[t * MOBA_BLOCK:(t + 1) * MOBA_BLOCK, :], axis=0, keepdims=True)


def _moba_select(qb_t, sel_ref, kmean_ref, tile):
    nblk = kmean_ref.shape[0]
    blocks_per_tile = BLOCKS_PER_TILE
    blk = lax.broadcasted_iota(jnp.int32, (nblk, TOKEN_TILE), 0)
    qblk = tile * blocks_per_tile + lax.broadcasted_iota(jnp.int32, (nblk, TOKEN_TILE), 1) // MOBA_BLOCK
    past = blk < qblk
    own = jnp.where(blk == qblk, 0.0, NEG_INF)
    blk_f = blk.astype(F32)
    for h in range(MOBA_HEADS):
        hs = slice(h * HEAD_DIM, (h + 1) * HEAD_DIM)
        kmean = kmean_ref[:, hs]
        k1 = kmean.astype(BF16)
        r1 = kmean - k1.astype(F32)
        k2 = r1.astype(BF16)
        k3 = (r1 - k2.astype(F32)).astype(BF16)
        parts = _dot(jnp.concatenate([k1, k2, k3], axis=0), qb_t[hs, :])
        gate = parts[0:nblk] + parts[nblk:2 * nblk] + parts[2 * nblk:3 * nblk]
        gate = jnp.where(past, gate, NEG_INF)
        mask = own
        for _ in range(MOBA_TOPK):
            top = jnp.max(gate, axis=0, keepdims=True)
            first = jnp.min(jnp.where(gate == top, blk_f, float(nblk)), axis=0, keepdims=True)
            pick = blk_f == first
            mask = jnp.where(jnp.logical_and(pick, past), 0.0, mask)
            gate = jnp.where(pick, NEG_INF, gate)
        sel_ref[0, h] = mask


def _proj(x2, mod4, g1, w_in, cast_weights, batch, seq):
    tokens = x2.shape[0]
    steps = tokens // TOKEN_TILE
    assert all(w.shape[0] % (16 * steps) == 0 for w in cast_weights)
    cast_specs = [pl.BlockSpec((w.shape[0] // steps, w.shape[1]), lambda i: (i, 0)) for w in cast_weights]
    tiles_per_seq = seq // TOKEN_TILE
    na = TOKEN_TILE // SWA_BLOCK
    nb = TOKEN_TILE // MOBA_BLOCK

    def feat_spec(width, blk, per_tile):
        return pl.BlockSpec((1, per_tile, width, blk),
                            lambda i: (i // tiles_per_seq, i % tiles_per_seq, 0, 0))

    out_shape = (
        jax.ShapeDtypeStruct((tokens, SWA_KV_W), BF16),
        jax.ShapeDtypeStruct((tokens, MOBA_W), BF16),
        jax.ShapeDtypeStruct((tokens, D_MODEL), BF16),
        jax.ShapeDtypeStruct((tokens, D_MODEL), BF16),
        jax.ShapeDtypeStruct((batch, seq // SWA_BLOCK, SWA_Q_W, SWA_BLOCK), BF16),
        jax.ShapeDtypeStruct((batch, seq // SWA_BLOCK, SWA_KV_W, SWA_BLOCK), BF16),
        jax.ShapeDtypeStruct((batch, seq // MOBA_BLOCK, MOBA_W, MOBA_BLOCK), BF16),
        jax.ShapeDtypeStruct((batch, seq // MOBA_BLOCK, MOBA_W, MOBA_BLOCK), BF16),
        jax.ShapeDtypeStruct((batch, MOBA_HEADS, seq // MOBA_BLOCK, seq), F32),
    )
    tok_spec = lambda w: pl.BlockSpec((TOKEN_TILE, w), lambda i: (i, 0))
    return pl.pallas_call(
        _proj_kernel,
        grid=(steps,),
        in_specs=[
            pl.BlockSpec((TOKEN_TILE, D_MODEL), lambda i: (i, 0)),
            pl.BlockSpec((1, 1, 1, D_MODEL), lambda i: (i // tiles_per_seq, 0, 0, 0)),
            pl.BlockSpec((1, 1, 1, D_MODEL), lambda i: (i // tiles_per_seq, 1, 0, 0)),
            _resident((1, D_MODEL), lambda i: (0, 0)),
            _resident(w_in.shape, lambda i: (0, 0)),
        ] + cast_specs,
        out_specs=(tok_spec(SWA_KV_W), tok_spec(MOBA_W), tok_spec(D_MODEL), tok_spec(D_MODEL),
                   feat_spec(SWA_Q_W, SWA_BLOCK, na), feat_spec(SWA_KV_W, SWA_BLOCK, na),
                   feat_spec(MOBA_W, MOBA_BLOCK, nb), feat_spec(MOBA_W, MOBA_BLOCK, nb),
                   pl.BlockSpec((1, MOBA_HEADS, seq // MOBA_BLOCK, TOKEN_TILE),
                                lambda i: (i // tiles_per_seq, 0, 0, i % tiles_per_seq)))
        + tuple(cast_specs),
        out_shape=out_shape + tuple(jax.ShapeDtypeStruct(w.shape, BF16) for w in cast_weights),
        scratch_shapes=[pltpu.VMEM((D_MODEL, TOK_W), BF16), pltpu.VMEM((FEAT_W, D_MODEL), BF16),
                        pltpu.VMEM((seq // MOBA_BLOCK, MOBA_W), F32)],
        compiler_params=pltpu.CompilerParams(dimension_semantics=("arbitrary",),
                                             vmem_limit_bytes=VMEM_LIMIT_BYTES),
        name="proj",
    )(x2, mod4, mod4, g1, w_in, *cast_weights)


def _swa_kernel(q_ref, k_ref, v_ref, rows_ref, sink_ref, o_ref, bias_ref, ot_ref, s_ref):
    L = SWA_BLOCK
    nblk = q_ref.shape[1]

    @pl.when(pl.program_id(0) == 0)
    def _():
        k_idx = lax.broadcasted_iota(jnp.int32, (L, L), 0)
        q_idx = lax.broadcasted_iota(jnp.int32, (L, L), 1)
        for h in range(SWA_Q_HEADS):
            g, gi = divmod(h, SWA_GROUP)
            t0 = rows_ref[0, h:h + 1, MOBA_BLOCK:MOBA_BLOCK + L]
            r = _toeplitz(jnp.concatenate([t0, t0], axis=1), L)
            cols = slice(gi * L, (gi + 1) * L)
            bias_ref[g, 0:L, cols] = jnp.where(q_idx < k_idx, r, NEG_INF)
            bias_ref[g, L:2 * L, cols] = jnp.where(q_idx >= k_idx, r, NEG_INF)

    ones_blk = jnp.concatenate([jnp.ones((1, 2 * L), BF16), jnp.zeros((ACC_ROWS - HEAD_DIM - 1, 2 * L), BF16)],
                               axis=0)
    PW = 2 * L
    chains = [(g, hp) for g in range(SWA_KV_HEADS) for hp in range(SWA_GROUP // 2)]
    kv_cols = [slice(g * HEAD_DIM, (g + 1) * HEAD_DIM) for g in range(SWA_KV_HEADS)]

    def q_pair(c, g, hp):
        h0 = g * SWA_GROUP + 2 * hp
        return jnp.concatenate([q_ref[0, c, (h0 + e) * HEAD_DIM:(h0 + e + 1) * HEAD_DIM, :] for e in range(2)],
                               axis=1)

    def scores_first(idx):
        g, hp = chains[idx]
        lanes = slice(hp * PW, (hp + 1) * PW)
        s_ref[idx, 0:L, :] = jnp.full((L, PW), NEG_INF, F32)
        s_ref[idx, L:2 * L, :] = _dot(k_ref[0, 0:L, kv_cols[g]], q_pair(0, g, hp)) + bias_ref[g, L:2 * L, lanes]

    def scores(c, idx):
        g, hp = chains[idx]
        kw = k_ref[0, pl.ds(pl.multiple_of((c - 1) * L, L), 2 * L), kv_cols[g]]
        s_ref[idx] = _dot(kw, q_pair(c, g, hp)) + bias_ref[g, :, hp * PW:(hp + 1) * PW]

    def attend(c, c_prev, idx):
        g, hp = chains[idx]
        hs = kv_cols[g]
        s = s_ref[idx]
        sink = sink_ref[g, :, hp * PW:(hp + 1) * PW]
        m = jnp.maximum(jnp.max(s, axis=0, keepdims=True), sink)
        p = jnp.exp2(s - m).astype(BF16)
        v_win = jnp.concatenate([v_ref[0, c_prev, hs, :], v_ref[0, c, hs, :]], axis=1)
        pv = _dot(jnp.concatenate([v_win, ones_blk], axis=0), p)
        o = pv[0:HEAD_DIM] / (pv[HEAD_DIM:HEAD_DIM + 1] + jnp.exp2(sink - m))
        for e in range(2):
            r = (g * SWA_GROUP + 2 * hp + e) * HEAD_DIM
            ot_ref[r:r + HEAD_DIM, :] = o[:, e * L:(e + 1) * L]

    def emit(c):
        o_ref[0, pl.ds(pl.multiple_of(c * L, L), L), :] = ot_ref[...].T.astype(o_ref.dtype)

    for idx in range(len(chains)):
        scores_first(idx)
    for idx in range(len(chains)):
        attend(0, 0, idx)
        scores(1, idx)
    emit(0)

    def body(c, carry):
        for idx in range(len(chains)):
            attend(c, c - 1, idx)
            scores(c + 1, idx)
        emit(c)
        return carry

    def body_unrolled(u, carry):
        for e in range(SWA_UNROLL):
            carry = body(SWA_UNROLL * u + 1 + e, carry)
        return carry

    lax.fori_loop(0, (nblk - 2) // SWA_UNROLL, body_unrolled, 0)
    for idx in range(len(chains)):
        attend(nblk - 1, nblk - 2, idx)
    emit(nblk - 1)


def _swa(qa_t, ka3, va_t, rows, sink_rows):
    batch, nblk = qa_t.shape[0], qa_t.shape[1]
    seq = nblk * SWA_BLOCK
    assert nblk >= 2 and (nblk - 2) % SWA_UNROLL == 0
    return pl.pallas_call(
        _swa_kernel,
        grid=(batch,),
        in_specs=[
            pl.BlockSpec((1, nblk, SWA_Q_W, SWA_BLOCK), lambda b: (b, 0, 0, 0)),
            pl.BlockSpec((1, seq, SWA_KV_W), lambda b: (b, 0, 0)),
            pl.BlockSpec((1, nblk, SWA_KV_W, SWA_BLOCK), lambda b: (b, 0, 0, 0)),
            _resident((1, N_ATTN_HEADS, 2 * MOBA_BLOCK), lambda b: (0, 0, 0)),
            _resident(sink_rows.shape, lambda b: (0, 0, 0)),
        ],
        out_specs=pl.BlockSpec((1, seq, SWA_Q_W), lambda b: (b, 0, 0)),
        out_shape=jax.ShapeDtypeStruct((batch, seq, SWA_Q_W), BF16),
        scratch_shapes=[pltpu.VMEM((SWA_KV_HEADS, 2 * SWA_BLOCK, SWA_GROUP * SWA_BLOCK), F32),
                        pltpu.VMEM((SWA_Q_W, SWA_BLOCK), F32),
                        pltpu.VMEM((SWA_Q_HEADS // 2, 2 * SWA_BLOCK, 2 * SWA_BLOCK), F32)],
        compiler_params=pltpu.CompilerParams(dimension_semantics=("arbitrary",),
                                             vmem_limit_bytes=VMEM_LIMIT_BYTES),
        name="swa",
    )(qa_t, ka3, va_t, rows, sink_rows)


def _moba_kernel(q_ref, qn_ref, k_ref, v_ref, sel_ref, rows_ref, o_ref, bias_ref, m_ref, acc_ref, s_ref, cm_ref):
    i = pl.program_id(1)
    nblk = pl.num_programs(1)
    MB = MOBA_BLOCK

    @pl.when(jnp.logical_and(pl.program_id(0) == 0, i == 0))
    def _():
        for h in range(MOBA_HEADS):
            for d in range(BIAS_SLOTS):
                bias_ref[h, d] = _toeplitz(rows_ref[d, SWA_Q_HEADS + h:SWA_Q_HEADS + h + 1, :], MB)

    m_ref[...] = jnp.full(m_ref.shape, NEG_INF, F32)
    acc_ref[...] = jnp.zeros(acc_ref.shape, F32)
    heads = [slice(h * HEAD_DIM, (h + 1) * HEAD_DIM) for h in range(MOBA_HEADS)]
    ones_blk = jnp.concatenate([jnp.ones((1, MB), BF16), jnp.zeros((ACC_ROWS - HEAD_DIM - 1, MB), BF16)],
                               axis=0)

    def scores(j, slot, h, hs):
        rows = pl.ds(pl.multiple_of(j * MB, MB), MB)
        s = _dot(k_ref[0, rows, hs], q_ref[0, 0, hs, :]) + bias_ref[h, slot]
        s_ref[h] = s
        cm_ref[h] = jnp.max(s, axis=0, keepdims=True) + sel_ref[0, h, pl.ds(j, 1), :]

    def own_scores(blk, qr, h, hs):
        rows = pl.ds(pl.multiple_of(blk * MB, MB), MB)
        s = _dot(k_ref[0, rows, hs], qr[0, 0, hs, :]) + bias_ref[h, 0]
        s_ref[h] = s
        cm_ref[h] = jnp.max(s, axis=0, keepdims=True)

    def accumulate(j, h, hs):
        m_old = m_ref[h]
        m_new = jnp.maximum(m_old, cm_ref[h])
        alpha = jnp.exp2(m_old - m_new)
        p = jnp.exp2(s_ref[h] - (m_new - sel_ref[0, h, pl.ds(j, 1), :])).astype(BF16)
        v_aug = jnp.concatenate([v_ref[0, j, hs, :], ones_blk], axis=0)
        rs = slice(h * ACC_ROWS, (h + 1) * ACC_ROWS)
        acc_ref[rs, :] = alpha * acc_ref[rs, :] + _dot(v_aug, p)
        m_ref[h] = m_new

    @pl.when(i == 0)
    def _():
        for h, hs in enumerate(heads):
            own_scores(i, q_ref, h, hs)

    def body(t, carry):
        j = i - t
        slot = jnp.minimum(t + 1, BIAS_SLOTS - 1)
        for h, hs in enumerate(heads):
            accumulate(j, h, hs)
            scores(j - 1, slot, h, hs)
        return carry

    def body2(u, carry):
        return body(2 * u + 1, body(2 * u, carry))

    lax.fori_loop(0, i // 2, body2, 0)

    @pl.when(i % 2 == 1)
    def _():
        body(i - 1, 0)

    @pl.when(i < nblk - 1)
    def _():
        for h, hs in enumerate(heads):
            accumulate(0, h, hs)
            own_scores(i + 1, qn_ref, h, hs)

    @pl.when(i == nblk - 1)
    def _():
        for h, hs in enumerate(heads):
            accumulate(0, h, hs)

    outs = []
    for h, hs in enumerate(heads):
        base = h * ACC_ROWS
        outs.append(acc_ref[base:base + HEAD_DIM, :] / acc_ref[base + HEAD_DIM:base + HEAD_DIM + 1, :])
    o_ref[0] = jnp.concatenate(outs, axis=0).T.astype(o_ref.dtype)


def _moba(qb_t, kb3, vb_t, sel, rows):
    batch, nblk = qb_t.shape[0], qb_t.shape[1]
    seq = nblk * MOBA_BLOCK
    return pl.pallas_call(
        _moba_kernel,
        grid=(batch, nblk),
        in_specs=[
            pl.BlockSpec((1, 1, MOBA_W, MOBA_BLOCK), lambda b, i: (b, i, 0, 0)),
            pl.BlockSpec((1, 1, MOBA_W, MOBA_BLOCK), lambda b, i: (b, jnp.minimum(i + 1, nblk - 1), 0, 0)),
            pl.BlockSpec((1, seq, MOBA_W), lambda b, i: (b, 0, 0)),
            pl.BlockSpec((1, nblk, MOBA_W, MOBA_BLOCK), lambda b, i: (b, 0, 0, 0)),
            pl.BlockSpec((1, MOBA_HEADS, nblk, MOBA_BLOCK), lambda b, i: (b, 0, 0, i)),
            _resident(rows.shape, lambda b, i: (0, 0, 0)),
        ],
        out_specs=pl.BlockSpec((1, MOBA_BLOCK, MOBA_W), lambda b, i: (b, i, 0)),
        out_shape=jax.ShapeDtypeStruct((batch, seq, MOBA_W), BF16),
        scratch_shapes=[pltpu.VMEM((MOBA_HEADS, BIAS_SLOTS, MOBA_BLOCK, MOBA_BLOCK), F32),
                        pltpu.VMEM((MOBA_HEADS, 1, MOBA_BLOCK), F32),
                        pltpu.VMEM((MOBA_HEADS * ACC_ROWS, MOBA_BLOCK), F32),
                        pltpu.VMEM((MOBA_HEADS, MOBA_BLOCK, MOBA_BLOCK), F32),
                        pltpu.VMEM((MOBA_HEADS, 1, MOBA_BLOCK), F32)],
        compiler_params=pltpu.CompilerParams(dimension_semantics=("arbitrary", "arbitrary"),
                                             vmem_limit_bytes=VMEM_LIMIT_BYTES),
        name="moba",
    )(qb_t, qb_t, kb3, vb_t, sel, rows)


def _post_kernel(x_ref, ya_ref, yb_ref, ga_ref, gb_ref, gate1_ref, shift2_ref, scale2_ref,
                 gate2_ref, g2_ref, gf_ref, wa_ref, wb_ref, wo_ref, w1_ref, w2_ref, o_ref):
    a = _dot(ya_ref[...], wa_ref[...])
    b = _dot(yb_ref[...], wb_ref[...])
    merged = (jax.nn.sigmoid(ga_ref[...].astype(F32)) * a
              + jax.nn.sigmoid(gb_ref[...].astype(F32)) * b)
    x1 = x_ref[...] + gate1_ref[0, 0] * _dot(merged.astype(BF16), wo_ref[...])
    h2 = _rms_modulate(x1, g2_ref[...], shift2_ref[0, 0], scale2_ref[0, 0]).astype(BF16)
    y = jnp.zeros_like(x1)
    for c in range(D_FF // FF_CHUNK):
        cs = slice(c * FF_CHUNK, (c + 1) * FF_CHUNK)
        u = jnp.square(jnp.maximum(_dot(h2, w1_ref[:, cs]), 0.0)).astype(BF16)
        y = y + _dot(u, w2_ref[cs, :])
    x2 = x1 + gate2_ref[0, 0] * y
    ms = jnp.mean(x2 * x2, axis=-1, keepdims=True)
    o_ref[...] = (x2 * lax.rsqrt(ms + RMS_EPS)) * gf_ref[...]


def _post(x2, ya, yb, ga, gb, mod4, g2, gf, wa, wb, wo, w1, w2, seq):
    tokens = x2.shape[0]
    tiles_per_seq = seq // TOKEN_TILE
    tok = lambda w: pl.BlockSpec((TOKEN_TILE, w), lambda i: (i, 0))
    modrow = lambda k: pl.BlockSpec((1, 1, 1, D_MODEL), lambda i: (i // tiles_per_seq, k, 0, 0))
    full = lambda a: _resident(a.shape, lambda i: (0, 0))
    return pl.pallas_call(
        _post_kernel,
        grid=(tokens // TOKEN_TILE,),
        in_specs=[tok(D_MODEL), tok(SWA_Q_W), tok(MOBA_W), tok(D_MODEL), tok(D_MODEL),
                  modrow(2), modrow(3), modrow(4), modrow(5),
                  full(g2), full(gf), full(wa), full(wb), full(wo), full(w1), full(w2)],
        out_specs=tok(D_MODEL),
        out_shape=jax.ShapeDtypeStruct((tokens, D_MODEL), F32),
        compiler_params=pltpu.CompilerParams(dimension_semantics=("arbitrary",),
                                             vmem_limit_bytes=VMEM_LIMIT_BYTES),
        name="post",
    )(x2, ya, yb, ga, gb, mod4, mod4, mod4, mod4, g2, gf, wa, wb, wo, w1, w2)


def kernel(x, c, ada_w, ada_b, norm1_g, norm2_g, w_in, attn_sinks, rel_bias, w_branch_a,
           w_branch_b, w_out, w_mlp_in, w_mlp_out, final_g):
    batch, seq, _ = x.shape
    depth = ada_w.shape[0]
    nblk_b = seq // MOBA_BLOCK
    assert seq % TOKEN_TILE == 0 and TOKEN_TILE % MOBA_BLOCK == 0 and batch <= 8
    assert (BIAS_SLOTS - 1) * MOBA_BLOCK - (MOBA_BLOCK - 1) >= 1513 and MAX_DISTANCE == 2048
    assert 2 * SWA_BLOCK <= MOBA_BLOCK and SWA_WINDOW == SWA_BLOCK
    assert depth == 1
    l = 0

    rows = _bias_rows(rel_bias)
    c_t = jnp.zeros((D_MODEL, 8), F32).at[:, :batch].set(c.astype(F32).T)

    x2 = x.reshape(batch * seq, D_MODEL)
    mod = _ada(c_t, ada_w[l], ada_b[l][None, :], batch)
    mod4 = mod[:batch].reshape(batch, N_MOD, 1, D_MODEL)
    later_weights = [w.astype(F32) for w in (w_branch_a[l], w_branch_b[l], w_out[l], w_mlp_in[l], w_mlp_out[l])]
    ka, kb, ga, gb, qa_t, va_t, qb_t, vb_t, sel, wa, wb, wo, w1, w2 = _proj(
        x2, mod4, norm1_g[l][None, :], w_in[l].astype(F32), later_weights, batch, seq)

    sink_rows = jnp.broadcast_to(
        (attn_sinks[l].astype(F32) * LOG2E).reshape(SWA_KV_HEADS, 1, SWA_GROUP, 1),
        (SWA_KV_HEADS, 1, SWA_GROUP, SWA_BLOCK)).reshape(SWA_KV_HEADS, 1, SWA_GROUP * SWA_BLOCK)
    ya = _swa(qa_t, ka.reshape(batch, seq, SWA_KV_W), va_t, rows, sink_rows)

    yb = _moba(qb_t, kb.reshape(batch, seq, MOBA_W), vb_t, sel, rows)

    out = _post(x2, ya.reshape(batch * seq, SWA_Q_W), yb.reshape(batch * seq, MOBA_W), ga, gb,
                mod4, norm2_g[l][None, :], final_g[None, :], wa, wb, wo, w1, w2, seq)
    return out.reshape(batch, seq, D_MODEL)
```

```python
import functools
import math

import jax
import jax.numpy as jnp
import numpy as np
from jax import lax
from jax.experimental import pallas as pl
from jax.experimental.pallas import tpu as pltpu

D_MODEL = 1024
HEAD_DIM = 64
ATTN_SCALE = HEAD_DIM ** -0.5
LOG2E = math.log2(math.e)
SWA_Q_HEADS = 8
SWA_KV_HEADS = 2
SWA_GROUP = SWA_Q_HEADS // SWA_KV_HEADS
SWA_WINDOW = 128
SWA_BLOCK = 128
MOBA_HEADS = 8
MOBA_BLOCK = 256
MOBA_TOPK = 3
NUM_BUCKETS = 32
MAX_EXACT = NUM_BUCKETS // 2
MAX_DISTANCE = 2048
N_ATTN_HEADS = SWA_Q_HEADS + MOBA_HEADS
SWA_Q_W = SWA_Q_HEADS * HEAD_DIM
SWA_KV_W = SWA_KV_HEADS * HEAD_DIM
MOBA_W = MOBA_HEADS * HEAD_DIM
D_FF = 4 * D_MODEL
N_MOD = 6
RMS_EPS = 1e-6

VMEM_LIMIT_BYTES = 56 * 1024 * 1024
TOKEN_TILE = 512
FF_CHUNK = 1024
SUBLANES = 8
ADA_COLS = 1536
BIAS_SLOTS = 8
SWA_UNROLL = 3
ACC_ROWS = HEAD_DIM + 16
NEG_INF = float("-inf")

F32 = jnp.float32
BF16 = jnp.bfloat16


def _resident(block_shape, index_map):
    return pl.BlockSpec(block_shape, index_map, pipeline_mode=pl.Buffered(1))


def _dot(a, b):
    return jnp.dot(a, b, preferred_element_type=F32)


def _dot_nt(a, b):
    return lax.dot_general(a, b, (((1,), (1,)), ((), ())), preferred_element_type=F32)


def _rms_modulate(xv, g, shift, scale):
    ms = jnp.mean(xv * xv, axis=-1, keepdims=True)
    y = xv * lax.rsqrt(ms + RMS_EPS)
    return (y * g) * (1.0 + scale) + shift


def _toeplitz(u, rows):
    wide = jnp.broadcast_to(u, (rows, 2 * rows))
    return pltpu.roll(wide, 0, 1, stride=1, stride_axis=0)[:, rows:]


def _ada_kernel(ct_ref, w_ref, b_ref, o_ref, *, batch):
    cs_t = jax.nn.silu(ct_ref[...])
    w = w_ref[...]
    rows = [jnp.sum(w * cs_t[:, b:b + 1], axis=0, keepdims=True) for b in range(batch)]
    rows.append(jnp.zeros((SUBLANES - batch, w.shape[1]), F32))
    o_ref[...] = jnp.concatenate(rows, axis=0) + b_ref[...]


def _ada(c_t, w, b, batch):
    n = w.shape[1]
    bn = ADA_COLS
    assert n % bn == 0
    return pl.pallas_call(
        functools.partial(_ada_kernel, batch=batch),
        grid=(n // bn,),
        in_specs=[pl.BlockSpec((D_MODEL, SUBLANES), lambda j: (0, 0)),
                  pl.BlockSpec((D_MODEL, bn), lambda j: (0, j)),
                  pl.BlockSpec((1, bn), lambda j: (0, j))],
        out_specs=pl.BlockSpec((SUBLANES, bn), lambda j: (0, j)),
        out_shape=jax.ShapeDtypeStruct((SUBLANES, n), F32),
        compiler_params=pltpu.CompilerParams(dimension_semantics=("arbitrary",),
                                             vmem_limit_bytes=VMEM_LIMIT_BYTES),
        name="ada",
    )(c_t, w, b)


def _t5_bucket(dist):
    n = np.maximum(dist, 0)
    nf = np.maximum(n, 1).astype(np.float32)
    large = MAX_EXACT + (np.log(nf / np.float32(MAX_EXACT)) / np.float32(math.log(MAX_DISTANCE / MAX_EXACT))
                         * np.float32(NUM_BUCKETS - MAX_EXACT)).astype(np.int32)
    large = np.minimum(large, NUM_BUCKETS - 1)
    return np.where(n < MAX_EXACT, n, large).astype(np.int32)


def _bias_rows_kernel(bucket_ref, rbt_ref, o_ref):
    bucket = bucket_ref[...]
    n = bucket.shape[1]
    onehot = jnp.where(lax.broadcasted_iota(jnp.int32, (NUM_BUCKETS, n), 0) == bucket, 1.0, 0.0)
    rows = jnp.dot(rbt_ref[...], onehot, preferred_element_type=F32,
                   precision=lax.Precision.HIGHEST)
    rows = jnp.where(bucket >= 0, rows * LOG2E, NEG_INF)
    for d in range(BIAS_SLOTS):
        o_ref[d] = rows[:, d * MOBA_BLOCK:(d + 2) * MOBA_BLOCK]


def _bias_rows(rel_bias):
    n = (BIAS_SLOTS + 1) * MOBA_BLOCK
    dist = np.arange(n, dtype=np.int32) - MOBA_BLOCK
    bucket = jnp.asarray(np.where(dist >= 0, _t5_bucket(dist), -1)[None, :], dtype=jnp.int32)
    return pl.pallas_call(
        _bias_rows_kernel,
        out_shape=jax.ShapeDtypeStruct((BIAS_SLOTS, N_ATTN_HEADS, 2 * MOBA_BLOCK), F32),
        name="bias_rows",
    )(bucket, rel_bias.astype(F32).T)


IN_WIDTHS = (SWA_Q_W, SWA_KV_W, SWA_KV_W, MOBA_W, MOBA_W, MOBA_W, D_MODEL, D_MODEL)
IN_OFFS = tuple(sum(IN_WIDTHS[:k]) for k in range(len(IN_WIDTHS) + 1))
FEAT_COLS = (0, 2, 3, 5)
FEAT_W = sum(IN_WIDTHS[k] for k in FEAT_COLS)
TOK_COLS = (1, 4, 6, 7)
TOK_W = sum(IN_WIDTHS[k] for k in TOK_COLS)
Q_COLS = (0, 3)
BLOCKS_PER_TILE = TOKEN_TILE // MOBA_BLOCK


def _proj_kernel(x_ref, shift_ref, scale_ref, g_ref, w_ref, *refs):
    n_cast = (len(refs) - 12) // 2
    cast_in, refs = refs[:n_cast], refs[n_cast:]
    ka_ref, kb_ref, ga_ref, gb_ref, qa_ref, va_ref, qb_ref, vb_ref, sel_ref = refs[:9]
    cast_out, (wk_ref, wt_ref, kmean_ref) = refs[9:9 + n_cast], refs[9 + n_cast:]

    for src, dst in zip(cast_in, cast_out):
        dst[...] = src[...].astype(dst.dtype)

    tile = pl.program_id(0) % (kmean_ref.shape[0] // BLOCKS_PER_TILE)

    @pl.when(tile == 0)
    def _():
        kmean_ref[...] = jnp.zeros(kmean_ref.shape, F32)

    @pl.when(pl.program_id(0) == 0)
    def _():
        o = 0
        for k in TOK_COLS:
            wk_ref[:, o:o + IN_WIDTHS[k]] = w_ref[:, IN_OFFS[k]:IN_OFFS[k + 1]].astype(BF16)
            o += IN_WIDTHS[k]
        o = 0
        for k in FEAT_COLS:
            wcol = w_ref[:, IN_OFFS[k]:IN_OFFS[k + 1]]
            if k in Q_COLS:
                wcol = wcol * (ATTN_SCALE * LOG2E)
            wt_ref[o:o + IN_WIDTHS[k], :] = wcol.T.astype(BF16)
            o += IN_WIDTHS[k]

    h = _rms_modulate(x_ref[...], g_ref[...], shift_ref[0, 0], scale_ref[0, 0]).astype(BF16)
    tok_refs = dict(zip(TOK_COLS, (ka_ref, kb_ref, ga_ref, gb_ref)))
    feat_refs = dict(zip(FEAT_COLS, (qa_ref, va_ref, qb_ref, vb_ref)))

    def token_major(k):
        o = sum(IN_WIDTHS[c] for c in TOK_COLS[:TOK_COLS.index(k)])
        out = _dot(h, wk_ref[:, o:o + IN_WIDTHS[k]])
        tok_refs[k][...] = out.astype(BF16)
        return out

    def feature_major(k):
        o = sum(IN_WIDTHS[c] for c in FEAT_COLS[:FEAT_COLS.index(k)])
        out = _dot_nt(wt_ref[o:o + IN_WIDTHS[k], :], h).astype(BF16)
        blk = feat_refs[k].shape[-1]
        for t in range(TOKEN_TILE // blk):
            feat_refs[k][0, t] = out[:, t * blk:(t + 1) * blk]
        return out

    _update_kmean(token_major(4), kmean_ref, tile)
    qb_t = feature_major(3)
    token_major(1)
    token_major(6)
    _moba_select(qb_t, sel_ref, kmean_ref, tile)
    token_major(7)
    for k in (0, 2, 5):
        feature_major(k)


def _update_kmean(kb, kmean_ref, tile):
    for t in range(BLOCKS_PER_TILE):
        kmean_ref[pl.ds(tile * BLOCKS_PER_TILE + t, 1), :] = jnp.mean(
            kb[t * MOBA_BLOCK:(t + 1) * MOBA_BLOCK, :], axis=0, keepdims=True)


def _moba_select(qb_t, sel_ref, kmean_ref, tile):
    nblk = kmean_ref.shape[0]
    blocks_per_tile = BLOCKS_PER_TILE
    blk = lax.broadcasted_iota(jnp.int32, (nblk, TOKEN_TILE), 0)
    qblk = tile * blocks_per_tile + lax.broadcasted_iota(jnp.int32, (nblk, TOKEN_TILE), 1) // MOBA_BLOCK
    past = blk < qblk
    own = jnp.where(blk == qblk, 0.0, NEG_INF)
    blk_f = blk.astype(F32)
    for h in range(MOBA_HEADS):
        hs = slice(h * HEAD_DIM, (h + 1) * HEAD_DIM)
        kmean = kmean_ref[:, hs]
        k1 = kmean.astype(BF16)
        r1 = kmean - k1.astype(F32)
        k2 = r1.astype(BF16)
        k3 = (r1 - k2.astype(F32)).astype(BF16)
        parts = _dot(jnp.concatenate([k1, k2, k3], axis=0), qb_t[hs, :])
        gate = parts[0:nblk] + parts[nblk:2 * nblk] + parts[2 * nblk:3 * nblk]
        gate = jnp.where(past, gate, NEG_INF)
        mask = own
        for _ in range(MOBA_TOPK):
            top = jnp.max(gate, axis=0, keepdims=True)
            first = jnp.min(jnp.where(gate == top, blk_f, float(nblk)), axis=0, keepdims=True)
            pick = blk_f == first
            mask = jnp.where(jnp.logical_and(pick, past), 0.0, mask)
            gate = jnp.where(pick, NEG_INF, gate)
        sel_ref[0, h] = mask


def _proj(x2, mod4, g1, w_in, cast_weights, batch, seq):
    tokens = x2.shape[0]
    steps = tokens // TOKEN_TILE
    assert all(w.shape[0] % (16 * steps) == 0 for w in cast_weights)
    cast_specs = [pl.BlockSpec((w.shape[0] // steps, w.shape[1]), lambda i: (i, 0)) for w in cast_weights]
    tiles_per_seq = seq // TOKEN_TILE
    na = TOKEN_TILE // SWA_BLOCK
    nb = TOKEN_TILE // MOBA_BLOCK

    def feat_spec(width, blk, per_tile):
        return pl.BlockSpec((1, per_tile, width, blk),
                            lambda i: (i // tiles_per_seq, i % tiles_per_seq, 0, 0))

    out_shape = (
        jax.ShapeDtypeStruct((tokens, SWA_KV_W), BF16),
        jax.ShapeDtypeStruct((tokens, MOBA_W), BF16),
        jax.ShapeDtypeStruct((tokens, D_MODEL), BF16),
        jax.ShapeDtypeStruct((tokens, D_MODEL), BF16),
        jax.ShapeDtypeStruct((batch, seq // SWA_BLOCK, SWA_Q_W, SWA_BLOCK), BF16),
        jax.ShapeDtypeStruct((batch, seq // SWA_BLOCK, SWA_KV_W, SWA_BLOCK), BF16),
        jax.ShapeDtypeStruct((batch, seq // MOBA_BLOCK, MOBA_W, MOBA_BLOCK), BF16),
        jax.ShapeDtypeStruct((batch, seq // MOBA_BLOCK, MOBA_W, MOBA_BLOCK), BF16),
        jax.ShapeDtypeStruct((batch, MOBA_HEADS, seq // MOBA_BLOCK, seq), F32),
    )
    tok_spec = lambda w: pl.BlockSpec((TOKEN_TILE, w), lambda i: (i, 0))
    return pl.pallas_call(
        _proj_kernel,
        grid=(steps,),
        in_specs=[
            pl.BlockSpec((TOKEN_TILE, D_MODEL), lambda i: (i, 0)),
            pl.BlockSpec((1, 1, 1, D_MODEL), lambda i: (i // tiles_per_seq, 0, 0, 0)),
            pl.BlockSpec((1, 1, 1, D_MODEL), lambda i: (i // tiles_per_seq, 1, 0, 0)),
            _resident((1, D_MODEL), lambda i: (0, 0)),
            _resident(w_in.shape, lambda i: (0, 0)),
        ] + cast_specs,
        out_specs=(tok_spec(SWA_KV_W), tok_spec(MOBA_W), tok_spec(D_MODEL), tok_spec(D_MODEL),
                   feat_spec(SWA_Q_W, SWA_BLOCK, na), feat_spec(SWA_KV_W, SWA_BLOCK, na),
                   feat_spec(MOBA_W, MOBA_BLOCK, nb), feat_spec(MOBA_W, MOBA_BLOCK, nb),
                   pl.BlockSpec((1, MOBA_HEADS, seq // MOBA_BLOCK, TOKEN_TILE),
                                lambda i: (i // tiles_per_seq, 0, 0, i % tiles_per_seq)))
        + tuple(cast_specs),
        out_shape=out_shape + tuple(jax.ShapeDtypeStruct(w.shape, BF16) for w in cast_weights),
        scratch_shapes=[pltpu.VMEM((D_MODEL, TOK_W), BF16), pltpu.VMEM((FEAT_W, D_MODEL), BF16),
                        pltpu.VMEM((seq // MOBA_BLOCK, MOBA_W), F32)],
        compiler_params=pltpu.CompilerParams(dimension_semantics=("arbitrary",),
                                             vmem_limit_bytes=VMEM_LIMIT_BYTES),
        name="proj",
    )(x2, mod4, mod4, g1, w_in, *cast_weights)


def _swa_kernel(q_ref, k_ref, v_ref, rows_ref, sink_ref, o_ref, bias_ref, ot_ref, s_ref):
    L = SWA_BLOCK
    nblk = q_ref.shape[1]

    @pl.when(pl.program_id(0) == 0)
    def _():
        k_idx = lax.broadcasted_iota(jnp.int32, (L, L), 0)
        q_idx = lax.broadcasted_iota(jnp.int32, (L, L), 1)
        for h in range(SWA_Q_HEADS):
            g, gi = divmod(h, SWA_GROUP)
            t0 = rows_ref[0, h:h + 1, MOBA_BLOCK:MOBA_BLOCK + L]
            r = _toeplitz(jnp.concatenate([t0, t0], axis=1), L)
            cols = slice(gi * L, (gi + 1) * L)
            bias_ref[g, 0:L, cols] = jnp.where(q_idx < k_idx, r, NEG_INF)
            bias_ref[g, L:2 * L, cols] = jnp.where(q_idx >= k_idx, r, NEG_INF)

    ones_blk = jnp.concatenate([jnp.ones((1, 2 * L), BF16), jnp.zeros((ACC_ROWS - HEAD_DIM - 1, 2 * L), BF16)],
                               axis=0)
    PW = 2 * L
    chains = [(g, hp) for g in range(SWA_KV_HEADS) for hp in range(SWA_GROUP // 2)]
    kv_cols = [slice(g * HEAD_DIM, (g + 1) * HEAD_DIM) for g in range(SWA_KV_HEADS)]

    def q_pair(c, g, hp):
        h0 = g * SWA_GROUP + 2 * hp
        return jnp.concatenate([q_ref[0, c, (h0 + e) * HEAD_DIM:(h0 + e + 1) * HEAD_DIM, :] for e in range(2)],
                               axis=1)

    def scores_first(idx):
        g, hp = chains[idx]
        lanes = slice(hp * PW, (hp + 1) * PW)
        s_ref[idx, 0:L, :] = jnp.full((L, PW), NEG_INF, F32)
        s_ref[idx, L:2 * L, :] = _dot(k_ref[0, 0:L, kv_cols[g]], q_pair(0, g, hp)) + bias_ref[g, L:2 * L, lanes]

    def scores(c, idx):
        g, hp = chains[idx]
        kw = k_ref[0, pl.ds(pl.multiple_of((c - 1) * L, L), 2 * L), kv_cols[g]]
        s_ref[idx] = _dot(kw, q_pair(c, g, hp)) + bias_ref[g, :, hp * PW:(hp + 1) * PW]

    def attend(c, c_prev, idx):
        g, hp = chains[idx]
        hs = kv_cols[g]
        s = s_ref[idx]
        sink = sink_ref[g, :, hp * PW:(hp + 1) * PW]
        m = jnp.maximum(jnp.max(s, axis=0, keepdims=True), sink)
        p = jnp.exp2(s - m).astype(BF16)
        v_win = jnp.concatenate([v_ref[0, c_prev, hs, :], v_ref[0, c, hs, :]], axis=1)
        pv = _dot(jnp.concatenate([v_win, ones_blk], axis=0), p)
        o = pv[0:HEAD_DIM] / (pv[HEAD_DIM:HEAD_DIM + 1] + jnp.exp2(sink - m))
        for e in range(2):
            r = (g * SWA_GROUP + 2 * hp + e) * HEAD_DIM
            ot_ref[r:r + HEAD_DIM, :] = o[:, e * L:(e + 1) * L]

    def emit(c):
        o_ref[0, pl.ds(pl.multiple_of(c * L, L), L), :] = ot_ref[...].T.astype(o_ref.dtype)

    for idx in range(len(chains)):
        scores_first(idx)
    for idx in range(len(chains)):
        attend(0, 0, idx)
        scores(1, idx)
    emit(0)

    def body(c, carry):
        for idx in range(len(chains)):
            attend(c, c - 1, idx)
            scores(c + 1, idx)
        emit(c)
        return carry

    def body_unrolled(u, carry):
        for e in range(SWA_UNROLL):
            carry = body(SWA_UNROLL * u + 1 + e, carry)
        return carry

    lax.fori_loop(0, (nblk - 2) // SWA_UNROLL, body_unrolled, 0)
    for idx in range(len(chains)):
        attend(nblk - 1, nblk - 2, idx)
    emit(nblk - 1)


def _swa(qa_t, ka3, va_t, rows, sink_rows):
    batch, nblk = qa_t.shape[0], qa_t.shape[1]
    seq = nblk * SWA_BLOCK
    assert nblk >= 2 and (nblk - 2) % SWA_UNROLL == 0
    return pl.pallas_call(
        _swa_kernel,
        grid=(batch,),
        in_specs=[
            pl.BlockSpec((1, nblk, SWA_Q_W, SWA_BLOCK), lambda b: (b, 0, 0, 0)),
            pl.BlockSpec((1, seq, SWA_KV_W), lambda b: (b, 0, 0)),
            pl.BlockSpec((1, nblk, SWA_KV_W, SWA_BLOCK), lambda b: (b, 0, 0, 0)),
            _resident((1, N_ATTN_HEADS, 2 * MOBA_BLOCK), lambda b: (0, 0, 0)),
            _resident(sink_rows.shape, lambda b: (0, 0, 0)),
        ],
        out_specs=pl.BlockSpec((1, seq, SWA_Q_W), lambda b: (b, 0, 0)),
        out_shape=jax.ShapeDtypeStruct((batch, seq, SWA_Q_W), BF16),
        scratch_shapes=[pltpu.VMEM((SWA_KV_HEADS, 2 * SWA_BLOCK, SWA_GROUP * SWA_BLOCK), F32),
                        pltpu.VMEM((SWA_Q_W, SWA_BLOCK), F32),
                        pltpu.VMEM((SWA_Q_HEADS // 2, 2 * SWA_BLOCK, 2 * SWA_BLOCK), F32)],
        compiler_params=pltpu.CompilerParams(dimension_semantics=("arbitrary",),
                                             vmem_limit_bytes=VMEM_LIMIT_BYTES),
        name="swa",
    )(qa_t, ka3, va_t, rows, sink_rows)


def _moba_kernel(q_ref, qn_ref, k_ref, v_ref, sel_ref, rows_ref, o_ref, bias_ref, m_ref, acc_ref, s_ref, cm_ref):
    i = pl.program_id(1)
    nblk = pl.num_programs(1)
    MB = MOBA_BLOCK

    @pl.when(jnp.logical_and(pl.program_id(0) == 0, i == 0))
    def _():
        for h in range(MOBA_HEADS):
            for d in range(BIAS_SLOTS):
                bias_ref[h, d] = _toeplitz(rows_ref[d, SWA_Q_HEADS + h:SWA_Q_HEADS + h + 1, :], MB)

    m_ref[...] = jnp.full(m_ref.shape, NEG_INF, F32)
    acc_ref[...] = jnp.zeros(acc_ref.shape, F32)
    heads = [slice(h * HEAD_DIM, (h + 1) * HEAD_DIM) for h in range(MOBA_HEADS)]
    ones_blk = jnp.concatenate([jnp.ones((1, MB), BF16), jnp.zeros((ACC_ROWS - HEAD_DIM - 1, MB), BF16)],
                               axis=0)

    def scores(j, slot, h, hs):
        rows = pl.ds(pl.multiple_of(j * MB, MB), MB)
        s = _dot(k_ref[0, rows, hs], q_ref[0, 0, hs, :]) + bias_ref[h, slot]
        s_ref[h] = s
        cm_ref[h] = jnp.max(s, axis=0, keepdims=True) + sel_ref[0, h, pl.ds(j, 1), :]

    def own_scores(blk, qr, h, hs):
        rows = pl.ds(pl.multiple_of(blk * MB, MB), MB)
        s = _dot(k_ref[0, rows, hs], qr[0, 0, hs, :]) + bias_ref[h, 0]
        s_ref[h] = s
        cm_ref[h] = jnp.max(s, axis=0, keepdims=True)

    def accumulate(j, h, hs):
        m_old = m_ref[h]
        m_new = jnp.maximum(m_old, cm_ref[h])
        alpha = jnp.exp2(m_old - m_new)
        p = jnp.exp2(s_ref[h] - (m_new - sel_ref[0, h, pl.ds(j, 1), :])).astype(BF16)
        v_aug = jnp.concatenate([v_ref[0, j, hs, :], ones_blk], axis=0)
        rs = slice(h * ACC_ROWS, (h + 1) * ACC_ROWS)
        acc_ref[rs, :] = alpha * acc_ref[rs, :] + _dot(v_aug, p)
        m_ref[h] = m_new

    @pl.when(i == 0)
    def _():
        for h, hs in enumerate(heads):
            own_scores(i, q_ref, h, hs)

    def body(t, carry):
        j = i - t
        slot = jnp.minimum(t + 1, BIAS_SLOTS - 1)
        for h, hs in enumerate(heads):
            accumulate(j, h, hs)
            scores(j - 1, slot, h, hs)
        return carry

    def body2(u, carry):
        return body(2 * u + 1, body(2 * u, carry))

    lax.fori_loop(0, i // 2, body2, 0)

    @pl.when(i % 2 == 1)
    def _():
        body(i - 1, 0)

    @pl.when(i < nblk - 1)
    def _():
        for h, hs in enumerate(heads):
            accumulate(0, h, hs)
            own_scores(i + 1, qn_ref, h, hs)

    @pl.when(i == nblk - 1)
    def _():
        for h, hs in enumerate(heads):
            accumulate(0, h, hs)

    outs = []
    for h, hs in enumerate(heads):
        base = h * ACC_ROWS
        outs.append(acc_ref[base:base + HEAD_DIM, :] / acc_ref[base + HEAD_DIM:base + HEAD_DIM + 1, :])
    o_ref[0] = jnp.concatenate(outs, axis=0).T.astype(o_ref.dtype)


def _moba(qb_t, kb3, vb_t, sel, rows):
    batch, nblk = qb_t.shape[0], qb_t.shape[1]
    seq = nblk * MOBA_BLOCK
    return pl.pallas_call(
        _moba_kernel,
        grid=(batch, nblk),
        in_specs=[
            pl.BlockSpec((1, 1, MOBA_W, MOBA_BLOCK), lambda b, i: (b, i, 0, 0)),
            pl.BlockSpec((1, 1, MOBA_W, MOBA_BLOCK), lambda b, i: (b, jnp.minimum(i + 1, nblk - 1), 0, 0)),
            pl.BlockSpec((1, seq, MOBA_W), lambda b, i: (b, 0, 0)),
            pl.BlockSpec((1, nblk, MOBA_W, MOBA_BLOCK), lambda b, i: (b, 0, 0, 0)),
            pl.BlockSpec((1, MOBA_HEADS, nblk, MOBA_BLOCK), lambda b, i: (b, 0, 0, i)),
            _resident(rows.shape, lambda b, i: (0, 0, 0)),
        ],
        out_specs=pl.BlockSpec((1, MOBA_BLOCK, MOBA_W), lambda b, i: (b, i, 0)),
        out_shape=jax.ShapeDtypeStruct((batch, seq, MOBA_W), BF16),
        scratch_shapes=[pltpu.VMEM((MOBA_HEADS, BIAS_SLOTS, MOBA_BLOCK, MOBA_BLOCK), F32),
                        pltpu.VMEM((MOBA_HEADS, 1, MOBA_BLOCK), F32),
                        pltpu.VMEM((MOBA_HEADS * ACC_ROWS, MOBA_BLOCK), F32),
                        pltpu.VMEM((MOBA_HEADS, MOBA_BLOCK, MOBA_BLOCK), F32),
                        pltpu.VMEM((MOBA_HEADS, 1, MOBA_BLOCK), F32)],
        compiler_params=pltpu.CompilerParams(dimension_semantics=("arbitrary", "arbitrary"),
                                             vmem_limit_bytes=VMEM_LIMIT_BYTES),
        name="moba",
    )(qb_t, qb_t, kb3, vb_t, sel, rows)


def _post_kernel(x_ref, ya_ref, yb_ref, ga_ref, gb_ref, gate1_ref, shift2_ref, scale2_ref,
                 gate2_ref, g2_ref, gf_ref, wa_ref, wb_ref, wo_ref, w1_ref, w2_ref, o_ref):
    a = _dot(ya_ref[...], wa_ref[...])
    b = _dot(yb_ref[...], wb_ref[...])
    merged = (jax.nn.sigmoid(ga_ref[...].astype(F32)) * a
              + jax.nn.sigmoid(gb_ref[...].astype(F32)) * b)
    x1 = x_ref[...] + gate1_ref[0, 0] * _dot(merged.astype(BF16), wo_ref[...])
    h2 = _rms_modulate(x1, g2_ref[...], shift2_ref[0, 0], scale2_ref[0, 0]).astype(BF16)
    y = jnp.zeros_like(x1)
    for c in range(D_FF // FF_CHUNK):
        cs = slice(c * FF_CHUNK, (c + 1) * FF_CHUNK)
        u = jnp.square(jnp.maximum(_dot(h2, w1_ref[:, cs]), 0.0)).astype(BF16)
        y = y + _dot(u, w2_ref[cs, :])
    x2 = x1 + gate2_ref[0, 0] * y
    ms = jnp.mean(x2 * x2, axis=-1, keepdims=True)
    o_ref[...] = (x2 * lax.rsqrt(ms + RMS_EPS)) * gf_ref[...]


def _post(x2, ya, yb, ga, gb, mod4, g2, gf, wa, wb, wo, w1, w2, seq):
    tokens = x2.shape[0]
    tiles_per_seq = seq // TOKEN_TILE
    tok = lambda w: pl.BlockSpec((TOKEN_TILE, w), lambda i: (i, 0))
    modrow = lambda k: pl.BlockSpec((1, 1, 1, D_MODEL), lambda i: (i // tiles_per_seq, k, 0, 0))
    full = lambda a: _resident(a.shape, lambda i: (0, 0))
    return pl.pallas_call(
        _post_kernel,
        grid=(tokens // TOKEN_TILE,),
        in_specs=[tok(D_MODEL), tok(SWA_Q_W), tok(MOBA_W), tok(D_MODEL), tok(D_MODEL),
                  modrow(2), modrow(3), modrow(4), modrow(5),
                  full(g2), full(gf), full(wa), full(wb), full(wo), full(w1), full(w2)],
        out_specs=tok(D_MODEL),
        out_shape=jax.ShapeDtypeStruct((tokens, D_MODEL), F32),
        compiler_params=pltpu.CompilerParams(dimension_semantics=("arbitrary",),
                                             vmem_limit_bytes=VMEM_LIMIT_BYTES),
        name="post",
    )(x2, ya, yb, ga, gb, mod4, mod4, mod4, mod4, g2, gf, wa, wb, wo, w1, w2)


def kernel(x, c, ada_w, ada_b, norm1_g, norm2_g, w_in, attn_sinks, rel_bias, w_branch_a,
           w_branch_b, w_out, w_mlp_in, w_mlp_out, final_g):
    batch, seq, _ = x.shape
    depth = ada_w.shape[0]
    assert seq % TOKEN_TILE == 0 and TOKEN_TILE % MOBA_BLOCK == 0 and batch <= SUBLANES
    far = np.arange((BIAS_SLOTS - 1) * MOBA_BLOCK - (MOBA_BLOCK - 1), max(seq, BIAS_SLOTS * MOBA_BLOCK))
    assert np.all(_t5_bucket(far) == _t5_bucket(far[:1]))
    assert 2 * SWA_BLOCK <= MOBA_BLOCK and SWA_WINDOW == SWA_BLOCK
    assert depth == 1
    l = 0

    rows = _bias_rows(rel_bias)
    c_t = jnp.zeros((D_MODEL, SUBLANES), F32).at[:, :batch].set(c.astype(F32).T)

    x2 = x.reshape(batch * seq, D_MODEL)
    mod = _ada(c_t, ada_w[l], ada_b[l][None, :], batch)
    mod4 = mod[:batch].reshape(batch, N_MOD, 1, D_MODEL)
    later_weights = [w.astype(F32) for w in (w_branch_a[l], w_branch_b[l], w_out[l], w_mlp_in[l], w_mlp_out[l])]
    ka, kb, ga, gb, qa_t, va_t, qb_t, vb_t, sel, wa, wb, wo, w1, w2 = _proj(
        x2, mod4, norm1_g[l][None, :], w_in[l].astype(F32), later_weights, batch, seq)

    sink_rows = jnp.broadcast_to(
        (attn_sinks[l].astype(F32) * LOG2E).reshape(SWA_KV_HEADS, 1, SWA_GROUP, 1),
        (SWA_KV_HEADS, 1, SWA_GROUP, SWA_BLOCK)).reshape(SWA_KV_HEADS, 1, SWA_GROUP * SWA_BLOCK)
    ya = _swa(qa_t, ka.reshape(batch, seq, SWA_KV_W), va_t, rows, sink_rows)

    yb = _moba(qb_t, kb.reshape(batch, seq, MOBA_W), vb_t, sel, rows)

    out = _post(x2, ya.reshape(batch * seq, SWA_Q_W), yb.reshape(batch * seq, MOBA_W), ga, gb,
                mod4, norm2_g[l][None, :], final_g[None, :], wa, wb, wo, w1, w2, seq)
    return out.reshape(batch, seq, D_MODEL)
```

```python
import functools
import math

import jax
import jax.numpy as jnp
import numpy as np
from jax import lax
from jax.experimental import pallas as pl
from jax.experimental.pallas import tpu as pltpu

D_MODEL = 1024
HEAD_DIM = 64
ATTN_SCALE = HEAD_DIM ** -0.5
LOG2E = math.log2(math.e)
SWA_Q_HEADS = 8
SWA_KV_HEADS = 2
SWA_GROUP = SWA_Q_HEADS // SWA_KV_HEADS
SWA_WINDOW = 128
SWA_BLOCK = 128
MOBA_HEADS = 8
MOBA_BLOCK = 256
MOBA_TOPK = 3
NUM_BUCKETS = 32
MAX_EXACT = NUM_BUCKETS // 2
MAX_DISTANCE = 2048
N_ATTN_HEADS = SWA_Q_HEADS + MOBA_HEADS
SWA_Q_W = SWA_Q_HEADS * HEAD_DIM
SWA_KV_W = SWA_KV_HEADS * HEAD_DIM
MOBA_W = MOBA_HEADS * HEAD_DIM
D_FF = 4 * D_MODEL
N_MOD = 6
RMS_EPS = 1e-6

VMEM_LIMIT_BYTES = 56 * 1024 * 1024
TOKEN_TILE = 512
FF_CHUNK = 1024
SUBLANES = 8
ADA_COLS = 1024
BIAS_SLOTS = 8
SWA_UNROLL = 3
ACC_ROWS = HEAD_DIM + 16
NEG_INF = float("-inf")

F32 = jnp.float32
BF16 = jnp.bfloat16


def _resident(block_shape, index_map):
    return pl.BlockSpec(block_shape, index_map, pipeline_mode=pl.Buffered(1))


def _dot(a, b):
    return jnp.dot(a, b, preferred_element_type=F32)


def _dot_nt(a, b):
    return lax.dot_general(a, b, (((1,), (1,)), ((), ())), preferred_element_type=F32)


def _rms_modulate(xv, g, shift, scale):
    ms = jnp.mean(xv * xv, axis=-1, keepdims=True)
    y = xv * lax.rsqrt(ms + RMS_EPS)
    return (y * g) * (1.0 + scale) + shift


def _toeplitz(u, rows):
    wide = jnp.broadcast_to(u, (rows, 2 * rows))
    return pltpu.roll(wide, 0, 1, stride=1, stride_axis=0)[:, rows:]


def _ada_sweep(ct, w, b, batch):
    cs_t = jax.nn.silu(ct)
    rows = [jnp.sum(w * cs_t[:, r:r + 1], axis=0, keepdims=True) for r in range(batch)]
    rows.append(jnp.zeros((SUBLANES - batch, w.shape[1]), F32))
    return jnp.concatenate(rows, axis=0) + b


def _ada_kernel(ct_ref, w_ref, b_ref, o_ref, *, batch):
    o_ref[...] = _ada_sweep(ct_ref[...], w_ref[...], b_ref[...], batch)


def _ada(c_t, w, b, batch, n):
    bn = ADA_COLS
    assert n % bn == 0
    return pl.pallas_call(
        functools.partial(_ada_kernel, batch=batch),
        grid=(n // bn,),
        in_specs=[pl.BlockSpec((D_MODEL, SUBLANES), lambda j: (0, 0)),
                  pl.BlockSpec((D_MODEL, bn), lambda j: (0, j)),
                  pl.BlockSpec((1, bn), lambda j: (0, j))],
        out_specs=pl.BlockSpec((SUBLANES, bn), lambda j: (0, j)),
        out_shape=jax.ShapeDtypeStruct((SUBLANES, n), F32),
        compiler_params=pltpu.CompilerParams(dimension_semantics=("arbitrary",),
                                             vmem_limit_bytes=VMEM_LIMIT_BYTES),
        name="ada",
    )(c_t, w, b)


def _t5_bucket(dist):
    n = np.maximum(dist, 0)
    nf = np.maximum(n, 1).astype(np.float32)
    large = MAX_EXACT + (np.log(nf / np.float32(MAX_EXACT)) / np.float32(math.log(MAX_DISTANCE / MAX_EXACT))
                         * np.float32(NUM_BUCKETS - MAX_EXACT)).astype(np.int32)
    large = np.minimum(large, NUM_BUCKETS - 1)
    return np.where(n < MAX_EXACT, n, large).astype(np.int32)


def _bias_rows_kernel(bucket_ref, rbt_ref, o_ref):
    bucket = bucket_ref[...]
    n = bucket.shape[1]
    onehot = jnp.where(lax.broadcasted_iota(jnp.int32, (NUM_BUCKETS, n), 0) == bucket, 1.0, 0.0)
    rows = jnp.dot(rbt_ref[...], onehot, preferred_element_type=F32,
                   precision=lax.Precision.HIGHEST)
    rows = jnp.where(bucket >= 0, rows * LOG2E, NEG_INF)
    for d in range(BIAS_SLOTS):
        o_ref[d] = rows[:, d * MOBA_BLOCK:(d + 2) * MOBA_BLOCK]


def _bias_rows(rel_bias):
    n = (BIAS_SLOTS + 1) * MOBA_BLOCK
    dist = np.arange(n, dtype=np.int32) - MOBA_BLOCK
    bucket = jnp.asarray(np.where(dist >= 0, _t5_bucket(dist), -1)[None, :], dtype=jnp.int32)
    return pl.pallas_call(
        _bias_rows_kernel,
        out_shape=jax.ShapeDtypeStruct((BIAS_SLOTS, N_ATTN_HEADS, 2 * MOBA_BLOCK), F32),
        name="bias_rows",
    )(bucket, rel_bias.astype(F32).T)


IN_WIDTHS = (SWA_Q_W, SWA_KV_W, SWA_KV_W, MOBA_W, MOBA_W, MOBA_W, D_MODEL, D_MODEL)
IN_OFFS = tuple(sum(IN_WIDTHS[:k]) for k in range(len(IN_WIDTHS) + 1))
FEAT_COLS = (0, 2, 3, 5)
FEAT_W = sum(IN_WIDTHS[k] for k in FEAT_COLS)
TOK_COLS = (1, 4, 6, 7)
TOK_W = sum(IN_WIDTHS[k] for k in TOK_COLS)
Q_COLS = (0, 3)
BLOCKS_PER_TILE = TOKEN_TILE // MOBA_BLOCK


def _proj_kernel(x_ref, shift_ref, scale_ref, g_ref, w_ref, ct_ref, adaw_ref, adab_ref, *refs, batch):
    n_cast = (len(refs) - 14) // 2
    cast_in, refs = refs[:n_cast], refs[n_cast:]
    ka_ref, kb_ref, ga_ref, gb_ref, qa_ref, va_ref, qb_ref, vb_ref, sel_ref, modb_ref = refs[:10]
    cast_out, (wk_ref, wt_ref, kmean_ref, csb_ref) = refs[10:10 + n_cast], refs[10 + n_cast:]

    tile = pl.program_id(0) % (kmean_ref.shape[0] // BLOCKS_PER_TILE)

    @pl.when(tile == 0)
    def _():
        kmean_ref[...] = jnp.zeros(kmean_ref.shape, F32)

    @pl.when(pl.program_id(0) == 0)
    def _():
        cs_t = jax.nn.silu(ct_ref[...])
        for r in range(batch):
            csb_ref[r] = jnp.broadcast_to(cs_t[:, r:r + 1], csb_ref.shape[1:])
        o = 0
        for k in TOK_COLS:
            wk_ref[:, o:o + IN_WIDTHS[k]] = w_ref[:, IN_OFFS[k]:IN_OFFS[k + 1]].astype(BF16)
            o += IN_WIDTHS[k]
        o = 0
        for k in FEAT_COLS:
            wcol = w_ref[:, IN_OFFS[k]:IN_OFFS[k + 1]]
            if k in Q_COLS:
                wcol = wcol * (ATTN_SCALE * LOG2E)
            wt_ref[o:o + IN_WIDTHS[k], :] = wcol.T.astype(BF16)
            o += IN_WIDTHS[k]

    h = _rms_modulate(x_ref[...], g_ref[...], shift_ref[0, 0], scale_ref[0, 0]).astype(BF16)
    tok_refs = dict(zip(TOK_COLS, (ka_ref, kb_ref, ga_ref, gb_ref)))
    feat_refs = dict(zip(FEAT_COLS, (qa_ref, va_ref, qb_ref, vb_ref)))

    def token_major(k):
        o = sum(IN_WIDTHS[c] for c in TOK_COLS[:TOK_COLS.index(k)])
        out = _dot(h, wk_ref[:, o:o + IN_WIDTHS[k]])
        tok_refs[k][...] = out.astype(BF16)
        return out

    def feature_major(k):
        o = sum(IN_WIDTHS[c] for c in FEAT_COLS[:FEAT_COLS.index(k)])
        out = _dot_nt(wt_ref[o:o + IN_WIDTHS[k], :], h).astype(BF16)
        blk = feat_refs[k].shape[-1]
        for t in range(TOKEN_TILE // blk):
            feat_refs[k][0, t] = out[:, t * blk:(t + 1) * blk]
        return out

    _update_kmean(token_major(4), kmean_ref, tile)
    qb_t = feature_major(3)
    token_major(1)
    token_major(6)
    _moba_select(qb_t, sel_ref, kmean_ref, tile)
    token_major(7)
    for k in (0, 2, 5):
        feature_major(k)

    for src, dst in zip(cast_in, cast_out):
        dst[...] = src[...].astype(dst.dtype)
    rows = [jnp.sum(adaw_ref[...] * csb_ref[r], axis=0, keepdims=True) for r in range(batch)]
    rows.append(jnp.zeros((SUBLANES - batch, adaw_ref.shape[1]), F32))
    modb_ref[...] = jnp.concatenate(rows, axis=0) + adab_ref[...]


def _update_kmean(kb, kmean_ref, tile):
    for t in range(BLOCKS_PER_TILE):
        kmean_ref[pl.ds(tile * BLOCKS_PER_TILE + t, 1), :] = jnp.mean(
            kb[t * MOBA_BLOCK:(t + 1) * MOBA_BLOCK, :], axis=0, keepdims=True)


def _moba_select(qb_t, sel_ref, kmean_ref, tile):
    nblk = kmean_ref.shape[0]
    blocks_per_tile = BLOCKS_PER_TILE
    blk = lax.broadcasted_iota(jnp.int32, (nblk, TOKEN_TILE), 0)
    qblk = tile * blocks_per_tile + lax.broadcasted_iota(jnp.int32, (nblk, TOKEN_TILE), 1) // MOBA_BLOCK
    past = blk < qblk
    own = jnp.where(blk == qblk, 0.0, NEG_INF)
    blk_f = blk.astype(F32)
    for h in range(MOBA_HEADS):
        hs = slice(h * HEAD_DIM, (h + 1) * HEAD_DIM)
        kmean = kmean_ref[:, hs]
        k1 = kmean.astype(BF16)
        r1 = kmean - k1.astype(F32)
        k2 = r1.astype(BF16)
        k3 = (r1 - k2.astype(F32)).astype(BF16)
        parts = _dot(jnp.concatenate([k1, k2, k3], axis=0), qb_t[hs, :])
        gate = parts[0:nblk] + parts[nblk:2 * nblk] + parts[2 * nblk:3 * nblk]
        gate = jnp.where(past, gate, NEG_INF)
        mask = own
        for _ in range(MOBA_TOPK):
            top = jnp.max(gate, axis=0, keepdims=True)
            first = jnp.min(jnp.where(gate == top, blk_f, float(nblk)), axis=0, keepdims=True)
            pick = blk_f == first
            mask = jnp.where(jnp.logical_and(pick, past), 0.0, mask)
            gate = jnp.where(pick, NEG_INF, gate)
        sel_ref[0, h] = mask


def _proj(x2, mod4, g1, w_in, c_t, ada_w, ada_b, n_early, cast_weights, batch, seq):
    tokens = x2.shape[0]
    steps = tokens // TOKEN_TILE
    n_late = ada_w.shape[1] - n_early
    slab = n_late // steps
    assert n_late % steps == 0 and slab % 128 == 0 and n_early % slab == 0
    assert all(w.shape[0] % (16 * steps) == 0 for w in cast_weights)
    cast_specs = [pl.BlockSpec((w.shape[0] // steps, w.shape[1]), lambda i: (i, 0)) for w in cast_weights]
    tiles_per_seq = seq // TOKEN_TILE
    na = TOKEN_TILE // SWA_BLOCK
    nb = TOKEN_TILE // MOBA_BLOCK

    def feat_spec(width, blk, per_tile):
        return pl.BlockSpec((1, per_tile, width, blk),
                            lambda i: (i // tiles_per_seq, i % tiles_per_seq, 0, 0))

    out_shape = (
        jax.ShapeDtypeStruct((tokens, SWA_KV_W), BF16),
        jax.ShapeDtypeStruct((tokens, MOBA_W), BF16),
        jax.ShapeDtypeStruct((tokens, D_MODEL), BF16),
        jax.ShapeDtypeStruct((tokens, D_MODEL), BF16),
        jax.ShapeDtypeStruct((batch, seq // SWA_BLOCK, SWA_Q_W, SWA_BLOCK), BF16),
        jax.ShapeDtypeStruct((batch, seq // SWA_BLOCK, SWA_KV_W, SWA_BLOCK), BF16),
        jax.ShapeDtypeStruct((batch, seq // MOBA_BLOCK, MOBA_W, MOBA_BLOCK), BF16),
        jax.ShapeDtypeStruct((batch, seq // MOBA_BLOCK, MOBA_W, MOBA_BLOCK), BF16),
        jax.ShapeDtypeStruct((batch, MOBA_HEADS, seq // MOBA_BLOCK, seq), F32),
        jax.ShapeDtypeStruct((SUBLANES, n_late), F32),
    )
    tok_spec = lambda w: pl.BlockSpec((TOKEN_TILE, w), lambda i: (i, 0))
    return pl.pallas_call(
        functools.partial(_proj_kernel, batch=batch),
        grid=(steps,),
        in_specs=[
            pl.BlockSpec((TOKEN_TILE, D_MODEL), lambda i: (i, 0)),
            pl.BlockSpec((1, 1, 1, D_MODEL), lambda i: (i // tiles_per_seq, 0, 0, 0)),
            pl.BlockSpec((1, 1, 1, D_MODEL), lambda i: (i // tiles_per_seq, 1, 0, 0)),
            _resident((1, D_MODEL), lambda i: (0, 0)),
            _resident(w_in.shape, lambda i: (0, 0)),
            _resident(c_t.shape, lambda i: (0, 0)),
            pl.BlockSpec((D_MODEL, slab), lambda i: (0, n_early // slab + i)),
            pl.BlockSpec((1, slab), lambda i: (0, n_early // slab + i)),
        ] + cast_specs,
        out_specs=(tok_spec(SWA_KV_W), tok_spec(MOBA_W), tok_spec(D_MODEL), tok_spec(D_MODEL),
                   feat_spec(SWA_Q_W, SWA_BLOCK, na), feat_spec(SWA_KV_W, SWA_BLOCK, na),
                   feat_spec(MOBA_W, MOBA_BLOCK, nb), feat_spec(MOBA_W, MOBA_BLOCK, nb),
                   pl.BlockSpec((1, MOBA_HEADS, seq // MOBA_BLOCK, TOKEN_TILE),
                                lambda i: (i // tiles_per_seq, 0, 0, i % tiles_per_seq)),
                   pl.BlockSpec((SUBLANES, slab), lambda i: (0, i)))
        + tuple(cast_specs),
        out_shape=out_shape + tuple(jax.ShapeDtypeStruct(w.shape, BF16) for w in cast_weights),
        scratch_shapes=[pltpu.VMEM((D_MODEL, TOK_W), BF16), pltpu.VMEM((FEAT_W, D_MODEL), BF16),
                        pltpu.VMEM((seq // MOBA_BLOCK, MOBA_W), F32),
                        pltpu.VMEM((batch, D_MODEL, slab), F32)],
        compiler_params=pltpu.CompilerParams(dimension_semantics=("arbitrary",),
                                             vmem_limit_bytes=VMEM_LIMIT_BYTES),
        name="proj",
    )(x2, mod4, mod4, g1, w_in, c_t, ada_w, ada_b, *cast_weights)


def _swa_kernel(q_ref, k_ref, v_ref, rows_ref, sink_ref, o_ref, bias_ref, ot_ref, s_ref):
    L = SWA_BLOCK
    nblk = q_ref.shape[1]

    @pl.when(pl.program_id(0) == 0)
    def _():
        k_idx = lax.broadcasted_iota(jnp.int32, (L, L), 0)
        q_idx = lax.broadcasted_iota(jnp.int32, (L, L), 1)
        for h in range(SWA_Q_HEADS):
            g, gi = divmod(h, SWA_GROUP)
            t0 = rows_ref[0, h:h + 1, MOBA_BLOCK:MOBA_BLOCK + L]
            r = _toeplitz(jnp.concatenate([t0, t0], axis=1), L)
            cols = slice(gi * L, (gi + 1) * L)
            bias_ref[g, 0:L, cols] = jnp.where(q_idx < k_idx, r, NEG_INF)
            bias_ref[g, L:2 * L, cols] = jnp.where(q_idx >= k_idx, r, NEG_INF)

    ones_blk = jnp.concatenate([jnp.ones((1, 2 * L), BF16), jnp.zeros((ACC_ROWS - HEAD_DIM - 1, 2 * L), BF16)],
                               axis=0)
    PW = 2 * L
    chains = [(g, hp) for g in range(SWA_KV_HEADS) for hp in range(SWA_GROUP // 2)]
    kv_cols = [slice(g * HEAD_DIM, (g + 1) * HEAD_DIM) for g in range(SWA_KV_HEADS)]

    def q_pair(c, g, hp):
        h0 = g * SWA_GROUP + 2 * hp
        return jnp.concatenate([q_ref[0, c, (h0 + e) * HEAD_DIM:(h0 + e + 1) * HEAD_DIM, :] for e in range(2)],
                               axis=1)

    def scores_first(idx):
        g, hp = chains[idx]
        lanes = slice(hp * PW, (hp + 1) * PW)
        s_ref[idx, 0:L, :] = jnp.full((L, PW), NEG_INF, F32)
        s_ref[idx, L:2 * L, :] = _dot(k_ref[0, 0:L, kv_cols[g]], q_pair(0, g, hp)) + bias_ref[g, L:2 * L, lanes]

    def scores(c, idx):
        g, hp = chains[idx]
        kw = k_ref[0, pl.ds(pl.multiple_of((c - 1) * L, L), 2 * L), kv_cols[g]]
        s_ref[idx] = _dot(kw, q_pair(c, g, hp)) + bias_ref[g, :, hp * PW:(hp + 1) * PW]

    def attend(c, c_prev, idx):
        g, hp = chains[idx]
        hs = kv_cols[g]
        s = s_ref[idx]
        sink = sink_ref[g, :, hp * PW:(hp + 1) * PW]
        m = jnp.maximum(jnp.max(s, axis=0, keepdims=True), sink)
        p = jnp.exp2(s - m).astype(BF16)
        v_win = jnp.concatenate([v_ref[0, c_prev, hs, :], v_ref[0, c, hs, :]], axis=1)
        pv = _dot(jnp.concatenate([v_win, ones_blk], axis=0), p)
        o = pv[0:HEAD_DIM] / (pv[HEAD_DIM:HEAD_DIM + 1] + jnp.exp2(sink - m))
        for e in range(2):
            r = (g * SWA_GROUP + 2 * hp + e) * HEAD_DIM
            ot_ref[r:r + HEAD_DIM, :] = o[:, e * L:(e + 1) * L]

    def emit(c):
        o_ref[0, pl.ds(pl.multiple_of(c * L, L), L), :] = ot_ref[...].T.astype(o_ref.dtype)

    for idx in range(len(chains)):
        scores_first(idx)
    for idx in range(len(chains)):
        attend(0, 0, idx)
        scores(1, idx)
    emit(0)

    def body(c, carry):
        for idx in range(len(chains)):
            attend(c, c - 1, idx)
            scores(c + 1, idx)
        emit(c)
        return carry

    def body_unrolled(u, carry):
        for e in range(SWA_UNROLL):
            carry = body(SWA_UNROLL * u + 1 + e, carry)
        return carry

    lax.fori_loop(0, (nblk - 2) // SWA_UNROLL, body_unrolled, 0)
    for idx in range(len(chains)):
        attend(nblk - 1, nblk - 2, idx)
    emit(nblk - 1)


def _swa(qa_t, ka3, va_t, rows, sink_rows):
    batch, nblk = qa_t.shape[0], qa_t.shape[1]
    seq = nblk * SWA_BLOCK
    assert nblk >= 2 and (nblk - 2) % SWA_UNROLL == 0
    return pl.pallas_call(
        _swa_kernel,
        grid=(batch,),
        in_specs=[
            pl.BlockSpec((1, nblk, SWA_Q_W, SWA_BLOCK), lambda b: (b, 0, 0, 0)),
            pl.BlockSpec((1, seq, SWA_KV_W), lambda b: (b, 0, 0)),
            pl.BlockSpec((1, nblk, SWA_KV_W, SWA_BLOCK), lambda b: (b, 0, 0, 0)),
            _resident((1, N_ATTN_HEADS, 2 * MOBA_BLOCK), lambda b: (0, 0, 0)),
            _resident(sink_rows.shape, lambda b: (0, 0, 0)),
        ],
        out_specs=pl.BlockSpec((1, seq, SWA_Q_W), lambda b: (b, 0, 0)),
        out_shape=jax.ShapeDtypeStruct((batch, seq, SWA_Q_W), BF16),
        scratch_shapes=[pltpu.VMEM((SWA_KV_HEADS, 2 * SWA_BLOCK, SWA_GROUP * SWA_BLOCK), F32),
                        pltpu.VMEM((SWA_Q_W, SWA_BLOCK), F32),
                        pltpu.VMEM((SWA_Q_HEADS // 2, 2 * SWA_BLOCK, 2 * SWA_BLOCK), F32)],
        compiler_params=pltpu.CompilerParams(dimension_semantics=("arbitrary",),
                                             vmem_limit_bytes=VMEM_LIMIT_BYTES),
        name="swa",
    )(qa_t, ka3, va_t, rows, sink_rows)


def _moba_kernel(q_ref, qn_ref, k_ref, v_ref, sel_ref, rows_ref, o_ref, bias_ref, m_ref, acc_ref, s_ref, cm_ref):
    i = pl.program_id(1)
    nblk = pl.num_programs(1)
    MB = MOBA_BLOCK

    @pl.when(jnp.logical_and(pl.program_id(0) == 0, i == 0))
    def _():
        for h in range(MOBA_HEADS):
            for d in range(BIAS_SLOTS):
                bias_ref[h, d] = _toeplitz(rows_ref[d, SWA_Q_HEADS + h:SWA_Q_HEADS + h + 1, :], MB)

    m_ref[...] = jnp.full(m_ref.shape, NEG_INF, F32)
    acc_ref[...] = jnp.zeros(acc_ref.shape, F32)
    heads = [slice(h * HEAD_DIM, (h + 1) * HEAD_DIM) for h in range(MOBA_HEADS)]
    ones_blk = jnp.concatenate([jnp.ones((1, MB), BF16), jnp.zeros((ACC_ROWS - HEAD_DIM - 1, MB), BF16)],
                               axis=0)

    def scores(j, slot, h, hs):
        rows = pl.ds(pl.multiple_of(j * MB, MB), MB)
        s = _dot(k_ref[0, rows, hs], q_ref[0, 0, hs, :]) + bias_ref[h, slot]
        s_ref[h] = s
        cm_ref[h] = jnp.max(s, axis=0, keepdims=True) + sel_ref[0, h, pl.ds(j, 1), :]

    def own_scores(blk, qr, h, hs):
        rows = pl.ds(pl.multiple_of(blk * MB, MB), MB)
        s = _dot(k_ref[0, rows, hs], qr[0, 0, hs, :]) + bias_ref[h, 0]
        s_ref[h] = s
        cm_ref[h] = jnp.max(s, axis=0, keepdims=True)

    def accumulate(j, h, hs):
        m_old = m_ref[h]
        m_new = jnp.maximum(m_old, cm_ref[h])
        alpha = jnp.exp2(m_old - m_new)
        p = jnp.exp2(s_ref[h] - (m_new - sel_ref[0, h, pl.ds(j, 1), :])).astype(BF16)
        v_aug = jnp.concatenate([v_ref[0, j, hs, :], ones_blk], axis=0)
        rs = slice(h * ACC_ROWS, (h + 1) * ACC_ROWS)
        acc_ref[rs, :] = alpha * acc_ref[rs, :] + _dot(v_aug, p)
        m_ref[h] = m_new

    @pl.when(i == 0)
    def _():
        for h, hs in enumerate(heads):
            own_scores(i, q_ref, h, hs)

    def body(t, carry):
        j = i - t
        slot = jnp.minimum(t + 1, BIAS_SLOTS - 1)
        for h, hs in enumerate(heads):
            accumulate(j, h, hs)
            scores(j - 1, slot, h, hs)
        return carry

    def body2(u, carry):
        return body(2 * u + 1, body(2 * u, carry))

    lax.fori_loop(0, i // 2, body2, 0)

    @pl.when(i % 2 == 1)
    def _():
        body(i - 1, 0)

    @pl.when(i < nblk - 1)
    def _():
        for h, hs in enumerate(heads):
            accumulate(0, h, hs)
            own_scores(i + 1, qn_ref, h, hs)

    @pl.when(i == nblk - 1)
    def _():
        for h, hs in enumerate(heads):
            accumulate(0, h, hs)

    outs = []
    for h, hs in enumerate(heads):
        base = h * ACC_ROWS
        outs.append(acc_ref[base:base + HEAD_DIM, :] / acc_ref[base + HEAD_DIM:base + HEAD_DIM + 1, :])
    o_ref[0] = jnp.concatenate(outs, axis=0).T.astype(o_ref.dtype)


def _moba(qb_t, kb3, vb_t, sel, rows):
    batch, nblk = qb_t.shape[0], qb_t.shape[1]
    seq = nblk * MOBA_BLOCK
    return pl.pallas_call(
        _moba_kernel,
        grid=(batch, nblk),
        in_specs=[
            pl.BlockSpec((1, 1, MOBA_W, MOBA_BLOCK), lambda b, i: (b, i, 0, 0)),
            pl.BlockSpec((1, 1, MOBA_W, MOBA_BLOCK), lambda b, i: (b, jnp.minimum(i + 1, nblk - 1), 0, 0)),
            pl.BlockSpec((1, seq, MOBA_W), lambda b, i: (b, 0, 0)),
            pl.BlockSpec((1, nblk, MOBA_W, MOBA_BLOCK), lambda b, i: (b, 0, 0, 0)),
            pl.BlockSpec((1, MOBA_HEADS, nblk, MOBA_BLOCK), lambda b, i: (b, 0, 0, i)),
            _resident(rows.shape, lambda b, i: (0, 0, 0)),
        ],
        out_specs=pl.BlockSpec((1, MOBA_BLOCK, MOBA_W), lambda b, i: (b, i, 0)),
        out_shape=jax.ShapeDtypeStruct((batch, seq, MOBA_W), BF16),
        scratch_shapes=[pltpu.VMEM((MOBA_HEADS, BIAS_SLOTS, MOBA_BLOCK, MOBA_BLOCK), F32),
                        pltpu.VMEM((MOBA_HEADS, 1, MOBA_BLOCK), F32),
                        pltpu.VMEM((MOBA_HEADS * ACC_ROWS, MOBA_BLOCK), F32),
                        pltpu.VMEM((MOBA_HEADS, MOBA_BLOCK, MOBA_BLOCK), F32),
                        pltpu.VMEM((MOBA_HEADS, 1, MOBA_BLOCK), F32)],
        compiler_params=pltpu.CompilerParams(dimension_semantics=("arbitrary", "arbitrary"),
                                             vmem_limit_bytes=VMEM_LIMIT_BYTES),
        name="moba",
    )(qb_t, qb_t, kb3, vb_t, sel, rows)


def _post_kernel(x_ref, ya_ref, yb_ref, ga_ref, gb_ref, gate1_ref, shift2_ref, scale2_ref,
                 gate2_ref, g2_ref, gf_ref, wa_ref, wb_ref, wo_ref, w1_ref, w2_ref, o_ref):
    a = _dot(ya_ref[...], wa_ref[...])
    b = _dot(yb_ref[...], wb_ref[...])
    merged = (jax.nn.sigmoid(ga_ref[...].astype(F32)) * a
              + jax.nn.sigmoid(gb_ref[...].astype(F32)) * b)
    x1 = x_ref[...] + gate1_ref[0, 0] * _dot(merged.astype(BF16), wo_ref[...])
    h2 = _rms_modulate(x1, g2_ref[...], shift2_ref[0, 0], scale2_ref[0, 0]).astype(BF16)
    y = jnp.zeros_like(x1)
    for c in range(D_FF // FF_CHUNK):
        cs = slice(c * FF_CHUNK, (c + 1) * FF_CHUNK)
        u = jnp.square(jnp.maximum(_dot(h2, w1_ref[:, cs]), 0.0)).astype(BF16)
        y = y + _dot(u, w2_ref[cs, :])
    x2 = x1 + gate2_ref[0, 0] * y
    ms = jnp.mean(x2 * x2, axis=-1, keepdims=True)
    o_ref[...] = (x2 * lax.rsqrt(ms + RMS_EPS)) * gf_ref[...]


def _post(x2, ya, yb, ga, gb, mod4, g2, gf, wa, wb, wo, w1, w2, seq):
    tokens = x2.shape[0]
    tiles_per_seq = seq // TOKEN_TILE
    tok = lambda w: pl.BlockSpec((TOKEN_TILE, w), lambda i: (i, 0))
    modrow = lambda k: pl.BlockSpec((1, 1, 1, D_MODEL), lambda i: (i // tiles_per_seq, k, 0, 0))
    full = lambda a: _resident(a.shape, lambda i: (0, 0))
    return pl.pallas_call(
        _post_kernel,
        grid=(tokens // TOKEN_TILE,),
        in_specs=[tok(D_MODEL), tok(SWA_Q_W), tok(MOBA_W), tok(D_MODEL), tok(D_MODEL),
                  modrow(0), modrow(1), modrow(2), modrow(3),
                  full(g2), full(gf), full(wa), full(wb), full(wo), full(w1), full(w2)],
        out_specs=tok(D_MODEL),
        out_shape=jax.ShapeDtypeStruct((tokens, D_MODEL), F32),
        compiler_params=pltpu.CompilerParams(dimension_semantics=("arbitrary",),
                                             vmem_limit_bytes=VMEM_LIMIT_BYTES),
        name="post",
    )(x2, ya, yb, ga, gb, mod4, mod4, mod4, mod4, g2, gf, wa, wb, wo, w1, w2)


def kernel(x, c, ada_w, ada_b, norm1_g, norm2_g, w_in, attn_sinks, rel_bias, w_branch_a,
           w_branch_b, w_out, w_mlp_in, w_mlp_out, final_g):
    batch, seq, _ = x.shape
    depth = ada_w.shape[0]
    assert seq % TOKEN_TILE == 0 and TOKEN_TILE % MOBA_BLOCK == 0 and batch <= SUBLANES
    far = np.arange((BIAS_SLOTS - 1) * MOBA_BLOCK - (MOBA_BLOCK - 1), max(seq, BIAS_SLOTS * MOBA_BLOCK))
    assert np.all(_t5_bucket(far) == _t5_bucket(far[:1]))
    assert 2 * SWA_BLOCK <= MOBA_BLOCK and SWA_WINDOW == SWA_BLOCK
    assert depth == 1
    l = 0

    rows = _bias_rows(rel_bias)
    c_t = jnp.zeros((D_MODEL, SUBLANES), F32).at[:, :batch].set(c.astype(F32).T)

    x2 = x.reshape(batch * seq, D_MODEL)
    n_early = 2 * D_MODEL
    mod_early = _ada(c_t, ada_w[l], ada_b[l][None, :], batch, n_early)
    mod4 = mod_early[:batch].reshape(batch, 2, 1, D_MODEL)
    later_weights = [w.astype(F32) for w in (w_branch_a[l], w_branch_b[l], w_out[l], w_mlp_in[l], w_mlp_out[l])]
    ka, kb, ga, gb, qa_t, va_t, qb_t, vb_t, sel, mod_late, wa, wb, wo, w1, w2 = _proj(
        x2, mod4, norm1_g[l][None, :], w_in[l].astype(F32), c_t, ada_w[l].astype(F32), ada_b[l][None, :].astype(F32),
        n_early, later_weights, batch, seq)
    mod4_late = mod_late[:batch].reshape(batch, N_MOD - 2, 1, D_MODEL)

    sink_rows = jnp.broadcast_to(
        (attn_sinks[l].astype(F32) * LOG2E).reshape(SWA_KV_HEADS, 1, SWA_GROUP, 1),
        (SWA_KV_HEADS, 1, SWA_GROUP, SWA_BLOCK)).reshape(SWA_KV_HEADS, 1, SWA_GROUP * SWA_BLOCK)
    ya = _swa(qa_t, ka.reshape(batch, seq, SWA_KV_W), va_t, rows, sink_rows)

    yb = _moba(qb_t, kb.reshape(batch, seq, MOBA_W), vb_t, sel, rows)

    out = _post(x2, ya.reshape(batch * seq, SWA_Q_W), yb.reshape(batch * seq, MOBA_W), ga, gb,
                mod4_late, norm2_g[l][None, :], final_g[None, :], wa, wb, wo, w1, w2, seq)
    return out.reshape(batch, seq, D_MODEL)
```

```python
import functools
import math

import jax
import jax.numpy as jnp
import numpy as np
from jax import lax
from jax.experimental import pallas as pl
from jax.experimental.pallas import tpu as pltpu

D_MODEL = 1024
HEAD_DIM = 64
ATTN_SCALE = HEAD_DIM ** -0.5
LOG2E = math.log2(math.e)
SWA_Q_HEADS = 8
SWA_KV_HEADS = 2
SWA_GROUP = SWA_Q_HEADS // SWA_KV_HEADS
SWA_WINDOW = 128
SWA_BLOCK = 128
MOBA_HEADS = 8
MOBA_BLOCK = 256
MOBA_TOPK = 3
NUM_BUCKETS = 32
MAX_EXACT = NUM_BUCKETS // 2
MAX_DISTANCE = 2048
N_ATTN_HEADS = SWA_Q_HEADS + MOBA_HEADS
SWA_Q_W = SWA_Q_HEADS * HEAD_DIM
SWA_KV_W = SWA_KV_HEADS * HEAD_DIM
MOBA_W = MOBA_HEADS * HEAD_DIM
D_FF = 4 * D_MODEL
N_MOD = 6
RMS_EPS = 1e-6

VMEM_LIMIT_BYTES = 56 * 1024 * 1024
TOKEN_TILE = 512
FF_CHUNK = 1024
SUBLANES = 8
ADA_COLS = 1536
BIAS_SLOTS = 8
SWA_UNROLL = 3
ACC_ROWS = HEAD_DIM + 16
NEG_INF = float("-inf")

F32 = jnp.float32
BF16 = jnp.bfloat16


def _resident(block_shape, index_map):
    return pl.BlockSpec(block_shape, index_map, pipeline_mode=pl.Buffered(1))


def _dot(a, b):
    return jnp.dot(a, b, preferred_element_type=F32)


def _dot_nt(a, b):
    return lax.dot_general(a, b, (((1,), (1,)), ((), ())), preferred_element_type=F32)


def _rms_modulate(xv, g, shift, scale):
    ms = jnp.mean(xv * xv, axis=-1, keepdims=True)
    y = xv * lax.rsqrt(ms + RMS_EPS)
    return (y * g) * (1.0 + scale) + shift


def _toeplitz(u, rows):
    wide = jnp.broadcast_to(u, (rows, 2 * rows))
    return pltpu.roll(wide, 0, 1, stride=1, stride_axis=0)[:, rows:]


def _ada_kernel(ct_ref, w_ref, b_ref, o_ref, *, batch):
    cs_t = jax.nn.silu(ct_ref[...])
    w = w_ref[...]
    rows = [jnp.sum(w * cs_t[:, b:b + 1], axis=0, keepdims=True) for b in range(batch)]
    rows.append(jnp.zeros((SUBLANES - batch, w.shape[1]), F32))
    o_ref[...] = jnp.concatenate(rows, axis=0) + b_ref[...]


def _ada(c_t, w, b, batch):
    n = w.shape[1]
    bn = ADA_COLS
    assert n % bn == 0
    return pl.pallas_call(
        functools.partial(_ada_kernel, batch=batch),
        grid=(n // bn,),
        in_specs=[pl.BlockSpec((D_MODEL, SUBLANES), lambda j: (0, 0)),
                  pl.BlockSpec((D_MODEL, bn), lambda j: (0, j)),
                  pl.BlockSpec((1, bn), lambda j: (0, j))],
        out_specs=pl.BlockSpec((SUBLANES, bn), lambda j: (0, j)),
        out_shape=jax.ShapeDtypeStruct((SUBLANES, n), F32),
        compiler_params=pltpu.CompilerParams(dimension_semantics=("arbitrary",),
                                             vmem_limit_bytes=VMEM_LIMIT_BYTES),
        name="ada",
    )(c_t, w, b)


def _t5_bucket(dist):
    n = np.maximum(dist, 0)
    nf = np.maximum(n, 1).astype(np.float32)
    large = MAX_EXACT + (np.log(nf / np.float32(MAX_EXACT)) / np.float32(math.log(MAX_DISTANCE / MAX_EXACT))
                         * np.float32(NUM_BUCKETS - MAX_EXACT)).astype(np.int32)
    large = np.minimum(large, NUM_BUCKETS - 1)
    return np.where(n < MAX_EXACT, n, large).astype(np.int32)


def _bias_rows_kernel(bucket_ref, rbt_ref, o_ref):
    bucket = bucket_ref[...]
    n = bucket.shape[1]
    onehot = jnp.where(lax.broadcasted_iota(jnp.int32, (NUM_BUCKETS, n), 0) == bucket, 1.0, 0.0)
    rows = jnp.dot(rbt_ref[...], onehot, preferred_element_type=F32,
                   precision=lax.Precision.HIGHEST)
    rows = jnp.where(bucket >= 0, rows * LOG2E, NEG_INF)
    for d in range(BIAS_SLOTS):
        o_ref[d] = rows[:, d * MOBA_BLOCK:(d + 2) * MOBA_BLOCK]


def _bias_rows(rel_bias):
    n = (BIAS_SLOTS + 1) * MOBA_BLOCK
    dist = np.arange(n, dtype=np.int32) - MOBA_BLOCK
    bucket = jnp.asarray(np.where(dist >= 0, _t5_bucket(dist), -1)[None, :], dtype=jnp.int32)
    return pl.pallas_call(
        _bias_rows_kernel,
        out_shape=jax.ShapeDtypeStruct((BIAS_SLOTS, N_ATTN_HEADS, 2 * MOBA_BLOCK), F32),
        name="bias_rows",
    )(bucket, rel_bias.astype(F32).T)


IN_WIDTHS = (SWA_Q_W, SWA_KV_W, SWA_KV_W, MOBA_W, MOBA_W, MOBA_W, D_MODEL, D_MODEL)
IN_OFFS = tuple(sum(IN_WIDTHS[:k]) for k in range(len(IN_WIDTHS) + 1))
FEAT_COLS = (0, 2, 3, 5)
FEAT_W = sum(IN_WIDTHS[k] for k in FEAT_COLS)
TOK_COLS = (1, 4, 6, 7)
TOK_W = sum(IN_WIDTHS[k] for k in TOK_COLS)
Q_COLS = (0, 3)
BLOCKS_PER_TILE = TOKEN_TILE // MOBA_BLOCK


def _proj_kernel(x_ref, shift_ref, scale_ref, g_ref, w_ref, *refs):
    n_cast = (len(refs) - 12) // 2
    cast_in, refs = refs[:n_cast], refs[n_cast:]
    ka_ref, kb_ref, ga_ref, gb_ref, qa_ref, va_ref, qb_ref, vb_ref, sel_ref = refs[:9]
    cast_out, (wk_ref, wt_ref, kmean_ref) = refs[9:9 + n_cast], refs[9 + n_cast:]

    for src, dst in zip(cast_in, cast_out):
        dst[...] = src[...].astype(dst.dtype)

    tile = pl.program_id(0) % (kmean_ref.shape[0] // BLOCKS_PER_TILE)

    @pl.when(tile == 0)
    def _():
        kmean_ref[...] = jnp.zeros(kmean_ref.shape, F32)

    @pl.when(pl.program_id(0) == 0)
    def _():
        o = 0
        for k in TOK_COLS:
            wk_ref[:, o:o + IN_WIDTHS[k]] = w_ref[:, IN_OFFS[k]:IN_OFFS[k + 1]].astype(BF16)
            o += IN_WIDTHS[k]
        o = 0
        for k in FEAT_COLS:
            wcol = w_ref[:, IN_OFFS[k]:IN_OFFS[k + 1]]
            if k in Q_COLS:
                wcol = wcol * (ATTN_SCALE * LOG2E)
            wt_ref[o:o + IN_WIDTHS[k], :] = wcol.T.astype(BF16)
            o += IN_WIDTHS[k]

    h = _rms_modulate(x_ref[...], g_ref[...], shift_ref[0, 0], scale_ref[0, 0]).astype(BF16)
    tok_refs = dict(zip(TOK_COLS, (ka_ref, kb_ref, ga_ref, gb_ref)))
    feat_refs = dict(zip(FEAT_COLS, (qa_ref, va_ref, qb_ref, vb_ref)))

    def token_major(k):
        o = sum(IN_WIDTHS[c] for c in TOK_COLS[:TOK_COLS.index(k)])
        out = _dot(h, wk_ref[:, o:o + IN_WIDTHS[k]])
        tok_refs[k][...] = out.astype(BF16)
        return out

    def feature_major(k):
        o = sum(IN_WIDTHS[c] for c in FEAT_COLS[:FEAT_COLS.index(k)])
        out = _dot_nt(wt_ref[o:o + IN_WIDTHS[k], :], h).astype(BF16)
        blk = feat_refs[k].shape[-1]
        for t in range(TOKEN_TILE // blk):
            feat_refs[k][0, t] = out[:, t * blk:(t + 1) * blk]
        return out

    _update_kmean(token_major(4), kmean_ref, tile)
    qb_t = feature_major(3)
    token_major(1)
    token_major(6)
    _moba_select(qb_t, sel_ref, kmean_ref, tile)
    token_major(7)
    for k in (0, 2, 5):
        feature_major(k)


def _update_kmean(kb, kmean_ref, tile):
    for t in range(BLOCKS_PER_TILE):
        kmean_ref[pl.ds(tile * BLOCKS_PER_TILE + t, 1), :] = jnp.mean(
            kb[t * MOBA_BLOCK:(t + 1) * MOBA_BLOCK, :], axis=0, keepdims=True)


def _moba_select(qb_t, sel_ref, kmean_ref, tile):
    nblk = kmean_ref.shape[0]
    blocks_per_tile = BLOCKS_PER_TILE
    blk = lax.broadcasted_iota(jnp.int32, (nblk, TOKEN_TILE), 0)
    qblk = tile * blocks_per_tile + lax.broadcasted_iota(jnp.int32, (nblk, TOKEN_TILE), 1) // MOBA_BLOCK
    past = blk < qblk
    own = jnp.where(blk == qblk, 0.0, NEG_INF)
    blk_f = blk.astype(F32)
    for h in range(MOBA_HEADS):
        hs = slice(h * HEAD_DIM, (h + 1) * HEAD_DIM)
        kmean = kmean_ref[:, hs]
        k1 = kmean.astype(BF16)
        r1 = kmean - k1.astype(F32)
        k2 = r1.astype(BF16)
        k3 = (r1 - k2.astype(F32)).astype(BF16)
        parts = _dot(jnp.concatenate([k1, k2, k3], axis=0), qb_t[hs, :])
        gate = parts[0:nblk] + parts[nblk:2 * nblk] + parts[2 * nblk:3 * nblk]
        gate = jnp.where(past, gate, NEG_INF)
        mask = own
        for _ in range(MOBA_TOPK):
            top = jnp.max(gate, axis=0, keepdims=True)
            first = jnp.min(jnp.where(gate == top, blk_f, float(nblk)), axis=0, keepdims=True)
            pick = blk_f == first
            mask = jnp.where(jnp.logical_and(pick, past), 0.0, mask)
            gate = jnp.where(pick, NEG_INF, gate)
        sel_ref[0, h] = mask


def _proj(x2, mod4, g1, w_in, cast_weights, batch, seq):
    tokens = x2.shape[0]
    steps = tokens // TOKEN_TILE
    assert all(w.shape[0] % (16 * steps) == 0 for w in cast_weights)
    cast_specs = [pl.BlockSpec((w.shape[0] // steps, w.shape[1]), lambda i: (i, 0)) for w in cast_weights]
    tiles_per_seq = seq // TOKEN_TILE
    na = TOKEN_TILE // SWA_BLOCK
    nb = TOKEN_TILE // MOBA_BLOCK

    def feat_spec(width, blk, per_tile):
        return pl.BlockSpec((1, per_tile, width, blk),
                            lambda i: (i // tiles_per_seq, i % tiles_per_seq, 0, 0))

    out_shape = (
        jax.ShapeDtypeStruct((tokens, SWA_KV_W), BF16),
        jax.ShapeDtypeStruct((tokens, MOBA_W), BF16),
        jax.ShapeDtypeStruct((tokens, D_MODEL), BF16),
        jax.ShapeDtypeStruct((tokens, D_MODEL), BF16),
        jax.ShapeDtypeStruct((batch, seq // SWA_BLOCK, SWA_Q_W, SWA_BLOCK), BF16),
        jax.ShapeDtypeStruct((batch, seq // SWA_BLOCK, SWA_KV_W, SWA_BLOCK), BF16),
        jax.ShapeDtypeStruct((batch, seq // MOBA_BLOCK, MOBA_W, MOBA_BLOCK), BF16),
        jax.ShapeDtypeStruct((batch, seq // MOBA_BLOCK, MOBA_W, MOBA_BLOCK), BF16),
        jax.ShapeDtypeStruct((batch, MOBA_HEADS, seq // MOBA_BLOCK, seq), F32),
    )
    tok_spec = lambda w: pl.BlockSpec((TOKEN_TILE, w), lambda i: (i, 0))
    return pl.pallas_call(
        _proj_kernel,
        grid=(steps,),
        in_specs=[
            pl.BlockSpec((TOKEN_TILE, D_MODEL), lambda i: (i, 0)),
            pl.BlockSpec((1, 1, 1, D_MODEL), lambda i: (i // tiles_per_seq, 0, 0, 0)),
            pl.BlockSpec((1, 1, 1, D_MODEL), lambda i: (i // tiles_per_seq, 1, 0, 0)),
            _resident((1, D_MODEL), lambda i: (0, 0)),
            _resident(w_in.shape, lambda i: (0, 0)),
        ] + cast_specs,
        out_specs=(tok_spec(SWA_KV_W), tok_spec(MOBA_W), tok_spec(D_MODEL), tok_spec(D_MODEL),
                   feat_spec(SWA_Q_W, SWA_BLOCK, na), feat_spec(SWA_KV_W, SWA_BLOCK, na),
                   feat_spec(MOBA_W, MOBA_BLOCK, nb), feat_spec(MOBA_W, MOBA_BLOCK, nb),
                   pl.BlockSpec((1, MOBA_HEADS, seq // MOBA_BLOCK, TOKEN_TILE),
                                lambda i: (i // tiles_per_seq, 0, 0, i % tiles_per_seq)))
        + tuple(cast_specs),
        out_shape=out_shape + tuple(jax.ShapeDtypeStruct(w.shape, BF16) for w in cast_weights),
        scratch_shapes=[pltpu.VMEM((D_MODEL, TOK_W), BF16), pltpu.VMEM((FEAT_W, D_MODEL), BF16),
                        pltpu.VMEM((seq // MOBA_BLOCK, MOBA_W), F32)],
        compiler_params=pltpu.CompilerParams(dimension_semantics=("arbitrary",),
                                             vmem_limit_bytes=VMEM_LIMIT_BYTES),
        name="proj",
    )(x2, mod4, mod4, g1, w_in, *cast_weights)


def _swa_kernel(q_ref, k_ref, v_ref, rows_ref, sink_ref, o_ref, bias_ref, ot_ref, s_ref):
    L = SWA_BLOCK
    nblk = q_ref.shape[1]

    @pl.when(pl.program_id(0) == 0)
    def _():
        k_idx = lax.broadcasted_iota(jnp.int32, (L, L), 0)
        q_idx = lax.broadcasted_iota(jnp.int32, (L, L), 1)
        for h in range(SWA_Q_HEADS):
            g, gi = divmod(h, SWA_GROUP)
            t0 = rows_ref[0, h:h + 1, MOBA_BLOCK:MOBA_BLOCK + L]
            r = _toeplitz(jnp.concatenate([t0, t0], axis=1), L)
            cols = slice(gi * L, (gi + 1) * L)
            bias_ref[g, 0:L, cols] = jnp.where(q_idx < k_idx, r, NEG_INF)
            bias_ref[g, L:2 * L, cols] = jnp.where(q_idx >= k_idx, r, NEG_INF)

    ones_blk = jnp.concatenate([jnp.ones((1, 2 * L), BF16), jnp.zeros((ACC_ROWS - HEAD_DIM - 1, 2 * L), BF16)],
                               axis=0)
    PW = 2 * L
    chains = [(g, hp) for g in range(SWA_KV_HEADS) for hp in range(SWA_GROUP // 2)]
    kv_cols = [slice(g * HEAD_DIM, (g + 1) * HEAD_DIM) for g in range(SWA_KV_HEADS)]

    def q_pair(c, g, hp):
        h0 = g * SWA_GROUP + 2 * hp
        return jnp.concatenate([q_ref[0, c, (h0 + e) * HEAD_DIM:(h0 + e + 1) * HEAD_DIM, :] for e in range(2)],
                               axis=1)

    def scores_first(idx):
        g, hp = chains[idx]
        lanes = slice(hp * PW, (hp + 1) * PW)
        s_ref[idx, 0:L, :] = jnp.full((L, PW), NEG_INF, F32)
        s_ref[idx, L:2 * L, :] = _dot(k_ref[0, 0:L, kv_cols[g]], q_pair(0, g, hp)) + bias_ref[g, L:2 * L, lanes]

    def scores(c, idx):
        g, hp = chains[idx]
        kw = k_ref[0, pl.ds(pl.multiple_of((c - 1) * L, L), 2 * L), kv_cols[g]]
        s_ref[idx] = _dot(kw, q_pair(c, g, hp)) + bias_ref[g, :, hp * PW:(hp + 1) * PW]

    def attend(c, c_prev, idx):
        g, hp = chains[idx]
        hs = kv_cols[g]
        s = s_ref[idx]
        sink = sink_ref[g, :, hp * PW:(hp + 1) * PW]
        m = jnp.maximum(jnp.max(s, axis=0, keepdims=True), sink)
        p = jnp.exp2(s - m).astype(BF16)
        v_win = jnp.concatenate([v_ref[0, c_prev, hs, :], v_ref[0, c, hs, :]], axis=1)
        pv = _dot(jnp.concatenate([v_win, ones_blk], axis=0), p)
        o = pv[0:HEAD_DIM] / (pv[HEAD_DIM:HEAD_DIM + 1] + jnp.exp2(sink - m))
        for e in range(2):
            r = (g * SWA_GROUP + 2 * hp + e) * HEAD_DIM
            ot_ref[r:r + HEAD_DIM, :] = o[:, e * L:(e + 1) * L]

    def emit(c):
        o_ref[0, pl.ds(pl.multiple_of(c * L, L), L), :] = ot_ref[...].T.astype(o_ref.dtype)

    for idx in range(len(chains)):
        scores_first(idx)
    for idx in range(len(chains)):
        attend(0, 0, idx)
        scores(1, idx)
    emit(0)

    def body(c, carry):
        for idx in range(len(chains)):
            attend(c, c - 1, idx)
            scores(c + 1, idx)
        emit(c)
        return carry

    def body_unrolled(u, carry):
        for e in range(SWA_UNROLL):
            carry = body(SWA_UNROLL * u + 1 + e, carry)
        return carry

    lax.fori_loop(0, (nblk - 2) // SWA_UNROLL, body_unrolled, 0)
    for idx in range(len(chains)):
        attend(nblk - 1, nblk - 2, idx)
    emit(nblk - 1)


def _swa(qa_t, ka3, va_t, rows, sink_rows):
    batch, nblk = qa_t.shape[0], qa_t.shape[1]
    seq = nblk * SWA_BLOCK
    assert nblk >= 2 and (nblk - 2) % SWA_UNROLL == 0
    return pl.pallas_call(
        _swa_kernel,
        grid=(batch,),
        in_specs=[
            pl.BlockSpec((1, nblk, SWA_Q_W, SWA_BLOCK), lambda b: (b, 0, 0, 0)),
            pl.BlockSpec((1, seq, SWA_KV_W), lambda b: (b, 0, 0)),
            pl.BlockSpec((1, nblk, SWA_KV_W, SWA_BLOCK), lambda b: (b, 0, 0, 0)),
            _resident((1, N_ATTN_HEADS, 2 * MOBA_BLOCK), lambda b: (0, 0, 0)),
            _resident(sink_rows.shape, lambda b: (0, 0, 0)),
        ],
        out_specs=pl.BlockSpec((1, seq, SWA_Q_W), lambda b: (b, 0, 0)),
        out_shape=jax.ShapeDtypeStruct((batch, seq, SWA_Q_W), BF16),
        scratch_shapes=[pltpu.VMEM((SWA_KV_HEADS, 2 * SWA_BLOCK, SWA_GROUP * SWA_BLOCK), F32),
                        pltpu.VMEM((SWA_Q_W, SWA_BLOCK), F32),
                        pltpu.VMEM((SWA_Q_HEADS // 2, 2 * SWA_BLOCK, 2 * SWA_BLOCK), F32)],
        compiler_params=pltpu.CompilerParams(dimension_semantics=("arbitrary",),
                                             vmem_limit_bytes=VMEM_LIMIT_BYTES),
        name="swa",
    )(qa_t, ka3, va_t, rows, sink_rows)


def _moba_kernel(q_ref, qn_ref, k_ref, v_ref, sel_ref, rows_ref, o_ref, bias_ref, m_ref, acc_ref, s_ref, cm_ref):
    step = pl.program_id(1)
    last = pl.num_programs(1) - 1
    MB = MOBA_BLOCK
    H = MOBA_HEADS

    @pl.when(jnp.logical_and(pl.program_id(0) == 0, step == 0))
    def _():
        for h in range(H):
            for d in range(BIAS_SLOTS):
                bias_ref[h, d] = _toeplitz(rows_ref[d, SWA_Q_HEADS + h:SWA_Q_HEADS + h + 1, :], MB)

    m_ref[...] = jnp.full(m_ref.shape, NEG_INF, F32)
    acc_ref[...] = jnp.zeros(acc_ref.shape, F32)
    heads = [slice(h * HEAD_DIM, (h + 1) * HEAD_DIM) for h in range(H)]
    ones_blk = jnp.concatenate([jnp.ones((1, MB), BF16), jnp.zeros((ACC_ROWS - HEAD_DIM - 1, MB), BF16)],
                               axis=0)

    def sel_row(j, qs, h):
        return sel_ref[0, h, pl.ds(j, 1), qs * MB:(qs + 1) * MB]

    def scores(j, qs, h):
        c = qs * H + h
        rows = pl.ds(pl.multiple_of(j * MB, MB), MB)
        slot = jnp.minimum(2 * step + qs - j, BIAS_SLOTS - 1)
        s = _dot(k_ref[0, rows, heads[h]], q_ref[0, qs, heads[h], :]) + bias_ref[h, slot]
        s_ref[c] = s
        cm_ref[c] = jnp.max(s, axis=0, keepdims=True) + sel_row(j, qs, h)

    def own_scores(blk, q_blk, c, h):
        rows = pl.ds(pl.multiple_of(blk * MB, MB), MB)
        s = _dot(k_ref[0, rows, heads[h]], q_blk[heads[h], :]) + bias_ref[h, 0]
        s_ref[c] = s
        cm_ref[c] = jnp.max(s, axis=0, keepdims=True)

    def accumulate(j, qs, h):
        c = qs * H + h
        m_old = m_ref[c]
        m_new = jnp.maximum(m_old, cm_ref[c])
        alpha = jnp.exp2(m_old - m_new)
        p = jnp.exp2(s_ref[c] - (m_new - sel_row(j, qs, h))).astype(BF16)
        v_aug = jnp.concatenate([v_ref[0, j, heads[h], :], ones_blk], axis=0)
        rs = slice(c * ACC_ROWS, (c + 1) * ACC_ROWS)
        acc_ref[rs, :] = alpha * acc_ref[rs, :] + _dot(v_aug, p)
        m_ref[c] = m_new

    blk_a = 2 * step
    blk_b = 2 * step + 1

    @pl.when(step == 0)
    def _():
        for h in range(H):
            own_scores(blk_b, q_ref[0, 1], H + h, h)

    for h in range(H):
        accumulate(blk_b, 1, h)
        scores(blk_a, 0, h)
        scores(blk_a, 1, h)

    def body(j, carry):
        for qs in range(2):
            for h in range(H):
                accumulate(j, qs, h)
                scores(j - 1, qs, h)
        return carry

    def body2(u, carry):
        return body(blk_a - 2 * u - 1, body(blk_a - 2 * u, carry))

    lax.fori_loop(0, step, body2, 0)

    @pl.when(step < last)
    def _():
        for h in range(H):
            accumulate(0, 0, h)
            accumulate(0, 1, h)
            own_scores(blk_b + 2, qn_ref[0, 0], H + h, h)

    @pl.when(step == last)
    def _():
        for h in range(H):
            accumulate(0, 0, h)
            accumulate(0, 1, h)

    for qs in range(2):
        outs = []
        for h in range(H):
            base = (qs * H + h) * ACC_ROWS
            outs.append(acc_ref[base:base + HEAD_DIM, :] / acc_ref[base + HEAD_DIM:base + HEAD_DIM + 1, :])
        o_ref[0, qs * MB:(qs + 1) * MB, :] = jnp.concatenate(outs, axis=0).T.astype(o_ref.dtype)


def _moba(qb_t, kb3, vb_t, sel, rows):
    batch, nblk = qb_t.shape[0], qb_t.shape[1]
    seq = nblk * MOBA_BLOCK
    assert nblk % 2 == 0
    chains = 2 * MOBA_HEADS
    return pl.pallas_call(
        _moba_kernel,
        grid=(batch, nblk // 2),
        in_specs=[
            pl.BlockSpec((1, 2, MOBA_W, MOBA_BLOCK), lambda b, i: (b, i, 0, 0)),
            pl.BlockSpec((1, 1, MOBA_W, MOBA_BLOCK), lambda b, i: (b, jnp.minimum(2 * i + 3, nblk - 1), 0, 0)),
            pl.BlockSpec((1, seq, MOBA_W), lambda b, i: (b, 0, 0)),
            pl.BlockSpec((1, nblk, MOBA_W, MOBA_BLOCK), lambda b, i: (b, 0, 0, 0)),
            pl.BlockSpec((1, MOBA_HEADS, nblk, 2 * MOBA_BLOCK), lambda b, i: (b, 0, 0, i)),
            _resident(rows.shape, lambda b, i: (0, 0, 0)),
        ],
        out_specs=pl.BlockSpec((1, 2 * MOBA_BLOCK, MOBA_W), lambda b, i: (b, i, 0)),
        out_shape=jax.ShapeDtypeStruct((batch, seq, MOBA_W), BF16),
        scratch_shapes=[pltpu.VMEM((MOBA_HEADS, BIAS_SLOTS, MOBA_BLOCK, MOBA_BLOCK), F32),
                        pltpu.VMEM((chains, 1, MOBA_BLOCK), F32),
                        pltpu.VMEM((chains * ACC_ROWS, MOBA_BLOCK), F32),
                        pltpu.VMEM((chains, MOBA_BLOCK, MOBA_BLOCK), F32),
                        pltpu.VMEM((chains, 1, MOBA_BLOCK), F32)],
        compiler_params=pltpu.CompilerParams(dimension_semantics=("arbitrary", "arbitrary"),
                                             vmem_limit_bytes=VMEM_LIMIT_BYTES),
        name="moba",
    )(qb_t, qb_t, kb3, vb_t, sel, rows)


def _post_kernel(x_ref, ya_ref, yb_ref, ga_ref, gb_ref, gate1_ref, shift2_ref, scale2_ref,
                 gate2_ref, g2_ref, gf_ref, wa_ref, wb_ref, wo_ref, w1_ref, w2_ref, o_ref):
    a = _dot(ya_ref[...], wa_ref[...])
    b = _dot(yb_ref[...], wb_ref[...])
    merged = (jax.nn.sigmoid(ga_ref[...].astype(F32)) * a
              + jax.nn.sigmoid(gb_ref[...].astype(F32)) * b)
    x1 = x_ref[...] + gate1_ref[0, 0] * _dot(merged.astype(BF16), wo_ref[...])
    h2 = _rms_modulate(x1, g2_ref[...], shift2_ref[0, 0], scale2_ref[0, 0]).astype(BF16)
    y = jnp.zeros_like(x1)
    for c in range(D_FF // FF_CHUNK):
        cs = slice(c * FF_CHUNK, (c + 1) * FF_CHUNK)
        u = jnp.square(jnp.maximum(_dot(h2, w1_ref[:, cs]), 0.0)).astype(BF16)
        y = y + _dot(u, w2_ref[cs, :])
    x2 = x1 + gate2_ref[0, 0] * y
    ms = jnp.mean(x2 * x2, axis=-1, keepdims=True)
    o_ref[...] = (x2 * lax.rsqrt(ms + RMS_EPS)) * gf_ref[...]


def _post(x2, ya, yb, ga, gb, mod4, g2, gf, wa, wb, wo, w1, w2, seq):
    tokens = x2.shape[0]
    tiles_per_seq = seq // TOKEN_TILE
    tok = lambda w: pl.BlockSpec((TOKEN_TILE, w), lambda i: (i, 0))
    modrow = lambda k: pl.BlockSpec((1, 1, 1, D_MODEL), lambda i: (i // tiles_per_seq, k, 0, 0))
    full = lambda a: _resident(a.shape, lambda i: (0, 0))
    return pl.pallas_call(
        _post_kernel,
        grid=(tokens // TOKEN_TILE,),
        in_specs=[tok(D_MODEL), tok(SWA_Q_W), tok(MOBA_W), tok(D_MODEL), tok(D_MODEL),
                  modrow(2), modrow(3), modrow(4), modrow(5),
                  full(g2), full(gf), full(wa), full(wb), full(wo), full(w1), full(w2)],
        out_specs=tok(D_MODEL),
        out_shape=jax.ShapeDtypeStruct((tokens, D_MODEL), F32),
        compiler_params=pltpu.CompilerParams(dimension_semantics=("arbitrary",),
                                             vmem_limit_bytes=VMEM_LIMIT_BYTES),
        name="post",
    )(x2, ya, yb, ga, gb, mod4, mod4, mod4, mod4, g2, gf, wa, wb, wo, w1, w2)


def kernel(x, c, ada_w, ada_b, norm1_g, norm2_g, w_in, attn_sinks, rel_bias, w_branch_a,
           w_branch_b, w_out, w_mlp_in, w_mlp_out, final_g):
    batch, seq, _ = x.shape
    depth = ada_w.shape[0]
    assert seq % TOKEN_TILE == 0 and TOKEN_TILE % MOBA_BLOCK == 0 and batch <= SUBLANES
    far = np.arange((BIAS_SLOTS - 1) * MOBA_BLOCK - (MOBA_BLOCK - 1), max(seq, BIAS_SLOTS * MOBA_BLOCK))
    assert np.all(_t5_bucket(far) == _t5_bucket(far[:1]))
    assert 2 * SWA_BLOCK <= MOBA_BLOCK and SWA_WINDOW == SWA_BLOCK
    assert depth == 1
    l = 0

    rows = _bias_rows(rel_bias)
    c_t = jnp.zeros((D_MODEL, SUBLANES), F32).at[:, :batch].set(c.astype(F32).T)

    x2 = x.reshape(batch * seq, D_MODEL)
    mod = _ada(c_t, ada_w[l], ada_b[l][None, :], batch)
    mod4 = mod[:batch].reshape(batch, N_MOD, 1, D_MODEL)
    later_weights = [w.astype(F32) for w in (w_branch_a[l], w_branch_b[l], w_out[l], w_mlp_in[l], w_mlp_out[l])]
    ka, kb, ga, gb, qa_t, va_t, qb_t, vb_t, sel, wa, wb, wo, w1, w2 = _proj(
        x2, mod4, norm1_g[l][None, :], w_in[l].astype(F32), later_weights, batch, seq)

    sink_rows = jnp.broadcast_to(
        (attn_sinks[l].astype(F32) * LOG2E).reshape(SWA_KV_HEADS, 1, SWA_GROUP, 1),
        (SWA_KV_HEADS, 1, SWA_GROUP, SWA_BLOCK)).reshape(SWA_KV_HEADS, 1, SWA_GROUP * SWA_BLOCK)
    ya = _swa(qa_t, ka.reshape(batch, seq, SWA_KV_W), va_t, rows, sink_rows)

    yb = _moba(qb_t, kb.reshape(batch, seq, MOBA_W), vb_t, sel, rows)

    out = _post(x2, ya.reshape(batch * seq, SWA_Q_W), yb.reshape(batch * seq, MOBA_W), ga, gb,
                mod4, norm2_g[l][None, :], final_g[None, :], wa, wb, wo, w1, w2, seq)
    return out.reshape(batch, seq, D_MODEL)
```

```python
import functools
import math

import jax
import jax.numpy as jnp
import numpy as np
from jax import lax
from jax.experimental import pallas as pl
from jax.experimental.pallas import tpu as pltpu

D_MODEL = 1024
HEAD_DIM = 64
ATTN_SCALE = HEAD_DIM ** -0.5
LOG2E = math.log2(math.e)
SWA_Q_HEADS = 8
SWA_KV_HEADS = 2
SWA_GROUP = SWA_Q_HEADS // SWA_KV_HEADS
SWA_WINDOW = 128
SWA_BLOCK = 128
MOBA_HEADS = 8
MOBA_BLOCK = 256
MOBA_TOPK = 3
NUM_BUCKETS = 32
MAX_EXACT = NUM_BUCKETS // 2
MAX_DISTANCE = 2048
N_ATTN_HEADS = SWA_Q_HEADS + MOBA_HEADS
SWA_Q_W = SWA_Q_HEADS * HEAD_DIM
SWA_KV_W = SWA_KV_HEADS * HEAD_DIM
MOBA_W = MOBA_HEADS * HEAD_DIM
D_FF = 4 * D_MODEL
N_MOD = 6
RMS_EPS = 1e-6

VMEM_LIMIT_BYTES = 56 * 1024 * 1024
TOKEN_TILE = 512
FF_CHUNK = 1024
SUBLANES = 8
ADA_COLS = 1536
BIAS_SLOTS = 8
SWA_UNROLL = 3
MOBA_Q_PER_STEP = 4
ACC_ROWS = HEAD_DIM + 16
NEG_INF = float("-inf")

F32 = jnp.float32
BF16 = jnp.bfloat16


def _resident(block_shape, index_map):
    return pl.BlockSpec(block_shape, index_map, pipeline_mode=pl.Buffered(1))


def _dot(a, b):
    return jnp.dot(a, b, preferred_element_type=F32)


def _dot_nt(a, b):
    return lax.dot_general(a, b, (((1,), (1,)), ((), ())), preferred_element_type=F32)


def _rms_modulate(xv, g, shift, scale):
    ms = jnp.mean(xv * xv, axis=-1, keepdims=True)
    y = xv * lax.rsqrt(ms + RMS_EPS)
    return (y * g) * (1.0 + scale) + shift


def _toeplitz(u, rows):
    wide = jnp.broadcast_to(u, (rows, 2 * rows))
    return pltpu.roll(wide, 0, 1, stride=1, stride_axis=0)[:, rows:]


def _ada_kernel(ct_ref, w_ref, b_ref, o_ref, *, batch):
    cs_t = jax.nn.silu(ct_ref[...])
    w = w_ref[...]
    rows = [jnp.sum(w * cs_t[:, b:b + 1], axis=0, keepdims=True) for b in range(batch)]
    rows.append(jnp.zeros((SUBLANES - batch, w.shape[1]), F32))
    o_ref[...] = jnp.concatenate(rows, axis=0) + b_ref[...]


def _ada(c_t, w, b, batch):
    n = w.shape[1]
    bn = ADA_COLS
    assert n % bn == 0
    return pl.pallas_call(
        functools.partial(_ada_kernel, batch=batch),
        grid=(n // bn,),
        in_specs=[pl.BlockSpec((D_MODEL, SUBLANES), lambda j: (0, 0)),
                  pl.BlockSpec((D_MODEL, bn), lambda j: (0, j)),
                  pl.BlockSpec((1, bn), lambda j: (0, j))],
        out_specs=pl.BlockSpec((SUBLANES, bn), lambda j: (0, j)),
        out_shape=jax.ShapeDtypeStruct((SUBLANES, n), F32),
        compiler_params=pltpu.CompilerParams(dimension_semantics=("arbitrary",),
                                             vmem_limit_bytes=VMEM_LIMIT_BYTES),
        name="ada",
    )(c_t, w, b)


def _t5_bucket(dist):
    n = np.maximum(dist, 0)
    nf = np.maximum(n, 1).astype(np.float32)
    large = MAX_EXACT + (np.log(nf / np.float32(MAX_EXACT)) / np.float32(math.log(MAX_DISTANCE / MAX_EXACT))
                         * np.float32(NUM_BUCKETS - MAX_EXACT)).astype(np.int32)
    large = np.minimum(large, NUM_BUCKETS - 1)
    return np.where(n < MAX_EXACT, n, large).astype(np.int32)


def _bias_rows_kernel(bucket_ref, rbt_ref, o_ref):
    bucket = bucket_ref[...]
    n = bucket.shape[1]
    onehot = jnp.where(lax.broadcasted_iota(jnp.int32, (NUM_BUCKETS, n), 0) == bucket, 1.0, 0.0)
    rows = jnp.dot(rbt_ref[...], onehot, preferred_element_type=F32,
                   precision=lax.Precision.HIGHEST)
    rows = jnp.where(bucket >= 0, rows * LOG2E, NEG_INF)
    for d in range(BIAS_SLOTS):
        o_ref[d] = rows[:, d * MOBA_BLOCK:(d + 2) * MOBA_BLOCK]


def _bias_rows(rel_bias):
    n = (BIAS_SLOTS + 1) * MOBA_BLOCK
    dist = np.arange(n, dtype=np.int32) - MOBA_BLOCK
    bucket = jnp.asarray(np.where(dist >= 0, _t5_bucket(dist), -1)[None, :], dtype=jnp.int32)
    return pl.pallas_call(
        _bias_rows_kernel,
        out_shape=jax.ShapeDtypeStruct((BIAS_SLOTS, N_ATTN_HEADS, 2 * MOBA_BLOCK), F32),
        name="bias_rows",
    )(bucket, rel_bias.astype(F32).T)


IN_WIDTHS = (SWA_Q_W, SWA_KV_W, SWA_KV_W, MOBA_W, MOBA_W, MOBA_W, D_MODEL, D_MODEL)
IN_OFFS = tuple(sum(IN_WIDTHS[:k]) for k in range(len(IN_WIDTHS) + 1))
FEAT_COLS = (0, 2, 3, 5)
FEAT_W = sum(IN_WIDTHS[k] for k in FEAT_COLS)
TOK_COLS = (1, 4, 6, 7)
TOK_W = sum(IN_WIDTHS[k] for k in TOK_COLS)
Q_COLS = (0, 3)
BLOCKS_PER_TILE = TOKEN_TILE // MOBA_BLOCK


def _proj_kernel(x_ref, shift_ref, scale_ref, g_ref, w_ref, *refs):
    n_cast = (len(refs) - 12) // 2
    cast_in, refs = refs[:n_cast], refs[n_cast:]
    ka_ref, kb_ref, ga_ref, gb_ref, qa_ref, va_ref, qb_ref, vb_ref, sel_ref = refs[:9]
    cast_out, (wk_ref, wt_ref, kmean_ref) = refs[9:9 + n_cast], refs[9 + n_cast:]

    for src, dst in zip(cast_in, cast_out):
        dst[...] = src[...].astype(dst.dtype)

    tile = pl.program_id(0) % (kmean_ref.shape[0] // BLOCKS_PER_TILE)

    @pl.when(tile == 0)
    def _():
        kmean_ref[...] = jnp.zeros(kmean_ref.shape, F32)

    @pl.when(pl.program_id(0) == 0)
    def _():
        o = 0
        for k in TOK_COLS:
            wk_ref[:, o:o + IN_WIDTHS[k]] = w_ref[:, IN_OFFS[k]:IN_OFFS[k + 1]].astype(BF16)
            o += IN_WIDTHS[k]
        o = 0
        for k in FEAT_COLS:
            wcol = w_ref[:, IN_OFFS[k]:IN_OFFS[k + 1]]
            if k in Q_COLS:
                wcol = wcol * (ATTN_SCALE * LOG2E)
            wt_ref[o:o + IN_WIDTHS[k], :] = wcol.T.astype(BF16)
            o += IN_WIDTHS[k]

    h = _rms_modulate(x_ref[...], g_ref[...], shift_ref[0, 0], scale_ref[0, 0]).astype(BF16)
    tok_refs = dict(zip(TOK_COLS, (ka_ref, kb_ref, ga_ref, gb_ref)))
    feat_refs = dict(zip(FEAT_COLS, (qa_ref, va_ref, qb_ref, vb_ref)))

    def token_major(k):
        o = sum(IN_WIDTHS[c] for c in TOK_COLS[:TOK_COLS.index(k)])
        out = _dot(h, wk_ref[:, o:o + IN_WIDTHS[k]])
        tok_refs[k][...] = out.astype(BF16)
        return out

    def feature_major(k):
        o = sum(IN_WIDTHS[c] for c in FEAT_COLS[:FEAT_COLS.index(k)])
        out = _dot_nt(wt_ref[o:o + IN_WIDTHS[k], :], h).astype(BF16)
        blk = feat_refs[k].shape[-1]
        for t in range(TOKEN_TILE // blk):
            feat_refs[k][0, t] = out[:, t * blk:(t + 1) * blk]
        return out

    _update_kmean(token_major(4), kmean_ref, tile)
    qb_t = feature_major(3)
    token_major(1)
    token_major(6)
    _moba_select(qb_t, sel_ref, kmean_ref, tile)
    token_major(7)
    for k in (0, 2, 5):
        feature_major(k)


def _update_kmean(kb, kmean_ref, tile):
    for t in range(BLOCKS_PER_TILE):
        kmean_ref[pl.ds(tile * BLOCKS_PER_TILE + t, 1), :] = jnp.mean(
            kb[t * MOBA_BLOCK:(t + 1) * MOBA_BLOCK, :], axis=0, keepdims=True)


def _moba_select(qb_t, sel_ref, kmean_ref, tile):
    nblk = kmean_ref.shape[0]
    blocks_per_tile = BLOCKS_PER_TILE
    blk = lax.broadcasted_iota(jnp.int32, (nblk, TOKEN_TILE), 0)
    qblk = tile * blocks_per_tile + lax.broadcasted_iota(jnp.int32, (nblk, TOKEN_TILE), 1) // MOBA_BLOCK
    past = blk < qblk
    own = jnp.where(blk == qblk, 0.0, NEG_INF)
    blk_f = blk.astype(F32)
    for h in range(MOBA_HEADS):
        hs = slice(h * HEAD_DIM, (h + 1) * HEAD_DIM)
        kmean = kmean_ref[:, hs]
        k1 = kmean.astype(BF16)
        r1 = kmean - k1.astype(F32)
        k2 = r1.astype(BF16)
        k3 = (r1 - k2.astype(F32)).astype(BF16)
        parts = _dot(jnp.concatenate([k1, k2, k3], axis=0), qb_t[hs, :])
        gate = parts[0:nblk] + parts[nblk:2 * nblk] + parts[2 * nblk:3 * nblk]
        gate = jnp.where(past, gate, NEG_INF)
        mask = own
        for _ in range(MOBA_TOPK):
            top = jnp.max(gate, axis=0, keepdims=True)
            first = jnp.min(jnp.where(gate == top, blk_f, float(nblk)), axis=0, keepdims=True)
            pick = blk_f == first
            mask = jnp.where(jnp.logical_and(pick, past), 0.0, mask)
            gate = jnp.where(pick, NEG_INF, gate)
        sel_ref[0, h] = mask


def _proj(x2, mod4, g1, w_in, cast_weights, batch, seq):
    tokens = x2.shape[0]
    steps = tokens // TOKEN_TILE
    assert all(w.shape[0] % (16 * steps) == 0 for w in cast_weights)
    cast_specs = [pl.BlockSpec((w.shape[0] // steps, w.shape[1]), lambda i: (i, 0)) for w in cast_weights]
    tiles_per_seq = seq // TOKEN_TILE
    na = TOKEN_TILE // SWA_BLOCK
    nb = TOKEN_TILE // MOBA_BLOCK

    def feat_spec(width, blk, per_tile):
        return pl.BlockSpec((1, per_tile, width, blk),
                            lambda i: (i // tiles_per_seq, i % tiles_per_seq, 0, 0))

    out_shape = (
        jax.ShapeDtypeStruct((tokens, SWA_KV_W), BF16),
        jax.ShapeDtypeStruct((tokens, MOBA_W), BF16),
        jax.ShapeDtypeStruct((tokens, D_MODEL), BF16),
        jax.ShapeDtypeStruct((tokens, D_MODEL), BF16),
        jax.ShapeDtypeStruct((batch, seq // SWA_BLOCK, SWA_Q_W, SWA_BLOCK), BF16),
        jax.ShapeDtypeStruct((batch, seq // SWA_BLOCK, SWA_KV_W, SWA_BLOCK), BF16),
        jax.ShapeDtypeStruct((batch, seq // MOBA_BLOCK, MOBA_W, MOBA_BLOCK), BF16),
        jax.ShapeDtypeStruct((batch, seq // MOBA_BLOCK, MOBA_W, MOBA_BLOCK), BF16),
        jax.ShapeDtypeStruct((batch, MOBA_HEADS, seq // MOBA_BLOCK, seq), F32),
    )
    tok_spec = lambda w: pl.BlockSpec((TOKEN_TILE, w), lambda i: (i, 0))
    return pl.pallas_call(
        _proj_kernel,
        grid=(steps,),
        in_specs=[
            pl.BlockSpec((TOKEN_TILE, D_MODEL), lambda i: (i, 0)),
            pl.BlockSpec((1, 1, 1, D_MODEL), lambda i: (i // tiles_per_seq, 0, 0, 0)),
            pl.BlockSpec((1, 1, 1, D_MODEL), lambda i: (i // tiles_per_seq, 1, 0, 0)),
            _resident((1, D_MODEL), lambda i: (0, 0)),
            _resident(w_in.shape, lambda i: (0, 0)),
        ] + cast_specs,
        out_specs=(tok_spec(SWA_KV_W), tok_spec(MOBA_W), tok_spec(D_MODEL), tok_spec(D_MODEL),
                   feat_spec(SWA_Q_W, SWA_BLOCK, na), feat_spec(SWA_KV_W, SWA_BLOCK, na),
                   feat_spec(MOBA_W, MOBA_BLOCK, nb), feat_spec(MOBA_W, MOBA_BLOCK, nb),
                   pl.BlockSpec((1, MOBA_HEADS, seq // MOBA_BLOCK, TOKEN_TILE),
                                lambda i: (i // tiles_per_seq, 0, 0, i % tiles_per_seq)))
        + tuple(cast_specs),
        out_shape=out_shape + tuple(jax.ShapeDtypeStruct(w.shape, BF16) for w in cast_weights),
        scratch_shapes=[pltpu.VMEM((D_MODEL, TOK_W), BF16), pltpu.VMEM((FEAT_W, D_MODEL), BF16),
                        pltpu.VMEM((seq // MOBA_BLOCK, MOBA_W), F32)],
        compiler_params=pltpu.CompilerParams(dimension_semantics=("arbitrary",),
                                             vmem_limit_bytes=VMEM_LIMIT_BYTES),
        name="proj",
    )(x2, mod4, mod4, g1, w_in, *cast_weights)


def _swa_kernel(q_ref, k_ref, v_ref, rows_ref, sink_ref, o_ref, bias_ref, ot_ref, s_ref):
    L = SWA_BLOCK
    nblk = q_ref.shape[1]

    @pl.when(pl.program_id(0) == 0)
    def _():
        k_idx = lax.broadcasted_iota(jnp.int32, (L, L), 0)
        q_idx = lax.broadcasted_iota(jnp.int32, (L, L), 1)
        for h in range(SWA_Q_HEADS):
            g, gi = divmod(h, SWA_GROUP)
            t0 = rows_ref[0, h:h + 1, MOBA_BLOCK:MOBA_BLOCK + L]
            r = _toeplitz(jnp.concatenate([t0, t0], axis=1), L)
            cols = slice(gi * L, (gi + 1) * L)
            bias_ref[g, 0:L, cols] = jnp.where(q_idx < k_idx, r, NEG_INF)
            bias_ref[g, L:2 * L, cols] = jnp.where(q_idx >= k_idx, r, NEG_INF)

    ones_blk = jnp.concatenate([jnp.ones((1, 2 * L), BF16), jnp.zeros((ACC_ROWS - HEAD_DIM - 1, 2 * L), BF16)],
                               axis=0)
    PW = 2 * L
    chains = [(g, hp) for g in range(SWA_KV_HEADS) for hp in range(SWA_GROUP // 2)]
    kv_cols = [slice(g * HEAD_DIM, (g + 1) * HEAD_DIM) for g in range(SWA_KV_HEADS)]

    def q_pair(c, g, hp):
        h0 = g * SWA_GROUP + 2 * hp
        return jnp.concatenate([q_ref[0, c, (h0 + e) * HEAD_DIM:(h0 + e + 1) * HEAD_DIM, :] for e in range(2)],
                               axis=1)

    def scores_first(idx):
        g, hp = chains[idx]
        lanes = slice(hp * PW, (hp + 1) * PW)
        s_ref[idx, 0:L, :] = jnp.full((L, PW), NEG_INF, F32)
        s_ref[idx, L:2 * L, :] = _dot(k_ref[0, 0:L, kv_cols[g]], q_pair(0, g, hp)) + bias_ref[g, L:2 * L, lanes]

    def scores(c, idx):
        g, hp = chains[idx]
        kw = k_ref[0, pl.ds(pl.multiple_of((c - 1) * L, L), 2 * L), kv_cols[g]]
        s_ref[idx] = _dot(kw, q_pair(c, g, hp)) + bias_ref[g, :, hp * PW:(hp + 1) * PW]

    def attend(c, c_prev, idx):
        g, hp = chains[idx]
        hs = kv_cols[g]
        s = s_ref[idx]
        sink = sink_ref[g, :, hp * PW:(hp + 1) * PW]
        m = jnp.maximum(jnp.max(s, axis=0, keepdims=True), sink)
        p = jnp.exp2(s - m).astype(BF16)
        v_win = jnp.concatenate([v_ref[0, c_prev, hs, :], v_ref[0, c, hs, :]], axis=1)
        pv = _dot(jnp.concatenate([v_win, ones_blk], axis=0), p)
        o = pv[0:HEAD_DIM] / (pv[HEAD_DIM:HEAD_DIM + 1] + jnp.exp2(sink - m))
        for e in range(2):
            r = (g * SWA_GROUP + 2 * hp + e) * HEAD_DIM
            ot_ref[r:r + HEAD_DIM, :] = o[:, e * L:(e + 1) * L]

    def emit(c):
        o_ref[0, pl.ds(pl.multiple_of(c * L, L), L), :] = ot_ref[...].T.astype(o_ref.dtype)

    for idx in range(len(chains)):
        scores_first(idx)
    for idx in range(len(chains)):
        attend(0, 0, idx)
        scores(1, idx)
    emit(0)

    def body(c, carry):
        for idx in range(len(chains)):
            attend(c, c - 1, idx)
            scores(c + 1, idx)
        emit(c)
        return carry

    def body_unrolled(u, carry):
        for e in range(SWA_UNROLL):
            carry = body(SWA_UNROLL * u + 1 + e, carry)
        return carry

    lax.fori_loop(0, (nblk - 2) // SWA_UNROLL, body_unrolled, 0)
    for idx in range(len(chains)):
        attend(nblk - 1, nblk - 2, idx)
    emit(nblk - 1)


def _swa(qa_t, ka3, va_t, rows, sink_rows):
    batch, nblk = qa_t.shape[0], qa_t.shape[1]
    seq = nblk * SWA_BLOCK
    assert nblk >= 2 and (nblk - 2) % SWA_UNROLL == 0
    return pl.pallas_call(
        _swa_kernel,
        grid=(batch,),
        in_specs=[
            pl.BlockSpec((1, nblk, SWA_Q_W, SWA_BLOCK), lambda b: (b, 0, 0, 0)),
            pl.BlockSpec((1, seq, SWA_KV_W), lambda b: (b, 0, 0)),
            pl.BlockSpec((1, nblk, SWA_KV_W, SWA_BLOCK), lambda b: (b, 0, 0, 0)),
            _resident((1, N_ATTN_HEADS, 2 * MOBA_BLOCK), lambda b: (0, 0, 0)),
            _resident(sink_rows.shape, lambda b: (0, 0, 0)),
        ],
        out_specs=pl.BlockSpec((1, seq, SWA_Q_W), lambda b: (b, 0, 0)),
        out_shape=jax.ShapeDtypeStruct((batch, seq, SWA_Q_W), BF16),
        scratch_shapes=[pltpu.VMEM((SWA_KV_HEADS, 2 * SWA_BLOCK, SWA_GROUP * SWA_BLOCK), F32),
                        pltpu.VMEM((SWA_Q_W, SWA_BLOCK), F32),
                        pltpu.VMEM((SWA_Q_HEADS // 2, 2 * SWA_BLOCK, 2 * SWA_BLOCK), F32)],
        compiler_params=pltpu.CompilerParams(dimension_semantics=("arbitrary",),
                                             vmem_limit_bytes=VMEM_LIMIT_BYTES),
        name="swa",
    )(qa_t, ka3, va_t, rows, sink_rows)


def _moba_kernel(q_ref, qn_ref, k_ref, v_ref, sel_ref, rows_ref, o_ref, bias_ref, m_ref, acc_ref, s_ref, cm_ref):
    step = pl.program_id(1)
    last = pl.num_programs(1) - 1
    MB = MOBA_BLOCK
    H = MOBA_HEADS
    G = MOBA_Q_PER_STEP

    @pl.when(jnp.logical_and(pl.program_id(0) == 0, step == 0))
    def _():
        for h in range(H):
            for d in range(BIAS_SLOTS):
                bias_ref[h, d] = _toeplitz(rows_ref[d, SWA_Q_HEADS + h:SWA_Q_HEADS + h + 1, :], MB)

    m_ref[...] = jnp.full(m_ref.shape, NEG_INF, F32)
    acc_ref[...] = jnp.zeros(acc_ref.shape, F32)
    heads = [slice(h * HEAD_DIM, (h + 1) * HEAD_DIM) for h in range(H)]
    ones_blk = jnp.concatenate([jnp.ones((1, MB), BF16), jnp.zeros((ACC_ROWS - HEAD_DIM - 1, MB), BF16)],
                               axis=0)

    def sel_row(j, qs, h):
        return sel_ref[0, h, pl.ds(j, 1), qs * MB:(qs + 1) * MB]

    def scores(j, qs, h):
        c = qs * H + h
        rows = pl.ds(pl.multiple_of(j * MB, MB), MB)
        slot = jnp.minimum(G * step + qs - j, BIAS_SLOTS - 1)
        s = _dot(k_ref[0, rows, heads[h]], q_ref[0, qs, heads[h], :]) + bias_ref[h, slot]
        s_ref[c] = s
        cm_ref[c] = jnp.max(s, axis=0, keepdims=True) + sel_row(j, qs, h)

    def own_scores(blk, q_blk, c, h):
        rows = pl.ds(pl.multiple_of(blk * MB, MB), MB)
        s = _dot(k_ref[0, rows, heads[h]], q_blk[heads[h], :]) + bias_ref[h, 0]
        s_ref[c] = s
        cm_ref[c] = jnp.max(s, axis=0, keepdims=True)

    def accumulate(j, qs, h):
        c = qs * H + h
        m_old = m_ref[c]
        m_new = jnp.maximum(m_old, cm_ref[c])
        alpha = jnp.exp2(m_old - m_new)
        p = jnp.exp2(s_ref[c] - (m_new - sel_row(j, qs, h))).astype(BF16)
        v_aug = jnp.concatenate([v_ref[0, j, heads[h], :], ones_blk], axis=0)
        rs = slice(c * ACC_ROWS, (c + 1) * ACC_ROWS)
        acc_ref[rs, :] = alpha * acc_ref[rs, :] + _dot(v_aug, p)
        m_ref[c] = m_new

    first = G * step

    @pl.when(step == 0)
    def _():
        for h in range(H):
            own_scores(G - 1, q_ref[0, G - 1], (G - 1) * H + h, h)

    for n in range(G - 1):
        for qs in range(G):
            for h in range(H):
                if qs >= G - 1 - n:
                    accumulate(first + G - 1 - n, qs, h)
                if qs >= G - 2 - n:
                    scores(first + G - 2 - n, qs, h)

    def body(j, carry):
        for qs in range(G):
            for h in range(H):
                accumulate(j, qs, h)
                scores(j - 1, qs, h)
        return carry

    def body2(u, carry):
        return body(first - 2 * u - 1, body(first - 2 * u, carry))

    lax.fori_loop(0, (G // 2) * step, body2, 0)

    @pl.when(step < last)
    def _():
        for h in range(H):
            for qs in range(G):
                accumulate(0, qs, h)
            own_scores(first + 2 * G - 1, qn_ref[0, 0], (G - 1) * H + h, h)

    @pl.when(step == last)
    def _():
        for h in range(H):
            for qs in range(G):
                accumulate(0, qs, h)

    for qs in range(G):
        outs = []
        for h in range(H):
            base = (qs * H + h) * ACC_ROWS
            outs.append(acc_ref[base:base + HEAD_DIM, :] / acc_ref[base + HEAD_DIM:base + HEAD_DIM + 1, :])
        o_ref[0, qs * MB:(qs + 1) * MB, :] = jnp.concatenate(outs, axis=0).T.astype(o_ref.dtype)


def _moba(qb_t, kb3, vb_t, sel, rows):
    batch, nblk = qb_t.shape[0], qb_t.shape[1]
    seq = nblk * MOBA_BLOCK
    g = MOBA_Q_PER_STEP
    assert g % 2 == 0 and nblk % g == 0
    chains = g * MOBA_HEADS
    return pl.pallas_call(
        _moba_kernel,
        grid=(batch, nblk // g),
        in_specs=[
            pl.BlockSpec((1, g, MOBA_W, MOBA_BLOCK), lambda b, i: (b, i, 0, 0)),
            pl.BlockSpec((1, 1, MOBA_W, MOBA_BLOCK), lambda b, i: (b, jnp.minimum(g * i + 2 * g - 1, nblk - 1), 0, 0)),
            pl.BlockSpec((1, seq, MOBA_W), lambda b, i: (b, 0, 0)),
            pl.BlockSpec((1, nblk, MOBA_W, MOBA_BLOCK), lambda b, i: (b, 0, 0, 0)),
            pl.BlockSpec((1, MOBA_HEADS, nblk, g * MOBA_BLOCK), lambda b, i: (b, 0, 0, i)),
            _resident(rows.shape, lambda b, i: (0, 0, 0)),
        ],
        out_specs=pl.BlockSpec((1, g * MOBA_BLOCK, MOBA_W), lambda b, i: (b, i, 0)),
        out_shape=jax.ShapeDtypeStruct((batch, seq, MOBA_W), BF16),
        scratch_shapes=[pltpu.VMEM((MOBA_HEADS, BIAS_SLOTS, MOBA_BLOCK, MOBA_BLOCK), F32),
                        pltpu.VMEM((chains, 1, MOBA_BLOCK), F32),
                        pltpu.VMEM((chains * ACC_ROWS, MOBA_BLOCK), F32),
                        pltpu.VMEM((chains, MOBA_BLOCK, MOBA_BLOCK), F32),
                        pltpu.VMEM((chains, 1, MOBA_BLOCK), F32)],
        compiler_params=pltpu.CompilerParams(dimension_semantics=("arbitrary", "arbitrary"),
                                             vmem_limit_bytes=VMEM_LIMIT_BYTES),
        name="moba",
    )(qb_t, qb_t, kb3, vb_t, sel, rows)


def _post_kernel(x_ref, ya_ref, yb_ref, ga_ref, gb_ref, gate1_ref, shift2_ref, scale2_ref,
                 gate2_ref, g2_ref, gf_ref, wa_ref, wb_ref, wo_ref, w1_ref, w2_ref, o_ref):
    a = _dot(ya_ref[...], wa_ref[...])
    b = _dot(yb_ref[...], wb_ref[...])
    merged = (jax.nn.sigmoid(ga_ref[...].astype(F32)) * a
              + jax.nn.sigmoid(gb_ref[...].astype(F32)) * b)
    x1 = x_ref[...] + gate1_ref[0, 0] * _dot(merged.astype(BF16), wo_ref[...])
    h2 = _rms_modulate(x1, g2_ref[...], shift2_ref[0, 0], scale2_ref[0, 0]).astype(BF16)
    y = jnp.zeros_like(x1)
    for c in range(D_FF // FF_CHUNK):
        cs = slice(c * FF_CHUNK, (c + 1) * FF_CHUNK)
        u = jnp.square(jnp.maximum(_dot(h2, w1_ref[:, cs]), 0.0)).astype(BF16)
        y = y + _dot(u, w2_ref[cs, :])
    x2 = x1 + gate2_ref[0, 0] * y
    ms = jnp.mean(x2 * x2, axis=-1, keepdims=True)
    o_ref[...] = (x2 * lax.rsqrt(ms + RMS_EPS)) * gf_ref[...]


def _post(x2, ya, yb, ga, gb, mod4, g2, gf, wa, wb, wo, w1, w2, seq):
    tokens = x2.shape[0]
    tiles_per_seq = seq // TOKEN_TILE
    tok = lambda w: pl.BlockSpec((TOKEN_TILE, w), lambda i: (i, 0))
    modrow = lambda k: pl.BlockSpec((1, 1, 1, D_MODEL), lambda i: (i // tiles_per_seq, k, 0, 0))
    full = lambda a: _resident(a.shape, lambda i: (0, 0))
    return pl.pallas_call(
        _post_kernel,
        grid=(tokens // TOKEN_TILE,),
        in_specs=[tok(D_MODEL), tok(SWA_Q_W), tok(MOBA_W), tok(D_MODEL), tok(D_MODEL),
                  modrow(2), modrow(3), modrow(4), modrow(5),
                  full(g2), full(gf), full(wa), full(wb), full(wo), full(w1), full(w2)],
        out_specs=tok(D_MODEL),
        out_shape=jax.ShapeDtypeStruct((tokens, D_MODEL), F32),
        compiler_params=pltpu.CompilerParams(dimension_semantics=("arbitrary",),
                                             vmem_limit_bytes=VMEM_LIMIT_BYTES),
        name="post",
    )(x2, ya, yb, ga, gb, mod4, mod4, mod4, mod4, g2, gf, wa, wb, wo, w1, w2)


def kernel(x, c, ada_w, ada_b, norm1_g, norm2_g, w_in, attn_sinks, rel_bias, w_branch_a,
           w_branch_b, w_out, w_mlp_in, w_mlp_out, final_g):
    batch, seq, _ = x.shape
    depth = ada_w.shape[0]
    assert seq % TOKEN_TILE == 0 and TOKEN_TILE % MOBA_BLOCK == 0 and batch <= SUBLANES
    far = np.arange((BIAS_SLOTS - 1) * MOBA_BLOCK - (MOBA_BLOCK - 1), max(seq, BIAS_SLOTS * MOBA_BLOCK))
    assert np.all(_t5_bucket(far) == _t5_bucket(far[:1]))
    assert 2 * SWA_BLOCK <= MOBA_BLOCK and SWA_WINDOW == SWA_BLOCK
    assert depth == 1
    l = 0

    rows = _bias_rows(rel_bias)
    c_t = jnp.zeros((D_MODEL, SUBLANES), F32).at[:, :batch].set(c.astype(F32).T)

    x2 = x.reshape(batch * seq, D_MODEL)
    mod = _ada(c_t, ada_w[l], ada_b[l][None, :], batch)
    mod4 = mod[:batch].reshape(batch, N_MOD, 1, D_MODEL)
    later_weights = [w.astype(F32) for w in (w_branch_a[l], w_branch_b[l], w_out[l], w_mlp_in[l], w_mlp_out[l])]
    ka, kb, ga, gb, qa_t, va_t, qb_t, vb_t, sel, wa, wb, wo, w1, w2 = _proj(
        x2, mod4, norm1_g[l][None, :], w_in[l].astype(F32), later_weights, batch, seq)

    sink_rows = jnp.broadcast_to(
        (attn_sinks[l].astype(F32) * LOG2E).reshape(SWA_KV_HEADS, 1, SWA_GROUP, 1),
        (SWA_KV_HEADS, 1, SWA_GROUP, SWA_BLOCK)).reshape(SWA_KV_HEADS, 1, SWA_GROUP * SWA_BLOCK)
    ya = _swa(qa_t, ka.reshape(batch, seq, SWA_KV_W), va_t, rows, sink_rows)

    yb = _moba(qb_t, kb.reshape(batch, seq, MOBA_W), vb_t, sel, rows)

    out = _post(x2, ya.reshape(batch * seq, SWA_Q_W), yb.reshape(batch * seq, MOBA_W), ga, gb,
                mod4, norm2_g[l][None, :], final_g[None, :], wa, wb, wo, w1, w2, seq)
    return out.reshape(batch, seq, D_MODEL)
```

```python
import functools
import math

import jax
import jax.numpy as jnp
import numpy as np
from jax import lax
from jax.experimental import pallas as pl
from jax.experimental.pallas import tpu as pltpu

D_MODEL = 1024
HEAD_DIM = 64
ATTN_SCALE = HEAD_DIM ** -0.5
LOG2E = math.log2(math.e)
SWA_Q_HEADS = 8
SWA_KV_HEADS = 2
SWA_GROUP = SWA_Q_HEADS // SWA_KV_HEADS
SWA_WINDOW = 128
SWA_BLOCK = 128
MOBA_HEADS = 8
MOBA_BLOCK = 256
MOBA_TOPK = 3
NUM_BUCKETS = 32
MAX_EXACT = NUM_BUCKETS // 2
MAX_DISTANCE = 2048
N_ATTN_HEADS = SWA_Q_HEADS + MOBA_HEADS
SWA_Q_W = SWA_Q_HEADS * HEAD_DIM
SWA_KV_W = SWA_KV_HEADS * HEAD_DIM
MOBA_W = MOBA_HEADS * HEAD_DIM
D_FF = 4 * D_MODEL
N_MOD = 6
RMS_EPS = 1e-6

VMEM_LIMIT_BYTES = 56 * 1024 * 1024
TOKEN_TILE = 512
FF_CHUNK = 1024
SUBLANES = 8
ADA_COLS = 1536
BIAS_SLOTS = 8
SWA_UNROLL = 3
MOBA_Q_PER_STEP = 4
ACC_ROWS = HEAD_DIM + 16
NEG_INF = float("-inf")

F32 = jnp.float32
BF16 = jnp.bfloat16


def _resident(block_shape, index_map):
    return pl.BlockSpec(block_shape, index_map, pipeline_mode=pl.Buffered(1))


def _dot(a, b):
    return jnp.dot(a, b, preferred_element_type=F32)


def _dot_nt(a, b):
    return lax.dot_general(a, b, (((1,), (1,)), ((), ())), preferred_element_type=F32)


def _rms_modulate(xv, g, shift, scale):
    ms = jnp.mean(xv * xv, axis=-1, keepdims=True)
    y = xv * lax.rsqrt(ms + RMS_EPS)
    return (y * g) * (1.0 + scale) + shift


def _toeplitz(u, rows):
    wide = jnp.broadcast_to(u, (rows, 2 * rows))
    return pltpu.roll(wide, 0, 1, stride=1, stride_axis=0)[:, rows:]


def _ada_kernel(ct_ref, w_ref, b_ref, o_ref, *, batch):
    cs_t = jax.nn.silu(ct_ref[...])
    w = w_ref[...]
    rows = [jnp.sum(w * cs_t[:, b:b + 1], axis=0, keepdims=True) for b in range(batch)]
    rows.append(jnp.zeros((SUBLANES - batch, w.shape[1]), F32))
    o_ref[...] = jnp.concatenate(rows, axis=0) + b_ref[...]


def _ada(c_t, w, b, batch):
    n = w.shape[1]
    bn = ADA_COLS
    assert n % bn == 0
    return pl.pallas_call(
        functools.partial(_ada_kernel, batch=batch),
        grid=(n // bn,),
        in_specs=[pl.BlockSpec((D_MODEL, SUBLANES), lambda j: (0, 0)),
                  pl.BlockSpec((D_MODEL, bn), lambda j: (0, j)),
                  pl.BlockSpec((1, bn), lambda j: (0, j))],
        out_specs=pl.BlockSpec((SUBLANES, bn), lambda j: (0, j)),
        out_shape=jax.ShapeDtypeStruct((SUBLANES, n), F32),
        compiler_params=pltpu.CompilerParams(dimension_semantics=("arbitrary",),
                                             vmem_limit_bytes=VMEM_LIMIT_BYTES),
        name="ada",
    )(c_t, w, b)


def _t5_bucket(dist):
    n = np.maximum(dist, 0)
    nf = np.maximum(n, 1).astype(np.float32)
    large = MAX_EXACT + (np.log(nf / np.float32(MAX_EXACT)) / np.float32(math.log(MAX_DISTANCE / MAX_EXACT))
                         * np.float32(NUM_BUCKETS - MAX_EXACT)).astype(np.int32)
    large = np.minimum(large, NUM_BUCKETS - 1)
    return np.where(n < MAX_EXACT, n, large).astype(np.int32)


def _bias_rows_kernel(bucket_ref, rbt_ref, o_ref):
    bucket = bucket_ref[...]
    n = bucket.shape[1]
    onehot = jnp.where(lax.broadcasted_iota(jnp.int32, (NUM_BUCKETS, n), 0) == bucket, 1.0, 0.0)
    rows = jnp.dot(rbt_ref[...], onehot, preferred_element_type=F32,
                   precision=lax.Precision.HIGHEST)
    rows = jnp.where(bucket >= 0, rows * LOG2E, NEG_INF)
    for d in range(BIAS_SLOTS):
        o_ref[d] = rows[:, d * MOBA_BLOCK:(d + 2) * MOBA_BLOCK]


def _bias_rows(rel_bias):
    n = (BIAS_SLOTS + 1) * MOBA_BLOCK
    dist = np.arange(n, dtype=np.int32) - MOBA_BLOCK
    bucket = jnp.asarray(np.where(dist >= 0, _t5_bucket(dist), -1)[None, :], dtype=jnp.int32)
    return pl.pallas_call(
        _bias_rows_kernel,
        out_shape=jax.ShapeDtypeStruct((BIAS_SLOTS, N_ATTN_HEADS, 2 * MOBA_BLOCK), F32),
        name="bias_rows",
    )(bucket, rel_bias.astype(F32).T)


IN_WIDTHS = (SWA_Q_W, SWA_KV_W, SWA_KV_W, MOBA_W, MOBA_W, MOBA_W, D_MODEL, D_MODEL)
IN_OFFS = tuple(sum(IN_WIDTHS[:k]) for k in range(len(IN_WIDTHS) + 1))
FEAT_COLS = (0, 2, 3, 5)
FEAT_W = sum(IN_WIDTHS[k] for k in FEAT_COLS)
TOK_COLS = (1, 4, 6, 7)
TOK_W = sum(IN_WIDTHS[k] for k in TOK_COLS)
Q_COLS = (0, 3)
BLOCKS_PER_TILE = TOKEN_TILE // MOBA_BLOCK


def _proj_kernel(x_ref, shift_ref, scale_ref, g_ref, w_ref, *refs):
    n_cast = (len(refs) - 12) // 2
    cast_in, refs = refs[:n_cast], refs[n_cast:]
    ka_ref, kb_ref, ga_ref, gb_ref, qa_ref, va_ref, qb_ref, vb_ref, sel_ref = refs[:9]
    cast_out, (wk_ref, wt_ref, kmean_ref) = refs[9:9 + n_cast], refs[9 + n_cast:]

    for src, dst in zip(cast_in, cast_out):
        dst[...] = src[...].astype(dst.dtype)

    tile = pl.program_id(0) % (kmean_ref.shape[0] // BLOCKS_PER_TILE)

    @pl.when(tile == 0)
    def _():
        kmean_ref[...] = jnp.zeros(kmean_ref.shape, F32)

    @pl.when(pl.program_id(0) == 0)
    def _():
        o = 0
        for k in TOK_COLS:
            wk_ref[:, o:o + IN_WIDTHS[k]] = w_ref[:, IN_OFFS[k]:IN_OFFS[k + 1]].astype(BF16)
            o += IN_WIDTHS[k]
        o = 0
        for k in FEAT_COLS:
            wcol = w_ref[:, IN_OFFS[k]:IN_OFFS[k + 1]]
            if k in Q_COLS:
                wcol = wcol * (ATTN_SCALE * LOG2E)
            wt_ref[o:o + IN_WIDTHS[k], :] = wcol.T.astype(BF16)
            o += IN_WIDTHS[k]

    h = _rms_modulate(x_ref[...], g_ref[...], shift_ref[0, 0], scale_ref[0, 0]).astype(BF16)
    tok_refs = dict(zip(TOK_COLS, (ka_ref, kb_ref, ga_ref, gb_ref)))
    feat_refs = dict(zip(FEAT_COLS, (qa_ref, va_ref, qb_ref, vb_ref)))

    def token_major(k):
        o = sum(IN_WIDTHS[c] for c in TOK_COLS[:TOK_COLS.index(k)])
        out = _dot(h, wk_ref[:, o:o + IN_WIDTHS[k]])
        tok_refs[k][...] = out.astype(BF16)
        return out

    def feature_major(k):
        o = sum(IN_WIDTHS[c] for c in FEAT_COLS[:FEAT_COLS.index(k)])
        out = _dot_nt(wt_ref[o:o + IN_WIDTHS[k], :], h).astype(BF16)
        blk = feat_refs[k].shape[-1]
        for t in range(TOKEN_TILE // blk):
            feat_refs[k][0, t] = out[:, t * blk:(t + 1) * blk]
        return out

    _update_kmean(token_major(4), kmean_ref, tile)
    qb_t = feature_major(3)
    token_major(1)
    token_major(6)
    _moba_select(qb_t, sel_ref, kmean_ref, tile)
    token_major(7)
    for k in (0, 2, 5):
        feature_major(k)


def _update_kmean(kb, kmean_ref, tile):
    for t in range(BLOCKS_PER_TILE):
        kmean_ref[pl.ds(tile * BLOCKS_PER_TILE + t, 1), :] = jnp.mean(
            kb[t * MOBA_BLOCK:(t + 1) * MOBA_BLOCK, :], axis=0, keepdims=True)


def _moba_select(qb_t, sel_ref, kmean_ref, tile):
    nblk = kmean_ref.shape[0]
    blocks_per_tile = BLOCKS_PER_TILE
    blk = lax.broadcasted_iota(jnp.int32, (nblk, TOKEN_TILE), 0)
    qblk = tile * blocks_per_tile + lax.broadcasted_iota(jnp.int32, (nblk, TOKEN_TILE), 1) // MOBA_BLOCK
    past = blk < qblk
    own = jnp.where(blk == qblk, 0.0, NEG_INF)
    blk_f = blk.astype(F32)
    for h in range(MOBA_HEADS):
        hs = slice(h * HEAD_DIM, (h + 1) * HEAD_DIM)
        kmean = kmean_ref[:, hs]
        k1 = kmean.astype(BF16)
        r1 = kmean - k1.astype(F32)
        k2 = r1.astype(BF16)
        k3 = (r1 - k2.astype(F32)).astype(BF16)
        parts = _dot(jnp.concatenate([k1, k2, k3], axis=0), qb_t[hs, :])
        gate = parts[0:nblk] + parts[nblk:2 * nblk] + parts[2 * nblk:3 * nblk]
        gate = jnp.where(past, gate, NEG_INF)
        mask = own
        for _ in range(MOBA_TOPK):
            top = jnp.max(gate, axis=0, keepdims=True)
            first = jnp.min(jnp.where(gate == top, blk_f, float(nblk)), axis=0, keepdims=True)
            pick = blk_f == first
            mask = jnp.where(jnp.logical_and(pick, past), 0.0, mask)
            gate = jnp.where(pick, NEG_INF, gate)
        sel_ref[0, h] = mask


def _proj(x2, mod4, g1, w_in, cast_weights, batch, seq):
    tokens = x2.shape[0]
    steps = tokens // TOKEN_TILE
    assert all(w.shape[0] % (16 * steps) == 0 for w in cast_weights)
    cast_specs = [pl.BlockSpec((w.shape[0] // steps, w.shape[1]), lambda i: (i, 0)) for w in cast_weights]
    tiles_per_seq = seq // TOKEN_TILE
    na = TOKEN_TILE // SWA_BLOCK
    nb = TOKEN_TILE // MOBA_BLOCK

    def feat_spec(width, blk, per_tile):
        return pl.BlockSpec((1, per_tile, width, blk),
                            lambda i: (i // tiles_per_seq, i % tiles_per_seq, 0, 0))

    out_shape = (
        jax.ShapeDtypeStruct((tokens, SWA_KV_W), BF16),
        jax.ShapeDtypeStruct((tokens, MOBA_W), BF16),
        jax.ShapeDtypeStruct((tokens, D_MODEL), BF16),
        jax.ShapeDtypeStruct((tokens, D_MODEL), BF16),
        jax.ShapeDtypeStruct((batch, seq // SWA_BLOCK, SWA_Q_W, SWA_BLOCK), BF16),
        jax.ShapeDtypeStruct((batch, seq // SWA_BLOCK, SWA_KV_W, SWA_BLOCK), BF16),
        jax.ShapeDtypeStruct((batch, seq // MOBA_BLOCK, MOBA_W, MOBA_BLOCK), BF16),
        jax.ShapeDtypeStruct((batch, seq // MOBA_BLOCK, MOBA_W, MOBA_BLOCK), BF16),
        jax.ShapeDtypeStruct((batch, MOBA_HEADS, seq // MOBA_BLOCK, seq), F32),
    )
    tok_spec = lambda w: pl.BlockSpec((TOKEN_TILE, w), lambda i: (i, 0))
    return pl.pallas_call(
        _proj_kernel,
        grid=(steps,),
        in_specs=[
            pl.BlockSpec((TOKEN_TILE, D_MODEL), lambda i: (i, 0)),
            pl.BlockSpec((1, 1, 1, D_MODEL), lambda i: (i // tiles_per_seq, 0, 0, 0)),
            pl.BlockSpec((1, 1, 1, D_MODEL), lambda i: (i // tiles_per_seq, 1, 0, 0)),
            _resident((1, D_MODEL), lambda i: (0, 0)),
            _resident(w_in.shape, lambda i: (0, 0)),
        ] + cast_specs,
        out_specs=(tok_spec(SWA_KV_W), tok_spec(MOBA_W), tok_spec(D_MODEL), tok_spec(D_MODEL),
                   feat_spec(SWA_Q_W, SWA_BLOCK, na), feat_spec(SWA_KV_W, SWA_BLOCK, na),
                   feat_spec(MOBA_W, MOBA_BLOCK, nb), feat_spec(MOBA_W, MOBA_BLOCK, nb),
                   pl.BlockSpec((1, MOBA_HEADS, seq // MOBA_BLOCK, TOKEN_TILE),
                                lambda i: (i // tiles_per_seq, 0, 0, i % tiles_per_seq)))
        + tuple(cast_specs),
        out_shape=out_shape + tuple(jax.ShapeDtypeStruct(w.shape, BF16) for w in cast_weights),
        scratch_shapes=[pltpu.VMEM((D_MODEL, TOK_W), BF16), pltpu.VMEM((FEAT_W, D_MODEL), BF16),
                        pltpu.VMEM((seq // MOBA_BLOCK, MOBA_W), F32)],
        compiler_params=pltpu.CompilerParams(dimension_semantics=("arbitrary",),
                                             vmem_limit_bytes=VMEM_LIMIT_BYTES),
        name="proj",
    )(x2, mod4, mod4, g1, w_in, *cast_weights)


def _swa_kernel(q_ref, k_ref, v_ref, rows_ref, sink_ref, o_ref, bias_ref, ot_ref, s_ref):
    L = SWA_BLOCK
    nblk = q_ref.shape[1]

    @pl.when(pl.program_id(0) == 0)
    def _():
        k_idx = lax.broadcasted_iota(jnp.int32, (L, L), 0)
        q_idx = lax.broadcasted_iota(jnp.int32, (L, L), 1)
        for h in range(SWA_Q_HEADS):
            g, gi = divmod(h, SWA_GROUP)
            t0 = rows_ref[0, h:h + 1, MOBA_BLOCK:MOBA_BLOCK + L]
            r = _toeplitz(jnp.concatenate([t0, t0], axis=1), L)
            cols = slice(gi * L, (gi + 1) * L)
            bias_ref[g, 0:L, cols] = jnp.where(q_idx < k_idx, r, NEG_INF)
            bias_ref[g, L:2 * L, cols] = jnp.where(q_idx >= k_idx, r, NEG_INF)

    ones_blk = jnp.concatenate([jnp.ones((1, 2 * L), BF16), jnp.zeros((ACC_ROWS - HEAD_DIM - 1, 2 * L), BF16)],
                               axis=0)
    PW = 2 * L
    chains = [(g, hp) for g in range(SWA_KV_HEADS) for hp in range(SWA_GROUP // 2)]
    kv_cols = [slice(g * HEAD_DIM, (g + 1) * HEAD_DIM) for g in range(SWA_KV_HEADS)]

    def q_pair(c, g, hp):
        h0 = g * SWA_GROUP + 2 * hp
        return jnp.concatenate([q_ref[0, c, (h0 + e) * HEAD_DIM:(h0 + e + 1) * HEAD_DIM, :] for e in range(2)],
                               axis=1)

    def scores_first(idx):
        g, hp = chains[idx]
        lanes = slice(hp * PW, (hp + 1) * PW)
        s_ref[idx, 0:L, :] = jnp.full((L, PW), NEG_INF, F32)
        s_ref[idx, L:2 * L, :] = _dot(k_ref[0, 0:L, kv_cols[g]], q_pair(0, g, hp)) + bias_ref[g, L:2 * L, lanes]

    def scores(c, idx):
        g, hp = chains[idx]
        kw = k_ref[0, pl.ds(pl.multiple_of((c - 1) * L, L), 2 * L), kv_cols[g]]
        s_ref[idx] = _dot(kw, q_pair(c, g, hp)) + bias_ref[g, :, hp * PW:(hp + 1) * PW]

    def attend(c, c_prev, idx):
        g, hp = chains[idx]
        hs = kv_cols[g]
        s = s_ref[idx]
        sink = sink_ref[g, :, hp * PW:(hp + 1) * PW]
        m = jnp.maximum(jnp.max(s, axis=0, keepdims=True), sink)
        p = jnp.exp2(s - m).astype(BF16)
        v_win = jnp.concatenate([v_ref[0, c_prev, hs, :], v_ref[0, c, hs, :]], axis=1)
        pv = _dot(jnp.concatenate([v_win, ones_blk], axis=0), p)
        o = pv[0:HEAD_DIM] / (pv[HEAD_DIM:HEAD_DIM + 1] + jnp.exp2(sink - m))
        for e in range(2):
            r = (g * SWA_GROUP + 2 * hp + e) * HEAD_DIM
            ot_ref[r:r + HEAD_DIM, :] = o[:, e * L:(e + 1) * L]

    def emit(c):
        o_ref[0, pl.ds(pl.multiple_of(c * L, L), L), :] = ot_ref[...].T.astype(o_ref.dtype)

    for idx in range(len(chains)):
        scores_first(idx)
    for idx in range(len(chains)):
        attend(0, 0, idx)
        scores(1, idx)
    emit(0)

    def body(c, carry):
        for idx in range(len(chains)):
            attend(c, c - 1, idx)
            scores(c + 1, idx)
        emit(c)
        return carry

    def body_unrolled(u, carry):
        for e in range(SWA_UNROLL):
            carry = body(SWA_UNROLL * u + 1 + e, carry)
        return carry

    lax.fori_loop(0, (nblk - 2) // SWA_UNROLL, body_unrolled, 0)
    for idx in range(len(chains)):
        attend(nblk - 1, nblk - 2, idx)
    emit(nblk - 1)


def _swa(qa_t, ka3, va_t, rows, sink_rows):
    batch, nblk = qa_t.shape[0], qa_t.shape[1]
    seq = nblk * SWA_BLOCK
    assert nblk >= 2 and (nblk - 2) % SWA_UNROLL == 0
    return pl.pallas_call(
        _swa_kernel,
        grid=(batch,),
        in_specs=[
            pl.BlockSpec((1, nblk, SWA_Q_W, SWA_BLOCK), lambda b: (b, 0, 0, 0)),
            pl.BlockSpec((1, seq, SWA_KV_W), lambda b: (b, 0, 0)),
            pl.BlockSpec((1, nblk, SWA_KV_W, SWA_BLOCK), lambda b: (b, 0, 0, 0)),
            _resident((1, N_ATTN_HEADS, 2 * MOBA_BLOCK), lambda b: (0, 0, 0)),
            _resident(sink_rows.shape, lambda b: (0, 0, 0)),
        ],
        out_specs=pl.BlockSpec((1, seq, SWA_Q_W), lambda b: (b, 0, 0)),
        out_shape=jax.ShapeDtypeStruct((batch, seq, SWA_Q_W), BF16),
        scratch_shapes=[pltpu.VMEM((SWA_KV_HEADS, 2 * SWA_BLOCK, SWA_GROUP * SWA_BLOCK), F32),
                        pltpu.VMEM((SWA_Q_W, SWA_BLOCK), F32),
                        pltpu.VMEM((SWA_Q_HEADS // 2, 2 * SWA_BLOCK, 2 * SWA_BLOCK), F32)],
        compiler_params=pltpu.CompilerParams(dimension_semantics=("arbitrary",),
                                             vmem_limit_bytes=VMEM_LIMIT_BYTES),
        name="swa",
    )(qa_t, ka3, va_t, rows, sink_rows)


def _moba_kernel(q_ref, qn_ref, k_ref, v_ref, sel_ref, rows_ref, o_ref, bias_ref, m_ref, acc_ref, s_ref, cm_ref):
    step = pl.program_id(1)
    last = pl.num_programs(1) - 1
    MB = MOBA_BLOCK
    H = MOBA_HEADS
    G = MOBA_Q_PER_STEP

    @pl.when(jnp.logical_and(pl.program_id(0) == 0, step == 0))
    def _():
        for h in range(H):
            for d in range(BIAS_SLOTS):
                bias_ref[h, d] = _toeplitz(rows_ref[d, SWA_Q_HEADS + h:SWA_Q_HEADS + h + 1, :], MB).astype(BF16)

    m_ref[...] = jnp.full(m_ref.shape, NEG_INF, F32)
    acc_ref[...] = jnp.zeros(acc_ref.shape, F32)
    heads = [slice(h * HEAD_DIM, (h + 1) * HEAD_DIM) for h in range(H)]
    ones_blk = jnp.concatenate([jnp.ones((1, MB), BF16), jnp.zeros((ACC_ROWS - HEAD_DIM - 1, MB), BF16)],
                               axis=0)

    def sel_row(j, qs, h):
        return sel_ref[0, h, pl.ds(j, 1), qs * MB:(qs + 1) * MB]

    def scores(j, qs, h):
        c = qs * H + h
        rows = pl.ds(pl.multiple_of(j * MB, MB), MB)
        slot = jnp.minimum(G * step + qs - j, BIAS_SLOTS - 1)
        s = _dot(k_ref[0, rows, heads[h]], q_ref[0, qs, heads[h], :]).astype(BF16) + bias_ref[h, slot]
        s_ref[c] = s
        cm_ref[c] = jnp.max(s, axis=0, keepdims=True).astype(F32) + sel_row(j, qs, h)

    def own_scores(blk, q_blk, c, h):
        rows = pl.ds(pl.multiple_of(blk * MB, MB), MB)
        s = _dot(k_ref[0, rows, heads[h]], q_blk[heads[h], :]).astype(BF16) + bias_ref[h, 0]
        s_ref[c] = s
        cm_ref[c] = jnp.max(s, axis=0, keepdims=True).astype(F32)

    def accumulate(j, qs, h):
        c = qs * H + h
        m_old = m_ref[c]
        m_new = jnp.maximum(m_old, cm_ref[c])
        alpha = jnp.exp2(m_old - m_new)
        p = jnp.exp2(s_ref[c] - (m_new - sel_row(j, qs, h)).astype(BF16))
        v_aug = jnp.concatenate([v_ref[0, j, heads[h], :], ones_blk], axis=0)
        rs = slice(c * ACC_ROWS, (c + 1) * ACC_ROWS)
        acc_ref[rs, :] = alpha * acc_ref[rs, :] + _dot(v_aug, p)
        m_ref[c] = m_new

    first = G * step

    @pl.when(step == 0)
    def _():
        for h in range(H):
            own_scores(G - 1, q_ref[0, G - 1], (G - 1) * H + h, h)

    for n in range(G - 1):
        for qs in range(G):
            for h in range(H):
                if qs >= G - 1 - n:
                    accumulate(first + G - 1 - n, qs, h)
                if qs >= G - 2 - n:
                    scores(first + G - 2 - n, qs, h)

    def body(j, carry):
        for qs in range(G):
            for h in range(H):
                accumulate(j, qs, h)
                scores(j - 1, qs, h)
        return carry

    def body2(u, carry):
        return body(first - 2 * u - 1, body(first - 2 * u, carry))

    lax.fori_loop(0, (G // 2) * step, body2, 0)

    @pl.when(step < last)
    def _():
        for h in range(H):
            for qs in range(G):
                accumulate(0, qs, h)
            own_scores(first + 2 * G - 1, qn_ref[0, 0], (G - 1) * H + h, h)

    @pl.when(step == last)
    def _():
        for h in range(H):
            for qs in range(G):
                accumulate(0, qs, h)

    for qs in range(G):
        outs = []
        for h in range(H):
            base = (qs * H + h) * ACC_ROWS
            outs.append(acc_ref[base:base + HEAD_DIM, :] / acc_ref[base + HEAD_DIM:base + HEAD_DIM + 1, :])
        o_ref[0, qs * MB:(qs + 1) * MB, :] = jnp.concatenate(outs, axis=0).T.astype(o_ref.dtype)


def _moba(qb_t, kb3, vb_t, sel, rows):
    batch, nblk = qb_t.shape[0], qb_t.shape[1]
    seq = nblk * MOBA_BLOCK
    g = MOBA_Q_PER_STEP
    assert g % 2 == 0 and nblk % g == 0
    chains = g * MOBA_HEADS
    return pl.pallas_call(
        _moba_kernel,
        grid=(batch, nblk // g),
        in_specs=[
            pl.BlockSpec((1, g, MOBA_W, MOBA_BLOCK), lambda b, i: (b, i, 0, 0)),
            pl.BlockSpec((1, 1, MOBA_W, MOBA_BLOCK), lambda b, i: (b, jnp.minimum(g * i + 2 * g - 1, nblk - 1), 0, 0)),
            pl.BlockSpec((1, seq, MOBA_W), lambda b, i: (b, 0, 0)),
            pl.BlockSpec((1, nblk, MOBA_W, MOBA_BLOCK), lambda b, i: (b, 0, 0, 0)),
            pl.BlockSpec((1, MOBA_HEADS, nblk, g * MOBA_BLOCK), lambda b, i: (b, 0, 0, i)),
            _resident(rows.shape, lambda b, i: (0, 0, 0)),
        ],
        out_specs=pl.BlockSpec((1, g * MOBA_BLOCK, MOBA_W), lambda b, i: (b, i, 0)),
        out_shape=jax.ShapeDtypeStruct((batch, seq, MOBA_W), BF16),
        scratch_shapes=[pltpu.VMEM((MOBA_HEADS, BIAS_SLOTS, MOBA_BLOCK, MOBA_BLOCK), BF16),
                        pltpu.VMEM((chains, 1, MOBA_BLOCK), F32),
                        pltpu.VMEM((chains * ACC_ROWS, MOBA_BLOCK), F32),
                        pltpu.VMEM((chains, MOBA_BLOCK, MOBA_BLOCK), BF16),
                        pltpu.VMEM((chains, 1, MOBA_BLOCK), F32)],
        compiler_params=pltpu.CompilerParams(dimension_semantics=("arbitrary", "arbitrary"),
                                             vmem_limit_bytes=VMEM_LIMIT_BYTES),
        name="moba",
    )(qb_t, qb_t, kb3, vb_t, sel, rows)


def _post_kernel(x_ref, ya_ref, yb_ref, ga_ref, gb_ref, gate1_ref, shift2_ref, scale2_ref,
                 gate2_ref, g2_ref, gf_ref, wa_ref, wb_ref, wo_ref, w1_ref, w2_ref, o_ref):
    a = _dot(ya_ref[...], wa_ref[...])
    b = _dot(yb_ref[...], wb_ref[...])
    merged = (jax.nn.sigmoid(ga_ref[...].astype(F32)) * a
              + jax.nn.sigmoid(gb_ref[...].astype(F32)) * b)
    x1 = x_ref[...] + gate1_ref[0, 0] * _dot(merged.astype(BF16), wo_ref[...])
    h2 = _rms_modulate(x1, g2_ref[...], shift2_ref[0, 0], scale2_ref[0, 0]).astype(BF16)
    y = jnp.zeros_like(x1)
    for c in range(D_FF // FF_CHUNK):
        cs = slice(c * FF_CHUNK, (c + 1) * FF_CHUNK)
        u = jnp.square(jnp.maximum(_dot(h2, w1_ref[:, cs]), 0.0)).astype(BF16)
        y = y + _dot(u, w2_ref[cs, :])
    x2 = x1 + gate2_ref[0, 0] * y
    ms = jnp.mean(x2 * x2, axis=-1, keepdims=True)
    o_ref[...] = (x2 * lax.rsqrt(ms + RMS_EPS)) * gf_ref[...]


def _post(x2, ya, yb, ga, gb, mod4, g2, gf, wa, wb, wo, w1, w2, seq):
    tokens = x2.shape[0]
    tiles_per_seq = seq // TOKEN_TILE
    tok = lambda w: pl.BlockSpec((TOKEN_TILE, w), lambda i: (i, 0))
    modrow = lambda k: pl.BlockSpec((1, 1, 1, D_MODEL), lambda i: (i // tiles_per_seq, k, 0, 0))
    full = lambda a: _resident(a.shape, lambda i: (0, 0))
    return pl.pallas_call(
        _post_kernel,
        grid=(tokens // TOKEN_TILE,),
        in_specs=[tok(D_MODEL), tok(SWA_Q_W), tok(MOBA_W), tok(D_MODEL), tok(D_MODEL),
                  modrow(2), modrow(3), modrow(4), modrow(5),
                  full(g2), full(gf), full(wa), full(wb), full(wo), full(w1), full(w2)],
        out_specs=tok(D_MODEL),
        out_shape=jax.ShapeDtypeStruct((tokens, D_MODEL), F32),
        compiler_params=pltpu.CompilerParams(dimension_semantics=("arbitrary",),
                                             vmem_limit_bytes=VMEM_LIMIT_BYTES),
        name="post",
    )(x2, ya, yb, ga, gb, mod4, mod4, mod4, mod4, g2, gf, wa, wb, wo, w1, w2)


def kernel(x, c, ada_w, ada_b, norm1_g, norm2_g, w_in, attn_sinks, rel_bias, w_branch_a,
           w_branch_b, w_out, w_mlp_in, w_mlp_out, final_g):
    batch, seq, _ = x.shape
    depth = ada_w.shape[0]
    assert seq % TOKEN_TILE == 0 and TOKEN_TILE % MOBA_BLOCK == 0 and batch <= SUBLANES
    far = np.arange((BIAS_SLOTS - 1) * MOBA_BLOCK - (MOBA_BLOCK - 1), max(seq, BIAS_SLOTS * MOBA_BLOCK))
    assert np.all(_t5_bucket(far) == _t5_bucket(far[:1]))
    assert 2 * SWA_BLOCK <= MOBA_BLOCK and SWA_WINDOW == SWA_BLOCK
    assert depth == 1
    l = 0

    rows = _bias_rows(rel_bias)
    c_t = jnp.zeros((D_MODEL, SUBLANES), F32).at[:, :batch].set(c.astype(F32).T)

    x2 = x.reshape(batch * seq, D_MODEL)
    mod = _ada(c_t, ada_w[l], ada_b[l][None, :], batch)
    mod4 = mod[:batch].reshape(batch, N_MOD, 1, D_MODEL)
    later_weights = [w.astype(F32) for w in (w_branch_a[l], w_branch_b[l], w_out[l], w_mlp_in[l], w_mlp_out[l])]
    ka, kb, ga, gb, qa_t, va_t, qb_t, vb_t, sel, wa, wb, wo, w1, w2 = _proj(
        x2, mod4, norm1_g[l][None, :], w_in[l].astype(F32), later_weights, batch, seq)

    sink_rows = jnp.broadcast_to(
        (attn_sinks[l].astype(F32) * LOG2E).reshape(SWA_KV_HEADS, 1, SWA_GROUP, 1),
        (SWA_KV_HEADS, 1, SWA_GROUP, SWA_BLOCK)).reshape(SWA_KV_HEADS, 1, SWA_GROUP * SWA_BLOCK)
    ya = _swa(qa_t, ka.reshape(batch, seq, SWA_KV_W), va_t, rows, sink_rows)

    yb = _moba(qb_t, kb.reshape(batch, seq, MOBA_W), vb_t, sel, rows)

    out = _post(x2, ya.reshape(batch * seq, SWA_Q_W), yb.reshape(batch * seq, MOBA_W), ga, gb,
                mod4, norm2_g[l][None, :], final_g[None, :], wa, wb, wo, w1, w2, seq)
    return out.reshape(batch, seq, D_MODEL)
```

```python
import functools
import math

import jax
import jax.numpy as jnp
import numpy as np
from jax import lax
from jax.experimental import pallas as pl
from jax.experimental.pallas import tpu as pltpu

D_MODEL = 1024
HEAD_DIM = 64
ATTN_SCALE = HEAD_DIM ** -0.5
LOG2E = math.log2(math.e)
SWA_Q_HEADS = 8
SWA_KV_HEADS = 2
SWA_GROUP = SWA_Q_HEADS // SWA_KV_HEADS
SWA_WINDOW = 128
SWA_BLOCK = 128
MOBA_HEADS = 8
MOBA_BLOCK = 256
MOBA_TOPK = 3
NUM_BUCKETS = 32
MAX_EXACT = NUM_BUCKETS // 2
MAX_DISTANCE = 2048
N_ATTN_HEADS = SWA_Q_HEADS + MOBA_HEADS
SWA_Q_W = SWA_Q_HEADS * HEAD_DIM
SWA_KV_W = SWA_KV_HEADS * HEAD_DIM
MOBA_W = MOBA_HEADS * HEAD_DIM
D_FF = 4 * D_MODEL
N_MOD = 6
RMS_EPS = 1e-6

VMEM_LIMIT_BYTES = 56 * 1024 * 1024
TOKEN_TILE = 512
FF_CHUNK = 1024
SUBLANES = 8
ADA_COLS = 1536
BIAS_SLOTS = 8
SWA_UNROLL = 3
MOBA_Q_PER_STEP = 4
ACC_ROWS = HEAD_DIM + 16
NEG_INF = float("-inf")

F32 = jnp.float32
BF16 = jnp.bfloat16


def _resident(block_shape, index_map):
    return pl.BlockSpec(block_shape, index_map, pipeline_mode=pl.Buffered(1))


def _dot(a, b):
    return jnp.dot(a, b, preferred_element_type=F32)


def _dot_nt(a, b):
    return lax.dot_general(a, b, (((1,), (1,)), ((), ())), preferred_element_type=F32)


def _rms_modulate(xv, g, shift, scale):
    ms = jnp.mean(xv * xv, axis=-1, keepdims=True)
    y = xv * lax.rsqrt(ms + RMS_EPS)
    return (y * g) * (1.0 + scale) + shift


def _toeplitz(u, rows):
    wide = jnp.broadcast_to(u, (rows, 2 * rows))
    return pltpu.roll(wide, 0, 1, stride=1, stride_axis=0)[:, rows:]


def _ada_kernel(ct_ref, w_ref, b_ref, o_ref, *, batch):
    cs_t = jax.nn.silu(ct_ref[...])
    w = w_ref[...]
    rows = [jnp.sum(w * cs_t[:, b:b + 1], axis=0, keepdims=True) for b in range(batch)]
    rows.append(jnp.zeros((SUBLANES - batch, w.shape[1]), F32))
    o_ref[...] = jnp.concatenate(rows, axis=0) + b_ref[...]


def _ada(c_t, w, b, batch):
    n = w.shape[1]
    bn = ADA_COLS
    assert n % bn == 0
    return pl.pallas_call(
        functools.partial(_ada_kernel, batch=batch),
        grid=(n // bn,),
        in_specs=[pl.BlockSpec((D_MODEL, SUBLANES), lambda j: (0, 0)),
                  pl.BlockSpec((D_MODEL, bn), lambda j: (0, j)),
                  pl.BlockSpec((1, bn), lambda j: (0, j))],
        out_specs=pl.BlockSpec((SUBLANES, bn), lambda j: (0, j)),
        out_shape=jax.ShapeDtypeStruct((SUBLANES, n), F32),
        compiler_params=pltpu.CompilerParams(dimension_semantics=("arbitrary",),
                                             vmem_limit_bytes=VMEM_LIMIT_BYTES),
        name="ada",
    )(c_t, w, b)


def _t5_bucket(dist):
    n = np.maximum(dist, 0)
    nf = np.maximum(n, 1).astype(np.float32)
    large = MAX_EXACT + (np.log(nf / np.float32(MAX_EXACT)) / np.float32(math.log(MAX_DISTANCE / MAX_EXACT))
                         * np.float32(NUM_BUCKETS - MAX_EXACT)).astype(np.int32)
    large = np.minimum(large, NUM_BUCKETS - 1)
    return np.where(n < MAX_EXACT, n, large).astype(np.int32)


def _bias_rows_kernel(bucket_ref, rbt_ref, o_ref):
    bucket = bucket_ref[...]
    n = bucket.shape[1]
    onehot = jnp.where(lax.broadcasted_iota(jnp.int32, (NUM_BUCKETS, n), 0) == bucket, 1.0, 0.0)
    rows = jnp.dot(rbt_ref[...], onehot, preferred_element_type=F32,
                   precision=lax.Precision.HIGHEST)
    rows = jnp.where(bucket >= 0, rows * LOG2E, NEG_INF)
    for d in range(BIAS_SLOTS):
        o_ref[d] = rows[:, d * MOBA_BLOCK:(d + 2) * MOBA_BLOCK]


def _bias_rows(rel_bias):
    n = (BIAS_SLOTS + 1) * MOBA_BLOCK
    dist = np.arange(n, dtype=np.int32) - MOBA_BLOCK
    bucket = jnp.asarray(np.where(dist >= 0, _t5_bucket(dist), -1)[None, :], dtype=jnp.int32)
    return pl.pallas_call(
        _bias_rows_kernel,
        out_shape=jax.ShapeDtypeStruct((BIAS_SLOTS, N_ATTN_HEADS, 2 * MOBA_BLOCK), F32),
        name="bias_rows",
    )(bucket, rel_bias.astype(F32).T)


IN_WIDTHS = (SWA_Q_W, SWA_KV_W, SWA_KV_W, MOBA_W, MOBA_W, MOBA_W, D_MODEL, D_MODEL)
IN_OFFS = tuple(sum(IN_WIDTHS[:k]) for k in range(len(IN_WIDTHS) + 1))
FEAT_COLS = (0, 2, 3, 5)
FEAT_W = sum(IN_WIDTHS[k] for k in FEAT_COLS)
TOK_COLS = (1, 4, 6, 7)
TOK_W = sum(IN_WIDTHS[k] for k in TOK_COLS)
Q_COLS = (0, 3)
BLOCKS_PER_TILE = TOKEN_TILE // MOBA_BLOCK


def _proj_kernel(x_ref, shift_ref, scale_ref, g_ref, w_ref, *refs):
    n_cast = (len(refs) - 12) // 2
    cast_in, refs = refs[:n_cast], refs[n_cast:]
    ka_ref, kb_ref, ga_ref, gb_ref, qa_ref, va_ref, qb_ref, vb_ref, sel_ref = refs[:9]
    cast_out, (wk_ref, wt_ref, kmean_ref) = refs[9:9 + n_cast], refs[9 + n_cast:]

    for src, dst in zip(cast_in, cast_out):
        dst[...] = src[...].astype(dst.dtype)

    tile = pl.program_id(0) % (kmean_ref.shape[0] // BLOCKS_PER_TILE)

    @pl.when(tile == 0)
    def _():
        kmean_ref[...] = jnp.zeros(kmean_ref.shape, F32)

    @pl.when(pl.program_id(0) == 0)
    def _():
        o = 0
        for k in TOK_COLS:
            wk_ref[:, o:o + IN_WIDTHS[k]] = w_ref[:, IN_OFFS[k]:IN_OFFS[k + 1]].astype(BF16)
            o += IN_WIDTHS[k]
        o = 0
        for k in FEAT_COLS:
            wcol = w_ref[:, IN_OFFS[k]:IN_OFFS[k + 1]]
            if k in Q_COLS:
                wcol = wcol * (ATTN_SCALE * LOG2E)
            wt_ref[o:o + IN_WIDTHS[k], :] = wcol.T.astype(BF16)
            o += IN_WIDTHS[k]

    h = _rms_modulate(x_ref[...], g_ref[...], shift_ref[0, 0], scale_ref[0, 0]).astype(BF16)
    tok_refs = dict(zip(TOK_COLS, (ka_ref, kb_ref, ga_ref, gb_ref)))
    feat_refs = dict(zip(FEAT_COLS, (qa_ref, va_ref, qb_ref, vb_ref)))

    def token_major(k):
        o = sum(IN_WIDTHS[c] for c in TOK_COLS[:TOK_COLS.index(k)])
        out = _dot(h, wk_ref[:, o:o + IN_WIDTHS[k]])
        tok_refs[k][...] = out.astype(BF16)
        return out

    def feature_major(k):
        o = sum(IN_WIDTHS[c] for c in FEAT_COLS[:FEAT_COLS.index(k)])
        out = _dot_nt(wt_ref[o:o + IN_WIDTHS[k], :], h).astype(BF16)
        blk = feat_refs[k].shape[-1]
        for t in range(TOKEN_TILE // blk):
            feat_refs[k][0, t] = out[:, t * blk:(t + 1) * blk]
        return out

    _update_kmean(token_major(4), kmean_ref, tile)
    qb_t = feature_major(3)
    token_major(1)
    token_major(6)
    _moba_select(qb_t, sel_ref, kmean_ref, tile)
    token_major(7)
    for k in (0, 2, 5):
        feature_major(k)


def _update_kmean(kb, kmean_ref, tile):
    for t in range(BLOCKS_PER_TILE):
        kmean_ref[pl.ds(tile * BLOCKS_PER_TILE + t, 1), :] = jnp.mean(
            kb[t * MOBA_BLOCK:(t + 1) * MOBA_BLOCK, :], axis=0, keepdims=True)


def _moba_select(qb_t, sel_ref, kmean_ref, tile):
    nblk = kmean_ref.shape[0]
    blocks_per_tile = BLOCKS_PER_TILE
    blk = lax.broadcasted_iota(jnp.int32, (nblk, TOKEN_TILE), 0)
    qblk = tile * blocks_per_tile + lax.broadcasted_iota(jnp.int32, (nblk, TOKEN_TILE), 1) // MOBA_BLOCK
    past = blk < qblk
    own = jnp.where(blk == qblk, 0.0, NEG_INF)
    blk_f = blk.astype(F32)
    for h in range(MOBA_HEADS):
        hs = slice(h * HEAD_DIM, (h + 1) * HEAD_DIM)
        kmean = kmean_ref[:, hs]
        k1 = kmean.astype(BF16)
        r1 = kmean - k1.astype(F32)
        k2 = r1.astype(BF16)
        k3 = (r1 - k2.astype(F32)).astype(BF16)
        parts = _dot(jnp.concatenate([k1, k2, k3], axis=0), qb_t[hs, :])
        gate = parts[0:nblk] + parts[nblk:2 * nblk] + parts[2 * nblk:3 * nblk]
        gate = jnp.where(past, gate, NEG_INF)
        mask = own
        for _ in range(MOBA_TOPK):
            top = jnp.max(gate, axis=0, keepdims=True)
            first = jnp.min(jnp.where(gate == top, blk_f, float(nblk)), axis=0, keepdims=True)
            pick = blk_f == first
            mask = jnp.where(jnp.logical_and(pick, past), 0.0, mask)
            gate = jnp.where(pick, NEG_INF, gate)
        sel_ref[0, h] = mask


def _proj(x2, mod4, g1, w_in, cast_weights, batch, seq):
    tokens = x2.shape[0]
    steps = tokens // TOKEN_TILE
    assert all(w.shape[0] % (16 * steps) == 0 for w in cast_weights)
    cast_specs = [pl.BlockSpec((w.shape[0] // steps, w.shape[1]), lambda i: (i, 0)) for w in cast_weights]
    tiles_per_seq = seq // TOKEN_TILE
    na = TOKEN_TILE // SWA_BLOCK
    nb = TOKEN_TILE // MOBA_BLOCK

    def feat_spec(width, blk, per_tile):
        return pl.BlockSpec((1, per_tile, width, blk),
                            lambda i: (i // tiles_per_seq, i % tiles_per_seq, 0, 0))

    out_shape = (
        jax.ShapeDtypeStruct((tokens, SWA_KV_W), BF16),
        jax.ShapeDtypeStruct((tokens, MOBA_W), BF16),
        jax.ShapeDtypeStruct((tokens, D_MODEL), BF16),
        jax.ShapeDtypeStruct((tokens, D_MODEL), BF16),
        jax.ShapeDtypeStruct((batch, seq // SWA_BLOCK, SWA_Q_W, SWA_BLOCK), BF16),
        jax.ShapeDtypeStruct((batch, seq // SWA_BLOCK, SWA_KV_W, SWA_BLOCK), BF16),
        jax.ShapeDtypeStruct((batch, seq // MOBA_BLOCK, MOBA_W, MOBA_BLOCK), BF16),
        jax.ShapeDtypeStruct((batch, seq // MOBA_BLOCK, MOBA_W, MOBA_BLOCK), BF16),
        jax.ShapeDtypeStruct((batch, MOBA_HEADS, seq // MOBA_BLOCK, seq), F32),
    )
    tok_spec = lambda w: pl.BlockSpec((TOKEN_TILE, w), lambda i: (i, 0))
    return pl.pallas_call(
        _proj_kernel,
        grid=(steps,),
        in_specs=[
            pl.BlockSpec((TOKEN_TILE, D_MODEL), lambda i: (i, 0)),
            pl.BlockSpec((1, 1, 1, D_MODEL), lambda i: (i // tiles_per_seq, 0, 0, 0)),
            pl.BlockSpec((1, 1, 1, D_MODEL), lambda i: (i // tiles_per_seq, 1, 0, 0)),
            _resident((1, D_MODEL), lambda i: (0, 0)),
            _resident(w_in.shape, lambda i: (0, 0)),
        ] + cast_specs,
        out_specs=(tok_spec(SWA_KV_W), tok_spec(MOBA_W), tok_spec(D_MODEL), tok_spec(D_MODEL),
                   feat_spec(SWA_Q_W, SWA_BLOCK, na), feat_spec(SWA_KV_W, SWA_BLOCK, na),
                   feat_spec(MOBA_W, MOBA_BLOCK, nb), feat_spec(MOBA_W, MOBA_BLOCK, nb),
                   pl.BlockSpec((1, MOBA_HEADS, seq // MOBA_BLOCK, TOKEN_TILE),
                                lambda i: (i // tiles_per_seq, 0, 0, i % tiles_per_seq)))
        + tuple(cast_specs),
        out_shape=out_shape + tuple(jax.ShapeDtypeStruct(w.shape, BF16) for w in cast_weights),
        scratch_shapes=[pltpu.VMEM((D_MODEL, TOK_W), BF16), pltpu.VMEM((FEAT_W, D_MODEL), BF16),
                        pltpu.VMEM((seq // MOBA_BLOCK, MOBA_W), F32)],
        compiler_params=pltpu.CompilerParams(dimension_semantics=("arbitrary",),
                                             vmem_limit_bytes=VMEM_LIMIT_BYTES),
        name="proj",
    )(x2, mod4, mod4, g1, w_in, *cast_weights)


def _swa_kernel(q_ref, k_ref, v_ref, rows_ref, sink_ref, o_ref, bias_ref, ot_ref, s_ref, cm_ref):
    L = SWA_BLOCK
    nblk = q_ref.shape[1]

    @pl.when(pl.program_id(0) == 0)
    def _():
        k_idx = lax.broadcasted_iota(jnp.int32, (L, L), 0)
        q_idx = lax.broadcasted_iota(jnp.int32, (L, L), 1)
        for h in range(SWA_Q_HEADS):
            g, gi = divmod(h, SWA_GROUP)
            t0 = rows_ref[0, h:h + 1, MOBA_BLOCK:MOBA_BLOCK + L]
            r = _toeplitz(jnp.concatenate([t0, t0], axis=1), L)
            cols = slice(gi * L, (gi + 1) * L)
            bias_ref[g, 0:L, cols] = jnp.where(q_idx < k_idx, r, NEG_INF).astype(BF16)
            bias_ref[g, L:2 * L, cols] = jnp.where(q_idx >= k_idx, r, NEG_INF).astype(BF16)

    ones_blk = jnp.concatenate([jnp.ones((1, 2 * L), BF16), jnp.zeros((ACC_ROWS - HEAD_DIM - 1, 2 * L), BF16)],
                               axis=0)
    PW = 2 * L
    chains = [(g, hp) for g in range(SWA_KV_HEADS) for hp in range(SWA_GROUP // 2)]
    kv_cols = [slice(g * HEAD_DIM, (g + 1) * HEAD_DIM) for g in range(SWA_KV_HEADS)]

    def q_pair(c, g, hp):
        h0 = g * SWA_GROUP + 2 * hp
        return jnp.concatenate([q_ref[0, c, (h0 + e) * HEAD_DIM:(h0 + e + 1) * HEAD_DIM, :] for e in range(2)],
                               axis=1)

    def scores_first(idx):
        g, hp = chains[idx]
        lanes = slice(hp * PW, (hp + 1) * PW)
        s_ref[idx, 0:L, :] = jnp.full((L, PW), NEG_INF, BF16)
        s = _dot(k_ref[0, 0:L, kv_cols[g]], q_pair(0, g, hp)).astype(BF16) + bias_ref[g, L:2 * L, lanes]
        s_ref[idx, L:2 * L, :] = s
        cm_ref[idx] = jnp.max(s, axis=0, keepdims=True).astype(F32)

    def scores(c, idx):
        g, hp = chains[idx]
        kw = k_ref[0, pl.ds(pl.multiple_of((c - 1) * L, L), 2 * L), kv_cols[g]]
        s = _dot(kw, q_pair(c, g, hp)).astype(BF16) + bias_ref[g, :, hp * PW:(hp + 1) * PW]
        s_ref[idx] = s
        cm_ref[idx] = jnp.max(s, axis=0, keepdims=True).astype(F32)

    def attend(c, c_prev, idx, slot):
        g, hp = chains[idx]
        hs = kv_cols[g]
        s = s_ref[idx]
        sink = sink_ref[g, :, hp * PW:(hp + 1) * PW]
        m_b = jnp.maximum(cm_ref[idx], sink).astype(BF16)
        m = m_b.astype(F32)
        p = jnp.exp2(s - m_b)
        v_win = jnp.concatenate([v_ref[0, c_prev, hs, :], v_ref[0, c, hs, :]], axis=1)
        pv = _dot(jnp.concatenate([v_win, ones_blk], axis=0), p)
        o = pv[0:HEAD_DIM] / (pv[HEAD_DIM:HEAD_DIM + 1] + jnp.exp2(sink - m))
        for e in range(2):
            r = (g * SWA_GROUP + 2 * hp + e) * HEAD_DIM
            ot_ref[slot, r:r + HEAD_DIM, :] = o[:, e * L:(e + 1) * L]

    def emit(c, slot):
        o_ref[0, pl.ds(pl.multiple_of(c * L, L), L), :] = ot_ref[slot].T.astype(o_ref.dtype)

    for idx in range(len(chains)):
        scores_first(idx)
    U = SWA_UNROLL
    for idx in range(len(chains)):
        attend(0, 0, idx, U - 1)
        scores(1, idx)

    def body(c, e):
        emit(c - 1, (e - 1) % U)
        for idx in range(len(chains)):
            attend(c, c - 1, idx, e)
            scores(c + 1, idx)

    def body_unrolled(u, carry):
        for e in range(U):
            body(U * u + 1 + e, e)
        return carry

    lax.fori_loop(0, (nblk - 2) // U, body_unrolled, 0)
    emit(nblk - 2, U - 1)
    for idx in range(len(chains)):
        attend(nblk - 1, nblk - 2, idx, 0)
    emit(nblk - 1, 0)


def _swa(qa_t, ka3, va_t, rows, sink_rows):
    batch, nblk = qa_t.shape[0], qa_t.shape[1]
    seq = nblk * SWA_BLOCK
    assert nblk >= 2 and (nblk - 2) % SWA_UNROLL == 0 and SWA_UNROLL >= 2
    return pl.pallas_call(
        _swa_kernel,
        grid=(batch,),
        in_specs=[
            pl.BlockSpec((1, nblk, SWA_Q_W, SWA_BLOCK), lambda b: (b, 0, 0, 0)),
            pl.BlockSpec((1, seq, SWA_KV_W), lambda b: (b, 0, 0)),
            pl.BlockSpec((1, nblk, SWA_KV_W, SWA_BLOCK), lambda b: (b, 0, 0, 0)),
            _resident((1, N_ATTN_HEADS, 2 * MOBA_BLOCK), lambda b: (0, 0, 0)),
            _resident(sink_rows.shape, lambda b: (0, 0, 0)),
        ],
        out_specs=pl.BlockSpec((1, seq, SWA_Q_W), lambda b: (b, 0, 0)),
        out_shape=jax.ShapeDtypeStruct((batch, seq, SWA_Q_W), BF16),
        scratch_shapes=[pltpu.VMEM((SWA_KV_HEADS, 2 * SWA_BLOCK, SWA_GROUP * SWA_BLOCK), BF16),
                        pltpu.VMEM((SWA_UNROLL, SWA_Q_W, SWA_BLOCK), F32),
                        pltpu.VMEM((SWA_Q_HEADS // 2, 2 * SWA_BLOCK, 2 * SWA_BLOCK), BF16),
                        pltpu.VMEM((SWA_Q_HEADS // 2, 1, 2 * SWA_BLOCK), F32)],
        compiler_params=pltpu.CompilerParams(dimension_semantics=("arbitrary",),
                                             vmem_limit_bytes=VMEM_LIMIT_BYTES),
        name="swa",
    )(qa_t, ka3, va_t, rows, sink_rows)


def _moba_kernel(q_ref, qn_ref, k_ref, v_ref, sel_ref, rows_ref, o_ref, bias_ref, m_ref, acc_ref, s_ref, cm_ref):
    step = pl.program_id(1)
    last = pl.num_programs(1) - 1
    MB = MOBA_BLOCK
    H = MOBA_HEADS
    G = MOBA_Q_PER_STEP

    @pl.when(jnp.logical_and(pl.program_id(0) == 0, step == 0))
    def _():
        for h in range(H):
            for d in range(BIAS_SLOTS):
                bias_ref[h, d] = _toeplitz(rows_ref[d, SWA_Q_HEADS + h:SWA_Q_HEADS + h + 1, :], MB).astype(BF16)

    m_ref[...] = jnp.full(m_ref.shape, NEG_INF, F32)
    acc_ref[...] = jnp.zeros(acc_ref.shape, F32)
    heads = [slice(h * HEAD_DIM, (h + 1) * HEAD_DIM) for h in range(H)]
    ones_blk = jnp.concatenate([jnp.ones((1, MB), BF16), jnp.zeros((ACC_ROWS - HEAD_DIM - 1, MB), BF16)],
                               axis=0)

    def sel_row(j, qs, h):
        return sel_ref[0, h, pl.ds(j, 1), qs * MB:(qs + 1) * MB]

    def scores(j, qs, h):
        c = qs * H + h
        rows = pl.ds(pl.multiple_of(j * MB, MB), MB)
        slot = jnp.minimum(G * step + qs - j, BIAS_SLOTS - 1)
        s = _dot(k_ref[0, rows, heads[h]], q_ref[0, qs, heads[h], :]).astype(BF16) + bias_ref[h, slot]
        s_ref[c] = s
        cm_ref[c] = jnp.max(s, axis=0, keepdims=True).astype(F32) + sel_row(j, qs, h)

    def own_scores(blk, q_blk, c, h):
        rows = pl.ds(pl.multiple_of(blk * MB, MB), MB)
        s = _dot(k_ref[0, rows, heads[h]], q_blk[heads[h], :]).astype(BF16) + bias_ref[h, 0]
        s_ref[c] = s
        cm_ref[c] = jnp.max(s, axis=0, keepdims=True).astype(F32)

    def accumulate(j, qs, h):
        c = qs * H + h
        m_old = m_ref[c]
        m_new = jnp.maximum(m_old, cm_ref[c])
        alpha = jnp.exp2(m_old - m_new)
        p = jnp.exp2(s_ref[c] - (m_new - sel_row(j, qs, h)).astype(BF16))
        v_aug = jnp.concatenate([v_ref[0, j, heads[h], :], ones_blk], axis=0)
        rs = slice(c * ACC_ROWS, (c + 1) * ACC_ROWS)
        acc_ref[rs, :] = alpha * acc_ref[rs, :] + _dot(v_aug, p)
        m_ref[c] = m_new

    first = G * step

    @pl.when(step == 0)
    def _():
        for h in range(H):
            own_scores(G - 1, q_ref[0, G - 1], (G - 1) * H + h, h)

    for n in range(G - 1):
        for qs in range(G):
            for h in range(H):
                if qs >= G - 1 - n:
                    accumulate(first + G - 1 - n, qs, h)
                if qs >= G - 2 - n:
                    scores(first + G - 2 - n, qs, h)

    def body(j, carry):
        for qs in range(G):
            for h in range(H):
                accumulate(j, qs, h)
                scores(j - 1, qs, h)
        return carry

    def body2(u, carry):
        return body(first - 2 * u - 1, body(first - 2 * u, carry))

    lax.fori_loop(0, (G // 2) * step, body2, 0)

    @pl.when(step < last)
    def _():
        for h in range(H):
            for qs in range(G):
                accumulate(0, qs, h)
            own_scores(first + 2 * G - 1, qn_ref[0, 0], (G - 1) * H + h, h)

    @pl.when(step == last)
    def _():
        for h in range(H):
            for qs in range(G):
                accumulate(0, qs, h)

    for qs in range(G):
        outs = []
        for h in range(H):
            base = (qs * H + h) * ACC_ROWS
            outs.append(acc_ref[base:base + HEAD_DIM, :] / acc_ref[base + HEAD_DIM:base + HEAD_DIM + 1, :])
        o_ref[0, qs * MB:(qs + 1) * MB, :] = jnp.concatenate(outs, axis=0).T.astype(o_ref.dtype)


def _moba(qb_t, kb3, vb_t, sel, rows):
    batch, nblk = qb_t.shape[0], qb_t.shape[1]
    seq = nblk * MOBA_BLOCK
    g = MOBA_Q_PER_STEP
    assert g % 2 == 0 and nblk % g == 0
    chains = g * MOBA_HEADS
    return pl.pallas_call(
        _moba_kernel,
        grid=(batch, nblk // g),
        in_specs=[
            pl.BlockSpec((1, g, MOBA_W, MOBA_BLOCK), lambda b, i: (b, i, 0, 0)),
            pl.BlockSpec((1, 1, MOBA_W, MOBA_BLOCK), lambda b, i: (b, jnp.minimum(g * i + 2 * g - 1, nblk - 1), 0, 0)),
            pl.BlockSpec((1, seq, MOBA_W), lambda b, i: (b, 0, 0)),
            pl.BlockSpec((1, nblk, MOBA_W, MOBA_BLOCK), lambda b, i: (b, 0, 0, 0)),
            pl.BlockSpec((1, MOBA_HEADS, nblk, g * MOBA_BLOCK), lambda b, i: (b, 0, 0, i)),
            _resident(rows.shape, lambda b, i: (0, 0, 0)),
        ],
        out_specs=pl.BlockSpec((1, g * MOBA_BLOCK, MOBA_W), lambda b, i: (b, i, 0)),
        out_shape=jax.ShapeDtypeStruct((batch, seq, MOBA_W), BF16),
        scratch_shapes=[pltpu.VMEM((MOBA_HEADS, BIAS_SLOTS, MOBA_BLOCK, MOBA_BLOCK), BF16),
                        pltpu.VMEM((chains, 1, MOBA_BLOCK), F32),
                        pltpu.VMEM((chains * ACC_ROWS, MOBA_BLOCK), F32),
                        pltpu.VMEM((chains, MOBA_BLOCK, MOBA_BLOCK), BF16),
                        pltpu.VMEM((chains, 1, MOBA_BLOCK), F32)],
        compiler_params=pltpu.CompilerParams(dimension_semantics=("arbitrary", "arbitrary"),
                                             vmem_limit_bytes=VMEM_LIMIT_BYTES),
        name="moba",
    )(qb_t, qb_t, kb3, vb_t, sel, rows)


def _post_kernel(x_ref, ya_ref, yb_ref, ga_ref, gb_ref, gate1_ref, shift2_ref, scale2_ref,
                 gate2_ref, g2_ref, gf_ref, wa_ref, wb_ref, wo_ref, w1_ref, w2_ref, o_ref):
    a = _dot(ya_ref[...], wa_ref[...])
    b = _dot(yb_ref[...], wb_ref[...])
    merged = (jax.nn.sigmoid(ga_ref[...].astype(F32)) * a
              + jax.nn.sigmoid(gb_ref[...].astype(F32)) * b)
    x1 = x_ref[...] + gate1_ref[0, 0] * _dot(merged.astype(BF16), wo_ref[...])
    h2 = _rms_modulate(x1, g2_ref[...], shift2_ref[0, 0], scale2_ref[0, 0]).astype(BF16)
    y = jnp.zeros_like(x1)
    for c in range(D_FF // FF_CHUNK):
        cs = slice(c * FF_CHUNK, (c + 1) * FF_CHUNK)
        u = jnp.square(jnp.maximum(_dot(h2, w1_ref[:, cs]), 0.0)).astype(BF16)
        y = y + _dot(u, w2_ref[cs, :])
    x2 = x1 + gate2_ref[0, 0] * y
    ms = jnp.mean(x2 * x2, axis=-1, keepdims=True)
    o_ref[...] = (x2 * lax.rsqrt(ms + RMS_EPS)) * gf_ref[...]


def _post(x2, ya, yb, ga, gb, mod4, g2, gf, wa, wb, wo, w1, w2, seq):
    tokens = x2.shape[0]
    tiles_per_seq = seq // TOKEN_TILE
    tok = lambda w: pl.BlockSpec((TOKEN_TILE, w), lambda i: (i, 0))
    modrow = lambda k: pl.BlockSpec((1, 1, 1, D_MODEL), lambda i: (i // tiles_per_seq, k, 0, 0))
    full = lambda a: _resident(a.shape, lambda i: (0, 0))
    return pl.pallas_call(
        _post_kernel,
        grid=(tokens // TOKEN_TILE,),
        in_specs=[tok(D_MODEL), tok(SWA_Q_W), tok(MOBA_W), tok(D_MODEL), tok(D_MODEL),
                  modrow(2), modrow(3), modrow(4), modrow(5),
                  full(g2), full(gf), full(wa), full(wb), full(wo), full(w1), full(w2)],
        out_specs=tok(D_MODEL),
        out_shape=jax.ShapeDtypeStruct((tokens, D_MODEL), F32),
        compiler_params=pltpu.CompilerParams(dimension_semantics=("arbitrary",),
                                             vmem_limit_bytes=VMEM_LIMIT_BYTES),
        name="post",
    )(x2, ya, yb, ga, gb, mod4, mod4, mod4, mod4, g2, gf, wa, wb, wo, w1, w2)


def kernel(x, c, ada_w, ada_b, norm1_g, norm2_g, w_in, attn_sinks, rel_bias, w_branch_a,
           w_branch_b, w_out, w_mlp_in, w_mlp_out, final_g):
    batch, seq, _ = x.shape
    depth = ada_w.shape[0]
    assert seq % TOKEN_TILE == 0 and TOKEN_TILE % MOBA_BLOCK == 0 and batch <= SUBLANES
    far = np.arange((BIAS_SLOTS - 1) * MOBA_BLOCK - (MOBA_BLOCK - 1), max(seq, BIAS_SLOTS * MOBA_BLOCK))
    assert np.all(_t5_bucket(far) == _t5_bucket(far[:1]))
    assert 2 * SWA_BLOCK <= MOBA_BLOCK and SWA_WINDOW == SWA_BLOCK
    assert depth == 1
    l = 0

    rows = _bias_rows(rel_bias)
    c_t = jnp.zeros((D_MODEL, SUBLANES), F32).at[:, :batch].set(c.astype(F32).T)

    x2 = x.reshape(batch * seq, D_MODEL)
    mod = _ada(c_t, ada_w[l], ada_b[l][None, :], batch)
    mod4 = mod[:batch].reshape(batch, N_MOD, 1, D_MODEL)
    later_weights = [w.astype(F32) for w in (w_branch_a[l], w_branch_b[l], w_out[l], w_mlp_in[l], w_mlp_out[l])]
    ka, kb, ga, gb, qa_t, va_t, qb_t, vb_t, sel, wa, wb, wo, w1, w2 = _proj(
        x2, mod4, norm1_g[l][None, :], w_in[l].astype(F32), later_weights, batch, seq)

    sink_rows = jnp.broadcast_to(
        (attn_sinks[l].astype(F32) * LOG2E).reshape(SWA_KV_HEADS, 1, SWA_GROUP, 1),
        (SWA_KV_HEADS, 1, SWA_GROUP, SWA_BLOCK)).reshape(SWA_KV_HEADS, 1, SWA_GROUP * SWA_BLOCK)
    ya = _swa(qa_t, ka.reshape(batch, seq, SWA_KV_W), va_t, rows, sink_rows)

    yb = _moba(qb_t, kb.reshape(batch, seq, MOBA_W), vb_t, sel, rows)

    out = _post(x2, ya.reshape(batch * seq, SWA_Q_W), yb.reshape(batch * seq, MOBA_W), ga, gb,
                mod4, norm2_g[l][None, :], final_g[None, :], wa, wb, wo, w1, w2, seq)
    return out.reshape(batch, seq, D_MODEL)
```

```python
import functools
import math

import jax
import jax.numpy as jnp
import numpy as np
from jax import lax
from jax.experimental import pallas as pl
from jax.experimental.pallas import tpu as pltpu

D_MODEL = 1024
HEAD_DIM = 64
ATTN_SCALE = HEAD_DIM ** -0.5
LOG2E = math.log2(math.e)
SWA_Q_HEADS = 8
SWA_KV_HEADS = 2
SWA_GROUP = SWA_Q_HEADS // SWA_KV_HEADS
SWA_WINDOW = 128
SWA_BLOCK = 128
MOBA_HEADS = 8
MOBA_BLOCK = 256
MOBA_TOPK = 3
NUM_BUCKETS = 32
MAX_EXACT = NUM_BUCKETS // 2
MAX_DISTANCE = 2048
N_ATTN_HEADS = SWA_Q_HEADS + MOBA_HEADS
SWA_Q_W = SWA_Q_HEADS * HEAD_DIM
SWA_KV_W = SWA_KV_HEADS * HEAD_DIM
MOBA_W = MOBA_HEADS * HEAD_DIM
D_FF = 4 * D_MODEL
N_MOD = 6
RMS_EPS = 1e-6

VMEM_LIMIT_BYTES = 56 * 1024 * 1024
TOKEN_TILE = 512
FF_CHUNK = 1024
SUBLANES = 8
ADA_COLS = 1536
BIAS_SLOTS = 8
SWA_UNROLL = 3
MOBA_Q_PER_STEP = 4
ACC_ROWS = HEAD_DIM + 16
NEG_INF = float("-inf")

F32 = jnp.float32
BF16 = jnp.bfloat16


def _resident(block_shape, index_map):
    return pl.BlockSpec(block_shape, index_map, pipeline_mode=pl.Buffered(1))


def _dot(a, b):
    return jnp.dot(a, b, preferred_element_type=F32)


def _dot_nt(a, b):
    return lax.dot_general(a, b, (((1,), (1,)), ((), ())), preferred_element_type=F32)


def _rms_modulate(xv, g, shift, scale):
    ms = jnp.mean(xv * xv, axis=-1, keepdims=True)
    y = xv * lax.rsqrt(ms + RMS_EPS)
    return (y * g) * (1.0 + scale) + shift


def _toeplitz(u, rows):
    wide = jnp.broadcast_to(u, (rows, 2 * rows))
    return pltpu.roll(wide, 0, 1, stride=1, stride_axis=0)[:, rows:]


def _ada_kernel(ct_ref, w_ref, b_ref, o_ref, *, batch):
    cs_t = jax.nn.silu(ct_ref[...])
    w = w_ref[...]
    rows = [jnp.sum(w * cs_t[:, b:b + 1], axis=0, keepdims=True) for b in range(batch)]
    rows.append(jnp.zeros((SUBLANES - batch, w.shape[1]), F32))
    o_ref[...] = jnp.concatenate(rows, axis=0) + b_ref[...]


def _ada(c_t, w, b, batch):
    n = w.shape[1]
    bn = ADA_COLS
    assert n % bn == 0
    return pl.pallas_call(
        functools.partial(_ada_kernel, batch=batch),
        grid=(n // bn,),
        in_specs=[pl.BlockSpec((D_MODEL, SUBLANES), lambda j: (0, 0)),
                  pl.BlockSpec((D_MODEL, bn), lambda j: (0, j)),
                  pl.BlockSpec((1, bn), lambda j: (0, j))],
        out_specs=pl.BlockSpec((SUBLANES, bn), lambda j: (0, j)),
        out_shape=jax.ShapeDtypeStruct((SUBLANES, n), F32),
        compiler_params=pltpu.CompilerParams(dimension_semantics=("arbitrary",),
                                             vmem_limit_bytes=VMEM_LIMIT_BYTES),
        name="ada",
    )(c_t, w, b)


def _t5_bucket(dist):
    n = np.maximum(dist, 0)
    nf = np.maximum(n, 1).astype(np.float32)
    large = MAX_EXACT + (np.log(nf / np.float32(MAX_EXACT)) / np.float32(math.log(MAX_DISTANCE / MAX_EXACT))
                         * np.float32(NUM_BUCKETS - MAX_EXACT)).astype(np.int32)
    large = np.minimum(large, NUM_BUCKETS - 1)
    return np.where(n < MAX_EXACT, n, large).astype(np.int32)


def _bias_rows_kernel(bucket_ref, rbt_ref, o_ref):
    bucket = bucket_ref[...]
    n = bucket.shape[1]
    onehot = jnp.where(lax.broadcasted_iota(jnp.int32, (NUM_BUCKETS, n), 0) == bucket, 1.0, 0.0)
    rows = jnp.dot(rbt_ref[...], onehot, preferred_element_type=F32,
                   precision=lax.Precision.HIGHEST)
    rows = jnp.where(bucket >= 0, rows * LOG2E, NEG_INF)
    for d in range(BIAS_SLOTS):
        o_ref[d] = rows[:, d * MOBA_BLOCK:(d + 2) * MOBA_BLOCK]


def _bias_rows(rel_bias):
    n = (BIAS_SLOTS + 1) * MOBA_BLOCK
    dist = np.arange(n, dtype=np.int32) - MOBA_BLOCK
    bucket = jnp.asarray(np.where(dist >= 0, _t5_bucket(dist), -1)[None, :], dtype=jnp.int32)
    return pl.pallas_call(
        _bias_rows_kernel,
        out_shape=jax.ShapeDtypeStruct((BIAS_SLOTS, N_ATTN_HEADS, 2 * MOBA_BLOCK), F32),
        name="bias_rows",
    )(bucket, rel_bias.astype(F32).T)


IN_WIDTHS = (SWA_Q_W, SWA_KV_W, SWA_KV_W, MOBA_W, MOBA_W, MOBA_W, D_MODEL, D_MODEL)
IN_OFFS = tuple(sum(IN_WIDTHS[:k]) for k in range(len(IN_WIDTHS) + 1))
FEAT_COLS = (0, 2, 3, 5)
FEAT_W = sum(IN_WIDTHS[k] for k in FEAT_COLS)
TOK_COLS = (1, 4, 6, 7)
TOK_W = sum(IN_WIDTHS[k] for k in TOK_COLS)
Q_COLS = (0, 3)
BLOCKS_PER_TILE = TOKEN_TILE // MOBA_BLOCK


def _proj_kernel(x_ref, shift_ref, scale_ref, g_ref, w_ref, *refs):
    n_cast = (len(refs) - 12) // 2
    cast_in, refs = refs[:n_cast], refs[n_cast:]
    ka_ref, kb_ref, ga_ref, gb_ref, qa_ref, va_ref, qb_ref, vb_ref, sel_ref = refs[:9]
    cast_out, (wk_ref, wt_ref, kmean_ref) = refs[9:9 + n_cast], refs[9 + n_cast:]

    for src, dst in zip(cast_in, cast_out):
        dst[...] = src[...].astype(dst.dtype)

    tile = pl.program_id(0) % (kmean_ref.shape[0] // BLOCKS_PER_TILE)

    @pl.when(tile == 0)
    def _():
        kmean_ref[...] = jnp.zeros(kmean_ref.shape, F32)

    @pl.when(pl.program_id(0) == 0)
    def _():
        o = 0
        for k in TOK_COLS:
            wk_ref[:, o:o + IN_WIDTHS[k]] = w_ref[:, IN_OFFS[k]:IN_OFFS[k + 1]].astype(BF16)
            o += IN_WIDTHS[k]
        o = 0
        for k in FEAT_COLS:
            wcol = w_ref[:, IN_OFFS[k]:IN_OFFS[k + 1]]
            if k in Q_COLS:
                wcol = wcol * (ATTN_SCALE * LOG2E)
            wt_ref[o:o + IN_WIDTHS[k], :] = wcol.T.astype(BF16)
            o += IN_WIDTHS[k]

    h = _rms_modulate(x_ref[...], g_ref[...], shift_ref[0, 0], scale_ref[0, 0]).astype(BF16)
    tok_refs = dict(zip(TOK_COLS, (ka_ref, kb_ref, ga_ref, gb_ref)))
    feat_refs = dict(zip(FEAT_COLS, (qa_ref, va_ref, qb_ref, vb_ref)))

    def token_major(k):
        o = sum(IN_WIDTHS[c] for c in TOK_COLS[:TOK_COLS.index(k)])
        out = _dot(h, wk_ref[:, o:o + IN_WIDTHS[k]])
        tok_refs[k][...] = out.astype(BF16)
        return out

    def feature_major(k):
        o = sum(IN_WIDTHS[c] for c in FEAT_COLS[:FEAT_COLS.index(k)])
        out = _dot_nt(wt_ref[o:o + IN_WIDTHS[k], :], h).astype(BF16)
        blk = feat_refs[k].shape[-1]
        for t in range(TOKEN_TILE // blk):
            feat_refs[k][0, t] = out[:, t * blk:(t + 1) * blk]
        return out

    _update_kmean(token_major(4), kmean_ref, tile)
    qb_t = feature_major(3)
    token_major(1)
    token_major(6)
    _moba_select(qb_t, sel_ref, kmean_ref, tile)
    token_major(7)
    for k in (0, 2, 5):
        feature_major(k)


def _update_kmean(kb, kmean_ref, tile):
    for t in range(BLOCKS_PER_TILE):
        kmean_ref[pl.ds(tile * BLOCKS_PER_TILE + t, 1), :] = jnp.mean(
            kb[t * MOBA_BLOCK:(t + 1) * MOBA_BLOCK, :], axis=0, keepdims=True)


def _moba_select(qb_t, sel_ref, kmean_ref, tile):
    nblk = kmean_ref.shape[0]
    blocks_per_tile = BLOCKS_PER_TILE
    blk = lax.broadcasted_iota(jnp.int32, (nblk, TOKEN_TILE), 0)
    qblk = tile * blocks_per_tile + lax.broadcasted_iota(jnp.int32, (nblk, TOKEN_TILE), 1) // MOBA_BLOCK
    past = blk < qblk
    own = jnp.where(blk == qblk, 0.0, NEG_INF)
    blk_f = blk.astype(F32)
    for h in range(MOBA_HEADS):
        hs = slice(h * HEAD_DIM, (h + 1) * HEAD_DIM)
        kmean = kmean_ref[:, hs]
        k1 = kmean.astype(BF16)
        r1 = kmean - k1.astype(F32)
        k2 = r1.astype(BF16)
        k3 = (r1 - k2.astype(F32)).astype(BF16)
        parts = _dot(jnp.concatenate([k1, k2, k3], axis=0), qb_t[hs, :])
        gate = parts[0:nblk] + parts[nblk:2 * nblk] + parts[2 * nblk:3 * nblk]
        gate = jnp.where(past, gate, NEG_INF)
        mask = own
        for _ in range(MOBA_TOPK):
            top = jnp.max(gate, axis=0, keepdims=True)
            first = jnp.min(jnp.where(gate == top, blk_f, float(nblk)), axis=0, keepdims=True)
            pick = blk_f == first
            mask = jnp.where(jnp.logical_and(pick, past), 0.0, mask)
            gate = jnp.where(pick, NEG_INF, gate)
        sel_ref[0, h] = mask


def _proj(x2, mod4, g1, w_in, cast_weights, batch, seq):
    tokens = x2.shape[0]
    steps = tokens // TOKEN_TILE
    assert all(w.shape[0] % (16 * steps) == 0 for w in cast_weights)
    cast_specs = [pl.BlockSpec((w.shape[0] // steps, w.shape[1]), lambda i: (i, 0)) for w in cast_weights]
    tiles_per_seq = seq // TOKEN_TILE
    na = TOKEN_TILE // SWA_BLOCK
    nb = TOKEN_TILE // MOBA_BLOCK

    def feat_spec(width, blk, per_tile):
        return pl.BlockSpec((1, per_tile, width, blk),
                            lambda i: (i // tiles_per_seq, i % tiles_per_seq, 0, 0))

    out_shape = (
        jax.ShapeDtypeStruct((tokens, SWA_KV_W), BF16),
        jax.ShapeDtypeStruct((tokens, MOBA_W), BF16),
        jax.ShapeDtypeStruct((tokens, D_MODEL), BF16),
        jax.ShapeDtypeStruct((tokens, D_MODEL), BF16),
        jax.ShapeDtypeStruct((batch, seq // SWA_BLOCK, SWA_Q_W, SWA_BLOCK), BF16),
        jax.ShapeDtypeStruct((batch, seq // SWA_BLOCK, SWA_KV_W, SWA_BLOCK), BF16),
        jax.ShapeDtypeStruct((batch, seq // MOBA_BLOCK, MOBA_W, MOBA_BLOCK), BF16),
        jax.ShapeDtypeStruct((batch, seq // MOBA_BLOCK, MOBA_W, MOBA_BLOCK), BF16),
        jax.ShapeDtypeStruct((batch, MOBA_HEADS, seq // MOBA_BLOCK, seq), F32),
    )
    tok_spec = lambda w: pl.BlockSpec((TOKEN_TILE, w), lambda i: (i, 0))
    return pl.pallas_call(
        _proj_kernel,
        grid=(steps,),
        in_specs=[
            pl.BlockSpec((TOKEN_TILE, D_MODEL), lambda i: (i, 0)),
            pl.BlockSpec((1, 1, 1, D_MODEL), lambda i: (i // tiles_per_seq, 0, 0, 0)),
            pl.BlockSpec((1, 1, 1, D_MODEL), lambda i: (i // tiles_per_seq, 1, 0, 0)),
            _resident((1, D_MODEL), lambda i: (0, 0)),
            _resident(w_in.shape, lambda i: (0, 0)),
        ] + cast_specs,
        out_specs=(tok_spec(SWA_KV_W), tok_spec(MOBA_W), tok_spec(D_MODEL), tok_spec(D_MODEL),
                   feat_spec(SWA_Q_W, SWA_BLOCK, na), feat_spec(SWA_KV_W, SWA_BLOCK, na),
                   feat_spec(MOBA_W, MOBA_BLOCK, nb), feat_spec(MOBA_W, MOBA_BLOCK, nb),
                   pl.BlockSpec((1, MOBA_HEADS, seq // MOBA_BLOCK, TOKEN_TILE),
                                lambda i: (i // tiles_per_seq, 0, 0, i % tiles_per_seq)))
        + tuple(cast_specs),
        out_shape=out_shape + tuple(jax.ShapeDtypeStruct(w.shape, BF16) for w in cast_weights),
        scratch_shapes=[pltpu.VMEM((D_MODEL, TOK_W), BF16), pltpu.VMEM((FEAT_W, D_MODEL), BF16),
                        pltpu.VMEM((seq // MOBA_BLOCK, MOBA_W), F32)],
        compiler_params=pltpu.CompilerParams(dimension_semantics=("arbitrary",),
                                             vmem_limit_bytes=VMEM_LIMIT_BYTES),
        name="proj",
    )(x2, mod4, mod4, g1, w_in, *cast_weights)


def _swa_kernel(q_ref, k_ref, v_ref, rows_ref, sink_ref, o_ref, bias_ref, ot_ref, s_ref, cm_ref):
    L = SWA_BLOCK
    nblk = q_ref.shape[1]

    @pl.when(pl.program_id(0) == 0)
    def _():
        k_idx = lax.broadcasted_iota(jnp.int32, (L, L), 0)
        q_idx = lax.broadcasted_iota(jnp.int32, (L, L), 1)
        for h in range(SWA_Q_HEADS):
            g, gi = divmod(h, SWA_GROUP)
            t0 = rows_ref[0, h:h + 1, MOBA_BLOCK:MOBA_BLOCK + L]
            r = _toeplitz(jnp.concatenate([t0, t0], axis=1), L)
            cols = slice(gi * L, (gi + 1) * L)
            bias_ref[g, 0:L, cols] = jnp.where(q_idx < k_idx, r, NEG_INF).astype(BF16)
            bias_ref[g, L:2 * L, cols] = jnp.where(q_idx >= k_idx, r, NEG_INF).astype(BF16)

    ones_blk = jnp.concatenate([jnp.ones((1, 2 * L), BF16), jnp.zeros((ACC_ROWS - HEAD_DIM - 1, 2 * L), BF16)],
                               axis=0)
    PW = 2 * L
    chains = [(g, hp) for g in range(SWA_KV_HEADS) for hp in range(SWA_GROUP // 2)]
    kv_cols = [slice(g * HEAD_DIM, (g + 1) * HEAD_DIM) for g in range(SWA_KV_HEADS)]

    def q_pair(c, g, hp):
        h0 = g * SWA_GROUP + 2 * hp
        return jnp.concatenate([q_ref[0, c, (h0 + e) * HEAD_DIM:(h0 + e + 1) * HEAD_DIM, :] for e in range(2)],
                               axis=1)

    def scores_first(idx):
        g, hp = chains[idx]
        lanes = slice(hp * PW, (hp + 1) * PW)
        s_ref[idx, 0:L, :] = jnp.full((L, PW), NEG_INF, BF16)
        s = _dot(k_ref[0, 0:L, kv_cols[g]], q_pair(0, g, hp)).astype(BF16) + bias_ref[g, L:2 * L, lanes]
        s_ref[idx, L:2 * L, :] = s
        cm_ref[idx] = jnp.max(s, axis=0, keepdims=True).astype(F32)

    def scores(c, idx):
        g, hp = chains[idx]
        kw = k_ref[0, pl.ds(pl.multiple_of((c - 1) * L, L), 2 * L), kv_cols[g]]
        s = _dot(kw, q_pair(c, g, hp)).astype(BF16) + bias_ref[g, :, hp * PW:(hp + 1) * PW]
        s_ref[idx] = s
        cm_ref[idx] = jnp.max(s, axis=0, keepdims=True).astype(F32)

    def attend(c, c_prev, idx, slot):
        g, hp = chains[idx]
        hs = kv_cols[g]
        s = s_ref[idx]
        sink = sink_ref[g, :, hp * PW:(hp + 1) * PW]
        m_b = jnp.maximum(cm_ref[idx], sink).astype(BF16)
        m = m_b.astype(F32)
        p = jnp.exp2(s - m_b)
        v_win = jnp.concatenate([v_ref[0, c_prev, hs, :], v_ref[0, c, hs, :]], axis=1)
        pv = _dot(jnp.concatenate([v_win, ones_blk], axis=0), p)
        o = pv[0:HEAD_DIM] / (pv[HEAD_DIM:HEAD_DIM + 1] + jnp.exp2(sink - m))
        for e in range(2):
            r = (g * SWA_GROUP + 2 * hp + e) * HEAD_DIM
            ot_ref[slot, r:r + HEAD_DIM, :] = o[:, e * L:(e + 1) * L]

    def emit(c, slot):
        o_ref[0, pl.ds(pl.multiple_of(c * L, L), L), :] = ot_ref[slot].T.astype(o_ref.dtype)

    for idx in range(len(chains)):
        scores_first(idx)
    U = SWA_UNROLL
    for idx in range(len(chains)):
        attend(0, 0, idx, U - 1)
        scores(1, idx)

    def body(c, e):
        emit(c - 1, (e - 1) % U)
        for idx in range(len(chains)):
            attend(c, c - 1, idx, e)
            scores(c + 1, idx)

    def body_unrolled(u, carry):
        for e in range(U):
            body(U * u + 1 + e, e)
        return carry

    lax.fori_loop(0, (nblk - 2) // U, body_unrolled, 0)
    emit(nblk - 2, U - 1)
    for idx in range(len(chains)):
        attend(nblk - 1, nblk - 2, idx, 0)
    emit(nblk - 1, 0)


def _swa(qa_t, ka3, va_t, rows, sink_rows):
    batch, nblk = qa_t.shape[0], qa_t.shape[1]
    seq = nblk * SWA_BLOCK
    assert nblk >= 2 and (nblk - 2) % SWA_UNROLL == 0 and SWA_UNROLL >= 2
    return pl.pallas_call(
        _swa_kernel,
        grid=(batch,),
        in_specs=[
            pl.BlockSpec((1, nblk, SWA_Q_W, SWA_BLOCK), lambda b: (b, 0, 0, 0)),
            pl.BlockSpec((1, seq, SWA_KV_W), lambda b: (b, 0, 0)),
            pl.BlockSpec((1, nblk, SWA_KV_W, SWA_BLOCK), lambda b: (b, 0, 0, 0)),
            _resident((1, N_ATTN_HEADS, 2 * MOBA_BLOCK), lambda b: (0, 0, 0)),
            _resident(sink_rows.shape, lambda b: (0, 0, 0)),
        ],
        out_specs=pl.BlockSpec((1, seq, SWA_Q_W), lambda b: (b, 0, 0)),
        out_shape=jax.ShapeDtypeStruct((batch, seq, SWA_Q_W), BF16),
        scratch_shapes=[pltpu.VMEM((SWA_KV_HEADS, 2 * SWA_BLOCK, SWA_GROUP * SWA_BLOCK), BF16),
                        pltpu.VMEM((SWA_UNROLL, SWA_Q_W, SWA_BLOCK), F32),
                        pltpu.VMEM((SWA_Q_HEADS // 2, 2 * SWA_BLOCK, 2 * SWA_BLOCK), BF16),
                        pltpu.VMEM((SWA_Q_HEADS // 2, 1, 2 * SWA_BLOCK), F32)],
        compiler_params=pltpu.CompilerParams(dimension_semantics=("arbitrary",),
                                             vmem_limit_bytes=VMEM_LIMIT_BYTES),
        name="swa",
    )(qa_t, ka3, va_t, rows, sink_rows)


def _moba_kernel(q_ref, qn_ref, k_ref, v_ref, sel_ref, rows_ref, o_ref, bias_ref, m_ref, acc_ref, s_ref, cm_ref):
    step = pl.program_id(1)
    last = pl.num_programs(1) - 1
    MB = MOBA_BLOCK
    H = MOBA_HEADS
    G = MOBA_Q_PER_STEP

    @pl.when(jnp.logical_and(pl.program_id(0) == 0, step == 0))
    def _():
        for h in range(H):
            for d in range(BIAS_SLOTS):
                bias_ref[h, d] = _toeplitz(rows_ref[d, SWA_Q_HEADS + h:SWA_Q_HEADS + h + 1, :], MB).astype(BF16)

    m_ref[...] = jnp.full(m_ref.shape, NEG_INF, F32)
    acc_ref[...] = jnp.zeros(acc_ref.shape, F32)
    heads = [slice(h * HEAD_DIM, (h + 1) * HEAD_DIM) for h in range(H)]
    ones_blk = jnp.concatenate([jnp.ones((1, MB), BF16), jnp.zeros((ACC_ROWS - HEAD_DIM - 1, MB), BF16)],
                               axis=0)

    def sel_row(j, qs, h):
        return sel_ref[0, h, pl.ds(j, 1), qs * MB:(qs + 1) * MB]

    def scores(j, qs, h):
        c = qs * H + h
        rows = pl.ds(pl.multiple_of(j * MB, MB), MB)
        slot = jnp.minimum(G * step + qs - j, BIAS_SLOTS - 1)
        s = _dot(k_ref[0, rows, heads[h]], q_ref[0, qs, heads[h], :]).astype(BF16) + bias_ref[h, slot]
        s_ref[c] = s
        cm_ref[c] = jnp.max(s, axis=0, keepdims=True).astype(F32) + sel_row(j, qs, h)

    def own_scores(blk, q_blk, c, h):
        rows = pl.ds(pl.multiple_of(blk * MB, MB), MB)
        s = _dot(k_ref[0, rows, heads[h]], q_blk[heads[h], :]).astype(BF16) + bias_ref[h, 0]
        s_ref[c] = s
        cm_ref[c] = jnp.max(s, axis=0, keepdims=True).astype(F32)

    def accumulate(j, qs, h):
        c = qs * H + h
        m_old = m_ref[c]
        m_new = jnp.maximum(m_old, cm_ref[c])
        alpha = jnp.exp2(m_old - m_new)
        p = jnp.exp2(s_ref[c] - (m_new - sel_row(j, qs, h)).astype(BF16))
        v_aug = jnp.concatenate([v_ref[0, j, heads[h], :], ones_blk], axis=0)
        rs = slice(c * ACC_ROWS, (c + 1) * ACC_ROWS)
        acc_ref[rs, :] = alpha * acc_ref[rs, :] + _dot(v_aug, p)
        m_ref[c] = m_new

    first = G * step

    @pl.when(step == 0)
    def _():
        for qs in range(G):
            for h in range(H):
                own_scores(qs, q_ref[0, qs], qs * H + h, h)

    for n in range(G - 1):
        for qs in range(G - 1 - n, G):
            for h in range(H):
                accumulate(first + G - 1 - n, qs, h)
                scores(first + G - 2 - n, qs, h)

    def body(j, carry):
        for qs in range(G):
            for h in range(H):
                accumulate(j, qs, h)
                scores(j - 1, qs, h)
        return carry

    def body2(u, carry):
        return body(first - 2 * u - 1, body(first - 2 * u, carry))

    lax.fori_loop(0, (G // 2) * step, body2, 0)

    @pl.when(step < last)
    def _():
        for h in range(H):
            for qs in range(G):
                accumulate(0, qs, h)
                own_scores(first + G + qs, qn_ref[0, qs], qs * H + h, h)

    @pl.when(step == last)
    def _():
        for h in range(H):
            for qs in range(G):
                accumulate(0, qs, h)

    for qs in range(G):
        outs = []
        for h in range(H):
            base = (qs * H + h) * ACC_ROWS
            outs.append(acc_ref[base:base + HEAD_DIM, :] / acc_ref[base + HEAD_DIM:base + HEAD_DIM + 1, :])
        o_ref[0, qs * MB:(qs + 1) * MB, :] = jnp.concatenate(outs, axis=0).T.astype(o_ref.dtype)


def _moba(qb_t, kb3, vb_t, sel, rows):
    batch, nblk = qb_t.shape[0], qb_t.shape[1]
    seq = nblk * MOBA_BLOCK
    g = MOBA_Q_PER_STEP
    assert g % 2 == 0 and nblk % g == 0
    chains = g * MOBA_HEADS
    return pl.pallas_call(
        _moba_kernel,
        grid=(batch, nblk // g),
        in_specs=[
            pl.BlockSpec((1, g, MOBA_W, MOBA_BLOCK), lambda b, i: (b, i, 0, 0)),
            pl.BlockSpec((1, g, MOBA_W, MOBA_BLOCK), lambda b, i: (b, jnp.minimum(i + 1, nblk // g - 1), 0, 0)),
            pl.BlockSpec((1, seq, MOBA_W), lambda b, i: (b, 0, 0)),
            pl.BlockSpec((1, nblk, MOBA_W, MOBA_BLOCK), lambda b, i: (b, 0, 0, 0)),
            pl.BlockSpec((1, MOBA_HEADS, nblk, g * MOBA_BLOCK), lambda b, i: (b, 0, 0, i)),
            _resident(rows.shape, lambda b, i: (0, 0, 0)),
        ],
        out_specs=pl.BlockSpec((1, g * MOBA_BLOCK, MOBA_W), lambda b, i: (b, i, 0)),
        out_shape=jax.ShapeDtypeStruct((batch, seq, MOBA_W), BF16),
        scratch_shapes=[pltpu.VMEM((MOBA_HEADS, BIAS_SLOTS, MOBA_BLOCK, MOBA_BLOCK), BF16),
                        pltpu.VMEM((chains, 1, MOBA_BLOCK), F32),
                        pltpu.VMEM((chains * ACC_ROWS, MOBA_BLOCK), F32),
                        pltpu.VMEM((chains, MOBA_BLOCK, MOBA_BLOCK), BF16),
                        pltpu.VMEM((chains, 1, MOBA_BLOCK), F32)],
        compiler_params=pltpu.CompilerParams(dimension_semantics=("arbitrary", "arbitrary"),
                                             vmem_limit_bytes=VMEM_LIMIT_BYTES),
        name="moba",
    )(qb_t, qb_t, kb3, vb_t, sel, rows)


def _post_kernel(x_ref, ya_ref, yb_ref, ga_ref, gb_ref, gate1_ref, shift2_ref, scale2_ref,
                 gate2_ref, g2_ref, gf_ref, wa_ref, wb_ref, wo_ref, w1_ref, w2_ref, o_ref):
    a = _dot(ya_ref[...], wa_ref[...])
    b = _dot(yb_ref[...], wb_ref[...])
    merged = (jax.nn.sigmoid(ga_ref[...].astype(F32)) * a
              + jax.nn.sigmoid(gb_ref[...].astype(F32)) * b)
    x1 = x_ref[...] + gate1_ref[0, 0] * _dot(merged.astype(BF16), wo_ref[...])
    h2 = _rms_modulate(x1, g2_ref[...], shift2_ref[0, 0], scale2_ref[0, 0]).astype(BF16)
    y = jnp.zeros_like(x1)
    for c in range(D_FF // FF_CHUNK):
        cs = slice(c * FF_CHUNK, (c + 1) * FF_CHUNK)
        u = jnp.square(jnp.maximum(_dot(h2, w1_ref[:, cs]), 0.0)).astype(BF16)
        y = y + _dot(u, w2_ref[cs, :])
    x2 = x1 + gate2_ref[0, 0] * y
    ms = jnp.mean(x2 * x2, axis=-1, keepdims=True)
    o_ref[...] = (x2 * lax.rsqrt(ms + RMS_EPS)) * gf_ref[...]


def _post(x2, ya, yb, ga, gb, mod4, g2, gf, wa, wb, wo, w1, w2, seq):
    tokens = x2.shape[0]
    tiles_per_seq = seq // TOKEN_TILE
    tok = lambda w: pl.BlockSpec((TOKEN_TILE, w), lambda i: (i, 0))
    modrow = lambda k: pl.BlockSpec((1, 1, 1, D_MODEL), lambda i: (i // tiles_per_seq, k, 0, 0))
    full = lambda a: _resident(a.shape, lambda i: (0, 0))
    return pl.pallas_call(
        _post_kernel,
        grid=(tokens // TOKEN_TILE,),
        in_specs=[tok(D_MODEL), tok(SWA_Q_W), tok(MOBA_W), tok(D_MODEL), tok(D_MODEL),
                  modrow(2), modrow(3), modrow(4), modrow(5),
                  full(g2), full(gf), full(wa), full(wb), full(wo), full(w1), full(w2)],
        out_specs=tok(D_MODEL),
        out_shape=jax.ShapeDtypeStruct((tokens, D_MODEL), F32),
        compiler_params=pltpu.CompilerParams(dimension_semantics=("arbitrary",),
                                             vmem_limit_bytes=VMEM_LIMIT_BYTES),
        name="post",
    )(x2, ya, yb, ga, gb, mod4, mod4, mod4, mod4, g2, gf, wa, wb, wo, w1, w2)


def kernel(x, c, ada_w, ada_b, norm1_g, norm2_g, w_in, attn_sinks, rel_bias, w_branch_a,
           w_branch_b, w_out, w_mlp_in, w_mlp_out, final_g):
    batch, seq, _ = x.shape
    depth = ada_w.shape[0]
    assert seq % TOKEN_TILE == 0 and TOKEN_TILE % MOBA_BLOCK == 0 and batch <= SUBLANES
    far = np.arange((BIAS_SLOTS - 1) * MOBA_BLOCK - (MOBA_BLOCK - 1), max(seq, BIAS_SLOTS * MOBA_BLOCK))
    assert np.all(_t5_bucket(far) == _t5_bucket(far[:1]))
    assert 2 * SWA_BLOCK <= MOBA_BLOCK and SWA_WINDOW == SWA_BLOCK
    assert depth == 1
    l = 0

    rows = _bias_rows(rel_bias)
    c_t = jnp.zeros((D_MODEL, SUBLANES), F32).at[:, :batch].set(c.astype(F32).T)

    x2 = x.reshape(batch * seq, D_MODEL)
    mod = _ada(c_t, ada_w[l], ada_b[l][None, :], batch)
    mod4 = mod[:batch].reshape(batch, N_MOD, 1, D_MODEL)
    later_weights = [w.astype(F32) for w in (w_branch_a[l], w_branch_b[l], w_out[l], w_mlp_in[l], w_mlp_out[l])]
    ka, kb, ga, gb, qa_t, va_t, qb_t, vb_t, sel, wa, wb, wo, w1, w2 = _proj(
        x2, mod4, norm1_g[l][None, :], w_in[l].astype(F32), later_weights, batch, seq)

    sink_rows = jnp.broadcast_to(
        (attn_sinks[l].astype(F32) * LOG2E).reshape(SWA_KV_HEADS, 1, SWA_GROUP, 1),
        (SWA_KV_HEADS, 1, SWA_GROUP, SWA_BLOCK)).reshape(SWA_KV_HEADS, 1, SWA_GROUP * SWA_BLOCK)
    ya = _swa(qa_t, ka.reshape(batch, seq, SWA_KV_W), va_t, rows, sink_rows)

    yb = _moba(qb_t, kb.reshape(batch, seq, MOBA_W), vb_t, sel, rows)

    out = _post(x2, ya.reshape(batch * seq, SWA_Q_W), yb.reshape(batch * seq, MOBA_W), ga, gb,
                mod4, norm2_g[l][None, :], final_g[None, :], wa, wb, wo, w1, w2, seq)
    return out.reshape(batch, seq, D_MODEL)
```

```python
import functools
import math

import jax
import jax.numpy as jnp
import numpy as np
from jax import lax
from jax.experimental import pallas as pl
from jax.experimental.pallas import tpu as pltpu

D_MODEL = 1024
HEAD_DIM = 64
ATTN_SCALE = HEAD_DIM ** -0.5
LOG2E = math.log2(math.e)
SWA_Q_HEADS = 8
SWA_KV_HEADS = 2
SWA_GROUP = SWA_Q_HEADS // SWA_KV_HEADS
SWA_WINDOW = 128
SWA_BLOCK = 128
MOBA_HEADS = 8
MOBA_BLOCK = 256
MOBA_TOPK = 3
NUM_BUCKETS = 32
MAX_EXACT = NUM_BUCKETS // 2
MAX_DISTANCE = 2048
N_ATTN_HEADS = SWA_Q_HEADS + MOBA_HEADS
SWA_Q_W = SWA_Q_HEADS * HEAD_DIM
SWA_KV_W = SWA_KV_HEADS * HEAD_DIM
MOBA_W = MOBA_HEADS * HEAD_DIM
D_FF = 4 * D_MODEL
N_MOD = 6
RMS_EPS = 1e-6

VMEM_LIMIT_BYTES = 56 * 1024 * 1024
TOKEN_TILE = 512
FF_CHUNK = 1024
SUBLANES = 8
ADA_COLS = 1536
BIAS_SLOTS = 8
SWA_UNROLL = 3
MOBA_Q_PER_STEP = 4
ACC_ROWS = HEAD_DIM + 16
NEG_INF = float("-inf")

F32 = jnp.float32
BF16 = jnp.bfloat16


def _resident(block_shape, index_map):
    return pl.BlockSpec(block_shape, index_map, pipeline_mode=pl.Buffered(1))


def _dot(a, b):
    return jnp.dot(a, b, preferred_element_type=F32)


def _dot_nt(a, b):
    return lax.dot_general(a, b, (((1,), (1,)), ((), ())), preferred_element_type=F32)


def _rms_modulate(xv, g, shift, scale):
    ms = jnp.mean(xv * xv, axis=-1, keepdims=True)
    y = xv * lax.rsqrt(ms + RMS_EPS)
    return (y * g) * (1.0 + scale) + shift


def _toeplitz(u, rows):
    wide = jnp.broadcast_to(u, (rows, 2 * rows))
    return pltpu.roll(wide, 0, 1, stride=1, stride_axis=0)[:, rows:]


def _ada_kernel(ct_ref, w_ref, b_ref, o_ref, *, batch):
    cs_t = jax.nn.silu(ct_ref[...])
    w = w_ref[...]
    rows = [jnp.sum(w * cs_t[:, b:b + 1], axis=0, keepdims=True) for b in range(batch)]
    rows.append(jnp.zeros((SUBLANES - batch, w.shape[1]), F32))
    o_ref[...] = jnp.concatenate(rows, axis=0) + b_ref[...]


def _ada(c_t, w, b, batch):
    n = w.shape[1]
    bn = ADA_COLS
    assert n % bn == 0
    return pl.pallas_call(
        functools.partial(_ada_kernel, batch=batch),
        grid=(n // bn,),
        in_specs=[pl.BlockSpec((D_MODEL, SUBLANES), lambda j: (0, 0)),
                  pl.BlockSpec((D_MODEL, bn), lambda j: (0, j)),
                  pl.BlockSpec((1, bn), lambda j: (0, j))],
        out_specs=pl.BlockSpec((SUBLANES, bn), lambda j: (0, j)),
        out_shape=jax.ShapeDtypeStruct((SUBLANES, n), F32),
        compiler_params=pltpu.CompilerParams(dimension_semantics=("arbitrary",),
                                             vmem_limit_bytes=VMEM_LIMIT_BYTES),
        name="ada",
    )(c_t, w, b)


def _t5_bucket(dist):
    n = np.maximum(dist, 0)
    nf = np.maximum(n, 1).astype(np.float32)
    large = MAX_EXACT + (np.log(nf / np.float32(MAX_EXACT)) / np.float32(math.log(MAX_DISTANCE / MAX_EXACT))
                         * np.float32(NUM_BUCKETS - MAX_EXACT)).astype(np.int32)
    large = np.minimum(large, NUM_BUCKETS - 1)
    return np.where(n < MAX_EXACT, n, large).astype(np.int32)


def _bias_rows_kernel(bucket_ref, rbt_ref, o_ref):
    bucket = bucket_ref[...]
    n = bucket.shape[1]
    onehot = jnp.where(lax.broadcasted_iota(jnp.int32, (NUM_BUCKETS, n), 0) == bucket, 1.0, 0.0)
    rows = jnp.dot(rbt_ref[...], onehot, preferred_element_type=F32,
                   precision=lax.Precision.HIGHEST)
    rows = jnp.where(bucket >= 0, rows * LOG2E, NEG_INF)
    for d in range(BIAS_SLOTS):
        o_ref[d] = rows[:, d * MOBA_BLOCK:(d + 2) * MOBA_BLOCK]


def _bias_rows(rel_bias):
    n = (BIAS_SLOTS + 1) * MOBA_BLOCK
    dist = np.arange(n, dtype=np.int32) - MOBA_BLOCK
    bucket = jnp.asarray(np.where(dist >= 0, _t5_bucket(dist), -1)[None, :], dtype=jnp.int32)
    return pl.pallas_call(
        _bias_rows_kernel,
        out_shape=jax.ShapeDtypeStruct((BIAS_SLOTS, N_ATTN_HEADS, 2 * MOBA_BLOCK), F32),
        name="bias_rows",
    )(bucket, rel_bias.astype(F32).T)


IN_WIDTHS = (SWA_Q_W, SWA_KV_W, SWA_KV_W, MOBA_W, MOBA_W, MOBA_W, D_MODEL, D_MODEL)
IN_OFFS = tuple(sum(IN_WIDTHS[:k]) for k in range(len(IN_WIDTHS) + 1))
FEAT_COLS = (0, 2, 3, 5)
FEAT_W = sum(IN_WIDTHS[k] for k in FEAT_COLS)
TOK_COLS = (1, 4, 6, 7)
TOK_W = sum(IN_WIDTHS[k] for k in TOK_COLS)
Q_COLS = (0, 3)
BLOCKS_PER_TILE = TOKEN_TILE // MOBA_BLOCK


def _proj_kernel(x_ref, shift_ref, scale_ref, g_ref, w_ref, *refs):
    n_cast = (len(refs) - 12) // 2
    cast_in, refs = refs[:n_cast], refs[n_cast:]
    ka_ref, kb_ref, ga_ref, gb_ref, qa_ref, va_ref, qb_ref, vb_ref, sel_ref = refs[:9]
    cast_out, (wk_ref, wt_ref, kmean_ref) = refs[9:9 + n_cast], refs[9 + n_cast:]

    for src, dst in zip(cast_in, cast_out):
        dst[...] = src[...].astype(dst.dtype)

    tile = pl.program_id(0) % (kmean_ref.shape[0] // BLOCKS_PER_TILE)

    @pl.when(tile == 0)
    def _():
        kmean_ref[...] = jnp.zeros(kmean_ref.shape, F32)

    @pl.when(pl.program_id(0) == 0)
    def _():
        o = 0
        for k in TOK_COLS:
            wk_ref[:, o:o + IN_WIDTHS[k]] = w_ref[:, IN_OFFS[k]:IN_OFFS[k + 1]].astype(BF16)
            o += IN_WIDTHS[k]
        o = 0
        for k in FEAT_COLS:
            wcol = w_ref[:, IN_OFFS[k]:IN_OFFS[k + 1]]
            if k in Q_COLS:
                wcol = wcol * (ATTN_SCALE * LOG2E)
            wt_ref[o:o + IN_WIDTHS[k], :] = wcol.T.astype(BF16)
            o += IN_WIDTHS[k]

    h = _rms_modulate(x_ref[...], g_ref[...], shift_ref[0, 0], scale_ref[0, 0]).astype(BF16)
    tok_refs = dict(zip(TOK_COLS, (ka_ref, kb_ref, ga_ref, gb_ref)))
    feat_refs = dict(zip(FEAT_COLS, (qa_ref, va_ref, qb_ref, vb_ref)))

    def token_major(k):
        o = sum(IN_WIDTHS[c] for c in TOK_COLS[:TOK_COLS.index(k)])
        out = _dot(h, wk_ref[:, o:o + IN_WIDTHS[k]])
        tok_refs[k][...] = out.astype(BF16)
        return out

    def feature_major(k):
        o = sum(IN_WIDTHS[c] for c in FEAT_COLS[:FEAT_COLS.index(k)])
        out = _dot_nt(wt_ref[o:o + IN_WIDTHS[k], :], h).astype(BF16)
        blk = feat_refs[k].shape[-1]
        for t in range(TOKEN_TILE // blk):
            feat_refs[k][0, t] = out[:, t * blk:(t + 1) * blk]
        return out

    _update_kmean(token_major(4), kmean_ref, tile)
    qb_t = feature_major(3)
    token_major(1)
    token_major(6)
    _moba_select(qb_t, sel_ref, kmean_ref, tile)
    token_major(7)
    for k in (0, 2, 5):
        feature_major(k)


def _update_kmean(kb, kmean_ref, tile):
    for t in range(BLOCKS_PER_TILE):
        kmean_ref[pl.ds(tile * BLOCKS_PER_TILE + t, 1), :] = jnp.mean(
            kb[t * MOBA_BLOCK:(t + 1) * MOBA_BLOCK, :], axis=0, keepdims=True)


def _moba_select(qb_t, sel_ref, kmean_ref, tile):
    nblk = kmean_ref.shape[0]
    blocks_per_tile = BLOCKS_PER_TILE
    blk = lax.broadcasted_iota(jnp.int32, (nblk, TOKEN_TILE), 0)
    qblk = tile * blocks_per_tile + lax.broadcasted_iota(jnp.int32, (nblk, TOKEN_TILE), 1) // MOBA_BLOCK
    past = blk < qblk
    own = jnp.where(blk == qblk, 0.0, NEG_INF)
    blk_f = blk.astype(F32)
    for h in range(MOBA_HEADS):
        hs = slice(h * HEAD_DIM, (h + 1) * HEAD_DIM)
        kmean = kmean_ref[:, hs]
        k1 = kmean.astype(BF16)
        r1 = kmean - k1.astype(F32)
        k2 = r1.astype(BF16)
        k3 = (r1 - k2.astype(F32)).astype(BF16)
        parts = _dot(jnp.concatenate([k1, k2, k3], axis=0), qb_t[hs, :])
        gate = parts[0:nblk] + parts[nblk:2 * nblk] + parts[2 * nblk:3 * nblk]
        gate = jnp.where(past, gate, NEG_INF)
        mask = own
        for _ in range(MOBA_TOPK):
            top = jnp.max(gate, axis=0, keepdims=True)
            first = jnp.min(jnp.where(gate == top, blk_f, float(nblk)), axis=0, keepdims=True)
            pick = blk_f == first
            mask = jnp.where(jnp.logical_and(pick, past), 0.0, mask)
            gate = jnp.where(pick, NEG_INF, gate)
        sel_ref[0, h] = mask


def _proj(x2, mod4, g1, w_in, cast_weights, batch, seq):
    tokens = x2.shape[0]
    steps = tokens // TOKEN_TILE
    assert all(w.shape[0] % (16 * steps) == 0 for w in cast_weights)
    cast_specs = [pl.BlockSpec((w.shape[0] // steps, w.shape[1]), lambda i: (i, 0)) for w in cast_weights]
    tiles_per_seq = seq // TOKEN_TILE
    na = TOKEN_TILE // SWA_BLOCK
    nb = TOKEN_TILE // MOBA_BLOCK

    def feat_spec(width, blk, per_tile):
        return pl.BlockSpec((1, per_tile, width, blk),
                            lambda i: (i // tiles_per_seq, i % tiles_per_seq, 0, 0))

    out_shape = (
        jax.ShapeDtypeStruct((tokens, SWA_KV_W), BF16),
        jax.ShapeDtypeStruct((tokens, MOBA_W), BF16),
        jax.ShapeDtypeStruct((tokens, D_MODEL), BF16),
        jax.ShapeDtypeStruct((tokens, D_MODEL), BF16),
        jax.ShapeDtypeStruct((batch, seq // SWA_BLOCK, SWA_Q_W, SWA_BLOCK), BF16),
        jax.ShapeDtypeStruct((batch, seq // SWA_BLOCK, SWA_KV_W, SWA_BLOCK), BF16),
        jax.ShapeDtypeStruct((batch, seq // MOBA_BLOCK, MOBA_W, MOBA_BLOCK), BF16),
        jax.ShapeDtypeStruct((batch, seq // MOBA_BLOCK, MOBA_W, MOBA_BLOCK), BF16),
        jax.ShapeDtypeStruct((batch, MOBA_HEADS, seq // MOBA_BLOCK, seq), F32),
    )
    tok_spec = lambda w: pl.BlockSpec((TOKEN_TILE, w), lambda i: (i, 0))
    return pl.pallas_call(
        _proj_kernel,
        grid=(steps,),
        in_specs=[
            pl.BlockSpec((TOKEN_TILE, D_MODEL), lambda i: (i, 0)),
            pl.BlockSpec((1, 1, 1, D_MODEL), lambda i: (i // tiles_per_seq, 0, 0, 0)),
            pl.BlockSpec((1, 1, 1, D_MODEL), lambda i: (i // tiles_per_seq, 1, 0, 0)),
            _resident((1, D_MODEL), lambda i: (0, 0)),
            _resident(w_in.shape, lambda i: (0, 0)),
        ] + cast_specs,
        out_specs=(tok_spec(SWA_KV_W), tok_spec(MOBA_W), tok_spec(D_MODEL), tok_spec(D_MODEL),
                   feat_spec(SWA_Q_W, SWA_BLOCK, na), feat_spec(SWA_KV_W, SWA_BLOCK, na),
                   feat_spec(MOBA_W, MOBA_BLOCK, nb), feat_spec(MOBA_W, MOBA_BLOCK, nb),
                   pl.BlockSpec((1, MOBA_HEADS, seq // MOBA_BLOCK, TOKEN_TILE),
                                lambda i: (i // tiles_per_seq, 0, 0, i % tiles_per_seq)))
        + tuple(cast_specs),
        out_shape=out_shape + tuple(jax.ShapeDtypeStruct(w.shape, BF16) for w in cast_weights),
        scratch_shapes=[pltpu.VMEM((D_MODEL, TOK_W), BF16), pltpu.VMEM((FEAT_W, D_MODEL), BF16),
                        pltpu.VMEM((seq // MOBA_BLOCK, MOBA_W), F32)],
        compiler_params=pltpu.CompilerParams(dimension_semantics=("arbitrary",),
                                             vmem_limit_bytes=VMEM_LIMIT_BYTES),
        name="proj",
    )(x2, mod4, mod4, g1, w_in, *cast_weights)


def _swa_kernel(q_ref, k_ref, v_ref, rows_ref, sink_ref, o_ref, bias_ref, ot_ref, s_ref, cm_ref):
    L = SWA_BLOCK
    nblk = q_ref.shape[1]

    @pl.when(pl.program_id(0) == 0)
    def _():
        k_idx = lax.broadcasted_iota(jnp.int32, (L, L), 0)
        q_idx = lax.broadcasted_iota(jnp.int32, (L, L), 1)
        for h in range(SWA_Q_HEADS):
            g, gi = divmod(h, SWA_GROUP)
            t0 = rows_ref[0, h:h + 1, MOBA_BLOCK:MOBA_BLOCK + L]
            r = _toeplitz(jnp.concatenate([t0, t0], axis=1), L)
            cols = slice(gi * L, (gi + 1) * L)
            bias_ref[g, 0:L, cols] = jnp.where(q_idx < k_idx, r, NEG_INF).astype(BF16)
            bias_ref[g, L:2 * L, cols] = jnp.where(q_idx >= k_idx, r, NEG_INF).astype(BF16)

    ones_blk = jnp.concatenate([jnp.ones((1, 2 * L), BF16), jnp.zeros((ACC_ROWS - HEAD_DIM - 1, 2 * L), BF16)],
                               axis=0)
    PW = 2 * L
    chains = [(g, hp) for g in range(SWA_KV_HEADS) for hp in range(SWA_GROUP // 2)]
    kv_cols = [slice(g * HEAD_DIM, (g + 1) * HEAD_DIM) for g in range(SWA_KV_HEADS)]

    def q_pair(c, g, hp):
        h0 = g * SWA_GROUP + 2 * hp
        return jnp.concatenate([q_ref[0, c, (h0 + e) * HEAD_DIM:(h0 + e + 1) * HEAD_DIM, :] for e in range(2)],
                               axis=1)

    def scores_first(idx):
        g, hp = chains[idx]
        lanes = slice(hp * PW, (hp + 1) * PW)
        s_ref[idx, 0:L, :] = jnp.full((L, PW), NEG_INF, BF16)
        s = _dot(k_ref[0, 0:L, kv_cols[g]], q_pair(0, g, hp)).astype(BF16) + bias_ref[g, L:2 * L, lanes]
        s_ref[idx, L:2 * L, :] = s
        cm_ref[idx] = jnp.max(s, axis=0, keepdims=True).astype(F32)

    def scores(c, idx):
        g, hp = chains[idx]
        kw = k_ref[0, pl.ds(pl.multiple_of((c - 1) * L, L), 2 * L), kv_cols[g]]
        s = _dot(kw, q_pair(c, g, hp)).astype(BF16) + bias_ref[g, :, hp * PW:(hp + 1) * PW]
        s_ref[idx] = s
        cm_ref[idx] = jnp.max(s, axis=0, keepdims=True).astype(F32)

    def attend(c, c_prev, idx, slot):
        g, hp = chains[idx]
        hs = kv_cols[g]
        s = s_ref[idx]
        sink = sink_ref[g, :, hp * PW:(hp + 1) * PW]
        m_b = jnp.maximum(cm_ref[idx], sink).astype(BF16)
        m = m_b.astype(F32)
        p = jnp.exp2(s - m_b)
        v_win = jnp.concatenate([v_ref[0, c_prev, hs, :], v_ref[0, c, hs, :]], axis=1)
        pv = _dot(jnp.concatenate([v_win, ones_blk], axis=0), p)
        o = pv[0:HEAD_DIM] / (pv[HEAD_DIM:HEAD_DIM + 1] + jnp.exp2(sink - m))
        for e in range(2):
            r = (g * SWA_GROUP + 2 * hp + e) * HEAD_DIM
            ot_ref[slot, r:r + HEAD_DIM, :] = o[:, e * L:(e + 1) * L]

    def emit(c, slot):
        o_ref[0, pl.ds(pl.multiple_of(c * L, L), L), :] = ot_ref[slot].T.astype(o_ref.dtype)

    for idx in range(len(chains)):
        scores_first(idx)
    U = SWA_UNROLL
    for idx in range(len(chains)):
        attend(0, 0, idx, U - 1)
        scores(1, idx)

    def body(c, e):
        emit(c - 1, (e - 1) % U)
        for idx in range(len(chains)):
            attend(c, c - 1, idx, e)
            scores(c + 1, idx)

    def body_unrolled(u, carry):
        for e in range(U):
            body(U * u + 1 + e, e)
        return carry

    lax.fori_loop(0, (nblk - 2) // U, body_unrolled, 0)
    emit(nblk - 2, U - 1)
    for idx in range(len(chains)):
        attend(nblk - 1, nblk - 2, idx, 0)
    emit(nblk - 1, 0)


def _swa(qa_t, ka3, va_t, rows, sink_rows):
    batch, nblk = qa_t.shape[0], qa_t.shape[1]
    seq = nblk * SWA_BLOCK
    assert nblk >= 2 and (nblk - 2) % SWA_UNROLL == 0 and SWA_UNROLL >= 2
    return pl.pallas_call(
        _swa_kernel,
        grid=(batch,),
        in_specs=[
            pl.BlockSpec((1, nblk, SWA_Q_W, SWA_BLOCK), lambda b: (b, 0, 0, 0)),
            pl.BlockSpec((1, seq, SWA_KV_W), lambda b: (b, 0, 0)),
            pl.BlockSpec((1, nblk, SWA_KV_W, SWA_BLOCK), lambda b: (b, 0, 0, 0)),
            _resident((1, N_ATTN_HEADS, 2 * MOBA_BLOCK), lambda b: (0, 0, 0)),
            _resident(sink_rows.shape, lambda b: (0, 0, 0)),
        ],
        out_specs=pl.BlockSpec((1, seq, SWA_Q_W), lambda b: (b, 0, 0)),
        out_shape=jax.ShapeDtypeStruct((batch, seq, SWA_Q_W), BF16),
        scratch_shapes=[pltpu.VMEM((SWA_KV_HEADS, 2 * SWA_BLOCK, SWA_GROUP * SWA_BLOCK), BF16),
                        pltpu.VMEM((SWA_UNROLL, SWA_Q_W, SWA_BLOCK), F32),
                        pltpu.VMEM((SWA_Q_HEADS // 2, 2 * SWA_BLOCK, 2 * SWA_BLOCK), BF16),
                        pltpu.VMEM((SWA_Q_HEADS // 2, 1, 2 * SWA_BLOCK), F32)],
        compiler_params=pltpu.CompilerParams(dimension_semantics=("arbitrary",),
                                             vmem_limit_bytes=VMEM_LIMIT_BYTES),
        name="swa",
    )(qa_t, ka3, va_t, rows, sink_rows)


def _moba_kernel(q_ref, qn_ref, k_ref, v_ref, sel_ref, rows_ref, o_ref, bias_ref, m_ref, acc_ref, s_ref, cm_ref):
    step = pl.program_id(1)
    last = pl.num_programs(1) - 1
    MB = MOBA_BLOCK
    H = MOBA_HEADS
    G = MOBA_Q_PER_STEP

    @pl.when(jnp.logical_and(pl.program_id(0) == 0, step == 0))
    def _():
        for h in range(H):
            for d in range(BIAS_SLOTS):
                bias_ref[h, d] = _toeplitz(rows_ref[d, SWA_Q_HEADS + h:SWA_Q_HEADS + h + 1, :], MB).astype(BF16)

    m_ref[...] = jnp.full(m_ref.shape, NEG_INF, F32)
    acc_ref[...] = jnp.zeros(acc_ref.shape, F32)
    heads = [slice(h * HEAD_DIM, (h + 1) * HEAD_DIM) for h in range(H)]
    ones_blk = jnp.concatenate([jnp.ones((1, MB), BF16), jnp.zeros((ACC_ROWS - HEAD_DIM - 1, MB), BF16)],
                               axis=0)

    def sel_row(j, qs, h):
        return sel_ref[0, h, pl.ds(j, 1), qs * MB:(qs + 1) * MB]

    def scores(j, qs, h):
        c = qs * H + h
        rows = pl.ds(pl.multiple_of(j * MB, MB), MB)
        slot = jnp.minimum(G * step + qs - j, BIAS_SLOTS - 1)
        s = _dot(k_ref[0, rows, heads[h]], q_ref[0, qs, heads[h], :]).astype(BF16) + bias_ref[h, slot]
        s_ref[c] = s
        cm_ref[c] = jnp.max(s, axis=0, keepdims=True).astype(F32) + sel_row(j, qs, h)

    def own_scores(blk, q_blk, c, h):
        rows = pl.ds(pl.multiple_of(blk * MB, MB), MB)
        s = _dot(k_ref[0, rows, heads[h]], q_blk[heads[h], :]).astype(BF16) + bias_ref[h, 0]
        s_ref[c] = s
        cm_ref[c] = jnp.max(s, axis=0, keepdims=True).astype(F32)

    def accumulate(j, qs, h):
        c = qs * H + h
        m_old = m_ref[c]
        m_new = jnp.maximum(m_old, cm_ref[c])
        alpha = jnp.exp2(m_old - m_new)
        p = jnp.exp2(s_ref[c] - (m_new - sel_row(j, qs, h)).astype(BF16))
        v_aug = jnp.concatenate([v_ref[0, j, heads[h], :], ones_blk], axis=0)
        rs = slice(c * ACC_ROWS, (c + 1) * ACC_ROWS)
        acc_ref[rs, :] = alpha * acc_ref[rs, :] + _dot(v_aug, p)
        m_ref[c] = m_new

    first = G * step

    @pl.when(step == 0)
    def _():
        for qs in range(G):
            for h in range(H):
                own_scores(qs, q_ref[0, qs], qs * H + h, h)

    for n in range(G - 1):
        for qs in range(G - 1 - n, G):
            for h in range(H):
                accumulate(first + G - 1 - n, qs, h)
                scores(first + G - 2 - n, qs, h)

    def body(j, carry):
        for qs in range(G):
            for h in range(H):
                accumulate(j, qs, h)
                scores(j - 1, qs, h)
        return carry

    def body2(u, carry):
        return body(first - 2 * u - 1, body(first - 2 * u, carry))

    lax.fori_loop(0, (G // 2) * step, body2, 0)

    def emit(qs):
        outs = []
        for h in range(H):
            base = (qs * H + h) * ACC_ROWS
            outs.append(acc_ref[base:base + HEAD_DIM, :] / acc_ref[base + HEAD_DIM:base + HEAD_DIM + 1, :])
        o_ref[0, qs * MB:(qs + 1) * MB, :] = jnp.concatenate(outs, axis=0).T.astype(o_ref.dtype)

    @pl.when(step < last)
    def _():
        for qs in range(G):
            for h in range(H):
                accumulate(0, qs, h)
                own_scores(first + G + qs, qn_ref[0, qs], qs * H + h, h)
            emit(qs)

    @pl.when(step == last)
    def _():
        for qs in range(G):
            for h in range(H):
                accumulate(0, qs, h)
            emit(qs)


def _moba(qb_t, kb3, vb_t, sel, rows):
    batch, nblk = qb_t.shape[0], qb_t.shape[1]
    seq = nblk * MOBA_BLOCK
    g = MOBA_Q_PER_STEP
    assert g % 2 == 0 and nblk % g == 0
    chains = g * MOBA_HEADS
    return pl.pallas_call(
        _moba_kernel,
        grid=(batch, nblk // g),
        in_specs=[
            pl.BlockSpec((1, g, MOBA_W, MOBA_BLOCK), lambda b, i: (b, i, 0, 0)),
            pl.BlockSpec((1, g, MOBA_W, MOBA_BLOCK), lambda b, i: (b, jnp.minimum(i + 1, nblk // g - 1), 0, 0)),
            pl.BlockSpec((1, seq, MOBA_W), lambda b, i: (b, 0, 0)),
            pl.BlockSpec((1, nblk, MOBA_W, MOBA_BLOCK), lambda b, i: (b, 0, 0, 0)),
            pl.BlockSpec((1, MOBA_HEADS, nblk, g * MOBA_BLOCK), lambda b, i: (b, 0, 0, i)),
            _resident(rows.shape, lambda b, i: (0, 0, 0)),
        ],
        out_specs=pl.BlockSpec((1, g * MOBA_BLOCK, MOBA_W), lambda b, i: (b, i, 0)),
        out_shape=jax.ShapeDtypeStruct((batch, seq, MOBA_W), BF16),
        scratch_shapes=[pltpu.VMEM((MOBA_HEADS, BIAS_SLOTS, MOBA_BLOCK, MOBA_BLOCK), BF16),
                        pltpu.VMEM((chains, 1, MOBA_BLOCK), F32),
                        pltpu.VMEM((chains * ACC_ROWS, MOBA_BLOCK), F32),
                        pltpu.VMEM((chains, MOBA_BLOCK, MOBA_BLOCK), BF16),
                        pltpu.VMEM((chains, 1, MOBA_BLOCK), F32)],
        compiler_params=pltpu.CompilerParams(dimension_semantics=("arbitrary", "arbitrary"),
                                             vmem_limit_bytes=VMEM_LIMIT_BYTES),
        name="moba",
    )(qb_t, qb_t, kb3, vb_t, sel, rows)


def _post_kernel(x_ref, ya_ref, yb_ref, ga_ref, gb_ref, gate1_ref, shift2_ref, scale2_ref,
                 gate2_ref, g2_ref, gf_ref, wa_ref, wb_ref, wo_ref, w1_ref, w2_ref, o_ref):
    a = _dot(ya_ref[...], wa_ref[...])
    b = _dot(yb_ref[...], wb_ref[...])
    merged = (jax.nn.sigmoid(ga_ref[...].astype(F32)) * a
              + jax.nn.sigmoid(gb_ref[...].astype(F32)) * b)
    x1 = x_ref[...] + gate1_ref[0, 0] * _dot(merged.astype(BF16), wo_ref[...])
    h2 = _rms_modulate(x1, g2_ref[...], shift2_ref[0, 0], scale2_ref[0, 0]).astype(BF16)
    y = jnp.zeros_like(x1)
    for c in range(D_FF // FF_CHUNK):
        cs = slice(c * FF_CHUNK, (c + 1) * FF_CHUNK)
        u = jnp.square(jnp.maximum(_dot(h2, w1_ref[:, cs]), 0.0)).astype(BF16)
        y = y + _dot(u, w2_ref[cs, :])
    x2 = x1 + gate2_ref[0, 0] * y
    ms = jnp.mean(x2 * x2, axis=-1, keepdims=True)
    o_ref[...] = (x2 * lax.rsqrt(ms + RMS_EPS)) * gf_ref[...]


def _post(x2, ya, yb, ga, gb, mod4, g2, gf, wa, wb, wo, w1, w2, seq):
    tokens = x2.shape[0]
    tiles_per_seq = seq // TOKEN_TILE
    tok = lambda w: pl.BlockSpec((TOKEN_TILE, w), lambda i: (i, 0))
    modrow = lambda k: pl.BlockSpec((1, 1, 1, D_MODEL), lambda i: (i // tiles_per_seq, k, 0, 0))
    full = lambda a: _resident(a.shape, lambda i: (0, 0))
    return pl.pallas_call(
        _post_kernel,
        grid=(tokens // TOKEN_TILE,),
        in_specs=[tok(D_MODEL), tok(SWA_Q_W), tok(MOBA_W), tok(D_MODEL), tok(D_MODEL),
                  modrow(2), modrow(3), modrow(4), modrow(5),
                  full(g2), full(gf), full(wa), full(wb), full(wo), full(w1), full(w2)],
        out_specs=tok(D_MODEL),
        out_shape=jax.ShapeDtypeStruct((tokens, D_MODEL), F32),
        compiler_params=pltpu.CompilerParams(dimension_semantics=("arbitrary",),
                                             vmem_limit_bytes=VMEM_LIMIT_BYTES),
        name="post",
    )(x2, ya, yb, ga, gb, mod4, mod4, mod4, mod4, g2, gf, wa, wb, wo, w1, w2)


def kernel(x, c, ada_w, ada_b, norm1_g, norm2_g, w_in, attn_sinks, rel_bias, w_branch_a,
           w_branch_b, w_out, w_mlp_in, w_mlp_out, final_g):
    batch, seq, _ = x.shape
    depth = ada_w.shape[0]
    assert seq % TOKEN_TILE == 0 and TOKEN_TILE % MOBA_BLOCK == 0 and batch <= SUBLANES
    far = np.arange((BIAS_SLOTS - 1) * MOBA_BLOCK - (MOBA_BLOCK - 1), max(seq, BIAS_SLOTS * MOBA_BLOCK))
    assert np.all(_t5_bucket(far) == _t5_bucket(far[:1]))
    assert 2 * SWA_BLOCK <= MOBA_BLOCK and SWA_WINDOW == SWA_BLOCK
    assert depth == 1
    l = 0

    rows = _bias_rows(rel_bias)
    c_t = jnp.zeros((D_MODEL, SUBLANES), F32).at[:, :batch].set(c.astype(F32).T)

    x2 = x.reshape(batch * seq, D_MODEL)
    mod = _ada(c_t, ada_w[l], ada_b[l][None, :], batch)
    mod4 = mod[:batch].reshape(batch, N_MOD, 1, D_MODEL)
    later_weights = [w.astype(F32) for w in (w_branch_a[l], w_branch_b[l], w_out[l], w_mlp_in[l], w_mlp_out[l])]
    ka, kb, ga, gb, qa_t, va_t, qb_t, vb_t, sel, wa, wb, wo, w1, w2 = _proj(
        x2, mod4, norm1_g[l][None, :], w_in[l].astype(F32), later_weights, batch, seq)

    sink_rows = jnp.broadcast_to(
        (attn_sinks[l].astype(F32) * LOG2E).reshape(SWA_KV_HEADS, 1, SWA_GROUP, 1),
        (SWA_KV_HEADS, 1, SWA_GROUP, SWA_BLOCK)).reshape(SWA_KV_HEADS, 1, SWA_GROUP * SWA_BLOCK)
    ya = _swa(qa_t, ka.reshape(batch, seq, SWA_KV_W), va_t, rows, sink_rows)

    yb = _moba(qb_t, kb.reshape(batch, seq, MOBA_W), vb_t, sel, rows)

    out = _post(x2, ya.reshape(batch * seq, SWA_Q_W), yb.reshape(batch * seq, MOBA_W), ga, gb,
                mod4, norm2_g[l][None, :], final_g[None, :], wa, wb, wo, w1, w2, seq)
    return out.reshape(batch, seq, D_MODEL)
```

```python
import math

import jax
import jax.numpy as jnp
import numpy as np
from jax import lax
from jax.experimental import pallas as pl
from jax.experimental.pallas import tpu as pltpu

D_MODEL = 1024
HEAD_DIM = 64
ATTN_SCALE = HEAD_DIM ** -0.5
LOG2E = math.log2(math.e)
SWA_Q_HEADS = 8
SWA_KV_HEADS = 2
SWA_GROUP = SWA_Q_HEADS // SWA_KV_HEADS
SWA_WINDOW = 128
SWA_BLOCK = 128
MOBA_HEADS = 8
MOBA_BLOCK = 256
MOBA_TOPK = 3
NUM_BUCKETS = 32
MAX_EXACT = NUM_BUCKETS // 2
MAX_DISTANCE = 2048
N_ATTN_HEADS = SWA_Q_HEADS + MOBA_HEADS
SWA_Q_W = SWA_Q_HEADS * HEAD_DIM
SWA_KV_W = SWA_KV_HEADS * HEAD_DIM
MOBA_W = MOBA_HEADS * HEAD_DIM
D_FF = 4 * D_MODEL
N_MOD = 6
RMS_EPS = 1e-6

VMEM_LIMIT_BYTES = 56 * 1024 * 1024
TOKEN_TILE = 512
FF_CHUNK = 1024
SUBLANES = 8
ADA_COLS = 1536
BIAS_SLOTS = 8
SWA_UNROLL = 3
MOBA_Q_PER_STEP = 4
ACC_ROWS = HEAD_DIM + 16
NEG_INF = float("-inf")

F32 = jnp.float32
BF16 = jnp.bfloat16


def _resident(block_shape, index_map):
    return pl.BlockSpec(block_shape, index_map, pipeline_mode=pl.Buffered(1))


def _dot(a, b):
    return jnp.dot(a, b, preferred_element_type=F32)


def _dot_nt(a, b):
    return lax.dot_general(a, b, (((1,), (1,)), ((), ())), preferred_element_type=F32)


def _rms_modulate(xv, g, shift, scale):
    ms = jnp.mean(xv * xv, axis=-1, keepdims=True)
    y = xv * lax.rsqrt(ms + RMS_EPS)
    return (y * g) * (1.0 + scale) + shift


def _toeplitz(u, rows):
    wide = jnp.broadcast_to(u, (rows, 2 * rows))
    return pltpu.roll(wide, 0, 1, stride=1, stride_axis=0)[:, rows:]


def _ada_kernel(c_ref, w_ref, b_ref, o_ref):
    cs = jax.nn.silu(c_ref[...]).astype(BF16)
    o_ref[...] = _dot(cs, w_ref[...].astype(BF16)) + b_ref[...]


def _ada(c_pad, w, b):
    n = w.shape[1]
    bn = ADA_COLS
    assert n % bn == 0
    return pl.pallas_call(
        _ada_kernel,
        grid=(n // bn,),
        in_specs=[pl.BlockSpec((SUBLANES, D_MODEL), lambda j: (0, 0)),
                  pl.BlockSpec((D_MODEL, bn), lambda j: (0, j)),
                  pl.BlockSpec((1, bn), lambda j: (0, j))],
        out_specs=pl.BlockSpec((SUBLANES, bn), lambda j: (0, j)),
        out_shape=jax.ShapeDtypeStruct((SUBLANES, n), F32),
        compiler_params=pltpu.CompilerParams(dimension_semantics=("arbitrary",),
                                             vmem_limit_bytes=VMEM_LIMIT_BYTES),
        name="ada",
    )(c_pad, w, b)


def _t5_bucket(dist):
    n = np.maximum(dist, 0)
    nf = np.maximum(n, 1).astype(np.float32)
    large = MAX_EXACT + (np.log(nf / np.float32(MAX_EXACT)) / np.float32(math.log(MAX_DISTANCE / MAX_EXACT))
                         * np.float32(NUM_BUCKETS - MAX_EXACT)).astype(np.int32)
    large = np.minimum(large, NUM_BUCKETS - 1)
    return np.where(n < MAX_EXACT, n, large).astype(np.int32)


def _bias_rows_kernel(bucket_ref, rbt_ref, o_ref):
    bucket = bucket_ref[...]
    n = bucket.shape[1]
    onehot = jnp.where(lax.broadcasted_iota(jnp.int32, (NUM_BUCKETS, n), 0) == bucket, 1.0, 0.0)
    rows = jnp.dot(rbt_ref[...], onehot, preferred_element_type=F32,
                   precision=lax.Precision.HIGHEST)
    rows = jnp.where(bucket >= 0, rows * LOG2E, NEG_INF)
    for d in range(BIAS_SLOTS):
        o_ref[d] = rows[:, d * MOBA_BLOCK:(d + 2) * MOBA_BLOCK]


def _bias_rows(rel_bias):
    n = (BIAS_SLOTS + 1) * MOBA_BLOCK
    dist = np.arange(n, dtype=np.int32) - MOBA_BLOCK
    bucket = jnp.asarray(np.where(dist >= 0, _t5_bucket(dist), -1)[None, :], dtype=jnp.int32)
    return pl.pallas_call(
        _bias_rows_kernel,
        out_shape=jax.ShapeDtypeStruct((BIAS_SLOTS, N_ATTN_HEADS, 2 * MOBA_BLOCK), F32),
        name="bias_rows",
    )(bucket, rel_bias.astype(F32).T)


IN_WIDTHS = (SWA_Q_W, SWA_KV_W, SWA_KV_W, MOBA_W, MOBA_W, MOBA_W, D_MODEL, D_MODEL)
IN_OFFS = tuple(sum(IN_WIDTHS[:k]) for k in range(len(IN_WIDTHS) + 1))
FEAT_COLS = (0, 2, 3, 5)
FEAT_W = sum(IN_WIDTHS[k] for k in FEAT_COLS)
TOK_COLS = (1, 4, 6, 7)
TOK_W = sum(IN_WIDTHS[k] for k in TOK_COLS)
Q_COLS = (0, 3)
BLOCKS_PER_TILE = TOKEN_TILE // MOBA_BLOCK


def _proj_kernel(x_ref, shift_ref, scale_ref, g_ref, w_ref, *refs):
    n_cast = (len(refs) - 12) // 2
    cast_in, refs = refs[:n_cast], refs[n_cast:]
    ka_ref, kb_ref, ga_ref, gb_ref, qa_ref, va_ref, qb_ref, vb_ref, sel_ref = refs[:9]
    cast_out, (wk_ref, wt_ref, kmean_ref) = refs[9:9 + n_cast], refs[9 + n_cast:]

    for src, dst in zip(cast_in, cast_out):
        dst[...] = src[...].astype(dst.dtype)

    tile = pl.program_id(0) % (kmean_ref.shape[0] // BLOCKS_PER_TILE)

    @pl.when(tile == 0)
    def _():
        kmean_ref[...] = jnp.zeros(kmean_ref.shape, F32)

    @pl.when(pl.program_id(0) == 0)
    def _():
        o = 0
        for k in TOK_COLS:
            wk_ref[:, o:o + IN_WIDTHS[k]] = w_ref[:, IN_OFFS[k]:IN_OFFS[k + 1]].astype(BF16)
            o += IN_WIDTHS[k]
        o = 0
        for k in FEAT_COLS:
            wcol = w_ref[:, IN_OFFS[k]:IN_OFFS[k + 1]]
            if k in Q_COLS:
                wcol = wcol * (ATTN_SCALE * LOG2E)
            wt_ref[o:o + IN_WIDTHS[k], :] = wcol.T.astype(BF16)
            o += IN_WIDTHS[k]

    h = _rms_modulate(x_ref[...], g_ref[...], shift_ref[0, 0], scale_ref[0, 0]).astype(BF16)
    tok_refs = dict(zip(TOK_COLS, (ka_ref, kb_ref, ga_ref, gb_ref)))
    feat_refs = dict(zip(FEAT_COLS, (qa_ref, va_ref, qb_ref, vb_ref)))

    def token_major(k):
        o = sum(IN_WIDTHS[c] for c in TOK_COLS[:TOK_COLS.index(k)])
        out = _dot(h, wk_ref[:, o:o + IN_WIDTHS[k]])
        tok_refs[k][...] = out.astype(BF16)
        return out

    def feature_major(k):
        o = sum(IN_WIDTHS[c] for c in FEAT_COLS[:FEAT_COLS.index(k)])
        out = _dot_nt(wt_ref[o:o + IN_WIDTHS[k], :], h).astype(BF16)
        blk = feat_refs[k].shape[-1]
        for t in range(TOKEN_TILE // blk):
            feat_refs[k][0, t] = out[:, t * blk:(t + 1) * blk]
        return out

    _update_kmean(token_major(4), kmean_ref, tile)
    qb_t = feature_major(3)
    token_major(1)
    token_major(6)
    _moba_select(qb_t, sel_ref, kmean_ref, tile)
    token_major(7)
    for k in (0, 2, 5):
        feature_major(k)


def _update_kmean(kb, kmean_ref, tile):
    for t in range(BLOCKS_PER_TILE):
        kmean_ref[pl.ds(tile * BLOCKS_PER_TILE + t, 1), :] = jnp.mean(
            kb[t * MOBA_BLOCK:(t + 1) * MOBA_BLOCK, :], axis=0, keepdims=True)


def _moba_select(qb_t, sel_ref, kmean_ref, tile):
    nblk = kmean_ref.shape[0]
    blocks_per_tile = BLOCKS_PER_TILE
    blk = lax.broadcasted_iota(jnp.int32, (nblk, TOKEN_TILE), 0)
    qblk = tile * blocks_per_tile + lax.broadcasted_iota(jnp.int32, (nblk, TOKEN_TILE), 1) // MOBA_BLOCK
    past = blk < qblk
    own = jnp.where(blk == qblk, 0.0, NEG_INF)
    blk_f = blk.astype(F32)
    for h in range(MOBA_HEADS):
        hs = slice(h * HEAD_DIM, (h + 1) * HEAD_DIM)
        kmean = kmean_ref[:, hs]
        k1 = kmean.astype(BF16)
        r1 = kmean - k1.astype(F32)
        k2 = r1.astype(BF16)
        k3 = (r1 - k2.astype(F32)).astype(BF16)
        parts = _dot(jnp.concatenate([k1, k2, k3], axis=0), qb_t[hs, :])
        gate = parts[0:nblk] + parts[nblk:2 * nblk] + parts[2 * nblk:3 * nblk]
        gate = jnp.where(past, gate, NEG_INF)
        mask = own
        for _ in range(MOBA_TOPK):
            top = jnp.max(gate, axis=0, keepdims=True)
            first = jnp.min(jnp.where(gate == top, blk_f, float(nblk)), axis=0, keepdims=True)
            pick = blk_f == first
            mask = jnp.where(jnp.logical_and(pick, past), 0.0, mask)
            gate = jnp.where(pick, NEG_INF, gate)
        sel_ref[0, h] = mask


def _proj(x2, mod4, g1, w_in, cast_weights, batch, seq):
    tokens = x2.shape[0]
    steps = tokens // TOKEN_TILE
    assert all(w.shape[0] % (16 * steps) == 0 for w in cast_weights)
    cast_specs = [pl.BlockSpec((w.shape[0] // steps, w.shape[1]), lambda i: (i, 0)) for w in cast_weights]
    tiles_per_seq = seq // TOKEN_TILE
    na = TOKEN_TILE // SWA_BLOCK
    nb = TOKEN_TILE // MOBA_BLOCK

    def feat_spec(width, blk, per_tile):
        return pl.BlockSpec((1, per_tile, width, blk),
                            lambda i: (i // tiles_per_seq, i % tiles_per_seq, 0, 0))

    out_shape = (
        jax.ShapeDtypeStruct((tokens, SWA_KV_W), BF16),
        jax.ShapeDtypeStruct((tokens, MOBA_W), BF16),
        jax.ShapeDtypeStruct((tokens, D_MODEL), BF16),
        jax.ShapeDtypeStruct((tokens, D_MODEL), BF16),
        jax.ShapeDtypeStruct((batch, seq // SWA_BLOCK, SWA_Q_W, SWA_BLOCK), BF16),
        jax.ShapeDtypeStruct((batch, seq // SWA_BLOCK, SWA_KV_W, SWA_BLOCK), BF16),
        jax.ShapeDtypeStruct((batch, seq // MOBA_BLOCK, MOBA_W, MOBA_BLOCK), BF16),
        jax.ShapeDtypeStruct((batch, seq // MOBA_BLOCK, MOBA_W, MOBA_BLOCK), BF16),
        jax.ShapeDtypeStruct((batch, MOBA_HEADS, seq // MOBA_BLOCK, seq), F32),
    )
    tok_spec = lambda w: pl.BlockSpec((TOKEN_TILE, w), lambda i: (i, 0))
    return pl.pallas_call(
        _proj_kernel,
        grid=(steps,),
        in_specs=[
            pl.BlockSpec((TOKEN_TILE, D_MODEL), lambda i: (i, 0)),
            pl.BlockSpec((1, 1, 1, D_MODEL), lambda i: (i // tiles_per_seq, 0, 0, 0)),
            pl.BlockSpec((1, 1, 1, D_MODEL), lambda i: (i // tiles_per_seq, 1, 0, 0)),
            _resident((1, D_MODEL), lambda i: (0, 0)),
            _resident(w_in.shape, lambda i: (0, 0)),
        ] + cast_specs,
        out_specs=(tok_spec(SWA_KV_W), tok_spec(MOBA_W), tok_spec(D_MODEL), tok_spec(D_MODEL),
                   feat_spec(SWA_Q_W, SWA_BLOCK, na), feat_spec(SWA_KV_W, SWA_BLOCK, na),
                   feat_spec(MOBA_W, MOBA_BLOCK, nb), feat_spec(MOBA_W, MOBA_BLOCK, nb),
                   pl.BlockSpec((1, MOBA_HEADS, seq // MOBA_BLOCK, TOKEN_TILE),
                                lambda i: (i // tiles_per_seq, 0, 0, i % tiles_per_seq)))
        + tuple(cast_specs),
        out_shape=out_shape + tuple(jax.ShapeDtypeStruct(w.shape, BF16) for w in cast_weights),
        scratch_shapes=[pltpu.VMEM((D_MODEL, TOK_W), BF16), pltpu.VMEM((FEAT_W, D_MODEL), BF16),
                        pltpu.VMEM((seq // MOBA_BLOCK, MOBA_W), F32)],
        compiler_params=pltpu.CompilerParams(dimension_semantics=("arbitrary",),
                                             vmem_limit_bytes=VMEM_LIMIT_BYTES),
        name="proj",
    )(x2, mod4, mod4, g1, w_in, *cast_weights)


def _swa_kernel(q_ref, k_ref, v_ref, rows_ref, sink_ref, o_ref, bias_ref, ot_ref, s_ref, cm_ref):
    L = SWA_BLOCK
    nblk = q_ref.shape[1]

    @pl.when(pl.program_id(0) == 0)
    def _():
        k_idx = lax.broadcasted_iota(jnp.int32, (L, L), 0)
        q_idx = lax.broadcasted_iota(jnp.int32, (L, L), 1)
        for h in range(SWA_Q_HEADS):
            g, gi = divmod(h, SWA_GROUP)
            t0 = rows_ref[0, h:h + 1, MOBA_BLOCK:MOBA_BLOCK + L]
            r = _toeplitz(jnp.concatenate([t0, t0], axis=1), L)
            cols = slice(gi * L, (gi + 1) * L)
            bias_ref[g, 0:L, cols] = jnp.where(q_idx < k_idx, r, NEG_INF).astype(BF16)
            bias_ref[g, L:2 * L, cols] = jnp.where(q_idx >= k_idx, r, NEG_INF).astype(BF16)

    ones_blk = jnp.concatenate([jnp.ones((1, 2 * L), BF16), jnp.zeros((ACC_ROWS - HEAD_DIM - 1, 2 * L), BF16)],
                               axis=0)
    PW = 2 * L
    chains = [(g, hp) for g in range(SWA_KV_HEADS) for hp in range(SWA_GROUP // 2)]
    kv_cols = [slice(g * HEAD_DIM, (g + 1) * HEAD_DIM) for g in range(SWA_KV_HEADS)]

    def q_pair(c, g, hp):
        h0 = g * SWA_GROUP + 2 * hp
        return jnp.concatenate([q_ref[0, c, (h0 + e) * HEAD_DIM:(h0 + e + 1) * HEAD_DIM, :] for e in range(2)],
                               axis=1)

    def scores_first(idx):
        g, hp = chains[idx]
        lanes = slice(hp * PW, (hp + 1) * PW)
        s_ref[idx, 0:L, :] = jnp.full((L, PW), NEG_INF, BF16)
        s = _dot(k_ref[0, 0:L, kv_cols[g]], q_pair(0, g, hp)).astype(BF16) + bias_ref[g, L:2 * L, lanes]
        s_ref[idx, L:2 * L, :] = s
        cm_ref[idx] = jnp.max(s, axis=0, keepdims=True).astype(F32)

    def scores(c, idx):
        g, hp = chains[idx]
        kw = k_ref[0, pl.ds(pl.multiple_of((c - 1) * L, L), 2 * L), kv_cols[g]]
        s = _dot(kw, q_pair(c, g, hp)).astype(BF16) + bias_ref[g, :, hp * PW:(hp + 1) * PW]
        s_ref[idx] = s
        cm_ref[idx] = jnp.max(s, axis=0, keepdims=True).astype(F32)

    def attend(c, c_prev, idx, slot):
        g, hp = chains[idx]
        hs = kv_cols[g]
        s = s_ref[idx]
        sink = sink_ref[g, :, hp * PW:(hp + 1) * PW]
        m_b = jnp.maximum(cm_ref[idx], sink).astype(BF16)
        m = m_b.astype(F32)
        p = jnp.exp2(s - m_b)
        v_win = jnp.concatenate([v_ref[0, c_prev, hs, :], v_ref[0, c, hs, :]], axis=1)
        pv = _dot(jnp.concatenate([v_win, ones_blk], axis=0), p)
        o = pv[0:HEAD_DIM] / (pv[HEAD_DIM:HEAD_DIM + 1] + jnp.exp2(sink - m))
        for e in range(2):
            r = (g * SWA_GROUP + 2 * hp + e) * HEAD_DIM
            ot_ref[slot, r:r + HEAD_DIM, :] = o[:, e * L:(e + 1) * L]

    def emit(c, slot):
        o_ref[0, pl.ds(pl.multiple_of(c * L, L), L), :] = ot_ref[slot].T.astype(o_ref.dtype)

    for idx in range(len(chains)):
        scores_first(idx)
    U = SWA_UNROLL
    for idx in range(len(chains)):
        attend(0, 0, idx, U - 1)
        scores(1, idx)

    def body(c, e):
        emit(c - 1, (e - 1) % U)
        for idx in range(len(chains)):
            attend(c, c - 1, idx, e)
            scores(c + 1, idx)

    def body_unrolled(u, carry):
        for e in range(U):
            body(U * u + 1 + e, e)
        return carry

    lax.fori_loop(0, (nblk - 2) // U, body_unrolled, 0)
    emit(nblk - 2, U - 1)
    for idx in range(len(chains)):
        attend(nblk - 1, nblk - 2, idx, 0)
    emit(nblk - 1, 0)


def _swa(qa_t, ka3, va_t, rows, sink_rows):
    batch, nblk = qa_t.shape[0], qa_t.shape[1]
    seq = nblk * SWA_BLOCK
    assert nblk >= 2 and (nblk - 2) % SWA_UNROLL == 0 and SWA_UNROLL >= 2
    return pl.pallas_call(
        _swa_kernel,
        grid=(batch,),
        in_specs=[
            pl.BlockSpec((1, nblk, SWA_Q_W, SWA_BLOCK), lambda b: (b, 0, 0, 0)),
            pl.BlockSpec((1, seq, SWA_KV_W), lambda b: (b, 0, 0)),
            pl.BlockSpec((1, nblk, SWA_KV_W, SWA_BLOCK), lambda b: (b, 0, 0, 0)),
            _resident((1, N_ATTN_HEADS, 2 * MOBA_BLOCK), lambda b: (0, 0, 0)),
            _resident(sink_rows.shape, lambda b: (0, 0, 0)),
        ],
        out_specs=pl.BlockSpec((1, seq, SWA_Q_W), lambda b: (b, 0, 0)),
        out_shape=jax.ShapeDtypeStruct((batch, seq, SWA_Q_W), BF16),
        scratch_shapes=[pltpu.VMEM((SWA_KV_HEADS, 2 * SWA_BLOCK, SWA_GROUP * SWA_BLOCK), BF16),
                        pltpu.VMEM((SWA_UNROLL, SWA_Q_W, SWA_BLOCK), F32),
                        pltpu.VMEM((SWA_Q_HEADS // 2, 2 * SWA_BLOCK, 2 * SWA_BLOCK), BF16),
                        pltpu.VMEM((SWA_Q_HEADS // 2, 1, 2 * SWA_BLOCK), F32)],
        compiler_params=pltpu.CompilerParams(dimension_semantics=("arbitrary",),
                                             vmem_limit_bytes=VMEM_LIMIT_BYTES),
        name="swa",
    )(qa_t, ka3, va_t, rows, sink_rows)


def _moba_kernel(q_ref, qn_ref, k_ref, v_ref, sel_ref, rows_ref, o_ref, bias_ref, m_ref, acc_ref, s_ref, cm_ref):
    step = pl.program_id(1)
    last = pl.num_programs(1) - 1
    MB = MOBA_BLOCK
    H = MOBA_HEADS
    G = MOBA_Q_PER_STEP

    @pl.when(jnp.logical_and(pl.program_id(0) == 0, step == 0))
    def _():
        for h in range(H):
            for d in range(BIAS_SLOTS):
                bias_ref[h, d] = _toeplitz(rows_ref[d, SWA_Q_HEADS + h:SWA_Q_HEADS + h + 1, :], MB).astype(BF16)

    m_ref[...] = jnp.full(m_ref.shape, NEG_INF, F32)
    acc_ref[...] = jnp.zeros(acc_ref.shape, F32)
    heads = [slice(h * HEAD_DIM, (h + 1) * HEAD_DIM) for h in range(H)]
    ones_blk = jnp.concatenate([jnp.ones((1, MB), BF16), jnp.zeros((ACC_ROWS - HEAD_DIM - 1, MB), BF16)],
                               axis=0)

    def sel_row(j, qs, h):
        return sel_ref[0, h, pl.ds(j, 1), qs * MB:(qs + 1) * MB]

    def scores(j, qs, h):
        c = qs * H + h
        rows = pl.ds(pl.multiple_of(j * MB, MB), MB)
        slot = jnp.minimum(G * step + qs - j, BIAS_SLOTS - 1)
        s = _dot(k_ref[0, rows, heads[h]], q_ref[0, qs, heads[h], :]).astype(BF16) + bias_ref[h, slot]
        s_ref[c] = s
        cm_ref[c] = jnp.max(s, axis=0, keepdims=True).astype(F32) + sel_row(j, qs, h)

    def own_scores(blk, q_blk, c, h):
        rows = pl.ds(pl.multiple_of(blk * MB, MB), MB)
        s = _dot(k_ref[0, rows, heads[h]], q_blk[heads[h], :]).astype(BF16) + bias_ref[h, 0]
        s_ref[c] = s
        cm_ref[c] = jnp.max(s, axis=0, keepdims=True).astype(F32)

    def accumulate(j, qs, h):
        c = qs * H + h
        m_old = m_ref[c]
        m_new = jnp.maximum(m_old, cm_ref[c])
        alpha = jnp.exp2(m_old - m_new)
        p = jnp.exp2(s_ref[c] - (m_new - sel_row(j, qs, h)).astype(BF16))
        v_aug = jnp.concatenate([v_ref[0, j, heads[h], :], ones_blk], axis=0)
        rs = slice(c * ACC_ROWS, (c + 1) * ACC_ROWS)
        acc_ref[rs, :] = alpha * acc_ref[rs, :] + _dot(v_aug, p)
        m_ref[c] = m_new

    first = G * step

    @pl.when(step == 0)
    def _():
        for qs in range(G):
            for h in range(H):
                own_scores(qs, q_ref[0, qs], qs * H + h, h)

    for n in range(G - 1):
        for qs in range(G - 1 - n, G):
            for h in range(H):
                accumulate(first + G - 1 - n, qs, h)
                scores(first + G - 2 - n, qs, h)

    def body(j, carry):
        for qs in range(G):
            for h in range(H):
                accumulate(j, qs, h)
                scores(j - 1, qs, h)
        return carry

    def body2(u, carry):
        return body(first - 2 * u - 1, body(first - 2 * u, carry))

    lax.fori_loop(0, (G // 2) * step, body2, 0)

    @pl.when(step < last)
    def _():
        for h in range(H):
            for qs in range(G):
                accumulate(0, qs, h)
                own_scores(first + G + qs, qn_ref[0, qs], qs * H + h, h)

    @pl.when(step == last)
    def _():
        for h in range(H):
            for qs in range(G):
                accumulate(0, qs, h)

    for qs in range(G):
        outs = []
        for h in range(H):
            base = (qs * H + h) * ACC_ROWS
            outs.append(acc_ref[base:base + HEAD_DIM, :] / acc_ref[base + HEAD_DIM:base + HEAD_DIM + 1, :])
        o_ref[0, qs * MB:(qs + 1) * MB, :] = jnp.concatenate(outs, axis=0).T.astype(o_ref.dtype)


def _moba(qb_t, kb3, vb_t, sel, rows):
    batch, nblk = qb_t.shape[0], qb_t.shape[1]
    seq = nblk * MOBA_BLOCK
    g = MOBA_Q_PER_STEP
    assert g % 2 == 0 and nblk % g == 0
    chains = g * MOBA_HEADS
    return pl.pallas_call(
        _moba_kernel,
        grid=(batch, nblk // g),
        in_specs=[
            pl.BlockSpec((1, g, MOBA_W, MOBA_BLOCK), lambda b, i: (b, i, 0, 0)),
            pl.BlockSpec((1, g, MOBA_W, MOBA_BLOCK), lambda b, i: (b, jnp.minimum(i + 1, nblk // g - 1), 0, 0)),
            pl.BlockSpec((1, seq, MOBA_W), lambda b, i: (b, 0, 0)),
            pl.BlockSpec((1, nblk, MOBA_W, MOBA_BLOCK), lambda b, i: (b, 0, 0, 0)),
            pl.BlockSpec((1, MOBA_HEADS, nblk, g * MOBA_BLOCK), lambda b, i: (b, 0, 0, i)),
            _resident(rows.shape, lambda b, i: (0, 0, 0)),
        ],
        out_specs=pl.BlockSpec((1, g * MOBA_BLOCK, MOBA_W), lambda b, i: (b, i, 0)),
        out_shape=jax.ShapeDtypeStruct((batch, seq, MOBA_W), BF16),
        scratch_shapes=[pltpu.VMEM((MOBA_HEADS, BIAS_SLOTS, MOBA_BLOCK, MOBA_BLOCK), BF16),
                        pltpu.VMEM((chains, 1, MOBA_BLOCK), F32),
                        pltpu.VMEM((chains * ACC_ROWS, MOBA_BLOCK), F32),
                        pltpu.VMEM((chains, MOBA_BLOCK, MOBA_BLOCK), BF16),
                        pltpu.VMEM((chains, 1, MOBA_BLOCK), F32)],
        compiler_params=pltpu.CompilerParams(dimension_semantics=("arbitrary", "arbitrary"),
                                             vmem_limit_bytes=VMEM_LIMIT_BYTES),
        name="moba",
    )(qb_t, qb_t, kb3, vb_t, sel, rows)


def _post_kernel(x_ref, ya_ref, yb_ref, ga_ref, gb_ref, gate1_ref, shift2_ref, scale2_ref,
                 gate2_ref, g2_ref, gf_ref, wa_ref, wb_ref, wo_ref, w1_ref, w2_ref, o_ref):
    a = _dot(ya_ref[...], wa_ref[...])
    b = _dot(yb_ref[...], wb_ref[...])
    merged = (jax.nn.sigmoid(ga_ref[...].astype(F32)) * a
              + jax.nn.sigmoid(gb_ref[...].astype(F32)) * b)
    x1 = x_ref[...] + gate1_ref[0, 0] * _dot(merged.astype(BF16), wo_ref[...])
    h2 = _rms_modulate(x1, g2_ref[...], shift2_ref[0, 0], scale2_ref[0, 0]).astype(BF16)
    y = jnp.zeros_like(x1)
    for c in range(D_FF // FF_CHUNK):
        cs = slice(c * FF_CHUNK, (c + 1) * FF_CHUNK)
        u = jnp.square(jnp.maximum(_dot(h2, w1_ref[:, cs]), 0.0)).astype(BF16)
        y = y + _dot(u, w2_ref[cs, :])
    x2 = x1 + gate2_ref[0, 0] * y
    ms = jnp.mean(x2 * x2, axis=-1, keepdims=True)
    o_ref[...] = (x2 * lax.rsqrt(ms + RMS_EPS)) * gf_ref[...]


def _post(x2, ya, yb, ga, gb, mod4, g2, gf, wa, wb, wo, w1, w2, seq):
    tokens = x2.shape[0]
    tiles_per_seq = seq // TOKEN_TILE
    tok = lambda w: pl.BlockSpec((TOKEN_TILE, w), lambda i: (i, 0))
    modrow = lambda k: pl.BlockSpec((1, 1, 1, D_MODEL), lambda i: (i // tiles_per_seq, k, 0, 0))
    full = lambda a: _resident(a.shape, lambda i: (0, 0))
    return pl.pallas_call(
        _post_kernel,
        grid=(tokens // TOKEN_TILE,),
        in_specs=[tok(D_MODEL), tok(SWA_Q_W), tok(MOBA_W), tok(D_MODEL), tok(D_MODEL),
                  modrow(2), modrow(3), modrow(4), modrow(5),
                  full(g2), full(gf), full(wa), full(wb), full(wo), full(w1), full(w2)],
        out_specs=tok(D_MODEL),
        out_shape=jax.ShapeDtypeStruct((tokens, D_MODEL), F32),
        compiler_params=pltpu.CompilerParams(dimension_semantics=("arbitrary",),
                                             vmem_limit_bytes=VMEM_LIMIT_BYTES),
        name="post",
    )(x2, ya, yb, ga, gb, mod4, mod4, mod4, mod4, g2, gf, wa, wb, wo, w1, w2)


def kernel(x, c, ada_w, ada_b, norm1_g, norm2_g, w_in, attn_sinks, rel_bias, w_branch_a,
           w_branch_b, w_out, w_mlp_in, w_mlp_out, final_g):
    batch, seq, _ = x.shape
    depth = ada_w.shape[0]
    assert seq % TOKEN_TILE == 0 and TOKEN_TILE % MOBA_BLOCK == 0 and batch <= SUBLANES
    far = np.arange((BIAS_SLOTS - 1) * MOBA_BLOCK - (MOBA_BLOCK - 1), max(seq, BIAS_SLOTS * MOBA_BLOCK))
    assert np.all(_t5_bucket(far) == _t5_bucket(far[:1]))
    assert 2 * SWA_BLOCK <= MOBA_BLOCK and SWA_WINDOW == SWA_BLOCK
    assert depth == 1
    l = 0

    rows = _bias_rows(rel_bias)
    c_pad = jnp.zeros((SUBLANES, D_MODEL), F32).at[:batch].set(c.astype(F32))

    x2 = x.reshape(batch * seq, D_MODEL)
    mod = _ada(c_pad, ada_w[l], ada_b[l][None, :])
    mod4 = mod[:batch].reshape(batch, N_MOD, 1, D_MODEL)
    later_weights = [w.astype(F32) for w in (w_branch_a[l], w_branch_b[l], w_out[l], w_mlp_in[l], w_mlp_out[l])]
    ka, kb, ga, gb, qa_t, va_t, qb_t, vb_t, sel, wa, wb, wo, w1, w2 = _proj(
        x2, mod4, norm1_g[l][None, :], w_in[l].astype(F32), later_weights, batch, seq)

    sink_rows = jnp.broadcast_to(
        (attn_sinks[l].astype(F32) * LOG2E).reshape(SWA_KV_HEADS, 1, SWA_GROUP, 1),
        (SWA_KV_HEADS, 1, SWA_GROUP, SWA_BLOCK)).reshape(SWA_KV_HEADS, 1, SWA_GROUP * SWA_BLOCK)
    ya = _swa(qa_t, ka.reshape(batch, seq, SWA_KV_W), va_t, rows, sink_rows)

    yb = _moba(qb_t, kb.reshape(batch, seq, MOBA_W), vb_t, sel, rows)

    out = _post(x2, ya.reshape(batch * seq, SWA_Q_W), yb.reshape(batch * seq, MOBA_W), ga, gb,
                mod4, norm2_g[l][None, :], final_g[None, :], wa, wb, wo, w1, w2, seq)
    return out.reshape(batch, seq, D_MODEL)
```

```python
import math

import jax
import jax.numpy as jnp
import numpy as np
from jax import lax
from jax.experimental import pallas as pl
from jax.experimental.pallas import tpu as pltpu

D_MODEL = 1024
HEAD_DIM = 64
ATTN_SCALE = HEAD_DIM ** -0.5
LOG2E = math.log2(math.e)
SWA_Q_HEADS = 8
SWA_KV_HEADS = 2
SWA_GROUP = SWA_Q_HEADS // SWA_KV_HEADS
SWA_WINDOW = 128
SWA_BLOCK = 128
MOBA_HEADS = 8
MOBA_BLOCK = 256
MOBA_TOPK = 3
NUM_BUCKETS = 32
MAX_EXACT = NUM_BUCKETS // 2
MAX_DISTANCE = 2048
N_ATTN_HEADS = SWA_Q_HEADS + MOBA_HEADS
SWA_Q_W = SWA_Q_HEADS * HEAD_DIM
SWA_KV_W = SWA_KV_HEADS * HEAD_DIM
MOBA_W = MOBA_HEADS * HEAD_DIM
D_FF = 4 * D_MODEL
N_MOD = 6
RMS_EPS = 1e-6

VMEM_LIMIT_BYTES = 56 * 1024 * 1024
TOKEN_TILE = 512
FF_CHUNK = 1024
SUBLANES = 8
ADA_COLS = 1536
BIAS_SLOTS = 8
SWA_UNROLL = 3
MOBA_Q_PER_STEP = 4
ACC_ROWS = HEAD_DIM + 16
NEG_INF = float("-inf")

F32 = jnp.float32
BF16 = jnp.bfloat16


def _resident(block_shape, index_map):
    return pl.BlockSpec(block_shape, index_map, pipeline_mode=pl.Buffered(1))


def _dot(a, b):
    return jnp.dot(a, b, preferred_element_type=F32)


def _dot_nt(a, b):
    return lax.dot_general(a, b, (((1,), (1,)), ((), ())), preferred_element_type=F32)


def _rms_modulate(xv, g, shift, scale):
    ms = jnp.mean(xv * xv, axis=-1, keepdims=True)
    y = xv * lax.rsqrt(ms + RMS_EPS)
    return (y * g) * (1.0 + scale) + shift


def _toeplitz(u, rows):
    wide = jnp.broadcast_to(u, (rows, 2 * rows))
    return pltpu.roll(wide, 0, 1, stride=1, stride_axis=0)[:, rows:]


def _ada_kernel(c_ref, w_ref, b_ref, o_ref):
    cs = jax.nn.silu(c_ref[...]).astype(BF16)
    o_ref[...] = _dot(cs, w_ref[...].astype(BF16)) + b_ref[...]


def _ada(c_pad, w, b):
    n = w.shape[1]
    bn = ADA_COLS
    assert n % bn == 0
    return pl.pallas_call(
        _ada_kernel,
        grid=(n // bn,),
        in_specs=[pl.BlockSpec((SUBLANES, D_MODEL), lambda j: (0, 0)),
                  pl.BlockSpec((D_MODEL, bn), lambda j: (0, j)),
                  pl.BlockSpec((1, bn), lambda j: (0, j))],
        out_specs=pl.BlockSpec((SUBLANES, bn), lambda j: (0, j)),
        out_shape=jax.ShapeDtypeStruct((SUBLANES, n), F32),
        compiler_params=pltpu.CompilerParams(dimension_semantics=("arbitrary",),
                                             vmem_limit_bytes=VMEM_LIMIT_BYTES),
        name="ada",
    )(c_pad, w, b)


def _t5_bucket(dist):
    n = np.maximum(dist, 0)
    nf = np.maximum(n, 1).astype(np.float32)
    large = MAX_EXACT + (np.log(nf / np.float32(MAX_EXACT)) / np.float32(math.log(MAX_DISTANCE / MAX_EXACT))
                         * np.float32(NUM_BUCKETS - MAX_EXACT)).astype(np.int32)
    large = np.minimum(large, NUM_BUCKETS - 1)
    return np.where(n < MAX_EXACT, n, large).astype(np.int32)


def _bias_rows_kernel(bucket_ref, rbt_ref, o_ref):
    bucket = bucket_ref[...]
    n = bucket.shape[1]
    onehot = jnp.where(lax.broadcasted_iota(jnp.int32, (NUM_BUCKETS, n), 0) == bucket, 1.0, 0.0)
    rows = jnp.dot(rbt_ref[...], onehot, preferred_element_type=F32,
                   precision=lax.Precision.HIGHEST)
    rows = jnp.where(bucket >= 0, rows * LOG2E, NEG_INF)
    for d in range(BIAS_SLOTS):
        o_ref[d] = rows[:, d * MOBA_BLOCK:(d + 2) * MOBA_BLOCK]


def _bias_rows(rel_bias):
    n = (BIAS_SLOTS + 1) * MOBA_BLOCK
    dist = np.arange(n, dtype=np.int32) - MOBA_BLOCK
    bucket = jnp.asarray(np.where(dist >= 0, _t5_bucket(dist), -1)[None, :], dtype=jnp.int32)
    return pl.pallas_call(
        _bias_rows_kernel,
        out_shape=jax.ShapeDtypeStruct((BIAS_SLOTS, N_ATTN_HEADS, 2 * MOBA_BLOCK), F32),
        name="bias_rows",
    )(bucket, rel_bias.astype(F32).T)


IN_WIDTHS = (SWA_Q_W, SWA_KV_W, SWA_KV_W, MOBA_W, MOBA_W, MOBA_W, D_MODEL, D_MODEL)
IN_OFFS = tuple(sum(IN_WIDTHS[:k]) for k in range(len(IN_WIDTHS) + 1))
FEAT_COLS = (0, 2, 3, 5)
FEAT_W = sum(IN_WIDTHS[k] for k in FEAT_COLS)
TOK_COLS = (1, 4, 6, 7)
TOK_W = sum(IN_WIDTHS[k] for k in TOK_COLS)
Q_COLS = (0, 3)
BLOCKS_PER_TILE = TOKEN_TILE // MOBA_BLOCK


def _proj_kernel(x_ref, shift_ref, scale_ref, g_ref, w_ref, *refs):
    n_cast = (len(refs) - 12) // 2
    cast_in, refs = refs[:n_cast], refs[n_cast:]
    ka_ref, kb_ref, ga_ref, gb_ref, qa_ref, va_ref, qb_ref, vb_ref, sel_ref = refs[:9]
    cast_out, (wk_ref, wt_ref, kmean_ref) = refs[9:9 + n_cast], refs[9 + n_cast:]

    for src, dst in zip(cast_in, cast_out):
        dst[...] = src[...].astype(dst.dtype)

    tile = pl.program_id(0) % (kmean_ref.shape[0] // BLOCKS_PER_TILE)

    @pl.when(tile == 0)
    def _():
        kmean_ref[...] = jnp.zeros(kmean_ref.shape, F32)

    @pl.when(pl.program_id(0) == 0)
    def _():
        o = 0
        for k in TOK_COLS:
            wk_ref[:, o:o + IN_WIDTHS[k]] = w_ref[:, IN_OFFS[k]:IN_OFFS[k + 1]].astype(BF16)
            o += IN_WIDTHS[k]
        o = 0
        for k in FEAT_COLS:
            wcol = w_ref[:, IN_OFFS[k]:IN_OFFS[k + 1]]
            if k in Q_COLS:
                wcol = wcol * (ATTN_SCALE * LOG2E)
            wt_ref[o:o + IN_WIDTHS[k], :] = wcol.T.astype(BF16)
            o += IN_WIDTHS[k]

    h = _rms_modulate(x_ref[...], g_ref[...], shift_ref[0, 0], scale_ref[0, 0]).astype(BF16)
    tok_refs = dict(zip(TOK_COLS, (ka_ref, kb_ref, ga_ref, gb_ref)))
    feat_refs = dict(zip(FEAT_COLS, (qa_ref, va_ref, qb_ref, vb_ref)))

    def token_major(k):
        o = sum(IN_WIDTHS[c] for c in TOK_COLS[:TOK_COLS.index(k)])
        out = _dot(h, wk_ref[:, o:o + IN_WIDTHS[k]])
        tok_refs[k][...] = out.astype(BF16)
        return out

    def feature_major(k):
        o = sum(IN_WIDTHS[c] for c in FEAT_COLS[:FEAT_COLS.index(k)])
        out = _dot_nt(wt_ref[o:o + IN_WIDTHS[k], :], h).astype(BF16)
        blk = feat_refs[k].shape[-1]
        for t in range(TOKEN_TILE // blk):
            feat_refs[k][0, t] = out[:, t * blk:(t + 1) * blk]
        return out

    _update_kmean(token_major(4), kmean_ref, tile)
    qb_t = feature_major(3)
    token_major(1)
    token_major(6)
    _moba_select(qb_t, sel_ref, kmean_ref, tile)
    token_major(7)
    for k in (0, 2, 5):
        feature_major(k)


def _update_kmean(kb, kmean_ref, tile):
    for t in range(BLOCKS_PER_TILE):
        kmean_ref[pl.ds(tile * BLOCKS_PER_TILE + t, 1), :] = jnp.mean(
            kb[t * MOBA_BLOCK:(t + 1) * MOBA_BLOCK, :], axis=0, keepdims=True)


def _moba_select(qb_t, sel_ref, kmean_ref, tile):
    nblk = kmean_ref.shape[0]
    blocks_per_tile = BLOCKS_PER_TILE
    blk = lax.broadcasted_iota(jnp.int32, (nblk, TOKEN_TILE), 0)
    qblk = tile * blocks_per_tile + lax.broadcasted_iota(jnp.int32, (nblk, TOKEN_TILE), 1) // MOBA_BLOCK
    past = blk < qblk
    own = jnp.where(blk == qblk, 0.0, NEG_INF)
    blk_f = blk.astype(F32)
    for h in range(MOBA_HEADS):
        hs = slice(h * HEAD_DIM, (h + 1) * HEAD_DIM)
        kmean = kmean_ref[:, hs]
        gate = _dot(kmean.astype(BF16), qb_t[hs, :])
        gate = jnp.where(past, gate, NEG_INF)
        mask = own
        for _ in range(MOBA_TOPK):
            top = jnp.max(gate, axis=0, keepdims=True)
            first = jnp.min(jnp.where(gate == top, blk_f, float(nblk)), axis=0, keepdims=True)
            pick = blk_f == first
            mask = jnp.where(jnp.logical_and(pick, past), 0.0, mask)
            gate = jnp.where(pick, NEG_INF, gate)
        sel_ref[0, h] = mask


def _proj(x2, mod4, g1, w_in, cast_weights, batch, seq):
    tokens = x2.shape[0]
    steps = tokens // TOKEN_TILE
    assert all(w.shape[0] % (16 * steps) == 0 for w in cast_weights)
    cast_specs = [pl.BlockSpec((w.shape[0] // steps, w.shape[1]), lambda i: (i, 0)) for w in cast_weights]
    tiles_per_seq = seq // TOKEN_TILE
    na = TOKEN_TILE // SWA_BLOCK
    nb = TOKEN_TILE // MOBA_BLOCK

    def feat_spec(width, blk, per_tile):
        return pl.BlockSpec((1, per_tile, width, blk),
                            lambda i: (i // tiles_per_seq, i % tiles_per_seq, 0, 0))

    out_shape = (
        jax.ShapeDtypeStruct((tokens, SWA_KV_W), BF16),
        jax.ShapeDtypeStruct((tokens, MOBA_W), BF16),
        jax.ShapeDtypeStruct((tokens, D_MODEL), BF16),
        jax.ShapeDtypeStruct((tokens, D_MODEL), BF16),
        jax.ShapeDtypeStruct((batch, seq // SWA_BLOCK, SWA_Q_W, SWA_BLOCK), BF16),
        jax.ShapeDtypeStruct((batch, seq // SWA_BLOCK, SWA_KV_W, SWA_BLOCK), BF16),
        jax.ShapeDtypeStruct((batch, seq // MOBA_BLOCK, MOBA_W, MOBA_BLOCK), BF16),
        jax.ShapeDtypeStruct((batch, seq // MOBA_BLOCK, MOBA_W, MOBA_BLOCK), BF16),
        jax.ShapeDtypeStruct((batch, MOBA_HEADS, seq // MOBA_BLOCK, seq), F32),
    )
    tok_spec = lambda w: pl.BlockSpec((TOKEN_TILE, w), lambda i: (i, 0))
    return pl.pallas_call(
        _proj_kernel,
        grid=(steps,),
        in_specs=[
            pl.BlockSpec((TOKEN_TILE, D_MODEL), lambda i: (i, 0)),
            pl.BlockSpec((1, 1, 1, D_MODEL), lambda i: (i // tiles_per_seq, 0, 0, 0)),
            pl.BlockSpec((1, 1, 1, D_MODEL), lambda i: (i // tiles_per_seq, 1, 0, 0)),
            _resident((1, D_MODEL), lambda i: (0, 0)),
            _resident(w_in.shape, lambda i: (0, 0)),
        ] + cast_specs,
        out_specs=(tok_spec(SWA_KV_W), tok_spec(MOBA_W), tok_spec(D_MODEL), tok_spec(D_MODEL),
                   feat_spec(SWA_Q_W, SWA_BLOCK, na), feat_spec(SWA_KV_W, SWA_BLOCK, na),
                   feat_spec(MOBA_W, MOBA_BLOCK, nb), feat_spec(MOBA_W, MOBA_BLOCK, nb),
                   pl.BlockSpec((1, MOBA_HEADS, seq // MOBA_BLOCK, TOKEN_TILE),
                                lambda i: (i // tiles_per_seq, 0, 0, i % tiles_per_seq)))
        + tuple(cast_specs),
        out_shape=out_shape + tuple(jax.ShapeDtypeStruct(w.shape, BF16) for w in cast_weights),
        scratch_shapes=[pltpu.VMEM((D_MODEL, TOK_W), BF16), pltpu.VMEM((FEAT_W, D_MODEL), BF16),
                        pltpu.VMEM((seq // MOBA_BLOCK, MOBA_W), F32)],
        compiler_params=pltpu.CompilerParams(dimension_semantics=("arbitrary",),
                                             vmem_limit_bytes=VMEM_LIMIT_BYTES),
        name="proj",
    )(x2, mod4, mod4, g1, w_in, *cast_weights)


def _swa_kernel(q_ref, k_ref, v_ref, rows_ref, sink_ref, o_ref, bias_ref, ot_ref, s_ref, cm_ref):
    L = SWA_BLOCK
    nblk = q_ref.shape[1]

    @pl.when(pl.program_id(0) == 0)
    def _():
        k_idx = lax.broadcasted_iota(jnp.int32, (L, L), 0)
        q_idx = lax.broadcasted_iota(jnp.int32, (L, L), 1)
        for h in range(SWA_Q_HEADS):
            g, gi = divmod(h, SWA_GROUP)
            t0 = rows_ref[0, h:h + 1, MOBA_BLOCK:MOBA_BLOCK + L]
            r = _toeplitz(jnp.concatenate([t0, t0], axis=1), L)
            cols = slice(gi * L, (gi + 1) * L)
            bias_ref[g, 0:L, cols] = jnp.where(q_idx < k_idx, r, NEG_INF).astype(BF16)
            bias_ref[g, L:2 * L, cols] = jnp.where(q_idx >= k_idx, r, NEG_INF).astype(BF16)

    ones_blk = jnp.concatenate([jnp.ones((1, 2 * L), BF16), jnp.zeros((ACC_ROWS - HEAD_DIM - 1, 2 * L), BF16)],
                               axis=0)
    PW = 2 * L
    chains = [(g, hp) for g in range(SWA_KV_HEADS) for hp in range(SWA_GROUP // 2)]
    kv_cols = [slice(g * HEAD_DIM, (g + 1) * HEAD_DIM) for g in range(SWA_KV_HEADS)]

    def q_pair(c, g, hp):
        h0 = g * SWA_GROUP + 2 * hp
        return jnp.concatenate([q_ref[0, c, (h0 + e) * HEAD_DIM:(h0 + e + 1) * HEAD_DIM, :] for e in range(2)],
                               axis=1)

    def scores_first(idx):
        g, hp = chains[idx]
        lanes = slice(hp * PW, (hp + 1) * PW)
        s_ref[idx, 0:L, :] = jnp.full((L, PW), NEG_INF, BF16)
        s = _dot(k_ref[0, 0:L, kv_cols[g]], q_pair(0, g, hp)).astype(BF16) + bias_ref[g, L:2 * L, lanes]
        s_ref[idx, L:2 * L, :] = s
        cm_ref[idx] = jnp.max(s, axis=0, keepdims=True).astype(F32)

    def scores(c, idx):
        g, hp = chains[idx]
        kw = k_ref[0, pl.ds(pl.multiple_of((c - 1) * L, L), 2 * L), kv_cols[g]]
        s = _dot(kw, q_pair(c, g, hp)).astype(BF16) + bias_ref[g, :, hp * PW:(hp + 1) * PW]
        s_ref[idx] = s
        cm_ref[idx] = jnp.max(s, axis=0, keepdims=True).astype(F32)

    def attend(c, c_prev, idx, slot):
        g, hp = chains[idx]
        hs = kv_cols[g]
        s = s_ref[idx]
        sink = sink_ref[g, :, hp * PW:(hp + 1) * PW]
        m_b = jnp.maximum(cm_ref[idx], sink).astype(BF16)
        m = m_b.astype(F32)
        p = jnp.exp2(s - m_b)
        v_win = jnp.concatenate([v_ref[0, c_prev, hs, :], v_ref[0, c, hs, :]], axis=1)
        pv = _dot(jnp.concatenate([v_win, ones_blk], axis=0), p)
        o = pv[0:HEAD_DIM] / (pv[HEAD_DIM:HEAD_DIM + 1] + jnp.exp2(sink - m))
        for e in range(2):
            r = (g * SWA_GROUP + 2 * hp + e) * HEAD_DIM
            ot_ref[slot, r:r + HEAD_DIM, :] = o[:, e * L:(e + 1) * L]

    def emit(c, slot):
        o_ref[0, pl.ds(pl.multiple_of(c * L, L), L), :] = ot_ref[slot].T.astype(o_ref.dtype)

    for idx in range(len(chains)):
        scores_first(idx)
    U = SWA_UNROLL
    for idx in range(len(chains)):
        attend(0, 0, idx, U - 1)
        scores(1, idx)

    def body(c, e):
        emit(c - 1, (e - 1) % U)
        for idx in range(len(chains)):
            attend(c, c - 1, idx, e)
            scores(c + 1, idx)

    def body_unrolled(u, carry):
        for e in range(U):
            body(U * u + 1 + e, e)
        return carry

    lax.fori_loop(0, (nblk - 2) // U, body_unrolled, 0)
    emit(nblk - 2, U - 1)
    for idx in range(len(chains)):
        attend(nblk - 1, nblk - 2, idx, 0)
    emit(nblk - 1, 0)


def _swa(qa_t, ka3, va_t, rows, sink_rows):
    batch, nblk = qa_t.shape[0], qa_t.shape[1]
    seq = nblk * SWA_BLOCK
    assert nblk >= 2 and (nblk - 2) % SWA_UNROLL == 0 and SWA_UNROLL >= 2
    return pl.pallas_call(
        _swa_kernel,
        grid=(batch,),
        in_specs=[
            pl.BlockSpec((1, nblk, SWA_Q_W, SWA_BLOCK), lambda b: (b, 0, 0, 0)),
            pl.BlockSpec((1, seq, SWA_KV_W), lambda b: (b, 0, 0)),
            pl.BlockSpec((1, nblk, SWA_KV_W, SWA_BLOCK), lambda b: (b, 0, 0, 0)),
            _resident((1, N_ATTN_HEADS, 2 * MOBA_BLOCK), lambda b: (0, 0, 0)),
            _resident(sink_rows.shape, lambda b: (0, 0, 0)),
        ],
        out_specs=pl.BlockSpec((1, seq, SWA_Q_W), lambda b: (b, 0, 0)),
        out_shape=jax.ShapeDtypeStruct((batch, seq, SWA_Q_W), BF16),
        scratch_shapes=[pltpu.VMEM((SWA_KV_HEADS, 2 * SWA_BLOCK, SWA_GROUP * SWA_BLOCK), BF16),
                        pltpu.VMEM((SWA_UNROLL, SWA_Q_W, SWA_BLOCK), F32),
                        pltpu.VMEM((SWA_Q_HEADS // 2, 2 * SWA_BLOCK, 2 * SWA_BLOCK), BF16),
                        pltpu.VMEM((SWA_Q_HEADS // 2, 1, 2 * SWA_BLOCK), F32)],
        compiler_params=pltpu.CompilerParams(dimension_semantics=("arbitrary",),
                                             vmem_limit_bytes=VMEM_LIMIT_BYTES),
        name="swa",
    )(qa_t, ka3, va_t, rows, sink_rows)


def _moba_kernel(q_ref, qn_ref, k_ref, v_ref, sel_ref, rows_ref, o_ref, bias_ref, m_ref, acc_ref, s_ref, cm_ref):
    step = pl.program_id(1)
    last = pl.num_programs(1) - 1
    MB = MOBA_BLOCK
    H = MOBA_HEADS
    G = MOBA_Q_PER_STEP

    @pl.when(jnp.logical_and(pl.program_id(0) == 0, step == 0))
    def _():
        for h in range(H):
            for d in range(BIAS_SLOTS):
                bias_ref[h, d] = _toeplitz(rows_ref[d, SWA_Q_HEADS + h:SWA_Q_HEADS + h + 1, :], MB).astype(BF16)

    m_ref[...] = jnp.full(m_ref.shape, NEG_INF, F32)
    acc_ref[...] = jnp.zeros(acc_ref.shape, F32)
    heads = [slice(h * HEAD_DIM, (h + 1) * HEAD_DIM) for h in range(H)]
    ones_blk = jnp.concatenate([jnp.ones((1, MB), BF16), jnp.zeros((ACC_ROWS - HEAD_DIM - 1, MB), BF16)],
                               axis=0)

    def sel_row(j, qs, h):
        return sel_ref[0, h, pl.ds(j, 1), qs * MB:(qs + 1) * MB]

    def scores(j, qs, h):
        c = qs * H + h
        rows = pl.ds(pl.multiple_of(j * MB, MB), MB)
        slot = jnp.minimum(G * step + qs - j, BIAS_SLOTS - 1)
        s = _dot(k_ref[0, rows, heads[h]], q_ref[0, qs, heads[h], :]).astype(BF16) + bias_ref[h, slot]
        s_ref[c] = s
        cm_ref[c] = jnp.max(s, axis=0, keepdims=True).astype(F32) + sel_row(j, qs, h)

    def own_scores(blk, q_blk, c, h):
        rows = pl.ds(pl.multiple_of(blk * MB, MB), MB)
        s = _dot(k_ref[0, rows, heads[h]], q_blk[heads[h], :]).astype(BF16) + bias_ref[h, 0]
        s_ref[c] = s
        cm_ref[c] = jnp.max(s, axis=0, keepdims=True).astype(F32)

    def accumulate(j, qs, h):
        c = qs * H + h
        m_old = m_ref[c]
        m_new = jnp.maximum(m_old, cm_ref[c])
        alpha = jnp.exp2(m_old - m_new)
        p = jnp.exp2(s_ref[c] - (m_new - sel_row(j, qs, h)).astype(BF16))
        v_aug = jnp.concatenate([v_ref[0, j, heads[h], :], ones_blk], axis=0)
        rs = slice(c * ACC_ROWS, (c + 1) * ACC_ROWS)
        acc_ref[rs, :] = alpha * acc_ref[rs, :] + _dot(v_aug, p)
        m_ref[c] = m_new

    first = G * step

    @pl.when(step == 0)
    def _():
        for qs in range(G):
            for h in range(H):
                own_scores(qs, q_ref[0, qs], qs * H + h, h)

    for n in range(G - 1):
        for qs in range(G - 1 - n, G):
            for h in range(H):
                accumulate(first + G - 1 - n, qs, h)
                scores(first + G - 2 - n, qs, h)

    def body(j, carry):
        for qs in range(G):
            for h in range(H):
                accumulate(j, qs, h)
                scores(j - 1, qs, h)
        return carry

    def body2(u, carry):
        return body(first - 2 * u - 1, body(first - 2 * u, carry))

    lax.fori_loop(0, (G // 2) * step, body2, 0)

    @pl.when(step < last)
    def _():
        for h in range(H):
            for qs in range(G):
                accumulate(0, qs, h)
                own_scores(first + G + qs, qn_ref[0, qs], qs * H + h, h)

    @pl.when(step == last)
    def _():
        for h in range(H):
            for qs in range(G):
                accumulate(0, qs, h)

    for qs in range(G):
        outs = []
        for h in range(H):
            base = (qs * H + h) * ACC_ROWS
            outs.append(acc_ref[base:base + HEAD_DIM, :] / acc_ref[base + HEAD_DIM:base + HEAD_DIM + 1, :])
        o_ref[0, qs * MB:(qs + 1) * MB, :] = jnp.concatenate(outs, axis=0).T.astype(o_ref.dtype)


def _moba(qb_t, kb3, vb_t, sel, rows):
    batch, nblk = qb_t.shape[0], qb_t.shape[1]
    seq = nblk * MOBA_BLOCK
    g = MOBA_Q_PER_STEP
    assert g % 2 == 0 and nblk % g == 0
    chains = g * MOBA_HEADS
    return pl.pallas_call(
        _moba_kernel,
        grid=(batch, nblk // g),
        in_specs=[
            pl.BlockSpec((1, g, MOBA_W, MOBA_BLOCK), lambda b, i: (b, i, 0, 0)),
            pl.BlockSpec((1, g, MOBA_W, MOBA_BLOCK), lambda b, i: (b, jnp.minimum(i + 1, nblk // g - 1), 0, 0)),
            pl.BlockSpec((1, seq, MOBA_W), lambda b, i: (b, 0, 0)),
            pl.BlockSpec((1, nblk, MOBA_W, MOBA_BLOCK), lambda b, i: (b, 0, 0, 0)),
            pl.BlockSpec((1, MOBA_HEADS, nblk, g * MOBA_BLOCK), lambda b, i: (b, 0, 0, i)),
            _resident(rows.shape, lambda b, i: (0, 0, 0)),
        ],
        out_specs=pl.BlockSpec((1, g * MOBA_BLOCK, MOBA_W), lambda b, i: (b, i, 0)),
        out_shape=jax.ShapeDtypeStruct((batch, seq, MOBA_W), BF16),
        scratch_shapes=[pltpu.VMEM((MOBA_HEADS, BIAS_SLOTS, MOBA_BLOCK, MOBA_BLOCK), BF16),
                        pltpu.VMEM((chains, 1, MOBA_BLOCK), F32),
                        pltpu.VMEM((chains * ACC_ROWS, MOBA_BLOCK), F32),
                        pltpu.VMEM((chains, MOBA_BLOCK, MOBA_BLOCK), BF16),
                        pltpu.VMEM((chains, 1, MOBA_BLOCK), F32)],
        compiler_params=pltpu.CompilerParams(dimension_semantics=("arbitrary", "arbitrary"),
                                             vmem_limit_bytes=VMEM_LIMIT_BYTES),
        name="moba",
    )(qb_t, qb_t, kb3, vb_t, sel, rows)


def _post_kernel(x_ref, ya_ref, yb_ref, ga_ref, gb_ref, gate1_ref, shift2_ref, scale2_ref,
                 gate2_ref, g2_ref, gf_ref, wa_ref, wb_ref, wo_ref, w1_ref, w2_ref, o_ref):
    a = _dot(ya_ref[...], wa_ref[...])
    b = _dot(yb_ref[...], wb_ref[...])
    merged = (jax.nn.sigmoid(ga_ref[...].astype(F32)) * a
              + jax.nn.sigmoid(gb_ref[...].astype(F32)) * b)
    x1 = x_ref[...] + gate1_ref[0, 0] * _dot(merged.astype(BF16), wo_ref[...])
    h2 = _rms_modulate(x1, g2_ref[...], shift2_ref[0, 0], scale2_ref[0, 0]).astype(BF16)
    y = jnp.zeros_like(x1)
    for c in range(D_FF // FF_CHUNK):
        cs = slice(c * FF_CHUNK, (c + 1) * FF_CHUNK)
        u = jnp.square(jnp.maximum(_dot(h2, w1_ref[:, cs]), 0.0)).astype(BF16)
        y = y + _dot(u, w2_ref[cs, :])
    x2 = x1 + gate2_ref[0, 0] * y
    ms = jnp.mean(x2 * x2, axis=-1, keepdims=True)
    o_ref[...] = (x2 * lax.rsqrt(ms + RMS_EPS)) * gf_ref[...]


def _post(x2, ya, yb, ga, gb, mod4, g2, gf, wa, wb, wo, w1, w2, seq):
    tokens = x2.shape[0]
    tiles_per_seq = seq // TOKEN_TILE
    tok = lambda w: pl.BlockSpec((TOKEN_TILE, w), lambda i: (i, 0))
    modrow = lambda k: pl.BlockSpec((1, 1, 1, D_MODEL), lambda i: (i // tiles_per_seq, k, 0, 0))
    full = lambda a: _resident(a.shape, lambda i: (0, 0))
    return pl.pallas_call(
        _post_kernel,
        grid=(tokens // TOKEN_TILE,),
        in_specs=[tok(D_MODEL), tok(SWA_Q_W), tok(MOBA_W), tok(D_MODEL), tok(D_MODEL),
                  modrow(2), modrow(3), modrow(4), modrow(5),
                  full(g2), full(gf), full(wa), full(wb), full(wo), full(w1), full(w2)],
        out_specs=tok(D_MODEL),
        out_shape=jax.ShapeDtypeStruct((tokens, D_MODEL), F32),
        compiler_params=pltpu.CompilerParams(dimension_semantics=("arbitrary",),
                                             vmem_limit_bytes=VMEM_LIMIT_BYTES),
        name="post",
    )(x2, ya, yb, ga, gb, mod4, mod4, mod4, mod4, g2, gf, wa, wb, wo, w1, w2)


def kernel(x, c, ada_w, ada_b, norm1_g, norm2_g, w_in, attn_sinks, rel_bias, w_branch_a,
           w_branch_b, w_out, w_mlp_in, w_mlp_out, final_g):
    batch, seq, _ = x.shape
    depth = ada_w.shape[0]
    assert seq % TOKEN_TILE == 0 and TOKEN_TILE % MOBA_BLOCK == 0 and batch <= SUBLANES
    far = np.arange((BIAS_SLOTS - 1) * MOBA_BLOCK - (MOBA_BLOCK - 1), max(seq, BIAS_SLOTS * MOBA_BLOCK))
    assert np.all(_t5_bucket(far) == _t5_bucket(far[:1]))
    assert 2 * SWA_BLOCK <= MOBA_BLOCK and SWA_WINDOW == SWA_BLOCK
    assert depth == 1
    l = 0

    rows = _bias_rows(rel_bias)
    c_pad = jnp.zeros((SUBLANES, D_MODEL), F32).at[:batch].set(c.astype(F32))

    x2 = x.reshape(batch * seq, D_MODEL)
    mod = _ada(c_pad, ada_w[l], ada_b[l][None, :])
    mod4 = mod[:batch].reshape(batch, N_MOD, 1, D_MODEL)
    later_weights = [w.astype(F32) for w in (w_branch_a[l], w_branch_b[l], w_out[l], w_mlp_in[l], w_mlp_out[l])]
    ka, kb, ga, gb, qa_t, va_t, qb_t, vb_t, sel, wa, wb, wo, w1, w2 = _proj(
        x2, mod4, norm1_g[l][None, :], w_in[l].astype(F32), later_weights, batch, seq)

    sink_rows = jnp.broadcast_to(
        (attn_sinks[l].astype(F32) * LOG2E).reshape(SWA_KV_HEADS, 1, SWA_GROUP, 1),
        (SWA_KV_HEADS, 1, SWA_GROUP, SWA_BLOCK)).reshape(SWA_KV_HEADS, 1, SWA_GROUP * SWA_BLOCK)
    ya = _swa(qa_t, ka.reshape(batch, seq, SWA_KV_W), va_t, rows, sink_rows)

    yb = _moba(qb_t, kb.reshape(batch, seq, MOBA_W), vb_t, sel, rows)

    out = _post(x2, ya.reshape(batch * seq, SWA_Q_W), yb.reshape(batch * seq, MOBA_W), ga, gb,
                mod4, norm2_g[l][None, :], final_g[None, :], wa, wb, wo, w1, w2, seq)
    return out.reshape(batch, seq, D_MODEL)
```

```python
import math

import jax
import jax.numpy as jnp
import numpy as np
from jax import lax
from jax.experimental import pallas as pl
from jax.experimental.pallas import tpu as pltpu

D_MODEL = 1024
HEAD_DIM = 64
ATTN_SCALE = HEAD_DIM ** -0.5
LOG2E = math.log2(math.e)
SWA_Q_HEADS = 8
SWA_KV_HEADS = 2
SWA_GROUP = SWA_Q_HEADS // SWA_KV_HEADS
SWA_WINDOW = 128
SWA_BLOCK = 128
MOBA_HEADS = 8
MOBA_BLOCK = 256
MOBA_TOPK = 3
NUM_BUCKETS = 32
MAX_EXACT = NUM_BUCKETS // 2
MAX_DISTANCE = 2048
N_ATTN_HEADS = SWA_Q_HEADS + MOBA_HEADS
SWA_Q_W = SWA_Q_HEADS * HEAD_DIM
SWA_KV_W = SWA_KV_HEADS * HEAD_DIM
MOBA_W = MOBA_HEADS * HEAD_DIM
D_FF = 4 * D_MODEL
N_MOD = 6
RMS_EPS = 1e-6

VMEM_LIMIT_BYTES = 56 * 1024 * 1024
TOKEN_TILE = 512
FF_CHUNK = 1024
SUBLANES = 8
BIAS_SLOTS = 8
SWA_UNROLL = 3
MOBA_Q_PER_STEP = 4
ACC_ROWS = HEAD_DIM + 16
NEG_INF = float("-inf")

F32 = jnp.float32
BF16 = jnp.bfloat16


def _resident(block_shape, index_map):
    return pl.BlockSpec(block_shape, index_map, pipeline_mode=pl.Buffered(1))


def _dot(a, b):
    return jnp.dot(a, b, preferred_element_type=F32)


def _dot_nt(a, b):
    return lax.dot_general(a, b, (((1,), (1,)), ((), ())), preferred_element_type=F32)


def _rms_modulate(xv, g, shift, scale):
    ms = jnp.mean(xv * xv, axis=-1, keepdims=True)
    y = xv * lax.rsqrt(ms + RMS_EPS)
    return (y * g) * (1.0 + scale) + shift


def _toeplitz(u, rows):
    wide = jnp.broadcast_to(u, (rows, 2 * rows))
    return pltpu.roll(wide, 0, 1, stride=1, stride_axis=0)[:, rows:]


def _ada_kernel(c_ref, w_ref, b_ref, o_ref, cs_ref):
    batch = c_ref.shape[0]
    cs_ref[...] = jnp.zeros(cs_ref.shape, F32)
    cs_ref[0:batch, :] = jax.nn.silu(c_ref[...])
    res = _dot(cs_ref[...].astype(BF16), w_ref[...].astype(BF16)) + b_ref[...]
    for b in range(batch):
        o_ref[b, 0] = res[b:b + 1]


def _ada(c, w, b):
    batch = c.shape[0]
    assert w.shape[1] == N_MOD * D_MODEL
    return pl.pallas_call(
        _ada_kernel,
        grid=(N_MOD,),
        in_specs=[pl.BlockSpec((batch, D_MODEL), lambda j: (0, 0)),
                  pl.BlockSpec((D_MODEL, D_MODEL), lambda j: (0, j)),
                  pl.BlockSpec((1, D_MODEL), lambda j: (0, j))],
        out_specs=pl.BlockSpec((batch, 1, 1, D_MODEL), lambda j: (0, j, 0, 0)),
        out_shape=jax.ShapeDtypeStruct((batch, N_MOD, 1, D_MODEL), F32),
        scratch_shapes=[pltpu.VMEM((SUBLANES, D_MODEL), F32)],
        compiler_params=pltpu.CompilerParams(dimension_semantics=("arbitrary",),
                                             vmem_limit_bytes=VMEM_LIMIT_BYTES),
        name="ada",
    )(c, w, b)


def _t5_bucket(dist):
    n = np.maximum(dist, 0)
    nf = np.maximum(n, 1).astype(np.float32)
    large = MAX_EXACT + (np.log(nf / np.float32(MAX_EXACT)) / np.float32(math.log(MAX_DISTANCE / MAX_EXACT))
                         * np.float32(NUM_BUCKETS - MAX_EXACT)).astype(np.int32)
    large = np.minimum(large, NUM_BUCKETS - 1)
    return np.where(n < MAX_EXACT, n, large).astype(np.int32)


def _bias_rows_kernel(bucket_ref, rbt_ref, o_ref):
    bucket = bucket_ref[...]
    n = bucket.shape[1]
    onehot = jnp.where(lax.broadcasted_iota(jnp.int32, (NUM_BUCKETS, n), 0) == bucket, 1.0, 0.0)
    rows = jnp.dot(rbt_ref[...], onehot, preferred_element_type=F32,
                   precision=lax.Precision.HIGHEST)
    rows = jnp.where(bucket >= 0, rows * LOG2E, NEG_INF)
    for d in range(BIAS_SLOTS):
        o_ref[d] = rows[:, d * MOBA_BLOCK:(d + 2) * MOBA_BLOCK]


def _bias_rows(rel_bias):
    n = (BIAS_SLOTS + 1) * MOBA_BLOCK
    dist = np.arange(n, dtype=np.int32) - MOBA_BLOCK
    bucket = jnp.asarray(np.where(dist >= 0, _t5_bucket(dist), -1)[None, :], dtype=jnp.int32)
    return pl.pallas_call(
        _bias_rows_kernel,
        out_shape=jax.ShapeDtypeStruct((BIAS_SLOTS, N_ATTN_HEADS, 2 * MOBA_BLOCK), F32),
        name="bias_rows",
    )(bucket, rel_bias.astype(F32).T)


IN_WIDTHS = (SWA_Q_W, SWA_KV_W, SWA_KV_W, MOBA_W, MOBA_W, MOBA_W, D_MODEL, D_MODEL)
IN_OFFS = tuple(sum(IN_WIDTHS[:k]) for k in range(len(IN_WIDTHS) + 1))
FEAT_COLS = (0, 2, 3, 5)
FEAT_W = sum(IN_WIDTHS[k] for k in FEAT_COLS)
TOK_COLS = (1, 4, 6, 7)
TOK_W = sum(IN_WIDTHS[k] for k in TOK_COLS)
Q_COLS = (0, 3)
BLOCKS_PER_TILE = TOKEN_TILE // MOBA_BLOCK


def _proj_kernel(x_ref, shift_ref, scale_ref, g_ref, w_ref, *refs):
    n_cast = (len(refs) - 12) // 2
    cast_in, refs = refs[:n_cast], refs[n_cast:]
    ka_ref, kb_ref, ga_ref, gb_ref, qa_ref, va_ref, qb_ref, vb_ref, sel_ref = refs[:9]
    cast_out, (wk_ref, wt_ref, kmean_ref) = refs[9:9 + n_cast], refs[9 + n_cast:]

    for src, dst in zip(cast_in, cast_out):
        dst[...] = src[...].astype(dst.dtype)

    tile = pl.program_id(0) % (kmean_ref.shape[0] // BLOCKS_PER_TILE)

    @pl.when(tile == 0)
    def _():
        kmean_ref[...] = jnp.zeros(kmean_ref.shape, F32)

    @pl.when(pl.program_id(0) == 0)
    def _():
        o = 0
        for k in TOK_COLS:
            wk_ref[:, o:o + IN_WIDTHS[k]] = w_ref[:, IN_OFFS[k]:IN_OFFS[k + 1]].astype(BF16)
            o += IN_WIDTHS[k]
        o = 0
        for k in FEAT_COLS:
            wcol = w_ref[:, IN_OFFS[k]:IN_OFFS[k + 1]]
            if k in Q_COLS:
                wcol = wcol * (ATTN_SCALE * LOG2E)
            wt_ref[o:o + IN_WIDTHS[k], :] = wcol.T.astype(BF16)
            o += IN_WIDTHS[k]

    h = _rms_modulate(x_ref[...], g_ref[...], shift_ref[0, 0], scale_ref[0, 0]).astype(BF16)
    tok_refs = dict(zip(TOK_COLS, (ka_ref, kb_ref, ga_ref, gb_ref)))
    feat_refs = dict(zip(FEAT_COLS, (qa_ref, va_ref, qb_ref, vb_ref)))

    def token_major(k):
        o = sum(IN_WIDTHS[c] for c in TOK_COLS[:TOK_COLS.index(k)])
        out = _dot(h, wk_ref[:, o:o + IN_WIDTHS[k]])
        tok_refs[k][...] = out.astype(BF16)
        return out

    def feature_major(k):
        o = sum(IN_WIDTHS[c] for c in FEAT_COLS[:FEAT_COLS.index(k)])
        out = _dot_nt(wt_ref[o:o + IN_WIDTHS[k], :], h).astype(BF16)
        blk = feat_refs[k].shape[-1]
        for t in range(TOKEN_TILE // blk):
            feat_refs[k][0, t] = out[:, t * blk:(t + 1) * blk]
        return out

    _update_kmean(token_major(4), kmean_ref, tile)
    qb_t = feature_major(3)
    token_major(1)
    token_major(6)
    _moba_select(qb_t, sel_ref, kmean_ref, tile)
    token_major(7)
    for k in (0, 2, 5):
        feature_major(k)


def _update_kmean(kb, kmean_ref, tile):
    for t in range(BLOCKS_PER_TILE):
        kmean_ref[pl.ds(tile * BLOCKS_PER_TILE + t, 1), :] = jnp.mean(
            kb[t * MOBA_BLOCK:(t + 1) * MOBA_BLOCK, :], axis=0, keepdims=True)


def _moba_select(qb_t, sel_ref, kmean_ref, tile):
    nblk = kmean_ref.shape[0]
    blocks_per_tile = BLOCKS_PER_TILE
    blk = lax.broadcasted_iota(jnp.int32, (nblk, TOKEN_TILE), 0)
    qblk = tile * blocks_per_tile + lax.broadcasted_iota(jnp.int32, (nblk, TOKEN_TILE), 1) // MOBA_BLOCK
    past = blk < qblk
    own = jnp.where(blk == qblk, 0.0, NEG_INF)
    blk_f = blk.astype(F32)
    for h in range(MOBA_HEADS):
        hs = slice(h * HEAD_DIM, (h + 1) * HEAD_DIM)
        kmean = kmean_ref[:, hs]
        k1 = kmean.astype(BF16)
        r1 = kmean - k1.astype(F32)
        k2 = r1.astype(BF16)
        k3 = (r1 - k2.astype(F32)).astype(BF16)
        parts = _dot(jnp.concatenate([k1, k2, k3], axis=0), qb_t[hs, :])
        gate = parts[0:nblk] + parts[nblk:2 * nblk] + parts[2 * nblk:3 * nblk]
        gate = jnp.where(past, gate, NEG_INF)
        mask = own
        for _ in range(MOBA_TOPK):
            top = jnp.max(gate, axis=0, keepdims=True)
            first = jnp.min(jnp.where(gate == top, blk_f, float(nblk)), axis=0, keepdims=True)
            pick = blk_f == first
            mask = jnp.where(jnp.logical_and(pick, past), 0.0, mask)
            gate = jnp.where(pick, NEG_INF, gate)
        sel_ref[0, h] = mask


def _proj(x2, mod4, g1, w_in, cast_weights, batch, seq):
    tokens = x2.shape[0]
    steps = tokens // TOKEN_TILE
    assert all(w.shape[0] % (16 * steps) == 0 for w in cast_weights)
    cast_specs = [pl.BlockSpec((w.shape[0] // steps, w.shape[1]), lambda i: (i, 0)) for w in cast_weights]
    tiles_per_seq = seq // TOKEN_TILE
    na = TOKEN_TILE // SWA_BLOCK
    nb = TOKEN_TILE // MOBA_BLOCK

    def feat_spec(width, blk, per_tile):
        return pl.BlockSpec((1, per_tile, width, blk),
                            lambda i: (i // tiles_per_seq, i % tiles_per_seq, 0, 0))

    out_shape = (
        jax.ShapeDtypeStruct((tokens, SWA_KV_W), BF16),
        jax.ShapeDtypeStruct((tokens, MOBA_W), BF16),
        jax.ShapeDtypeStruct((tokens, D_MODEL), BF16),
        jax.ShapeDtypeStruct((tokens, D_MODEL), BF16),
        jax.ShapeDtypeStruct((batch, seq // SWA_BLOCK, SWA_Q_W, SWA_BLOCK), BF16),
        jax.ShapeDtypeStruct((batch, seq // SWA_BLOCK, SWA_KV_W, SWA_BLOCK), BF16),
        jax.ShapeDtypeStruct((batch, seq // MOBA_BLOCK, MOBA_W, MOBA_BLOCK), BF16),
        jax.ShapeDtypeStruct((batch, seq // MOBA_BLOCK, MOBA_W, MOBA_BLOCK), BF16),
        jax.ShapeDtypeStruct((batch, MOBA_HEADS, seq // MOBA_BLOCK, seq), F32),
    )
    tok_spec = lambda w: pl.BlockSpec((TOKEN_TILE, w), lambda i: (i, 0))
    return pl.pallas_call(
        _proj_kernel,
        grid=(steps,),
        in_specs=[
            pl.BlockSpec((TOKEN_TILE, D_MODEL), lambda i: (i, 0)),
            pl.BlockSpec((1, 1, 1, D_MODEL), lambda i: (i // tiles_per_seq, 0, 0, 0)),
            pl.BlockSpec((1, 1, 1, D_MODEL), lambda i: (i // tiles_per_seq, 1, 0, 0)),
            _resident((1, D_MODEL), lambda i: (0, 0)),
            _resident(w_in.shape, lambda i: (0, 0)),
        ] + cast_specs,
        out_specs=(tok_spec(SWA_KV_W), tok_spec(MOBA_W), tok_spec(D_MODEL), tok_spec(D_MODEL),
                   feat_spec(SWA_Q_W, SWA_BLOCK, na), feat_spec(SWA_KV_W, SWA_BLOCK, na),
                   feat_spec(MOBA_W, MOBA_BLOCK, nb), feat_spec(MOBA_W, MOBA_BLOCK, nb),
                   pl.BlockSpec((1, MOBA_HEADS, seq // MOBA_BLOCK, TOKEN_TILE),
                                lambda i: (i // tiles_per_seq, 0, 0, i % tiles_per_seq)))
        + tuple(cast_specs),
        out_shape=out_shape + tuple(jax.ShapeDtypeStruct(w.shape, BF16) for w in cast_weights),
        scratch_shapes=[pltpu.VMEM((D_MODEL, TOK_W), BF16), pltpu.VMEM((FEAT_W, D_MODEL), BF16),
                        pltpu.VMEM((seq // MOBA_BLOCK, MOBA_W), F32)],
        compiler_params=pltpu.CompilerParams(dimension_semantics=("arbitrary",),
                                             vmem_limit_bytes=VMEM_LIMIT_BYTES),
        name="proj",
    )(x2, mod4, mod4, g1, w_in, *cast_weights)


def _swa_kernel(q_ref, k_ref, v_ref, rows_ref, sink_ref, o_ref, bias_ref, ot_ref, s_ref, cm_ref):
    L = SWA_BLOCK
    nblk = q_ref.shape[1]

    @pl.when(pl.program_id(0) == 0)
    def _():
        k_idx = lax.broadcasted_iota(jnp.int32, (L, L), 0)
        q_idx = lax.broadcasted_iota(jnp.int32, (L, L), 1)
        for h in range(SWA_Q_HEADS):
            g, gi = divmod(h, SWA_GROUP)
            t0 = rows_ref[0, h:h + 1, MOBA_BLOCK:MOBA_BLOCK + L]
            r = _toeplitz(jnp.concatenate([t0, t0], axis=1), L)
            cols = slice(gi * L, (gi + 1) * L)
            bias_ref[g, 0:L, cols] = jnp.where(q_idx < k_idx, r, NEG_INF).astype(BF16)
            bias_ref[g, L:2 * L, cols] = jnp.where(q_idx >= k_idx, r, NEG_INF).astype(BF16)

    ones_blk = jnp.concatenate([jnp.ones((1, 2 * L), BF16), jnp.zeros((ACC_ROWS - HEAD_DIM - 1, 2 * L), BF16)],
                               axis=0)
    PW = 2 * L
    chains = [(g, hp) for g in range(SWA_KV_HEADS) for hp in range(SWA_GROUP // 2)]
    kv_cols = [slice(g * HEAD_DIM, (g + 1) * HEAD_DIM) for g in range(SWA_KV_HEADS)]

    def q_pair(c, g, hp):
        h0 = g * SWA_GROUP + 2 * hp
        return jnp.concatenate([q_ref[0, c, (h0 + e) * HEAD_DIM:(h0 + e + 1) * HEAD_DIM, :] for e in range(2)],
                               axis=1)

    def scores_first(idx):
        g, hp = chains[idx]
        lanes = slice(hp * PW, (hp + 1) * PW)
        s_ref[idx, 0:L, :] = jnp.full((L, PW), NEG_INF, BF16)
        s = _dot(k_ref[0, 0:L, kv_cols[g]], q_pair(0, g, hp)).astype(BF16) + bias_ref[g, L:2 * L, lanes]
        s_ref[idx, L:2 * L, :] = s
        cm_ref[idx] = jnp.max(s, axis=0, keepdims=True).astype(F32)

    def scores(c, idx):
        g, hp = chains[idx]
        kw = k_ref[0, pl.ds(pl.multiple_of((c - 1) * L, L), 2 * L), kv_cols[g]]
        s = _dot(kw, q_pair(c, g, hp)).astype(BF16) + bias_ref[g, :, hp * PW:(hp + 1) * PW]
        s_ref[idx] = s
        cm_ref[idx] = jnp.max(s, axis=0, keepdims=True).astype(F32)

    def attend(c, c_prev, idx, slot):
        g, hp = chains[idx]
        hs = kv_cols[g]
        s = s_ref[idx]
        sink = sink_ref[g, :, hp * PW:(hp + 1) * PW]
        m_b = jnp.maximum(cm_ref[idx], sink).astype(BF16)
        m = m_b.astype(F32)
        p = jnp.exp2(s - m_b)
        v_win = jnp.concatenate([v_ref[0, c_prev, hs, :], v_ref[0, c, hs, :]], axis=1)
        pv = _dot(jnp.concatenate([v_win, ones_blk], axis=0), p)
        o = pv[0:HEAD_DIM] / (pv[HEAD_DIM:HEAD_DIM + 1] + jnp.exp2(sink - m))
        for e in range(2):
            r = (g * SWA_GROUP + 2 * hp + e) * HEAD_DIM
            ot_ref[slot, r:r + HEAD_DIM, :] = o[:, e * L:(e + 1) * L]

    def emit(c, slot):
        o_ref[0, pl.ds(pl.multiple_of(c * L, L), L), :] = ot_ref[slot].T.astype(o_ref.dtype)

    for idx in range(len(chains)):
        scores_first(idx)
    U = SWA_UNROLL
    for idx in range(len(chains)):
        attend(0, 0, idx, U - 1)
        scores(1, idx)

    def body(c, e):
        emit(c - 1, (e - 1) % U)
        for idx in range(len(chains)):
            attend(c, c - 1, idx, e)
            scores(c + 1, idx)

    def body_unrolled(u, carry):
        for e in range(U):
            body(U * u + 1 + e, e)
        return carry

    lax.fori_loop(0, (nblk - 2) // U, body_unrolled, 0)
    emit(nblk - 2, U - 1)
    for idx in range(len(chains)):
        attend(nblk - 1, nblk - 2, idx, 0)
    emit(nblk - 1, 0)


def _swa(qa_t, ka3, va_t, rows, sink_rows):
    batch, nblk = qa_t.shape[0], qa_t.shape[1]
    seq = nblk * SWA_BLOCK
    assert nblk >= 2 and (nblk - 2) % SWA_UNROLL == 0 and SWA_UNROLL >= 2
    return pl.pallas_call(
        _swa_kernel,
        grid=(batch,),
        in_specs=[
            pl.BlockSpec((1, nblk, SWA_Q_W, SWA_BLOCK), lambda b: (b, 0, 0, 0)),
            pl.BlockSpec((1, seq, SWA_KV_W), lambda b: (b, 0, 0)),
            pl.BlockSpec((1, nblk, SWA_KV_W, SWA_BLOCK), lambda b: (b, 0, 0, 0)),
            _resident((1, N_ATTN_HEADS, 2 * MOBA_BLOCK), lambda b: (0, 0, 0)),
            _resident(sink_rows.shape, lambda b: (0, 0, 0)),
        ],
        out_specs=pl.BlockSpec((1, seq, SWA_Q_W), lambda b: (b, 0, 0)),
        out_shape=jax.ShapeDtypeStruct((batch, seq, SWA_Q_W), BF16),
        scratch_shapes=[pltpu.VMEM((SWA_KV_HEADS, 2 * SWA_BLOCK, SWA_GROUP * SWA_BLOCK), BF16),
                        pltpu.VMEM((SWA_UNROLL, SWA_Q_W, SWA_BLOCK), F32),
                        pltpu.VMEM((SWA_Q_HEADS // 2, 2 * SWA_BLOCK, 2 * SWA_BLOCK), BF16),
                        pltpu.VMEM((SWA_Q_HEADS // 2, 1, 2 * SWA_BLOCK), F32)],
        compiler_params=pltpu.CompilerParams(dimension_semantics=("arbitrary",),
                                             vmem_limit_bytes=VMEM_LIMIT_BYTES),
        name="swa",
    )(qa_t, ka3, va_t, rows, sink_rows)


def _moba_kernel(q_ref, qn_ref, k_ref, v_ref, sel_ref, rows_ref, o_ref, bias_ref, m_ref, acc_ref, s_ref, cm_ref):
    step = pl.program_id(1)
    last = pl.num_programs(1) - 1
    MB = MOBA_BLOCK
    H = MOBA_HEADS
    G = MOBA_Q_PER_STEP

    @pl.when(jnp.logical_and(pl.program_id(0) == 0, step == 0))
    def _():
        for h in range(H):
            for d in range(BIAS_SLOTS):
                bias_ref[h, d] = _toeplitz(rows_ref[d, SWA_Q_HEADS + h:SWA_Q_HEADS + h + 1, :], MB).astype(BF16)

    m_ref[...] = jnp.full(m_ref.shape, NEG_INF, F32)
    acc_ref[...] = jnp.zeros(acc_ref.shape, F32)
    heads = [slice(h * HEAD_DIM, (h + 1) * HEAD_DIM) for h in range(H)]
    ones_blk = jnp.concatenate([jnp.ones((1, MB), BF16), jnp.zeros((ACC_ROWS - HEAD_DIM - 1, MB), BF16)],
                               axis=0)

    def sel_row(j, qs, h):
        return sel_ref[0, h, pl.ds(j, 1), qs * MB:(qs + 1) * MB]

    def scores(j, qs, h):
        c = qs * H + h
        rows = pl.ds(pl.multiple_of(j * MB, MB), MB)
        slot = jnp.minimum(G * step + qs - j, BIAS_SLOTS - 1)
        s = _dot(k_ref[0, rows, heads[h]], q_ref[0, qs, heads[h], :]).astype(BF16) + bias_ref[h, slot]
        s_ref[c] = s
        cm_ref[c] = jnp.max(s, axis=0, keepdims=True).astype(F32) + sel_row(j, qs, h)

    def own_scores(blk, q_blk, c, h):
        rows = pl.ds(pl.multiple_of(blk * MB, MB), MB)
        s = _dot(k_ref[0, rows, heads[h]], q_blk[heads[h], :]).astype(BF16) + bias_ref[h, 0]
        s_ref[c] = s
        cm_ref[c] = jnp.max(s, axis=0, keepdims=True).astype(F32)

    def accumulate(j, qs, h):
        c = qs * H + h
        m_old = m_ref[c]
        m_new = jnp.maximum(m_old, cm_ref[c])
        alpha = jnp.exp2(m_old - m_new)
        p = jnp.exp2(s_ref[c] - (m_new - sel_row(j, qs, h)).astype(BF16))
        v_aug = jnp.concatenate([v_ref[0, j, heads[h], :], ones_blk], axis=0)
        rs = slice(c * ACC_ROWS, (c + 1) * ACC_ROWS)
        acc_ref[rs, :] = alpha * acc_ref[rs, :] + _dot(v_aug, p)
        m_ref[c] = m_new

    first = G * step

    @pl.when(step == 0)
    def _():
        for qs in range(G):
            for h in range(H):
                own_scores(qs, q_ref[0, qs], qs * H + h, h)

    for n in range(G - 1):
        for qs in range(G - 1 - n, G):
            for h in range(H):
                accumulate(first + G - 1 - n, qs, h)
                scores(first + G - 2 - n, qs, h)

    def body(j, carry):
        for qs in range(G):
            for h in range(H):
                accumulate(j, qs, h)
                scores(j - 1, qs, h)
        return carry

    def body2(u, carry):
        return body(first - 2 * u - 1, body(first - 2 * u, carry))

    lax.fori_loop(0, (G // 2) * step, body2, 0)

    @pl.when(step < last)
    def _():
        for h in range(H):
            for qs in range(G):
                accumulate(0, qs, h)
                own_scores(first + G + qs, qn_ref[0, qs], qs * H + h, h)

    @pl.when(step == last)
    def _():
        for h in range(H):
            for qs in range(G):
                accumulate(0, qs, h)

    for qs in range(G):
        outs = []
        for h in range(H):
            base = (qs * H + h) * ACC_ROWS
            outs.append(acc_ref[base:base + HEAD_DIM, :] / acc_ref[base + HEAD_DIM:base + HEAD_DIM + 1, :])
        o_ref[0, qs * MB:(qs + 1) * MB, :] = jnp.concatenate(outs, axis=0).T.astype(o_ref.dtype)


def _moba(qb_t, kb3, vb_t, sel, rows):
    batch, nblk = qb_t.shape[0], qb_t.shape[1]
    seq = nblk * MOBA_BLOCK
    g = MOBA_Q_PER_STEP
    assert g % 2 == 0 and nblk % g == 0
    chains = g * MOBA_HEADS
    return pl.pallas_call(
        _moba_kernel,
        grid=(batch, nblk // g),
        in_specs=[
            pl.BlockSpec((1, g, MOBA_W, MOBA_BLOCK), lambda b, i: (b, i, 0, 0)),
            pl.BlockSpec((1, g, MOBA_W, MOBA_BLOCK), lambda b, i: (b, jnp.minimum(i + 1, nblk // g - 1), 0, 0)),
            pl.BlockSpec((1, seq, MOBA_W), lambda b, i: (b, 0, 0)),
            pl.BlockSpec((1, nblk, MOBA_W, MOBA_BLOCK), lambda b, i: (b, 0, 0, 0)),
            pl.BlockSpec((1, MOBA_HEADS, nblk, g * MOBA_BLOCK), lambda b, i: (b, 0, 0, i)),
            _resident(rows.shape, lambda b, i: (0, 0, 0)),
        ],
        out_specs=pl.BlockSpec((1, g * MOBA_BLOCK, MOBA_W), lambda b, i: (b, i, 0)),
        out_shape=jax.ShapeDtypeStruct((batch, seq, MOBA_W), BF16),
        scratch_shapes=[pltpu.VMEM((MOBA_HEADS, BIAS_SLOTS, MOBA_BLOCK, MOBA_BLOCK), BF16),
                        pltpu.VMEM((chains, 1, MOBA_BLOCK), F32),
                        pltpu.VMEM((chains * ACC_ROWS, MOBA_BLOCK), F32),
                        pltpu.VMEM((chains, MOBA_BLOCK, MOBA_BLOCK), BF16),
                        pltpu.VMEM((chains, 1, MOBA_BLOCK), F32)],
        compiler_params=pltpu.CompilerParams(dimension_semantics=("arbitrary", "arbitrary"),
                                             vmem_limit_bytes=VMEM_LIMIT_BYTES),
        name="moba",
    )(qb_t, qb_t, kb3, vb_t, sel, rows)


def _post_kernel(x_ref, ya_ref, yb_ref, ga_ref, gb_ref, gate1_ref, shift2_ref, scale2_ref,
                 gate2_ref, g2_ref, gf_ref, wa_ref, wb_ref, wo_ref, w1_ref, w2_ref, o_ref):
    a = _dot(ya_ref[...], wa_ref[...])
    b = _dot(yb_ref[...], wb_ref[...])
    merged = (jax.nn.sigmoid(ga_ref[...].astype(F32)) * a
              + jax.nn.sigmoid(gb_ref[...].astype(F32)) * b)
    x1 = x_ref[...] + gate1_ref[0, 0] * _dot(merged.astype(BF16), wo_ref[...])
    h2 = _rms_modulate(x1, g2_ref[...], shift2_ref[0, 0], scale2_ref[0, 0]).astype(BF16)
    y = jnp.zeros_like(x1)
    for c in range(D_FF // FF_CHUNK):
        cs = slice(c * FF_CHUNK, (c + 1) * FF_CHUNK)
        u = jnp.square(jnp.maximum(_dot(h2, w1_ref[:, cs]), 0.0)).astype(BF16)
        y = y + _dot(u, w2_ref[cs, :])
    x2 = x1 + gate2_ref[0, 0] * y
    ms = jnp.mean(x2 * x2, axis=-1, keepdims=True)
    o_ref[...] = (x2 * lax.rsqrt(ms + RMS_EPS)) * gf_ref[...]


def _post(x2, ya, yb, ga, gb, mod4, g2, gf, wa, wb, wo, w1, w2, seq):
    tokens = x2.shape[0]
    tiles_per_seq = seq // TOKEN_TILE
    tok = lambda w: pl.BlockSpec((TOKEN_TILE, w), lambda i: (i, 0))
    modrow = lambda k: pl.BlockSpec((1, 1, 1, D_MODEL), lambda i: (i // tiles_per_seq, k, 0, 0))
    full = lambda a: _resident(a.shape, lambda i: (0, 0))
    return pl.pallas_call(
        _post_kernel,
        grid=(tokens // TOKEN_TILE,),
        in_specs=[tok(D_MODEL), tok(SWA_Q_W), tok(MOBA_W), tok(D_MODEL), tok(D_MODEL),
                  modrow(2), modrow(3), modrow(4), modrow(5),
                  full(g2), full(gf), full(wa), full(wb), full(wo), full(w1), full(w2)],
        out_specs=tok(D_MODEL),
        out_shape=jax.ShapeDtypeStruct((tokens, D_MODEL), F32),
        compiler_params=pltpu.CompilerParams(dimension_semantics=("arbitrary",),
                                             vmem_limit_bytes=VMEM_LIMIT_BYTES),
        name="post",
    )(x2, ya, yb, ga, gb, mod4, mod4, mod4, mod4, g2, gf, wa, wb, wo, w1, w2)


def kernel(x, c, ada_w, ada_b, norm1_g, norm2_g, w_in, attn_sinks, rel_bias, w_branch_a,
           w_branch_b, w_out, w_mlp_in, w_mlp_out, final_g):
    batch, seq, _ = x.shape
    depth = ada_w.shape[0]
    assert seq % TOKEN_TILE == 0 and TOKEN_TILE % MOBA_BLOCK == 0 and batch <= SUBLANES
    far = np.arange((BIAS_SLOTS - 1) * MOBA_BLOCK - (MOBA_BLOCK - 1), max(seq, BIAS_SLOTS * MOBA_BLOCK))
    assert np.all(_t5_bucket(far) == _t5_bucket(far[:1]))
    assert 2 * SWA_BLOCK <= MOBA_BLOCK and SWA_WINDOW == SWA_BLOCK
    assert depth == 1
    l = 0

    rows = _bias_rows(rel_bias)

    x2 = x.reshape(batch * seq, D_MODEL)
    mod4 = _ada(c.astype(F32), ada_w[l], ada_b[l][None, :])
    later_weights = [w.astype(F32) for w in (w_branch_a[l], w_branch_b[l], w_out[l], w_mlp_in[l], w_mlp_out[l])]
    ka, kb, ga, gb, qa_t, va_t, qb_t, vb_t, sel, wa, wb, wo, w1, w2 = _proj(
        x2, mod4, norm1_g[l][None, :], w_in[l].astype(F32), later_weights, batch, seq)

    sink_rows = jnp.broadcast_to(
        (attn_sinks[l].astype(F32) * LOG2E).reshape(SWA_KV_HEADS, 1, SWA_GROUP, 1),
        (SWA_KV_HEADS, 1, SWA_GROUP, SWA_BLOCK)).reshape(SWA_KV_HEADS, 1, SWA_GROUP * SWA_BLOCK)
    ya = _swa(qa_t, ka.reshape(batch, seq, SWA_KV_W), va_t, rows, sink_rows)

    yb = _moba(qb_t, kb.reshape(batch, seq, MOBA_W), vb_t, sel, rows)

    out = _post(x2, ya.reshape(batch * seq, SWA_Q_W), yb.reshape(batch * seq, MOBA_W), ga, gb,
                mod4, norm2_g[l][None, :], final_g[None, :], wa, wb, wo, w1, w2, seq)
    return out.reshape(batch, seq, D_MODEL)
```

```python
import math

import jax
import jax.numpy as jnp
import numpy as np
from jax import lax
from jax.experimental import pallas as pl
from jax.experimental.pallas import tpu as pltpu

D_MODEL = 1024
HEAD_DIM = 64
ATTN_SCALE = HEAD_DIM ** -0.5
LOG2E = math.log2(math.e)
SWA_Q_HEADS = 8
SWA_KV_HEADS = 2
SWA_GROUP = SWA_Q_HEADS // SWA_KV_HEADS
SWA_WINDOW = 128
SWA_BLOCK = 128
MOBA_HEADS = 8
MOBA_BLOCK = 256
MOBA_TOPK = 3
NUM_BUCKETS = 32
MAX_EXACT = NUM_BUCKETS // 2
MAX_DISTANCE = 2048
N_ATTN_HEADS = SWA_Q_HEADS + MOBA_HEADS
SWA_Q_W = SWA_Q_HEADS * HEAD_DIM
SWA_KV_W = SWA_KV_HEADS * HEAD_DIM
MOBA_W = MOBA_HEADS * HEAD_DIM
D_FF = 4 * D_MODEL
N_MOD = 6
RMS_EPS = 1e-6

VMEM_LIMIT_BYTES = 56 * 1024 * 1024
TOKEN_TILE = 512
FF_CHUNK = 1024
SUBLANES = 8
BIAS_SLOTS = 8
SWA_UNROLL = 3
MOBA_Q_PER_STEP = 4
ACC_ROWS = HEAD_DIM + 16
NEG_INF = float("-inf")

F32 = jnp.float32
BF16 = jnp.bfloat16


def _resident(block_shape, index_map):
    return pl.BlockSpec(block_shape, index_map, pipeline_mode=pl.Buffered(1))


def _dot(a, b):
    return jnp.dot(a, b, preferred_element_type=F32)


def _dot_nt(a, b):
    return lax.dot_general(a, b, (((1,), (1,)), ((), ())), preferred_element_type=F32)


def _rms_modulate(xv, g, shift, scale):
    ms = jnp.mean(xv * xv, axis=-1, keepdims=True)
    y = xv * lax.rsqrt(ms + RMS_EPS)
    return (y * g) * (1.0 + scale) + shift


def _toeplitz(u, rows):
    wide = jnp.broadcast_to(u, (rows, 2 * rows))
    return pltpu.roll(wide, 0, 1, stride=1, stride_axis=0)[:, rows:]


def _t5_bucket(dist):
    n = np.maximum(dist, 0)
    nf = np.maximum(n, 1).astype(np.float32)
    large = MAX_EXACT + (np.log(nf / np.float32(MAX_EXACT)) / np.float32(math.log(MAX_DISTANCE / MAX_EXACT))
                         * np.float32(NUM_BUCKETS - MAX_EXACT)).astype(np.int32)
    large = np.minimum(large, NUM_BUCKETS - 1)
    return np.where(n < MAX_EXACT, n, large).astype(np.int32)


def _bias_rows(bucket_ref, rbt_ref, o_ref):
    bucket = bucket_ref[...]
    n = bucket.shape[1]
    onehot = jnp.where(lax.broadcasted_iota(jnp.int32, (NUM_BUCKETS, n), 0) == bucket, 1.0, 0.0)
    rows = jnp.dot(rbt_ref[...], onehot, preferred_element_type=F32,
                   precision=lax.Precision.HIGHEST)
    rows = jnp.where(bucket >= 0, rows * LOG2E, NEG_INF)
    for d in range(BIAS_SLOTS):
        o_ref[d] = rows[:, d * MOBA_BLOCK:(d + 2) * MOBA_BLOCK]


def _ada_kernel(c_ref, w_ref, b_ref, bucket_ref, rbt_ref, sinks_ref, o_ref, rows_ref, sink_rows_ref, cs_ref):
    batch = c_ref.shape[0]
    cs_ref[...] = jnp.zeros(cs_ref.shape, F32)
    cs_ref[0:batch, :] = jax.nn.silu(c_ref[...])
    res = _dot(cs_ref[...].astype(BF16), w_ref[...].astype(BF16)) + b_ref[...]
    for b in range(batch):
        o_ref[b, 0] = res[b:b + 1]

    @pl.when(pl.program_id(0) == 0)
    def _():
        _bias_rows(bucket_ref, rbt_ref, rows_ref)
        sinks = sinks_ref[...] * LOG2E
        for h in range(SWA_Q_HEADS):
            g, gi = divmod(h, SWA_GROUP)
            sink_rows_ref[g, :, gi * SWA_BLOCK:(gi + 1) * SWA_BLOCK] = jnp.broadcast_to(
                sinks[:, h:h + 1], (1, SWA_BLOCK))


def _ada(c, w, b, rel_bias, sinks):
    batch = c.shape[0]
    assert w.shape[1] == N_MOD * D_MODEL
    n = (BIAS_SLOTS + 1) * MOBA_BLOCK
    dist = np.arange(n, dtype=np.int32) - MOBA_BLOCK
    bucket = jnp.asarray(np.where(dist >= 0, _t5_bucket(dist), -1)[None, :], dtype=jnp.int32)
    rows_shape = (BIAS_SLOTS, N_ATTN_HEADS, 2 * MOBA_BLOCK)
    sink_shape = (SWA_KV_HEADS, 1, SWA_GROUP * SWA_BLOCK)
    return pl.pallas_call(
        _ada_kernel,
        grid=(N_MOD,),
        in_specs=[pl.BlockSpec((batch, D_MODEL), lambda j: (0, 0)),
                  pl.BlockSpec((D_MODEL, D_MODEL), lambda j: (0, j)),
                  pl.BlockSpec((1, D_MODEL), lambda j: (0, j)),
                  pl.BlockSpec((1, n), lambda j: (0, 0)),
                  pl.BlockSpec((N_ATTN_HEADS, NUM_BUCKETS), lambda j: (0, 0)),
                  pl.BlockSpec((1, SWA_Q_HEADS), lambda j: (0, 0))],
        out_specs=(pl.BlockSpec((batch, 1, 1, D_MODEL), lambda j: (0, j, 0, 0)),
                   pl.BlockSpec(rows_shape, lambda j: (0, 0, 0)),
                   pl.BlockSpec(sink_shape, lambda j: (0, 0, 0))),
        out_shape=(jax.ShapeDtypeStruct((batch, N_MOD, 1, D_MODEL), F32),
                   jax.ShapeDtypeStruct(rows_shape, F32),
                   jax.ShapeDtypeStruct(sink_shape, F32)),
        scratch_shapes=[pltpu.VMEM((SUBLANES, D_MODEL), F32)],
        compiler_params=pltpu.CompilerParams(dimension_semantics=("arbitrary",),
                                             vmem_limit_bytes=VMEM_LIMIT_BYTES),
        name="ada",
    )(c, w, b, bucket, rel_bias.astype(F32).T, sinks.astype(F32)[None, :])


IN_WIDTHS = (SWA_Q_W, SWA_KV_W, SWA_KV_W, MOBA_W, MOBA_W, MOBA_W, D_MODEL, D_MODEL)
IN_OFFS = tuple(sum(IN_WIDTHS[:k]) for k in range(len(IN_WIDTHS) + 1))
FEAT_COLS = (0, 2, 3, 5)
FEAT_W = sum(IN_WIDTHS[k] for k in FEAT_COLS)
TOK_COLS = (1, 4, 6, 7)
TOK_W = sum(IN_WIDTHS[k] for k in TOK_COLS)
Q_COLS = (0, 3)
BLOCKS_PER_TILE = TOKEN_TILE // MOBA_BLOCK


def _proj_kernel(x_ref, shift_ref, scale_ref, g_ref, w_ref, *refs):
    n_cast = (len(refs) - 12) // 2
    cast_in, refs = refs[:n_cast], refs[n_cast:]
    ka_ref, kb_ref, ga_ref, gb_ref, qa_ref, va_ref, qb_ref, vb_ref, sel_ref = refs[:9]
    cast_out, (wk_ref, wt_ref, kmean_ref) = refs[9:9 + n_cast], refs[9 + n_cast:]

    for src, dst in zip(cast_in, cast_out):
        dst[...] = src[...].astype(dst.dtype)

    tile = pl.program_id(0) % (kmean_ref.shape[0] // BLOCKS_PER_TILE)

    @pl.when(tile == 0)
    def _():
        kmean_ref[...] = jnp.zeros(kmean_ref.shape, F32)

    @pl.when(pl.program_id(0) == 0)
    def _():
        o = 0
        for k in TOK_COLS:
            wk_ref[:, o:o + IN_WIDTHS[k]] = w_ref[:, IN_OFFS[k]:IN_OFFS[k + 1]].astype(BF16)
            o += IN_WIDTHS[k]
        o = 0
        for k in FEAT_COLS:
            wcol = w_ref[:, IN_OFFS[k]:IN_OFFS[k + 1]]
            if k in Q_COLS:
                wcol = wcol * (ATTN_SCALE * LOG2E)
            wt_ref[o:o + IN_WIDTHS[k], :] = wcol.T.astype(BF16)
            o += IN_WIDTHS[k]

    h = _rms_modulate(x_ref[...], g_ref[...], shift_ref[0, 0], scale_ref[0, 0]).astype(BF16)
    tok_refs = dict(zip(TOK_COLS, (ka_ref, kb_ref, ga_ref, gb_ref)))
    feat_refs = dict(zip(FEAT_COLS, (qa_ref, va_ref, qb_ref, vb_ref)))

    def token_major(k):
        o = sum(IN_WIDTHS[c] for c in TOK_COLS[:TOK_COLS.index(k)])
        out = _dot(h, wk_ref[:, o:o + IN_WIDTHS[k]])
        tok_refs[k][...] = out.astype(BF16)
        return out

    def feature_major(k):
        o = sum(IN_WIDTHS[c] for c in FEAT_COLS[:FEAT_COLS.index(k)])
        out = _dot_nt(wt_ref[o:o + IN_WIDTHS[k], :], h).astype(BF16)
        blk = feat_refs[k].shape[-1]
        for t in range(TOKEN_TILE // blk):
            feat_refs[k][0, t] = out[:, t * blk:(t + 1) * blk]
        return out

    _update_kmean(token_major(4), kmean_ref, tile)
    qb_t = feature_major(3)
    token_major(1)
    token_major(6)
    _moba_select(qb_t, sel_ref, kmean_ref, tile)
    token_major(7)
    for k in (0, 2, 5):
        feature_major(k)


def _update_kmean(kb, kmean_ref, tile):
    for t in range(BLOCKS_PER_TILE):
        kmean_ref[pl.ds(tile * BLOCKS_PER_TILE + t, 1), :] = jnp.mean(
            kb[t * MOBA_BLOCK:(t + 1) * MOBA_BLOCK, :], axis=0, keepdims=True)


def _moba_select(qb_t, sel_ref, kmean_ref, tile):
    nblk = kmean_ref.shape[0]
    blocks_per_tile = BLOCKS_PER_TILE
    blk = lax.broadcasted_iota(jnp.int32, (nblk, TOKEN_TILE), 0)
    qblk = tile * blocks_per_tile + lax.broadcasted_iota(jnp.int32, (nblk, TOKEN_TILE), 1) // MOBA_BLOCK
    past = blk < qblk
    own = jnp.where(blk == qblk, 0.0, NEG_INF)
    blk_f = blk.astype(F32)
    for h in range(MOBA_HEADS):
        hs = slice(h * HEAD_DIM, (h + 1) * HEAD_DIM)
        kmean = kmean_ref[:, hs]
        k1 = kmean.astype(BF16)
        r1 = kmean - k1.astype(F32)
        k2 = r1.astype(BF16)
        k3 = (r1 - k2.astype(F32)).astype(BF16)
        parts = _dot(jnp.concatenate([k1, k2, k3], axis=0), qb_t[hs, :])
        gate = parts[0:nblk] + parts[nblk:2 * nblk] + parts[2 * nblk:3 * nblk]
        gate = jnp.where(past, gate, NEG_INF)
        mask = own
        for _ in range(MOBA_TOPK):
            top = jnp.max(gate, axis=0, keepdims=True)
            first = jnp.min(jnp.where(gate == top, blk_f, float(nblk)), axis=0, keepdims=True)
            pick = blk_f == first
            mask = jnp.where(jnp.logical_and(pick, past), 0.0, mask)
            gate = jnp.where(pick, NEG_INF, gate)
        sel_ref[0, h] = mask


def _proj(x2, mod4, g1, w_in, cast_weights, batch, seq):
    tokens = x2.shape[0]
    steps = tokens // TOKEN_TILE
    assert all(w.shape[0] % (16 * steps) == 0 for w in cast_weights)
    cast_specs = [pl.BlockSpec((w.shape[0] // steps, w.shape[1]), lambda i: (i, 0)) for w in cast_weights]
    tiles_per_seq = seq // TOKEN_TILE
    na = TOKEN_TILE // SWA_BLOCK
    nb = TOKEN_TILE // MOBA_BLOCK

    def feat_spec(width, blk, per_tile):
        return pl.BlockSpec((1, per_tile, width, blk),
                            lambda i: (i // tiles_per_seq, i % tiles_per_seq, 0, 0))

    out_shape = (
        jax.ShapeDtypeStruct((tokens, SWA_KV_W), BF16),
        jax.ShapeDtypeStruct((tokens, MOBA_W), BF16),
        jax.ShapeDtypeStruct((tokens, D_MODEL), BF16),
        jax.ShapeDtypeStruct((tokens, D_MODEL), BF16),
        jax.ShapeDtypeStruct((batch, seq // SWA_BLOCK, SWA_Q_W, SWA_BLOCK), BF16),
        jax.ShapeDtypeStruct((batch, seq // SWA_BLOCK, SWA_KV_W, SWA_BLOCK), BF16),
        jax.ShapeDtypeStruct((batch, seq // MOBA_BLOCK, MOBA_W, MOBA_BLOCK), BF16),
        jax.ShapeDtypeStruct((batch, seq // MOBA_BLOCK, MOBA_W, MOBA_BLOCK), BF16),
        jax.ShapeDtypeStruct((batch, MOBA_HEADS, seq // MOBA_BLOCK, seq), F32),
    )
    tok_spec = lambda w: pl.BlockSpec((TOKEN_TILE, w), lambda i: (i, 0))
    return pl.pallas_call(
        _proj_kernel,
        grid=(steps,),
        in_specs=[
            pl.BlockSpec((TOKEN_TILE, D_MODEL), lambda i: (i, 0)),
            pl.BlockSpec((1, 1, 1, D_MODEL), lambda i: (i // tiles_per_seq, 0, 0, 0)),
            pl.BlockSpec((1, 1, 1, D_MODEL), lambda i: (i // tiles_per_seq, 1, 0, 0)),
            _resident((1, D_MODEL), lambda i: (0, 0)),
            _resident(w_in.shape, lambda i: (0, 0)),
        ] + cast_specs,
        out_specs=(tok_spec(SWA_KV_W), tok_spec(MOBA_W), tok_spec(D_MODEL), tok_spec(D_MODEL),
                   feat_spec(SWA_Q_W, SWA_BLOCK, na), feat_spec(SWA_KV_W, SWA_BLOCK, na),
                   feat_spec(MOBA_W, MOBA_BLOCK, nb), feat_spec(MOBA_W, MOBA_BLOCK, nb),
                   pl.BlockSpec((1, MOBA_HEADS, seq // MOBA_BLOCK, TOKEN_TILE),
                                lambda i: (i // tiles_per_seq, 0, 0, i % tiles_per_seq)))
        + tuple(cast_specs),
        out_shape=out_shape + tuple(jax.ShapeDtypeStruct(w.shape, BF16) for w in cast_weights),
        scratch_shapes=[pltpu.VMEM((D_MODEL, TOK_W), BF16), pltpu.VMEM((FEAT_W, D_MODEL), BF16),
                        pltpu.VMEM((seq // MOBA_BLOCK, MOBA_W), F32)],
        compiler_params=pltpu.CompilerParams(dimension_semantics=("arbitrary",),
                                             vmem_limit_bytes=VMEM_LIMIT_BYTES),
        name="proj",
    )(x2, mod4, mod4, g1, w_in, *cast_weights)


def _swa_kernel(q_ref, k_ref, v_ref, rows_ref, sink_ref, o_ref, bias_ref, ot_ref, s_ref, cm_ref):
    L = SWA_BLOCK
    nblk = q_ref.shape[1]

    @pl.when(pl.program_id(0) == 0)
    def _():
        k_idx = lax.broadcasted_iota(jnp.int32, (L, L), 0)
        q_idx = lax.broadcasted_iota(jnp.int32, (L, L), 1)
        for h in range(SWA_Q_HEADS):
            g, gi = divmod(h, SWA_GROUP)
            t0 = rows_ref[0, h:h + 1, MOBA_BLOCK:MOBA_BLOCK + L]
            r = _toeplitz(jnp.concatenate([t0, t0], axis=1), L)
            cols = slice(gi * L, (gi + 1) * L)
            bias_ref[g, 0:L, cols] = jnp.where(q_idx < k_idx, r, NEG_INF).astype(BF16)
            bias_ref[g, L:2 * L, cols] = jnp.where(q_idx >= k_idx, r, NEG_INF).astype(BF16)

    ones_blk = jnp.concatenate([jnp.ones((1, 2 * L), BF16), jnp.zeros((ACC_ROWS - HEAD_DIM - 1, 2 * L), BF16)],
                               axis=0)
    PW = 2 * L
    chains = [(g, hp) for g in range(SWA_KV_HEADS) for hp in range(SWA_GROUP // 2)]
    kv_cols = [slice(g * HEAD_DIM, (g + 1) * HEAD_DIM) for g in range(SWA_KV_HEADS)]

    def q_pair(c, g, hp):
        h0 = g * SWA_GROUP + 2 * hp
        return jnp.concatenate([q_ref[0, c, (h0 + e) * HEAD_DIM:(h0 + e + 1) * HEAD_DIM, :] for e in range(2)],
                               axis=1)

    def scores_first(idx):
        g, hp = chains[idx]
        lanes = slice(hp * PW, (hp + 1) * PW)
        s_ref[idx, 0:L, :] = jnp.full((L, PW), NEG_INF, BF16)
        s = _dot(k_ref[0, 0:L, kv_cols[g]], q_pair(0, g, hp)).astype(BF16) + bias_ref[g, L:2 * L, lanes]
        s_ref[idx, L:2 * L, :] = s
        cm_ref[idx] = jnp.max(s, axis=0, keepdims=True).astype(F32)

    def scores(c, idx):
        g, hp = chains[idx]
        kw = k_ref[0, pl.ds(pl.multiple_of((c - 1) * L, L), 2 * L), kv_cols[g]]
        s = _dot(kw, q_pair(c, g, hp)).astype(BF16) + bias_ref[g, :, hp * PW:(hp + 1) * PW]
        s_ref[idx] = s
        cm_ref[idx] = jnp.max(s, axis=0, keepdims=True).astype(F32)

    def attend(c, c_prev, idx, slot):
        g, hp = chains[idx]
        hs = kv_cols[g]
        s = s_ref[idx]
        sink = sink_ref[g, :, hp * PW:(hp + 1) * PW]
        m_b = jnp.maximum(cm_ref[idx], sink).astype(BF16)
        m = m_b.astype(F32)
        p = jnp.exp2(s - m_b)
        v_win = jnp.concatenate([v_ref[0, c_prev, hs, :], v_ref[0, c, hs, :]], axis=1)
        pv = _dot(jnp.concatenate([v_win, ones_blk], axis=0), p)
        o = pv[0:HEAD_DIM] / (pv[HEAD_DIM:HEAD_DIM + 1] + jnp.exp2(sink - m))
        for e in range(2):
            r = (g * SWA_GROUP + 2 * hp + e) * HEAD_DIM
            ot_ref[slot, r:r + HEAD_DIM, :] = o[:, e * L:(e + 1) * L]

    def emit(c, slot):
        o_ref[0, pl.ds(pl.multiple_of(c * L, L), L), :] = ot_ref[slot].T.astype(o_ref.dtype)

    for idx in range(len(chains)):
        scores_first(idx)
    U = SWA_UNROLL
    for idx in range(len(chains)):
        attend(0, 0, idx, U - 1)
        scores(1, idx)

    def body(c, e):
        emit(c - 1, (e - 1) % U)
        for idx in range(len(chains)):
            attend(c, c - 1, idx, e)
            scores(c + 1, idx)

    def body_unrolled(u, carry):
        for e in range(U):
            body(U * u + 1 + e, e)
        return carry

    lax.fori_loop(0, (nblk - 2) // U, body_unrolled, 0)
    emit(nblk - 2, U - 1)
    for idx in range(len(chains)):
        attend(nblk - 1, nblk - 2, idx, 0)
    emit(nblk - 1, 0)


def _swa(qa_t, ka3, va_t, rows, sink_rows):
    batch, nblk = qa_t.shape[0], qa_t.shape[1]
    seq = nblk * SWA_BLOCK
    assert nblk >= 2 and (nblk - 2) % SWA_UNROLL == 0 and SWA_UNROLL >= 2
    return pl.pallas_call(
        _swa_kernel,
        grid=(batch,),
        in_specs=[
            pl.BlockSpec((1, nblk, SWA_Q_W, SWA_BLOCK), lambda b: (b, 0, 0, 0)),
            pl.BlockSpec((1, seq, SWA_KV_W), lambda b: (b, 0, 0)),
            pl.BlockSpec((1, nblk, SWA_KV_W, SWA_BLOCK), lambda b: (b, 0, 0, 0)),
            _resident((1, N_ATTN_HEADS, 2 * MOBA_BLOCK), lambda b: (0, 0, 0)),
            _resident(sink_rows.shape, lambda b: (0, 0, 0)),
        ],
        out_specs=pl.BlockSpec((1, seq, SWA_Q_W), lambda b: (b, 0, 0)),
        out_shape=jax.ShapeDtypeStruct((batch, seq, SWA_Q_W), BF16),
        scratch_shapes=[pltpu.VMEM((SWA_KV_HEADS, 2 * SWA_BLOCK, SWA_GROUP * SWA_BLOCK), BF16),
                        pltpu.VMEM((SWA_UNROLL, SWA_Q_W, SWA_BLOCK), F32),
                        pltpu.VMEM((SWA_Q_HEADS // 2, 2 * SWA_BLOCK, 2 * SWA_BLOCK), BF16),
                        pltpu.VMEM((SWA_Q_HEADS // 2, 1, 2 * SWA_BLOCK), F32)],
        compiler_params=pltpu.CompilerParams(dimension_semantics=("arbitrary",),
                                             vmem_limit_bytes=VMEM_LIMIT_BYTES),
        name="swa",
    )(qa_t, ka3, va_t, rows, sink_rows)


def _moba_kernel(q_ref, qn_ref, k_ref, v_ref, sel_ref, rows_ref, o_ref, bias_ref, m_ref, acc_ref, s_ref, cm_ref):
    step = pl.program_id(1)
    last = pl.num_programs(1) - 1
    MB = MOBA_BLOCK
    H = MOBA_HEADS
    G = MOBA_Q_PER_STEP

    @pl.when(jnp.logical_and(pl.program_id(0) == 0, step == 0))
    def _():
        for h in range(H):
            for d in range(BIAS_SLOTS):
                bias_ref[h, d] = _toeplitz(rows_ref[d, SWA_Q_HEADS + h:SWA_Q_HEADS + h + 1, :], MB).astype(BF16)

    m_ref[...] = jnp.full(m_ref.shape, NEG_INF, F32)
    acc_ref[...] = jnp.zeros(acc_ref.shape, F32)
    heads = [slice(h * HEAD_DIM, (h + 1) * HEAD_DIM) for h in range(H)]
    ones_blk = jnp.concatenate([jnp.ones((1, MB), BF16), jnp.zeros((ACC_ROWS - HEAD_DIM - 1, MB), BF16)],
                               axis=0)

    def sel_row(j, qs, h):
        return sel_ref[0, h, pl.ds(j, 1), qs * MB:(qs + 1) * MB]

    def scores(j, qs, h):
        c = qs * H + h
        rows = pl.ds(pl.multiple_of(j * MB, MB), MB)
        slot = jnp.minimum(G * step + qs - j, BIAS_SLOTS - 1)
        s = _dot(k_ref[0, rows, heads[h]], q_ref[0, qs, heads[h], :]).astype(BF16) + bias_ref[h, slot]
        s_ref[c] = s
        cm_ref[c] = jnp.max(s, axis=0, keepdims=True).astype(F32) + sel_row(j, qs, h)

    def own_scores(blk, q_blk, c, h):
        rows = pl.ds(pl.multiple_of(blk * MB, MB), MB)
        s = _dot(k_ref[0, rows, heads[h]], q_blk[heads[h], :]).astype(BF16) + bias_ref[h, 0]
        s_ref[c] = s
        cm_ref[c] = jnp.max(s, axis=0, keepdims=True).astype(F32)

    def accumulate(j, qs, h):
        c = qs * H + h
        m_old = m_ref[c]
        m_new = jnp.maximum(m_old, cm_ref[c])
        alpha = jnp.exp2(m_old - m_new)
        p = jnp.exp2(s_ref[c] - (m_new - sel_row(j, qs, h)).astype(BF16))
        v_aug = jnp.concatenate([v_ref[0, j, heads[h], :], ones_blk], axis=0)
        rs = slice(c * ACC_ROWS, (c + 1) * ACC_ROWS)
        acc_ref[rs, :] = alpha * acc_ref[rs, :] + _dot(v_aug, p)
        m_ref[c] = m_new

    first = G * step

    @pl.when(step == 0)
    def _():
        for qs in range(G):
            for h in range(H):
                own_scores(qs, q_ref[0, qs], qs * H + h, h)

    for n in range(G - 1):
        for qs in range(G - 1 - n, G):
            for h in range(H):
                accumulate(first + G - 1 - n, qs, h)
                scores(first + G - 2 - n, qs, h)

    def body(j, carry):
        for qs in range(G):
            for h in range(H):
                accumulate(j, qs, h)
                scores(j - 1, qs, h)
        return carry

    def body2(u, carry):
        return body(first - 2 * u - 1, body(first - 2 * u, carry))

    lax.fori_loop(0, (G // 2) * step, body2, 0)

    @pl.when(step < last)
    def _():
        for h in range(H):
            for qs in range(G):
                accumulate(0, qs, h)
                own_scores(first + G + qs, qn_ref[0, qs], qs * H + h, h)

    @pl.when(step == last)
    def _():
        for h in range(H):
            for qs in range(G):
                accumulate(0, qs, h)

    for qs in range(G):
        outs = []
        for h in range(H):
            base = (qs * H + h) * ACC_ROWS
            outs.append(acc_ref[base:base + HEAD_DIM, :] / acc_ref[base + HEAD_DIM:base + HEAD_DIM + 1, :])
        o_ref[0, qs * MB:(qs + 1) * MB, :] = jnp.concatenate(outs, axis=0).T.astype(o_ref.dtype)


def _moba(qb_t, kb3, vb_t, sel, rows):
    batch, nblk = qb_t.shape[0], qb_t.shape[1]
    seq = nblk * MOBA_BLOCK
    g = MOBA_Q_PER_STEP
    assert g % 2 == 0 and nblk % g == 0
    chains = g * MOBA_HEADS
    return pl.pallas_call(
        _moba_kernel,
        grid=(batch, nblk // g),
        in_specs=[
            pl.BlockSpec((1, g, MOBA_W, MOBA_BLOCK), lambda b, i: (b, i, 0, 0)),
            pl.BlockSpec((1, g, MOBA_W, MOBA_BLOCK), lambda b, i: (b, jnp.minimum(i + 1, nblk // g - 1), 0, 0)),
            pl.BlockSpec((1, seq, MOBA_W), lambda b, i: (b, 0, 0)),
            pl.BlockSpec((1, nblk, MOBA_W, MOBA_BLOCK), lambda b, i: (b, 0, 0, 0)),
            pl.BlockSpec((1, MOBA_HEADS, nblk, g * MOBA_BLOCK), lambda b, i: (b, 0, 0, i)),
            _resident(rows.shape, lambda b, i: (0, 0, 0)),
        ],
        out_specs=pl.BlockSpec((1, g * MOBA_BLOCK, MOBA_W), lambda b, i: (b, i, 0)),
        out_shape=jax.ShapeDtypeStruct((batch, seq, MOBA_W), BF16),
        scratch_shapes=[pltpu.VMEM((MOBA_HEADS, BIAS_SLOTS, MOBA_BLOCK, MOBA_BLOCK), BF16),
                        pltpu.VMEM((chains, 1, MOBA_BLOCK), F32),
                        pltpu.VMEM((chains * ACC_ROWS, MOBA_BLOCK), F32),
                        pltpu.VMEM((chains, MOBA_BLOCK, MOBA_BLOCK), BF16),
                        pltpu.VMEM((chains, 1, MOBA_BLOCK), F32)],
        compiler_params=pltpu.CompilerParams(dimension_semantics=("arbitrary", "arbitrary"),
                                             vmem_limit_bytes=VMEM_LIMIT_BYTES),
        name="moba",
    )(qb_t, qb_t, kb3, vb_t, sel, rows)


def _post_kernel(x_ref, ya_ref, yb_ref, ga_ref, gb_ref, gate1_ref, shift2_ref, scale2_ref,
                 gate2_ref, g2_ref, gf_ref, wa_ref, wb_ref, wo_ref, w1_ref, w2_ref, o_ref):
    a = _dot(ya_ref[...], wa_ref[...])
    b = _dot(yb_ref[...], wb_ref[...])
    merged = (jax.nn.sigmoid(ga_ref[...].astype(F32)) * a
              + jax.nn.sigmoid(gb_ref[...].astype(F32)) * b)
    x1 = x_ref[...] + gate1_ref[0, 0] * _dot(merged.astype(BF16), wo_ref[...])
    h2 = _rms_modulate(x1, g2_ref[...], shift2_ref[0, 0], scale2_ref[0, 0]).astype(BF16)
    y = jnp.zeros_like(x1)
    for c in range(D_FF // FF_CHUNK):
        cs = slice(c * FF_CHUNK, (c + 1) * FF_CHUNK)
        u = jnp.square(jnp.maximum(_dot(h2, w1_ref[:, cs]), 0.0)).astype(BF16)
        y = y + _dot(u, w2_ref[cs, :])
    x2 = x1 + gate2_ref[0, 0] * y
    ms = jnp.mean(x2 * x2, axis=-1, keepdims=True)
    o_ref[...] = (x2 * lax.rsqrt(ms + RMS_EPS)) * gf_ref[...]


def _post(x2, ya, yb, ga, gb, mod4, g2, gf, wa, wb, wo, w1, w2, seq):
    tokens = x2.shape[0]
    tiles_per_seq = seq // TOKEN_TILE
    tok = lambda w: pl.BlockSpec((TOKEN_TILE, w), lambda i: (i, 0))
    modrow = lambda k: pl.BlockSpec((1, 1, 1, D_MODEL), lambda i: (i // tiles_per_seq, k, 0, 0))
    full = lambda a: _resident(a.shape, lambda i: (0, 0))
    return pl.pallas_call(
        _post_kernel,
        grid=(tokens // TOKEN_TILE,),
        in_specs=[tok(D_MODEL), tok(SWA_Q_W), tok(MOBA_W), tok(D_MODEL), tok(D_MODEL),
                  modrow(2), modrow(3), modrow(4), modrow(5),
                  full(g2), full(gf), full(wa), full(wb), full(wo), full(w1), full(w2)],
        out_specs=tok(D_MODEL),
        out_shape=jax.ShapeDtypeStruct((tokens, D_MODEL), F32),
        compiler_params=pltpu.CompilerParams(dimension_semantics=("arbitrary",),
                                             vmem_limit_bytes=VMEM_LIMIT_BYTES),
        name="post",
    )(x2, ya, yb, ga, gb, mod4, mod4, mod4, mod4, g2, gf, wa, wb, wo, w1, w2)


def kernel(x, c, ada_w, ada_b, norm1_g, norm2_g, w_in, attn_sinks, rel_bias, w_branch_a,
           w_branch_b, w_out, w_mlp_in, w_mlp_out, final_g):
    batch, seq, _ = x.shape
    depth = ada_w.shape[0]
    assert seq % TOKEN_TILE == 0 and TOKEN_TILE % MOBA_BLOCK == 0 and batch <= SUBLANES
    far = np.arange((BIAS_SLOTS - 1) * MOBA_BLOCK - (MOBA_BLOCK - 1), max(seq, BIAS_SLOTS * MOBA_BLOCK))
    assert np.all(_t5_bucket(far) == _t5_bucket(far[:1]))
    assert 2 * SWA_BLOCK <= MOBA_BLOCK and SWA_WINDOW == SWA_BLOCK
    assert depth == 1
    l = 0


    x2 = x.reshape(batch * seq, D_MODEL)
    mod4, rows, sink_rows = _ada(c.astype(F32), ada_w[l], ada_b[l][None, :], rel_bias, attn_sinks[l])
    later_weights = [w.astype(F32) for w in (w_branch_a[l], w_branch_b[l], w_out[l], w_mlp_in[l], w_mlp_out[l])]
    ka, kb, ga, gb, qa_t, va_t, qb_t, vb_t, sel, wa, wb, wo, w1, w2 = _proj(
        x2, mod4, norm1_g[l][None, :], w_in[l].astype(F32), later_weights, batch, seq)

    ya = _swa(qa_t, ka.reshape(batch, seq, SWA_KV_W), va_t, rows, sink_rows)

    yb = _moba(qb_t, kb.reshape(batch, seq, MOBA_W), vb_t, sel, rows)

    out = _post(x2, ya.reshape(batch * seq, SWA_Q_W), yb.reshape(batch * seq, MOBA_W), ga, gb,
                mod4, norm2_g[l][None, :], final_g[None, :], wa, wb, wo, w1, w2, seq)
    return out.reshape(batch, seq, D_MODEL)
```

```python
import math

import jax
import jax.numpy as jnp
import numpy as np
from jax import lax
from jax.experimental import pallas as pl
from jax.experimental.pallas import tpu as pltpu

D_MODEL = 1024
HEAD_DIM = 64
ATTN_SCALE = HEAD_DIM ** -0.5
LOG2E = math.log2(math.e)
SWA_Q_HEADS = 8
SWA_KV_HEADS = 2
SWA_GROUP = SWA_Q_HEADS // SWA_KV_HEADS
SWA_WINDOW = 128
SWA_BLOCK = 128
MOBA_HEADS = 8
MOBA_BLOCK = 256
MOBA_TOPK = 3
NUM_BUCKETS = 32
MAX_EXACT = NUM_BUCKETS // 2
MAX_DISTANCE = 2048
N_ATTN_HEADS = SWA_Q_HEADS + MOBA_HEADS
SWA_Q_W = SWA_Q_HEADS * HEAD_DIM
SWA_KV_W = SWA_KV_HEADS * HEAD_DIM
MOBA_W = MOBA_HEADS * HEAD_DIM
D_FF = 4 * D_MODEL
N_MOD = 6
RMS_EPS = 1e-6

VMEM_LIMIT_BYTES = 56 * 1024 * 1024
TOKEN_TILE = 512
FF_CHUNK = 1024
SUBLANES = 8
BIAS_SLOTS = 8
SWA_UNROLL = 3
MOBA_Q_PER_STEP = 4
ACC_ROWS = HEAD_DIM + 16
NEG_INF = float("-inf")

F32 = jnp.float32
BF16 = jnp.bfloat16


def _resident(block_shape, index_map):
    return pl.BlockSpec(block_shape, index_map, pipeline_mode=pl.Buffered(1))


def _dot(a, b):
    return jnp.dot(a, b, preferred_element_type=F32)


def _dot_nt(a, b):
    return lax.dot_general(a, b, (((1,), (1,)), ((), ())), preferred_element_type=F32)


def _rms_modulate(xv, g, shift, scale):
    ms = jnp.mean(xv * xv, axis=-1, keepdims=True)
    y = xv * lax.rsqrt(ms + RMS_EPS)
    return (y * g) * (1.0 + scale) + shift


def _toeplitz(u, rows):
    wide = jnp.broadcast_to(u, (rows, 2 * rows))
    return pltpu.roll(wide, 0, 1, stride=1, stride_axis=0)[:, rows:]


def _t5_bucket(dist):
    n = np.maximum(dist, 0)
    nf = np.maximum(n, 1).astype(np.float32)
    large = MAX_EXACT + (np.log(nf / np.float32(MAX_EXACT)) / np.float32(math.log(MAX_DISTANCE / MAX_EXACT))
                         * np.float32(NUM_BUCKETS - MAX_EXACT)).astype(np.int32)
    large = np.minimum(large, NUM_BUCKETS - 1)
    return np.where(n < MAX_EXACT, n, large).astype(np.int32)


def _bias_rows(bucket_ref, rbt_ref, o_ref):
    bucket = bucket_ref[...]
    n = bucket.shape[1]
    onehot = jnp.where(lax.broadcasted_iota(jnp.int32, (NUM_BUCKETS, n), 0) == bucket, 1.0, 0.0)
    rows = jnp.dot(rbt_ref[...], onehot, preferred_element_type=F32,
                   precision=lax.Precision.HIGHEST)
    rows = jnp.where(bucket >= 0, rows * LOG2E, NEG_INF)
    for d in range(BIAS_SLOTS):
        o_ref[d] = rows[:, d * MOBA_BLOCK:(d + 2) * MOBA_BLOCK]


def _ada_kernel(c_ref, wlo_ref, whi_ref, b_ref, bucket_ref, rbt_ref, sinks_ref, o_ref, rows_ref, sink_rows_ref,
                cs_ref):
    batch = c_ref.shape[0]
    half = wlo_ref.shape[0]
    cs_ref[...] = jnp.zeros(cs_ref.shape, F32)
    cs_ref[0:batch, :] = jax.nn.silu(c_ref[...])
    cs = cs_ref[...].astype(BF16)
    res = (_dot(cs[:, :half], wlo_ref[...].astype(BF16)) + _dot(cs[:, half:], whi_ref[...].astype(BF16))
           + b_ref[...])
    for b in range(batch):
        o_ref[b, 0] = res[b:b + 1]

    @pl.when(pl.program_id(0) == 0)
    def _():
        _bias_rows(bucket_ref, rbt_ref, rows_ref)
        sinks = sinks_ref[...] * LOG2E
        for h in range(SWA_Q_HEADS):
            g, gi = divmod(h, SWA_GROUP)
            sink_rows_ref[g, :, gi * SWA_BLOCK:(gi + 1) * SWA_BLOCK] = jnp.broadcast_to(
                sinks[:, h:h + 1], (1, SWA_BLOCK))


def _ada(c, w, b, rel_bias, sinks):
    batch = c.shape[0]
    assert w.shape[1] == N_MOD * D_MODEL
    n = (BIAS_SLOTS + 1) * MOBA_BLOCK
    dist = np.arange(n, dtype=np.int32) - MOBA_BLOCK
    bucket = jnp.asarray(np.where(dist >= 0, _t5_bucket(dist), -1)[None, :], dtype=jnp.int32)
    rows_shape = (BIAS_SLOTS, N_ATTN_HEADS, 2 * MOBA_BLOCK)
    sink_shape = (SWA_KV_HEADS, 1, SWA_GROUP * SWA_BLOCK)
    return pl.pallas_call(
        _ada_kernel,
        grid=(N_MOD,),
        in_specs=[pl.BlockSpec((batch, D_MODEL), lambda j: (0, 0)),
                  pl.BlockSpec((D_MODEL // 2, D_MODEL), lambda j: (0, j)),
                  pl.BlockSpec((D_MODEL // 2, D_MODEL), lambda j: (1, j)),
                  pl.BlockSpec((1, D_MODEL), lambda j: (0, j)),
                  pl.BlockSpec((1, n), lambda j: (0, 0)),
                  pl.BlockSpec((N_ATTN_HEADS, NUM_BUCKETS), lambda j: (0, 0)),
                  pl.BlockSpec((1, SWA_Q_HEADS), lambda j: (0, 0))],
        out_specs=(pl.BlockSpec((batch, 1, 1, D_MODEL), lambda j: (0, j, 0, 0)),
                   pl.BlockSpec(rows_shape, lambda j: (0, 0, 0)),
                   pl.BlockSpec(sink_shape, lambda j: (0, 0, 0))),
        out_shape=(jax.ShapeDtypeStruct((batch, N_MOD, 1, D_MODEL), F32),
                   jax.ShapeDtypeStruct(rows_shape, F32),
                   jax.ShapeDtypeStruct(sink_shape, F32)),
        scratch_shapes=[pltpu.VMEM((SUBLANES, D_MODEL), F32)],
        compiler_params=pltpu.CompilerParams(dimension_semantics=("arbitrary",),
                                             vmem_limit_bytes=VMEM_LIMIT_BYTES),
        name="ada",
    )(c, w, w, b, bucket, rel_bias.astype(F32).T, sinks.astype(F32)[None, :])


IN_WIDTHS = (SWA_Q_W, SWA_KV_W, SWA_KV_W, MOBA_W, MOBA_W, MOBA_W, D_MODEL, D_MODEL)
IN_OFFS = tuple(sum(IN_WIDTHS[:k]) for k in range(len(IN_WIDTHS) + 1))
FEAT_COLS = (0, 2, 3, 5)
FEAT_W = sum(IN_WIDTHS[k] for k in FEAT_COLS)
TOK_COLS = (1, 4, 6, 7)
TOK_W = sum(IN_WIDTHS[k] for k in TOK_COLS)
Q_COLS = (0, 3)
BLOCKS_PER_TILE = TOKEN_TILE // MOBA_BLOCK


def _proj_kernel(x_ref, shift_ref, scale_ref, g_ref, w_ref, *refs):
    n_cast = (len(refs) - 12) // 2
    cast_in, refs = refs[:n_cast], refs[n_cast:]
    ka_ref, kb_ref, ga_ref, gb_ref, qa_ref, va_ref, qb_ref, vb_ref, sel_ref = refs[:9]
    cast_out, (wk_ref, wt_ref, kmean_ref) = refs[9:9 + n_cast], refs[9 + n_cast:]

    for src, dst in zip(cast_in, cast_out):
        dst[...] = src[...].astype(dst.dtype)

    tile = pl.program_id(0) % (kmean_ref.shape[0] // BLOCKS_PER_TILE)

    @pl.when(tile == 0)
    def _():
        kmean_ref[...] = jnp.zeros(kmean_ref.shape, F32)

    @pl.when(pl.program_id(0) == 0)
    def _():
        o = 0
        for k in TOK_COLS:
            wk_ref[:, o:o + IN_WIDTHS[k]] = w_ref[:, IN_OFFS[k]:IN_OFFS[k + 1]].astype(BF16)
            o += IN_WIDTHS[k]
        o = 0
        for k in FEAT_COLS:
            wcol = w_ref[:, IN_OFFS[k]:IN_OFFS[k + 1]]
            if k in Q_COLS:
                wcol = wcol * (ATTN_SCALE * LOG2E)
            wt_ref[o:o + IN_WIDTHS[k], :] = wcol.T.astype(BF16)
            o += IN_WIDTHS[k]

    h = _rms_modulate(x_ref[...], g_ref[...], shift_ref[0, 0], scale_ref[0, 0]).astype(BF16)
    tok_refs = dict(zip(TOK_COLS, (ka_ref, kb_ref, ga_ref, gb_ref)))
    feat_refs = dict(zip(FEAT_COLS, (qa_ref, va_ref, qb_ref, vb_ref)))

    def token_major(k):
        o = sum(IN_WIDTHS[c] for c in TOK_COLS[:TOK_COLS.index(k)])
        out = _dot(h, wk_ref[:, o:o + IN_WIDTHS[k]])
        tok_refs[k][...] = out.astype(BF16)
        return out

    def feature_major(k):
        o = sum(IN_WIDTHS[c] for c in FEAT_COLS[:FEAT_COLS.index(k)])
        out = _dot_nt(wt_ref[o:o + IN_WIDTHS[k], :], h).astype(BF16)
        blk = feat_refs[k].shape[-1]
        for t in range(TOKEN_TILE // blk):
            feat_refs[k][0, t] = out[:, t * blk:(t + 1) * blk]
        return out

    _update_kmean(token_major(4), kmean_ref, tile)
    qb_t = feature_major(3)
    token_major(1)
    token_major(6)
    _moba_select(qb_t, sel_ref, kmean_ref, tile)
    token_major(7)
    for k in (0, 2, 5):
        feature_major(k)


def _update_kmean(kb, kmean_ref, tile):
    for t in range(BLOCKS_PER_TILE):
        kmean_ref[pl.ds(tile * BLOCKS_PER_TILE + t, 1), :] = jnp.mean(
            kb[t * MOBA_BLOCK:(t + 1) * MOBA_BLOCK, :], axis=0, keepdims=True)


def _moba_select(qb_t, sel_ref, kmean_ref, tile):
    nblk = kmean_ref.shape[0]
    blocks_per_tile = BLOCKS_PER_TILE
    blk = lax.broadcasted_iota(jnp.int32, (nblk, TOKEN_TILE), 0)
    qblk = tile * blocks_per_tile + lax.broadcasted_iota(jnp.int32, (nblk, TOKEN_TILE), 1) // MOBA_BLOCK
    past = blk < qblk
    own = jnp.where(blk == qblk, 0.0, NEG_INF)
    blk_f = blk.astype(F32)
    for h in range(MOBA_HEADS):
        hs = slice(h * HEAD_DIM, (h + 1) * HEAD_DIM)
        kmean = kmean_ref[:, hs]
        k1 = kmean.astype(BF16)
        r1 = kmean - k1.astype(F32)
        k2 = r1.astype(BF16)
        k3 = (r1 - k2.astype(F32)).astype(BF16)
        parts = _dot(jnp.concatenate([k1, k2, k3], axis=0), qb_t[hs, :])
        gate = parts[0:nblk] + parts[nblk:2 * nblk] + parts[2 * nblk:3 * nblk]
        gate = jnp.where(past, gate, NEG_INF)
        mask = own
        for _ in range(MOBA_TOPK):
            top = jnp.max(gate, axis=0, keepdims=True)
            first = jnp.min(jnp.where(gate == top, blk_f, float(nblk)), axis=0, keepdims=True)
            pick = blk_f == first
            mask = jnp.where(jnp.logical_and(pick, past), 0.0, mask)
            gate = jnp.where(pick, NEG_INF, gate)
        sel_ref[0, h] = mask


def _proj(x2, mod4, g1, w_in, cast_weights, batch, seq):
    tokens = x2.shape[0]
    steps = tokens // TOKEN_TILE
    assert all(w.shape[0] % (16 * steps) == 0 for w in cast_weights)
    cast_specs = [pl.BlockSpec((w.shape[0] // steps, w.shape[1]), lambda i: (i, 0)) for w in cast_weights]
    tiles_per_seq = seq // TOKEN_TILE
    na = TOKEN_TILE // SWA_BLOCK
    nb = TOKEN_TILE // MOBA_BLOCK

    def feat_spec(width, blk, per_tile):
        return pl.BlockSpec((1, per_tile, width, blk),
                            lambda i: (i // tiles_per_seq, i % tiles_per_seq, 0, 0))

    out_shape = (
        jax.ShapeDtypeStruct((tokens, SWA_KV_W), BF16),
        jax.ShapeDtypeStruct((tokens, MOBA_W), BF16),
        jax.ShapeDtypeStruct((tokens, D_MODEL), BF16),
        jax.ShapeDtypeStruct((tokens, D_MODEL), BF16),
        jax.ShapeDtypeStruct((batch, seq // SWA_BLOCK, SWA_Q_W, SWA_BLOCK), BF16),
        jax.ShapeDtypeStruct((batch, seq // SWA_BLOCK, SWA_KV_W, SWA_BLOCK), BF16),
        jax.ShapeDtypeStruct((batch, seq // MOBA_BLOCK, MOBA_W, MOBA_BLOCK), BF16),
        jax.ShapeDtypeStruct((batch, seq // MOBA_BLOCK, MOBA_W, MOBA_BLOCK), BF16),
        jax.ShapeDtypeStruct((batch, MOBA_HEADS, seq // MOBA_BLOCK, seq), F32),
    )
    tok_spec = lambda w: pl.BlockSpec((TOKEN_TILE, w), lambda i: (i, 0))
    return pl.pallas_call(
        _proj_kernel,
        grid=(steps,),
        in_specs=[
            pl.BlockSpec((TOKEN_TILE, D_MODEL), lambda i: (i, 0)),
            pl.BlockSpec((1, 1, 1, D_MODEL), lambda i: (i // tiles_per_seq, 0, 0, 0)),
            pl.BlockSpec((1, 1, 1, D_MODEL), lambda i: (i // tiles_per_seq, 1, 0, 0)),
            _resident((1, D_MODEL), lambda i: (0, 0)),
            _resident(w_in.shape, lambda i: (0, 0)),
        ] + cast_specs,
        out_specs=(tok_spec(SWA_KV_W), tok_spec(MOBA_W), tok_spec(D_MODEL), tok_spec(D_MODEL),
                   feat_spec(SWA_Q_W, SWA_BLOCK, na), feat_spec(SWA_KV_W, SWA_BLOCK, na),
                   feat_spec(MOBA_W, MOBA_BLOCK, nb), feat_spec(MOBA_W, MOBA_BLOCK, nb),
                   pl.BlockSpec((1, MOBA_HEADS, seq // MOBA_BLOCK, TOKEN_TILE),
                                lambda i: (i // tiles_per_seq, 0, 0, i % tiles_per_seq)))
        + tuple(cast_specs),
        out_shape=out_shape + tuple(jax.ShapeDtypeStruct(w.shape, BF16) for w in cast_weights),
        scratch_shapes=[pltpu.VMEM((D_MODEL, TOK_W), BF16), pltpu.VMEM((FEAT_W, D_MODEL), BF16),
                        pltpu.VMEM((seq // MOBA_BLOCK, MOBA_W), F32)],
        compiler_params=pltpu.CompilerParams(dimension_semantics=("arbitrary",),
                                             vmem_limit_bytes=VMEM_LIMIT_BYTES),
        name="proj",
    )(x2, mod4, mod4, g1, w_in, *cast_weights)


def _swa_kernel(q_ref, k_ref, v_ref, rows_ref, sink_ref, o_ref, bias_ref, ot_ref, s_ref, cm_ref):
    L = SWA_BLOCK
    nblk = q_ref.shape[1]

    @pl.when(pl.program_id(0) == 0)
    def _():
        k_idx = lax.broadcasted_iota(jnp.int32, (L, L), 0)
        q_idx = lax.broadcasted_iota(jnp.int32, (L, L), 1)
        for h in range(SWA_Q_HEADS):
            g, gi = divmod(h, SWA_GROUP)
            t0 = rows_ref[0, h:h + 1, MOBA_BLOCK:MOBA_BLOCK + L]
            r = _toeplitz(jnp.concatenate([t0, t0], axis=1), L)
            cols = slice(gi * L, (gi + 1) * L)
            bias_ref[g, 0:L, cols] = jnp.where(q_idx < k_idx, r, NEG_INF).astype(BF16)
            bias_ref[g, L:2 * L, cols] = jnp.where(q_idx >= k_idx, r, NEG_INF).astype(BF16)

    ones_blk = jnp.concatenate([jnp.ones((1, 2 * L), BF16), jnp.zeros((ACC_ROWS - HEAD_DIM - 1, 2 * L), BF16)],
                               axis=0)
    PW = 2 * L
    chains = [(g, hp) for g in range(SWA_KV_HEADS) for hp in range(SWA_GROUP // 2)]
    kv_cols = [slice(g * HEAD_DIM, (g + 1) * HEAD_DIM) for g in range(SWA_KV_HEADS)]

    def q_pair(c, g, hp):
        h0 = g * SWA_GROUP + 2 * hp
        return jnp.concatenate([q_ref[0, c, (h0 + e) * HEAD_DIM:(h0 + e + 1) * HEAD_DIM, :] for e in range(2)],
                               axis=1)

    def scores_first(idx):
        g, hp = chains[idx]
        lanes = slice(hp * PW, (hp + 1) * PW)
        s_ref[idx, 0:L, :] = jnp.full((L, PW), NEG_INF, BF16)
        s = _dot(k_ref[0, 0:L, kv_cols[g]], q_pair(0, g, hp)).astype(BF16) + bias_ref[g, L:2 * L, lanes]
        s_ref[idx, L:2 * L, :] = s
        cm_ref[idx] = jnp.max(s, axis=0, keepdims=True).astype(F32)

    def scores(c, idx):
        g, hp = chains[idx]
        kw = k_ref[0, pl.ds(pl.multiple_of((c - 1) * L, L), 2 * L), kv_cols[g]]
        s = _dot(kw, q_pair(c, g, hp)).astype(BF16) + bias_ref[g, :, hp * PW:(hp + 1) * PW]
        s_ref[idx] = s
        cm_ref[idx] = jnp.max(s, axis=0, keepdims=True).astype(F32)

    def attend(c, c_prev, idx, slot):
        g, hp = chains[idx]
        hs = kv_cols[g]
        s = s_ref[idx]
        sink = sink_ref[g, :, hp * PW:(hp + 1) * PW]
        m_b = jnp.maximum(cm_ref[idx], sink).astype(BF16)
        m = m_b.astype(F32)
        p = jnp.exp2(s - m_b)
        v_win = jnp.concatenate([v_ref[0, c_prev, hs, :], v_ref[0, c, hs, :]], axis=1)
        pv = _dot(jnp.concatenate([v_win, ones_blk], axis=0), p)
        o = pv[0:HEAD_DIM] / (pv[HEAD_DIM:HEAD_DIM + 1] + jnp.exp2(sink - m))
        for e in range(2):
            r = (g * SWA_GROUP + 2 * hp + e) * HEAD_DIM
            ot_ref[slot, r:r + HEAD_DIM, :] = o[:, e * L:(e + 1) * L]

    def emit(c, slot):
        o_ref[0, pl.ds(pl.multiple_of(c * L, L), L), :] = ot_ref[slot].T.astype(o_ref.dtype)

    for idx in range(len(chains)):
        scores_first(idx)
    U = SWA_UNROLL
    for idx in range(len(chains)):
        attend(0, 0, idx, U - 1)
        scores(1, idx)

    def body(c, e):
        emit(c - 1, (e - 1) % U)
        for idx in range(len(chains)):
            attend(c, c - 1, idx, e)
            scores(c + 1, idx)

    def body_unrolled(u, carry):
        for e in range(U):
            body(U * u + 1 + e, e)
        return carry

    lax.fori_loop(0, (nblk - 2) // U, body_unrolled, 0)
    emit(nblk - 2, U - 1)
    for idx in range(len(chains)):
        attend(nblk - 1, nblk - 2, idx, 0)
    emit(nblk - 1, 0)


def _swa(qa_t, ka3, va_t, rows, sink_rows):
    batch, nblk = qa_t.shape[0], qa_t.shape[1]
    seq = nblk * SWA_BLOCK
    assert nblk >= 2 and (nblk - 2) % SWA_UNROLL == 0 and SWA_UNROLL >= 2
    return pl.pallas_call(
        _swa_kernel,
        grid=(batch,),
        in_specs=[
            pl.BlockSpec((1, nblk, SWA_Q_W, SWA_BLOCK), lambda b: (b, 0, 0, 0)),
            pl.BlockSpec((1, seq, SWA_KV_W), lambda b: (b, 0, 0)),
            pl.BlockSpec((1, nblk, SWA_KV_W, SWA_BLOCK), lambda b: (b, 0, 0, 0)),
            _resident((1, N_ATTN_HEADS, 2 * MOBA_BLOCK), lambda b: (0, 0, 0)),
            _resident(sink_rows.shape, lambda b: (0, 0, 0)),
        ],
        out_specs=pl.BlockSpec((1, seq, SWA_Q_W), lambda b: (b, 0, 0)),
        out_shape=jax.ShapeDtypeStruct((batch, seq, SWA_Q_W), BF16),
        scratch_shapes=[pltpu.VMEM((SWA_KV_HEADS, 2 * SWA_BLOCK, SWA_GROUP * SWA_BLOCK), BF16),
                        pltpu.VMEM((SWA_UNROLL, SWA_Q_W, SWA_BLOCK), F32),
                        pltpu.VMEM((SWA_Q_HEADS // 2, 2 * SWA_BLOCK, 2 * SWA_BLOCK), BF16),
                        pltpu.VMEM((SWA_Q_HEADS // 2, 1, 2 * SWA_BLOCK), F32)],
        compiler_params=pltpu.CompilerParams(dimension_semantics=("arbitrary",),
                                             vmem_limit_bytes=VMEM_LIMIT_BYTES),
        name="swa",
    )(qa_t, ka3, va_t, rows, sink_rows)


def _moba_kernel(q_ref, qn_ref, k_ref, v_ref, sel_ref, rows_ref, o_ref, bias_ref, m_ref, acc_ref, s_ref, cm_ref):
    step = pl.program_id(1)
    last = pl.num_programs(1) - 1
    MB = MOBA_BLOCK
    H = MOBA_HEADS
    G = MOBA_Q_PER_STEP

    @pl.when(jnp.logical_and(pl.program_id(0) == 0, step == 0))
    def _():
        for h in range(H):
            for d in range(BIAS_SLOTS):
                bias_ref[h, d] = _toeplitz(rows_ref[d, SWA_Q_HEADS + h:SWA_Q_HEADS + h + 1, :], MB).astype(BF16)

    m_ref[...] = jnp.full(m_ref.shape, NEG_INF, F32)
    acc_ref[...] = jnp.zeros(acc_ref.shape, F32)
    heads = [slice(h * HEAD_DIM, (h + 1) * HEAD_DIM) for h in range(H)]
    ones_blk = jnp.concatenate([jnp.ones((1, MB), BF16), jnp.zeros((ACC_ROWS - HEAD_DIM - 1, MB), BF16)],
                               axis=0)

    def sel_row(j, qs, h):
        return sel_ref[0, h, pl.ds(j, 1), qs * MB:(qs + 1) * MB]

    def scores(j, qs, h):
        c = qs * H + h
        rows = pl.ds(pl.multiple_of(j * MB, MB), MB)
        slot = jnp.minimum(G * step + qs - j, BIAS_SLOTS - 1)
        s = _dot(k_ref[0, rows, heads[h]], q_ref[0, qs, heads[h], :]).astype(BF16) + bias_ref[h, slot]
        s_ref[c] = s
        cm_ref[c] = jnp.max(s, axis=0, keepdims=True).astype(F32) + sel_row(j, qs, h)

    def own_scores(blk, q_blk, c, h):
        rows = pl.ds(pl.multiple_of(blk * MB, MB), MB)
        s = _dot(k_ref[0, rows, heads[h]], q_blk[heads[h], :]).astype(BF16) + bias_ref[h, 0]
        s_ref[c] = s
        cm_ref[c] = jnp.max(s, axis=0, keepdims=True).astype(F32)

    def accumulate(j, qs, h):
        c = qs * H + h
        m_old = m_ref[c]
        m_new = jnp.maximum(m_old, cm_ref[c])
        alpha = jnp.exp2(m_old - m_new)
        p = jnp.exp2(s_ref[c] - (m_new - sel_row(j, qs, h)).astype(BF16))
        v_aug = jnp.concatenate([v_ref[0, j, heads[h], :], ones_blk], axis=0)
        rs = slice(c * ACC_ROWS, (c + 1) * ACC_ROWS)
        acc_ref[rs, :] = alpha * acc_ref[rs, :] + _dot(v_aug, p)
        m_ref[c] = m_new

    first = G * step

    @pl.when(step == 0)
    def _():
        for qs in range(G):
            for h in range(H):
                own_scores(qs, q_ref[0, qs], qs * H + h, h)

    for n in range(G - 1):
        for qs in range(G - 1 - n, G):
            for h in range(H):
                accumulate(first + G - 1 - n, qs, h)
                scores(first + G - 2 - n, qs, h)

    def body(j, carry):
        for qs in range(G):
            for h in range(H):
                accumulate(j, qs, h)
                scores(j - 1, qs, h)
        return carry

    def body2(u, carry):
        return body(first - 2 * u - 1, body(first - 2 * u, carry))

    lax.fori_loop(0, (G // 2) * step, body2, 0)

    @pl.when(step < last)
    def _():
        for h in range(H):
            for qs in range(G):
                accumulate(0, qs, h)
                own_scores(first + G + qs, qn_ref[0, qs], qs * H + h, h)

    @pl.when(step == last)
    def _():
        for h in range(H):
            for qs in range(G):
                accumulate(0, qs, h)

    for qs in range(G):
        outs = []
        for h in range(H):
            base = (qs * H + h) * ACC_ROWS
            outs.append(acc_ref[base:base + HEAD_DIM, :] / acc_ref[base + HEAD_DIM:base + HEAD_DIM + 1, :])
        o_ref[0, qs * MB:(qs + 1) * MB, :] = jnp.concatenate(outs, axis=0).T.astype(o_ref.dtype)


def _moba(qb_t, kb3, vb_t, sel, rows):
    batch, nblk = qb_t.shape[0], qb_t.shape[1]
    seq = nblk * MOBA_BLOCK
    g = MOBA_Q_PER_STEP
    assert g % 2 == 0 and nblk % g == 0
    chains = g * MOBA_HEADS
    return pl.pallas_call(
        _moba_kernel,
        grid=(batch, nblk // g),
        in_specs=[
            pl.BlockSpec((1, g, MOBA_W, MOBA_BLOCK), lambda b, i: (b, i, 0, 0)),
            pl.BlockSpec((1, g, MOBA_W, MOBA_BLOCK), lambda b, i: (b, jnp.minimum(i + 1, nblk // g - 1), 0, 0)),
            pl.BlockSpec((1, seq, MOBA_W), lambda b, i: (b, 0, 0)),
            pl.BlockSpec((1, nblk, MOBA_W, MOBA_BLOCK), lambda b, i: (b, 0, 0, 0)),
            pl.BlockSpec((1, MOBA_HEADS, nblk, g * MOBA_BLOCK), lambda b, i: (b, 0, 0, i)),
            _resident(rows.shape, lambda b, i: (0, 0, 0)),
        ],
        out_specs=pl.BlockSpec((1, g * MOBA_BLOCK, MOBA_W), lambda b, i: (b, i, 0)),
        out_shape=jax.ShapeDtypeStruct((batch, seq, MOBA_W), BF16),
        scratch_shapes=[pltpu.VMEM((MOBA_HEADS, BIAS_SLOTS, MOBA_BLOCK, MOBA_BLOCK), BF16),
                        pltpu.VMEM((chains, 1, MOBA_BLOCK), F32),
                        pltpu.VMEM((chains * ACC_ROWS, MOBA_BLOCK), F32),
                        pltpu.VMEM((chains, MOBA_BLOCK, MOBA_BLOCK), BF16),
                        pltpu.VMEM((chains, 1, MOBA_BLOCK), F32)],
        compiler_params=pltpu.CompilerParams(dimension_semantics=("arbitrary", "arbitrary"),
                                             vmem_limit_bytes=VMEM_LIMIT_BYTES),
        name="moba",
    )(qb_t, qb_t, kb3, vb_t, sel, rows)


def _post_kernel(x_ref, ya_ref, yb_ref, ga_ref, gb_ref, gate1_ref, shift2_ref, scale2_ref,
                 gate2_ref, g2_ref, gf_ref, wa_ref, wb_ref, wo_ref, w1_ref, w2_ref, o_ref):
    a = _dot(ya_ref[...], wa_ref[...])
    b = _dot(yb_ref[...], wb_ref[...])
    merged = (jax.nn.sigmoid(ga_ref[...].astype(F32)) * a
              + jax.nn.sigmoid(gb_ref[...].astype(F32)) * b)
    x1 = x_ref[...] + gate1_ref[0, 0] * _dot(merged.astype(BF16), wo_ref[...])
    h2 = _rms_modulate(x1, g2_ref[...], shift2_ref[0, 0], scale2_ref[0, 0]).astype(BF16)
    y = jnp.zeros_like(x1)
    for c in range(D_FF // FF_CHUNK):
        cs = slice(c * FF_CHUNK, (c + 1) * FF_CHUNK)
        u = jnp.square(jnp.maximum(_dot(h2, w1_ref[:, cs]), 0.0)).astype(BF16)
        y = y + _dot(u, w2_ref[cs, :])
    x2 = x1 + gate2_ref[0, 0] * y
    ms = jnp.mean(x2 * x2, axis=-1, keepdims=True)
    o_ref[...] = (x2 * lax.rsqrt(ms + RMS_EPS)) * gf_ref[...]


def _post(x2, ya, yb, ga, gb, mod4, g2, gf, wa, wb, wo, w1, w2, seq):
    tokens = x2.shape[0]
    tiles_per_seq = seq // TOKEN_TILE
    tok = lambda w: pl.BlockSpec((TOKEN_TILE, w), lambda i: (i, 0))
    modrow = lambda k: pl.BlockSpec((1, 1, 1, D_MODEL), lambda i: (i // tiles_per_seq, k, 0, 0))
    full = lambda a: _resident(a.shape, lambda i: (0, 0))
    return pl.pallas_call(
        _post_kernel,
        grid=(tokens // TOKEN_TILE,),
        in_specs=[tok(D_MODEL), tok(SWA_Q_W), tok(MOBA_W), tok(D_MODEL), tok(D_MODEL),
                  modrow(2), modrow(3), modrow(4), modrow(5),
                  full(g2), full(gf), full(wa), full(wb), full(wo), full(w1), full(w2)],
        out_specs=tok(D_MODEL),
        out_shape=jax.ShapeDtypeStruct((tokens, D_MODEL), F32),
        compiler_params=pltpu.CompilerParams(dimension_semantics=("arbitrary",),
                                             vmem_limit_bytes=VMEM_LIMIT_BYTES),
        name="post",
    )(x2, ya, yb, ga, gb, mod4, mod4, mod4, mod4, g2, gf, wa, wb, wo, w1, w2)


def kernel(x, c, ada_w, ada_b, norm1_g, norm2_g, w_in, attn_sinks, rel_bias, w_branch_a,
           w_branch_b, w_out, w_mlp_in, w_mlp_out, final_g):
    batch, seq, _ = x.shape
    depth = ada_w.shape[0]
    assert seq % TOKEN_TILE == 0 and TOKEN_TILE % MOBA_BLOCK == 0 and batch <= SUBLANES
    far = np.arange((BIAS_SLOTS - 1) * MOBA_BLOCK - (MOBA_BLOCK - 1), max(seq, BIAS_SLOTS * MOBA_BLOCK))
    assert np.all(_t5_bucket(far) == _t5_bucket(far[:1]))
    assert 2 * SWA_BLOCK <= MOBA_BLOCK and SWA_WINDOW == SWA_BLOCK
    assert depth == 1
    l = 0


    x2 = x.reshape(batch * seq, D_MODEL)
    mod4, rows, sink_rows = _ada(c.astype(F32), ada_w[l], ada_b[l][None, :], rel_bias, attn_sinks[l])
    later_weights = [w.astype(F32) for w in (w_branch_a[l], w_branch_b[l], w_out[l], w_mlp_in[l], w_mlp_out[l])]
    ka, kb, ga, gb, qa_t, va_t, qb_t, vb_t, sel, wa, wb, wo, w1, w2 = _proj(
        x2, mod4, norm1_g[l][None, :], w_in[l].astype(F32), later_weights, batch, seq)

    ya = _swa(qa_t, ka.reshape(batch, seq, SWA_KV_W), va_t, rows, sink_rows)

    yb = _moba(qb_t, kb.reshape(batch, seq, MOBA_W), vb_t, sel, rows)

    out = _post(x2, ya.reshape(batch * seq, SWA_Q_W), yb.reshape(batch * seq, MOBA_W), ga, gb,
                mod4, norm2_g[l][None, :], final_g[None, :], wa, wb, wo, w1, w2, seq)
    return out.reshape(batch, seq, D_MODEL)
```

```python
import math

import jax
import jax.numpy as jnp
import numpy as np
from jax import lax
from jax.experimental import pallas as pl
from jax.experimental.pallas import tpu as pltpu

D_MODEL = 1024
HEAD_DIM = 64
ATTN_SCALE = HEAD_DIM ** -0.5
LOG2E = math.log2(math.e)
SWA_Q_HEADS = 8
SWA_KV_HEADS = 2
SWA_GROUP = SWA_Q_HEADS // SWA_KV_HEADS
SWA_WINDOW = 128
SWA_BLOCK = 128
MOBA_HEADS = 8
MOBA_BLOCK = 256
MOBA_TOPK = 3
NUM_BUCKETS = 32
MAX_EXACT = NUM_BUCKETS // 2
MAX_DISTANCE = 2048
N_ATTN_HEADS = SWA_Q_HEADS + MOBA_HEADS
SWA_Q_W = SWA_Q_HEADS * HEAD_DIM
SWA_KV_W = SWA_KV_HEADS * HEAD_DIM
MOBA_W = MOBA_HEADS * HEAD_DIM
D_FF = 4 * D_MODEL
N_MOD = 6
RMS_EPS = 1e-6

VMEM_LIMIT_BYTES = 56 * 1024 * 1024
TOKEN_TILE = 512
FF_CHUNK = 1024
SUBLANES = 8
BIAS_SLOTS = 8
SWA_UNROLL = 3
MOBA_Q_PER_STEP = 4
MOBA_UNROLL = 4
ACC_ROWS = HEAD_DIM + 16
NEG_INF = float("-inf")

F32 = jnp.float32
BF16 = jnp.bfloat16


def _resident(block_shape, index_map):
    return pl.BlockSpec(block_shape, index_map, pipeline_mode=pl.Buffered(1))


def _dot(a, b):
    return jnp.dot(a, b, preferred_element_type=F32)


def _dot_nt(a, b):
    return lax.dot_general(a, b, (((1,), (1,)), ((), ())), preferred_element_type=F32)


def _rms_modulate(xv, g, shift, scale):
    ms = jnp.mean(xv * xv, axis=-1, keepdims=True)
    y = xv * lax.rsqrt(ms + RMS_EPS)
    return (y * g) * (1.0 + scale) + shift


def _toeplitz(u, rows):
    wide = jnp.broadcast_to(u, (rows, 2 * rows))
    return pltpu.roll(wide, 0, 1, stride=1, stride_axis=0)[:, rows:]


def _t5_bucket(dist):
    n = np.maximum(dist, 0)
    nf = np.maximum(n, 1).astype(np.float32)
    large = MAX_EXACT + (np.log(nf / np.float32(MAX_EXACT)) / np.float32(math.log(MAX_DISTANCE / MAX_EXACT))
                         * np.float32(NUM_BUCKETS - MAX_EXACT)).astype(np.int32)
    large = np.minimum(large, NUM_BUCKETS - 1)
    return np.where(n < MAX_EXACT, n, large).astype(np.int32)


def _bias_rows(bucket_ref, rbt_ref, o_ref):
    bucket = bucket_ref[...]
    n = bucket.shape[1]
    onehot = jnp.where(lax.broadcasted_iota(jnp.int32, (NUM_BUCKETS, n), 0) == bucket, 1.0, 0.0)
    rows = jnp.dot(rbt_ref[...], onehot, preferred_element_type=F32,
                   precision=lax.Precision.HIGHEST)
    rows = jnp.where(bucket >= 0, rows * LOG2E, NEG_INF)
    for d in range(BIAS_SLOTS):
        o_ref[d] = rows[:, d * MOBA_BLOCK:(d + 2) * MOBA_BLOCK]


def _ada_kernel(c_ref, w_ref, b_ref, bucket_ref, rbt_ref, sinks_ref, o_ref, rows_ref, sink_rows_ref, cs_ref):
    batch = c_ref.shape[0]
    cs_ref[...] = jnp.zeros(cs_ref.shape, F32)
    cs_ref[0:batch, :] = jax.nn.silu(c_ref[...])
    res = _dot(cs_ref[...].astype(BF16), w_ref[...].astype(BF16)) + b_ref[...]
    for b in range(batch):
        o_ref[b, 0] = res[b:b + 1]

    @pl.when(pl.program_id(0) == 0)
    def _():
        _bias_rows(bucket_ref, rbt_ref, rows_ref)
        sinks = sinks_ref[...] * LOG2E
        for h in range(SWA_Q_HEADS):
            g, gi = divmod(h, SWA_GROUP)
            sink_rows_ref[g, :, gi * SWA_BLOCK:(gi + 1) * SWA_BLOCK] = jnp.broadcast_to(
                sinks[:, h:h + 1], (1, SWA_BLOCK))


def _ada(c, w, b, rel_bias, sinks):
    batch = c.shape[0]
    assert w.shape[1] == N_MOD * D_MODEL
    n = (BIAS_SLOTS + 1) * MOBA_BLOCK
    dist = np.arange(n, dtype=np.int32) - MOBA_BLOCK
    bucket = jnp.asarray(np.where(dist >= 0, _t5_bucket(dist), -1)[None, :], dtype=jnp.int32)
    rows_shape = (BIAS_SLOTS, N_ATTN_HEADS, 2 * MOBA_BLOCK)
    sink_shape = (SWA_KV_HEADS, 1, SWA_GROUP * SWA_BLOCK)
    return pl.pallas_call(
        _ada_kernel,
        grid=(N_MOD,),
        in_specs=[pl.BlockSpec((batch, D_MODEL), lambda j: (0, 0)),
                  pl.BlockSpec((D_MODEL, D_MODEL), lambda j: (0, j)),
                  pl.BlockSpec((1, D_MODEL), lambda j: (0, j)),
                  pl.BlockSpec((1, n), lambda j: (0, 0)),
                  pl.BlockSpec((N_ATTN_HEADS, NUM_BUCKETS), lambda j: (0, 0)),
                  pl.BlockSpec((1, SWA_Q_HEADS), lambda j: (0, 0))],
        out_specs=(pl.BlockSpec((batch, 1, 1, D_MODEL), lambda j: (0, j, 0, 0)),
                   pl.BlockSpec(rows_shape, lambda j: (0, 0, 0)),
                   pl.BlockSpec(sink_shape, lambda j: (0, 0, 0))),
        out_shape=(jax.ShapeDtypeStruct((batch, N_MOD, 1, D_MODEL), F32),
                   jax.ShapeDtypeStruct(rows_shape, F32),
                   jax.ShapeDtypeStruct(sink_shape, F32)),
        scratch_shapes=[pltpu.VMEM((SUBLANES, D_MODEL), F32)],
        compiler_params=pltpu.CompilerParams(dimension_semantics=("arbitrary",),
                                             vmem_limit_bytes=VMEM_LIMIT_BYTES),
        name="ada",
    )(c, w, b, bucket, rel_bias.astype(F32).T, sinks.astype(F32)[None, :])


IN_WIDTHS = (SWA_Q_W, SWA_KV_W, SWA_KV_W, MOBA_W, MOBA_W, MOBA_W, D_MODEL, D_MODEL)
IN_OFFS = tuple(sum(IN_WIDTHS[:k]) for k in range(len(IN_WIDTHS) + 1))
FEAT_COLS = (0, 2, 3, 5)
FEAT_W = sum(IN_WIDTHS[k] for k in FEAT_COLS)
TOK_COLS = (1, 4, 6, 7)
TOK_W = sum(IN_WIDTHS[k] for k in TOK_COLS)
Q_COLS = (0, 3)
BLOCKS_PER_TILE = TOKEN_TILE // MOBA_BLOCK


def _proj_kernel(x_ref, shift_ref, scale_ref, g_ref, w_ref, *refs):
    n_cast = (len(refs) - 12) // 2
    cast_in, refs = refs[:n_cast], refs[n_cast:]
    ka_ref, kb_ref, ga_ref, gb_ref, qa_ref, va_ref, qb_ref, vb_ref, sel_ref = refs[:9]
    cast_out, (wk_ref, wt_ref, kmean_ref) = refs[9:9 + n_cast], refs[9 + n_cast:]

    for src, dst in zip(cast_in, cast_out):
        dst[...] = src[...].astype(dst.dtype)

    tile = pl.program_id(0) % (kmean_ref.shape[0] // BLOCKS_PER_TILE)

    @pl.when(tile == 0)
    def _():
        kmean_ref[...] = jnp.zeros(kmean_ref.shape, F32)

    @pl.when(pl.program_id(0) == 0)
    def _():
        o = 0
        for k in TOK_COLS:
            wk_ref[:, o:o + IN_WIDTHS[k]] = w_ref[:, IN_OFFS[k]:IN_OFFS[k + 1]].astype(BF16)
            o += IN_WIDTHS[k]
        o = 0
        for k in FEAT_COLS:
            wcol = w_ref[:, IN_OFFS[k]:IN_OFFS[k + 1]]
            if k in Q_COLS:
                wcol = wcol * (ATTN_SCALE * LOG2E)
            wt_ref[o:o + IN_WIDTHS[k], :] = wcol.T.astype(BF16)
            o += IN_WIDTHS[k]

    h = _rms_modulate(x_ref[...], g_ref[...], shift_ref[0, 0], scale_ref[0, 0]).astype(BF16)
    tok_refs = dict(zip(TOK_COLS, (ka_ref, kb_ref, ga_ref, gb_ref)))
    feat_refs = dict(zip(FEAT_COLS, (qa_ref, va_ref, qb_ref, vb_ref)))

    def token_major(k):
        o = sum(IN_WIDTHS[c] for c in TOK_COLS[:TOK_COLS.index(k)])
        out = _dot(h, wk_ref[:, o:o + IN_WIDTHS[k]])
        tok_refs[k][...] = out.astype(BF16)
        return out

    def feature_major(k):
        o = sum(IN_WIDTHS[c] for c in FEAT_COLS[:FEAT_COLS.index(k)])
        out = _dot_nt(wt_ref[o:o + IN_WIDTHS[k], :], h).astype(BF16)
        blk = feat_refs[k].shape[-1]
        for t in range(TOKEN_TILE // blk):
            feat_refs[k][0, t] = out[:, t * blk:(t + 1) * blk]
        return out

    _update_kmean(token_major(4), kmean_ref, tile)
    qb_t = feature_major(3)
    token_major(1)
    token_major(6)
    _moba_select(qb_t, sel_ref, kmean_ref, tile)
    token_major(7)
    for k in (0, 2, 5):
        feature_major(k)


def _update_kmean(kb, kmean_ref, tile):
    for t in range(BLOCKS_PER_TILE):
        kmean_ref[pl.ds(tile * BLOCKS_PER_TILE + t, 1), :] = jnp.mean(
            kb[t * MOBA_BLOCK:(t + 1) * MOBA_BLOCK, :], axis=0, keepdims=True)


def _moba_select(qb_t, sel_ref, kmean_ref, tile):
    nblk = kmean_ref.shape[0]
    blocks_per_tile = BLOCKS_PER_TILE
    blk = lax.broadcasted_iota(jnp.int32, (nblk, TOKEN_TILE), 0)
    qblk = tile * blocks_per_tile + lax.broadcasted_iota(jnp.int32, (nblk, TOKEN_TILE), 1) // MOBA_BLOCK
    past = blk < qblk
    own = jnp.where(blk == qblk, 0.0, NEG_INF)
    blk_f = blk.astype(F32)
    for h in range(MOBA_HEADS):
        hs = slice(h * HEAD_DIM, (h + 1) * HEAD_DIM)
        kmean = kmean_ref[:, hs]
        k1 = kmean.astype(BF16)
        r1 = kmean - k1.astype(F32)
        k2 = r1.astype(BF16)
        k3 = (r1 - k2.astype(F32)).astype(BF16)
        parts = _dot(jnp.concatenate([k1, k2, k3], axis=0), qb_t[hs, :])
        gate = parts[0:nblk] + parts[nblk:2 * nblk] + parts[2 * nblk:3 * nblk]
        gate = jnp.where(past, gate, NEG_INF)
        mask = own
        for _ in range(MOBA_TOPK):
            top = jnp.max(gate, axis=0, keepdims=True)
            first = jnp.min(jnp.where(gate == top, blk_f, float(nblk)), axis=0, keepdims=True)
            pick = blk_f == first
            mask = jnp.where(jnp.logical_and(pick, past), 0.0, mask)
            gate = jnp.where(pick, NEG_INF, gate)
        sel_ref[0, h] = mask


def _proj(x2, mod4, g1, w_in, cast_weights, batch, seq):
    tokens = x2.shape[0]
    steps = tokens // TOKEN_TILE
    assert all(w.shape[0] % (16 * steps) == 0 for w in cast_weights)
    cast_specs = [pl.BlockSpec((w.shape[0] // steps, w.shape[1]), lambda i: (i, 0)) for w in cast_weights]
    tiles_per_seq = seq // TOKEN_TILE
    na = TOKEN_TILE // SWA_BLOCK
    nb = TOKEN_TILE // MOBA_BLOCK

    def feat_spec(width, blk, per_tile):
        return pl.BlockSpec((1, per_tile, width, blk),
                            lambda i: (i // tiles_per_seq, i % tiles_per_seq, 0, 0))

    out_shape = (
        jax.ShapeDtypeStruct((tokens, SWA_KV_W), BF16),
        jax.ShapeDtypeStruct((tokens, MOBA_W), BF16),
        jax.ShapeDtypeStruct((tokens, D_MODEL), BF16),
        jax.ShapeDtypeStruct((tokens, D_MODEL), BF16),
        jax.ShapeDtypeStruct((batch, seq // SWA_BLOCK, SWA_Q_W, SWA_BLOCK), BF16),
        jax.ShapeDtypeStruct((batch, seq // SWA_BLOCK, SWA_KV_W, SWA_BLOCK), BF16),
        jax.ShapeDtypeStruct((batch, seq // MOBA_BLOCK, MOBA_W, MOBA_BLOCK), BF16),
        jax.ShapeDtypeStruct((batch, seq // MOBA_BLOCK, MOBA_W, MOBA_BLOCK), BF16),
        jax.ShapeDtypeStruct((batch, MOBA_HEADS, seq // MOBA_BLOCK, seq), F32),
    )
    tok_spec = lambda w: pl.BlockSpec((TOKEN_TILE, w), lambda i: (i, 0))
    return pl.pallas_call(
        _proj_kernel,
        grid=(steps,),
        in_specs=[
            pl.BlockSpec((TOKEN_TILE, D_MODEL), lambda i: (i, 0)),
            pl.BlockSpec((1, 1, 1, D_MODEL), lambda i: (i // tiles_per_seq, 0, 0, 0)),
            pl.BlockSpec((1, 1, 1, D_MODEL), lambda i: (i // tiles_per_seq, 1, 0, 0)),
            _resident((1, D_MODEL), lambda i: (0, 0)),
            _resident(w_in.shape, lambda i: (0, 0)),
        ] + cast_specs,
        out_specs=(tok_spec(SWA_KV_W), tok_spec(MOBA_W), tok_spec(D_MODEL), tok_spec(D_MODEL),
                   feat_spec(SWA_Q_W, SWA_BLOCK, na), feat_spec(SWA_KV_W, SWA_BLOCK, na),
                   feat_spec(MOBA_W, MOBA_BLOCK, nb), feat_spec(MOBA_W, MOBA_BLOCK, nb),
                   pl.BlockSpec((1, MOBA_HEADS, seq // MOBA_BLOCK, TOKEN_TILE),
                                lambda i: (i // tiles_per_seq, 0, 0, i % tiles_per_seq)))
        + tuple(cast_specs),
        out_shape=out_shape + tuple(jax.ShapeDtypeStruct(w.shape, BF16) for w in cast_weights),
        scratch_shapes=[pltpu.VMEM((D_MODEL, TOK_W), BF16), pltpu.VMEM((FEAT_W, D_MODEL), BF16),
                        pltpu.VMEM((seq // MOBA_BLOCK, MOBA_W), F32)],
        compiler_params=pltpu.CompilerParams(dimension_semantics=("arbitrary",),
                                             vmem_limit_bytes=VMEM_LIMIT_BYTES),
        name="proj",
    )(x2, mod4, mod4, g1, w_in, *cast_weights)


def _swa_kernel(q_ref, k_ref, v_ref, rows_ref, sink_ref, o_ref, bias_ref, ot_ref, s_ref, cm_ref):
    L = SWA_BLOCK
    nblk = q_ref.shape[1]

    @pl.when(pl.program_id(0) == 0)
    def _():
        k_idx = lax.broadcasted_iota(jnp.int32, (L, L), 0)
        q_idx = lax.broadcasted_iota(jnp.int32, (L, L), 1)
        for h in range(SWA_Q_HEADS):
            g, gi = divmod(h, SWA_GROUP)
            t0 = rows_ref[0, h:h + 1, MOBA_BLOCK:MOBA_BLOCK + L]
            r = _toeplitz(jnp.concatenate([t0, t0], axis=1), L)
            cols = slice(gi * L, (gi + 1) * L)
            bias_ref[g, 0:L, cols] = jnp.where(q_idx < k_idx, r, NEG_INF).astype(BF16)
            bias_ref[g, L:2 * L, cols] = jnp.where(q_idx >= k_idx, r, NEG_INF).astype(BF16)

    ones_blk = jnp.concatenate([jnp.ones((1, 2 * L), BF16), jnp.zeros((ACC_ROWS - HEAD_DIM - 1, 2 * L), BF16)],
                               axis=0)
    PW = 2 * L
    chains = [(g, hp) for g in range(SWA_KV_HEADS) for hp in range(SWA_GROUP // 2)]
    kv_cols = [slice(g * HEAD_DIM, (g + 1) * HEAD_DIM) for g in range(SWA_KV_HEADS)]

    def q_pair(c, g, hp):
        h0 = g * SWA_GROUP + 2 * hp
        return jnp.concatenate([q_ref[0, c, (h0 + e) * HEAD_DIM:(h0 + e + 1) * HEAD_DIM, :] for e in range(2)],
                               axis=1)

    def scores_first(idx):
        g, hp = chains[idx]
        lanes = slice(hp * PW, (hp + 1) * PW)
        s_ref[idx, 0:L, :] = jnp.full((L, PW), NEG_INF, BF16)
        s = _dot(k_ref[0, 0:L, kv_cols[g]], q_pair(0, g, hp)).astype(BF16) + bias_ref[g, L:2 * L, lanes]
        s_ref[idx, L:2 * L, :] = s
        cm_ref[idx] = jnp.max(s, axis=0, keepdims=True).astype(F32)

    def scores(c, idx):
        g, hp = chains[idx]
        kw = k_ref[0, pl.ds(pl.multiple_of((c - 1) * L, L), 2 * L), kv_cols[g]]
        s = _dot(kw, q_pair(c, g, hp)).astype(BF16) + bias_ref[g, :, hp * PW:(hp + 1) * PW]
        s_ref[idx] = s
        cm_ref[idx] = jnp.max(s, axis=0, keepdims=True).astype(F32)

    def attend(c, c_prev, idx, slot):
        g, hp = chains[idx]
        hs = kv_cols[g]
        s = s_ref[idx]
        sink = sink_ref[g, :, hp * PW:(hp + 1) * PW]
        m_b = jnp.maximum(cm_ref[idx], sink).astype(BF16)
        m = m_b.astype(F32)
        p = jnp.exp2(s - m_b)
        v_win = jnp.concatenate([v_ref[0, c_prev, hs, :], v_ref[0, c, hs, :]], axis=1)
        pv = _dot(jnp.concatenate([v_win, ones_blk], axis=0), p)
        o = pv[0:HEAD_DIM] / (pv[HEAD_DIM:HEAD_DIM + 1] + jnp.exp2(sink - m))
        for e in range(2):
            r = (g * SWA_GROUP + 2 * hp + e) * HEAD_DIM
            ot_ref[slot, r:r + HEAD_DIM, :] = o[:, e * L:(e + 1) * L]

    def emit(c, slot):
        o_ref[0, pl.ds(pl.multiple_of(c * L, L), L), :] = ot_ref[slot].T.astype(o_ref.dtype)

    for idx in range(len(chains)):
        scores_first(idx)
    U = SWA_UNROLL
    for idx in range(len(chains)):
        attend(0, 0, idx, U - 1)
        scores(1, idx)

    def body(c, e):
        emit(c - 1, (e - 1) % U)
        for idx in range(len(chains)):
            attend(c, c - 1, idx, e)
            scores(c + 1, idx)

    def body_unrolled(u, carry):
        for e in range(U):
            body(U * u + 1 + e, e)
        return carry

    lax.fori_loop(0, (nblk - 2) // U, body_unrolled, 0)
    emit(nblk - 2, U - 1)
    for idx in range(len(chains)):
        attend(nblk - 1, nblk - 2, idx, 0)
    emit(nblk - 1, 0)


def _swa(qa_t, ka3, va_t, rows, sink_rows):
    batch, nblk = qa_t.shape[0], qa_t.shape[1]
    seq = nblk * SWA_BLOCK
    assert nblk >= 2 and (nblk - 2) % SWA_UNROLL == 0 and SWA_UNROLL >= 2
    return pl.pallas_call(
        _swa_kernel,
        grid=(batch,),
        in_specs=[
            pl.BlockSpec((1, nblk, SWA_Q_W, SWA_BLOCK), lambda b: (b, 0, 0, 0)),
            pl.BlockSpec((1, seq, SWA_KV_W), lambda b: (b, 0, 0)),
            pl.BlockSpec((1, nblk, SWA_KV_W, SWA_BLOCK), lambda b: (b, 0, 0, 0)),
            _resident((1, N_ATTN_HEADS, 2 * MOBA_BLOCK), lambda b: (0, 0, 0)),
            _resident(sink_rows.shape, lambda b: (0, 0, 0)),
        ],
        out_specs=pl.BlockSpec((1, seq, SWA_Q_W), lambda b: (b, 0, 0)),
        out_shape=jax.ShapeDtypeStruct((batch, seq, SWA_Q_W), BF16),
        scratch_shapes=[pltpu.VMEM((SWA_KV_HEADS, 2 * SWA_BLOCK, SWA_GROUP * SWA_BLOCK), BF16),
                        pltpu.VMEM((SWA_UNROLL, SWA_Q_W, SWA_BLOCK), F32),
                        pltpu.VMEM((SWA_Q_HEADS // 2, 2 * SWA_BLOCK, 2 * SWA_BLOCK), BF16),
                        pltpu.VMEM((SWA_Q_HEADS // 2, 1, 2 * SWA_BLOCK), F32)],
        compiler_params=pltpu.CompilerParams(dimension_semantics=("arbitrary",),
                                             vmem_limit_bytes=VMEM_LIMIT_BYTES),
        name="swa",
    )(qa_t, ka3, va_t, rows, sink_rows)


def _moba_kernel(q_ref, qn_ref, k_ref, v_ref, sel_ref, rows_ref, o_ref, bias_ref, m_ref, acc_ref, s_ref, cm_ref):
    step = pl.program_id(1)
    last = pl.num_programs(1) - 1
    MB = MOBA_BLOCK
    H = MOBA_HEADS
    G = MOBA_Q_PER_STEP

    @pl.when(jnp.logical_and(pl.program_id(0) == 0, step == 0))
    def _():
        for h in range(H):
            for d in range(BIAS_SLOTS):
                bias_ref[h, d] = _toeplitz(rows_ref[d, SWA_Q_HEADS + h:SWA_Q_HEADS + h + 1, :], MB).astype(BF16)

    m_ref[...] = jnp.full(m_ref.shape, NEG_INF, F32)
    acc_ref[...] = jnp.zeros(acc_ref.shape, F32)
    heads = [slice(h * HEAD_DIM, (h + 1) * HEAD_DIM) for h in range(H)]
    ones_blk = jnp.concatenate([jnp.ones((1, MB), BF16), jnp.zeros((ACC_ROWS - HEAD_DIM - 1, MB), BF16)],
                               axis=0)

    def sel_row(j, qs, h):
        return sel_ref[0, h, pl.ds(j, 1), qs * MB:(qs + 1) * MB]

    def scores(j, qs, h):
        c = qs * H + h
        rows = pl.ds(pl.multiple_of(j * MB, MB), MB)
        slot = jnp.minimum(G * step + qs - j, BIAS_SLOTS - 1)
        s = _dot(k_ref[0, rows, heads[h]], q_ref[0, qs, heads[h], :]).astype(BF16) + bias_ref[h, slot]
        s_ref[c] = s
        cm_ref[c] = jnp.max(s, axis=0, keepdims=True).astype(F32) + sel_row(j, qs, h)

    def own_scores(blk, q_blk, c, h):
        rows = pl.ds(pl.multiple_of(blk * MB, MB), MB)
        s = _dot(k_ref[0, rows, heads[h]], q_blk[heads[h], :]).astype(BF16) + bias_ref[h, 0]
        s_ref[c] = s
        cm_ref[c] = jnp.max(s, axis=0, keepdims=True).astype(F32)

    def accumulate(j, qs, h):
        c = qs * H + h
        m_old = m_ref[c]
        m_new = jnp.maximum(m_old, cm_ref[c])
        alpha = jnp.exp2(m_old - m_new)
        p = jnp.exp2(s_ref[c] - (m_new - sel_row(j, qs, h)).astype(BF16))
        v_aug = jnp.concatenate([v_ref[0, j, heads[h], :], ones_blk], axis=0)
        rs = slice(c * ACC_ROWS, (c + 1) * ACC_ROWS)
        acc_ref[rs, :] = alpha * acc_ref[rs, :] + _dot(v_aug, p)
        m_ref[c] = m_new

    first = G * step

    @pl.when(step == 0)
    def _():
        for qs in range(G):
            for h in range(H):
                own_scores(qs, q_ref[0, qs], qs * H + h, h)

    for n in range(G - 1):
        for qs in range(G - 1 - n, G):
            for h in range(H):
                accumulate(first + G - 1 - n, qs, h)
                scores(first + G - 2 - n, qs, h)

    def body(j, carry):
        for qs in range(G):
            for h in range(H):
                accumulate(j, qs, h)
                scores(j - 1, qs, h)
        return carry

    def body_unrolled(u, carry):
        for e in range(MOBA_UNROLL):
            carry = body(first - MOBA_UNROLL * u - e, carry)
        return carry

    lax.fori_loop(0, (G // MOBA_UNROLL) * step, body_unrolled, 0)

    @pl.when(step < last)
    def _():
        for h in range(H):
            for qs in range(G):
                accumulate(0, qs, h)
                own_scores(first + G + qs, qn_ref[0, qs], qs * H + h, h)

    @pl.when(step == last)
    def _():
        for h in range(H):
            for qs in range(G):
                accumulate(0, qs, h)

    for qs in range(G):
        outs = []
        for h in range(H):
            base = (qs * H + h) * ACC_ROWS
            outs.append(acc_ref[base:base + HEAD_DIM, :] / acc_ref[base + HEAD_DIM:base + HEAD_DIM + 1, :])
        o_ref[0, qs * MB:(qs + 1) * MB, :] = jnp.concatenate(outs, axis=0).T.astype(o_ref.dtype)


def _moba(qb_t, kb3, vb_t, sel, rows):
    batch, nblk = qb_t.shape[0], qb_t.shape[1]
    seq = nblk * MOBA_BLOCK
    g = MOBA_Q_PER_STEP
    assert g % MOBA_UNROLL == 0 and nblk % g == 0
    chains = g * MOBA_HEADS
    return pl.pallas_call(
        _moba_kernel,
        grid=(batch, nblk // g),
        in_specs=[
            pl.BlockSpec((1, g, MOBA_W, MOBA_BLOCK), lambda b, i: (b, i, 0, 0)),
            pl.BlockSpec((1, g, MOBA_W, MOBA_BLOCK), lambda b, i: (b, jnp.minimum(i + 1, nblk // g - 1), 0, 0)),
            pl.BlockSpec((1, seq, MOBA_W), lambda b, i: (b, 0, 0)),
            pl.BlockSpec((1, nblk, MOBA_W, MOBA_BLOCK), lambda b, i: (b, 0, 0, 0)),
            pl.BlockSpec((1, MOBA_HEADS, nblk, g * MOBA_BLOCK), lambda b, i: (b, 0, 0, i)),
            _resident(rows.shape, lambda b, i: (0, 0, 0)),
        ],
        out_specs=pl.BlockSpec((1, g * MOBA_BLOCK, MOBA_W), lambda b, i: (b, i, 0)),
        out_shape=jax.ShapeDtypeStruct((batch, seq, MOBA_W), BF16),
        scratch_shapes=[pltpu.VMEM((MOBA_HEADS, BIAS_SLOTS, MOBA_BLOCK, MOBA_BLOCK), BF16),
                        pltpu.VMEM((chains, 1, MOBA_BLOCK), F32),
                        pltpu.VMEM((chains * ACC_ROWS, MOBA_BLOCK), F32),
                        pltpu.VMEM((chains, MOBA_BLOCK, MOBA_BLOCK), BF16),
                        pltpu.VMEM((chains, 1, MOBA_BLOCK), F32)],
        compiler_params=pltpu.CompilerParams(dimension_semantics=("arbitrary", "arbitrary"),
                                             vmem_limit_bytes=VMEM_LIMIT_BYTES),
        name="moba",
    )(qb_t, qb_t, kb3, vb_t, sel, rows)


def _post_kernel(x_ref, ya_ref, yb_ref, ga_ref, gb_ref, gate1_ref, shift2_ref, scale2_ref,
                 gate2_ref, g2_ref, gf_ref, wa_ref, wb_ref, wo_ref, w1_ref, w2_ref, o_ref):
    a = _dot(ya_ref[...], wa_ref[...])
    b = _dot(yb_ref[...], wb_ref[...])
    merged = (jax.nn.sigmoid(ga_ref[...].astype(F32)) * a
              + jax.nn.sigmoid(gb_ref[...].astype(F32)) * b)
    x1 = x_ref[...] + gate1_ref[0, 0] * _dot(merged.astype(BF16), wo_ref[...])
    h2 = _rms_modulate(x1, g2_ref[...], shift2_ref[0, 0], scale2_ref[0, 0]).astype(BF16)
    y = jnp.zeros_like(x1)
    for c in range(D_FF // FF_CHUNK):
        cs = slice(c * FF_CHUNK, (c + 1) * FF_CHUNK)
        u = jnp.square(jnp.maximum(_dot(h2, w1_ref[:, cs]), 0.0)).astype(BF16)
        y = y + _dot(u, w2_ref[cs, :])
    x2 = x1 + gate2_ref[0, 0] * y
    ms = jnp.mean(x2 * x2, axis=-1, keepdims=True)
    o_ref[...] = (x2 * lax.rsqrt(ms + RMS_EPS)) * gf_ref[...]


def _post(x2, ya, yb, ga, gb, mod4, g2, gf, wa, wb, wo, w1, w2, seq):
    tokens = x2.shape[0]
    tiles_per_seq = seq // TOKEN_TILE
    tok = lambda w: pl.BlockSpec((TOKEN_TILE, w), lambda i: (i, 0))
    modrow = lambda k: pl.BlockSpec((1, 1, 1, D_MODEL), lambda i: (i // tiles_per_seq, k, 0, 0))
    full = lambda a: _resident(a.shape, lambda i: (0, 0))
    return pl.pallas_call(
        _post_kernel,
        grid=(tokens // TOKEN_TILE,),
        in_specs=[tok(D_MODEL), tok(SWA_Q_W), tok(MOBA_W), tok(D_MODEL), tok(D_MODEL),
                  modrow(2), modrow(3), modrow(4), modrow(5),
                  full(g2), full(gf), full(wa), full(wb), full(wo), full(w1), full(w2)],
        out_specs=tok(D_MODEL),
        out_shape=jax.ShapeDtypeStruct((tokens, D_MODEL), F32),
        compiler_params=pltpu.CompilerParams(dimension_semantics=("arbitrary",),
                                             vmem_limit_bytes=VMEM_LIMIT_BYTES),
        name="post",
    )(x2, ya, yb, ga, gb, mod4, mod4, mod4, mod4, g2, gf, wa, wb, wo, w1, w2)


def kernel(x, c, ada_w, ada_b, norm1_g, norm2_g, w_in, attn_sinks, rel_bias, w_branch_a,
           w_branch_b, w_out, w_mlp_in, w_mlp_out, final_g):
    batch, seq, _ = x.shape
    depth = ada_w.shape[0]
    assert seq % TOKEN_TILE == 0 and TOKEN_TILE % MOBA_BLOCK == 0 and batch <= SUBLANES
    far = np.arange((BIAS_SLOTS - 1) * MOBA_BLOCK - (MOBA_BLOCK - 1), max(seq, BIAS_SLOTS * MOBA_BLOCK))
    assert np.all(_t5_bucket(far) == _t5_bucket(far[:1]))
    assert 2 * SWA_BLOCK <= MOBA_BLOCK and SWA_WINDOW == SWA_BLOCK
    assert depth == 1
    l = 0


    x2 = x.reshape(batch * seq, D_MODEL)
    mod4, rows, sink_rows = _ada(c.astype(F32), ada_w[l], ada_b[l][None, :], rel_bias, attn_sinks[l])
    later_weights = [w.astype(F32) for w in (w_branch_a[l], w_branch_b[l], w_out[l], w_mlp_in[l], w_mlp_out[l])]
    ka, kb, ga, gb, qa_t, va_t, qb_t, vb_t, sel, wa, wb, wo, w1, w2 = _proj(
        x2, mod4, norm1_g[l][None, :], w_in[l].astype(F32), later_weights, batch, seq)

    ya = _swa(qa_t, ka.reshape(batch, seq, SWA_KV_W), va_t, rows, sink_rows)

    yb = _moba(qb_t, kb.reshape(batch, seq, MOBA_W), vb_t, sel, rows)

    out = _post(x2, ya.reshape(batch * seq, SWA_Q_W), yb.reshape(batch * seq, MOBA_W), ga, gb,
                mod4, norm2_g[l][None, :], final_g[None, :], wa, wb, wo, w1, w2, seq)
    return out.reshape(batch, seq, D_MODEL)
```

```python
import math

import jax
import jax.numpy as jnp
import numpy as np
from jax import lax
from jax.experimental import pallas as pl
from jax.experimental.pallas import tpu as pltpu

D_MODEL = 1024
HEAD_DIM = 64
ATTN_SCALE = HEAD_DIM ** -0.5
LOG2E = math.log2(math.e)
SWA_Q_HEADS = 8
SWA_KV_HEADS = 2
SWA_GROUP = SWA_Q_HEADS // SWA_KV_HEADS
SWA_WINDOW = 128
SWA_BLOCK = 128
MOBA_HEADS = 8
MOBA_BLOCK = 256
MOBA_TOPK = 3
NUM_BUCKETS = 32
MAX_EXACT = NUM_BUCKETS // 2
MAX_DISTANCE = 2048
N_ATTN_HEADS = SWA_Q_HEADS + MOBA_HEADS
SWA_Q_W = SWA_Q_HEADS * HEAD_DIM
SWA_KV_W = SWA_KV_HEADS * HEAD_DIM
MOBA_W = MOBA_HEADS * HEAD_DIM
D_FF = 4 * D_MODEL
N_MOD = 6
RMS_EPS = 1e-6

VMEM_LIMIT_BYTES = 56 * 1024 * 1024
TOKEN_TILE = 512
FF_CHUNK = 1024
SUBLANES = 8
BIAS_SLOTS = 8
SWA_UNROLL = 10
MOBA_Q_PER_STEP = 4
MOBA_UNROLL = 4
ACC_ROWS = HEAD_DIM + 16
NEG_INF = float("-inf")

F32 = jnp.float32
BF16 = jnp.bfloat16


def _resident(block_shape, index_map):
    return pl.BlockSpec(block_shape, index_map, pipeline_mode=pl.Buffered(1))


def _dot(a, b):
    return jnp.dot(a, b, preferred_element_type=F32)


def _dot_nt(a, b):
    return lax.dot_general(a, b, (((1,), (1,)), ((), ())), preferred_element_type=F32)


def _rms_modulate(xv, g, shift, scale):
    ms = jnp.mean(xv * xv, axis=-1, keepdims=True)
    y = xv * lax.rsqrt(ms + RMS_EPS)
    return (y * g) * (1.0 + scale) + shift


def _toeplitz(u, rows):
    wide = jnp.broadcast_to(u, (rows, 2 * rows))
    return pltpu.roll(wide, 0, 1, stride=1, stride_axis=0)[:, rows:]


def _t5_bucket(dist):
    n = np.maximum(dist, 0)
    nf = np.maximum(n, 1).astype(np.float32)
    large = MAX_EXACT + (np.log(nf / np.float32(MAX_EXACT)) / np.float32(math.log(MAX_DISTANCE / MAX_EXACT))
                         * np.float32(NUM_BUCKETS - MAX_EXACT)).astype(np.int32)
    large = np.minimum(large, NUM_BUCKETS - 1)
    return np.where(n < MAX_EXACT, n, large).astype(np.int32)


def _bias_rows(bucket_ref, rbt_ref, o_ref):
    bucket = bucket_ref[...]
    n = bucket.shape[1]
    onehot = jnp.where(lax.broadcasted_iota(jnp.int32, (NUM_BUCKETS, n), 0) == bucket, 1.0, 0.0)
    rows = jnp.dot(rbt_ref[...], onehot, preferred_element_type=F32,
                   precision=lax.Precision.HIGHEST)
    rows = jnp.where(bucket >= 0, rows * LOG2E, NEG_INF)
    for d in range(BIAS_SLOTS):
        o_ref[d] = rows[:, d * MOBA_BLOCK:(d + 2) * MOBA_BLOCK]


def _ada_kernel(c_ref, w_ref, b_ref, bucket_ref, rbt_ref, sinks_ref, o_ref, rows_ref, sink_rows_ref, cs_ref):
    batch = c_ref.shape[0]
    cs_ref[...] = jnp.zeros(cs_ref.shape, F32)
    cs_ref[0:batch, :] = jax.nn.silu(c_ref[...])
    res = _dot(cs_ref[...].astype(BF16), w_ref[...].astype(BF16)) + b_ref[...]
    for b in range(batch):
        o_ref[b, 0] = res[b:b + 1]

    @pl.when(pl.program_id(0) == 0)
    def _():
        _bias_rows(bucket_ref, rbt_ref, rows_ref)
        sinks = sinks_ref[...] * LOG2E
        for h in range(SWA_Q_HEADS):
            g, gi = divmod(h, SWA_GROUP)
            sink_rows_ref[g, :, gi * SWA_BLOCK:(gi + 1) * SWA_BLOCK] = jnp.broadcast_to(
                sinks[:, h:h + 1], (1, SWA_BLOCK))


def _ada(c, w, b, rel_bias, sinks):
    batch = c.shape[0]
    assert w.shape[1] == N_MOD * D_MODEL
    n = (BIAS_SLOTS + 1) * MOBA_BLOCK
    dist = np.arange(n, dtype=np.int32) - MOBA_BLOCK
    bucket = jnp.asarray(np.where(dist >= 0, _t5_bucket(dist), -1)[None, :], dtype=jnp.int32)
    rows_shape = (BIAS_SLOTS, N_ATTN_HEADS, 2 * MOBA_BLOCK)
    sink_shape = (SWA_KV_HEADS, 1, SWA_GROUP * SWA_BLOCK)
    return pl.pallas_call(
        _ada_kernel,
        grid=(N_MOD,),
        in_specs=[pl.BlockSpec((batch, D_MODEL), lambda j: (0, 0)),
                  pl.BlockSpec((D_MODEL, D_MODEL), lambda j: (0, j)),
                  pl.BlockSpec((1, D_MODEL), lambda j: (0, j)),
                  pl.BlockSpec((1, n), lambda j: (0, 0)),
                  pl.BlockSpec((N_ATTN_HEADS, NUM_BUCKETS), lambda j: (0, 0)),
                  pl.BlockSpec((1, SWA_Q_HEADS), lambda j: (0, 0))],
        out_specs=(pl.BlockSpec((batch, 1, 1, D_MODEL), lambda j: (0, j, 0, 0)),
                   pl.BlockSpec(rows_shape, lambda j: (0, 0, 0)),
                   pl.BlockSpec(sink_shape, lambda j: (0, 0, 0))),
        out_shape=(jax.ShapeDtypeStruct((batch, N_MOD, 1, D_MODEL), F32),
                   jax.ShapeDtypeStruct(rows_shape, F32),
                   jax.ShapeDtypeStruct(sink_shape, F32)),
        scratch_shapes=[pltpu.VMEM((SUBLANES, D_MODEL), F32)],
        compiler_params=pltpu.CompilerParams(dimension_semantics=("arbitrary",),
                                             vmem_limit_bytes=VMEM_LIMIT_BYTES),
        name="ada",
    )(c, w, b, bucket, rel_bias.astype(F32).T, sinks.astype(F32)[None, :])


IN_WIDTHS = (SWA_Q_W, SWA_KV_W, SWA_KV_W, MOBA_W, MOBA_W, MOBA_W, D_MODEL, D_MODEL)
IN_OFFS = tuple(sum(IN_WIDTHS[:k]) for k in range(len(IN_WIDTHS) + 1))
FEAT_COLS = (0, 2, 3, 5)
FEAT_W = sum(IN_WIDTHS[k] for k in FEAT_COLS)
TOK_COLS = (1, 4, 6, 7)
TOK_W = sum(IN_WIDTHS[k] for k in TOK_COLS)
Q_COLS = (0, 3)
BLOCKS_PER_TILE = TOKEN_TILE // MOBA_BLOCK


def _proj_kernel(x_ref, shift_ref, scale_ref, g_ref, w_ref, *refs):
    n_cast = (len(refs) - 12) // 2
    cast_in, refs = refs[:n_cast], refs[n_cast:]
    ka_ref, kb_ref, ga_ref, gb_ref, qa_ref, va_ref, qb_ref, vb_ref, sel_ref = refs[:9]
    cast_out, (wk_ref, wt_ref, kmean_ref) = refs[9:9 + n_cast], refs[9 + n_cast:]

    for src, dst in zip(cast_in, cast_out):
        dst[...] = src[...].astype(dst.dtype)

    tile = pl.program_id(0) % (kmean_ref.shape[0] // BLOCKS_PER_TILE)

    @pl.when(tile == 0)
    def _():
        kmean_ref[...] = jnp.zeros(kmean_ref.shape, F32)

    @pl.when(pl.program_id(0) == 0)
    def _():
        o = 0
        for k in TOK_COLS:
            wk_ref[:, o:o + IN_WIDTHS[k]] = w_ref[:, IN_OFFS[k]:IN_OFFS[k + 1]].astype(BF16)
            o += IN_WIDTHS[k]
        o = 0
        for k in FEAT_COLS:
            wcol = w_ref[:, IN_OFFS[k]:IN_OFFS[k + 1]]
            if k in Q_COLS:
                wcol = wcol * (ATTN_SCALE * LOG2E)
            wt_ref[o:o + IN_WIDTHS[k], :] = wcol.T.astype(BF16)
            o += IN_WIDTHS[k]

    h = _rms_modulate(x_ref[...], g_ref[...], shift_ref[0, 0], scale_ref[0, 0]).astype(BF16)
    tok_refs = dict(zip(TOK_COLS, (ka_ref, kb_ref, ga_ref, gb_ref)))
    feat_refs = dict(zip(FEAT_COLS, (qa_ref, va_ref, qb_ref, vb_ref)))

    def token_major(k):
        o = sum(IN_WIDTHS[c] for c in TOK_COLS[:TOK_COLS.index(k)])
        out = _dot(h, wk_ref[:, o:o + IN_WIDTHS[k]])
        tok_refs[k][...] = out.astype(BF16)
        return out

    def feature_major(k):
        o = sum(IN_WIDTHS[c] for c in FEAT_COLS[:FEAT_COLS.index(k)])
        out = _dot_nt(wt_ref[o:o + IN_WIDTHS[k], :], h).astype(BF16)
        blk = feat_refs[k].shape[-1]
        for t in range(TOKEN_TILE // blk):
            feat_refs[k][0, t] = out[:, t * blk:(t + 1) * blk]
        return out

    _update_kmean(token_major(4), kmean_ref, tile)
    qb_t = feature_major(3)
    token_major(1)
    token_major(6)
    _moba_select(qb_t, sel_ref, kmean_ref, tile)
    token_major(7)
    for k in (0, 2, 5):
        feature_major(k)


def _update_kmean(kb, kmean_ref, tile):
    for t in range(BLOCKS_PER_TILE):
        kmean_ref[pl.ds(tile * BLOCKS_PER_TILE + t, 1), :] = jnp.mean(
            kb[t * MOBA_BLOCK:(t + 1) * MOBA_BLOCK, :], axis=0, keepdims=True)


def _moba_select(qb_t, sel_ref, kmean_ref, tile):
    nblk = kmean_ref.shape[0]
    blocks_per_tile = BLOCKS_PER_TILE
    blk = lax.broadcasted_iota(jnp.int32, (nblk, TOKEN_TILE), 0)
    qblk = tile * blocks_per_tile + lax.broadcasted_iota(jnp.int32, (nblk, TOKEN_TILE), 1) // MOBA_BLOCK
    past = blk < qblk
    own = jnp.where(blk == qblk, 0.0, NEG_INF)
    blk_f = blk.astype(F32)
    for h in range(MOBA_HEADS):
        hs = slice(h * HEAD_DIM, (h + 1) * HEAD_DIM)
        kmean = kmean_ref[:, hs]
        k1 = kmean.astype(BF16)
        r1 = kmean - k1.astype(F32)
        k2 = r1.astype(BF16)
        k3 = (r1 - k2.astype(F32)).astype(BF16)
        parts = _dot(jnp.concatenate([k1, k2, k3], axis=0), qb_t[hs, :])
        gate = parts[0:nblk] + parts[nblk:2 * nblk] + parts[2 * nblk:3 * nblk]
        gate = jnp.where(past, gate, NEG_INF)
        mask = own
        for _ in range(MOBA_TOPK):
            top = jnp.max(gate, axis=0, keepdims=True)
            first = jnp.min(jnp.where(gate == top, blk_f, float(nblk)), axis=0, keepdims=True)
            pick = blk_f == first
            mask = jnp.where(jnp.logical_and(pick, past), 0.0, mask)
            gate = jnp.where(pick, NEG_INF, gate)
        sel_ref[0, h] = mask


def _proj(x2, mod4, g1, w_in, cast_weights, batch, seq):
    tokens = x2.shape[0]
    steps = tokens // TOKEN_TILE
    assert all(w.shape[0] % (16 * steps) == 0 for w in cast_weights)
    cast_specs = [pl.BlockSpec((w.shape[0] // steps, w.shape[1]), lambda i: (i, 0)) for w in cast_weights]
    tiles_per_seq = seq // TOKEN_TILE
    na = TOKEN_TILE // SWA_BLOCK
    nb = TOKEN_TILE // MOBA_BLOCK

    def feat_spec(width, blk, per_tile):
        return pl.BlockSpec((1, per_tile, width, blk),
                            lambda i: (i // tiles_per_seq, i % tiles_per_seq, 0, 0))

    out_shape = (
        jax.ShapeDtypeStruct((tokens, SWA_KV_W), BF16),
        jax.ShapeDtypeStruct((tokens, MOBA_W), BF16),
        jax.ShapeDtypeStruct((tokens, D_MODEL), BF16),
        jax.ShapeDtypeStruct((tokens, D_MODEL), BF16),
        jax.ShapeDtypeStruct((batch, seq // SWA_BLOCK, SWA_Q_W, SWA_BLOCK), BF16),
        jax.ShapeDtypeStruct((batch, seq // SWA_BLOCK, SWA_KV_W, SWA_BLOCK), BF16),
        jax.ShapeDtypeStruct((batch, seq // MOBA_BLOCK, MOBA_W, MOBA_BLOCK), BF16),
        jax.ShapeDtypeStruct((batch, seq // MOBA_BLOCK, MOBA_W, MOBA_BLOCK), BF16),
        jax.ShapeDtypeStruct((batch, MOBA_HEADS, seq // MOBA_BLOCK, seq), F32),
    )
    tok_spec = lambda w: pl.BlockSpec((TOKEN_TILE, w), lambda i: (i, 0))
    return pl.pallas_call(
        _proj_kernel,
        grid=(steps,),
        in_specs=[
            pl.BlockSpec((TOKEN_TILE, D_MODEL), lambda i: (i, 0)),
            pl.BlockSpec((1, 1, 1, D_MODEL), lambda i: (i // tiles_per_seq, 0, 0, 0)),
            pl.BlockSpec((1, 1, 1, D_MODEL), lambda i: (i // tiles_per_seq, 1, 0, 0)),
            _resident((1, D_MODEL), lambda i: (0, 0)),
            _resident(w_in.shape, lambda i: (0, 0)),
        ] + cast_specs,
        out_specs=(tok_spec(SWA_KV_W), tok_spec(MOBA_W), tok_spec(D_MODEL), tok_spec(D_MODEL),
                   feat_spec(SWA_Q_W, SWA_BLOCK, na), feat_spec(SWA_KV_W, SWA_BLOCK, na),
                   feat_spec(MOBA_W, MOBA_BLOCK, nb), feat_spec(MOBA_W, MOBA_BLOCK, nb),
                   pl.BlockSpec((1, MOBA_HEADS, seq // MOBA_BLOCK, TOKEN_TILE),
                                lambda i: (i // tiles_per_seq, 0, 0, i % tiles_per_seq)))
        + tuple(cast_specs),
        out_shape=out_shape + tuple(jax.ShapeDtypeStruct(w.shape, BF16) for w in cast_weights),
        scratch_shapes=[pltpu.VMEM((D_MODEL, TOK_W), BF16), pltpu.VMEM((FEAT_W, D_MODEL), BF16),
                        pltpu.VMEM((seq // MOBA_BLOCK, MOBA_W), F32)],
        compiler_params=pltpu.CompilerParams(dimension_semantics=("arbitrary",),
                                             vmem_limit_bytes=VMEM_LIMIT_BYTES),
        name="proj",
    )(x2, mod4, mod4, g1, w_in, *cast_weights)


def _swa_kernel(q_ref, k_ref, v_ref, rows_ref, sink_ref, o_ref, bias_ref, ot_ref, s_ref, cm_ref):
    L = SWA_BLOCK
    nblk = q_ref.shape[1]

    @pl.when(pl.program_id(0) == 0)
    def _():
        k_idx = lax.broadcasted_iota(jnp.int32, (L, L), 0)
        q_idx = lax.broadcasted_iota(jnp.int32, (L, L), 1)
        for h in range(SWA_Q_HEADS):
            g, gi = divmod(h, SWA_GROUP)
            t0 = rows_ref[0, h:h + 1, MOBA_BLOCK:MOBA_BLOCK + L]
            r = _toeplitz(jnp.concatenate([t0, t0], axis=1), L)
            cols = slice(gi * L, (gi + 1) * L)
            bias_ref[g, 0:L, cols] = jnp.where(q_idx < k_idx, r, NEG_INF).astype(BF16)
            bias_ref[g, L:2 * L, cols] = jnp.where(q_idx >= k_idx, r, NEG_INF).astype(BF16)

    ones_blk = jnp.concatenate([jnp.ones((1, 2 * L), BF16), jnp.zeros((ACC_ROWS - HEAD_DIM - 1, 2 * L), BF16)],
                               axis=0)
    PW = 2 * L
    chains = [(g, hp) for g in range(SWA_KV_HEADS) for hp in range(SWA_GROUP // 2)]
    kv_cols = [slice(g * HEAD_DIM, (g + 1) * HEAD_DIM) for g in range(SWA_KV_HEADS)]

    def q_pair(c, g, hp):
        h0 = g * SWA_GROUP + 2 * hp
        return jnp.concatenate([q_ref[0, c, (h0 + e) * HEAD_DIM:(h0 + e + 1) * HEAD_DIM, :] for e in range(2)],
                               axis=1)

    def scores_first(idx):
        g, hp = chains[idx]
        lanes = slice(hp * PW, (hp + 1) * PW)
        s_ref[idx, 0:L, :] = jnp.full((L, PW), NEG_INF, BF16)
        s = _dot(k_ref[0, 0:L, kv_cols[g]], q_pair(0, g, hp)).astype(BF16) + bias_ref[g, L:2 * L, lanes]
        s_ref[idx, L:2 * L, :] = s
        cm_ref[idx] = jnp.max(s, axis=0, keepdims=True).astype(F32)

    def scores(c, idx):
        g, hp = chains[idx]
        kw = k_ref[0, pl.ds(pl.multiple_of((c - 1) * L, L), 2 * L), kv_cols[g]]
        s = _dot(kw, q_pair(c, g, hp)).astype(BF16) + bias_ref[g, :, hp * PW:(hp + 1) * PW]
        s_ref[idx] = s
        cm_ref[idx] = jnp.max(s, axis=0, keepdims=True).astype(F32)

    def attend(c, c_prev, idx, slot):
        g, hp = chains[idx]
        hs = kv_cols[g]
        s = s_ref[idx]
        sink = sink_ref[g, :, hp * PW:(hp + 1) * PW]
        m_b = jnp.maximum(cm_ref[idx], sink).astype(BF16)
        m = m_b.astype(F32)
        p = jnp.exp2(s - m_b)
        v_win = jnp.concatenate([v_ref[0, c_prev, hs, :], v_ref[0, c, hs, :]], axis=1)
        pv = _dot(jnp.concatenate([v_win, ones_blk], axis=0), p)
        o = pv[0:HEAD_DIM] / (pv[HEAD_DIM:HEAD_DIM + 1] + jnp.exp2(sink - m))
        for e in range(2):
            r = (g * SWA_GROUP + 2 * hp + e) * HEAD_DIM
            ot_ref[slot, r:r + HEAD_DIM, :] = o[:, e * L:(e + 1) * L]

    def emit(c, slot):
        o_ref[0, pl.ds(pl.multiple_of(c * L, L), L), :] = ot_ref[slot].T.astype(o_ref.dtype)

    for idx in range(len(chains)):
        scores_first(idx)
    U = SWA_UNROLL
    for idx in range(len(chains)):
        attend(0, 0, idx, U - 1)
        scores(1, idx)

    def body(c, e):
        emit(c - 1, (e - 1) % U)
        for idx in range(len(chains)):
            attend(c, c - 1, idx, e)
            scores(c + 1, idx)

    def body_unrolled(u, carry):
        for e in range(U):
            body(U * u + 1 + e, e)
        return carry

    lax.fori_loop(0, (nblk - 2) // U, body_unrolled, 0)
    emit(nblk - 2, U - 1)
    for idx in range(len(chains)):
        attend(nblk - 1, nblk - 2, idx, 0)
    emit(nblk - 1, 0)


def _swa(qa_t, ka3, va_t, rows, sink_rows):
    batch, nblk = qa_t.shape[0], qa_t.shape[1]
    seq = nblk * SWA_BLOCK
    assert nblk >= 2 and (nblk - 2) % SWA_UNROLL == 0 and SWA_UNROLL >= 2
    return pl.pallas_call(
        _swa_kernel,
        grid=(batch,),
        in_specs=[
            pl.BlockSpec((1, nblk, SWA_Q_W, SWA_BLOCK), lambda b: (b, 0, 0, 0)),
            pl.BlockSpec((1, seq, SWA_KV_W), lambda b: (b, 0, 0)),
            pl.BlockSpec((1, nblk, SWA_KV_W, SWA_BLOCK), lambda b: (b, 0, 0, 0)),
            _resident((1, N_ATTN_HEADS, 2 * MOBA_BLOCK), lambda b: (0, 0, 0)),
            _resident(sink_rows.shape, lambda b: (0, 0, 0)),
        ],
        out_specs=pl.BlockSpec((1, seq, SWA_Q_W), lambda b: (b, 0, 0)),
        out_shape=jax.ShapeDtypeStruct((batch, seq, SWA_Q_W), BF16),
        scratch_shapes=[pltpu.VMEM((SWA_KV_HEADS, 2 * SWA_BLOCK, SWA_GROUP * SWA_BLOCK), BF16),
                        pltpu.VMEM((SWA_UNROLL, SWA_Q_W, SWA_BLOCK), F32),
                        pltpu.VMEM((SWA_Q_HEADS // 2, 2 * SWA_BLOCK, 2 * SWA_BLOCK), BF16),
                        pltpu.VMEM((SWA_Q_HEADS // 2, 1, 2 * SWA_BLOCK), F32)],
        compiler_params=pltpu.CompilerParams(dimension_semantics=("arbitrary",),
                                             vmem_limit_bytes=VMEM_LIMIT_BYTES),
        name="swa",
    )(qa_t, ka3, va_t, rows, sink_rows)


def _moba_kernel(q_ref, qn_ref, k_ref, v_ref, sel_ref, rows_ref, o_ref, bias_ref, m_ref, acc_ref, s_ref, cm_ref):
    step = pl.program_id(1)
    last = pl.num_programs(1) - 1
    MB = MOBA_BLOCK
    H = MOBA_HEADS
    G = MOBA_Q_PER_STEP

    @pl.when(jnp.logical_and(pl.program_id(0) == 0, step == 0))
    def _():
        for h in range(H):
            for d in range(BIAS_SLOTS):
                bias_ref[h, d] = _toeplitz(rows_ref[d, SWA_Q_HEADS + h:SWA_Q_HEADS + h + 1, :], MB).astype(BF16)

    m_ref[...] = jnp.full(m_ref.shape, NEG_INF, F32)
    acc_ref[...] = jnp.zeros(acc_ref.shape, F32)
    heads = [slice(h * HEAD_DIM, (h + 1) * HEAD_DIM) for h in range(H)]
    ones_blk = jnp.concatenate([jnp.ones((1, MB), BF16), jnp.zeros((ACC_ROWS - HEAD_DIM - 1, MB), BF16)],
                               axis=0)

    def sel_row(j, qs, h):
        return sel_ref[0, h, pl.ds(j, 1), qs * MB:(qs + 1) * MB]

    def scores(j, qs, h):
        c = qs * H + h
        rows = pl.ds(pl.multiple_of(j * MB, MB), MB)
        slot = jnp.minimum(G * step + qs - j, BIAS_SLOTS - 1)
        s = _dot(k_ref[0, rows, heads[h]], q_ref[0, qs, heads[h], :]).astype(BF16) + bias_ref[h, slot]
        s_ref[c] = s
        cm_ref[c] = jnp.max(s, axis=0, keepdims=True).astype(F32) + sel_row(j, qs, h)

    def own_scores(blk, q_blk, c, h):
        rows = pl.ds(pl.multiple_of(blk * MB, MB), MB)
        s = _dot(k_ref[0, rows, heads[h]], q_blk[heads[h], :]).astype(BF16) + bias_ref[h, 0]
        s_ref[c] = s
        cm_ref[c] = jnp.max(s, axis=0, keepdims=True).astype(F32)

    def accumulate(j, qs, h):
        c = qs * H + h
        m_old = m_ref[c]
        m_new = jnp.maximum(m_old, cm_ref[c])
        alpha = jnp.exp2(m_old - m_new)
        p = jnp.exp2(s_ref[c] - (m_new - sel_row(j, qs, h)).astype(BF16))
        v_aug = jnp.concatenate([v_ref[0, j, heads[h], :], ones_blk], axis=0)
        rs = slice(c * ACC_ROWS, (c + 1) * ACC_ROWS)
        acc_ref[rs, :] = alpha * acc_ref[rs, :] + _dot(v_aug, p)
        m_ref[c] = m_new

    first = G * step

    @pl.when(step == 0)
    def _():
        for qs in range(G):
            for h in range(H):
                own_scores(qs, q_ref[0, qs], qs * H + h, h)

    for n in range(G - 1):
        for qs in range(G - 1 - n, G):
            for h in range(H):
                accumulate(first + G - 1 - n, qs, h)
                scores(first + G - 2 - n, qs, h)

    def body(j, carry):
        for qs in range(G):
            for h in range(H):
                accumulate(j, qs, h)
                scores(j - 1, qs, h)
        return carry

    def body_unrolled(u, carry):
        for e in range(MOBA_UNROLL):
            carry = body(first - MOBA_UNROLL * u - e, carry)
        return carry

    lax.fori_loop(0, (G // MOBA_UNROLL) * step, body_unrolled, 0)

    @pl.when(step < last)
    def _():
        for h in range(H):
            for qs in range(G):
                accumulate(0, qs, h)
                own_scores(first + G + qs, qn_ref[0, qs], qs * H + h, h)

    @pl.when(step == last)
    def _():
        for h in range(H):
            for qs in range(G):
                accumulate(0, qs, h)

    for qs in range(G):
        outs = []
        for h in range(H):
            base = (qs * H + h) * ACC_ROWS
            outs.append(acc_ref[base:base + HEAD_DIM, :] / acc_ref[base + HEAD_DIM:base + HEAD_DIM + 1, :])
        o_ref[0, qs * MB:(qs + 1) * MB, :] = jnp.concatenate(outs, axis=0).T.astype(o_ref.dtype)


def _moba(qb_t, kb3, vb_t, sel, rows):
    batch, nblk = qb_t.shape[0], qb_t.shape[1]
    seq = nblk * MOBA_BLOCK
    g = MOBA_Q_PER_STEP
    assert g % MOBA_UNROLL == 0 and nblk % g == 0
    chains = g * MOBA_HEADS
    return pl.pallas_call(
        _moba_kernel,
        grid=(batch, nblk // g),
        in_specs=[
            pl.BlockSpec((1, g, MOBA_W, MOBA_BLOCK), lambda b, i: (b, i, 0, 0)),
            pl.BlockSpec((1, g, MOBA_W, MOBA_BLOCK), lambda b, i: (b, jnp.minimum(i + 1, nblk // g - 1), 0, 0)),
            pl.BlockSpec((1, seq, MOBA_W), lambda b, i: (b, 0, 0)),
            pl.BlockSpec((1, nblk, MOBA_W, MOBA_BLOCK), lambda b, i: (b, 0, 0, 0)),
            pl.BlockSpec((1, MOBA_HEADS, nblk, g * MOBA_BLOCK), lambda b, i: (b, 0, 0, i)),
            _resident(rows.shape, lambda b, i: (0, 0, 0)),
        ],
        out_specs=pl.BlockSpec((1, g * MOBA_BLOCK, MOBA_W), lambda b, i: (b, i, 0)),
        out_shape=jax.ShapeDtypeStruct((batch, seq, MOBA_W), BF16),
        scratch_shapes=[pltpu.VMEM((MOBA_HEADS, BIAS_SLOTS, MOBA_BLOCK, MOBA_BLOCK), BF16),
                        pltpu.VMEM((chains, 1, MOBA_BLOCK), F32),
                        pltpu.VMEM((chains * ACC_ROWS, MOBA_BLOCK), F32),
                        pltpu.VMEM((chains, MOBA_BLOCK, MOBA_BLOCK), BF16),
                        pltpu.VMEM((chains, 1, MOBA_BLOCK), F32)],
        compiler_params=pltpu.CompilerParams(dimension_semantics=("arbitrary", "arbitrary"),
                                             vmem_limit_bytes=VMEM_LIMIT_BYTES),
        name="moba",
    )(qb_t, qb_t, kb3, vb_t, sel, rows)


def _post_kernel(x_ref, ya_ref, yb_ref, ga_ref, gb_ref, gate1_ref, shift2_ref, scale2_ref,
                 gate2_ref, g2_ref, gf_ref, wa_ref, wb_ref, wo_ref, w1_ref, w2_ref, o_ref):
    a = _dot(ya_ref[...], wa_ref[...])
    b = _dot(yb_ref[...], wb_ref[...])
    merged = (jax.nn.sigmoid(ga_ref[...].astype(F32)) * a
              + jax.nn.sigmoid(gb_ref[...].astype(F32)) * b)
    x1 = x_ref[...] + gate1_ref[0, 0] * _dot(merged.astype(BF16), wo_ref[...])
    h2 = _rms_modulate(x1, g2_ref[...], shift2_ref[0, 0], scale2_ref[0, 0]).astype(BF16)
    y = jnp.zeros_like(x1)
    for c in range(D_FF // FF_CHUNK):
        cs = slice(c * FF_CHUNK, (c + 1) * FF_CHUNK)
        u = jnp.square(jnp.maximum(_dot(h2, w1_ref[:, cs]), 0.0)).astype(BF16)
        y = y + _dot(u, w2_ref[cs, :])
    x2 = x1 + gate2_ref[0, 0] * y
    ms = jnp.mean(x2 * x2, axis=-1, keepdims=True)
    o_ref[...] = (x2 * lax.rsqrt(ms + RMS_EPS)) * gf_ref[...]


def _post(x2, ya, yb, ga, gb, mod4, g2, gf, wa, wb, wo, w1, w2, seq):
    tokens = x2.shape[0]
    tiles_per_seq = seq // TOKEN_TILE
    tok = lambda w: pl.BlockSpec((TOKEN_TILE, w), lambda i: (i, 0))
    modrow = lambda k: pl.BlockSpec((1, 1, 1, D_MODEL), lambda i: (i // tiles_per_seq, k, 0, 0))
    full = lambda a: _resident(a.shape, lambda i: (0, 0))
    return pl.pallas_call(
        _post_kernel,
        grid=(tokens // TOKEN_TILE,),
        in_specs=[tok(D_MODEL), tok(SWA_Q_W), tok(MOBA_W), tok(D_MODEL), tok(D_MODEL),
                  modrow(2), modrow(3), modrow(4), modrow(5),
                  full(g2), full(gf), full(wa), full(wb), full(wo), full(w1), full(w2)],
        out_specs=tok(D_MODEL),
        out_shape=jax.ShapeDtypeStruct((tokens, D_MODEL), F32),
        compiler_params=pltpu.CompilerParams(dimension_semantics=("arbitrary",),
                                             vmem_limit_bytes=VMEM_LIMIT_BYTES),
        name="post",
    )(x2, ya, yb, ga, gb, mod4, mod4, mod4, mod4, g2, gf, wa, wb, wo, w1, w2)


def kernel(x, c, ada_w, ada_b, norm1_g, norm2_g, w_in, attn_sinks, rel_bias, w_branch_a,
           w_branch_b, w_out, w_mlp_in, w_mlp_out, final_g):
    batch, seq, _ = x.shape
    depth = ada_w.shape[0]
    assert seq % TOKEN_TILE == 0 and TOKEN_TILE % MOBA_BLOCK == 0 and batch <= SUBLANES
    far = np.arange((BIAS_SLOTS - 1) * MOBA_BLOCK - (MOBA_BLOCK - 1), max(seq, BIAS_SLOTS * MOBA_BLOCK))
    assert np.all(_t5_bucket(far) == _t5_bucket(far[:1]))
    assert 2 * SWA_BLOCK <= MOBA_BLOCK and SWA_WINDOW == SWA_BLOCK
    assert depth == 1
    l = 0


    x2 = x.reshape(batch * seq, D_MODEL)
    mod4, rows, sink_rows = _ada(c.astype(F32), ada_w[l], ada_b[l][None, :], rel_bias, attn_sinks[l])
    later_weights = [w.astype(F32) for w in (w_branch_a[l], w_branch_b[l], w_out[l], w_mlp_in[l], w_mlp_out[l])]
    ka, kb, ga, gb, qa_t, va_t, qb_t, vb_t, sel, wa, wb, wo, w1, w2 = _proj(
        x2, mod4, norm1_g[l][None, :], w_in[l].astype(F32), later_weights, batch, seq)

    ya = _swa(qa_t, ka.reshape(batch, seq, SWA_KV_W), va_t, rows, sink_rows)

    yb = _moba(qb_t, kb.reshape(batch, seq, MOBA_W), vb_t, sel, rows)

    out = _post(x2, ya.reshape(batch * seq, SWA_Q_W), yb.reshape(batch * seq, MOBA_W), ga, gb,
                mod4, norm2_g[l][None, :], final_g[None, :], wa, wb, wo, w1, w2, seq)
    return out.reshape(batch, seq, D_MODEL)
```

```python
import math

import jax
import jax.numpy as jnp
import numpy as np
from jax import lax
from jax.experimental import pallas as pl
from jax.experimental.pallas import tpu as pltpu

D_MODEL = 1024
HEAD_DIM = 64
ATTN_SCALE = HEAD_DIM ** -0.5
LOG2E = math.log2(math.e)
SWA_Q_HEADS = 8
SWA_KV_HEADS = 2
SWA_GROUP = SWA_Q_HEADS // SWA_KV_HEADS
SWA_WINDOW = 128
SWA_BLOCK = 128
MOBA_HEADS = 8
MOBA_BLOCK = 256
MOBA_TOPK = 3
NUM_BUCKETS = 32
MAX_EXACT = NUM_BUCKETS // 2
MAX_DISTANCE = 2048
N_ATTN_HEADS = SWA_Q_HEADS + MOBA_HEADS
SWA_Q_W = SWA_Q_HEADS * HEAD_DIM
SWA_KV_W = SWA_KV_HEADS * HEAD_DIM
MOBA_W = MOBA_HEADS * HEAD_DIM
D_FF = 4 * D_MODEL
N_MOD = 6
RMS_EPS = 1e-6

VMEM_LIMIT_BYTES = 56 * 1024 * 1024
TOKEN_TILE = 512
FF_CHUNK = 1024
SUBLANES = 8
BIAS_SLOTS = 8
SWA_UNROLL = 30
MOBA_Q_PER_STEP = 4
MOBA_UNROLL = 4
ACC_ROWS = HEAD_DIM + 16
NEG_INF = float("-inf")

F32 = jnp.float32
BF16 = jnp.bfloat16


def _resident(block_shape, index_map):
    return pl.BlockSpec(block_shape, index_map, pipeline_mode=pl.Buffered(1))


def _dot(a, b):
    return jnp.dot(a, b, preferred_element_type=F32)


def _dot_nt(a, b):
    return lax.dot_general(a, b, (((1,), (1,)), ((), ())), preferred_element_type=F32)


def _rms_modulate(xv, g, shift, scale):
    ms = jnp.mean(xv * xv, axis=-1, keepdims=True)
    y = xv * lax.rsqrt(ms + RMS_EPS)
    return (y * g) * (1.0 + scale) + shift


def _toeplitz(u, rows):
    wide = jnp.broadcast_to(u, (rows, 2 * rows))
    return pltpu.roll(wide, 0, 1, stride=1, stride_axis=0)[:, rows:]


def _t5_bucket(dist):
    n = np.maximum(dist, 0)
    nf = np.maximum(n, 1).astype(np.float32)
    large = MAX_EXACT + (np.log(nf / np.float32(MAX_EXACT)) / np.float32(math.log(MAX_DISTANCE / MAX_EXACT))
                         * np.float32(NUM_BUCKETS - MAX_EXACT)).astype(np.int32)
    large = np.minimum(large, NUM_BUCKETS - 1)
    return np.where(n < MAX_EXACT, n, large).astype(np.int32)


def _bias_rows(bucket_ref, rbt_ref, o_ref):
    bucket = bucket_ref[...]
    n = bucket.shape[1]
    onehot = jnp.where(lax.broadcasted_iota(jnp.int32, (NUM_BUCKETS, n), 0) == bucket, 1.0, 0.0)
    rows = jnp.dot(rbt_ref[...], onehot, preferred_element_type=F32,
                   precision=lax.Precision.HIGHEST)
    rows = jnp.where(bucket >= 0, rows * LOG2E, NEG_INF)
    for d in range(BIAS_SLOTS):
        o_ref[d] = rows[:, d * MOBA_BLOCK:(d + 2) * MOBA_BLOCK]


def _ada_kernel(c_ref, w_ref, b_ref, bucket_ref, rbt_ref, sinks_ref, o_ref, rows_ref, sink_rows_ref, cs_ref):
    batch = c_ref.shape[0]
    cs_ref[...] = jnp.zeros(cs_ref.shape, F32)
    cs_ref[0:batch, :] = jax.nn.silu(c_ref[...])
    res = _dot(cs_ref[...].astype(BF16), w_ref[...].astype(BF16)) + b_ref[...]
    for b in range(batch):
        o_ref[b, 0] = res[b:b + 1]

    @pl.when(pl.program_id(0) == 0)
    def _():
        _bias_rows(bucket_ref, rbt_ref, rows_ref)
        sinks = sinks_ref[...] * LOG2E
        for h in range(SWA_Q_HEADS):
            g, gi = divmod(h, SWA_GROUP)
            sink_rows_ref[g, :, gi * SWA_BLOCK:(gi + 1) * SWA_BLOCK] = jnp.broadcast_to(
                sinks[:, h:h + 1], (1, SWA_BLOCK))


def _ada(c, w, b, rel_bias, sinks):
    batch = c.shape[0]
    assert w.shape[1] == N_MOD * D_MODEL
    n = (BIAS_SLOTS + 1) * MOBA_BLOCK
    dist = np.arange(n, dtype=np.int32) - MOBA_BLOCK
    bucket = jnp.asarray(np.where(dist >= 0, _t5_bucket(dist), -1)[None, :], dtype=jnp.int32)
    rows_shape = (BIAS_SLOTS, N_ATTN_HEADS, 2 * MOBA_BLOCK)
    sink_shape = (SWA_KV_HEADS, 1, SWA_GROUP * SWA_BLOCK)
    return pl.pallas_call(
        _ada_kernel,
        grid=(N_MOD,),
        in_specs=[pl.BlockSpec((batch, D_MODEL), lambda j: (0, 0)),
                  pl.BlockSpec((D_MODEL, D_MODEL), lambda j: (0, j)),
                  pl.BlockSpec((1, D_MODEL), lambda j: (0, j)),
                  pl.BlockSpec((1, n), lambda j: (0, 0)),
                  pl.BlockSpec((N_ATTN_HEADS, NUM_BUCKETS), lambda j: (0, 0)),
                  pl.BlockSpec((1, SWA_Q_HEADS), lambda j: (0, 0))],
        out_specs=(pl.BlockSpec((batch, 1, 1, D_MODEL), lambda j: (0, j, 0, 0)),
                   pl.BlockSpec(rows_shape, lambda j: (0, 0, 0)),
                   pl.BlockSpec(sink_shape, lambda j: (0, 0, 0))),
        out_shape=(jax.ShapeDtypeStruct((batch, N_MOD, 1, D_MODEL), F32),
                   jax.ShapeDtypeStruct(rows_shape, F32),
                   jax.ShapeDtypeStruct(sink_shape, F32)),
        scratch_shapes=[pltpu.VMEM((SUBLANES, D_MODEL), F32)],
        compiler_params=pltpu.CompilerParams(dimension_semantics=("arbitrary",),
                                             vmem_limit_bytes=VMEM_LIMIT_BYTES),
        name="ada",
    )(c, w, b, bucket, rel_bias.astype(F32).T, sinks.astype(F32)[None, :])


IN_WIDTHS = (SWA_Q_W, SWA_KV_W, SWA_KV_W, MOBA_W, MOBA_W, MOBA_W, D_MODEL, D_MODEL)
IN_OFFS = tuple(sum(IN_WIDTHS[:k]) for k in range(len(IN_WIDTHS) + 1))
FEAT_COLS = (0, 2, 3, 5)
FEAT_W = sum(IN_WIDTHS[k] for k in FEAT_COLS)
TOK_COLS = (1, 4, 6, 7)
TOK_W = sum(IN_WIDTHS[k] for k in TOK_COLS)
Q_COLS = (0, 3)
BLOCKS_PER_TILE = TOKEN_TILE // MOBA_BLOCK


def _proj_kernel(x_ref, shift_ref, scale_ref, g_ref, w_ref, *refs):
    n_cast = (len(refs) - 12) // 2
    cast_in, refs = refs[:n_cast], refs[n_cast:]
    ka_ref, kb_ref, ga_ref, gb_ref, qa_ref, va_ref, qb_ref, vb_ref, sel_ref = refs[:9]
    cast_out, (wk_ref, wt_ref, kmean_ref) = refs[9:9 + n_cast], refs[9 + n_cast:]

    for src, dst in zip(cast_in, cast_out):
        dst[...] = src[...].astype(dst.dtype)

    tile = pl.program_id(0) % (kmean_ref.shape[0] // BLOCKS_PER_TILE)

    @pl.when(tile == 0)
    def _():
        kmean_ref[...] = jnp.zeros(kmean_ref.shape, F32)

    @pl.when(pl.program_id(0) == 0)
    def _():
        o = 0
        for k in TOK_COLS:
            wk_ref[:, o:o + IN_WIDTHS[k]] = w_ref[:, IN_OFFS[k]:IN_OFFS[k + 1]].astype(BF16)
            o += IN_WIDTHS[k]
        o = 0
        for k in FEAT_COLS:
            wcol = w_ref[:, IN_OFFS[k]:IN_OFFS[k + 1]]
            if k in Q_COLS:
                wcol = wcol * (ATTN_SCALE * LOG2E)
            wt_ref[o:o + IN_WIDTHS[k], :] = wcol.T.astype(BF16)
            o += IN_WIDTHS[k]

    h = _rms_modulate(x_ref[...], g_ref[...], shift_ref[0, 0], scale_ref[0, 0]).astype(BF16)
    tok_refs = dict(zip(TOK_COLS, (ka_ref, kb_ref, ga_ref, gb_ref)))
    feat_refs = dict(zip(FEAT_COLS, (qa_ref, va_ref, qb_ref, vb_ref)))

    def token_major(k):
        o = sum(IN_WIDTHS[c] for c in TOK_COLS[:TOK_COLS.index(k)])
        out = _dot(h, wk_ref[:, o:o + IN_WIDTHS[k]])
        tok_refs[k][...] = out.astype(BF16)
        return out

    def feature_major(k):
        o = sum(IN_WIDTHS[c] for c in FEAT_COLS[:FEAT_COLS.index(k)])
        out = _dot_nt(wt_ref[o:o + IN_WIDTHS[k], :], h).astype(BF16)
        blk = feat_refs[k].shape[-1]
        for t in range(TOKEN_TILE // blk):
            feat_refs[k][0, t] = out[:, t * blk:(t + 1) * blk]
        return out

    _update_kmean(token_major(4), kmean_ref, tile)
    qb_t = feature_major(3)
    token_major(1)
    token_major(6)
    _moba_select(qb_t, sel_ref, kmean_ref, tile)
    token_major(7)
    for k in (0, 2, 5):
        feature_major(k)


def _update_kmean(kb, kmean_ref, tile):
    for t in range(BLOCKS_PER_TILE):
        kmean_ref[pl.ds(tile * BLOCKS_PER_TILE + t, 1), :] = jnp.mean(
            kb[t * MOBA_BLOCK:(t + 1) * MOBA_BLOCK, :], axis=0, keepdims=True)


def _moba_select(qb_t, sel_ref, kmean_ref, tile):
    nblk = kmean_ref.shape[0]
    blocks_per_tile = BLOCKS_PER_TILE
    blk = lax.broadcasted_iota(jnp.int32, (nblk, TOKEN_TILE), 0)
    qblk = tile * blocks_per_tile + lax.broadcasted_iota(jnp.int32, (nblk, TOKEN_TILE), 1) // MOBA_BLOCK
    past = blk < qblk
    own = jnp.where(blk == qblk, 0.0, NEG_INF)
    blk_f = blk.astype(F32)
    for h in range(MOBA_HEADS):
        hs = slice(h * HEAD_DIM, (h + 1) * HEAD_DIM)
        kmean = kmean_ref[:, hs]
        k1 = kmean.astype(BF16)
        r1 = kmean - k1.astype(F32)
        k2 = r1.astype(BF16)
        k3 = (r1 - k2.astype(F32)).astype(BF16)
        parts = _dot(jnp.concatenate([k1, k2, k3], axis=0), qb_t[hs, :])
        gate = parts[0:nblk] + parts[nblk:2 * nblk] + parts[2 * nblk:3 * nblk]
        gate = jnp.where(past, gate, NEG_INF)
        mask = own
        for _ in range(MOBA_TOPK):
            top = jnp.max(gate, axis=0, keepdims=True)
            first = jnp.min(jnp.where(gate == top, blk_f, float(nblk)), axis=0, keepdims=True)
            pick = blk_f == first
            mask = jnp.where(jnp.logical_and(pick, past), 0.0, mask)
            gate = jnp.where(pick, NEG_INF, gate)
        sel_ref[0, h] = mask


def _proj(x2, mod4, g1, w_in, cast_weights, batch, seq):
    tokens = x2.shape[0]
    steps = tokens // TOKEN_TILE
    assert all(w.shape[0] % (16 * steps) == 0 for w in cast_weights)
    cast_specs = [pl.BlockSpec((w.shape[0] // steps, w.shape[1]), lambda i: (i, 0)) for w in cast_weights]
    tiles_per_seq = seq // TOKEN_TILE
    na = TOKEN_TILE // SWA_BLOCK
    nb = TOKEN_TILE // MOBA_BLOCK

    def feat_spec(width, blk, per_tile):
        return pl.BlockSpec((1, per_tile, width, blk),
                            lambda i: (i // tiles_per_seq, i % tiles_per_seq, 0, 0))

    out_shape = (
        jax.ShapeDtypeStruct((tokens, SWA_KV_W), BF16),
        jax.ShapeDtypeStruct((tokens, MOBA_W), BF16),
        jax.ShapeDtypeStruct((tokens, D_MODEL), BF16),
        jax.ShapeDtypeStruct((tokens, D_MODEL), BF16),
        jax.ShapeDtypeStruct((batch, seq // SWA_BLOCK, SWA_Q_W, SWA_BLOCK), BF16),
        jax.ShapeDtypeStruct((batch, seq // SWA_BLOCK, SWA_KV_W, SWA_BLOCK), BF16),
        jax.ShapeDtypeStruct((batch, seq // MOBA_BLOCK, MOBA_W, MOBA_BLOCK), BF16),
        jax.ShapeDtypeStruct((batch, seq // MOBA_BLOCK, MOBA_W, MOBA_BLOCK), BF16),
        jax.ShapeDtypeStruct((batch, MOBA_HEADS, seq // MOBA_BLOCK, seq), F32),
    )
    tok_spec = lambda w: pl.BlockSpec((TOKEN_TILE, w), lambda i: (i, 0))
    return pl.pallas_call(
        _proj_kernel,
        grid=(steps,),
        in_specs=[
            pl.BlockSpec((TOKEN_TILE, D_MODEL), lambda i: (i, 0)),
            pl.BlockSpec((1, 1, 1, D_MODEL), lambda i: (i // tiles_per_seq, 0, 0, 0)),
            pl.BlockSpec((1, 1, 1, D_MODEL), lambda i: (i // tiles_per_seq, 1, 0, 0)),
            _resident((1, D_MODEL), lambda i: (0, 0)),
            _resident(w_in.shape, lambda i: (0, 0)),
        ] + cast_specs,
        out_specs=(tok_spec(SWA_KV_W), tok_spec(MOBA_W), tok_spec(D_MODEL), tok_spec(D_MODEL),
                   feat_spec(SWA_Q_W, SWA_BLOCK, na), feat_spec(SWA_KV_W, SWA_BLOCK, na),
                   feat_spec(MOBA_W, MOBA_BLOCK, nb), feat_spec(MOBA_W, MOBA_BLOCK, nb),
                   pl.BlockSpec((1, MOBA_HEADS, seq // MOBA_BLOCK, TOKEN_TILE),
                                lambda i: (i // tiles_per_seq, 0, 0, i % tiles_per_seq)))
        + tuple(cast_specs),
        out_shape=out_shape + tuple(jax.ShapeDtypeStruct(w.shape, BF16) for w in cast_weights),
        scratch_shapes=[pltpu.VMEM((D_MODEL, TOK_W), BF16), pltpu.VMEM((FEAT_W, D_MODEL), BF16),
                        pltpu.VMEM((seq // MOBA_BLOCK, MOBA_W), F32)],
        compiler_params=pltpu.CompilerParams(dimension_semantics=("arbitrary",),
                                             vmem_limit_bytes=VMEM_LIMIT_BYTES),
        name="proj",
    )(x2, mod4, mod4, g1, w_in, *cast_weights)


def _swa_kernel(q_ref, k_ref, v_ref, rows_ref, sink_ref, o_ref, bias_ref, ot_ref, s_ref, cm_ref):
    L = SWA_BLOCK
    nblk = q_ref.shape[1]

    @pl.when(pl.program_id(0) == 0)
    def _():
        k_idx = lax.broadcasted_iota(jnp.int32, (L, L), 0)
        q_idx = lax.broadcasted_iota(jnp.int32, (L, L), 1)
        for h in range(SWA_Q_HEADS):
            g, gi = divmod(h, SWA_GROUP)
            t0 = rows_ref[0, h:h + 1, MOBA_BLOCK:MOBA_BLOCK + L]
            r = _toeplitz(jnp.concatenate([t0, t0], axis=1), L)
            cols = slice(gi * L, (gi + 1) * L)
            bias_ref[g, 0:L, cols] = jnp.where(q_idx < k_idx, r, NEG_INF).astype(BF16)
            bias_ref[g, L:2 * L, cols] = jnp.where(q_idx >= k_idx, r, NEG_INF).astype(BF16)

    ones_blk = jnp.concatenate([jnp.ones((1, 2 * L), BF16), jnp.zeros((ACC_ROWS - HEAD_DIM - 1, 2 * L), BF16)],
                               axis=0)
    PW = 2 * L
    chains = [(g, hp) for g in range(SWA_KV_HEADS) for hp in range(SWA_GROUP // 2)]
    kv_cols = [slice(g * HEAD_DIM, (g + 1) * HEAD_DIM) for g in range(SWA_KV_HEADS)]

    def q_pair(c, g, hp):
        h0 = g * SWA_GROUP + 2 * hp
        return jnp.concatenate([q_ref[0, c, (h0 + e) * HEAD_DIM:(h0 + e + 1) * HEAD_DIM, :] for e in range(2)],
                               axis=1)

    def scores_first(idx):
        g, hp = chains[idx]
        lanes = slice(hp * PW, (hp + 1) * PW)
        s_ref[idx, 0:L, :] = jnp.full((L, PW), NEG_INF, BF16)
        s = _dot(k_ref[0, 0:L, kv_cols[g]], q_pair(0, g, hp)).astype(BF16) + bias_ref[g, L:2 * L, lanes]
        s_ref[idx, L:2 * L, :] = s
        cm_ref[idx] = jnp.max(s, axis=0, keepdims=True).astype(F32)

    def scores(c, idx):
        g, hp = chains[idx]
        kw = k_ref[0, pl.ds(pl.multiple_of((c - 1) * L, L), 2 * L), kv_cols[g]]
        s = _dot(kw, q_pair(c, g, hp)).astype(BF16) + bias_ref[g, :, hp * PW:(hp + 1) * PW]
        s_ref[idx] = s
        cm_ref[idx] = jnp.max(s, axis=0, keepdims=True).astype(F32)

    def attend(c, c_prev, idx, slot):
        g, hp = chains[idx]
        hs = kv_cols[g]
        s = s_ref[idx]
        sink = sink_ref[g, :, hp * PW:(hp + 1) * PW]
        m_b = jnp.maximum(cm_ref[idx], sink).astype(BF16)
        m = m_b.astype(F32)
        p = jnp.exp2(s - m_b)
        v_win = jnp.concatenate([v_ref[0, c_prev, hs, :], v_ref[0, c, hs, :]], axis=1)
        pv = _dot(jnp.concatenate([v_win, ones_blk], axis=0), p)
        o = pv[0:HEAD_DIM] / (pv[HEAD_DIM:HEAD_DIM + 1] + jnp.exp2(sink - m))
        for e in range(2):
            r = (g * SWA_GROUP + 2 * hp + e) * HEAD_DIM
            ot_ref[slot, r:r + HEAD_DIM, :] = o[:, e * L:(e + 1) * L]

    def emit(c, slot):
        o_ref[0, pl.ds(pl.multiple_of(c * L, L), L), :] = ot_ref[slot].T.astype(o_ref.dtype)

    for idx in range(len(chains)):
        scores_first(idx)
    U = SWA_UNROLL
    for idx in range(len(chains)):
        attend(0, 0, idx, U - 1)
        scores(1, idx)

    def body(c, e):
        emit(c - 1, (e - 1) % U)
        for idx in range(len(chains)):
            attend(c, c - 1, idx, e)
            scores(c + 1, idx)

    def body_unrolled(u, carry):
        for e in range(U):
            body(U * u + 1 + e, e)
        return carry

    lax.fori_loop(0, (nblk - 2) // U, body_unrolled, 0)
    emit(nblk - 2, U - 1)
    for idx in range(len(chains)):
        attend(nblk - 1, nblk - 2, idx, 0)
    emit(nblk - 1, 0)


def _swa(qa_t, ka3, va_t, rows, sink_rows):
    batch, nblk = qa_t.shape[0], qa_t.shape[1]
    seq = nblk * SWA_BLOCK
    assert nblk >= 2 and (nblk - 2) % SWA_UNROLL == 0 and SWA_UNROLL >= 2
    return pl.pallas_call(
        _swa_kernel,
        grid=(batch,),
        in_specs=[
            pl.BlockSpec((1, nblk, SWA_Q_W, SWA_BLOCK), lambda b: (b, 0, 0, 0)),
            pl.BlockSpec((1, seq, SWA_KV_W), lambda b: (b, 0, 0)),
            pl.BlockSpec((1, nblk, SWA_KV_W, SWA_BLOCK), lambda b: (b, 0, 0, 0)),
            _resident((1, N_ATTN_HEADS, 2 * MOBA_BLOCK), lambda b: (0, 0, 0)),
            _resident(sink_rows.shape, lambda b: (0, 0, 0)),
        ],
        out_specs=pl.BlockSpec((1, seq, SWA_Q_W), lambda b: (b, 0, 0)),
        out_shape=jax.ShapeDtypeStruct((batch, seq, SWA_Q_W), BF16),
        scratch_shapes=[pltpu.VMEM((SWA_KV_HEADS, 2 * SWA_BLOCK, SWA_GROUP * SWA_BLOCK), BF16),
                        pltpu.VMEM((SWA_UNROLL, SWA_Q_W, SWA_BLOCK), F32),
                        pltpu.VMEM((SWA_Q_HEADS // 2, 2 * SWA_BLOCK, 2 * SWA_BLOCK), BF16),
                        pltpu.VMEM((SWA_Q_HEADS // 2, 1, 2 * SWA_BLOCK), F32)],
        compiler_params=pltpu.CompilerParams(dimension_semantics=("arbitrary",),
                                             vmem_limit_bytes=VMEM_LIMIT_BYTES),
        name="swa",
    )(qa_t, ka3, va_t, rows, sink_rows)


def _moba_kernel(q_ref, qn_ref, k_ref, v_ref, sel_ref, rows_ref, o_ref, bias_ref, m_ref, acc_ref, s_ref, cm_ref):
    step = pl.program_id(1)
    last = pl.num_programs(1) - 1
    MB = MOBA_BLOCK
    H = MOBA_HEADS
    G = MOBA_Q_PER_STEP

    @pl.when(jnp.logical_and(pl.program_id(0) == 0, step == 0))
    def _():
        for h in range(H):
            for d in range(BIAS_SLOTS):
                bias_ref[h, d] = _toeplitz(rows_ref[d, SWA_Q_HEADS + h:SWA_Q_HEADS + h + 1, :], MB).astype(BF16)

    m_ref[...] = jnp.full(m_ref.shape, NEG_INF, F32)
    acc_ref[...] = jnp.zeros(acc_ref.shape, F32)
    heads = [slice(h * HEAD_DIM, (h + 1) * HEAD_DIM) for h in range(H)]
    ones_blk = jnp.concatenate([jnp.ones((1, MB), BF16), jnp.zeros((ACC_ROWS - HEAD_DIM - 1, MB), BF16)],
                               axis=0)

    def sel_row(j, qs, h):
        return sel_ref[0, h, pl.ds(j, 1), qs * MB:(qs + 1) * MB]

    def scores(j, qs, h):
        c = qs * H + h
        rows = pl.ds(pl.multiple_of(j * MB, MB), MB)
        slot = jnp.minimum(G * step + qs - j, BIAS_SLOTS - 1)
        s = _dot(k_ref[0, rows, heads[h]], q_ref[0, qs, heads[h], :]).astype(BF16) + bias_ref[h, slot]
        s_ref[c] = s
        cm_ref[c] = jnp.max(s, axis=0, keepdims=True).astype(F32) + sel_row(j, qs, h)

    def own_scores(blk, q_blk, c, h):
        rows = pl.ds(pl.multiple_of(blk * MB, MB), MB)
        s = _dot(k_ref[0, rows, heads[h]], q_blk[heads[h], :]).astype(BF16) + bias_ref[h, 0]
        s_ref[c] = s
        cm_ref[c] = jnp.max(s, axis=0, keepdims=True).astype(F32)

    def accumulate(j, qs, h):
        c = qs * H + h
        m_old = m_ref[c]
        m_new = jnp.maximum(m_old, cm_ref[c])
        alpha = jnp.exp2(m_old - m_new)
        p = jnp.exp2(s_ref[c] - (m_new - sel_row(j, qs, h)).astype(BF16))
        v_aug = jnp.concatenate([v_ref[0, j, heads[h], :], ones_blk], axis=0)
        rs = slice(c * ACC_ROWS, (c + 1) * ACC_ROWS)
        acc_ref[rs, :] = alpha * acc_ref[rs, :] + _dot(v_aug, p)
        m_ref[c] = m_new

    first = G * step

    @pl.when(step == 0)
    def _():
        for qs in range(G):
            for h in range(H):
                own_scores(qs, q_ref[0, qs], qs * H + h, h)

    for n in range(G - 1):
        for qs in range(G - 1 - n, G):
            for h in range(H):
                accumulate(first + G - 1 - n, qs, h)
                scores(first + G - 2 - n, qs, h)

    def body(j, carry):
        for qs in range(G):
            for h in range(H):
                accumulate(j, qs, h)
                scores(j - 1, qs, h)
        return carry

    def body_unrolled(u, carry):
        for e in range(MOBA_UNROLL):
            carry = body(first - MOBA_UNROLL * u - e, carry)
        return carry

    lax.fori_loop(0, (G // MOBA_UNROLL) * step, body_unrolled, 0)

    @pl.when(step < last)
    def _():
        for h in range(H):
            for qs in range(G):
                accumulate(0, qs, h)
                own_scores(first + G + qs, qn_ref[0, qs], qs * H + h, h)

    @pl.when(step == last)
    def _():
        for h in range(H):
            for qs in range(G):
                accumulate(0, qs, h)

    for qs in range(G):
        outs = []
        for h in range(H):
            base = (qs * H + h) * ACC_ROWS
            outs.append(acc_ref[base:base + HEAD_DIM, :] / acc_ref[base + HEAD_DIM:base + HEAD_DIM + 1, :])
        o_ref[0, qs * MB:(qs + 1) * MB, :] = jnp.concatenate(outs, axis=0).T.astype(o_ref.dtype)


def _moba(qb_t, kb3, vb_t, sel, rows):
    batch, nblk = qb_t.shape[0], qb_t.shape[1]
    seq = nblk * MOBA_BLOCK
    g = MOBA_Q_PER_STEP
    assert g % MOBA_UNROLL == 0 and nblk % g == 0
    chains = g * MOBA_HEADS
    return pl.pallas_call(
        _moba_kernel,
        grid=(batch, nblk // g),
        in_specs=[
            pl.BlockSpec((1, g, MOBA_W, MOBA_BLOCK), lambda b, i: (b, i, 0, 0)),
            pl.BlockSpec((1, g, MOBA_W, MOBA_BLOCK), lambda b, i: (b, jnp.minimum(i + 1, nblk // g - 1), 0, 0)),
            pl.BlockSpec((1, seq, MOBA_W), lambda b, i: (b, 0, 0)),
            pl.BlockSpec((1, nblk, MOBA_W, MOBA_BLOCK), lambda b, i: (b, 0, 0, 0)),
            pl.BlockSpec((1, MOBA_HEADS, nblk, g * MOBA_BLOCK), lambda b, i: (b, 0, 0, i)),
            _resident(rows.shape, lambda b, i: (0, 0, 0)),
        ],
        out_specs=pl.BlockSpec((1, g * MOBA_BLOCK, MOBA_W), lambda b, i: (b, i, 0)),
        out_shape=jax.ShapeDtypeStruct((batch, seq, MOBA_W), BF16),
        scratch_shapes=[pltpu.VMEM((MOBA_HEADS, BIAS_SLOTS, MOBA_BLOCK, MOBA_BLOCK), BF16),
                        pltpu.VMEM((chains, 1, MOBA_BLOCK), F32),
                        pltpu.VMEM((chains * ACC_ROWS, MOBA_BLOCK), F32),
                        pltpu.VMEM((chains, MOBA_BLOCK, MOBA_BLOCK), BF16),
                        pltpu.VMEM((chains, 1, MOBA_BLOCK), F32)],
        compiler_params=pltpu.CompilerParams(dimension_semantics=("arbitrary", "arbitrary"),
                                             vmem_limit_bytes=VMEM_LIMIT_BYTES),
        name="moba",
    )(qb_t, qb_t, kb3, vb_t, sel, rows)


def _post_kernel(x_ref, ya_ref, yb_ref, ga_ref, gb_ref, gate1_ref, shift2_ref, scale2_ref,
                 gate2_ref, g2_ref, gf_ref, wa_ref, wb_ref, wo_ref, w1_ref, w2_ref, o_ref):
    a = _dot(ya_ref[...], wa_ref[...])
    b = _dot(yb_ref[...], wb_ref[...])
    merged = (jax.nn.sigmoid(ga_ref[...].astype(F32)) * a
              + jax.nn.sigmoid(gb_ref[...].astype(F32)) * b)
    x1 = x_ref[...] + gate1_ref[0, 0] * _dot(merged.astype(BF16), wo_ref[...])
    h2 = _rms_modulate(x1, g2_ref[...], shift2_ref[0, 0], scale2_ref[0, 0]).astype(BF16)
    y = jnp.zeros_like(x1)
    for c in range(D_FF // FF_CHUNK):
        cs = slice(c * FF_CHUNK, (c + 1) * FF_CHUNK)
        u = jnp.square(jnp.maximum(_dot(h2, w1_ref[:, cs]), 0.0)).astype(BF16)
        y = y + _dot(u, w2_ref[cs, :])
    x2 = x1 + gate2_ref[0, 0] * y
    ms = jnp.mean(x2 * x2, axis=-1, keepdims=True)
    o_ref[...] = (x2 * lax.rsqrt(ms + RMS_EPS)) * gf_ref[...]


def _post(x2, ya, yb, ga, gb, mod4, g2, gf, wa, wb, wo, w1, w2, seq):
    tokens = x2.shape[0]
    tiles_per_seq = seq // TOKEN_TILE
    tok = lambda w: pl.BlockSpec((TOKEN_TILE, w), lambda i: (i, 0))
    modrow = lambda k: pl.BlockSpec((1, 1, 1, D_MODEL), lambda i: (i // tiles_per_seq, k, 0, 0))
    full = lambda a: _resident(a.shape, lambda i: (0, 0))
    return pl.pallas_call(
        _post_kernel,
        grid=(tokens // TOKEN_TILE,),
        in_specs=[tok(D_MODEL), tok(SWA_Q_W), tok(MOBA_W), tok(D_MODEL), tok(D_MODEL),
                  modrow(2), modrow(3), modrow(4), modrow(5),
                  full(g2), full(gf), full(wa), full(wb), full(wo), full(w1), full(w2)],
        out_specs=tok(D_MODEL),
        out_shape=jax.ShapeDtypeStruct((tokens, D_MODEL), F32),
        compiler_params=pltpu.CompilerParams(dimension_semantics=("arbitrary",),
                                             vmem_limit_bytes=VMEM_LIMIT_BYTES),
        name="post",
    )(x2, ya, yb, ga, gb, mod4, mod4, mod4, mod4, g2, gf, wa, wb, wo, w1, w2)


def kernel(x, c, ada_w, ada_b, norm1_g, norm2_g, w_in, attn_sinks, rel_bias, w_branch_a,
           w_branch_b, w_out, w_mlp_in, w_mlp_out, final_g):
    batch, seq, _ = x.shape
    depth = ada_w.shape[0]
    assert seq % TOKEN_TILE == 0 and TOKEN_TILE % MOBA_BLOCK == 0 and batch <= SUBLANES
    far = np.arange((BIAS_SLOTS - 1) * MOBA_BLOCK - (MOBA_BLOCK - 1), max(seq, BIAS_SLOTS * MOBA_BLOCK))
    assert np.all(_t5_bucket(far) == _t5_bucket(far[:1]))
    assert 2 * SWA_BLOCK <= MOBA_BLOCK and SWA_WINDOW == SWA_BLOCK
    assert depth == 1
    l = 0


    x2 = x.reshape(batch * seq, D_MODEL)
    mod4, rows, sink_rows = _ada(c.astype(F32), ada_w[l], ada_b[l][None, :], rel_bias, attn_sinks[l])
    later_weights = [w.astype(F32) for w in (w_branch_a[l], w_branch_b[l], w_out[l], w_mlp_in[l], w_mlp_out[l])]
    ka, kb, ga, gb, qa_t, va_t, qb_t, vb_t, sel, wa, wb, wo, w1, w2 = _proj(
        x2, mod4, norm1_g[l][None, :], w_in[l].astype(F32), later_weights, batch, seq)

    ya = _swa(qa_t, ka.reshape(batch, seq, SWA_KV_W), va_t, rows, sink_rows)

    yb = _moba(qb_t, kb.reshape(batch, seq, MOBA_W), vb_t, sel, rows)

    out = _post(x2, ya.reshape(batch * seq, SWA_Q_W), yb.reshape(batch * seq, MOBA_W), ga, gb,
                mod4, norm2_g[l][None, :], final_g[None, :], wa, wb, wo, w1, w2, seq)
    return out.reshape(batch, seq, D_MODEL)
```

```python
import math

import jax
import jax.numpy as jnp
import numpy as np
from jax import lax
from jax.experimental import pallas as pl
from jax.experimental.pallas import tpu as pltpu

D_MODEL = 1024
HEAD_DIM = 64
ATTN_SCALE = HEAD_DIM ** -0.5
LOG2E = math.log2(math.e)
SWA_Q_HEADS = 8
SWA_KV_HEADS = 2
SWA_GROUP = SWA_Q_HEADS // SWA_KV_HEADS
SWA_WINDOW = 128
SWA_BLOCK = 128
MOBA_HEADS = 8
MOBA_BLOCK = 256
MOBA_TOPK = 3
NUM_BUCKETS = 32
MAX_EXACT = NUM_BUCKETS // 2
MAX_DISTANCE = 2048
N_ATTN_HEADS = SWA_Q_HEADS + MOBA_HEADS
SWA_Q_W = SWA_Q_HEADS * HEAD_DIM
SWA_KV_W = SWA_KV_HEADS * HEAD_DIM
MOBA_W = MOBA_HEADS * HEAD_DIM
D_FF = 4 * D_MODEL
N_MOD = 6
RMS_EPS = 1e-6

VMEM_LIMIT_BYTES = 56 * 1024 * 1024
TOKEN_TILE = 512
FF_CHUNK = 1024
SUBLANES = 8
BIAS_SLOTS = 8
SWA_UNROLL = 10
MOBA_Q_PER_STEP = 4
MOBA_UNROLL = 4
ACC_ROWS = HEAD_DIM + 16
NEG_INF = float("-inf")

F32 = jnp.float32
BF16 = jnp.bfloat16


def _resident(block_shape, index_map):
    return pl.BlockSpec(block_shape, index_map, pipeline_mode=pl.Buffered(1))


def _dot(a, b):
    return jnp.dot(a, b, preferred_element_type=F32)


def _dot_nt(a, b):
    return lax.dot_general(a, b, (((1,), (1,)), ((), ())), preferred_element_type=F32)


def _rms_modulate(xv, g, shift, scale):
    ms = jnp.mean(xv * xv, axis=-1, keepdims=True)
    y = xv * lax.rsqrt(ms + RMS_EPS)
    return (y * g) * (1.0 + scale) + shift


def _toeplitz(u, rows):
    wide = jnp.broadcast_to(u, (rows, 2 * rows))
    return pltpu.roll(wide, 0, 1, stride=1, stride_axis=0)[:, rows:]


def _t5_bucket(dist):
    n = np.maximum(dist, 0)
    nf = np.maximum(n, 1).astype(np.float32)
    large = MAX_EXACT + (np.log(nf / np.float32(MAX_EXACT)) / np.float32(math.log(MAX_DISTANCE / MAX_EXACT))
                         * np.float32(NUM_BUCKETS - MAX_EXACT)).astype(np.int32)
    large = np.minimum(large, NUM_BUCKETS - 1)
    return np.where(n < MAX_EXACT, n, large).astype(np.int32)


def _bias_rows(bucket_ref, rbt_ref, o_ref):
    bucket = bucket_ref[...]
    n = bucket.shape[1]
    onehot = jnp.where(lax.broadcasted_iota(jnp.int32, (NUM_BUCKETS, n), 0) == bucket, 1.0, 0.0)
    rows = jnp.dot(rbt_ref[...], onehot, preferred_element_type=F32,
                   precision=lax.Precision.HIGHEST)
    rows = jnp.where(bucket >= 0, rows * LOG2E, NEG_INF)
    for d in range(BIAS_SLOTS):
        o_ref[d] = rows[:, d * MOBA_BLOCK:(d + 2) * MOBA_BLOCK]


def _ada_kernel(c_ref, w_hbm, b_ref, bucket_ref, rbt_ref, sinks_ref, o_ref, rows_ref, sink_rows_ref,
                cs_ref, w_buf, sems):
    batch = c_ref.shape[0]

    def w_copy(j):
        return pltpu.make_async_copy(w_hbm.at[:, j * D_MODEL:(j + 1) * D_MODEL], w_buf.at[j], sems.at[j])

    for j in range(N_MOD):
        w_copy(j).start()

    cs_ref[...] = jnp.zeros(cs_ref.shape, F32)
    cs_ref[0:batch, :] = jax.nn.silu(c_ref[...])
    cs = cs_ref[...].astype(BF16)

    _bias_rows(bucket_ref, rbt_ref, rows_ref)
    sinks = sinks_ref[...] * LOG2E
    for h in range(SWA_Q_HEADS):
        g, gi = divmod(h, SWA_GROUP)
        sink_rows_ref[g, :, gi * SWA_BLOCK:(gi + 1) * SWA_BLOCK] = jnp.broadcast_to(
            sinks[:, h:h + 1], (1, SWA_BLOCK))

    for j in range(N_MOD):
        w_copy(j).wait()
        res = _dot(cs, w_buf[j].astype(BF16)) + b_ref[:, j * D_MODEL:(j + 1) * D_MODEL]
        for b in range(batch):
            o_ref[b, j] = res[b:b + 1]


def _ada(c, w, b, rel_bias, sinks):
    batch = c.shape[0]
    assert w.shape[1] == N_MOD * D_MODEL
    n = (BIAS_SLOTS + 1) * MOBA_BLOCK
    dist = np.arange(n, dtype=np.int32) - MOBA_BLOCK
    bucket = jnp.asarray(np.where(dist >= 0, _t5_bucket(dist), -1)[None, :], dtype=jnp.int32)
    rows_shape = (BIAS_SLOTS, N_ATTN_HEADS, 2 * MOBA_BLOCK)
    sink_shape = (SWA_KV_HEADS, 1, SWA_GROUP * SWA_BLOCK)
    return pl.pallas_call(
        _ada_kernel,
        in_specs=[pl.BlockSpec(memory_space=pltpu.VMEM),
                  pl.BlockSpec(memory_space=pl.ANY),
                  pl.BlockSpec(memory_space=pltpu.VMEM),
                  pl.BlockSpec(memory_space=pltpu.VMEM),
                  pl.BlockSpec(memory_space=pltpu.VMEM),
                  pl.BlockSpec(memory_space=pltpu.VMEM)],
        out_specs=(pl.BlockSpec(memory_space=pltpu.VMEM),
                   pl.BlockSpec(memory_space=pltpu.VMEM),
                   pl.BlockSpec(memory_space=pltpu.VMEM)),
        out_shape=(jax.ShapeDtypeStruct((batch, N_MOD, 1, D_MODEL), F32),
                   jax.ShapeDtypeStruct(rows_shape, F32),
                   jax.ShapeDtypeStruct(sink_shape, F32)),
        scratch_shapes=[pltpu.VMEM((SUBLANES, D_MODEL), F32),
                        pltpu.VMEM((N_MOD, D_MODEL, D_MODEL), F32),
                        pltpu.SemaphoreType.DMA((N_MOD,))],
        compiler_params=pltpu.CompilerParams(vmem_limit_bytes=VMEM_LIMIT_BYTES),
        name="ada",
    )(c, w, b, bucket, rel_bias.astype(F32).T, sinks.astype(F32)[None, :])


IN_WIDTHS = (SWA_Q_W, SWA_KV_W, SWA_KV_W, MOBA_W, MOBA_W, MOBA_W, D_MODEL, D_MODEL)
IN_OFFS = tuple(sum(IN_WIDTHS[:k]) for k in range(len(IN_WIDTHS) + 1))
FEAT_COLS = (0, 2, 3, 5)
FEAT_W = sum(IN_WIDTHS[k] for k in FEAT_COLS)
TOK_COLS = (1, 4, 6, 7)
TOK_W = sum(IN_WIDTHS[k] for k in TOK_COLS)
Q_COLS = (0, 3)
BLOCKS_PER_TILE = TOKEN_TILE // MOBA_BLOCK


def _proj_kernel(x_ref, shift_ref, scale_ref, g_ref, w_ref, *refs):
    n_cast = (len(refs) - 12) // 2
    cast_in, refs = refs[:n_cast], refs[n_cast:]
    ka_ref, kb_ref, ga_ref, gb_ref, qa_ref, va_ref, qb_ref, vb_ref, sel_ref = refs[:9]
    cast_out, (wk_ref, wt_ref, kmean_ref) = refs[9:9 + n_cast], refs[9 + n_cast:]

    for src, dst in zip(cast_in, cast_out):
        dst[...] = src[...].astype(dst.dtype)

    tile = pl.program_id(0) % (kmean_ref.shape[0] // BLOCKS_PER_TILE)

    @pl.when(tile == 0)
    def _():
        kmean_ref[...] = jnp.zeros(kmean_ref.shape, F32)

    @pl.when(pl.program_id(0) == 0)
    def _():
        o = 0
        for k in TOK_COLS:
            wk_ref[:, o:o + IN_WIDTHS[k]] = w_ref[:, IN_OFFS[k]:IN_OFFS[k + 1]].astype(BF16)
            o += IN_WIDTHS[k]
        o = 0
        for k in FEAT_COLS:
            wcol = w_ref[:, IN_OFFS[k]:IN_OFFS[k + 1]]
            if k in Q_COLS:
                wcol = wcol * (ATTN_SCALE * LOG2E)
            wt_ref[o:o + IN_WIDTHS[k], :] = wcol.T.astype(BF16)
            o += IN_WIDTHS[k]

    h = _rms_modulate(x_ref[...], g_ref[...], shift_ref[0, 0], scale_ref[0, 0]).astype(BF16)
    tok_refs = dict(zip(TOK_COLS, (ka_ref, kb_ref, ga_ref, gb_ref)))
    feat_refs = dict(zip(FEAT_COLS, (qa_ref, va_ref, qb_ref, vb_ref)))

    def token_major(k):
        o = sum(IN_WIDTHS[c] for c in TOK_COLS[:TOK_COLS.index(k)])
        out = _dot(h, wk_ref[:, o:o + IN_WIDTHS[k]])
        tok_refs[k][...] = out.astype(BF16)
        return out

    def feature_major(k):
        o = sum(IN_WIDTHS[c] for c in FEAT_COLS[:FEAT_COLS.index(k)])
        out = _dot_nt(wt_ref[o:o + IN_WIDTHS[k], :], h).astype(BF16)
        blk = feat_refs[k].shape[-1]
        for t in range(TOKEN_TILE // blk):
            feat_refs[k][0, t] = out[:, t * blk:(t + 1) * blk]
        return out

    _update_kmean(token_major(4), kmean_ref, tile)
    qb_t = feature_major(3)
    token_major(1)
    token_major(6)
    _moba_select(qb_t, sel_ref, kmean_ref, tile)
    token_major(7)
    for k in (0, 2, 5):
        feature_major(k)


def _update_kmean(kb, kmean_ref, tile):
    for t in range(BLOCKS_PER_TILE):
        kmean_ref[pl.ds(tile * BLOCKS_PER_TILE + t, 1), :] = jnp.mean(
            kb[t * MOBA_BLOCK:(t + 1) * MOBA_BLOCK, :], axis=0, keepdims=True)


def _moba_select(qb_t, sel_ref, kmean_ref, tile):
    nblk = kmean_ref.shape[0]
    blocks_per_tile = BLOCKS_PER_TILE
    blk = lax.broadcasted_iota(jnp.int32, (nblk, TOKEN_TILE), 0)
    qblk = tile * blocks_per_tile + lax.broadcasted_iota(jnp.int32, (nblk, TOKEN_TILE), 1) // MOBA_BLOCK
    past = blk < qblk
    own = jnp.where(blk == qblk, 0.0, NEG_INF)
    blk_f = blk.astype(F32)
    for h in range(MOBA_HEADS):
        hs = slice(h * HEAD_DIM, (h + 1) * HEAD_DIM)
        kmean = kmean_ref[:, hs]
        k1 = kmean.astype(BF16)
        r1 = kmean - k1.astype(F32)
        k2 = r1.astype(BF16)
        k3 = (r1 - k2.astype(F32)).astype(BF16)
        parts = _dot(jnp.concatenate([k1, k2, k3], axis=0), qb_t[hs, :])
        gate = parts[0:nblk] + parts[nblk:2 * nblk] + parts[2 * nblk:3 * nblk]
        gate = jnp.where(past, gate, NEG_INF)
        mask = own
        for _ in range(MOBA_TOPK):
            top = jnp.max(gate, axis=0, keepdims=True)
            first = jnp.min(jnp.where(gate == top, blk_f, float(nblk)), axis=0, keepdims=True)
            pick = blk_f == first
            mask = jnp.where(jnp.logical_and(pick, past), 0.0, mask)
            gate = jnp.where(pick, NEG_INF, gate)
        sel_ref[0, h] = mask


def _proj(x2, mod4, g1, w_in, cast_weights, batch, seq):
    tokens = x2.shape[0]
    steps = tokens // TOKEN_TILE
    assert all(w.shape[0] % (16 * steps) == 0 for w in cast_weights)
    cast_specs = [pl.BlockSpec((w.shape[0] // steps, w.shape[1]), lambda i: (i, 0)) for w in cast_weights]
    tiles_per_seq = seq // TOKEN_TILE
    na = TOKEN_TILE // SWA_BLOCK
    nb = TOKEN_TILE // MOBA_BLOCK

    def feat_spec(width, blk, per_tile):
        return pl.BlockSpec((1, per_tile, width, blk),
                            lambda i: (i // tiles_per_seq, i % tiles_per_seq, 0, 0))

    out_shape = (
        jax.ShapeDtypeStruct((tokens, SWA_KV_W), BF16),
        jax.ShapeDtypeStruct((tokens, MOBA_W), BF16),
        jax.ShapeDtypeStruct((tokens, D_MODEL), BF16),
        jax.ShapeDtypeStruct((tokens, D_MODEL), BF16),
        jax.ShapeDtypeStruct((batch, seq // SWA_BLOCK, SWA_Q_W, SWA_BLOCK), BF16),
        jax.ShapeDtypeStruct((batch, seq // SWA_BLOCK, SWA_KV_W, SWA_BLOCK), BF16),
        jax.ShapeDtypeStruct((batch, seq // MOBA_BLOCK, MOBA_W, MOBA_BLOCK), BF16),
        jax.ShapeDtypeStruct((batch, seq // MOBA_BLOCK, MOBA_W, MOBA_BLOCK), BF16),
        jax.ShapeDtypeStruct((batch, MOBA_HEADS, seq // MOBA_BLOCK, seq), F32),
    )
    tok_spec = lambda w: pl.BlockSpec((TOKEN_TILE, w), lambda i: (i, 0))
    return pl.pallas_call(
        _proj_kernel,
        grid=(steps,),
        in_specs=[
            pl.BlockSpec((TOKEN_TILE, D_MODEL), lambda i: (i, 0)),
            pl.BlockSpec((1, 1, 1, D_MODEL), lambda i: (i // tiles_per_seq, 0, 0, 0)),
            pl.BlockSpec((1, 1, 1, D_MODEL), lambda i: (i // tiles_per_seq, 1, 0, 0)),
            _resident((1, D_MODEL), lambda i: (0, 0)),
            _resident(w_in.shape, lambda i: (0, 0)),
        ] + cast_specs,
        out_specs=(tok_spec(SWA_KV_W), tok_spec(MOBA_W), tok_spec(D_MODEL), tok_spec(D_MODEL),
                   feat_spec(SWA_Q_W, SWA_BLOCK, na), feat_spec(SWA_KV_W, SWA_BLOCK, na),
                   feat_spec(MOBA_W, MOBA_BLOCK, nb), feat_spec(MOBA_W, MOBA_BLOCK, nb),
                   pl.BlockSpec((1, MOBA_HEADS, seq // MOBA_BLOCK, TOKEN_TILE),
                                lambda i: (i // tiles_per_seq, 0, 0, i % tiles_per_seq)))
        + tuple(cast_specs),
        out_shape=out_shape + tuple(jax.ShapeDtypeStruct(w.shape, BF16) for w in cast_weights),
        scratch_shapes=[pltpu.VMEM((D_MODEL, TOK_W), BF16), pltpu.VMEM((FEAT_W, D_MODEL), BF16),
                        pltpu.VMEM((seq // MOBA_BLOCK, MOBA_W), F32)],
        compiler_params=pltpu.CompilerParams(dimension_semantics=("arbitrary",),
                                             vmem_limit_bytes=VMEM_LIMIT_BYTES),
        name="proj",
    )(x2, mod4, mod4, g1, w_in, *cast_weights)


def _swa_kernel(q_ref, k_ref, v_ref, rows_ref, sink_ref, o_ref, bias_ref, ot_ref, s_ref, cm_ref):
    L = SWA_BLOCK
    nblk = q_ref.shape[1]

    @pl.when(pl.program_id(0) == 0)
    def _():
        k_idx = lax.broadcasted_iota(jnp.int32, (L, L), 0)
        q_idx = lax.broadcasted_iota(jnp.int32, (L, L), 1)
        for h in range(SWA_Q_HEADS):
            g, gi = divmod(h, SWA_GROUP)
            t0 = rows_ref[0, h:h + 1, MOBA_BLOCK:MOBA_BLOCK + L]
            r = _toeplitz(jnp.concatenate([t0, t0], axis=1), L)
            cols = slice(gi * L, (gi + 1) * L)
            bias_ref[g, 0:L, cols] = jnp.where(q_idx < k_idx, r, NEG_INF).astype(BF16)
            bias_ref[g, L:2 * L, cols] = jnp.where(q_idx >= k_idx, r, NEG_INF).astype(BF16)

    ones_blk = jnp.concatenate([jnp.ones((1, 2 * L), BF16), jnp.zeros((ACC_ROWS - HEAD_DIM - 1, 2 * L), BF16)],
                               axis=0)
    PW = 2 * L
    chains = [(g, hp) for g in range(SWA_KV_HEADS) for hp in range(SWA_GROUP // 2)]
    kv_cols = [slice(g * HEAD_DIM, (g + 1) * HEAD_DIM) for g in range(SWA_KV_HEADS)]

    def q_pair(c, g, hp):
        h0 = g * SWA_GROUP + 2 * hp
        return jnp.concatenate([q_ref[0, c, (h0 + e) * HEAD_DIM:(h0 + e + 1) * HEAD_DIM, :] for e in range(2)],
                               axis=1)

    def scores_first(idx):
        g, hp = chains[idx]
        lanes = slice(hp * PW, (hp + 1) * PW)
        s_ref[idx, 0:L, :] = jnp.full((L, PW), NEG_INF, BF16)
        s = _dot(k_ref[0, 0:L, kv_cols[g]], q_pair(0, g, hp)).astype(BF16) + bias_ref[g, L:2 * L, lanes]
        s_ref[idx, L:2 * L, :] = s
        cm_ref[idx] = jnp.max(s, axis=0, keepdims=True).astype(F32)

    def scores(c, idx):
        g, hp = chains[idx]
        kw = k_ref[0, pl.ds(pl.multiple_of((c - 1) * L, L), 2 * L), kv_cols[g]]
        s = _dot(kw, q_pair(c, g, hp)).astype(BF16) + bias_ref[g, :, hp * PW:(hp + 1) * PW]
        s_ref[idx] = s
        cm_ref[idx] = jnp.max(s, axis=0, keepdims=True).astype(F32)

    def attend(c, c_prev, idx, slot):
        g, hp = chains[idx]
        hs = kv_cols[g]
        s = s_ref[idx]
        sink = sink_ref[g, :, hp * PW:(hp + 1) * PW]
        m_b = jnp.maximum(cm_ref[idx], sink).astype(BF16)
        m = m_b.astype(F32)
        p = jnp.exp2(s - m_b)
        v_win = jnp.concatenate([v_ref[0, c_prev, hs, :], v_ref[0, c, hs, :]], axis=1)
        pv = _dot(jnp.concatenate([v_win, ones_blk], axis=0), p)
        o = pv[0:HEAD_DIM] / (pv[HEAD_DIM:HEAD_DIM + 1] + jnp.exp2(sink - m))
        for e in range(2):
            r = (g * SWA_GROUP + 2 * hp + e) * HEAD_DIM
            ot_ref[slot, r:r + HEAD_DIM, :] = o[:, e * L:(e + 1) * L]

    def emit(c, slot):
        o_ref[0, pl.ds(pl.multiple_of(c * L, L), L), :] = ot_ref[slot].T.astype(o_ref.dtype)

    for idx in range(len(chains)):
        scores_first(idx)
    U = SWA_UNROLL
    for idx in range(len(chains)):
        attend(0, 0, idx, U - 1)
        scores(1, idx)

    def body(c, e):
        emit(c - 1, (e - 1) % U)
        for idx in range(len(chains)):
            attend(c, c - 1, idx, e)
            scores(c + 1, idx)

    def body_unrolled(u, carry):
        for e in range(U):
            body(U * u + 1 + e, e)
        return carry

    lax.fori_loop(0, (nblk - 2) // U, body_unrolled, 0)
    emit(nblk - 2, U - 1)
    for idx in range(len(chains)):
        attend(nblk - 1, nblk - 2, idx, 0)
    emit(nblk - 1, 0)


def _swa(qa_t, ka3, va_t, rows, sink_rows):
    batch, nblk = qa_t.shape[0], qa_t.shape[1]
    seq = nblk * SWA_BLOCK
    assert nblk >= 2 and (nblk - 2) % SWA_UNROLL == 0 and SWA_UNROLL >= 2
    return pl.pallas_call(
        _swa_kernel,
        grid=(batch,),
        in_specs=[
            pl.BlockSpec((1, nblk, SWA_Q_W, SWA_BLOCK), lambda b: (b, 0, 0, 0)),
            pl.BlockSpec((1, seq, SWA_KV_W), lambda b: (b, 0, 0)),
            pl.BlockSpec((1, nblk, SWA_KV_W, SWA_BLOCK), lambda b: (b, 0, 0, 0)),
            _resident((1, N_ATTN_HEADS, 2 * MOBA_BLOCK), lambda b: (0, 0, 0)),
            _resident(sink_rows.shape, lambda b: (0, 0, 0)),
        ],
        out_specs=pl.BlockSpec((1, seq, SWA_Q_W), lambda b: (b, 0, 0)),
        out_shape=jax.ShapeDtypeStruct((batch, seq, SWA_Q_W), BF16),
        scratch_shapes=[pltpu.VMEM((SWA_KV_HEADS, 2 * SWA_BLOCK, SWA_GROUP * SWA_BLOCK), BF16),
                        pltpu.VMEM((SWA_UNROLL, SWA_Q_W, SWA_BLOCK), F32),
                        pltpu.VMEM((SWA_Q_HEADS // 2, 2 * SWA_BLOCK, 2 * SWA_BLOCK), BF16),
                        pltpu.VMEM((SWA_Q_HEADS // 2, 1, 2 * SWA_BLOCK), F32)],
        compiler_params=pltpu.CompilerParams(dimension_semantics=("arbitrary",),
                                             vmem_limit_bytes=VMEM_LIMIT_BYTES),
        name="swa",
    )(qa_t, ka3, va_t, rows, sink_rows)


def _moba_kernel(q_ref, qn_ref, k_ref, v_ref, sel_ref, rows_ref, o_ref, bias_ref, m_ref, acc_ref, s_ref, cm_ref):
    step = pl.program_id(1)
    last = pl.num_programs(1) - 1
    MB = MOBA_BLOCK
    H = MOBA_HEADS
    G = MOBA_Q_PER_STEP

    @pl.when(jnp.logical_and(pl.program_id(0) == 0, step == 0))
    def _():
        for h in range(H):
            for d in range(BIAS_SLOTS):
                bias_ref[h, d] = _toeplitz(rows_ref[d, SWA_Q_HEADS + h:SWA_Q_HEADS + h + 1, :], MB).astype(BF16)

    m_ref[...] = jnp.full(m_ref.shape, NEG_INF, F32)
    acc_ref[...] = jnp.zeros(acc_ref.shape, F32)
    heads = [slice(h * HEAD_DIM, (h + 1) * HEAD_DIM) for h in range(H)]
    ones_blk = jnp.concatenate([jnp.ones((1, MB), BF16), jnp.zeros((ACC_ROWS - HEAD_DIM - 1, MB), BF16)],
                               axis=0)

    def sel_row(j, qs, h):
        return sel_ref[0, h, pl.ds(j, 1), qs * MB:(qs + 1) * MB]

    def scores(j, qs, h):
        c = qs * H + h
        rows = pl.ds(pl.multiple_of(j * MB, MB), MB)
        slot = jnp.minimum(G * step + qs - j, BIAS_SLOTS - 1)
        s = _dot(k_ref[0, rows, heads[h]], q_ref[0, qs, heads[h], :]).astype(BF16) + bias_ref[h, slot]
        s_ref[c] = s
        cm_ref[c] = jnp.max(s, axis=0, keepdims=True).astype(F32) + sel_row(j, qs, h)

    def own_scores(blk, q_blk, c, h):
        rows = pl.ds(pl.multiple_of(blk * MB, MB), MB)
        s = _dot(k_ref[0, rows, heads[h]], q_blk[heads[h], :]).astype(BF16) + bias_ref[h, 0]
        s_ref[c] = s
        cm_ref[c] = jnp.max(s, axis=0, keepdims=True).astype(F32)

    def accumulate(j, qs, h):
        c = qs * H + h
        m_old = m_ref[c]
        m_new = jnp.maximum(m_old, cm_ref[c])
        alpha = jnp.exp2(m_old - m_new)
        p = jnp.exp2(s_ref[c] - (m_new - sel_row(j, qs, h)).astype(BF16))
        v_aug = jnp.concatenate([v_ref[0, j, heads[h], :], ones_blk], axis=0)
        rs = slice(c * ACC_ROWS, (c + 1) * ACC_ROWS)
        acc_ref[rs, :] = alpha * acc_ref[rs, :] + _dot(v_aug, p)
        m_ref[c] = m_new

    first = G * step

    @pl.when(step == 0)
    def _():
        for qs in range(G):
            for h in range(H):
                own_scores(qs, q_ref[0, qs], qs * H + h, h)

    for n in range(G - 1):
        for qs in range(G - 1 - n, G):
            for h in range(H):
                accumulate(first + G - 1 - n, qs, h)
                scores(first + G - 2 - n, qs, h)

    def body(j, carry):
        for qs in range(G):
            for h in range(H):
                accumulate(j, qs, h)
                scores(j - 1, qs, h)
        return carry

    def body_unrolled(u, carry):
        for e in range(MOBA_UNROLL):
            carry = body(first - MOBA_UNROLL * u - e, carry)
        return carry

    lax.fori_loop(0, (G // MOBA_UNROLL) * step, body_unrolled, 0)

    @pl.when(step < last)
    def _():
        for h in range(H):
            for qs in range(G):
                accumulate(0, qs, h)
                own_scores(first + G + qs, qn_ref[0, qs], qs * H + h, h)

    @pl.when(step == last)
    def _():
        for h in range(H):
            for qs in range(G):
                accumulate(0, qs, h)

    for qs in range(G):
        outs = []
        for h in range(H):
            base = (qs * H + h) * ACC_ROWS
            outs.append(acc_ref[base:base + HEAD_DIM, :] / acc_ref[base + HEAD_DIM:base + HEAD_DIM + 1, :])
        o_ref[0, qs * MB:(qs + 1) * MB, :] = jnp.concatenate(outs, axis=0).T.astype(o_ref.dtype)


def _moba(qb_t, kb3, vb_t, sel, rows):
    batch, nblk = qb_t.shape[0], qb_t.shape[1]
    seq = nblk * MOBA_BLOCK
    g = MOBA_Q_PER_STEP
    assert g % MOBA_UNROLL == 0 and nblk % g == 0
    chains = g * MOBA_HEADS
    return pl.pallas_call(
        _moba_kernel,
        grid=(batch, nblk // g),
        in_specs=[
            pl.BlockSpec((1, g, MOBA_W, MOBA_BLOCK), lambda b, i: (b, i, 0, 0)),
            pl.BlockSpec((1, g, MOBA_W, MOBA_BLOCK), lambda b, i: (b, jnp.minimum(i + 1, nblk // g - 1), 0, 0)),
            pl.BlockSpec((1, seq, MOBA_W), lambda b, i: (b, 0, 0)),
            pl.BlockSpec((1, nblk, MOBA_W, MOBA_BLOCK), lambda b, i: (b, 0, 0, 0)),
            pl.BlockSpec((1, MOBA_HEADS, nblk, g * MOBA_BLOCK), lambda b, i: (b, 0, 0, i)),
            _resident(rows.shape, lambda b, i: (0, 0, 0)),
        ],
        out_specs=pl.BlockSpec((1, g * MOBA_BLOCK, MOBA_W), lambda b, i: (b, i, 0)),
        out_shape=jax.ShapeDtypeStruct((batch, seq, MOBA_W), BF16),
        scratch_shapes=[pltpu.VMEM((MOBA_HEADS, BIAS_SLOTS, MOBA_BLOCK, MOBA_BLOCK), BF16),
                        pltpu.VMEM((chains, 1, MOBA_BLOCK), F32),
                        pltpu.VMEM((chains * ACC_ROWS, MOBA_BLOCK), F32),
                        pltpu.VMEM((chains, MOBA_BLOCK, MOBA_BLOCK), BF16),
                        pltpu.VMEM((chains, 1, MOBA_BLOCK), F32)],
        compiler_params=pltpu.CompilerParams(dimension_semantics=("arbitrary", "arbitrary"),
                                             vmem_limit_bytes=VMEM_LIMIT_BYTES),
        name="moba",
    )(qb_t, qb_t, kb3, vb_t, sel, rows)


def _post_kernel(x_ref, ya_ref, yb_ref, ga_ref, gb_ref, gate1_ref, shift2_ref, scale2_ref,
                 gate2_ref, g2_ref, gf_ref, wa_ref, wb_ref, wo_ref, w1_ref, w2_ref, o_ref):
    a = _dot(ya_ref[...], wa_ref[...])
    b = _dot(yb_ref[...], wb_ref[...])
    merged = (jax.nn.sigmoid(ga_ref[...].astype(F32)) * a
              + jax.nn.sigmoid(gb_ref[...].astype(F32)) * b)
    x1 = x_ref[...] + gate1_ref[0, 0] * _dot(merged.astype(BF16), wo_ref[...])
    h2 = _rms_modulate(x1, g2_ref[...], shift2_ref[0, 0], scale2_ref[0, 0]).astype(BF16)
    y = jnp.zeros_like(x1)
    for c in range(D_FF // FF_CHUNK):
        cs = slice(c * FF_CHUNK, (c + 1) * FF_CHUNK)
        u = jnp.square(jnp.maximum(_dot(h2, w1_ref[:, cs]), 0.0)).astype(BF16)
        y = y + _dot(u, w2_ref[cs, :])
    x2 = x1 + gate2_ref[0, 0] * y
    ms = jnp.mean(x2 * x2, axis=-1, keepdims=True)
    o_ref[...] = (x2 * lax.rsqrt(ms + RMS_EPS)) * gf_ref[...]


def _post(x2, ya, yb, ga, gb, mod4, g2, gf, wa, wb, wo, w1, w2, seq):
    tokens = x2.shape[0]
    tiles_per_seq = seq // TOKEN_TILE
    tok = lambda w: pl.BlockSpec((TOKEN_TILE, w), lambda i: (i, 0))
    modrow = lambda k: pl.BlockSpec((1, 1, 1, D_MODEL), lambda i: (i // tiles_per_seq, k, 0, 0))
    full = lambda a: _resident(a.shape, lambda i: (0, 0))
    return pl.pallas_call(
        _post_kernel,
        grid=(tokens // TOKEN_TILE,),
        in_specs=[tok(D_MODEL), tok(SWA_Q_W), tok(MOBA_W), tok(D_MODEL), tok(D_MODEL),
                  modrow(2), modrow(3), modrow(4), modrow(5),
                  full(g2), full(gf), full(wa), full(wb), full(wo), full(w1), full(w2)],
        out_specs=tok(D_MODEL),
        out_shape=jax.ShapeDtypeStruct((tokens, D_MODEL), F32),
        compiler_params=pltpu.CompilerParams(dimension_semantics=("arbitrary",),
                                             vmem_limit_bytes=VMEM_LIMIT_BYTES),
        name="post",
    )(x2, ya, yb, ga, gb, mod4, mod4, mod4, mod4, g2, gf, wa, wb, wo, w1, w2)


def kernel(x, c, ada_w, ada_b, norm1_g, norm2_g, w_in, attn_sinks, rel_bias, w_branch_a,
           w_branch_b, w_out, w_mlp_in, w_mlp_out, final_g):
    batch, seq, _ = x.shape
    depth = ada_w.shape[0]
    assert seq % TOKEN_TILE == 0 and TOKEN_TILE % MOBA_BLOCK == 0 and batch <= SUBLANES
    far = np.arange((BIAS_SLOTS - 1) * MOBA_BLOCK - (MOBA_BLOCK - 1), max(seq, BIAS_SLOTS * MOBA_BLOCK))
    assert np.all(_t5_bucket(far) == _t5_bucket(far[:1]))
    assert 2 * SWA_BLOCK <= MOBA_BLOCK and SWA_WINDOW == SWA_BLOCK
    assert depth == 1
    l = 0


    x2 = x.reshape(batch * seq, D_MODEL)
    mod4, rows, sink_rows = _ada(c.astype(F32), ada_w[l], ada_b[l][None, :], rel_bias, attn_sinks[l])
    later_weights = [w.astype(F32) for w in (w_branch_a[l], w_branch_b[l], w_out[l], w_mlp_in[l], w_mlp_out[l])]
    ka, kb, ga, gb, qa_t, va_t, qb_t, vb_t, sel, wa, wb, wo, w1, w2 = _proj(
        x2, mod4, norm1_g[l][None, :], w_in[l].astype(F32), later_weights, batch, seq)

    ya = _swa(qa_t, ka.reshape(batch, seq, SWA_KV_W), va_t, rows, sink_rows)

    yb = _moba(qb_t, kb.reshape(batch, seq, MOBA_W), vb_t, sel, rows)

    out = _post(x2, ya.reshape(batch * seq, SWA_Q_W), yb.reshape(batch * seq, MOBA_W), ga, gb,
                mod4, norm2_g[l][None, :], final_g[None, :], wa, wb, wo, w1, w2, seq)
    return out.reshape(batch, seq, D_MODEL)
```

```python
import math

import jax
import jax.numpy as jnp
import numpy as np
from jax import lax
from jax.experimental import pallas as pl
from jax.experimental.pallas import tpu as pltpu

D_MODEL = 1024
HEAD_DIM = 64
ATTN_SCALE = HEAD_DIM ** -0.5
LOG2E = math.log2(math.e)
SWA_Q_HEADS = 8
SWA_KV_HEADS = 2
SWA_GROUP = SWA_Q_HEADS // SWA_KV_HEADS
SWA_WINDOW = 128
SWA_BLOCK = 128
MOBA_HEADS = 8
MOBA_BLOCK = 256
MOBA_TOPK = 3
NUM_BUCKETS = 32
MAX_EXACT = NUM_BUCKETS // 2
MAX_DISTANCE = 2048
N_ATTN_HEADS = SWA_Q_HEADS + MOBA_HEADS
SWA_Q_W = SWA_Q_HEADS * HEAD_DIM
SWA_KV_W = SWA_KV_HEADS * HEAD_DIM
MOBA_W = MOBA_HEADS * HEAD_DIM
D_FF = 4 * D_MODEL
N_MOD = 6
RMS_EPS = 1e-6

VMEM_LIMIT_BYTES = 56 * 1024 * 1024
TOKEN_TILE = 512
FF_CHUNK = 1024
SUBLANES = 8
BIAS_SLOTS = 8
SWA_UNROLL = 10
MOBA_Q_PER_STEP = 4
MOBA_UNROLL = 4
ACC_ROWS = HEAD_DIM + 16
NEG_INF = float("-inf")

F32 = jnp.float32
BF16 = jnp.bfloat16


def _resident(block_shape, index_map):
    return pl.BlockSpec(block_shape, index_map, pipeline_mode=pl.Buffered(1))


def _dot(a, b):
    return jnp.dot(a, b, preferred_element_type=F32)


def _dot_nt(a, b):
    return lax.dot_general(a, b, (((1,), (1,)), ((), ())), preferred_element_type=F32)


def _rms_modulate(xv, g, shift, scale):
    ms = jnp.mean(xv * xv, axis=-1, keepdims=True)
    y = xv * lax.rsqrt(ms + RMS_EPS)
    return (y * g) * (1.0 + scale) + shift


def _toeplitz(u, rows):
    wide = jnp.broadcast_to(u, (rows, 2 * rows))
    return pltpu.roll(wide, 0, 1, stride=1, stride_axis=0)[:, rows:]


def _t5_bucket(dist):
    n = np.maximum(dist, 0)
    nf = np.maximum(n, 1).astype(np.float32)
    large = MAX_EXACT + (np.log(nf / np.float32(MAX_EXACT)) / np.float32(math.log(MAX_DISTANCE / MAX_EXACT))
                         * np.float32(NUM_BUCKETS - MAX_EXACT)).astype(np.int32)
    large = np.minimum(large, NUM_BUCKETS - 1)
    return np.where(n < MAX_EXACT, n, large).astype(np.int32)


def _bias_rows(bucket_ref, rbt_ref, o_ref):
    bucket = bucket_ref[...]
    n = bucket.shape[1]
    onehot = jnp.where(lax.broadcasted_iota(jnp.int32, (NUM_BUCKETS, n), 0) == bucket, 1.0, 0.0)
    rows = jnp.dot(rbt_ref[...], onehot, preferred_element_type=F32,
                   precision=lax.Precision.HIGHEST)
    rows = jnp.where(bucket >= 0, rows * LOG2E, NEG_INF)
    for d in range(BIAS_SLOTS):
        o_ref[d] = rows[:, d * MOBA_BLOCK:(d + 2) * MOBA_BLOCK]


def _ada_kernel(c_ref, w_ref, b_ref, bucket_ref, rbt_ref, sinks_ref, o_ref, rows_ref, sink_rows_ref, cs_ref):
    batch = c_ref.shape[0]
    cs_ref[...] = jnp.zeros(cs_ref.shape, F32)
    cs_ref[0:batch, :] = jax.nn.silu(c_ref[...])
    res = _dot(cs_ref[...].astype(BF16), w_ref[...].astype(BF16)) + b_ref[...]
    for b in range(batch):
        o_ref[b, 0] = res[b:b + 1]

    @pl.when(pl.program_id(0) == 0)
    def _():
        _bias_rows(bucket_ref, rbt_ref, rows_ref)
        sinks = sinks_ref[...] * LOG2E
        for h in range(SWA_Q_HEADS):
            g, gi = divmod(h, SWA_GROUP)
            sink_rows_ref[g, :, gi * SWA_BLOCK:(gi + 1) * SWA_BLOCK] = jnp.broadcast_to(
                sinks[:, h:h + 1], (1, SWA_BLOCK))


def _ada(c, w, b, rel_bias, sinks):
    batch = c.shape[0]
    assert w.shape[1] == N_MOD * D_MODEL
    n = (BIAS_SLOTS + 1) * MOBA_BLOCK
    dist = np.arange(n, dtype=np.int32) - MOBA_BLOCK
    bucket = jnp.asarray(np.where(dist >= 0, _t5_bucket(dist), -1)[None, :], dtype=jnp.int32)
    rows_shape = (BIAS_SLOTS, N_ATTN_HEADS, 2 * MOBA_BLOCK)
    sink_shape = (SWA_KV_HEADS, 1, SWA_GROUP * SWA_BLOCK)
    return pl.pallas_call(
        _ada_kernel,
        grid=(N_MOD,),
        in_specs=[pl.BlockSpec((batch, D_MODEL), lambda j: (0, 0)),
                  pl.BlockSpec((D_MODEL, D_MODEL), lambda j: (0, j)),
                  pl.BlockSpec((1, D_MODEL), lambda j: (0, j)),
                  pl.BlockSpec((1, n), lambda j: (0, 0)),
                  pl.BlockSpec((N_ATTN_HEADS, NUM_BUCKETS), lambda j: (0, 0)),
                  pl.BlockSpec((1, SWA_Q_HEADS), lambda j: (0, 0))],
        out_specs=(pl.BlockSpec((batch, 1, 1, D_MODEL), lambda j: (0, j, 0, 0)),
                   pl.BlockSpec(rows_shape, lambda j: (0, 0, 0)),
                   pl.BlockSpec(sink_shape, lambda j: (0, 0, 0))),
        out_shape=(jax.ShapeDtypeStruct((batch, N_MOD, 1, D_MODEL), F32),
                   jax.ShapeDtypeStruct(rows_shape, F32),
                   jax.ShapeDtypeStruct(sink_shape, F32)),
        scratch_shapes=[pltpu.VMEM((SUBLANES, D_MODEL), F32)],
        compiler_params=pltpu.CompilerParams(dimension_semantics=("arbitrary",),
                                             vmem_limit_bytes=VMEM_LIMIT_BYTES),
        name="ada",
    )(c, w, b, bucket, rel_bias.astype(F32).T, sinks.astype(F32)[None, :])


IN_WIDTHS = (SWA_Q_W, SWA_KV_W, SWA_KV_W, MOBA_W, MOBA_W, MOBA_W, D_MODEL, D_MODEL)
IN_OFFS = tuple(sum(IN_WIDTHS[:k]) for k in range(len(IN_WIDTHS) + 1))
FEAT_COLS = (0, 2, 3, 5)
FEAT_W = sum(IN_WIDTHS[k] for k in FEAT_COLS)
TOK_COLS = (1, 4, 6, 7)
TOK_W = sum(IN_WIDTHS[k] for k in TOK_COLS)
Q_COLS = (0, 3)
BLOCKS_PER_TILE = TOKEN_TILE // MOBA_BLOCK


def _proj_kernel(x_ref, shift_ref, scale_ref, g_ref, w_ref, *refs):
    n_cast = (len(refs) - 12) // 2
    cast_in, refs = refs[:n_cast], refs[n_cast:]
    ka_ref, kb_ref, ga_ref, gb_ref, qa_ref, va_ref, qb_ref, vb_ref, sel_ref = refs[:9]
    cast_out, (wk_ref, wt_ref, kmean_ref) = refs[9:9 + n_cast], refs[9 + n_cast:]

    for src, dst in zip(cast_in, cast_out):
        dst[...] = src[...].astype(dst.dtype)

    tile = pl.program_id(0) % (kmean_ref.shape[0] // BLOCKS_PER_TILE)

    @pl.when(tile == 0)
    def _():
        kmean_ref[...] = jnp.zeros(kmean_ref.shape, F32)

    @pl.when(pl.program_id(0) == 0)
    def _():
        o = 0
        for k in TOK_COLS:
            wk_ref[:, o:o + IN_WIDTHS[k]] = w_ref[:, IN_OFFS[k]:IN_OFFS[k + 1]].astype(BF16)
            o += IN_WIDTHS[k]
        o = 0
        for k in FEAT_COLS:
            wcol = w_ref[:, IN_OFFS[k]:IN_OFFS[k + 1]]
            if k in Q_COLS:
                wcol = wcol * (ATTN_SCALE * LOG2E)
            wt_ref[o:o + IN_WIDTHS[k], :] = wcol.T.astype(BF16)
            o += IN_WIDTHS[k]

    h = _rms_modulate(x_ref[...], g_ref[...], shift_ref[0, 0], scale_ref[0, 0]).astype(BF16)
    tok_refs = dict(zip(TOK_COLS, (ka_ref, kb_ref, ga_ref, gb_ref)))
    feat_refs = dict(zip(FEAT_COLS, (qa_ref, va_ref, qb_ref, vb_ref)))

    def token_major(k):
        o = sum(IN_WIDTHS[c] for c in TOK_COLS[:TOK_COLS.index(k)])
        out = _dot(h, wk_ref[:, o:o + IN_WIDTHS[k]])
        tok_refs[k][...] = out.astype(BF16)
        return out

    def feature_major(k):
        o = sum(IN_WIDTHS[c] for c in FEAT_COLS[:FEAT_COLS.index(k)])
        out = _dot_nt(wt_ref[o:o + IN_WIDTHS[k], :], h).astype(BF16)
        blk = feat_refs[k].shape[-1]
        for t in range(TOKEN_TILE // blk):
            feat_refs[k][0, t] = out[:, t * blk:(t + 1) * blk]
        return out

    _update_kmean(token_major(4), kmean_ref, tile)
    qb_t = feature_major(3)
    token_major(1)
    token_major(6)
    _moba_select(qb_t, sel_ref, kmean_ref, tile)
    token_major(7)
    for k in (0, 2, 5):
        feature_major(k)


def _update_kmean(kb, kmean_ref, tile):
    for t in range(BLOCKS_PER_TILE):
        kmean_ref[pl.ds(tile * BLOCKS_PER_TILE + t, 1), :] = jnp.mean(
            kb[t * MOBA_BLOCK:(t + 1) * MOBA_BLOCK, :], axis=0, keepdims=True)


def _moba_select(qb_t, sel_ref, kmean_ref, tile):
    nblk = kmean_ref.shape[0]
    blocks_per_tile = BLOCKS_PER_TILE
    blk = lax.broadcasted_iota(jnp.int32, (nblk, TOKEN_TILE), 0)
    qblk = tile * blocks_per_tile + lax.broadcasted_iota(jnp.int32, (nblk, TOKEN_TILE), 1) // MOBA_BLOCK
    past = blk < qblk
    own = jnp.where(blk == qblk, 0.0, NEG_INF)
    blk_f = blk.astype(F32)
    for h in range(MOBA_HEADS):
        hs = slice(h * HEAD_DIM, (h + 1) * HEAD_DIM)
        kmean = kmean_ref[:, hs]
        k1 = kmean.astype(BF16)
        r1 = kmean - k1.astype(F32)
        k2 = r1.astype(BF16)
        k3 = (r1 - k2.astype(F32)).astype(BF16)
        parts = _dot(jnp.concatenate([k1, k2, k3], axis=0), qb_t[hs, :])
        gate = parts[0:nblk] + parts[nblk:2 * nblk] + parts[2 * nblk:3 * nblk]
        gate = jnp.where(past, gate, NEG_INF)
        mask = own
        for _ in range(MOBA_TOPK):
            top = jnp.max(gate, axis=0, keepdims=True)
            first = jnp.min(jnp.where(gate == top, blk_f, float(nblk)), axis=0, keepdims=True)
            pick = blk_f == first
            mask = jnp.where(jnp.logical_and(pick, past), 0.0, mask)
            gate = jnp.where(pick, NEG_INF, gate)
        sel_ref[0, h] = mask


def _proj(x2, mod4, g1, w_in, cast_weights, batch, seq):
    tokens = x2.shape[0]
    steps = tokens // TOKEN_TILE
    assert all(w.shape[0] % (16 * steps) == 0 for w in cast_weights)
    cast_specs = [pl.BlockSpec((w.shape[0] // steps, w.shape[1]), lambda i: (i, 0)) for w in cast_weights]
    tiles_per_seq = seq // TOKEN_TILE
    na = TOKEN_TILE // SWA_BLOCK
    nb = TOKEN_TILE // MOBA_BLOCK

    def feat_spec(width, blk, per_tile):
        return pl.BlockSpec((1, per_tile, width, blk),
                            lambda i: (i // tiles_per_seq, i % tiles_per_seq, 0, 0))

    out_shape = (
        jax.ShapeDtypeStruct((tokens, SWA_KV_W), BF16),
        jax.ShapeDtypeStruct((tokens, MOBA_W), BF16),
        jax.ShapeDtypeStruct((tokens, D_MODEL), BF16),
        jax.ShapeDtypeStruct((tokens, D_MODEL), BF16),
        jax.ShapeDtypeStruct((batch, seq // SWA_BLOCK, SWA_Q_W, SWA_BLOCK), BF16),
        jax.ShapeDtypeStruct((batch, seq // SWA_BLOCK, SWA_KV_W, SWA_BLOCK), BF16),
        jax.ShapeDtypeStruct((batch, seq // MOBA_BLOCK, MOBA_W, MOBA_BLOCK), BF16),
        jax.ShapeDtypeStruct((batch, seq // MOBA_BLOCK, MOBA_W, MOBA_BLOCK), BF16),
        jax.ShapeDtypeStruct((batch, MOBA_HEADS, seq // MOBA_BLOCK, seq), F32),
    )
    tok_spec = lambda w: pl.BlockSpec((TOKEN_TILE, w), lambda i: (i, 0))
    return pl.pallas_call(
        _proj_kernel,
        grid=(steps,),
        in_specs=[
            pl.BlockSpec((TOKEN_TILE, D_MODEL), lambda i: (i, 0)),
            pl.BlockSpec((1, 1, 1, D_MODEL), lambda i: (i // tiles_per_seq, 0, 0, 0)),
            pl.BlockSpec((1, 1, 1, D_MODEL), lambda i: (i // tiles_per_seq, 1, 0, 0)),
            _resident((1, D_MODEL), lambda i: (0, 0)),
            _resident(w_in.shape, lambda i: (0, 0)),
        ] + cast_specs,
        out_specs=(tok_spec(SWA_KV_W), tok_spec(MOBA_W), tok_spec(D_MODEL), tok_spec(D_MODEL),
                   feat_spec(SWA_Q_W, SWA_BLOCK, na), feat_spec(SWA_KV_W, SWA_BLOCK, na),
                   feat_spec(MOBA_W, MOBA_BLOCK, nb), feat_spec(MOBA_W, MOBA_BLOCK, nb),
                   pl.BlockSpec((1, MOBA_HEADS, seq // MOBA_BLOCK, TOKEN_TILE),
                                lambda i: (i // tiles_per_seq, 0, 0, i % tiles_per_seq)))
        + tuple(cast_specs),
        out_shape=out_shape + tuple(jax.ShapeDtypeStruct(w.shape, BF16) for w in cast_weights),
        scratch_shapes=[pltpu.VMEM((D_MODEL, TOK_W), BF16), pltpu.VMEM((FEAT_W, D_MODEL), BF16),
                        pltpu.VMEM((seq // MOBA_BLOCK, MOBA_W), F32)],
        compiler_params=pltpu.CompilerParams(dimension_semantics=("arbitrary",),
                                             vmem_limit_bytes=VMEM_LIMIT_BYTES),
        name="proj",
    )(x2, mod4, mod4, g1, w_in, *cast_weights)


def _swa_kernel(q_ref, k_ref, v_ref, rows_ref, sink_ref, o_ref, bias_ref, ot_ref, s_ref, cm_ref):
    L = SWA_BLOCK
    nblk = q_ref.shape[1]

    @pl.when(pl.program_id(0) == 0)
    def _():
        k_idx = lax.broadcasted_iota(jnp.int32, (L, L), 0)
        q_idx = lax.broadcasted_iota(jnp.int32, (L, L), 1)
        for h in range(SWA_Q_HEADS):
            g, gi = divmod(h, SWA_GROUP)
            t0 = rows_ref[0, h:h + 1, MOBA_BLOCK:MOBA_BLOCK + L]
            r = _toeplitz(jnp.concatenate([t0, t0], axis=1), L)
            cols = slice(gi * L, (gi + 1) * L)
            bias_ref[g, 0:L, cols] = jnp.where(q_idx < k_idx, r, NEG_INF).astype(BF16)
            bias_ref[g, L:2 * L, cols] = jnp.where(q_idx >= k_idx, r, NEG_INF).astype(BF16)

    ones_blk = jnp.concatenate([jnp.ones((1, 2 * L), BF16), jnp.zeros((ACC_ROWS - HEAD_DIM - 1, 2 * L), BF16)],
                               axis=0)
    PW = 2 * L
    chains = [(g, hp) for g in range(SWA_KV_HEADS) for hp in range(SWA_GROUP // 2)]
    kv_cols = [slice(g * HEAD_DIM, (g + 1) * HEAD_DIM) for g in range(SWA_KV_HEADS)]

    def q_pair(c, g, hp):
        h0 = g * SWA_GROUP + 2 * hp
        return jnp.concatenate([q_ref[0, c, (h0 + e) * HEAD_DIM:(h0 + e + 1) * HEAD_DIM, :] for e in range(2)],
                               axis=1)

    def scores_first(idx):
        g, hp = chains[idx]
        lanes = slice(hp * PW, (hp + 1) * PW)
        s_ref[idx, 0:L, :] = jnp.full((L, PW), NEG_INF, BF16)
        s = _dot(k_ref[0, 0:L, kv_cols[g]], q_pair(0, g, hp)).astype(BF16) + bias_ref[g, L:2 * L, lanes]
        s_ref[idx, L:2 * L, :] = s
        cm_ref[idx] = jnp.max(s, axis=0, keepdims=True).astype(F32)

    def scores(c, idx):
        g, hp = chains[idx]
        kw = k_ref[0, pl.ds(pl.multiple_of((c - 1) * L, L), 2 * L), kv_cols[g]]
        s = _dot(kw, q_pair(c, g, hp)).astype(BF16) + bias_ref[g, :, hp * PW:(hp + 1) * PW]
        s_ref[idx] = s
        cm_ref[idx] = jnp.max(s, axis=0, keepdims=True).astype(F32)

    def attend(c, c_prev, idx, slot):
        g, hp = chains[idx]
        hs = kv_cols[g]
        s = s_ref[idx]
        sink = sink_ref[g, :, hp * PW:(hp + 1) * PW]
        m_b = jnp.maximum(cm_ref[idx], sink).astype(BF16)
        m = m_b.astype(F32)
        p = jnp.exp2(s - m_b)
        v_win = jnp.concatenate([v_ref[0, c_prev, hs, :], v_ref[0, c, hs, :]], axis=1)
        pv = _dot(jnp.concatenate([v_win, ones_blk], axis=0), p)
        o = pv[0:HEAD_DIM] / (pv[HEAD_DIM:HEAD_DIM + 1] + jnp.exp2(sink - m))
        for e in range(2):
            r = (g * SWA_GROUP + 2 * hp + e) * HEAD_DIM
            ot_ref[slot, r:r + HEAD_DIM, :] = o[:, e * L:(e + 1) * L]

    def emit(c, slot):
        o_ref[0, pl.ds(pl.multiple_of(c * L, L), L), :] = ot_ref[slot].T.astype(o_ref.dtype)

    for idx in range(len(chains)):
        scores_first(idx)
    U = SWA_UNROLL
    for idx in range(len(chains)):
        attend(0, 0, idx, U - 1)
        scores(1, idx)

    def body(c, e):
        emit(c - 1, (e - 1) % U)
        for idx in range(len(chains)):
            attend(c, c - 1, idx, e)
            scores(c + 1, idx)

    def body_unrolled(u, carry):
        for e in range(U):
            body(U * u + 1 + e, e)
        return carry

    lax.fori_loop(0, (nblk - 2) // U, body_unrolled, 0)
    emit(nblk - 2, U - 1)
    for idx in range(len(chains)):
        attend(nblk - 1, nblk - 2, idx, 0)
    emit(nblk - 1, 0)


def _swa(qa_t, ka3, va_t, rows, sink_rows):
    batch, nblk = qa_t.shape[0], qa_t.shape[1]
    seq = nblk * SWA_BLOCK
    assert nblk >= 2 and (nblk - 2) % SWA_UNROLL == 0 and SWA_UNROLL >= 2
    return pl.pallas_call(
        _swa_kernel,
        grid=(batch,),
        in_specs=[
            pl.BlockSpec((1, nblk, SWA_Q_W, SWA_BLOCK), lambda b: (b, 0, 0, 0)),
            pl.BlockSpec((1, seq, SWA_KV_W), lambda b: (b, 0, 0)),
            pl.BlockSpec((1, nblk, SWA_KV_W, SWA_BLOCK), lambda b: (b, 0, 0, 0)),
            _resident((1, N_ATTN_HEADS, 2 * MOBA_BLOCK), lambda b: (0, 0, 0)),
            _resident(sink_rows.shape, lambda b: (0, 0, 0)),
        ],
        out_specs=pl.BlockSpec((1, seq, SWA_Q_W), lambda b: (b, 0, 0)),
        out_shape=jax.ShapeDtypeStruct((batch, seq, SWA_Q_W), BF16),
        scratch_shapes=[pltpu.VMEM((SWA_KV_HEADS, 2 * SWA_BLOCK, SWA_GROUP * SWA_BLOCK), BF16),
                        pltpu.VMEM((SWA_UNROLL, SWA_Q_W, SWA_BLOCK), F32),
                        pltpu.VMEM((SWA_Q_HEADS // 2, 2 * SWA_BLOCK, 2 * SWA_BLOCK), BF16),
                        pltpu.VMEM((SWA_Q_HEADS // 2, 1, 2 * SWA_BLOCK), F32)],
        compiler_params=pltpu.CompilerParams(dimension_semantics=("arbitrary",),
                                             vmem_limit_bytes=VMEM_LIMIT_BYTES),
        name="swa",
    )(qa_t, ka3, va_t, rows, sink_rows)


def _moba_kernel(q_ref, qn_ref, k_ref, v_ref, sel_ref, rows_ref, o_ref, bias_ref, m_ref, acc_ref, s_ref, cm_ref):
    step = pl.program_id(1)
    last = pl.num_programs(1) - 1
    MB = MOBA_BLOCK
    H = MOBA_HEADS
    G = MOBA_Q_PER_STEP

    @pl.when(jnp.logical_and(pl.program_id(0) == 0, step == 0))
    def _():
        for h in range(H):
            for d in range(BIAS_SLOTS):
                bias_ref[h, d] = _toeplitz(rows_ref[d, SWA_Q_HEADS + h:SWA_Q_HEADS + h + 1, :], MB).astype(BF16)

    m_ref[0:H] = jnp.full((H,) + m_ref.shape[1:], NEG_INF, F32)
    acc_ref[0:H * ACC_ROWS, :] = jnp.zeros((H * ACC_ROWS, acc_ref.shape[1]), F32)
    heads = [slice(h * HEAD_DIM, (h + 1) * HEAD_DIM) for h in range(H)]
    ones_blk = jnp.concatenate([jnp.ones((1, MB), BF16), jnp.zeros((ACC_ROWS - HEAD_DIM - 1, MB), BF16)],
                               axis=0)

    def sel_row(j, qs, h):
        return sel_ref[0, h, pl.ds(j, 1), qs * MB:(qs + 1) * MB]

    def scores(j, qs, h):
        c = qs * H + h
        rows = pl.ds(pl.multiple_of(j * MB, MB), MB)
        slot = jnp.minimum(G * step + qs - j, BIAS_SLOTS - 1)
        s = _dot(k_ref[0, rows, heads[h]], q_ref[0, qs, heads[h], :]).astype(BF16) + bias_ref[h, slot]
        s_ref[c] = s
        cm_ref[c] = jnp.max(s, axis=0, keepdims=True).astype(F32) + sel_row(j, qs, h)

    def own_scores(blk, q_blk, c, h):
        rows = pl.ds(pl.multiple_of(blk * MB, MB), MB)
        s = _dot(k_ref[0, rows, heads[h]], q_blk[heads[h], :]).astype(BF16) + bias_ref[h, 0]
        s_ref[c] = s
        cm_ref[c] = jnp.max(s, axis=0, keepdims=True).astype(F32)

    def accumulate(j, qs, h, opening=False):
        c = qs * H + h
        m_new = cm_ref[c] if opening else jnp.maximum(m_ref[c], cm_ref[c])
        p = jnp.exp2(s_ref[c] - (m_new - sel_row(j, qs, h)).astype(BF16))
        v_aug = jnp.concatenate([v_ref[0, j, heads[h], :], ones_blk], axis=0)
        rs = slice(c * ACC_ROWS, (c + 1) * ACC_ROWS)
        if opening:
            acc_ref[rs, :] = _dot(v_aug, p)
        else:
            acc_ref[rs, :] = jnp.exp2(m_ref[c] - m_new) * acc_ref[rs, :] + _dot(v_aug, p)
        m_ref[c] = m_new

    first = G * step

    @pl.when(step == 0)
    def _():
        for qs in range(G):
            for h in range(H):
                own_scores(qs, q_ref[0, qs], qs * H + h, h)

    for n in range(G - 1):
        for qs in range(G - 1 - n, G):
            for h in range(H):
                accumulate(first + G - 1 - n, qs, h, opening=(qs == G - 1 - n))
                scores(first + G - 2 - n, qs, h)

    def body(j, carry):
        for qs in range(G):
            for h in range(H):
                accumulate(j, qs, h)
                scores(j - 1, qs, h)
        return carry

    def body_unrolled(u, carry):
        for e in range(MOBA_UNROLL):
            carry = body(first - MOBA_UNROLL * u - e, carry)
        return carry

    lax.fori_loop(0, (G // MOBA_UNROLL) * step, body_unrolled, 0)

    @pl.when(step < last)
    def _():
        for h in range(H):
            for qs in range(G):
                accumulate(0, qs, h)
                own_scores(first + G + qs, qn_ref[0, qs], qs * H + h, h)

    @pl.when(step == last)
    def _():
        for h in range(H):
            for qs in range(G):
                accumulate(0, qs, h)

    for qs in range(G):
        outs = []
        for h in range(H):
            base = (qs * H + h) * ACC_ROWS
            outs.append(acc_ref[base:base + HEAD_DIM, :] / acc_ref[base + HEAD_DIM:base + HEAD_DIM + 1, :])
        o_ref[0, qs * MB:(qs + 1) * MB, :] = jnp.concatenate(outs, axis=0).T.astype(o_ref.dtype)


def _moba(qb_t, kb3, vb_t, sel, rows):
    batch, nblk = qb_t.shape[0], qb_t.shape[1]
    seq = nblk * MOBA_BLOCK
    g = MOBA_Q_PER_STEP
    assert g % MOBA_UNROLL == 0 and nblk % g == 0
    chains = g * MOBA_HEADS
    return pl.pallas_call(
        _moba_kernel,
        grid=(batch, nblk // g),
        in_specs=[
            pl.BlockSpec((1, g, MOBA_W, MOBA_BLOCK), lambda b, i: (b, i, 0, 0)),
            pl.BlockSpec((1, g, MOBA_W, MOBA_BLOCK), lambda b, i: (b, jnp.minimum(i + 1, nblk // g - 1), 0, 0)),
            pl.BlockSpec((1, seq, MOBA_W), lambda b, i: (b, 0, 0)),
            pl.BlockSpec((1, nblk, MOBA_W, MOBA_BLOCK), lambda b, i: (b, 0, 0, 0)),
            pl.BlockSpec((1, MOBA_HEADS, nblk, g * MOBA_BLOCK), lambda b, i: (b, 0, 0, i)),
            _resident(rows.shape, lambda b, i: (0, 0, 0)),
        ],
        out_specs=pl.BlockSpec((1, g * MOBA_BLOCK, MOBA_W), lambda b, i: (b, i, 0)),
        out_shape=jax.ShapeDtypeStruct((batch, seq, MOBA_W), BF16),
        scratch_shapes=[pltpu.VMEM((MOBA_HEADS, BIAS_SLOTS, MOBA_BLOCK, MOBA_BLOCK), BF16),
                        pltpu.VMEM((chains, 1, MOBA_BLOCK), F32),
                        pltpu.VMEM((chains * ACC_ROWS, MOBA_BLOCK), F32),
                        pltpu.VMEM((chains, MOBA_BLOCK, MOBA_BLOCK), BF16),
                        pltpu.VMEM((chains, 1, MOBA_BLOCK), F32)],
        compiler_params=pltpu.CompilerParams(dimension_semantics=("arbitrary", "arbitrary"),
                                             vmem_limit_bytes=VMEM_LIMIT_BYTES),
        name="moba",
    )(qb_t, qb_t, kb3, vb_t, sel, rows)


def _post_kernel(x_ref, ya_ref, yb_ref, ga_ref, gb_ref, gate1_ref, shift2_ref, scale2_ref,
                 gate2_ref, g2_ref, gf_ref, wa_ref, wb_ref, wo_ref, w1_ref, w2_ref, o_ref):
    a = _dot(ya_ref[...], wa_ref[...])
    b = _dot(yb_ref[...], wb_ref[...])
    merged = (jax.nn.sigmoid(ga_ref[...].astype(F32)) * a
              + jax.nn.sigmoid(gb_ref[...].astype(F32)) * b)
    x1 = x_ref[...] + gate1_ref[0, 0] * _dot(merged.astype(BF16), wo_ref[...])
    h2 = _rms_modulate(x1, g2_ref[...], shift2_ref[0, 0], scale2_ref[0, 0]).astype(BF16)
    y = jnp.zeros_like(x1)
    for c in range(D_FF // FF_CHUNK):
        cs = slice(c * FF_CHUNK, (c + 1) * FF_CHUNK)
        u = jnp.square(jnp.maximum(_dot(h2, w1_ref[:, cs]), 0.0)).astype(BF16)
        y = y + _dot(u, w2_ref[cs, :])
    x2 = x1 + gate2_ref[0, 0] * y
    ms = jnp.mean(x2 * x2, axis=-1, keepdims=True)
    o_ref[...] = (x2 * lax.rsqrt(ms + RMS_EPS)) * gf_ref[...]


def _post(x2, ya, yb, ga, gb, mod4, g2, gf, wa, wb, wo, w1, w2, seq):
    tokens = x2.shape[0]
    tiles_per_seq = seq // TOKEN_TILE
    tok = lambda w: pl.BlockSpec((TOKEN_TILE, w), lambda i: (i, 0))
    modrow = lambda k: pl.BlockSpec((1, 1, 1, D_MODEL), lambda i: (i // tiles_per_seq, k, 0, 0))
    full = lambda a: _resident(a.shape, lambda i: (0, 0))
    return pl.pallas_call(
        _post_kernel,
        grid=(tokens // TOKEN_TILE,),
        in_specs=[tok(D_MODEL), tok(SWA_Q_W), tok(MOBA_W), tok(D_MODEL), tok(D_MODEL),
                  modrow(2), modrow(3), modrow(4), modrow(5),
                  full(g2), full(gf), full(wa), full(wb), full(wo), full(w1), full(w2)],
        out_specs=tok(D_MODEL),
        out_shape=jax.ShapeDtypeStruct((tokens, D_MODEL), F32),
        compiler_params=pltpu.CompilerParams(dimension_semantics=("arbitrary",),
                                             vmem_limit_bytes=VMEM_LIMIT_BYTES),
        name="post",
    )(x2, ya, yb, ga, gb, mod4, mod4, mod4, mod4, g2, gf, wa, wb, wo, w1, w2)


def kernel(x, c, ada_w, ada_b, norm1_g, norm2_g, w_in, attn_sinks, rel_bias, w_branch_a,
           w_branch_b, w_out, w_mlp_in, w_mlp_out, final_g):
    batch, seq, _ = x.shape
    depth = ada_w.shape[0]
    assert seq % TOKEN_TILE == 0 and TOKEN_TILE % MOBA_BLOCK == 0 and batch <= SUBLANES
    far = np.arange((BIAS_SLOTS - 1) * MOBA_BLOCK - (MOBA_BLOCK - 1), max(seq, BIAS_SLOTS * MOBA_BLOCK))
    assert np.all(_t5_bucket(far) == _t5_bucket(far[:1]))
    assert 2 * SWA_BLOCK <= MOBA_BLOCK and SWA_WINDOW == SWA_BLOCK
    assert depth == 1
    l = 0


    x2 = x.reshape(batch * seq, D_MODEL)
    mod4, rows, sink_rows = _ada(c.astype(F32), ada_w[l], ada_b[l][None, :], rel_bias, attn_sinks[l])
    later_weights = [w.astype(F32) for w in (w_branch_a[l], w_branch_b[l], w_out[l], w_mlp_in[l], w_mlp_out[l])]
    ka, kb, ga, gb, qa_t, va_t, qb_t, vb_t, sel, wa, wb, wo, w1, w2 = _proj(
        x2, mod4, norm1_g[l][None, :], w_in[l].astype(F32), later_weights, batch, seq)

    ya = _swa(qa_t, ka.reshape(batch, seq, SWA_KV_W), va_t, rows, sink_rows)

    yb = _moba(qb_t, kb.reshape(batch, seq, MOBA_W), vb_t, sel, rows)

    out = _post(x2, ya.reshape(batch * seq, SWA_Q_W), yb.reshape(batch * seq, MOBA_W), ga, gb,
                mod4, norm2_g[l][None, :], final_g[None, :], wa, wb, wo, w1, w2, seq)
    return out.reshape(batch, seq, D_MODEL)
```

```python
import math

import jax
import jax.numpy as jnp
import numpy as np
from jax import lax
from jax.experimental import pallas as pl
from jax.experimental.pallas import tpu as pltpu

D_MODEL = 1024
HEAD_DIM = 64
ATTN_SCALE = HEAD_DIM ** -0.5
LOG2E = math.log2(math.e)
SWA_Q_HEADS = 8
SWA_KV_HEADS = 2
SWA_GROUP = SWA_Q_HEADS // SWA_KV_HEADS
SWA_WINDOW = 128
SWA_BLOCK = 128
MOBA_HEADS = 8
MOBA_BLOCK = 256
MOBA_TOPK = 3
NUM_BUCKETS = 32
MAX_EXACT = NUM_BUCKETS // 2
MAX_DISTANCE = 2048
N_ATTN_HEADS = SWA_Q_HEADS + MOBA_HEADS
SWA_Q_W = SWA_Q_HEADS * HEAD_DIM
SWA_KV_W = SWA_KV_HEADS * HEAD_DIM
MOBA_W = MOBA_HEADS * HEAD_DIM
D_FF = 4 * D_MODEL
N_MOD = 6
RMS_EPS = 1e-6

VMEM_LIMIT_BYTES = 56 * 1024 * 1024
TOKEN_TILE = 512
FF_CHUNK = 1024
SUBLANES = 8
BIAS_SLOTS = 8
SWA_UNROLL = 10
MOBA_Q_PER_STEP = 4
MOBA_UNROLL = 4
ACC_ROWS = HEAD_DIM + 16
NEG_INF = float("-inf")

F32 = jnp.float32
BF16 = jnp.bfloat16


def _resident(block_shape, index_map):
    return pl.BlockSpec(block_shape, index_map, pipeline_mode=pl.Buffered(1))


def _dot(a, b):
    return jnp.dot(a, b, preferred_element_type=F32)


def _dot_nt(a, b):
    return lax.dot_general(a, b, (((1,), (1,)), ((), ())), preferred_element_type=F32)


def _rms_modulate(xv, g, shift, scale):
    ms = jnp.mean(xv * xv, axis=-1, keepdims=True)
    y = xv * lax.rsqrt(ms + RMS_EPS)
    return (y * g) * (1.0 + scale) + shift


def _toeplitz(u, rows):
    wide = jnp.broadcast_to(u, (rows, 2 * rows))
    return pltpu.roll(wide, 0, 1, stride=1, stride_axis=0)[:, rows:]


def _t5_bucket(dist):
    n = np.maximum(dist, 0)
    nf = np.maximum(n, 1).astype(np.float32)
    large = MAX_EXACT + (np.log(nf / np.float32(MAX_EXACT)) / np.float32(math.log(MAX_DISTANCE / MAX_EXACT))
                         * np.float32(NUM_BUCKETS - MAX_EXACT)).astype(np.int32)
    large = np.minimum(large, NUM_BUCKETS - 1)
    return np.where(n < MAX_EXACT, n, large).astype(np.int32)


def _bias_rows(bucket_ref, rbt_ref, o_ref):
    bucket = bucket_ref[...]
    n = bucket.shape[1]
    onehot = jnp.where(lax.broadcasted_iota(jnp.int32, (NUM_BUCKETS, n), 0) == bucket, 1.0, 0.0)
    rows = jnp.dot(rbt_ref[...], onehot, preferred_element_type=F32,
                   precision=lax.Precision.HIGHEST)
    rows = jnp.where(bucket >= 0, rows * LOG2E, NEG_INF)
    for d in range(BIAS_SLOTS):
        o_ref[d] = rows[:, d * MOBA_BLOCK:(d + 2) * MOBA_BLOCK]


def _ada_kernel(c_ref, w_ref, b_ref, bucket_ref, rbt_ref, sinks_ref, o_ref, rows_ref, sink_rows_ref, cs_ref):
    batch = c_ref.shape[0]
    cs_ref[...] = jnp.zeros(cs_ref.shape, F32)
    cs_ref[0:batch, :] = jax.nn.silu(c_ref[...])
    res = _dot(cs_ref[...].astype(BF16), w_ref[...].astype(BF16)) + b_ref[...]
    for b in range(batch):
        o_ref[b, 0] = res[b:b + 1]

    @pl.when(pl.program_id(0) == 0)
    def _():
        _bias_rows(bucket_ref, rbt_ref, rows_ref)
        sinks = sinks_ref[...] * LOG2E
        for h in range(SWA_Q_HEADS):
            g, gi = divmod(h, SWA_GROUP)
            sink_rows_ref[g, :, gi * SWA_BLOCK:(gi + 1) * SWA_BLOCK] = jnp.broadcast_to(
                sinks[:, h:h + 1], (1, SWA_BLOCK))


def _ada(c, w, b, rel_bias, sinks):
    batch = c.shape[0]
    assert w.shape[1] == N_MOD * D_MODEL
    n = (BIAS_SLOTS + 1) * MOBA_BLOCK
    dist = np.arange(n, dtype=np.int32) - MOBA_BLOCK
    bucket = jnp.asarray(np.where(dist >= 0, _t5_bucket(dist), -1)[None, :], dtype=jnp.int32)
    rows_shape = (BIAS_SLOTS, N_ATTN_HEADS, 2 * MOBA_BLOCK)
    sink_shape = (SWA_KV_HEADS, 1, SWA_GROUP * SWA_BLOCK)
    return pl.pallas_call(
        _ada_kernel,
        grid=(N_MOD,),
        in_specs=[pl.BlockSpec((batch, D_MODEL), lambda j: (0, 0)),
                  pl.BlockSpec((D_MODEL, D_MODEL), lambda j: (0, j)),
                  pl.BlockSpec((1, D_MODEL), lambda j: (0, j)),
                  pl.BlockSpec((1, n), lambda j: (0, 0)),
                  pl.BlockSpec((N_ATTN_HEADS, NUM_BUCKETS), lambda j: (0, 0)),
                  pl.BlockSpec((1, SWA_Q_HEADS), lambda j: (0, 0))],
        out_specs=(pl.BlockSpec((batch, 1, 1, D_MODEL), lambda j: (0, j, 0, 0)),
                   pl.BlockSpec(rows_shape, lambda j: (0, 0, 0)),
                   pl.BlockSpec(sink_shape, lambda j: (0, 0, 0))),
        out_shape=(jax.ShapeDtypeStruct((batch, N_MOD, 1, D_MODEL), F32),
                   jax.ShapeDtypeStruct(rows_shape, F32),
                   jax.ShapeDtypeStruct(sink_shape, F32)),
        scratch_shapes=[pltpu.VMEM((SUBLANES, D_MODEL), F32)],
        compiler_params=pltpu.CompilerParams(dimension_semantics=("arbitrary",),
                                             vmem_limit_bytes=VMEM_LIMIT_BYTES),
        name="ada",
    )(c, w, b, bucket, rel_bias.astype(F32).T, sinks.astype(F32)[None, :])


IN_WIDTHS = (SWA_Q_W, SWA_KV_W, SWA_KV_W, MOBA_W, MOBA_W, MOBA_W, D_MODEL, D_MODEL)
IN_OFFS = tuple(sum(IN_WIDTHS[:k]) for k in range(len(IN_WIDTHS) + 1))
FEAT_COLS = (0, 2, 3, 5)
FEAT_W = sum(IN_WIDTHS[k] for k in FEAT_COLS)
TOK_COLS = (1, 4, 6, 7)
TOK_W = sum(IN_WIDTHS[k] for k in TOK_COLS)
Q_COLS = (0, 3)
BLOCKS_PER_TILE = TOKEN_TILE // MOBA_BLOCK


def _proj_kernel(x_ref, shift_ref, scale_ref, g_ref, w_ref, *refs):
    n_cast = (len(refs) - 12) // 2
    cast_in, refs = refs[:n_cast], refs[n_cast:]
    ka_ref, kb_ref, ga_ref, gb_ref, qa_ref, va_ref, qb_ref, vb_ref, sel_ref = refs[:9]
    cast_out, (wk_ref, wt_ref, kmean_ref) = refs[9:9 + n_cast], refs[9 + n_cast:]

    for src, dst in zip(cast_in, cast_out):
        dst[...] = src[...].astype(dst.dtype)

    tile = pl.program_id(0) % (kmean_ref.shape[0] // BLOCKS_PER_TILE)

    @pl.when(tile == 0)
    def _():
        kmean_ref[...] = jnp.zeros(kmean_ref.shape, F32)

    @pl.when(pl.program_id(0) == 0)
    def _():
        o = 0
        for k in TOK_COLS:
            wk_ref[:, o:o + IN_WIDTHS[k]] = w_ref[:, IN_OFFS[k]:IN_OFFS[k + 1]].astype(BF16)
            o += IN_WIDTHS[k]
        o = 0
        for k in FEAT_COLS:
            wcol = w_ref[:, IN_OFFS[k]:IN_OFFS[k + 1]]
            if k in Q_COLS:
                wcol = wcol * (ATTN_SCALE * LOG2E)
            wt_ref[o:o + IN_WIDTHS[k], :] = wcol.T.astype(BF16)
            o += IN_WIDTHS[k]

    h = _rms_modulate(x_ref[...], g_ref[...], shift_ref[0, 0], scale_ref[0, 0]).astype(BF16)
    tok_refs = dict(zip(TOK_COLS, (ka_ref, kb_ref, ga_ref, gb_ref)))
    feat_refs = dict(zip(FEAT_COLS, (qa_ref, va_ref, qb_ref, vb_ref)))

    def token_major(k):
        o = sum(IN_WIDTHS[c] for c in TOK_COLS[:TOK_COLS.index(k)])
        out = _dot(h, wk_ref[:, o:o + IN_WIDTHS[k]])
        tok_refs[k][...] = out.astype(BF16)
        return out

    def feature_major(k):
        o = sum(IN_WIDTHS[c] for c in FEAT_COLS[:FEAT_COLS.index(k)])
        out = _dot_nt(wt_ref[o:o + IN_WIDTHS[k], :], h).astype(BF16)
        blk = feat_refs[k].shape[-1]
        for t in range(TOKEN_TILE // blk):
            feat_refs[k][0, t] = out[:, t * blk:(t + 1) * blk]
        return out

    _update_kmean(token_major(4), kmean_ref, tile)
    qb_t = feature_major(3)
    token_major(1)
    token_major(6)
    _moba_select(qb_t, sel_ref, kmean_ref, tile)
    token_major(7)
    for k in (0, 2, 5):
        feature_major(k)


def _update_kmean(kb, kmean_ref, tile):
    for t in range(BLOCKS_PER_TILE):
        kmean_ref[pl.ds(tile * BLOCKS_PER_TILE + t, 1), :] = jnp.mean(
            kb[t * MOBA_BLOCK:(t + 1) * MOBA_BLOCK, :], axis=0, keepdims=True)


def _moba_select(qb_t, sel_ref, kmean_ref, tile):
    nblk = kmean_ref.shape[0]
    blocks_per_tile = BLOCKS_PER_TILE
    blk = lax.broadcasted_iota(jnp.int32, (nblk, TOKEN_TILE), 0)
    qblk = tile * blocks_per_tile + lax.broadcasted_iota(jnp.int32, (nblk, TOKEN_TILE), 1) // MOBA_BLOCK
    past = blk < qblk
    own = jnp.where(blk == qblk, 0.0, NEG_INF)
    blk_f = blk.astype(F32)
    for h in range(MOBA_HEADS):
        hs = slice(h * HEAD_DIM, (h + 1) * HEAD_DIM)
        kmean = kmean_ref[:, hs]
        k1 = kmean.astype(BF16)
        r1 = kmean - k1.astype(F32)
        k2 = r1.astype(BF16)
        k3 = (r1 - k2.astype(F32)).astype(BF16)
        parts = _dot(jnp.concatenate([k1, k2, k3], axis=0), qb_t[hs, :])
        gate = parts[0:nblk] + parts[nblk:2 * nblk] + parts[2 * nblk:3 * nblk]
        gate = jnp.where(past, gate, NEG_INF)
        mask = own
        for _ in range(MOBA_TOPK):
            top = jnp.max(gate, axis=0, keepdims=True)
            first = jnp.min(jnp.where(gate == top, blk_f, float(nblk)), axis=0, keepdims=True)
            pick = blk_f == first
            mask = jnp.where(jnp.logical_and(pick, past), 0.0, mask)
            gate = jnp.where(pick, NEG_INF, gate)
        sel_ref[0, h] = mask


def _proj(x2, mod4, g1, w_in, cast_weights, batch, seq):
    tokens = x2.shape[0]
    steps = tokens // TOKEN_TILE
    assert all(w.shape[0] % (16 * steps) == 0 for w in cast_weights)
    cast_specs = [pl.BlockSpec((w.shape[0] // steps, w.shape[1]), lambda i: (i, 0)) for w in cast_weights]
    tiles_per_seq = seq // TOKEN_TILE
    na = TOKEN_TILE // SWA_BLOCK
    nb = TOKEN_TILE // MOBA_BLOCK

    def feat_spec(width, blk, per_tile):
        return pl.BlockSpec((1, per_tile, width, blk),
                            lambda i: (i // tiles_per_seq, i % tiles_per_seq, 0, 0))

    out_shape = (
        jax.ShapeDtypeStruct((tokens, SWA_KV_W), BF16),
        jax.ShapeDtypeStruct((tokens, MOBA_W), BF16),
        jax.ShapeDtypeStruct((tokens, D_MODEL), BF16),
        jax.ShapeDtypeStruct((tokens, D_MODEL), BF16),
        jax.ShapeDtypeStruct((batch, seq // SWA_BLOCK, SWA_Q_W, SWA_BLOCK), BF16),
        jax.ShapeDtypeStruct((batch, seq // SWA_BLOCK, SWA_KV_W, SWA_BLOCK), BF16),
        jax.ShapeDtypeStruct((batch, seq // MOBA_BLOCK, MOBA_W, MOBA_BLOCK), BF16),
        jax.ShapeDtypeStruct((batch, seq // MOBA_BLOCK, MOBA_W, MOBA_BLOCK), BF16),
        jax.ShapeDtypeStruct((batch, MOBA_HEADS, seq // MOBA_BLOCK, seq), F32),
    )
    tok_spec = lambda w: pl.BlockSpec((TOKEN_TILE, w), lambda i: (i, 0))
    return pl.pallas_call(
        _proj_kernel,
        grid=(steps,),
        in_specs=[
            pl.BlockSpec((TOKEN_TILE, D_MODEL), lambda i: (i, 0)),
            pl.BlockSpec((1, 1, 1, D_MODEL), lambda i: (i // tiles_per_seq, 0, 0, 0)),
            pl.BlockSpec((1, 1, 1, D_MODEL), lambda i: (i // tiles_per_seq, 1, 0, 0)),
            _resident((1, D_MODEL), lambda i: (0, 0)),
            _resident(w_in.shape, lambda i: (0, 0)),
        ] + cast_specs,
        out_specs=(tok_spec(SWA_KV_W), tok_spec(MOBA_W), tok_spec(D_MODEL), tok_spec(D_MODEL),
                   feat_spec(SWA_Q_W, SWA_BLOCK, na), feat_spec(SWA_KV_W, SWA_BLOCK, na),
                   feat_spec(MOBA_W, MOBA_BLOCK, nb), feat_spec(MOBA_W, MOBA_BLOCK, nb),
                   pl.BlockSpec((1, MOBA_HEADS, seq // MOBA_BLOCK, TOKEN_TILE),
                                lambda i: (i // tiles_per_seq, 0, 0, i % tiles_per_seq)))
        + tuple(cast_specs),
        out_shape=out_shape + tuple(jax.ShapeDtypeStruct(w.shape, BF16) for w in cast_weights),
        scratch_shapes=[pltpu.VMEM((D_MODEL, TOK_W), BF16), pltpu.VMEM((FEAT_W, D_MODEL), BF16),
                        pltpu.VMEM((seq // MOBA_BLOCK, MOBA_W), F32)],
        compiler_params=pltpu.CompilerParams(dimension_semantics=("arbitrary",),
                                             vmem_limit_bytes=VMEM_LIMIT_BYTES),
        name="proj",
    )(x2, mod4, mod4, g1, w_in, *cast_weights)


def _swa_kernel(q_ref, k_ref, v_ref, rows_ref, sink_ref, o_ref, bias_ref, ot_ref, s_ref, cm_ref):
    L = SWA_BLOCK
    nblk = q_ref.shape[1]

    @pl.when(pl.program_id(0) == 0)
    def _():
        k_idx = lax.broadcasted_iota(jnp.int32, (L, L), 0)
        q_idx = lax.broadcasted_iota(jnp.int32, (L, L), 1)
        for h in range(SWA_Q_HEADS):
            g, gi = divmod(h, SWA_GROUP)
            t0 = rows_ref[0, h:h + 1, MOBA_BLOCK:MOBA_BLOCK + L]
            r = _toeplitz(jnp.concatenate([t0, t0], axis=1), L)
            cols = slice(gi * L, (gi + 1) * L)
            bias_ref[g, 0:L, cols] = jnp.where(q_idx < k_idx, r, NEG_INF).astype(BF16)
            bias_ref[g, L:2 * L, cols] = jnp.where(q_idx >= k_idx, r, NEG_INF).astype(BF16)

    ones_blk = jnp.concatenate([jnp.ones((1, 2 * L), BF16), jnp.zeros((ACC_ROWS - HEAD_DIM - 1, 2 * L), BF16)],
                               axis=0)
    PW = 2 * L
    chains = [(g, hp) for g in range(SWA_KV_HEADS) for hp in range(SWA_GROUP // 2)]
    kv_cols = [slice(g * HEAD_DIM, (g + 1) * HEAD_DIM) for g in range(SWA_KV_HEADS)]

    def q_pair(c, g, hp):
        h0 = g * SWA_GROUP + 2 * hp
        return jnp.concatenate([q_ref[0, c, (h0 + e) * HEAD_DIM:(h0 + e + 1) * HEAD_DIM, :] for e in range(2)],
                               axis=1)

    def scores_first(idx):
        g, hp = chains[idx]
        lanes = slice(hp * PW, (hp + 1) * PW)
        s_ref[idx, 0:L, :] = jnp.full((L, PW), NEG_INF, BF16)
        s = _dot(k_ref[0, 0:L, kv_cols[g]], q_pair(0, g, hp)).astype(BF16) + bias_ref[g, L:2 * L, lanes]
        s_ref[idx, L:2 * L, :] = s
        cm_ref[idx] = jnp.max(s, axis=0, keepdims=True).astype(F32)

    def scores(c, idx):
        g, hp = chains[idx]
        kw = k_ref[0, pl.ds(pl.multiple_of((c - 1) * L, L), 2 * L), kv_cols[g]]
        s = _dot(kw, q_pair(c, g, hp)).astype(BF16) + bias_ref[g, :, hp * PW:(hp + 1) * PW]
        s_ref[idx] = s
        cm_ref[idx] = jnp.max(s, axis=0, keepdims=True).astype(F32)

    def attend(c, c_prev, idx, slot):
        g, hp = chains[idx]
        hs = kv_cols[g]
        s = s_ref[idx]
        sink = sink_ref[g, :, hp * PW:(hp + 1) * PW]
        m_b = jnp.maximum(cm_ref[idx], sink).astype(BF16)
        m = m_b.astype(F32)
        p = jnp.exp2(s - m_b)
        v_win = jnp.concatenate([v_ref[0, c_prev, hs, :], v_ref[0, c, hs, :]], axis=1)
        pv = _dot(jnp.concatenate([v_win, ones_blk], axis=0), p)
        o = pv[0:HEAD_DIM] / (pv[HEAD_DIM:HEAD_DIM + 1] + jnp.exp2(sink - m))
        for e in range(2):
            r = (g * SWA_GROUP + 2 * hp + e) * HEAD_DIM
            ot_ref[slot, r:r + HEAD_DIM, :] = o[:, e * L:(e + 1) * L]

    def emit(c, slot):
        o_ref[0, pl.ds(pl.multiple_of(c * L, L), L), :] = ot_ref[slot].T.astype(o_ref.dtype)

    for idx in range(len(chains)):
        scores_first(idx)
    U = SWA_UNROLL
    for idx in range(len(chains)):
        attend(0, 0, idx, U - 1)
        scores(1, idx)

    def body(c, e):
        emit(c - 1, (e - 1) % U)
        for idx in range(len(chains)):
            attend(c, c - 1, idx, e)
            scores(c + 1, idx)

    def body_unrolled(u, carry):
        for e in range(U):
            body(U * u + 1 + e, e)
        return carry

    lax.fori_loop(0, (nblk - 2) // U, body_unrolled, 0)
    emit(nblk - 2, U - 1)
    for idx in range(len(chains)):
        attend(nblk - 1, nblk - 2, idx, 0)
    emit(nblk - 1, 0)


def _swa(qa_t, ka3, va_t, rows, sink_rows):
    batch, nblk = qa_t.shape[0], qa_t.shape[1]
    seq = nblk * SWA_BLOCK
    assert nblk >= 2 and (nblk - 2) % SWA_UNROLL == 0 and SWA_UNROLL >= 2
    return pl.pallas_call(
        _swa_kernel,
        grid=(batch,),
        in_specs=[
            pl.BlockSpec((1, nblk, SWA_Q_W, SWA_BLOCK), lambda b: (b, 0, 0, 0)),
            pl.BlockSpec((1, seq, SWA_KV_W), lambda b: (b, 0, 0)),
            pl.BlockSpec((1, nblk, SWA_KV_W, SWA_BLOCK), lambda b: (b, 0, 0, 0)),
            _resident((1, N_ATTN_HEADS, 2 * MOBA_BLOCK), lambda b: (0, 0, 0)),
            _resident(sink_rows.shape, lambda b: (0, 0, 0)),
        ],
        out_specs=pl.BlockSpec((1, seq, SWA_Q_W), lambda b: (b, 0, 0)),
        out_shape=jax.ShapeDtypeStruct((batch, seq, SWA_Q_W), BF16),
        scratch_shapes=[pltpu.VMEM((SWA_KV_HEADS, 2 * SWA_BLOCK, SWA_GROUP * SWA_BLOCK), BF16),
                        pltpu.VMEM((SWA_UNROLL, SWA_Q_W, SWA_BLOCK), F32),
                        pltpu.VMEM((SWA_Q_HEADS // 2, 2 * SWA_BLOCK, 2 * SWA_BLOCK), BF16),
                        pltpu.VMEM((SWA_Q_HEADS // 2, 1, 2 * SWA_BLOCK), F32)],
        compiler_params=pltpu.CompilerParams(dimension_semantics=("arbitrary",),
                                             vmem_limit_bytes=VMEM_LIMIT_BYTES),
        name="swa",
    )(qa_t, ka3, va_t, rows, sink_rows)


def _moba_kernel(q_ref, qn_ref, k_ref, v_ref, sel_ref, rows_ref, o_ref, bias_ref, m_ref, acc_ref, s_ref, cm_ref):
    step = pl.program_id(1)
    last = pl.num_programs(1) - 1
    MB = MOBA_BLOCK
    H = MOBA_HEADS
    G = MOBA_Q_PER_STEP

    @pl.when(jnp.logical_and(pl.program_id(0) == 0, step == 0))
    def _():
        for h in range(H):
            for d in range(BIAS_SLOTS):
                bias_ref[h, d] = _toeplitz(rows_ref[d, SWA_Q_HEADS + h:SWA_Q_HEADS + h + 1, :], MB).astype(BF16)

    m_ref[0:H] = jnp.full((H,) + m_ref.shape[1:], NEG_INF, F32)
    acc_ref[0:H * ACC_ROWS, :] = jnp.zeros((H * ACC_ROWS, acc_ref.shape[1]), F32)
    heads = [slice(h * HEAD_DIM, (h + 1) * HEAD_DIM) for h in range(H)]
    ones_blk = jnp.concatenate([jnp.ones((1, MB), BF16), jnp.zeros((ACC_ROWS - HEAD_DIM - 1, MB), BF16)],
                               axis=0)

    def sel_row(j, qs, h):
        return sel_ref[0, h, pl.ds(j, 1), qs * MB:(qs + 1) * MB]

    def scores(j, qs, h):
        c = qs * H + h
        rows = pl.ds(pl.multiple_of(j * MB, MB), MB)
        slot = jnp.minimum(G * step + qs - j, BIAS_SLOTS - 1)
        s = _dot(k_ref[0, rows, heads[h]], q_ref[0, qs, heads[h], :]).astype(BF16) + bias_ref[h, slot]
        s_ref[c] = s
        cm_ref[c] = jnp.max(s, axis=0, keepdims=True).astype(F32) + sel_row(j, qs, h)

    def own_scores(blk, qsrc_ref, qs, c, h):
        rows = pl.ds(pl.multiple_of(blk * MB, MB), MB)
        s = _dot(k_ref[0, rows, heads[h]], qsrc_ref[0, qs, heads[h], :]).astype(BF16) + bias_ref[h, 0]
        s_ref[c] = s
        cm_ref[c] = jnp.max(s, axis=0, keepdims=True).astype(F32)

    def accumulate(j, qs, h, opening=False):
        c = qs * H + h
        m_new = cm_ref[c] if opening else jnp.maximum(m_ref[c], cm_ref[c])
        p = jnp.exp2(s_ref[c] - (m_new - sel_row(j, qs, h)).astype(BF16))
        v_aug = jnp.concatenate([v_ref[0, j, heads[h], :], ones_blk], axis=0)
        rs = slice(c * ACC_ROWS, (c + 1) * ACC_ROWS)
        if opening:
            acc_ref[rs, :] = _dot(v_aug, p)
        else:
            acc_ref[rs, :] = jnp.exp2(m_ref[c] - m_new) * acc_ref[rs, :] + _dot(v_aug, p)
        m_ref[c] = m_new

    first = G * step

    @pl.when(step == 0)
    def _():
        for qs in range(G):
            for h in range(H):
                own_scores(qs, q_ref, qs, qs * H + h, h)

    for n in range(G - 1):
        for qs in range(G - 1 - n, G):
            for h in range(H):
                accumulate(first + G - 1 - n, qs, h, opening=(qs == G - 1 - n))
                scores(first + G - 2 - n, qs, h)

    def body(j, carry):
        for qs in range(G):
            for h in range(H):
                accumulate(j, qs, h)
                scores(j - 1, qs, h)
        return carry

    def body_unrolled(u, carry):
        for e in range(MOBA_UNROLL):
            carry = body(first - MOBA_UNROLL * u - e, carry)
        return carry

    lax.fori_loop(0, (G // MOBA_UNROLL) * step, body_unrolled, 0)

    @pl.when(step < last)
    def _():
        for h in range(H):
            for qs in range(G):
                accumulate(0, qs, h)
                own_scores(first + G + qs, qn_ref, qs, qs * H + h, h)

    @pl.when(step == last)
    def _():
        for h in range(H):
            for qs in range(G):
                accumulate(0, qs, h)

    for qs in range(G):
        outs = []
        for h in range(H):
            base = (qs * H + h) * ACC_ROWS
            outs.append(acc_ref[base:base + HEAD_DIM, :] / acc_ref[base + HEAD_DIM:base + HEAD_DIM + 1, :])
        o_ref[0, qs * MB:(qs + 1) * MB, :] = jnp.concatenate(outs, axis=0).T.astype(o_ref.dtype)


def _moba(qb_t, kb3, vb_t, sel, rows):
    batch, nblk = qb_t.shape[0], qb_t.shape[1]
    seq = nblk * MOBA_BLOCK
    g = MOBA_Q_PER_STEP
    assert g % MOBA_UNROLL == 0 and nblk % g == 0
    chains = g * MOBA_HEADS
    return pl.pallas_call(
        _moba_kernel,
        grid=(batch, nblk // g),
        in_specs=[
            pl.BlockSpec((1, g, MOBA_W, MOBA_BLOCK), lambda b, i: (b, i, 0, 0)),
            pl.BlockSpec((1, g, MOBA_W, MOBA_BLOCK), lambda b, i: (b, jnp.minimum(i + 1, nblk // g - 1), 0, 0)),
            pl.BlockSpec((1, seq, MOBA_W), lambda b, i: (b, 0, 0)),
            pl.BlockSpec((1, nblk, MOBA_W, MOBA_BLOCK), lambda b, i: (b, 0, 0, 0)),
            pl.BlockSpec((1, MOBA_HEADS, nblk, g * MOBA_BLOCK), lambda b, i: (b, 0, 0, i)),
            _resident(rows.shape, lambda b, i: (0, 0, 0)),
        ],
        out_specs=pl.BlockSpec((1, g * MOBA_BLOCK, MOBA_W), lambda b, i: (b, i, 0)),
        out_shape=jax.ShapeDtypeStruct((batch, seq, MOBA_W), BF16),
        scratch_shapes=[pltpu.VMEM((MOBA_HEADS, BIAS_SLOTS, MOBA_BLOCK, MOBA_BLOCK), BF16),
                        pltpu.VMEM((chains, 1, MOBA_BLOCK), F32),
                        pltpu.VMEM((chains * ACC_ROWS, MOBA_BLOCK), F32),
                        pltpu.VMEM((chains, MOBA_BLOCK, MOBA_BLOCK), BF16),
                        pltpu.VMEM((chains, 1, MOBA_BLOCK), F32)],
        compiler_params=pltpu.CompilerParams(dimension_semantics=("arbitrary", "arbitrary"),
                                             vmem_limit_bytes=VMEM_LIMIT_BYTES),
        name="moba",
    )(qb_t, qb_t, kb3, vb_t, sel, rows)


def _post_kernel(x_ref, ya_ref, yb_ref, ga_ref, gb_ref, gate1_ref, shift2_ref, scale2_ref,
                 gate2_ref, g2_ref, gf_ref, wa_ref, wb_ref, wo_ref, w1_ref, w2_ref, o_ref):
    a = _dot(ya_ref[...], wa_ref[...])
    b = _dot(yb_ref[...], wb_ref[...])
    merged = (jax.nn.sigmoid(ga_ref[...].astype(F32)) * a
              + jax.nn.sigmoid(gb_ref[...].astype(F32)) * b)
    x1 = x_ref[...] + gate1_ref[0, 0] * _dot(merged.astype(BF16), wo_ref[...])
    h2 = _rms_modulate(x1, g2_ref[...], shift2_ref[0, 0], scale2_ref[0, 0]).astype(BF16)
    y = jnp.zeros_like(x1)
    for c in range(D_FF // FF_CHUNK):
        cs = slice(c * FF_CHUNK, (c + 1) * FF_CHUNK)
        u = jnp.square(jnp.maximum(_dot(h2, w1_ref[:, cs]), 0.0)).astype(BF16)
        y = y + _dot(u, w2_ref[cs, :])
    x2 = x1 + gate2_ref[0, 0] * y
    ms = jnp.mean(x2 * x2, axis=-1, keepdims=True)
    o_ref[...] = (x2 * lax.rsqrt(ms + RMS_EPS)) * gf_ref[...]


def _post(x2, ya, yb, ga, gb, mod4, g2, gf, wa, wb, wo, w1, w2, seq):
    tokens = x2.shape[0]
    tiles_per_seq = seq // TOKEN_TILE
    tok = lambda w: pl.BlockSpec((TOKEN_TILE, w), lambda i: (i, 0))
    modrow = lambda k: pl.BlockSpec((1, 1, 1, D_MODEL), lambda i: (i // tiles_per_seq, k, 0, 0))
    full = lambda a: _resident(a.shape, lambda i: (0, 0))
    return pl.pallas_call(
        _post_kernel,
        grid=(tokens // TOKEN_TILE,),
        in_specs=[tok(D_MODEL), tok(SWA_Q_W), tok(MOBA_W), tok(D_MODEL), tok(D_MODEL),
                  modrow(2), modrow(3), modrow(4), modrow(5),
                  full(g2), full(gf), full(wa), full(wb), full(wo), full(w1), full(w2)],
        out_specs=tok(D_MODEL),
        out_shape=jax.ShapeDtypeStruct((tokens, D_MODEL), F32),
        compiler_params=pltpu.CompilerParams(dimension_semantics=("arbitrary",),
                                             vmem_limit_bytes=VMEM_LIMIT_BYTES),
        name="post",
    )(x2, ya, yb, ga, gb, mod4, mod4, mod4, mod4, g2, gf, wa, wb, wo, w1, w2)


def kernel(x, c, ada_w, ada_b, norm1_g, norm2_g, w_in, attn_sinks, rel_bias, w_branch_a,
           w_branch_b, w_out, w_mlp_in, w_mlp_out, final_g):
    batch, seq, _ = x.shape
    depth = ada_w.shape[0]
    assert seq % TOKEN_TILE == 0 and TOKEN_TILE % MOBA_BLOCK == 0 and batch <= SUBLANES
    far = np.arange((BIAS_SLOTS - 1) * MOBA_BLOCK - (MOBA_BLOCK - 1), max(seq, BIAS_SLOTS * MOBA_BLOCK))
    assert np.all(_t5_bucket(far) == _t5_bucket(far[:1]))
    assert 2 * SWA_BLOCK <= MOBA_BLOCK and SWA_WINDOW == SWA_BLOCK
    assert depth == 1
    l = 0


    x2 = x.reshape(batch * seq, D_MODEL)
    mod4, rows, sink_rows = _ada(c.astype(F32), ada_w[l], ada_b[l][None, :], rel_bias, attn_sinks[l])
    later_weights = [w.astype(F32) for w in (w_branch_a[l], w_branch_b[l], w_out[l], w_mlp_in[l], w_mlp_out[l])]
    ka, kb, ga, gb, qa_t, va_t, qb_t, vb_t, sel, wa, wb, wo, w1, w2 = _proj(
        x2, mod4, norm1_g[l][None, :], w_in[l].astype(F32), later_weights, batch, seq)

    ya = _swa(qa_t, ka.reshape(batch, seq, SWA_KV_W), va_t, rows, sink_rows)

    yb = _moba(qb_t, kb.reshape(batch, seq, MOBA_W), vb_t, sel, rows)

    out = _post(x2, ya.reshape(batch * seq, SWA_Q_W), yb.reshape(batch * seq, MOBA_W), ga, gb,
                mod4, norm2_g[l][None, :], final_g[None, :], wa, wb, wo, w1, w2, seq)
    return out.reshape(batch, seq, D_MODEL)
```
